```python
import jax, jax.numpy as jnp
from jax import lax
import numpy as np

D_MODEL = 1024
BATCH = 4
SEQ = 4096
DEPTH = 2
DEC_BATCH = 128
DEC_SEQ = 1
PAST_LEN = 2048
PAGE_SIZE = 128

N_EVEN = (DEPTH + 1) // 2
N_ODD = DEPTH // 2
MIX_W = D_MODEL
HALF_W = MIX_W // 2
D_FF = 4 * D_MODEL
CONV_W = 3
CONV_CH = HALF_W
ML_HEADS = 4
ML_DIM = HALF_W // ML_HEADS
ML_CHUNK = 64
HEAD_DIM = 64
KV_GROUPS = 2
NSA_HEADS = HALF_W // HEAD_DIM
DSA_HEADS = HALF_W // HEAD_DIM
BLK = 64
NSA_TOPN = 8
WINDOW = 256
IDX_HEADS = 4
IDX_DIM = 32
DSA_TOPK_MAX = 256
Q_BLOCK = 128
KV_SLOTS = 6
FORCE_SCORE = 1e4
ALPHA = (2.0 * DEPTH) ** 0.25
BETA = (8.0 * DEPTH) ** -0.25
LN_EPS = 1e-5
E_SIZES = (CONV_CH, CONV_CH, CONV_CH, HALF_W, HALF_W, HALF_W, HALF_W, ML_HEADS, ML_HEADS)
O_SIZES = (NSA_HEADS * HEAD_DIM, 2 * KV_GROUPS * HEAD_DIM, 2 * KV_GROUPS * HEAD_DIM, 2 * KV_GROUPS * HEAD_DIM, 3 * NSA_HEADS, DSA_HEADS * HEAD_DIM, 2 * KV_GROUPS * HEAD_DIM, IDX_HEADS * IDX_DIM, IDX_DIM, IDX_HEADS)
E_COLS = sum(E_SIZES)
O_COLS = sum(O_SIZES)

kernel_name = 'hybrid_conv_mlstm_nsa_dsa_step'


def _split(z, sizes):
    cuts = [int(c) for c in np.cumsum(sizes)[:-1]]
    return jnp.split(z, cuts, axis=-1)


def alibi_slopes(n):
    return jnp.asarray(np.power(2.0, -8.0 * np.arange(1, n + 1) / n), dtype=jnp.float32)


def layer_norm(x, g, b):
    xf = x.astype(jnp.float32)
    mu = jnp.mean(xf, axis=-1, keepdims=True)
    var = jnp.mean(jnp.square(xf - mu), axis=-1, keepdims=True)
    y = (xf - mu) * lax.rsqrt(var + LN_EPS) * g.astype(jnp.float32) + b.astype(jnp.float32)
    return y.astype(x.dtype)


def masked_softmax(s, mask):
    s = jnp.where(mask, s.astype(jnp.float32), -jnp.inf)
    m = jnp.max(s, axis=-1, keepdims=True)
    e = jnp.exp(s - jnp.where(jnp.isfinite(m), m, 0.0))
    return e / jnp.maximum(jnp.sum(e, axis=-1, keepdims=True), 1e-30)


def sq_relu_mlp(x, w_up, w_down):
    return jnp.square(jax.nn.relu(x @ w_up)) @ w_down


def short_conv(u_ext, w, b):
    t = u_ext.shape[1] - (CONV_W - 1)
    return b + sum(w[j] * u_ext[:, j:j + t] for j in range(CONV_W))


def mlstm_chunk(carry, inp):
    c_st, n_st, m_st = carry
    q, k, v, ig, lf = inp
    L = q.shape[2]
    b = jnp.cumsum(lf, axis=-1)
    causal = jnp.tril(jnp.ones((L, L), dtype=bool))
    dmat = jnp.where(causal, b[..., :, None] - b[..., None, :] + ig[..., None, :], -jnp.inf)
    inter = b + m_st[..., None]
    m_t = jnp.maximum(inter, jnp.max(dmat, axis=-1))
    w_intra = jnp.exp(dmat - m_t[..., None])
    w_inter = jnp.exp(inter - m_t)
    s = jnp.einsum('bhtd,bhsd->bhts', q, k) * w_intra
    num = jnp.einsum('bhts,bhsd->bhtd', s, v) + w_inter[..., None] * jnp.einsum('bhed,bhtd->bhte', c_st, q)
    den = jnp.sum(s, axis=-1) + w_inter * jnp.einsum('bhd,bhtd->bht', n_st, q)
    h = num / jnp.maximum(jnp.abs(den), jnp.exp(-m_t))[..., None]
    m_new = m_t[..., -1]
    w_state = jnp.exp(b[..., -1:] - b + ig - m_new[..., None])
    decay = jnp.exp(b[..., -1] + m_st - m_new)
    c_new = decay[..., None, None] * c_st + jnp.einsum('bhs,bhse,bhsd->bhed', w_state, v, k)
    n_new = decay[..., None] * n_st + jnp.einsum('bhs,bhsd->bhd', w_state, k)
    return (c_new, n_new, m_new), h


def even_mixer(x, w_in, gate_b, conv_w, conv_b, w_out, conv_prev, ml_state, chunk):
    f32 = jnp.float32
    bn, t, _ = x.shape
    h, bg, cg, q, k, v, og, ig, fg = _split(x @ w_in, E_SIZES)
    u = cg * h
    u_ext = jnp.concatenate([conv_prev.astype(u.dtype), u], axis=1)
    y_a = bg * short_conv(u_ext, conv_w, conv_b)
    new_conv = u_ext[:, u_ext.shape[1] - (CONV_W - 1):]
    heads = lambda a: a.reshape(bn, t, ML_HEADS, ML_DIM).astype(f32)
    q, k, v = heads(q), heads(k) * (ML_DIM ** -0.5), heads(v)
    ig = (ig + gate_b[:ML_HEADS]).astype(f32)
    lf = jax.nn.log_sigmoid((fg + gate_b[ML_HEADS:]).astype(f32))
    nc = t // chunk

    def to_chunks(a):
        a = jnp.moveaxis(a.reshape((bn, nc, chunk) + a.shape[2:]), 1, 0)
        return jnp.swapaxes(a, 2, 3)

    carry = tuple(s.astype(f32) for s in ml_state)
    carry, hs = lax.scan(mlstm_chunk, carry, tuple(to_chunks(a) for a in (q, k, v, ig, lf)))
    hs = jnp.swapaxes(jnp.moveaxis(hs, 0, 1), 2, 3).reshape(bn, t, HALF_W)
    y_b = jax.nn.sigmoid(og) * hs.astype(x.dtype)
    out = jnp.concatenate([y_a, y_b], axis=-1) @ w_out
    return out, new_conv, tuple(s.astype(x.dtype) for s in carry)


def odd_attend(qn, gates, qd, qi, wi, q_pos, kc, vc, slc, dsa_kv, kidx, kw, kw_pos, topn, topk):
    f32 = jnp.float32
    bn, tq, g, r, _ = qn.shape
    nb = kc.shape[1]
    L = kidx.shape[1]
    scale = HEAD_DIM ** -0.5
    m_n = alibi_slopes(NSA_HEADS).reshape(g, r)
    m_d = alibi_slopes(DSA_HEADS).reshape(KV_GROUPS, DSA_HEADS // KV_GROUPS)
    jb = jnp.arange(nb)
    dist_c = q_pos[:, None] - (jb * BLK + BLK - 1)[None, :]
    s_c = jnp.einsum('btgrd,bjgd->btgrj', qn, kc).astype(f32) * scale - m_n[:, :, None] * dist_c[:, None, None, :].astype(f32)
    p_c = masked_softmax(s_c, (dist_c >= 0)[:, None, None, :])
    o_c = jnp.einsum('btgrj,bjgd->btgrd', p_c.astype(vc.dtype), vc)
    cur = q_pos // BLK
    imp = jnp.sum(p_c, axis=3)
    forced = (jb[None, :] == 0) | (jb[None, :] == cur[:, None]) | (jb[None, :] == cur[:, None] - 1)
    imp = jnp.where(forced[:, None, :], FORCE_SCORE, imp)
    imp = jnp.where((jb[None, :] <= cur[:, None])[:, None, :], imp, -jnp.inf)
    top_val, top_idx = lax.top_k(imp, topn)
    b_ix = jnp.arange(bn)[:, None, None, None]
    g_ix = jnp.arange(g)[None, None, :, None]
    sel = slc[b_ix, g_ix, top_idx]
    pos = top_idx[..., None] * BLK + jnp.arange(BLK)
    dist_s = q_pos[None, :, None, None, None] - pos
    mask_s = (dist_s >= 0) & jnp.isfinite(top_val)[..., None]
    s_s = jnp.einsum('btgrd,btgnpd->btgrnp', qn, sel[..., 0, :]).astype(f32) * scale - m_n[None, None, :, :, None, None] * dist_s[:, :, :, None].astype(f32)
    shp = s_s.shape
    p_s = masked_softmax(s_s.reshape(shp[:4] + (-1,)), mask_s[:, :, :, None].reshape(bn, tq, g, 1, -1)).reshape(shp)
    o_s = jnp.einsum('btgrnp,btgnpd->btgrd', p_s.astype(sel.dtype), sel[..., 1, :])
    dist_w = q_pos[:, None] - kw_pos[None, :]
    mask_w = (dist_w >= 0) & (dist_w < WINDOW) & (kw_pos >= 0)[None, :]
    s_w = jnp.einsum('btgrd,blgd->btgrl', qn, kw[:, :, 0]).astype(f32) * scale - m_n[:, :, None] * dist_w[:, None, None, :].astype(f32)
    p_w = masked_softmax(s_w, mask_w[:, None, None, :])
    o_w = jnp.einsum('btgrl,blgd->btgrd', p_w.astype(kw.dtype), kw[:, :, 1])
    o_n = (gates[..., 0:1] * o_c + gates[..., 1:2] * o_s + gates[..., 2:3] * o_w).astype(qn.dtype).reshape(bn, tq, -1)
    k_pos = jnp.arange(L)
    isc = jnp.einsum('bthe,ble->bthl', qi, kidx).astype(f32) * (IDX_DIM ** -0.5)
    isc = jnp.einsum('bthl,bth->btl', jax.nn.relu(isc), wi.astype(f32)) * (IDX_HEADS ** -0.5)
    isc = jnp.where((k_pos[None, :] <= q_pos[:, None])[None], isc, -jnp.inf)
    i_val, i_idx = lax.top_k(isc, topk)
    sel_d = dsa_kv[jnp.arange(bn)[:, None, None], i_idx]
    dist_d = (q_pos[None, :, None] - i_idx).astype(f32)
    s_d = jnp.einsum('btgrd,btkgd->btgrk', qd, sel_d[:, :, :, 0]).astype(f32) * scale - m_d[None, None, :, :, None] * dist_d[:, :, None, None, :]
    p_d = masked_softmax(s_d, jnp.isfinite(i_val)[:, :, None, None, :])
    o_d = jnp.einsum('btgrk,btkgd->btgrd', p_d.astype(sel_d.dtype), sel_d[:, :, :, 1]).reshape(bn, tq, -1)
    return jnp.concatenate([o_n, o_d.astype(qn.dtype)], axis=-1)


def odd_mixer(x, w_in, w_cmp, w_out, past_kv, past_idx, past_win, keep):
    bn, t, _ = x.shape
    p_len = past_kv.shape[1]
    wp = past_win.shape[1]
    g, r = KV_GROUPS, NSA_HEADS // KV_GROUPS
    qn, kvc, kvs, kvw, gates, qd, kvd, qi, ki, wi = _split(x @ w_in, O_SIZES)
    qn = qn.reshape(bn, t, g, r, HEAD_DIM)
    qd = qd.reshape(bn, t, g, DSA_HEADS // g, HEAD_DIM)
    gates = jax.nn.sigmoid(gates.astype(jnp.float32)).reshape(bn, t, g, r, 3)
    qi = qi.reshape(bn, t, IDX_HEADS, IDX_DIM)
    kv_new = jnp.concatenate([kvc, kvs, kvd], axis=-1).reshape(bn, t, KV_SLOTS, g, HEAD_DIM)
    kvw_new = kvw.reshape(bn, t, 2, g, HEAD_DIM)
    kv = jnp.concatenate([past_kv.astype(kv_new.dtype), kv_new], axis=1)
    kidx = jnp.concatenate([past_idx.astype(ki.dtype), ki], axis=1)
    win_all = jnp.concatenate([past_win.astype(kvw_new.dtype), kvw_new], axis=1)
    L = p_len + t
    nb = -(-L // BLK)
    kvb = jnp.pad(kv[:, :, :4], ((0, 0), (0, nb * BLK - L), (0, 0), (0, 0), (0, 0)))
    kvb = kvb.reshape(bn, nb, BLK, 4, g, HEAD_DIM)
    kc = jnp.einsum('bjpgd,pd->bjgd', kvb[:, :, :, 0], w_cmp[0])
    vc = jnp.einsum('bjpgd,pd->bjgd', kvb[:, :, :, 1], w_cmp[1])
    slc = jnp.transpose(kvb[:, :, :, 2:4], (0, 4, 1, 2, 3, 5))
    dsa_kv = kv[:, :, 4:6]
    topn = min(NSA_TOPN, nb)
    topk = min(DSA_TOPK_MAX, L // 4)
    qb = t if t <= Q_BLOCK else Q_BLOCK
    nqb = t // qb

    def blocks(a):
        return jnp.moveaxis(a.reshape((bn, nqb, qb) + a.shape[2:]), 1, 0)

    def body(args):
        qn_b, g_b, qd_b, qi_b, wi_b, start = args
        q_pos = p_len + start + jnp.arange(qb)
        kw = lax.dynamic_slice_in_dim(win_all, start, wp + qb, axis=1)
        kw_pos = p_len - wp + start + jnp.arange(wp + qb)
        return odd_attend(qn_b, g_b, qd_b, qi_b, wi_b, q_pos, kc, vc, slc, dsa_kv, kidx, kw, kw_pos, topn, topk)

    starts = jnp.arange(nqb, dtype=jnp.int32) * qb
    o = lax.map(body, (blocks(qn), blocks(gates), blocks(qd), blocks(qi), blocks(wi), starts))
    o = jnp.moveaxis(o, 0, 1).reshape(bn, t, MIX_W)
    new_win = win_all[:, win_all.shape[1] - keep:]
    return o @ w_out, kv_new, ki, new_win


def setup_inputs(seed: int = 0) -> dict:
    key = jax.random.key(seed)
    ks = jax.random.split(key, 24)

    def nrm(k, shape, s=1.0):
        return s * jax.random.normal(k, shape, jnp.float32)

    n_pages = PAST_LEN // PAGE_SIZE
    n_used = DEC_BATCH * n_pages
    n_pool = n_used + max(1, n_used // 4)
    win_buf = min(WINDOW, PAST_LEN)
    x_prompt = nrm(ks[0], (BATCH, SEQ, D_MODEL))
    x_sample = nrm(ks[1], (DEC_BATCH, DEC_SEQ, D_MODEL))
    state_conv = nrm(ks[2], (N_EVEN, DEC_BATCH, CONV_W - 1, CONV_CH))
    state_C = nrm(ks[3], (N_EVEN, DEC_BATCH, ML_HEADS, ML_DIM, ML_DIM), 0.1)
    state_n = nrm(ks[4], (N_EVEN, DEC_BATCH, ML_HEADS, ML_DIM), 0.1)
    state_m = nrm(ks[5], (N_EVEN, DEC_BATCH, ML_HEADS))
    cache_kv = nrm(ks[6], (N_ODD, n_pool, PAGE_SIZE, KV_SLOTS, KV_GROUPS, HEAD_DIM))
    cache_idx = nrm(ks[7], (N_ODD, n_pool, PAGE_SIZE, IDX_DIM))
    cache_win = nrm(ks[8], (N_ODD, DEC_BATCH, win_buf, 2, KV_GROUPS, HEAD_DIM))
    page_table = jax.random.permutation(ks[9], n_pool)[:n_used].reshape(DEC_BATCH, n_pages).astype(jnp.int32)
    w_in_e = nrm(ks[10], (N_EVEN, D_MODEL, E_COLS), D_MODEL ** -0.5)
    gate_b_e = jnp.concatenate([nrm(ks[11], (N_EVEN, ML_HEADS), 0.1), 3.0 + nrm(ks[12], (N_EVEN, ML_HEADS), 0.1)], axis=-1)
    conv_w = nrm(ks[13], (N_EVEN, CONV_W, CONV_CH), CONV_W ** -0.5)
    conv_b = nrm(ks[14], (N_EVEN, CONV_CH), 0.01)
    w_out_e = nrm(ks[15], (N_EVEN, MIX_W, D_MODEL), BETA * MIX_W ** -0.5)
    w_in_o = nrm(ks[16], (N_ODD, D_MODEL, O_COLS), D_MODEL ** -0.5)
    w_cmp = (1.0 + nrm(ks[17], (N_ODD, 2, BLK, HEAD_DIM), 0.1)) / BLK
    w_out_o = nrm(ks[18], (N_ODD, MIX_W, D_MODEL), BETA * MIX_W ** -0.5)
    w_up = nrm(ks[19], (DEPTH, D_MODEL, D_FF), D_MODEL ** -0.5)
    w_down = nrm(ks[20], (DEPTH, D_FF, D_MODEL), BETA * D_FF ** -0.5)
    ln_g = 1.0 + nrm(ks[21], (DEPTH, 2, D_MODEL), 0.05)
    ln_b = nrm(ks[22], (DEPTH, 2, D_MODEL), 0.02)
    return {'x_prompt': x_prompt, 'x_sample': x_sample, 'state_conv': state_conv, 'state_C': state_C,
            'state_n': state_n, 'state_m': state_m, 'cache_kv': cache_kv, 'cache_idx': cache_idx,
            'cache_win': cache_win, 'page_table': page_table, 'w_in_e': w_in_e, 'gate_b_e': gate_b_e,
            'conv_w': conv_w, 'conv_b': conv_b, 'w_out_e': w_out_e, 'w_in_o': w_in_o, 'w_cmp': w_cmp,
            'w_out_o': w_out_o, 'w_up': w_up, 'w_down': w_down, 'ln_g': ln_g, 'ln_b': ln_b}


def reference(x_prompt, x_sample, state_conv, state_C, state_n, state_m, cache_kv, cache_idx, cache_win, page_table,
              w_in_e, gate_b_e, conv_w, conv_b, w_out_e, w_in_o, w_cmp, w_out_o, w_up, w_down, ln_g, ln_b):
    f32 = jnp.float32
    bp, dbs = x_prompt.shape[0], x_sample.shape[0]
    dt = x_prompt.dtype
    past = page_table.shape[1] * cache_kv.shape[2]
    keep = cache_win.shape[2]
    yp, ys = x_prompt, x_sample
    cv_pl, cv_sl, c_pl, c_sl, n_pl, n_sl, m_pl, m_sl = [], [], [], [], [], [], [], []
    kv_pl, kv_sl, ix_pl, ix_sl, w_pl, w_sl = [], [], [], [], [], []
    for layer in range(DEPTH):
        i = layer // 2
        if layer % 2 == 0:
            zero_state = (jnp.zeros((bp, ML_HEADS, ML_DIM, ML_DIM), f32), jnp.zeros((bp, ML_HEADS, ML_DIM), f32), jnp.zeros((bp, ML_HEADS), f32))
            mp, cvp, stp = even_mixer(yp, w_in_e[i], gate_b_e[i], conv_w[i], conv_b[i], w_out_e[i],
                                      jnp.zeros((bp, CONV_W - 1, CONV_CH), dt), zero_state, ML_CHUNK)
            ms, cvs, sts = even_mixer(ys, w_in_e[i], gate_b_e[i], conv_w[i], conv_b[i], w_out_e[i],
                                      state_conv[i], (state_C[i], state_n[i], state_m[i]), ys.shape[1])
            cv_pl.append(cvp); cv_sl.append(cvs)
            c_pl.append(stp[0]); c_sl.append(sts[0])
            n_pl.append(stp[1]); n_sl.append(sts[1])
            m_pl.append(stp[2]); m_sl.append(sts[2])
        else:
            past_kv = cache_kv[i][page_table].reshape(dbs, past, KV_SLOTS, KV_GROUPS, HEAD_DIM)
            past_idx = cache_idx[i][page_table].reshape(dbs, past, IDX_DIM)
            mp, kvp, ixp, wnp = odd_mixer(yp, w_in_o[i], w_cmp[i], w_out_o[i],
                                          jnp.zeros((bp, 0, KV_SLOTS, KV_GROUPS, HEAD_DIM), dt),
                                          jnp.zeros((bp, 0, IDX_DIM), dt),
                                          jnp.zeros((bp, WINDOW, 2, KV_GROUPS, HEAD_DIM), dt), keep)
            ms, kvs, ixs, wns = odd_mixer(ys, w_in_o[i], w_cmp[i], w_out_o[i], past_kv, past_idx, cache_win[i], keep)
            kv_pl.append(kvp); kv_sl.append(kvs)
            ix_pl.append(ixp); ix_sl.append(ixs)
            w_pl.append(wnp); w_sl.append(wns)
        yp = layer_norm(ALPHA * yp + mp, ln_g[layer, 0], ln_b[layer, 0])
        ys = layer_norm(ALPHA * ys + ms, ln_g[layer, 0], ln_b[layer, 0])
        yp = layer_norm(ALPHA * yp + sq_relu_mlp(yp, w_up[layer], w_down[layer]), ln_g[layer, 1], ln_b[layer, 1])
        ys = layer_norm(ALPHA * ys + sq_relu_mlp(ys, w_up[layer], w_down[layer]), ln_g[layer, 1], ln_b[layer, 1])
    conv_p, conv_s = jnp.stack(cv_pl), jnp.stack(cv_sl)
    c_p, c_s = jnp.stack(c_pl), jnp.stack(c_sl)
    n_p, n_s = jnp.stack(n_pl), jnp.stack(n_sl)
    m_p, m_s = jnp.stack(m_pl), jnp.stack(m_sl)
    kv_p, kv_s = jnp.stack(kv_pl), jnp.stack(kv_sl)
    idx_p, idx_s = jnp.stack(ix_pl), jnp.stack(ix_sl)
    win_p, win_s = jnp.stack(w_pl), jnp.stack(w_sl)
    return (yp, ys, conv_p, conv_s, c_p, c_s, n_p, n_s, m_p, m_s, kv_p, kv_s, idx_p, idx_s, win_p, win_s)
```

```python
import functools

import jax
import jax.numpy as jnp
import numpy as np
from jax import lax
from jax.experimental import pallas as pl
from jax.experimental.pallas import tpu as pltpu

D_MODEL = 1024
DEPTH = 2
HALF_W = 512
D_FF = 4096
CONV_W = 3
ML_HEADS = 4
ML_DIM = 128
ML_CHUNK = 64
HEAD_DIM = 64
KV_GROUPS = 2
NSA_HEADS = 8
DSA_HEADS = 8
BLK = 64
NSA_TOPN = 8
WINDOW = 256
IDX_HEADS = 4
IDX_DIM = 32
DSA_TOPK_MAX = 256
Q_BLOCK = 128
KV_SLOTS = 6
FORCE_SCORE = 1e4
ALPHA = (2.0 * DEPTH) ** 0.25
LN_EPS = 1e-5
E_SIZES = (512, 512, 512, 512, 512, 512, 512, 4, 4)
O_SIZES = (512, 256, 256, 256, 24, 512, 256, 128, 32, 4)
E_COLS = sum(E_SIZES)
O_COLS = sum(O_SIZES)
LANES = 128
VMEM_LIMIT = 48 * 1024 * 1024


def _round_up(n, m):
    return -(-n // m) * m


def _proj_kernel(x_ref, w_ref, o_ref):
    o_ref[...] = jnp.dot(x_ref[...].astype(jnp.bfloat16), w_ref[...],
                         preferred_element_type=jnp.float32)


def _proj(x, w_bf16, tm):
    m, k = x.shape
    n = w_bf16.shape[1]
    tn = n
    for cand in (1024, 768, 512, 256, 128):
        if n % cand == 0:
            tn = cand
            break
    return pl.pallas_call(
        _proj_kernel,
        grid=(n // tn, m // tm),
        in_specs=[pl.BlockSpec((tm, k), lambda j, i: (i, 0)),
                  pl.BlockSpec((k, tn), lambda j, i: (0, j))],
        out_specs=pl.BlockSpec((tm, tn), lambda j, i: (i, j)),
        out_shape=jax.ShapeDtypeStruct((m, n), jnp.float32),
        compiler_params=pltpu.CompilerParams(
            dimension_semantics=("arbitrary", "arbitrary"), vmem_limit_bytes=VMEM_LIMIT),
        name="proj",
    )(x, w_bf16)


def _layer_norm_rows(v, g, b):
    mu = jnp.mean(v, axis=-1, keepdims=True)
    d = v - mu
    var = jnp.mean(d * d, axis=-1, keepdims=True)
    return d * lax.rsqrt(var + LN_EPS) * g + b


def _outproj_ln_kernel(y_ref, w_ref, x_ref, g_ref, b_ref, o_ref):
    mix = jnp.dot(y_ref[...].astype(jnp.bfloat16), w_ref[...], preferred_element_type=jnp.float32)
    o_ref[...] = _layer_norm_rows(ALPHA * x_ref[...] + mix, g_ref[...], b_ref[...])


def _outproj_ln(y, w_bf16, x, g, b, tm):
    m, d = x.shape
    k = y.shape[1]
    return pl.pallas_call(
        _outproj_ln_kernel,
        grid=(m // tm,),
        in_specs=[pl.BlockSpec((tm, k), lambda i: (i, 0)),
                  pl.BlockSpec((k, d), lambda i: (0, 0)),
                  pl.BlockSpec((tm, d), lambda i: (i, 0)),
                  pl.BlockSpec((1, d), lambda i: (0, 0)),
                  pl.BlockSpec((1, d), lambda i: (0, 0))],
        out_specs=pl.BlockSpec((tm, d), lambda i: (i, 0)),
        out_shape=jax.ShapeDtypeStruct((m, d), jnp.float32),
        compiler_params=pltpu.CompilerParams(
            dimension_semantics=("arbitrary",), vmem_limit_bytes=VMEM_LIMIT),
        name="outproj_ln",
    )(y, w_bf16, x, g.reshape(1, d), b.reshape(1, d))


def _mlp_ln_kernel(x_ref, wu_ref, wd_ref, g_ref, b_ref, o_ref, acc_ref):
    f = pl.program_id(1)

    @pl.when(f == 0)
    def _():
        acc_ref[...] = jnp.zeros_like(acc_ref)

    up = jnp.dot(x_ref[...].astype(jnp.bfloat16), wu_ref[...], preferred_element_type=jnp.float32)
    act = jnp.square(jnp.maximum(up, 0.0))
    acc_ref[...] += jnp.dot(act.astype(jnp.bfloat16), wd_ref[...], preferred_element_type=jnp.float32)

    @pl.when(f == pl.num_programs(1) - 1)
    def _():
        o_ref[...] = _layer_norm_rows(ALPHA * x_ref[...] + acc_ref[...], g_ref[...], b_ref[...])


def _mlp_ln(x, wu_bf16, wd_bf16, g, b, tm, tf):
    m, d = x.shape
    ff = wu_bf16.shape[1]
    return pl.pallas_call(
        _mlp_ln_kernel,
        grid=(m // tm, ff // tf),
        in_specs=[pl.BlockSpec((tm, d), lambda i, f: (i, 0)),
                  pl.BlockSpec((d, tf), lambda i, f: (0, f)),
                  pl.BlockSpec((tf, d), lambda i, f: (f, 0)),
                  pl.BlockSpec((1, d), lambda i, f: (0, 0)),
                  pl.BlockSpec((1, d), lambda i, f: (0, 0))],
        out_specs=pl.BlockSpec((tm, d), lambda i, f: (i, 0)),
        out_shape=jax.ShapeDtypeStruct((m, d), jnp.float32),
        scratch_shapes=[pltpu.VMEM((tm, d), jnp.float32)],
        compiler_params=pltpu.CompilerParams(
            dimension_semantics=("arbitrary", "arbitrary"), vmem_limit_bytes=VMEM_LIMIT),
        name="mlp_ln",
    )(x, wu_bf16, wd_bf16, g.reshape(1, d), b.reshape(1, d))


def _split(z, sizes):
    cuts = [int(c) for c in np.cumsum(sizes)[:-1]]
    return jnp.split(z, cuts, axis=-1)


def _alibi_slopes(n):
    return jnp.asarray(np.power(2.0, -8.0 * np.arange(1, n + 1) / n), dtype=jnp.float32)


def _masked_softmax(s, mask):
    s = jnp.where(mask, s.astype(jnp.float32), -jnp.inf)
    m = jnp.max(s, axis=-1, keepdims=True)
    e = jnp.exp(s - jnp.where(jnp.isfinite(m), m, 0.0))
    return e / jnp.maximum(jnp.sum(e, axis=-1, keepdims=True), 1e-30)


def _short_conv(u_ext, w, b):
    t = u_ext.shape[1] - (CONV_W - 1)
    return b + sum(w[j] * u_ext[:, j:j + t] for j in range(CONV_W))


def _mlstm_chunk(carry, inp):
    c_st, n_st, m_st = carry
    q, k, v, ig, lf = inp
    L = q.shape[2]
    b = jnp.cumsum(lf, axis=-1)
    causal = jnp.tril(jnp.ones((L, L), dtype=bool))
    dmat = jnp.where(causal, b[..., :, None] - b[..., None, :] + ig[..., None, :], -jnp.inf)
    inter = b + m_st[..., None]
    m_t = jnp.maximum(inter, jnp.max(dmat, axis=-1))
    w_intra = jnp.exp(dmat - m_t[..., None])
    w_inter = jnp.exp(inter - m_t)
    s = jnp.einsum('bhtd,bhsd->bhts', q, k) * w_intra
    num = jnp.einsum('bhts,bhsd->bhtd', s, v) + w_inter[..., None] * jnp.einsum('bhed,bhtd->bhte', c_st, q)
    den = jnp.sum(s, axis=-1) + w_inter * jnp.einsum('bhd,bhtd->bht', n_st, q)
    h = num / jnp.maximum(jnp.abs(den), jnp.exp(-m_t))[..., None]
    m_new = m_t[..., -1]
    w_state = jnp.exp(b[..., -1:] - b + ig - m_new[..., None])
    decay = jnp.exp(b[..., -1] + m_st - m_new)
    c_new = decay[..., None, None] * c_st + jnp.einsum('bhs,bhse,bhsd->bhed', w_state, v, k)
    n_new = decay[..., None] * n_st + jnp.einsum('bhs,bhsd->bhd', w_state, k)
    return (c_new, n_new, m_new), h


def _even_mixer(z, gate_b, conv_w, conv_b, conv_prev, ml_state, chunk):
    f32 = jnp.float32
    bn, t, _ = z.shape
    h, bg, cg, q, k, v, og, ig, fg = _split(z, E_SIZES)
    u = cg * h
    u_ext = jnp.concatenate([conv_prev.astype(u.dtype), u], axis=1)
    y_a = bg * _short_conv(u_ext, conv_w, conv_b)
    new_conv = u_ext[:, u_ext.shape[1] - (CONV_W - 1):]
    heads = lambda a: a.reshape(bn, t, ML_HEADS, ML_DIM).astype(f32)
    q, k, v = heads(q), heads(k) * (ML_DIM ** -0.5), heads(v)
    ig = (ig + gate_b[:ML_HEADS]).astype(f32)
    lf = jax.nn.log_sigmoid((fg + gate_b[ML_HEADS:]).astype(f32))
    nc = t // chunk

    def to_chunks(a):
        a = jnp.moveaxis(a.reshape((bn, nc, chunk) + a.shape[2:]), 1, 0)
        return jnp.swapaxes(a, 2, 3)

    carry = tuple(s.astype(f32) for s in ml_state)
    carry, hs = lax.scan(_mlstm_chunk, carry, tuple(to_chunks(a) for a in (q, k, v, ig, lf)))
    hs = jnp.swapaxes(jnp.moveaxis(hs, 0, 1), 2, 3).reshape(bn, t, HALF_W)
    y_b = jax.nn.sigmoid(og) * hs
    return jnp.concatenate([y_a, y_b], axis=-1), new_conv, carry


def _odd_attend(qn, gates, qd, qi, wi, q_pos, kc, vc, slc, dsa_kv, kidx, kw, kw_pos, topn, topk):
    f32 = jnp.float32
    bn, tq, g, r, _ = qn.shape
    nb = kc.shape[1]
    L = kidx.shape[1]
    scale = HEAD_DIM ** -0.5
    m_n = _alibi_slopes(NSA_HEADS).reshape(g, r)
    m_d = _alibi_slopes(DSA_HEADS).reshape(KV_GROUPS, DSA_HEADS // KV_GROUPS)
    jb = jnp.arange(nb)
    dist_c = q_pos[:, None] - (jb * BLK + BLK - 1)[None, :]
    s_c = jnp.einsum('btgrd,bjgd->btgrj', qn, kc).astype(f32) * scale - m_n[:, :, None] * dist_c[:, None, None, :].astype(f32)
    p_c = _masked_softmax(s_c, (dist_c >= 0)[:, None, None, :])
    o_c = jnp.einsum('btgrj,bjgd->btgrd', p_c.astype(vc.dtype), vc)
    cur = q_pos // BLK
    imp = jnp.sum(p_c, axis=3)
    forced = (jb[None, :] == 0) | (jb[None, :] == cur[:, None]) | (jb[None, :] == cur[:, None] - 1)
    imp = jnp.where(forced[:, None, :], FORCE_SCORE, imp)
    imp = jnp.where((jb[None, :] <= cur[:, None])[:, None, :], imp, -jnp.inf)
    top_val, top_idx = lax.top_k(imp, topn)
    b_ix = jnp.arange(bn)[:, None, None, None]
    g_ix = jnp.arange(g)[None, None, :, None]
    sel = slc[b_ix, g_ix, top_idx]
    pos = top_idx[..., None] * BLK + jnp.arange(BLK)
    dist_s = q_pos[None, :, None, None, None] - pos
    mask_s = (dist_s >= 0) & jnp.isfinite(top_val)[..., None]
    s_s = jnp.einsum('btgrd,btgnpd->btgrnp', qn, sel[..., 0, :]).astype(f32) * scale - m_n[None, None, :, :, None, None] * dist_s[:, :, :, None].astype(f32)
    shp = s_s.shape
    p_s = _masked_softmax(s_s.reshape(shp[:4] + (-1,)), mask_s[:, :, :, None].reshape(bn, tq, g, 1, -1)).reshape(shp)
    o_s = jnp.einsum('btgrnp,btgnpd->btgrd', p_s.astype(sel.dtype), sel[..., 1, :])
    dist_w = q_pos[:, None] - kw_pos[None, :]
    mask_w = (dist_w >= 0) & (dist_w < WINDOW) & (kw_pos >= 0)[None, :]
    s_w = jnp.einsum('btgrd,blgd->btgrl', qn, kw[:, :, 0]).astype(f32) * scale - m_n[:, :, None] * dist_w[:, None, None, :].astype(f32)
    p_w = _masked_softmax(s_w, mask_w[:, None, None, :])
    o_w = jnp.einsum('btgrl,blgd->btgrd', p_w.astype(kw.dtype), kw[:, :, 1])
    o_n = (gates[..., 0:1] * o_c + gates[..., 1:2] * o_s + gates[..., 2:3] * o_w).astype(qn.dtype).reshape(bn, tq, -1)
    k_pos = jnp.arange(L)
    isc = jnp.einsum('bthe,ble->bthl', qi, kidx).astype(f32) * (IDX_DIM ** -0.5)
    isc = jnp.einsum('bthl,bth->btl', jax.nn.relu(isc), wi.astype(f32)) * (IDX_HEADS ** -0.5)
    isc = jnp.where((k_pos[None, :] <= q_pos[:, None])[None], isc, -jnp.inf)
    i_val, i_idx = lax.top_k(isc, topk)
    sel_d = dsa_kv[jnp.arange(bn)[:, None, None], i_idx]
    dist_d = (q_pos[None, :, None] - i_idx).astype(f32)
    s_d = jnp.einsum('btgrd,btkgd->btgrk', qd, sel_d[:, :, :, 0]).astype(f32) * scale - m_d[None, None, :, :, None] * dist_d[:, :, None, None, :]
    p_d = _masked_softmax(s_d, jnp.isfinite(i_val)[:, :, None, None, :])
    o_d = jnp.einsum('btgrk,btkgd->btgrd', p_d.astype(sel_d.dtype), sel_d[:, :, :, 1]).reshape(bn, tq, -1)
    return jnp.concatenate([o_n, o_d.astype(qn.dtype)], axis=-1)


def _odd_mixer(z, w_cmp, past_kv, past_idx, past_win, keep):
    bn, t, _ = z.shape
    p_len = past_kv.shape[1]
    wp = past_win.shape[1]
    g, r = KV_GROUPS, NSA_HEADS // KV_GROUPS
    qn, kvc, kvs, kvw, gates, qd, kvd, qi, ki, wi = _split(z, O_SIZES)
    qn = qn.reshape(bn, t, g, r, HEAD_DIM)
    qd = qd.reshape(bn, t, g, DSA_HEADS // g, HEAD_DIM)
    gates = jax.nn.sigmoid(gates.astype(jnp.float32)).reshape(bn, t, g, r, 3)
    qi = qi.reshape(bn, t, IDX_HEADS, IDX_DIM)
    kv_new = jnp.concatenate([kvc, kvs, kvd], axis=-1).reshape(bn, t, KV_SLOTS, g, HEAD_DIM)
    kvw_new = kvw.reshape(bn, t, 2, g, HEAD_DIM)
    kv = jnp.concatenate([past_kv.astype(kv_new.dtype), kv_new], axis=1)
    kidx = jnp.concatenate([past_idx.astype(ki.dtype), ki], axis=1)
    win_all = jnp.concatenate([past_win.astype(kvw_new.dtype), kvw_new], axis=1)
    L = p_len + t
    nb = -(-L // BLK)
    kvb = jnp.pad(kv[:, :, :4], ((0, 0), (0, nb * BLK - L), (0, 0), (0, 0), (0, 0)))
    kvb = kvb.reshape(bn, nb, BLK, 4, g, HEAD_DIM)
    kc = jnp.einsum('bjpgd,pd->bjgd', kvb[:, :, :, 0], w_cmp[0])
    vc = jnp.einsum('bjpgd,pd->bjgd', kvb[:, :, :, 1], w_cmp[1])
    slc = jnp.transpose(kvb[:, :, :, 2:4], (0, 4, 1, 2, 3, 5))
    dsa_kv = kv[:, :, 4:6]
    topn = min(NSA_TOPN, nb)
    topk = min(DSA_TOPK_MAX, L // 4)
    qb = t if t <= Q_BLOCK else Q_BLOCK
    nqb = t // qb

    def blocks(a):
        return jnp.moveaxis(a.reshape((bn, nqb, qb) + a.shape[2:]), 1, 0)

    def body(args):
        qn_b, g_b, qd_b, qi_b, wi_b, start = args
        q_pos = p_len + start + jnp.arange(qb)
        kw = lax.dynamic_slice_in_dim(win_all, start, wp + qb, axis=1)
        kw_pos = p_len - wp + start + jnp.arange(wp + qb)
        return _odd_attend(qn_b, g_b, qd_b, qi_b, wi_b, q_pos, kc, vc, slc, dsa_kv, kidx, kw, kw_pos, topn, topk)

    starts = jnp.arange(nqb, dtype=jnp.int32) * qb
    o = lax.map(body, (blocks(qn), blocks(gates), blocks(qd), blocks(qi), blocks(wi), starts))
    o = jnp.moveaxis(o, 0, 1).reshape(bn, t, 2 * HALF_W)
    new_win = win_all[:, win_all.shape[1] - keep:]
    return o, kv_new, ki, new_win


def _pad_cols(w, n):
    return jnp.pad(w, ((0, 0), (0, n - w.shape[1])))


def kernel(x_prompt, x_sample, state_conv, state_C, state_n, state_m, cache_kv, cache_idx, cache_win, page_table,
           w_in_e, gate_b_e, conv_w, conv_b, w_out_e, w_in_o, w_cmp, w_out_o, w_up, w_down, ln_g, ln_b):
    f32, bf16 = jnp.float32, jnp.bfloat16
    bp, tp, d = x_prompt.shape
    dbs = x_sample.shape[0]
    past = page_table.shape[1] * cache_kv.shape[2]
    keep = cache_win.shape[2]
    yp = x_prompt.reshape(bp * tp, d)
    ys = x_sample.reshape(dbs, d)
    tm_p, tm_s = 512, dbs
    outs = {}
    for layer in range(DEPTH):
        i = layer // 2
        if layer % 2 == 0:
            n_pad = _round_up(E_COLS, LANES)
            w_in = _pad_cols(w_in_e[i], n_pad).astype(bf16)
            zp = _proj(yp, w_in, tm_p)[:, :E_COLS].reshape(bp, tp, E_COLS)
            zs = _proj(ys, w_in, tm_s)[:, :E_COLS].reshape(dbs, 1, E_COLS)
            zero_state = (jnp.zeros((bp, ML_HEADS, ML_DIM, ML_DIM), f32), jnp.zeros((bp, ML_HEADS, ML_DIM), f32),
                          jnp.zeros((bp, ML_HEADS), f32))
            mp, cvp, stp = _even_mixer(zp, gate_b_e[i], conv_w[i], conv_b[i],
                                       jnp.zeros((bp, CONV_W - 1, HALF_W), f32), zero_state, ML_CHUNK)
            ms, cvs, sts = _even_mixer(zs, gate_b_e[i], conv_w[i], conv_b[i], state_conv[i],
                                       (state_C[i], state_n[i], state_m[i]), 1)
            outs['conv'] = (cvp[None], cvs[None])
            outs['c'] = (stp[0][None], sts[0][None])
            outs['n'] = (stp[1][None], sts[1][None])
            outs['m'] = (stp[2][None], sts[2][None])
            w_out = w_out_e[i].astype(bf16)
        else:
            n_pad = _round_up(O_COLS, LANES)
            w_in = _pad_cols(w_in_o[i], n_pad).astype(bf16)
            zp = _proj(yp, w_in, tm_p)[:, :O_COLS].reshape(bp, tp, O_COLS)
            zs = _proj(ys, w_in, tm_s)[:, :O_COLS].reshape(dbs, 1, O_COLS)
            past_kv = cache_kv[i][page_table].reshape(dbs, past, KV_SLOTS, KV_GROUPS, HEAD_DIM)
            past_idx = cache_idx[i][page_table].reshape(dbs, past, IDX_DIM)
            mp, kvp, ixp, wnp = _odd_mixer(zp, w_cmp[i], jnp.zeros((bp, 0, KV_SLOTS, KV_GROUPS, HEAD_DIM), f32),
                                           jnp.zeros((bp, 0, IDX_DIM), f32),
                                           jnp.zeros((bp, WINDOW, 2, KV_GROUPS, HEAD_DIM), f32), keep)
            ms, kvs, ixs, wns = _odd_mixer(zs, w_cmp[i], past_kv, past_idx, cache_win[i], keep)
            outs['kv'] = (kvp[None], kvs[None])
            outs['idx'] = (ixp[None], ixs[None])
            outs['win'] = (wnp[None], wns[None])
            w_out = w_out_o[i].astype(bf16)
        yp = _outproj_ln(mp.reshape(bp * tp, d), w_out, yp, ln_g[layer, 0], ln_b[layer, 0], tm_p)
        ys = _outproj_ln(ms.reshape(dbs, d), w_out, ys, ln_g[layer, 0], ln_b[layer, 0], tm_s)
        wu, wd = w_up[layer].astype(bf16), w_down[layer].astype(bf16)
        yp = _mlp_ln(yp, wu, wd, ln_g[layer, 1], ln_b[layer, 1], tm_p, 1024)
        ys = _mlp_ln(ys, wu, wd, ln_g[layer, 1], ln_b[layer, 1], tm_s, 1024)
    return (yp.reshape(bp, tp, d), ys.reshape(dbs, 1, d),
            outs['conv'][0], outs['conv'][1], outs['c'][0], outs['c'][1],
            outs['n'][0], outs['n'][1], outs['m'][0], outs['m'][1],
            outs['kv'][0], outs['kv'][1], outs['idx'][0], outs['idx'][1],
            outs['win'][0], outs['win'][1])
```

```python
import functools

import jax
import jax.numpy as jnp
import numpy as np
from jax import lax
from jax.experimental import pallas as pl
from jax.experimental.pallas import tpu as pltpu

D_MODEL = 1024
DEPTH = 2
HALF_W = 512
D_FF = 4096
CONV_W = 3
ML_HEADS = 4
ML_DIM = 128
ML_CHUNK = 64
HEAD_DIM = 64
KV_GROUPS = 2
NSA_HEADS = 8
DSA_HEADS = 8
BLK = 64
NSA_TOPN = 8
WINDOW = 256
IDX_HEADS = 4
IDX_DIM = 32
DSA_TOPK_MAX = 256
Q_BLOCK = 128
KV_SLOTS = 6
FORCE_SCORE = 1e4
ALPHA = (2.0 * DEPTH) ** 0.25
LN_EPS = 1e-5
E_SIZES = (512, 512, 512, 512, 512, 512, 512, 4, 4)
O_SIZES = (512, 256, 256, 256, 24, 512, 256, 128, 32, 4)
E_COLS = sum(E_SIZES)
O_COLS = sum(O_SIZES)
LANES = 128
VMEM_LIMIT = 48 * 1024 * 1024


def _round_up(n, m):
    return -(-n // m) * m


def _proj_kernel(x_ref, w_ref, o_ref):
    o_ref[...] = jnp.dot(x_ref[...].astype(jnp.bfloat16), w_ref[...],
                         preferred_element_type=jnp.float32)


def _proj(x, w_bf16, tm):
    m, k = x.shape
    n = w_bf16.shape[1]
    tn = n
    for cand in (1024, 768, 512, 256, 128):
        if n % cand == 0:
            tn = cand
            break
    return pl.pallas_call(
        _proj_kernel,
        grid=(n // tn, m // tm),
        in_specs=[pl.BlockSpec((tm, k), lambda j, i: (i, 0)),
                  pl.BlockSpec((k, tn), lambda j, i: (0, j))],
        out_specs=pl.BlockSpec((tm, tn), lambda j, i: (i, j)),
        out_shape=jax.ShapeDtypeStruct((m, n), jnp.float32),
        compiler_params=pltpu.CompilerParams(
            dimension_semantics=("arbitrary", "arbitrary"), vmem_limit_bytes=VMEM_LIMIT),
        name="proj",
    )(x, w_bf16)


def _layer_norm_rows(v, g, b):
    mu = jnp.mean(v, axis=-1, keepdims=True)
    d = v - mu
    var = jnp.mean(d * d, axis=-1, keepdims=True)
    return d * lax.rsqrt(var + LN_EPS) * g + b


def _outproj_ln_kernel(y_ref, w_ref, x_ref, g_ref, b_ref, o_ref):
    mix = jnp.dot(y_ref[...].astype(jnp.bfloat16), w_ref[...], preferred_element_type=jnp.float32)
    o_ref[...] = _layer_norm_rows(ALPHA * x_ref[...] + mix, g_ref[...], b_ref[...])


def _outproj_ln(y, w_bf16, x, g, b, tm):
    m, d = x.shape
    k = y.shape[1]
    return pl.pallas_call(
        _outproj_ln_kernel,
        grid=(m // tm,),
        in_specs=[pl.BlockSpec((tm, k), lambda i: (i, 0)),
                  pl.BlockSpec((k, d), lambda i: (0, 0)),
                  pl.BlockSpec((tm, d), lambda i: (i, 0)),
                  pl.BlockSpec((1, d), lambda i: (0, 0)),
                  pl.BlockSpec((1, d), lambda i: (0, 0))],
        out_specs=pl.BlockSpec((tm, d), lambda i: (i, 0)),
        out_shape=jax.ShapeDtypeStruct((m, d), jnp.float32),
        compiler_params=pltpu.CompilerParams(
            dimension_semantics=("arbitrary",), vmem_limit_bytes=VMEM_LIMIT),
        name="outproj_ln",
    )(y, w_bf16, x, g.reshape(1, d), b.reshape(1, d))


def _mlp_ln_kernel(x_ref, wu_ref, wd_ref, g_ref, b_ref, o_ref, acc_ref):
    f = pl.program_id(1)

    @pl.when(f == 0)
    def _():
        acc_ref[...] = jnp.zeros_like(acc_ref)

    up = jnp.dot(x_ref[...].astype(jnp.bfloat16), wu_ref[...], preferred_element_type=jnp.float32)
    act = jnp.square(jnp.maximum(up, 0.0))
    acc_ref[...] += jnp.dot(act.astype(jnp.bfloat16), wd_ref[...], preferred_element_type=jnp.float32)

    @pl.when(f == pl.num_programs(1) - 1)
    def _():
        o_ref[...] = _layer_norm_rows(ALPHA * x_ref[...] + acc_ref[...], g_ref[...], b_ref[...])


def _mlp_ln(x, wu_bf16, wd_bf16, g, b, tm, tf):
    m, d = x.shape
    ff = wu_bf16.shape[1]
    return pl.pallas_call(
        _mlp_ln_kernel,
        grid=(m // tm, ff // tf),
        in_specs=[pl.BlockSpec((tm, d), lambda i, f: (i, 0)),
                  pl.BlockSpec((d, tf), lambda i, f: (0, f)),
                  pl.BlockSpec((tf, d), lambda i, f: (f, 0)),
                  pl.BlockSpec((1, d), lambda i, f: (0, 0)),
                  pl.BlockSpec((1, d), lambda i, f: (0, 0))],
        out_specs=pl.BlockSpec((tm, d), lambda i, f: (i, 0)),
        out_shape=jax.ShapeDtypeStruct((m, d), jnp.float32),
        scratch_shapes=[pltpu.VMEM((tm, d), jnp.float32)],
        compiler_params=pltpu.CompilerParams(
            dimension_semantics=("arbitrary", "arbitrary"), vmem_limit_bytes=VMEM_LIMIT),
        name="mlp_ln",
    )(x, wu_bf16, wd_bf16, g.reshape(1, d), b.reshape(1, d))


O_Q = 0
O_KV = 1024
O_WIN = 1792
O_QI = 2048
O_MISC = 2176
O_PACKED = 2304
MISC_WI = IDX_DIM
MISC_GATE = IDX_DIM + IDX_HEADS
NEG_BIG = -(2.0 ** 100)
INT_MIN = -(2 ** 31)
ALIBI = tuple(float(2.0 ** (-8.0 * (h + 1) / NSA_HEADS)) for h in range(NSA_HEADS))


def _pack_w_in_o(w):
    qn, kvc, kvs, kvw, gates, qd, kvd, qi, ki, wi = _split(w, O_SIZES)
    scale = HEAD_DIM ** -0.5
    pad = jnp.zeros((w.shape[0], O_PACKED - O_MISC - IDX_DIM - IDX_HEADS - 3 * NSA_HEADS), w.dtype)
    return jnp.concatenate([qn * scale, qd * scale, kvc, kvs, kvd, kvw, qi, ki, wi, gates, pad], axis=1)


def _odd_proj_kernel(x_ref, w_ref, zq_ref, kv_ref, kvb_ref, win_ref, misc_ref):
    z = jnp.dot(x_ref[...].astype(jnp.bfloat16), w_ref[...], preferred_element_type=jnp.float32)
    zq_ref[:, 0:O_KV] = z[:, 0:O_KV].astype(jnp.bfloat16)
    zq_ref[:, O_KV:O_KV + LANES] = z[:, O_QI:O_MISC].astype(jnp.bfloat16)
    kv_ref[...] = z[:, O_KV:O_WIN]
    kvb_ref[...] = z[:, O_KV:O_QI].astype(jnp.bfloat16)
    win_ref[...] = z[:, O_WIN:O_QI]
    misc_ref[...] = z[:, O_MISC:O_PACKED]


def _odd_proj(x, w_packed_bf16, tm):
    m, k = x.shape
    widths = (O_KV + LANES, O_WIN - O_KV, O_QI - O_KV, O_QI - O_WIN, LANES)
    dtypes = (jnp.bfloat16, jnp.float32, jnp.bfloat16, jnp.float32, jnp.float32)
    return pl.pallas_call(
        _odd_proj_kernel,
        grid=(m // tm,),
        in_specs=[pl.BlockSpec((tm, k), lambda i: (i, 0)),
                  pl.BlockSpec((k, O_PACKED), lambda i: (0, 0))],
        out_specs=[pl.BlockSpec((tm, n), lambda i: (i, 0)) for n in widths],
        out_shape=[jax.ShapeDtypeStruct((m, n), dt) for n, dt in zip(widths, dtypes)],
        compiler_params=pltpu.CompilerParams(
            dimension_semantics=("arbitrary",), vmem_limit_bytes=VMEM_LIMIT),
        name="odd_proj",
    )(x, w_packed_bf16)


def _dot_nt(a, b):
    return lax.dot_general(a, b, (((1,), (1,)), ((), ())), preferred_element_type=jnp.float32)


def _flash_tiles(q4, kv_ref, k_col, v_col, tk, lo, hi, slope_col4, bias_fn):
    rows = q4.shape[0]

    def body(kt, carry):
        m, l, acc = carry
        start = pl.multiple_of(kt * tk, tk)
        k = kv_ref[0, pl.ds(start, tk), k_col:k_col + HEAD_DIM]
        v = kv_ref[0, pl.ds(start, tk), v_col:v_col + HEAD_DIM]
        s = _dot_nt(q4, k)
        sp = (start + lax.broadcasted_iota(jnp.int32, (1, tk), 1)).astype(jnp.float32)
        b = bias_fn(kt)
        s = s + slope_col4 * sp + jnp.concatenate([b, b, b, b], axis=0)
        m_new = jnp.maximum(m, jnp.max(s, axis=-1, keepdims=True))
        alpha = jnp.exp(m - m_new)
        p = jnp.exp(s - m_new)
        l = alpha * l + jnp.sum(p, axis=-1, keepdims=True)
        acc = alpha * acc + jnp.dot(p.astype(jnp.bfloat16), v, preferred_element_type=jnp.float32)
        return m_new, l, acc

    m0 = jnp.full((rows, 1), -jnp.inf, jnp.float32)
    l0 = jnp.zeros((rows, 1), jnp.float32)
    a0 = jnp.zeros((rows, HEAD_DIM), jnp.float32)
    _, l, acc = lax.fori_loop(lo, hi, body, (m0, l0, a0))
    return acc * (1.0 / l)


def _odd_attn_kernel(zq_ref, miscq_ref, kvb_ref, misck_ref, wc_ref, o_ref,
                     kc_ref, vc_ref, ki_ref, key_ref, dbias_ref, *, t_len, tq, tk, tkw, topk):
    f32, bf16, i32 = jnp.float32, jnp.bfloat16, jnp.int32
    nb = t_len // BLK
    qi_blk = pl.program_id(1)
    q0 = qi_blk * tq

    @pl.when(qi_blk == 0)
    def _():
        ck = kvb_ref[0, :, 0:LANES].astype(f32).reshape(nb, BLK, LANES)
        kc_ref[...] = jnp.sum(ck * wc_ref[0][None], axis=1).astype(bf16)
        cv = kvb_ref[0, :, LANES:2 * LANES].astype(f32).reshape(nb, BLK, LANES)
        vc_ref[...] = jnp.sum(cv * wc_ref[1][None], axis=1).astype(bf16)
        ki_ref[...] = misck_ref[0, :, 0:IDX_DIM].astype(bf16)

    zq = zq_ref[0]
    misc = miscq_ref[0]
    gates = jax.nn.sigmoid(misc[:, MISC_GATE:MISC_GATE + 3 * NSA_HEADS])
    t_col = q0 + lax.broadcasted_iota(i32, (tq, 1), 0)

    def head_cols(base, g):
        return [zq[:, base + (g * 4 + r) * HEAD_DIM: base + (g * 4 + r + 1) * HEAD_DIM] for r in range(4)]

    def slope4(g):
        return jnp.concatenate([jnp.full((tq, 1), ALIBI[g * 4 + r], f32) for r in range(4)], axis=0)

    def gate4(g, c):
        return jnp.concatenate([gates[:, (g * 4 + r) * 3 + c:(g * 4 + r) * 3 + c + 1] for r in range(4)], axis=0)

    def unstack(x4):
        return jnp.concatenate([x4[r * tq:(r + 1) * tq] for r in range(4)], axis=1)

    jb = lax.broadcasted_iota(i32, (1, nb), 1)
    dist_c = t_col - (jb * BLK + (BLK - 1))
    mask_c = dist_c >= 0
    dist_cf = dist_c.astype(f32)
    cur = t_col // BLK
    forced = (jb == 0) | (jb == cur) | (jb == cur - 1)
    admissible = jb <= cur
    jb_full = lax.broadcasted_iota(i32, (tq, nb), 1)
    o_cmp, sel_bias = [], []
    for g in range(KV_GROUPS):
        kc_g = kc_ref[:, g * HEAD_DIM:(g + 1) * HEAD_DIM]
        vc_g = vc_ref[:, g * HEAD_DIM:(g + 1) * HEAD_DIM]
        imp = jnp.zeros((tq, nb), f32)
        heads = []
        for r, qh in enumerate(head_cols(O_Q, g)):
            s = _dot_nt(qh, kc_g) - ALIBI[g * 4 + r] * dist_cf
            s = jnp.where(mask_c, s, NEG_BIG)
            m = jnp.max(s, axis=-1, keepdims=True)
            e = jnp.where(mask_c, jnp.exp(s - m), 0.0)
            p = e * (1.0 / jnp.maximum(jnp.sum(e, axis=-1, keepdims=True), 1e-30))
            imp = imp + p
            heads.append(jnp.dot(p.astype(bf16), vc_g, preferred_element_type=f32))
        o_cmp.append(jnp.concatenate(heads, axis=0))
        imp = jnp.where(forced, FORCE_SCORE, imp)
        imp = jnp.where(admissible, imp, -jnp.inf)
        sel = jnp.zeros((tq, nb), f32)
        for _ in range(min(NSA_TOPN, nb)):
            m = jnp.max(imp, axis=-1, keepdims=True)
            first = jnp.min(jnp.where(imp == m, jb_full, nb), axis=-1, keepdims=True)
            hit = jb_full == first
            sel = jnp.where(hit & (m > -jnp.inf), 1.0, sel)
            imp = jnp.where(hit, -jnp.inf, imp)
        sel_bias.append(jnp.where(sel > 0.0, 0.0, NEG_BIG).astype(bf16))

    n_sel_tiles = (q0 + tq + tk - 1) // tk
    blocks_per_tile = tk // BLK

    def causal_bias(start, width):
        sp = start + lax.broadcasted_iota(i32, (1, width), 1)
        return jnp.where(sp <= t_col, 0.0, NEG_BIG)

    def sel_bias_fn(g):
        def fn(kt):
            row = lax.broadcasted_iota(i32, (nb, tk), 0)
            col = lax.broadcasted_iota(i32, (nb, tk), 1)
            expand = jnp.where(row == kt * blocks_per_tile + col // BLK, 1.0, 0.0).astype(bf16)
            return jnp.dot(sel_bias[g], expand, preferred_element_type=f32) + causal_bias(kt * tk, tk)
        return fn

    def win_bias_fn(kt):
        sp = kt * tkw + lax.broadcasted_iota(i32, (1, tkw), 1)
        dist = t_col - sp
        return jnp.where((dist >= 0) & (dist < WINDOW), 0.0, NEG_BIG)

    win_lo = jnp.maximum(q0 - WINDOW, 0) // tkw
    win_hi = (q0 + tq + tkw - 1) // tkw

    qi_heads = [zq[:, O_KV + h * IDX_DIM: O_KV + (h + 1) * IDX_DIM] for h in range(IDX_HEADS)]
    wi_cols = [misc[:, MISC_WI + h: MISC_WI + h + 1] for h in range(IDX_HEADS)]

    def index_tile(kt, carry):
        start = pl.multiple_of(kt * tk, tk)
        kik = ki_ref[pl.ds(start, tk), :]
        tot = jnp.zeros((tq, tk), f32)
        for h in range(IDX_HEADS):
            sc = _dot_nt(qi_heads[h], kik) * (IDX_DIM ** -0.5)
            tot = tot + jnp.maximum(sc, 0.0) * wi_cols[h]
        tot = tot * (IDX_HEADS ** -0.5)
        tot = jnp.where(tot == 0.0, 0.0, tot)
        bits = lax.bitcast_convert_type(tot, i32)
        key = jnp.where(bits < 0, bits ^ jnp.int32(0x7FFFFFFF), bits)
        sp = start + lax.broadcasted_iota(i32, (1, tk), 1)
        key_ref[kt] = jnp.where(sp <= t_col, key, INT_MIN)
        return carry

    lax.fori_loop(0, n_sel_tiles, index_tile, 0)

    def count_ge(trial):
        def body(kt, acc):
            c = (key_ref[kt] >= trial).astype(i32)
            for j in range(tk // LANES):
                acc = acc + c[:, j * LANES:(j + 1) * LANES]
            return acc
        acc = lax.fori_loop(0, n_sel_tiles, body, jnp.zeros((tq, LANES), i32))
        return jnp.sum(acc, axis=-1, keepdims=True)

    c_pos = count_ge(jnp.zeros((tq, 1), i32))
    v0 = jnp.where(c_pos >= topk, 0, INT_MIN).astype(i32)
    c0 = jnp.where(c_pos >= topk, c_pos, n_sel_tiles * tk)

    def bit_step(i, carry):
        v, cge = carry
        trial = v | lax.shift_left(jnp.int32(1), 30 - i)
        c = count_ge(trial)
        ok = c >= topk
        return jnp.where(ok, trial, v), jnp.where(ok, c, cge)

    v_thr, c_ge = lax.fori_loop(0, 31, bit_step, (v0, c0))
    v_eff = jnp.maximum(v_thr, INT_MIN + 1)

    def write_plain(kt, carry):
        dbias_ref[kt] = jnp.where(key_ref[kt] >= v_eff, 0.0, NEG_BIG)
        return carry

    def plain_path():
        lax.fori_loop(0, n_sel_tiles, write_plain, 0)

    def tie_path():
        need = topk - count_ge(v_thr + 1)

        def count_tie_below(limit):
            def body(kt, acc):
                sp = kt * tk + lax.broadcasted_iota(i32, (1, tk), 1)
                c = ((key_ref[kt] == v_thr) & (sp < limit)).astype(i32)
                for j in range(tk // LANES):
                    acc = acc + c[:, j * LANES:(j + 1) * LANES]
                return acc
            acc = lax.fori_loop(0, n_sel_tiles, body, jnp.zeros((tq, LANES), i32))
            return jnp.sum(acc, axis=-1, keepdims=True)

        n_bits = max(1, (t_len - 1).bit_length())

        def idx_step(i, w):
            trial = w | lax.shift_left(jnp.int32(1), n_bits - 1 - i)
            return jnp.where(count_tie_below(trial) < need, trial, w)

        w_last = lax.fori_loop(0, n_bits, idx_step, jnp.zeros((tq, 1), i32))

        def write_tie(kt, carry):
            sp = kt * tk + lax.broadcasted_iota(i32, (1, tk), 1)
            key = key_ref[kt]
            take = (key > v_thr) | ((key == v_thr) & (sp <= w_last))
            dbias_ref[kt] = jnp.where(take & (key >= v_eff), 0.0, NEG_BIG)
            return carry

        lax.fori_loop(0, n_sel_tiles, write_tie, 0)

    lax.cond(jnp.max(c_ge) > topk, tie_path, plain_path)

    def dsa_bias_fn(kt):
        return dbias_ref[kt]

    out_n, out_d = [], []
    for g in range(KV_GROUPS):
        qn4 = jnp.concatenate(head_cols(O_Q, g), axis=0)
        qd4 = jnp.concatenate(head_cols(O_Q + NSA_HEADS * HEAD_DIM, g), axis=0)
        sl4 = slope4(g)
        o_s = _flash_tiles(qn4, kvb_ref, 2 * LANES + g * HEAD_DIM, 3 * LANES + g * HEAD_DIM, tk,
                           0, n_sel_tiles, sl4, sel_bias_fn(g))
        o_w = _flash_tiles(qn4, kvb_ref, 6 * LANES + g * HEAD_DIM, 7 * LANES + g * HEAD_DIM, tkw,
                           win_lo, win_hi, sl4, win_bias_fn)
        o_d = _flash_tiles(qd4, kvb_ref, 4 * LANES + g * HEAD_DIM, 5 * LANES + g * HEAD_DIM, tk,
                           0, n_sel_tiles, sl4, dsa_bias_fn)
        o_n = gate4(g, 0) * o_cmp[g] + gate4(g, 1) * o_s + gate4(g, 2) * o_w
        out_n.append(unstack(o_n))
        out_d.append(unstack(o_d))
    o_ref[0] = jnp.concatenate(out_n + out_d, axis=1).astype(o_ref.dtype)


def _odd_attn_prompt(zq, misc, kvb, wc2, bsz, t_len, tq=128, tk=512, tkw=128):
    tk = min(tk, t_len)
    topk = min(DSA_TOPK_MAX, t_len // 4)
    nb = t_len // BLK
    n_kt = t_len // tk
    zq3 = zq.reshape(bsz, t_len, zq.shape[1])
    misc3 = misc.reshape(bsz, t_len, LANES)
    kvb3 = kvb.reshape(bsz, t_len, kvb.shape[1])
    kern = functools.partial(_odd_attn_kernel, t_len=t_len, tq=tq, tk=tk, tkw=tkw, topk=topk)
    out = pl.pallas_call(
        kern,
        grid=(bsz, t_len // tq),
        in_specs=[pl.BlockSpec((1, tq, zq.shape[1]), lambda b, i: (b, i, 0)),
                  pl.BlockSpec((1, tq, LANES), lambda b, i: (b, i, 0)),
                  pl.BlockSpec((1, t_len, kvb.shape[1]), lambda b, i: (b, 0, 0)),
                  pl.BlockSpec((1, t_len, LANES), lambda b, i: (b, 0, 0)),
                  pl.BlockSpec((2, BLK, LANES), lambda b, i: (0, 0, 0))],
        out_specs=pl.BlockSpec((1, tq, 2 * HALF_W), lambda b, i: (b, i, 0)),
        out_shape=jax.ShapeDtypeStruct((bsz, t_len, 2 * HALF_W), jnp.bfloat16),
        scratch_shapes=[pltpu.VMEM((nb, LANES), jnp.bfloat16),
                        pltpu.VMEM((nb, LANES), jnp.bfloat16),
                        pltpu.VMEM((t_len, IDX_DIM), jnp.bfloat16),
                        pltpu.VMEM((n_kt, tq, tk), jnp.int32),
                        pltpu.VMEM((n_kt, tq, tk), jnp.float32)],
        compiler_params=pltpu.CompilerParams(
            dimension_semantics=("arbitrary", "arbitrary"), vmem_limit_bytes=VMEM_LIMIT),
        name="odd_attn_prompt",
    )(zq3, misc3, kvb3, misc3, wc2)
    return out.reshape(bsz * t_len, 2 * HALF_W)


def _split(z, sizes):
    cuts = [int(c) for c in np.cumsum(sizes)[:-1]]
    return jnp.split(z, cuts, axis=-1)


def _alibi_slopes(n):
    return jnp.asarray(np.power(2.0, -8.0 * np.arange(1, n + 1) / n), dtype=jnp.float32)


def _masked_softmax(s, mask):
    s = jnp.where(mask, s.astype(jnp.float32), -jnp.inf)
    m = jnp.max(s, axis=-1, keepdims=True)
    e = jnp.exp(s - jnp.where(jnp.isfinite(m), m, 0.0))
    return e / jnp.maximum(jnp.sum(e, axis=-1, keepdims=True), 1e-30)


def _short_conv(u_ext, w, b):
    t = u_ext.shape[1] - (CONV_W - 1)
    return b + sum(w[j] * u_ext[:, j:j + t] for j in range(CONV_W))


def _mlstm_chunk(carry, inp):
    c_st, n_st, m_st = carry
    q, k, v, ig, lf = inp
    L = q.shape[2]
    b = jnp.cumsum(lf, axis=-1)
    causal = jnp.tril(jnp.ones((L, L), dtype=bool))
    dmat = jnp.where(causal, b[..., :, None] - b[..., None, :] + ig[..., None, :], -jnp.inf)
    inter = b + m_st[..., None]
    m_t = jnp.maximum(inter, jnp.max(dmat, axis=-1))
    w_intra = jnp.exp(dmat - m_t[..., None])
    w_inter = jnp.exp(inter - m_t)
    s = jnp.einsum('bhtd,bhsd->bhts', q, k) * w_intra
    num = jnp.einsum('bhts,bhsd->bhtd', s, v) + w_inter[..., None] * jnp.einsum('bhed,bhtd->bhte', c_st, q)
    den = jnp.sum(s, axis=-1) + w_inter * jnp.einsum('bhd,bhtd->bht', n_st, q)
    h = num / jnp.maximum(jnp.abs(den), jnp.exp(-m_t))[..., None]
    m_new = m_t[..., -1]
    w_state = jnp.exp(b[..., -1:] - b + ig - m_new[..., None])
    decay = jnp.exp(b[..., -1] + m_st - m_new)
    c_new = decay[..., None, None] * c_st + jnp.einsum('bhs,bhse,bhsd->bhed', w_state, v, k)
    n_new = decay[..., None] * n_st + jnp.einsum('bhs,bhsd->bhd', w_state, k)
    return (c_new, n_new, m_new), h


def _even_mixer(z, gate_b, conv_w, conv_b, conv_prev, ml_state, chunk):
    f32 = jnp.float32
    bn, t, _ = z.shape
    h, bg, cg, q, k, v, og, ig, fg = _split(z, E_SIZES)
    u = cg * h
    u_ext = jnp.concatenate([conv_prev.astype(u.dtype), u], axis=1)
    y_a = bg * _short_conv(u_ext, conv_w, conv_b)
    new_conv = u_ext[:, u_ext.shape[1] - (CONV_W - 1):]
    heads = lambda a: a.reshape(bn, t, ML_HEADS, ML_DIM).astype(f32)
    q, k, v = heads(q), heads(k) * (ML_DIM ** -0.5), heads(v)
    ig = (ig + gate_b[:ML_HEADS]).astype(f32)
    lf = jax.nn.log_sigmoid((fg + gate_b[ML_HEADS:]).astype(f32))
    nc = t // chunk

    def to_chunks(a):
        a = jnp.moveaxis(a.reshape((bn, nc, chunk) + a.shape[2:]), 1, 0)
        return jnp.swapaxes(a, 2, 3)

    carry = tuple(s.astype(f32) for s in ml_state)
    carry, hs = lax.scan(_mlstm_chunk, carry, tuple(to_chunks(a) for a in (q, k, v, ig, lf)))
    hs = jnp.swapaxes(jnp.moveaxis(hs, 0, 1), 2, 3).reshape(bn, t, HALF_W)
    y_b = jax.nn.sigmoid(og) * hs
    return jnp.concatenate([y_a, y_b], axis=-1), new_conv, carry


def _odd_attend(qn, gates, qd, qi, wi, q_pos, kc, vc, slc, dsa_kv, kidx, kw, kw_pos, topn, topk):
    f32 = jnp.float32
    bn, tq, g, r, _ = qn.shape
    nb = kc.shape[1]
    L = kidx.shape[1]
    scale = HEAD_DIM ** -0.5
    m_n = _alibi_slopes(NSA_HEADS).reshape(g, r)
    m_d = _alibi_slopes(DSA_HEADS).reshape(KV_GROUPS, DSA_HEADS // KV_GROUPS)
    jb = jnp.arange(nb)
    dist_c = q_pos[:, None] - (jb * BLK + BLK - 1)[None, :]
    s_c = jnp.einsum('btgrd,bjgd->btgrj', qn, kc).astype(f32) * scale - m_n[:, :, None] * dist_c[:, None, None, :].astype(f32)
    p_c = _masked_softmax(s_c, (dist_c >= 0)[:, None, None, :])
    o_c = jnp.einsum('btgrj,bjgd->btgrd', p_c.astype(vc.dtype), vc)
    cur = q_pos // BLK
    imp = jnp.sum(p_c, axis=3)
    forced = (jb[None, :] == 0) | (jb[None, :] == cur[:, None]) | (jb[None, :] == cur[:, None] - 1)
    imp = jnp.where(forced[:, None, :], FORCE_SCORE, imp)
    imp = jnp.where((jb[None, :] <= cur[:, None])[:, None, :], imp, -jnp.inf)
    top_val, top_idx = lax.top_k(imp, topn)
    b_ix = jnp.arange(bn)[:, None, None, None]
    g_ix = jnp.arange(g)[None, None, :, None]
    sel = slc[b_ix, g_ix, top_idx]
    pos = top_idx[..., None] * BLK + jnp.arange(BLK)
    dist_s = q_pos[None, :, None, None, None] - pos
    mask_s = (dist_s >= 0) & jnp.isfinite(top_val)[..., None]
    s_s = jnp.einsum('btgrd,btgnpd->btgrnp', qn, sel[..., 0, :]).astype(f32) * scale - m_n[None, None, :, :, None, None] * dist_s[:, :, :, None].astype(f32)
    shp = s_s.shape
    p_s = _masked_softmax(s_s.reshape(shp[:4] + (-1,)), mask_s[:, :, :, None].reshape(bn, tq, g, 1, -1)).reshape(shp)
    o_s = jnp.einsum('btgrnp,btgnpd->btgrd', p_s.astype(sel.dtype), sel[..., 1, :])
    dist_w = q_pos[:, None] - kw_pos[None, :]
    mask_w = (dist_w >= 0) & (dist_w < WINDOW) & (kw_pos >= 0)[None, :]
    s_w = jnp.einsum('btgrd,blgd->btgrl', qn, kw[:, :, 0]).astype(f32) * scale - m_n[:, :, None] * dist_w[:, None, None, :].astype(f32)
    p_w = _masked_softmax(s_w, mask_w[:, None, None, :])
    o_w = jnp.einsum('btgrl,blgd->btgrd', p_w.astype(kw.dtype), kw[:, :, 1])
    o_n = (gates[..., 0:1] * o_c + gates[..., 1:2] * o_s + gates[..., 2:3] * o_w).astype(qn.dtype).reshape(bn, tq, -1)
    k_pos = jnp.arange(L)
    isc = jnp.einsum('bthe,ble->bthl', qi, kidx).astype(f32) * (IDX_DIM ** -0.5)
    isc = jnp.einsum('bthl,bth->btl', jax.nn.relu(isc), wi.astype(f32)) * (IDX_HEADS ** -0.5)
    isc = jnp.where((k_pos[None, :] <= q_pos[:, None])[None], isc, -jnp.inf)
    i_val, i_idx = lax.top_k(isc, topk)
    sel_d = dsa_kv[jnp.arange(bn)[:, None, None], i_idx]
    dist_d = (q_pos[None, :, None] - i_idx).astype(f32)
    s_d = jnp.einsum('btgrd,btkgd->btgrk', qd, sel_d[:, :, :, 0]).astype(f32) * scale - m_d[None, None, :, :, None] * dist_d[:, :, None, None, :]
    p_d = _masked_softmax(s_d, jnp.isfinite(i_val)[:, :, None, None, :])
    o_d = jnp.einsum('btgrk,btkgd->btgrd', p_d.astype(sel_d.dtype), sel_d[:, :, :, 1]).reshape(bn, tq, -1)
    return jnp.concatenate([o_n, o_d.astype(qn.dtype)], axis=-1)


def _odd_mixer(z, w_cmp, past_kv, past_idx, past_win, keep):
    bn, t, _ = z.shape
    p_len = past_kv.shape[1]
    wp = past_win.shape[1]
    g, r = KV_GROUPS, NSA_HEADS // KV_GROUPS
    qn, kvc, kvs, kvw, gates, qd, kvd, qi, ki, wi = _split(z, O_SIZES)
    qn = qn.reshape(bn, t, g, r, HEAD_DIM)
    qd = qd.reshape(bn, t, g, DSA_HEADS // g, HEAD_DIM)
    gates = jax.nn.sigmoid(gates.astype(jnp.float32)).reshape(bn, t, g, r, 3)
    qi = qi.reshape(bn, t, IDX_HEADS, IDX_DIM)
    kv_new = jnp.concatenate([kvc, kvs, kvd], axis=-1).reshape(bn, t, KV_SLOTS, g, HEAD_DIM)
    kvw_new = kvw.reshape(bn, t, 2, g, HEAD_DIM)
    kv = jnp.concatenate([past_kv.astype(kv_new.dtype), kv_new], axis=1)
    kidx = jnp.concatenate([past_idx.astype(ki.dtype), ki], axis=1)
    win_all = jnp.concatenate([past_win.astype(kvw_new.dtype), kvw_new], axis=1)
    L = p_len + t
    nb = -(-L // BLK)
    kvb = jnp.pad(kv[:, :, :4], ((0, 0), (0, nb * BLK - L), (0, 0), (0, 0), (0, 0)))
    kvb = kvb.reshape(bn, nb, BLK, 4, g, HEAD_DIM)
    kc = jnp.einsum('bjpgd,pd->bjgd', kvb[:, :, :, 0], w_cmp[0])
    vc = jnp.einsum('bjpgd,pd->bjgd', kvb[:, :, :, 1], w_cmp[1])
    slc = jnp.transpose(kvb[:, :, :, 2:4], (0, 4, 1, 2, 3, 5))
    dsa_kv = kv[:, :, 4:6]
    topn = min(NSA_TOPN, nb)
    topk = min(DSA_TOPK_MAX, L // 4)
    qb = t if t <= Q_BLOCK else Q_BLOCK
    nqb = t // qb

    def blocks(a):
        return jnp.moveaxis(a.reshape((bn, nqb, qb) + a.shape[2:]), 1, 0)

    def body(args):
        qn_b, g_b, qd_b, qi_b, wi_b, start = args
        q_pos = p_len + start + jnp.arange(qb)
        kw = lax.dynamic_slice_in_dim(win_all, start, wp + qb, axis=1)
        kw_pos = p_len - wp + start + jnp.arange(wp + qb)
        return _odd_attend(qn_b, g_b, qd_b, qi_b, wi_b, q_pos, kc, vc, slc, dsa_kv, kidx, kw, kw_pos, topn, topk)

    starts = jnp.arange(nqb, dtype=jnp.int32) * qb
    o = lax.map(body, (blocks(qn), blocks(gates), blocks(qd), blocks(qi), blocks(wi), starts))
    o = jnp.moveaxis(o, 0, 1).reshape(bn, t, 2 * HALF_W)
    new_win = win_all[:, win_all.shape[1] - keep:]
    return o, kv_new, ki, new_win


def _pad_cols(w, n):
    return jnp.pad(w, ((0, 0), (0, n - w.shape[1])))


def kernel(x_prompt, x_sample, state_conv, state_C, state_n, state_m, cache_kv, cache_idx, cache_win, page_table,
           w_in_e, gate_b_e, conv_w, conv_b, w_out_e, w_in_o, w_cmp, w_out_o, w_up, w_down, ln_g, ln_b):
    f32, bf16 = jnp.float32, jnp.bfloat16
    bp, tp, d = x_prompt.shape
    dbs = x_sample.shape[0]
    past = page_table.shape[1] * cache_kv.shape[2]
    keep = cache_win.shape[2]
    yp = x_prompt.reshape(bp * tp, d)
    ys = x_sample.reshape(dbs, d)
    tm_p, tm_s = 512, dbs
    outs = {}
    for layer in range(DEPTH):
        i = layer // 2
        if layer % 2 == 0:
            n_pad = _round_up(E_COLS, LANES)
            w_in = _pad_cols(w_in_e[i], n_pad).astype(bf16)
            zp = _proj(yp, w_in, tm_p)[:, :E_COLS].reshape(bp, tp, E_COLS)
            zs = _proj(ys, w_in, tm_s)[:, :E_COLS].reshape(dbs, 1, E_COLS)
            zero_state = (jnp.zeros((bp, ML_HEADS, ML_DIM, ML_DIM), f32), jnp.zeros((bp, ML_HEADS, ML_DIM), f32),
                          jnp.zeros((bp, ML_HEADS), f32))
            mp, cvp, stp = _even_mixer(zp, gate_b_e[i], conv_w[i], conv_b[i],
                                       jnp.zeros((bp, CONV_W - 1, HALF_W), f32), zero_state, ML_CHUNK)
            ms, cvs, sts = _even_mixer(zs, gate_b_e[i], conv_w[i], conv_b[i], state_conv[i],
                                       (state_C[i], state_n[i], state_m[i]), 1)
            outs['conv'] = (cvp[None], cvs[None])
            outs['c'] = (stp[0][None], sts[0][None])
            outs['n'] = (stp[1][None], sts[1][None])
            outs['m'] = (stp[2][None], sts[2][None])
            w_out = w_out_e[i].astype(bf16)
        else:
            n_pad = _round_up(O_COLS, LANES)
            w_in = _pad_cols(w_in_o[i], n_pad).astype(bf16)
            zs = _proj(ys, w_in, tm_s)[:, :O_COLS].reshape(dbs, 1, O_COLS)
            past_kv = cache_kv[i][page_table].reshape(dbs, past, KV_SLOTS, KV_GROUPS, HEAD_DIM)
            past_idx = cache_idx[i][page_table].reshape(dbs, past, IDX_DIM)
            zq, kvp, kvb, winp, miscp = _odd_proj(yp, _pack_w_in_o(w_in_o[i]).astype(bf16), tm_p)
            wc2 = jnp.concatenate([w_cmp[i], w_cmp[i]], axis=-1)
            mp = _odd_attn_prompt(zq, miscp, kvb, wc2, bp, tp)
            kvp = kvp.reshape(bp, tp, KV_SLOTS, KV_GROUPS, HEAD_DIM)
            ixp = miscp[:, :IDX_DIM].reshape(bp, tp, IDX_DIM)
            wnp = winp.reshape(bp, tp, 2, KV_GROUPS, HEAD_DIM)[:, tp - keep:]
            ms, kvs, ixs, wns = _odd_mixer(zs, w_cmp[i], past_kv, past_idx, cache_win[i], keep)
            outs['kv'] = (kvp[None], kvs[None])
            outs['idx'] = (ixp[None], ixs[None])
            outs['win'] = (wnp[None], wns[None])
            w_out = w_out_o[i].astype(bf16)
        yp = _outproj_ln(mp.reshape(bp * tp, d), w_out, yp, ln_g[layer, 0], ln_b[layer, 0], tm_p)
        ys = _outproj_ln(ms.reshape(dbs, d), w_out, ys, ln_g[layer, 0], ln_b[layer, 0], tm_s)
        wu, wd = w_up[layer].astype(bf16), w_down[layer].astype(bf16)
        yp = _mlp_ln(yp, wu, wd, ln_g[layer, 1], ln_b[layer, 1], tm_p, 1024)
        ys = _mlp_ln(ys, wu, wd, ln_g[layer, 1], ln_b[layer, 1], tm_s, 1024)
    return (yp.reshape(bp, tp, d), ys.reshape(dbs, 1, d),
            outs['conv'][0], outs['conv'][1], outs['c'][0], outs['c'][1],
            outs['n'][0], outs['n'][1], outs['m'][0], outs['m'][1],
            outs['kv'][0], outs['kv'][1], outs['idx'][0], outs['idx'][1],
            outs['win'][0], outs['win'][1])
```

```python
import functools

import jax
import jax.numpy as jnp
import numpy as np
from jax import lax
from jax.experimental import pallas as pl
from jax.experimental.pallas import tpu as pltpu

D_MODEL = 1024
DEPTH = 2
HALF_W = 512
D_FF = 4096
CONV_W = 3
ML_HEADS = 4
ML_DIM = 128
ML_CHUNK = 64
HEAD_DIM = 64
KV_GROUPS = 2
NSA_HEADS = 8
DSA_HEADS = 8
BLK = 64
NSA_TOPN = 8
WINDOW = 256
IDX_HEADS = 4
IDX_DIM = 32
DSA_TOPK_MAX = 256
Q_BLOCK = 128
KV_SLOTS = 6
FORCE_SCORE = 1e4
ALPHA = (2.0 * DEPTH) ** 0.25
LN_EPS = 1e-5
E_SIZES = (512, 512, 512, 512, 512, 512, 512, 4, 4)
O_SIZES = (512, 256, 256, 256, 24, 512, 256, 128, 32, 4)
E_COLS = sum(E_SIZES)
O_COLS = sum(O_SIZES)
LANES = 128
VMEM_LIMIT = 48 * 1024 * 1024


def _round_up(n, m):
    return -(-n // m) * m


def _proj_kernel(x_ref, w_ref, o_ref):
    o_ref[...] = jnp.dot(x_ref[...].astype(jnp.bfloat16), w_ref[...],
                         preferred_element_type=jnp.float32)


def _proj(x, w_bf16, tm):
    m, k = x.shape
    n = w_bf16.shape[1]
    tn = n
    for cand in (1024, 768, 512, 256, 128):
        if n % cand == 0:
            tn = cand
            break
    return pl.pallas_call(
        _proj_kernel,
        grid=(n // tn, m // tm),
        in_specs=[pl.BlockSpec((tm, k), lambda j, i: (i, 0)),
                  pl.BlockSpec((k, tn), lambda j, i: (0, j))],
        out_specs=pl.BlockSpec((tm, tn), lambda j, i: (i, j)),
        out_shape=jax.ShapeDtypeStruct((m, n), jnp.float32),
        compiler_params=pltpu.CompilerParams(
            dimension_semantics=("arbitrary", "arbitrary"), vmem_limit_bytes=VMEM_LIMIT),
        name="proj",
    )(x, w_bf16)


def _layer_norm_rows(v, g, b):
    mu = jnp.mean(v, axis=-1, keepdims=True)
    d = v - mu
    var = jnp.mean(d * d, axis=-1, keepdims=True)
    return d * lax.rsqrt(var + LN_EPS) * g + b


def _outproj_ln_kernel(y_ref, w_ref, x_ref, g_ref, b_ref, o_ref):
    mix = jnp.dot(y_ref[...].astype(jnp.bfloat16), w_ref[...], preferred_element_type=jnp.float32)
    o_ref[...] = _layer_norm_rows(ALPHA * x_ref[...] + mix, g_ref[...], b_ref[...])


def _outproj_ln(y, w_bf16, x, g, b, tm):
    m, d = x.shape
    k = y.shape[1]
    return pl.pallas_call(
        _outproj_ln_kernel,
        grid=(m // tm,),
        in_specs=[pl.BlockSpec((tm, k), lambda i: (i, 0)),
                  pl.BlockSpec((k, d), lambda i: (0, 0)),
                  pl.BlockSpec((tm, d), lambda i: (i, 0)),
                  pl.BlockSpec((1, d), lambda i: (0, 0)),
                  pl.BlockSpec((1, d), lambda i: (0, 0))],
        out_specs=pl.BlockSpec((tm, d), lambda i: (i, 0)),
        out_shape=jax.ShapeDtypeStruct((m, d), jnp.float32),
        compiler_params=pltpu.CompilerParams(
            dimension_semantics=("arbitrary",), vmem_limit_bytes=VMEM_LIMIT),
        name="outproj_ln",
    )(y, w_bf16, x, g.reshape(1, d), b.reshape(1, d))


def _mlp_ln_kernel(x_ref, wu_ref, wd_ref, g_ref, b_ref, o_ref, acc_ref):
    f = pl.program_id(1)

    @pl.when(f == 0)
    def _():
        acc_ref[...] = jnp.zeros_like(acc_ref)

    up = jnp.dot(x_ref[...].astype(jnp.bfloat16), wu_ref[...], preferred_element_type=jnp.float32)
    act = jnp.square(jnp.maximum(up, 0.0))
    acc_ref[...] += jnp.dot(act.astype(jnp.bfloat16), wd_ref[...], preferred_element_type=jnp.float32)

    @pl.when(f == pl.num_programs(1) - 1)
    def _():
        o_ref[...] = _layer_norm_rows(ALPHA * x_ref[...] + acc_ref[...], g_ref[...], b_ref[...])


def _mlp_ln(x, wu_bf16, wd_bf16, g, b, tm, tf):
    m, d = x.shape
    ff = wu_bf16.shape[1]
    return pl.pallas_call(
        _mlp_ln_kernel,
        grid=(m // tm, ff // tf),
        in_specs=[pl.BlockSpec((tm, d), lambda i, f: (i, 0)),
                  pl.BlockSpec((d, tf), lambda i, f: (0, f)),
                  pl.BlockSpec((tf, d), lambda i, f: (f, 0)),
                  pl.BlockSpec((1, d), lambda i, f: (0, 0)),
                  pl.BlockSpec((1, d), lambda i, f: (0, 0))],
        out_specs=pl.BlockSpec((tm, d), lambda i, f: (i, 0)),
        out_shape=jax.ShapeDtypeStruct((m, d), jnp.float32),
        scratch_shapes=[pltpu.VMEM((tm, d), jnp.float32)],
        compiler_params=pltpu.CompilerParams(
            dimension_semantics=("arbitrary", "arbitrary"), vmem_limit_bytes=VMEM_LIMIT),
        name="mlp_ln",
    )(x, wu_bf16, wd_bf16, g.reshape(1, d), b.reshape(1, d))


E_QKV = 3 * HALF_W
E_OG = 6 * HALF_W
E_GATE = 7 * HALF_W
E_PACKED = 7 * HALF_W + LANES
CONV_TAIL = 8


def _even_proj_kernel(x_ref, w_ref, cw_ref, cb_ref, ya_ref, qkv_ref, og_ref, gate_ref, tail_ref, carry_ref,
                      *, tiles_per_seq):
    f32, bf16 = jnp.float32, jnp.bfloat16
    tm = x_ref.shape[0]

    @pl.when(pl.program_id(0) % tiles_per_seq == 0)
    def _():
        carry_ref[...] = jnp.zeros_like(carry_ref)

    z = jnp.dot(x_ref[...].astype(bf16), w_ref[...], preferred_element_type=f32)
    u = z[:, 2 * HALF_W:3 * HALF_W] * z[:, 0:HALF_W]
    prev = carry_ref[...]
    row = lax.broadcasted_iota(jnp.int32, (tm, 1), 0)
    conv = cb_ref[...] + cw_ref[CONV_W - 1:CONV_W, :] * u
    for back in range(1, CONV_W):
        shifted = pltpu.roll(u, back, 0)
        for r in range(back):
            shifted = jnp.where(row == r, prev[CONV_TAIL - back + r:CONV_TAIL - back + r + 1, :], shifted)
        conv = conv + cw_ref[CONV_W - 1 - back:CONV_W - back, :] * shifted
    ya_ref[...] = (z[:, HALF_W:2 * HALF_W] * conv).astype(bf16)
    tail = u[tm - CONV_TAIL:]
    carry_ref[...] = tail
    tail_ref[0] = tail
    qkv_ref[:, 0:HALF_W] = z[:, E_QKV:E_QKV + HALF_W].astype(bf16)
    qkv_ref[:, HALF_W:2 * HALF_W] = (z[:, E_QKV + HALF_W:E_QKV + 2 * HALF_W] * (ML_DIM ** -0.5)).astype(bf16)
    qkv_ref[:, 2 * HALF_W:3 * HALF_W] = z[:, E_QKV + 2 * HALF_W:E_OG].astype(bf16)
    og_ref[...] = z[:, E_OG:E_GATE]
    gate_ref[...] = z[:, E_GATE:E_PACKED]


def _even_proj(x, w_bf16, conv_w, conv_b, tm, seq_len):
    m, k = x.shape
    widths = (HALF_W, 3 * HALF_W, HALF_W, LANES)
    dtypes = (jnp.bfloat16, jnp.bfloat16, jnp.float32, jnp.float32)
    kern = functools.partial(_even_proj_kernel, tiles_per_seq=seq_len // tm)
    return pl.pallas_call(
        kern,
        grid=(m // tm,),
        in_specs=[pl.BlockSpec((tm, k), lambda i: (i, 0)),
                  pl.BlockSpec((k, E_PACKED), lambda i: (0, 0)),
                  pl.BlockSpec((CONV_W, HALF_W), lambda i: (0, 0)),
                  pl.BlockSpec((1, HALF_W), lambda i: (0, 0))],
        out_specs=[pl.BlockSpec((tm, n), lambda i: (i, 0)) for n in widths]
        + [pl.BlockSpec((1, CONV_TAIL, HALF_W), lambda i: (i, 0, 0))],
        out_shape=[jax.ShapeDtypeStruct((m, n), dt) for n, dt in zip(widths, dtypes)]
        + [jax.ShapeDtypeStruct((m // tm, CONV_TAIL, HALF_W), jnp.float32)],
        scratch_shapes=[pltpu.VMEM((CONV_TAIL, HALF_W), jnp.float32)],
        compiler_params=pltpu.CompilerParams(
            dimension_semantics=("arbitrary",), vmem_limit_bytes=VMEM_LIMIT),
        name="even_proj",
    )(x, w_bf16, conv_w, conv_b.reshape(1, HALF_W))


def _mlstm_kernel(qkv_ref, og_ref, gate_ref, gb_ref, yb_ref, c_ref, n_ref, m_ref):
    f32, bf16 = jnp.float32, jnp.bfloat16
    L = qkv_ref.shape[1]

    @pl.when(pl.program_id(1) == 0)
    def _():
        c_ref[...] = jnp.zeros_like(c_ref)
        n_ref[...] = jnp.zeros_like(n_ref)
        m_ref[...] = jnp.zeros_like(m_ref)

    pre = gate_ref[0] + gb_ref[...]
    lf = jax.nn.log_sigmoid(pre)
    row = lax.broadcasted_iota(jnp.int32, (L, L), 0)
    col = lax.broadcasted_iota(jnp.int32, (L, L), 1)
    causal = col <= row
    tri = jnp.where(causal, 1.0, 0.0).astype(f32)
    b_all = jnp.dot(tri, lf, preferred_element_type=f32, precision=lax.Precision.HIGHEST)
    pre_t = pre.T
    b_t = b_all.T
    for h in range(ML_HEADS):
        q = qkv_ref[0, :, h * ML_DIM:(h + 1) * ML_DIM]
        k = qkv_ref[0, :, HALF_W + h * ML_DIM:HALF_W + (h + 1) * ML_DIM]
        v = qkv_ref[0, :, 2 * HALF_W + h * ML_DIM:2 * HALF_W + (h + 1) * ML_DIM]
        ig_col = pre[:, h:h + 1]
        b_col = b_all[:, ML_HEADS + h:ML_HEADS + h + 1]
        a_row = pre_t[h:h + 1, :] - b_t[ML_HEADS + h:ML_HEADS + h + 1, :]
        m_st = m_ref[0, h:h + 1, 0:1]
        c_st = c_ref[0, h]
        n_st = n_ref[0, h:h + 1, :]
        dmat = jnp.where(causal, b_col + a_row, -jnp.inf)
        inter = b_col + m_st
        m_t = jnp.maximum(inter, jnp.max(dmat, axis=-1, keepdims=True))
        w_intra = jnp.exp(dmat - m_t)
        w_inter = jnp.exp(inter - m_t)
        s = _dot_nt(q, k) * w_intra
        num = jnp.dot(s.astype(bf16), v, preferred_element_type=f32) + w_inter * _dot_nt(q, c_st.astype(bf16))
        den = jnp.sum(s, axis=-1, keepdims=True) + w_inter * jnp.sum(q.astype(f32) * n_st, axis=-1, keepdims=True)
        hs = num * (1.0 / jnp.maximum(jnp.abs(den), jnp.exp(-m_t)))
        m_new = m_t[L - 1:L, :]
        b_last = b_col[L - 1:L, :]
        w_state = jnp.exp(b_last - b_col + ig_col - m_new)
        decay = jnp.exp(b_last + m_st - m_new)
        vw = (v.astype(f32) * w_state).astype(bf16)
        c_ref[0, h] = decay * c_st + lax.dot_general(vw, k, (((0,), (0,)), ((), ())), preferred_element_type=f32)
        n_ref[0, h:h + 1, :] = decay * n_st + jnp.sum(k.astype(f32) * w_state, axis=0, keepdims=True)
        m_ref[0, h:h + 1, :] = jnp.broadcast_to(m_new, (1, LANES))
        og = og_ref[0, :, h * ML_DIM:(h + 1) * ML_DIM]
        yb_ref[0, :, h * ML_DIM:(h + 1) * ML_DIM] = (jax.nn.sigmoid(og) * hs).astype(bf16)


def _mlstm_prompt(qkv, og, gates, gate_b, bsz, t_len, chunk=128):
    gb = jnp.pad(gate_b, (0, LANES - gate_b.shape[0])).reshape(1, LANES)
    return pl.pallas_call(
        _mlstm_kernel,
        grid=(bsz, t_len // chunk),
        in_specs=[pl.BlockSpec((1, chunk, 3 * HALF_W), lambda b, c: (b, c, 0)),
                  pl.BlockSpec((1, chunk, HALF_W), lambda b, c: (b, c, 0)),
                  pl.BlockSpec((1, chunk, LANES), lambda b, c: (b, c, 0)),
                  pl.BlockSpec((1, LANES), lambda b, c: (0, 0))],
        out_specs=[pl.BlockSpec((1, chunk, HALF_W), lambda b, c: (b, c, 0)),
                   pl.BlockSpec((1, ML_HEADS, ML_DIM, ML_DIM), lambda b, c: (b, 0, 0, 0)),
                   pl.BlockSpec((1, ML_HEADS, ML_DIM), lambda b, c: (b, 0, 0)),
                   pl.BlockSpec((1, ML_HEADS, LANES), lambda b, c: (b, 0, 0))],
        out_shape=[jax.ShapeDtypeStruct((bsz, t_len, HALF_W), jnp.bfloat16),
                   jax.ShapeDtypeStruct((bsz, ML_HEADS, ML_DIM, ML_DIM), jnp.float32),
                   jax.ShapeDtypeStruct((bsz, ML_HEADS, ML_DIM), jnp.float32),
                   jax.ShapeDtypeStruct((bsz, ML_HEADS, LANES), jnp.float32)],
        compiler_params=pltpu.CompilerParams(
            dimension_semantics=("arbitrary", "arbitrary"), vmem_limit_bytes=VMEM_LIMIT),
        name="mlstm_prompt",
    )(qkv.reshape(bsz, t_len, 3 * HALF_W), og.reshape(bsz, t_len, HALF_W),
      gates.reshape(bsz, t_len, LANES), gb)


def _outproj2_ln_kernel(ya_ref, yb_ref, w_ref, x_ref, g_ref, b_ref, o_ref):
    mix = jnp.dot(ya_ref[...], w_ref[0:HALF_W, :], preferred_element_type=jnp.float32)
    mix = mix + jnp.dot(yb_ref[...], w_ref[HALF_W:, :], preferred_element_type=jnp.float32)
    o_ref[...] = _layer_norm_rows(ALPHA * x_ref[...] + mix, g_ref[...], b_ref[...])


def _outproj2_ln(ya, yb, w_bf16, x, g, b, tm):
    m, d = x.shape
    return pl.pallas_call(
        _outproj2_ln_kernel,
        grid=(m // tm,),
        in_specs=[pl.BlockSpec((tm, HALF_W), lambda i: (i, 0)),
                  pl.BlockSpec((tm, HALF_W), lambda i: (i, 0)),
                  pl.BlockSpec((2 * HALF_W, d), lambda i: (0, 0)),
                  pl.BlockSpec((tm, d), lambda i: (i, 0)),
                  pl.BlockSpec((1, d), lambda i: (0, 0)),
                  pl.BlockSpec((1, d), lambda i: (0, 0))],
        out_specs=pl.BlockSpec((tm, d), lambda i: (i, 0)),
        out_shape=jax.ShapeDtypeStruct((m, d), jnp.float32),
        compiler_params=pltpu.CompilerParams(
            dimension_semantics=("arbitrary",), vmem_limit_bytes=VMEM_LIMIT),
        name="outproj2_ln",
    )(ya, yb, w_bf16, x, g.reshape(1, d), b.reshape(1, d))


O_Q = 0
O_KV = 1024
O_WIN = 1792
O_QI = 2048
O_MISC = 2176
O_PACKED = 2304
MISC_WI = IDX_DIM
MISC_GATE = IDX_DIM + IDX_HEADS
NEG_BIG = -(2.0 ** 100)
INT_MIN = -(2 ** 31)
ALIBI = tuple(float(2.0 ** (-8.0 * (h + 1) / NSA_HEADS)) for h in range(NSA_HEADS))


def _pack_w_in_o(w):
    qn, kvc, kvs, kvw, gates, qd, kvd, qi, ki, wi = _split(w, O_SIZES)
    scale = HEAD_DIM ** -0.5
    pad = jnp.zeros((w.shape[0], O_PACKED - O_MISC - IDX_DIM - IDX_HEADS - 3 * NSA_HEADS), w.dtype)
    return jnp.concatenate([qn * scale, qd * scale, kvc, kvs, kvd, kvw, qi, ki, wi, gates, pad], axis=1)


def _odd_proj_kernel(x_ref, w_ref, zq_ref, kv_ref, kvb_ref, win_ref, misc_ref):
    z = jnp.dot(x_ref[...].astype(jnp.bfloat16), w_ref[...], preferred_element_type=jnp.float32)
    zq_ref[:, 0:O_KV] = z[:, 0:O_KV].astype(jnp.bfloat16)
    zq_ref[:, O_KV:O_KV + LANES] = z[:, O_QI:O_MISC].astype(jnp.bfloat16)
    kv_ref[...] = z[:, O_KV:O_WIN]
    kvb_ref[...] = z[:, O_KV:O_QI].astype(jnp.bfloat16)
    win_ref[...] = z[:, O_WIN:O_QI]
    misc_ref[...] = z[:, O_MISC:O_PACKED]


def _odd_proj(x, w_packed_bf16, tm):
    m, k = x.shape
    widths = (O_KV + LANES, O_WIN - O_KV, O_QI - O_KV, O_QI - O_WIN, LANES)
    dtypes = (jnp.bfloat16, jnp.float32, jnp.bfloat16, jnp.float32, jnp.float32)
    return pl.pallas_call(
        _odd_proj_kernel,
        grid=(m // tm,),
        in_specs=[pl.BlockSpec((tm, k), lambda i: (i, 0)),
                  pl.BlockSpec((k, O_PACKED), lambda i: (0, 0))],
        out_specs=[pl.BlockSpec((tm, n), lambda i: (i, 0)) for n in widths],
        out_shape=[jax.ShapeDtypeStruct((m, n), dt) for n, dt in zip(widths, dtypes)],
        compiler_params=pltpu.CompilerParams(
            dimension_semantics=("arbitrary",), vmem_limit_bytes=VMEM_LIMIT),
        name="odd_proj",
    )(x, w_packed_bf16)


def _dot_nt(a, b):
    return lax.dot_general(a, b, (((1,), (1,)), ((), ())), preferred_element_type=jnp.float32)


def _flash_tiles(q4, kv_ref, k_col, v_col, tk, lo, hi, slope_col4, bias_fn):
    rows = q4.shape[0]

    def body(kt, carry):
        m, l, acc = carry
        start = pl.multiple_of(kt * tk, tk)
        k = kv_ref[0, pl.ds(start, tk), k_col:k_col + HEAD_DIM]
        v = kv_ref[0, pl.ds(start, tk), v_col:v_col + HEAD_DIM]
        s = _dot_nt(q4, k)
        sp = (start + lax.broadcasted_iota(jnp.int32, (1, tk), 1)).astype(jnp.float32)
        b = bias_fn(kt)
        s = s + slope_col4 * sp + jnp.concatenate([b, b, b, b], axis=0)
        m_new = jnp.maximum(m, jnp.max(s, axis=-1, keepdims=True))
        alpha = jnp.exp(m - m_new)
        p = jnp.exp(s - m_new)
        l = alpha * l + jnp.sum(p, axis=-1, keepdims=True)
        acc = alpha * acc + jnp.dot(p.astype(jnp.bfloat16), v, preferred_element_type=jnp.float32)
        return m_new, l, acc

    m0 = jnp.full((rows, 1), -jnp.inf, jnp.float32)
    l0 = jnp.zeros((rows, 1), jnp.float32)
    a0 = jnp.zeros((rows, HEAD_DIM), jnp.float32)
    _, l, acc = lax.fori_loop(lo, hi, body, (m0, l0, a0))
    return acc * (1.0 / l)


def _odd_attn_kernel(zq_ref, miscq_ref, kvb_ref, misck_ref, wc_ref, o_ref,
                     kc_ref, vc_ref, ki_ref, key_ref, dbias_ref, *, t_len, tq, tk, tkw, topk):
    f32, bf16, i32 = jnp.float32, jnp.bfloat16, jnp.int32
    nb = t_len // BLK
    qi_blk = pl.program_id(1)
    q0 = qi_blk * tq

    @pl.when(qi_blk == 0)
    def _():
        ck = kvb_ref[0, :, 0:LANES].astype(f32).reshape(nb, BLK, LANES)
        kc_ref[...] = jnp.sum(ck * wc_ref[0][None], axis=1).astype(bf16)
        cv = kvb_ref[0, :, LANES:2 * LANES].astype(f32).reshape(nb, BLK, LANES)
        vc_ref[...] = jnp.sum(cv * wc_ref[1][None], axis=1).astype(bf16)
        ki_ref[...] = misck_ref[0, :, 0:IDX_DIM].astype(bf16)

    zq = zq_ref[0]
    misc = miscq_ref[0]
    gates = jax.nn.sigmoid(misc[:, MISC_GATE:MISC_GATE + 3 * NSA_HEADS])
    t_col = q0 + lax.broadcasted_iota(i32, (tq, 1), 0)

    def head_cols(base, g):
        return [zq[:, base + (g * 4 + r) * HEAD_DIM: base + (g * 4 + r + 1) * HEAD_DIM] for r in range(4)]

    def slope4(g):
        return jnp.concatenate([jnp.full((tq, 1), ALIBI[g * 4 + r], f32) for r in range(4)], axis=0)

    def gate4(g, c):
        return jnp.concatenate([gates[:, (g * 4 + r) * 3 + c:(g * 4 + r) * 3 + c + 1] for r in range(4)], axis=0)

    def unstack(x4):
        return jnp.concatenate([x4[r * tq:(r + 1) * tq] for r in range(4)], axis=1)

    jb = lax.broadcasted_iota(i32, (1, nb), 1)
    dist_c = t_col - (jb * BLK + (BLK - 1))
    mask_c = dist_c >= 0
    dist_cf = dist_c.astype(f32)
    cur = t_col // BLK
    forced = (jb == 0) | (jb == cur) | (jb == cur - 1)
    admissible = jb <= cur
    jb_full = lax.broadcasted_iota(i32, (tq, nb), 1)
    o_cmp, sel_bias = [], []
    for g in range(KV_GROUPS):
        kc_g = kc_ref[:, g * HEAD_DIM:(g + 1) * HEAD_DIM]
        vc_g = vc_ref[:, g * HEAD_DIM:(g + 1) * HEAD_DIM]
        imp = jnp.zeros((tq, nb), f32)
        heads = []
        for r, qh in enumerate(head_cols(O_Q, g)):
            s = _dot_nt(qh, kc_g) - ALIBI[g * 4 + r] * dist_cf
            s = jnp.where(mask_c, s, NEG_BIG)
            m = jnp.max(s, axis=-1, keepdims=True)
            e = jnp.where(mask_c, jnp.exp(s - m), 0.0)
            p = e * (1.0 / jnp.maximum(jnp.sum(e, axis=-1, keepdims=True), 1e-30))
            imp = imp + p
            heads.append(jnp.dot(p.astype(bf16), vc_g, preferred_element_type=f32))
        o_cmp.append(jnp.concatenate(heads, axis=0))
        imp = jnp.where(forced, FORCE_SCORE, imp)
        imp = jnp.where(admissible, imp, -jnp.inf)
        sel = jnp.zeros((tq, nb), f32)
        for _ in range(min(NSA_TOPN, nb)):
            m = jnp.max(imp, axis=-1, keepdims=True)
            first = jnp.min(jnp.where(imp == m, jb_full, nb), axis=-1, keepdims=True)
            hit = jb_full == first
            sel = jnp.where(hit & (m > -jnp.inf), 1.0, sel)
            imp = jnp.where(hit, -jnp.inf, imp)
        sel_bias.append(jnp.where(sel > 0.0, 0.0, NEG_BIG).astype(bf16))

    n_sel_tiles = (q0 + tq + tk - 1) // tk
    blocks_per_tile = tk // BLK

    def causal_bias(start, width):
        sp = start + lax.broadcasted_iota(i32, (1, width), 1)
        return jnp.where(sp <= t_col, 0.0, NEG_BIG)

    def sel_bias_fn(g):
        def fn(kt):
            row = lax.broadcasted_iota(i32, (nb, tk), 0)
            col = lax.broadcasted_iota(i32, (nb, tk), 1)
            expand = jnp.where(row == kt * blocks_per_tile + col // BLK, 1.0, 0.0).astype(bf16)
            return jnp.dot(sel_bias[g], expand, preferred_element_type=f32) + causal_bias(kt * tk, tk)
        return fn

    def win_bias_fn(kt):
        sp = kt * tkw + lax.broadcasted_iota(i32, (1, tkw), 1)
        dist = t_col - sp
        return jnp.where((dist >= 0) & (dist < WINDOW), 0.0, NEG_BIG)

    win_lo = jnp.maximum(q0 - WINDOW, 0) // tkw
    win_hi = (q0 + tq + tkw - 1) // tkw

    qi_heads = [zq[:, O_KV + h * IDX_DIM: O_KV + (h + 1) * IDX_DIM] for h in range(IDX_HEADS)]
    wi_cols = [misc[:, MISC_WI + h: MISC_WI + h + 1] for h in range(IDX_HEADS)]

    def index_tile(kt, carry):
        start = pl.multiple_of(kt * tk, tk)
        kik = ki_ref[pl.ds(start, tk), :]
        tot = jnp.zeros((tq, tk), f32)
        for h in range(IDX_HEADS):
            sc = _dot_nt(qi_heads[h], kik) * (IDX_DIM ** -0.5)
            tot = tot + jnp.maximum(sc, 0.0) * wi_cols[h]
        tot = tot * (IDX_HEADS ** -0.5)
        tot = jnp.where(tot == 0.0, 0.0, tot)
        bits = lax.bitcast_convert_type(tot, i32)
        key = jnp.where(bits < 0, bits ^ jnp.int32(0x7FFFFFFF), bits)
        sp = start + lax.broadcasted_iota(i32, (1, tk), 1)
        key_ref[kt] = jnp.where(sp <= t_col, key, INT_MIN)
        return carry

    lax.fori_loop(0, n_sel_tiles, index_tile, 0)

    def count_ge(trial):
        def body(kt, acc):
            c = (key_ref[kt] >= trial).astype(i32)
            for j in range(tk // LANES):
                acc = acc + c[:, j * LANES:(j + 1) * LANES]
            return acc
        acc = lax.fori_loop(0, n_sel_tiles, body, jnp.zeros((tq, LANES), i32))
        return jnp.sum(acc, axis=-1, keepdims=True)

    c_pos = count_ge(jnp.zeros((tq, 1), i32))
    v0 = jnp.where(c_pos >= topk, 0, INT_MIN).astype(i32)
    c0 = jnp.where(c_pos >= topk, c_pos, n_sel_tiles * tk)

    def bit_step(i, carry):
        v, cge = carry
        trial = v | lax.shift_left(jnp.int32(1), 30 - i)
        c = count_ge(trial)
        ok = c >= topk
        return jnp.where(ok, trial, v), jnp.where(ok, c, cge)

    v_thr, c_ge = lax.fori_loop(0, 31, bit_step, (v0, c0))
    v_eff = jnp.maximum(v_thr, INT_MIN + 1)

    def write_plain(kt, carry):
        dbias_ref[kt] = jnp.where(key_ref[kt] >= v_eff, 0.0, NEG_BIG)
        return carry

    def plain_path():
        lax.fori_loop(0, n_sel_tiles, write_plain, 0)

    def tie_path():
        need = topk - count_ge(v_thr + 1)

        def count_tie_below(limit):
            def body(kt, acc):
                sp = kt * tk + lax.broadcasted_iota(i32, (1, tk), 1)
                c = ((key_ref[kt] == v_thr) & (sp < limit)).astype(i32)
                for j in range(tk // LANES):
                    acc = acc + c[:, j * LANES:(j + 1) * LANES]
                return acc
            acc = lax.fori_loop(0, n_sel_tiles, body, jnp.zeros((tq, LANES), i32))
            return jnp.sum(acc, axis=-1, keepdims=True)

        n_bits = max(1, (t_len - 1).bit_length())

        def idx_step(i, w):
            trial = w | lax.shift_left(jnp.int32(1), n_bits - 1 - i)
            return jnp.where(count_tie_below(trial) < need, trial, w)

        w_last = lax.fori_loop(0, n_bits, idx_step, jnp.zeros((tq, 1), i32))

        def write_tie(kt, carry):
            sp = kt * tk + lax.broadcasted_iota(i32, (1, tk), 1)
            key = key_ref[kt]
            take = (key > v_thr) | ((key == v_thr) & (sp <= w_last))
            dbias_ref[kt] = jnp.where(take & (key >= v_eff), 0.0, NEG_BIG)
            return carry

        lax.fori_loop(0, n_sel_tiles, write_tie, 0)

    lax.cond(jnp.max(c_ge) > topk, tie_path, plain_path)

    def dsa_bias_fn(kt):
        return dbias_ref[kt]

    out_n, out_d = [], []
    for g in range(KV_GROUPS):
        qn4 = jnp.concatenate(head_cols(O_Q, g), axis=0)
        qd4 = jnp.concatenate(head_cols(O_Q + NSA_HEADS * HEAD_DIM, g), axis=0)
        sl4 = slope4(g)
        o_s = _flash_tiles(qn4, kvb_ref, 2 * LANES + g * HEAD_DIM, 3 * LANES + g * HEAD_DIM, tk,
                           0, n_sel_tiles, sl4, sel_bias_fn(g))
        o_w = _flash_tiles(qn4, kvb_ref, 6 * LANES + g * HEAD_DIM, 7 * LANES + g * HEAD_DIM, tkw,
                           win_lo, win_hi, sl4, win_bias_fn)
        o_d = _flash_tiles(qd4, kvb_ref, 4 * LANES + g * HEAD_DIM, 5 * LANES + g * HEAD_DIM, tk,
                           0, n_sel_tiles, sl4, dsa_bias_fn)
        o_n = gate4(g, 0) * o_cmp[g] + gate4(g, 1) * o_s + gate4(g, 2) * o_w
        out_n.append(unstack(o_n))
        out_d.append(unstack(o_d))
    o_ref[0] = jnp.concatenate(out_n + out_d, axis=1).astype(o_ref.dtype)


def _odd_attn_prompt(zq, misc, kvb, wc2, bsz, t_len, tq=128, tk=512, tkw=128):
    tk = min(tk, t_len)
    topk = min(DSA_TOPK_MAX, t_len // 4)
    nb = t_len // BLK
    n_kt = t_len // tk
    zq3 = zq.reshape(bsz, t_len, zq.shape[1])
    misc3 = misc.reshape(bsz, t_len, LANES)
    kvb3 = kvb.reshape(bsz, t_len, kvb.shape[1])
    kern = functools.partial(_odd_attn_kernel, t_len=t_len, tq=tq, tk=tk, tkw=tkw, topk=topk)
    out = pl.pallas_call(
        kern,
        grid=(bsz, t_len // tq),
        in_specs=[pl.BlockSpec((1, tq, zq.shape[1]), lambda b, i: (b, i, 0)),
                  pl.BlockSpec((1, tq, LANES), lambda b, i: (b, i, 0)),
                  pl.BlockSpec((1, t_len, kvb.shape[1]), lambda b, i: (b, 0, 0)),
                  pl.BlockSpec((1, t_len, LANES), lambda b, i: (b, 0, 0)),
                  pl.BlockSpec((2, BLK, LANES), lambda b, i: (0, 0, 0))],
        out_specs=pl.BlockSpec((1, tq, 2 * HALF_W), lambda b, i: (b, i, 0)),
        out_shape=jax.ShapeDtypeStruct((bsz, t_len, 2 * HALF_W), jnp.bfloat16),
        scratch_shapes=[pltpu.VMEM((nb, LANES), jnp.bfloat16),
                        pltpu.VMEM((nb, LANES), jnp.bfloat16),
                        pltpu.VMEM((t_len, IDX_DIM), jnp.bfloat16),
                        pltpu.VMEM((n_kt, tq, tk), jnp.int32),
                        pltpu.VMEM((n_kt, tq, tk), jnp.float32)],
        compiler_params=pltpu.CompilerParams(
            dimension_semantics=("arbitrary", "arbitrary"), vmem_limit_bytes=VMEM_LIMIT),
        name="odd_attn_prompt",
    )(zq3, misc3, kvb3, misc3, wc2)
    return out.reshape(bsz * t_len, 2 * HALF_W)


def _split(z, sizes):
    cuts = [int(c) for c in np.cumsum(sizes)[:-1]]
    return jnp.split(z, cuts, axis=-1)


def _alibi_slopes(n):
    return jnp.asarray(np.power(2.0, -8.0 * np.arange(1, n + 1) / n), dtype=jnp.float32)


def _masked_softmax(s, mask):
    s = jnp.where(mask, s.astype(jnp.float32), -jnp.inf)
    m = jnp.max(s, axis=-1, keepdims=True)
    e = jnp.exp(s - jnp.where(jnp.isfinite(m), m, 0.0))
    return e / jnp.maximum(jnp.sum(e, axis=-1, keepdims=True), 1e-30)


def _short_conv(u_ext, w, b):
    t = u_ext.shape[1] - (CONV_W - 1)
    return b + sum(w[j] * u_ext[:, j:j + t] for j in range(CONV_W))


def _mlstm_chunk(carry, inp):
    c_st, n_st, m_st = carry
    q, k, v, ig, lf = inp
    L = q.shape[2]
    b = jnp.cumsum(lf, axis=-1)
    causal = jnp.tril(jnp.ones((L, L), dtype=bool))
    dmat = jnp.where(causal, b[..., :, None] - b[..., None, :] + ig[..., None, :], -jnp.inf)
    inter = b + m_st[..., None]
    m_t = jnp.maximum(inter, jnp.max(dmat, axis=-1))
    w_intra = jnp.exp(dmat - m_t[..., None])
    w_inter = jnp.exp(inter - m_t)
    s = jnp.einsum('bhtd,bhsd->bhts', q, k) * w_intra
    num = jnp.einsum('bhts,bhsd->bhtd', s, v) + w_inter[..., None] * jnp.einsum('bhed,bhtd->bhte', c_st, q)
    den = jnp.sum(s, axis=-1) + w_inter * jnp.einsum('bhd,bhtd->bht', n_st, q)
    h = num / jnp.maximum(jnp.abs(den), jnp.exp(-m_t))[..., None]
    m_new = m_t[..., -1]
    w_state = jnp.exp(b[..., -1:] - b + ig - m_new[..., None])
    decay = jnp.exp(b[..., -1] + m_st - m_new)
    c_new = decay[..., None, None] * c_st + jnp.einsum('bhs,bhse,bhsd->bhed', w_state, v, k)
    n_new = decay[..., None] * n_st + jnp.einsum('bhs,bhsd->bhd', w_state, k)
    return (c_new, n_new, m_new), h


def _even_mixer(z, gate_b, conv_w, conv_b, conv_prev, ml_state, chunk):
    f32 = jnp.float32
    bn, t, _ = z.shape
    h, bg, cg, q, k, v, og, ig, fg = _split(z, E_SIZES)
    u = cg * h
    u_ext = jnp.concatenate([conv_prev.astype(u.dtype), u], axis=1)
    y_a = bg * _short_conv(u_ext, conv_w, conv_b)
    new_conv = u_ext[:, u_ext.shape[1] - (CONV_W - 1):]
    heads = lambda a: a.reshape(bn, t, ML_HEADS, ML_DIM).astype(f32)
    q, k, v = heads(q), heads(k) * (ML_DIM ** -0.5), heads(v)
    ig = (ig + gate_b[:ML_HEADS]).astype(f32)
    lf = jax.nn.log_sigmoid((fg + gate_b[ML_HEADS:]).astype(f32))
    nc = t // chunk

    def to_chunks(a):
        a = jnp.moveaxis(a.reshape((bn, nc, chunk) + a.shape[2:]), 1, 0)
        return jnp.swapaxes(a, 2, 3)

    carry = tuple(s.astype(f32) for s in ml_state)
    carry, hs = lax.scan(_mlstm_chunk, carry, tuple(to_chunks(a) for a in (q, k, v, ig, lf)))
    hs = jnp.swapaxes(jnp.moveaxis(hs, 0, 1), 2, 3).reshape(bn, t, HALF_W)
    y_b = jax.nn.sigmoid(og) * hs
    return jnp.concatenate([y_a, y_b], axis=-1), new_conv, carry


def _odd_attend(qn, gates, qd, qi, wi, q_pos, kc, vc, slc, dsa_kv, kidx, kw, kw_pos, topn, topk):
    f32 = jnp.float32
    bn, tq, g, r, _ = qn.shape
    nb = kc.shape[1]
    L = kidx.shape[1]
    scale = HEAD_DIM ** -0.5
    m_n = _alibi_slopes(NSA_HEADS).reshape(g, r)
    m_d = _alibi_slopes(DSA_HEADS).reshape(KV_GROUPS, DSA_HEADS // KV_GROUPS)
    jb = jnp.arange(nb)
    dist_c = q_pos[:, None] - (jb * BLK + BLK - 1)[None, :]
    s_c = jnp.einsum('btgrd,bjgd->btgrj', qn, kc).astype(f32) * scale - m_n[:, :, None] * dist_c[:, None, None, :].astype(f32)
    p_c = _masked_softmax(s_c, (dist_c >= 0)[:, None, None, :])
    o_c = jnp.einsum('btgrj,bjgd->btgrd', p_c.astype(vc.dtype), vc)
    cur = q_pos // BLK
    imp = jnp.sum(p_c, axis=3)
    forced = (jb[None, :] == 0) | (jb[None, :] == cur[:, None]) | (jb[None, :] == cur[:, None] - 1)
    imp = jnp.where(forced[:, None, :], FORCE_SCORE, imp)
    imp = jnp.where((jb[None, :] <= cur[:, None])[:, None, :], imp, -jnp.inf)
    top_val, top_idx = lax.top_k(imp, topn)
    b_ix = jnp.arange(bn)[:, None, None, None]
    g_ix = jnp.arange(g)[None, None, :, None]
    sel = slc[b_ix, g_ix, top_idx]
    pos = top_idx[..., None] * BLK + jnp.arange(BLK)
    dist_s = q_pos[None, :, None, None, None] - pos
    mask_s = (dist_s >= 0) & jnp.isfinite(top_val)[..., None]
    s_s = jnp.einsum('btgrd,btgnpd->btgrnp', qn, sel[..., 0, :]).astype(f32) * scale - m_n[None, None, :, :, None, None] * dist_s[:, :, :, None].astype(f32)
    shp = s_s.shape
    p_s = _masked_softmax(s_s.reshape(shp[:4] + (-1,)), mask_s[:, :, :, None].reshape(bn, tq, g, 1, -1)).reshape(shp)
    o_s = jnp.einsum('btgrnp,btgnpd->btgrd', p_s.astype(sel.dtype), sel[..., 1, :])
    dist_w = q_pos[:, None] - kw_pos[None, :]
    mask_w = (dist_w >= 0) & (dist_w < WINDOW) & (kw_pos >= 0)[None, :]
    s_w = jnp.einsum('btgrd,blgd->btgrl', qn, kw[:, :, 0]).astype(f32) * scale - m_n[:, :, None] * dist_w[:, None, None, :].astype(f32)
    p_w = _masked_softmax(s_w, mask_w[:, None, None, :])
    o_w = jnp.einsum('btgrl,blgd->btgrd', p_w.astype(kw.dtype), kw[:, :, 1])
    o_n = (gates[..., 0:1] * o_c + gates[..., 1:2] * o_s + gates[..., 2:3] * o_w).astype(qn.dtype).reshape(bn, tq, -1)
    k_pos = jnp.arange(L)
    isc = jnp.einsum('bthe,ble->bthl', qi, kidx).astype(f32) * (IDX_DIM ** -0.5)
    isc = jnp.einsum('bthl,bth->btl', jax.nn.relu(isc), wi.astype(f32)) * (IDX_HEADS ** -0.5)
    isc = jnp.where((k_pos[None, :] <= q_pos[:, None])[None], isc, -jnp.inf)
    i_val, i_idx = lax.top_k(isc, topk)
    sel_d = dsa_kv[jnp.arange(bn)[:, None, None], i_idx]
    dist_d = (q_pos[None, :, None] - i_idx).astype(f32)
    s_d = jnp.einsum('btgrd,btkgd->btgrk', qd, sel_d[:, :, :, 0]).astype(f32) * scale - m_d[None, None, :, :, None] * dist_d[:, :, None, None, :]
    p_d = _masked_softmax(s_d, jnp.isfinite(i_val)[:, :, None, None, :])
    o_d = jnp.einsum('btgrk,btkgd->btgrd', p_d.astype(sel_d.dtype), sel_d[:, :, :, 1]).reshape(bn, tq, -1)
    return jnp.concatenate([o_n, o_d.astype(qn.dtype)], axis=-1)


def _odd_mixer(z, w_cmp, past_kv, past_idx, past_win, keep):
    bn, t, _ = z.shape
    p_len = past_kv.shape[1]
    wp = past_win.shape[1]
    g, r = KV_GROUPS, NSA_HEADS // KV_GROUPS
    qn, kvc, kvs, kvw, gates, qd, kvd, qi, ki, wi = _split(z, O_SIZES)
    qn = qn.reshape(bn, t, g, r, HEAD_DIM)
    qd = qd.reshape(bn, t, g, DSA_HEADS // g, HEAD_DIM)
    gates = jax.nn.sigmoid(gates.astype(jnp.float32)).reshape(bn, t, g, r, 3)
    qi = qi.reshape(bn, t, IDX_HEADS, IDX_DIM)
    kv_new = jnp.concatenate([kvc, kvs, kvd], axis=-1).reshape(bn, t, KV_SLOTS, g, HEAD_DIM)
    kvw_new = kvw.reshape(bn, t, 2, g, HEAD_DIM)
    kv = jnp.concatenate([past_kv.astype(kv_new.dtype), kv_new], axis=1)
    kidx = jnp.concatenate([past_idx.astype(ki.dtype), ki], axis=1)
    win_all = jnp.concatenate([past_win.astype(kvw_new.dtype), kvw_new], axis=1)
    L = p_len + t
    nb = -(-L // BLK)
    kvb = jnp.pad(kv[:, :, :4], ((0, 0), (0, nb * BLK - L), (0, 0), (0, 0), (0, 0)))
    kvb = kvb.reshape(bn, nb, BLK, 4, g, HEAD_DIM)
    kc = jnp.einsum('bjpgd,pd->bjgd', kvb[:, :, :, 0], w_cmp[0])
    vc = jnp.einsum('bjpgd,pd->bjgd', kvb[:, :, :, 1], w_cmp[1])
    slc = jnp.transpose(kvb[:, :, :, 2:4], (0, 4, 1, 2, 3, 5))
    dsa_kv = kv[:, :, 4:6]
    topn = min(NSA_TOPN, nb)
    topk = min(DSA_TOPK_MAX, L // 4)
    qb = t if t <= Q_BLOCK else Q_BLOCK
    nqb = t // qb

    def blocks(a):
        return jnp.moveaxis(a.reshape((bn, nqb, qb) + a.shape[2:]), 1, 0)

    def body(args):
        qn_b, g_b, qd_b, qi_b, wi_b, start = args
        q_pos = p_len + start + jnp.arange(qb)
        kw = lax.dynamic_slice_in_dim(win_all, start, wp + qb, axis=1)
        kw_pos = p_len - wp + start + jnp.arange(wp + qb)
        return _odd_attend(qn_b, g_b, qd_b, qi_b, wi_b, q_pos, kc, vc, slc, dsa_kv, kidx, kw, kw_pos, topn, topk)

    starts = jnp.arange(nqb, dtype=jnp.int32) * qb
    o = lax.map(body, (blocks(qn), blocks(gates), blocks(qd), blocks(qi), blocks(wi), starts))
    o = jnp.moveaxis(o, 0, 1).reshape(bn, t, 2 * HALF_W)
    new_win = win_all[:, win_all.shape[1] - keep:]
    return o, kv_new, ki, new_win


def _pad_cols(w, n):
    return jnp.pad(w, ((0, 0), (0, n - w.shape[1])))


def kernel(x_prompt, x_sample, state_conv, state_C, state_n, state_m, cache_kv, cache_idx, cache_win, page_table,
           w_in_e, gate_b_e, conv_w, conv_b, w_out_e, w_in_o, w_cmp, w_out_o, w_up, w_down, ln_g, ln_b):
    f32, bf16 = jnp.float32, jnp.bfloat16
    bp, tp, d = x_prompt.shape
    dbs = x_sample.shape[0]
    past = page_table.shape[1] * cache_kv.shape[2]
    keep = cache_win.shape[2]
    yp = x_prompt.reshape(bp * tp, d)
    ys = x_sample.reshape(dbs, d)
    tm_p, tm_s = 512, dbs
    outs = {}
    for layer in range(DEPTH):
        i = layer // 2
        if layer % 2 == 0:
            n_pad = _round_up(E_COLS, LANES)
            w_in = _pad_cols(w_in_e[i], n_pad).astype(bf16)
            zs = _proj(ys, w_in, tm_s)[:, :E_COLS].reshape(dbs, 1, E_COLS)
            ya, qkv, og, gts, tails = _even_proj(yp, w_in, conv_w[i], conv_b[i], tm_p, tp)
            yb, c_fin, n_fin, m_fin = _mlstm_prompt(qkv, og, gts, gate_b_e[i], bp, tp)
            cvp = tails.reshape(bp, tp // tm_p, CONV_TAIL, HALF_W)[:, -1, CONV_TAIL - (CONV_W - 1):]
            ms, cvs, sts = _even_mixer(zs, gate_b_e[i], conv_w[i], conv_b[i], state_conv[i],
                                       (state_C[i], state_n[i], state_m[i]), 1)
            outs['conv'] = (cvp[None], cvs[None])
            outs['c'] = (c_fin[None], sts[0][None])
            outs['n'] = (n_fin[None], sts[1][None])
            outs['m'] = (m_fin[:, :, 0][None], sts[2][None])
            w_out = w_out_e[i].astype(bf16)
        else:
            n_pad = _round_up(O_COLS, LANES)
            w_in = _pad_cols(w_in_o[i], n_pad).astype(bf16)
            zs = _proj(ys, w_in, tm_s)[:, :O_COLS].reshape(dbs, 1, O_COLS)
            past_kv = cache_kv[i][page_table].reshape(dbs, past, KV_SLOTS, KV_GROUPS, HEAD_DIM)
            past_idx = cache_idx[i][page_table].reshape(dbs, past, IDX_DIM)
            zq, kvp, kvb, winp, miscp = _odd_proj(yp, _pack_w_in_o(w_in_o[i]).astype(bf16), tm_p)
            wc2 = jnp.concatenate([w_cmp[i], w_cmp[i]], axis=-1)
            mp = _odd_attn_prompt(zq, miscp, kvb, wc2, bp, tp)
            kvp = kvp.reshape(bp, tp, KV_SLOTS, KV_GROUPS, HEAD_DIM)
            ixp = miscp[:, :IDX_DIM].reshape(bp, tp, IDX_DIM)
            wnp = winp.reshape(bp, tp, 2, KV_GROUPS, HEAD_DIM)[:, tp - keep:]
            ms, kvs, ixs, wns = _odd_mixer(zs, w_cmp[i], past_kv, past_idx, cache_win[i], keep)
            outs['kv'] = (kvp[None], kvs[None])
            outs['idx'] = (ixp[None], ixs[None])
            outs['win'] = (wnp[None], wns[None])
            w_out = w_out_o[i].astype(bf16)
        if layer % 2 == 0:
            yp = _outproj2_ln(ya, yb.reshape(bp * tp, HALF_W), w_out, yp, ln_g[layer, 0], ln_b[layer, 0], tm_p)
        else:
            yp = _outproj_ln(mp, w_out, yp, ln_g[layer, 0], ln_b[layer, 0], tm_p)
        ys = _outproj_ln(ms.reshape(dbs, d), w_out, ys, ln_g[layer, 0], ln_b[layer, 0], tm_s)
        wu, wd = w_up[layer].astype(bf16), w_down[layer].astype(bf16)
        yp = _mlp_ln(yp, wu, wd, ln_g[layer, 1], ln_b[layer, 1], tm_p, 1024)
        ys = _mlp_ln(ys, wu, wd, ln_g[layer, 1], ln_b[layer, 1], tm_s, 1024)
    return (yp.reshape(bp, tp, d), ys.reshape(dbs, 1, d),
            outs['conv'][0], outs['conv'][1], outs['c'][0], outs['c'][1],
            outs['n'][0], outs['n'][1], outs['m'][0], outs['m'][1],
            outs['kv'][0], outs['kv'][1], outs['idx'][0], outs['idx'][1],
            outs['win'][0], outs['win'][1])
```

```python
import functools

import jax
import jax.numpy as jnp
import numpy as np
from jax import lax
from jax.experimental import pallas as pl
from jax.experimental.pallas import tpu as pltpu

D_MODEL = 1024
DEPTH = 2
HALF_W = 512
D_FF = 4096
CONV_W = 3
ML_HEADS = 4
ML_DIM = 128
ML_CHUNK = 64
HEAD_DIM = 64
KV_GROUPS = 2
NSA_HEADS = 8
DSA_HEADS = 8
BLK = 64
NSA_TOPN = 8
WINDOW = 256
IDX_HEADS = 4
IDX_DIM = 32
DSA_TOPK_MAX = 256
Q_BLOCK = 128
KV_SLOTS = 6
FORCE_SCORE = 1e4
ALPHA = (2.0 * DEPTH) ** 0.25
LN_EPS = 1e-5
E_SIZES = (512, 512, 512, 512, 512, 512, 512, 4, 4)
O_SIZES = (512, 256, 256, 256, 24, 512, 256, 128, 32, 4)
E_COLS = sum(E_SIZES)
O_COLS = sum(O_SIZES)
LANES = 128
VMEM_LIMIT = 48 * 1024 * 1024


def _round_up(n, m):
    return -(-n // m) * m


def _proj_kernel(x_ref, w_ref, o_ref):
    o_ref[...] = jnp.dot(x_ref[...].astype(jnp.bfloat16), w_ref[...],
                         preferred_element_type=jnp.float32)


def _proj(x, w_bf16, tm):
    m, k = x.shape
    n = w_bf16.shape[1]
    tn = n
    for cand in (1024, 768, 512, 256, 128):
        if n % cand == 0:
            tn = cand
            break
    return pl.pallas_call(
        _proj_kernel,
        grid=(n // tn, m // tm),
        in_specs=[pl.BlockSpec((tm, k), lambda j, i: (i, 0)),
                  pl.BlockSpec((k, tn), lambda j, i: (0, j))],
        out_specs=pl.BlockSpec((tm, tn), lambda j, i: (i, j)),
        out_shape=jax.ShapeDtypeStruct((m, n), jnp.float32),
        compiler_params=pltpu.CompilerParams(
            dimension_semantics=("arbitrary", "arbitrary"), vmem_limit_bytes=VMEM_LIMIT),
        name="proj",
    )(x, w_bf16)


def _layer_norm_rows(v, g, b):
    mu = jnp.mean(v, axis=-1, keepdims=True)
    d = v - mu
    var = jnp.mean(d * d, axis=-1, keepdims=True)
    return d * lax.rsqrt(var + LN_EPS) * g + b


def _outproj_ln_kernel(y_ref, w_ref, x_ref, g_ref, b_ref, o_ref):
    mix = jnp.dot(y_ref[...].astype(jnp.bfloat16), w_ref[...], preferred_element_type=jnp.float32)
    o_ref[...] = _layer_norm_rows(ALPHA * x_ref[...] + mix, g_ref[...], b_ref[...])


def _outproj_ln(y, w_bf16, x, g, b, tm):
    m, d = x.shape
    k = y.shape[1]
    return pl.pallas_call(
        _outproj_ln_kernel,
        grid=(m // tm,),
        in_specs=[pl.BlockSpec((tm, k), lambda i: (i, 0)),
                  pl.BlockSpec((k, d), lambda i: (0, 0)),
                  pl.BlockSpec((tm, d), lambda i: (i, 0)),
                  pl.BlockSpec((1, d), lambda i: (0, 0)),
                  pl.BlockSpec((1, d), lambda i: (0, 0))],
        out_specs=pl.BlockSpec((tm, d), lambda i: (i, 0)),
        out_shape=jax.ShapeDtypeStruct((m, d), jnp.float32),
        compiler_params=pltpu.CompilerParams(
            dimension_semantics=("arbitrary",), vmem_limit_bytes=VMEM_LIMIT),
        name="outproj_ln",
    )(y, w_bf16, x, g.reshape(1, d), b.reshape(1, d))


def _mlp_ln_kernel(x_ref, wu_ref, wd_ref, g_ref, b_ref, o_ref, acc_ref):
    f = pl.program_id(1)

    @pl.when(f == 0)
    def _():
        acc_ref[...] = jnp.zeros_like(acc_ref)

    up = jnp.dot(x_ref[...].astype(jnp.bfloat16), wu_ref[...], preferred_element_type=jnp.float32)
    act = jnp.square(jnp.maximum(up, 0.0))
    acc_ref[...] += jnp.dot(act.astype(jnp.bfloat16), wd_ref[...], preferred_element_type=jnp.float32)

    @pl.when(f == pl.num_programs(1) - 1)
    def _():
        o_ref[...] = _layer_norm_rows(ALPHA * x_ref[...] + acc_ref[...], g_ref[...], b_ref[...])


def _mlp_ln(x, wu_bf16, wd_bf16, g, b, tm, tf):
    m, d = x.shape
    ff = wu_bf16.shape[1]
    return pl.pallas_call(
        _mlp_ln_kernel,
        grid=(m // tm, ff // tf),
        in_specs=[pl.BlockSpec((tm, d), lambda i, f: (i, 0)),
                  pl.BlockSpec((d, tf), lambda i, f: (0, f)),
                  pl.BlockSpec((tf, d), lambda i, f: (f, 0)),
                  pl.BlockSpec((1, d), lambda i, f: (0, 0)),
                  pl.BlockSpec((1, d), lambda i, f: (0, 0))],
        out_specs=pl.BlockSpec((tm, d), lambda i, f: (i, 0)),
        out_shape=jax.ShapeDtypeStruct((m, d), jnp.float32),
        scratch_shapes=[pltpu.VMEM((tm, d), jnp.float32)],
        compiler_params=pltpu.CompilerParams(
            dimension_semantics=("arbitrary", "arbitrary"), vmem_limit_bytes=VMEM_LIMIT),
        name="mlp_ln",
    )(x, wu_bf16, wd_bf16, g.reshape(1, d), b.reshape(1, d))


E_QKV = 3 * HALF_W
E_OG = 6 * HALF_W
E_GATE = 7 * HALF_W
E_PACKED = 7 * HALF_W + LANES
CONV_TAIL = 8


def _even_proj_kernel(x_ref, w_ref, cw_ref, cb_ref, ya_ref, qkv_ref, og_ref, gate_ref, tail_ref, carry_ref,
                      *, tiles_per_seq):
    f32, bf16 = jnp.float32, jnp.bfloat16
    tm = x_ref.shape[0]

    @pl.when(pl.program_id(0) % tiles_per_seq == 0)
    def _():
        carry_ref[...] = jnp.zeros_like(carry_ref)

    z = jnp.dot(x_ref[...].astype(bf16), w_ref[...], preferred_element_type=f32)
    u = z[:, 2 * HALF_W:3 * HALF_W] * z[:, 0:HALF_W]
    prev = carry_ref[...]
    row = lax.broadcasted_iota(jnp.int32, (tm, 1), 0)
    conv = cb_ref[...] + cw_ref[CONV_W - 1:CONV_W, :] * u
    for back in range(1, CONV_W):
        shifted = pltpu.roll(u, back, 0)
        for r in range(back):
            shifted = jnp.where(row == r, prev[CONV_TAIL - back + r:CONV_TAIL - back + r + 1, :], shifted)
        conv = conv + cw_ref[CONV_W - 1 - back:CONV_W - back, :] * shifted
    ya_ref[...] = (z[:, HALF_W:2 * HALF_W] * conv).astype(bf16)
    tail = u[tm - CONV_TAIL:]
    carry_ref[...] = tail
    tail_ref[0] = tail
    qkv_ref[:, 0:HALF_W] = z[:, E_QKV:E_QKV + HALF_W].astype(bf16)
    qkv_ref[:, HALF_W:2 * HALF_W] = (z[:, E_QKV + HALF_W:E_QKV + 2 * HALF_W] * (ML_DIM ** -0.5)).astype(bf16)
    qkv_ref[:, 2 * HALF_W:3 * HALF_W] = z[:, E_QKV + 2 * HALF_W:E_OG].astype(bf16)
    og_ref[...] = z[:, E_OG:E_GATE]
    gate_ref[...] = z[:, E_GATE:E_PACKED]


def _even_proj(x, w_bf16, conv_w, conv_b, tm, seq_len):
    m, k = x.shape
    widths = (HALF_W, 3 * HALF_W, HALF_W, LANES)
    dtypes = (jnp.bfloat16, jnp.bfloat16, jnp.float32, jnp.float32)
    kern = functools.partial(_even_proj_kernel, tiles_per_seq=seq_len // tm)
    return pl.pallas_call(
        kern,
        grid=(m // tm,),
        in_specs=[pl.BlockSpec((tm, k), lambda i: (i, 0)),
                  pl.BlockSpec((k, E_PACKED), lambda i: (0, 0)),
                  pl.BlockSpec((CONV_W, HALF_W), lambda i: (0, 0)),
                  pl.BlockSpec((1, HALF_W), lambda i: (0, 0))],
        out_specs=[pl.BlockSpec((tm, n), lambda i: (i, 0)) for n in widths]
        + [pl.BlockSpec((1, CONV_TAIL, HALF_W), lambda i: (i, 0, 0))],
        out_shape=[jax.ShapeDtypeStruct((m, n), dt) for n, dt in zip(widths, dtypes)]
        + [jax.ShapeDtypeStruct((m // tm, CONV_TAIL, HALF_W), jnp.float32)],
        scratch_shapes=[pltpu.VMEM((CONV_TAIL, HALF_W), jnp.float32)],
        compiler_params=pltpu.CompilerParams(
            dimension_semantics=("arbitrary",), vmem_limit_bytes=VMEM_LIMIT),
        name="even_proj",
    )(x, w_bf16, conv_w, conv_b.reshape(1, HALF_W))


def _mlstm_kernel(qkv_ref, og_ref, gate_ref, gb_ref, yb_ref, c_ref, n_ref, m_ref):
    f32, bf16 = jnp.float32, jnp.bfloat16
    L = qkv_ref.shape[1]

    @pl.when(pl.program_id(1) == 0)
    def _():
        c_ref[...] = jnp.zeros_like(c_ref)
        n_ref[...] = jnp.zeros_like(n_ref)
        m_ref[...] = jnp.zeros_like(m_ref)

    pre = gate_ref[0] + gb_ref[...]
    lf = jax.nn.log_sigmoid(pre)
    row = lax.broadcasted_iota(jnp.int32, (L, L), 0)
    col = lax.broadcasted_iota(jnp.int32, (L, L), 1)
    causal = col <= row
    tri = jnp.where(causal, 1.0, 0.0).astype(f32)
    b_all = jnp.dot(tri, lf, preferred_element_type=f32, precision=lax.Precision.HIGHEST)
    pre_t = pre.T
    b_t = b_all.T
    for h in range(ML_HEADS):
        q = qkv_ref[0, :, h * ML_DIM:(h + 1) * ML_DIM]
        k = qkv_ref[0, :, HALF_W + h * ML_DIM:HALF_W + (h + 1) * ML_DIM]
        v = qkv_ref[0, :, 2 * HALF_W + h * ML_DIM:2 * HALF_W + (h + 1) * ML_DIM]
        ig_col = pre[:, h:h + 1]
        b_col = b_all[:, ML_HEADS + h:ML_HEADS + h + 1]
        a_row = pre_t[h:h + 1, :] - b_t[ML_HEADS + h:ML_HEADS + h + 1, :]
        m_st = m_ref[0, h:h + 1, 0:1]
        c_st = c_ref[0, h]
        n_st = n_ref[0, h:h + 1, :]
        dmat = jnp.where(causal, b_col + a_row, -jnp.inf)
        inter = b_col + m_st
        m_t = jnp.maximum(inter, jnp.max(dmat, axis=-1, keepdims=True))
        w_intra = jnp.exp(dmat - m_t)
        w_inter = jnp.exp(inter - m_t)
        s = _dot_nt(q, k) * w_intra
        num = jnp.dot(s.astype(bf16), v, preferred_element_type=f32) + w_inter * _dot_nt(q, c_st.astype(bf16))
        den = jnp.sum(s, axis=-1, keepdims=True) + w_inter * jnp.sum(q.astype(f32) * n_st, axis=-1, keepdims=True)
        hs = num * (1.0 / jnp.maximum(jnp.abs(den), jnp.exp(-m_t)))
        m_new = m_t[L - 1:L, :]
        b_last = b_col[L - 1:L, :]
        w_state = jnp.exp(b_last - b_col + ig_col - m_new)
        decay = jnp.exp(b_last + m_st - m_new)
        vw = (v.astype(f32) * w_state).astype(bf16)
        c_ref[0, h] = decay * c_st + lax.dot_general(vw, k, (((0,), (0,)), ((), ())), preferred_element_type=f32)
        n_ref[0, h:h + 1, :] = decay * n_st + jnp.sum(k.astype(f32) * w_state, axis=0, keepdims=True)
        m_ref[0, h:h + 1, :] = jnp.broadcast_to(m_new, (1, LANES))
        og = og_ref[0, :, h * ML_DIM:(h + 1) * ML_DIM]
        yb_ref[0, :, h * ML_DIM:(h + 1) * ML_DIM] = (jax.nn.sigmoid(og) * hs).astype(bf16)


def _mlstm_prompt(qkv, og, gates, gate_b, bsz, t_len, chunk=128):
    gb = jnp.pad(gate_b, (0, LANES - gate_b.shape[0])).reshape(1, LANES)
    return pl.pallas_call(
        _mlstm_kernel,
        grid=(bsz, t_len // chunk),
        in_specs=[pl.BlockSpec((1, chunk, 3 * HALF_W), lambda b, c: (b, c, 0)),
                  pl.BlockSpec((1, chunk, HALF_W), lambda b, c: (b, c, 0)),
                  pl.BlockSpec((1, chunk, LANES), lambda b, c: (b, c, 0)),
                  pl.BlockSpec((1, LANES), lambda b, c: (0, 0))],
        out_specs=[pl.BlockSpec((1, chunk, HALF_W), lambda b, c: (b, c, 0)),
                   pl.BlockSpec((1, ML_HEADS, ML_DIM, ML_DIM), lambda b, c: (b, 0, 0, 0)),
                   pl.BlockSpec((1, ML_HEADS, ML_DIM), lambda b, c: (b, 0, 0)),
                   pl.BlockSpec((1, ML_HEADS, LANES), lambda b, c: (b, 0, 0))],
        out_shape=[jax.ShapeDtypeStruct((bsz, t_len, HALF_W), jnp.bfloat16),
                   jax.ShapeDtypeStruct((bsz, ML_HEADS, ML_DIM, ML_DIM), jnp.float32),
                   jax.ShapeDtypeStruct((bsz, ML_HEADS, ML_DIM), jnp.float32),
                   jax.ShapeDtypeStruct((bsz, ML_HEADS, LANES), jnp.float32)],
        compiler_params=pltpu.CompilerParams(
            dimension_semantics=("arbitrary", "arbitrary"), vmem_limit_bytes=VMEM_LIMIT),
        name="mlstm_prompt",
    )(qkv.reshape(bsz, t_len, 3 * HALF_W), og.reshape(bsz, t_len, HALF_W),
      gates.reshape(bsz, t_len, LANES), gb)


def _outproj2_ln_kernel(ya_ref, yb_ref, w_ref, x_ref, g_ref, b_ref, o_ref):
    mix = jnp.dot(ya_ref[...], w_ref[0:HALF_W, :], preferred_element_type=jnp.float32)
    mix = mix + jnp.dot(yb_ref[...], w_ref[HALF_W:, :], preferred_element_type=jnp.float32)
    o_ref[...] = _layer_norm_rows(ALPHA * x_ref[...] + mix, g_ref[...], b_ref[...])


def _outproj2_ln(ya, yb, w_bf16, x, g, b, tm):
    m, d = x.shape
    return pl.pallas_call(
        _outproj2_ln_kernel,
        grid=(m // tm,),
        in_specs=[pl.BlockSpec((tm, HALF_W), lambda i: (i, 0)),
                  pl.BlockSpec((tm, HALF_W), lambda i: (i, 0)),
                  pl.BlockSpec((2 * HALF_W, d), lambda i: (0, 0)),
                  pl.BlockSpec((tm, d), lambda i: (i, 0)),
                  pl.BlockSpec((1, d), lambda i: (0, 0)),
                  pl.BlockSpec((1, d), lambda i: (0, 0))],
        out_specs=pl.BlockSpec((tm, d), lambda i: (i, 0)),
        out_shape=jax.ShapeDtypeStruct((m, d), jnp.float32),
        compiler_params=pltpu.CompilerParams(
            dimension_semantics=("arbitrary",), vmem_limit_bytes=VMEM_LIMIT),
        name="outproj2_ln",
    )(ya, yb, w_bf16, x, g.reshape(1, d), b.reshape(1, d))


O_Q = 0
O_KV = 1024
O_WIN = 1792
O_QI = 2048
O_MISC = 2176
O_PACKED = 2304
MISC_WI = IDX_DIM
MISC_GATE = IDX_DIM + IDX_HEADS
NEG_BIG = -(2.0 ** 100)
INT_MIN = -(2 ** 31)
ALIBI = tuple(float(2.0 ** (-8.0 * (h + 1) / NSA_HEADS)) for h in range(NSA_HEADS))


def _pack_w_in_o(w):
    qn, kvc, kvs, kvw, gates, qd, kvd, qi, ki, wi = _split(w, O_SIZES)
    scale = HEAD_DIM ** -0.5
    pad = jnp.zeros((w.shape[0], O_PACKED - O_MISC - IDX_DIM - IDX_HEADS - 3 * NSA_HEADS), w.dtype)
    return jnp.concatenate([qn * scale, qd * scale, kvc, kvs, kvd, kvw, qi, ki, wi, gates, pad], axis=1)


def _odd_proj_kernel(x_ref, w_ref, zq_ref, kv_ref, kvb_ref, win_ref, misc_ref):
    z = jnp.dot(x_ref[...].astype(jnp.bfloat16), w_ref[...], preferred_element_type=jnp.float32)
    zq_ref[:, 0:O_KV] = z[:, 0:O_KV].astype(jnp.bfloat16)
    zq_ref[:, O_KV:O_KV + LANES] = z[:, O_QI:O_MISC].astype(jnp.bfloat16)
    kv_ref[...] = z[:, O_KV:O_WIN]
    kvb_ref[...] = z[:, O_KV:O_QI].astype(jnp.bfloat16)
    win_ref[...] = z[:, O_WIN:O_QI]
    misc_ref[...] = z[:, O_MISC:O_PACKED]


def _odd_proj(x, w_packed_bf16, tm):
    m, k = x.shape
    widths = (O_KV + LANES, O_WIN - O_KV, O_QI - O_KV, O_QI - O_WIN, LANES)
    dtypes = (jnp.bfloat16, jnp.float32, jnp.bfloat16, jnp.float32, jnp.float32)
    return pl.pallas_call(
        _odd_proj_kernel,
        grid=(m // tm,),
        in_specs=[pl.BlockSpec((tm, k), lambda i: (i, 0)),
                  pl.BlockSpec((k, O_PACKED), lambda i: (0, 0))],
        out_specs=[pl.BlockSpec((tm, n), lambda i: (i, 0)) for n in widths],
        out_shape=[jax.ShapeDtypeStruct((m, n), dt) for n, dt in zip(widths, dtypes)],
        compiler_params=pltpu.CompilerParams(
            dimension_semantics=("arbitrary",), vmem_limit_bytes=VMEM_LIMIT),
        name="odd_proj",
    )(x, w_packed_bf16)


def _dot_nt(a, b):
    return lax.dot_general(a, b, (((1,), (1,)), ((), ())), preferred_element_type=jnp.float32)


def _flash_tiles(q4, kv_ref, k_col, v_col, tk, lo, hi, slope_col4, bias_fn):
    rows = q4.shape[0]

    def body(kt, carry):
        m, l, acc = carry
        start = pl.multiple_of(kt * tk, tk)
        k = kv_ref[0, pl.ds(start, tk), k_col:k_col + HEAD_DIM]
        v = kv_ref[0, pl.ds(start, tk), v_col:v_col + HEAD_DIM]
        s = _dot_nt(q4, k)
        sp = (start + lax.broadcasted_iota(jnp.int32, (1, tk), 1)).astype(jnp.float32)
        b = bias_fn(kt)
        s = s + slope_col4 * sp + jnp.concatenate([b, b, b, b], axis=0)
        m_new = jnp.maximum(m, jnp.max(s, axis=-1, keepdims=True))
        alpha = jnp.exp(m - m_new)
        p = jnp.exp(s - m_new)
        l = alpha * l + jnp.sum(p, axis=-1, keepdims=True)
        acc = alpha * acc + jnp.dot(p.astype(jnp.bfloat16), v, preferred_element_type=jnp.float32)
        return m_new, l, acc

    m0 = jnp.full((rows, 1), -jnp.inf, jnp.float32)
    l0 = jnp.zeros((rows, 1), jnp.float32)
    a0 = jnp.zeros((rows, HEAD_DIM), jnp.float32)
    _, l, acc = lax.fori_loop(lo, hi, body, (m0, l0, a0))
    return acc * (1.0 / l)


def _odd_attn_kernel(zq_ref, miscq_ref, kvb_ref, misck_ref, wc_ref, o_ref,
                     kc_ref, vc_ref, ki_ref, key_ref, dbias_ref, *, t_len, tq, tk, tkw, topk):
    f32, bf16, i32 = jnp.float32, jnp.bfloat16, jnp.int32
    nb = t_len // BLK
    qi_blk = pl.program_id(1)
    q0 = qi_blk * tq

    @pl.when(qi_blk == 0)
    def _():
        ck = kvb_ref[0, :, 0:LANES].astype(f32).reshape(nb, BLK, LANES)
        kc_ref[...] = jnp.sum(ck * wc_ref[0][None], axis=1).astype(bf16)
        cv = kvb_ref[0, :, LANES:2 * LANES].astype(f32).reshape(nb, BLK, LANES)
        vc_ref[...] = jnp.sum(cv * wc_ref[1][None], axis=1).astype(bf16)
        ki_ref[...] = misck_ref[0, :, 0:IDX_DIM].astype(bf16)

    zq = zq_ref[0]
    misc = miscq_ref[0]
    gates = jax.nn.sigmoid(misc[:, MISC_GATE:MISC_GATE + 3 * NSA_HEADS])
    t_col = q0 + lax.broadcasted_iota(i32, (tq, 1), 0)

    def head_cols(base, g):
        return [zq[:, base + (g * 4 + r) * HEAD_DIM: base + (g * 4 + r + 1) * HEAD_DIM] for r in range(4)]

    def slope4(g):
        return jnp.concatenate([jnp.full((tq, 1), ALIBI[g * 4 + r], f32) for r in range(4)], axis=0)

    def gate4(g, c):
        return jnp.concatenate([gates[:, (g * 4 + r) * 3 + c:(g * 4 + r) * 3 + c + 1] for r in range(4)], axis=0)

    def unstack(x4):
        return jnp.concatenate([x4[r * tq:(r + 1) * tq] for r in range(4)], axis=1)

    jb = lax.broadcasted_iota(i32, (1, nb), 1)
    dist_c = t_col - (jb * BLK + (BLK - 1))
    mask_c = dist_c >= 0
    dist_cf = dist_c.astype(f32)
    cur = t_col // BLK
    forced = (jb == 0) | (jb == cur) | (jb == cur - 1)
    admissible = jb <= cur
    jb_full = lax.broadcasted_iota(i32, (tq, nb), 1)
    o_cmp, sel_bias = [], []
    for g in range(KV_GROUPS):
        kc_g = kc_ref[:, g * HEAD_DIM:(g + 1) * HEAD_DIM]
        vc_g = vc_ref[:, g * HEAD_DIM:(g + 1) * HEAD_DIM]
        imp = jnp.zeros((tq, nb), f32)
        heads = []
        for r, qh in enumerate(head_cols(O_Q, g)):
            s = _dot_nt(qh, kc_g) - ALIBI[g * 4 + r] * dist_cf
            s = jnp.where(mask_c, s, NEG_BIG)
            m = jnp.max(s, axis=-1, keepdims=True)
            e = jnp.where(mask_c, jnp.exp(s - m), 0.0)
            p = e * (1.0 / jnp.maximum(jnp.sum(e, axis=-1, keepdims=True), 1e-30))
            imp = imp + p
            heads.append(jnp.dot(p.astype(bf16), vc_g, preferred_element_type=f32))
        o_cmp.append(jnp.concatenate(heads, axis=0))
        imp = jnp.where(forced, FORCE_SCORE, imp)
        imp = jnp.where(admissible, imp, -jnp.inf)
        sel = jnp.zeros((tq, nb), f32)
        for _ in range(min(NSA_TOPN, nb)):
            m = jnp.max(imp, axis=-1, keepdims=True)
            first = jnp.min(jnp.where(imp == m, jb_full, nb), axis=-1, keepdims=True)
            hit = jb_full == first
            sel = jnp.where(hit & (m > -jnp.inf), 1.0, sel)
            imp = jnp.where(hit, -jnp.inf, imp)
        sel_bias.append(jnp.where(sel > 0.0, 0.0, NEG_BIG).astype(bf16))

    n_sel_tiles = (q0 + tq + tk - 1) // tk
    blocks_per_tile = tk // BLK

    def causal_bias(start, width):
        sp = start + lax.broadcasted_iota(i32, (1, width), 1)
        return jnp.where(sp <= t_col, 0.0, NEG_BIG)

    def sel_bias_fn(g):
        def fn(kt):
            row = lax.broadcasted_iota(i32, (nb, tk), 0)
            col = lax.broadcasted_iota(i32, (nb, tk), 1)
            expand = jnp.where(row == kt * blocks_per_tile + col // BLK, 1.0, 0.0).astype(bf16)
            return jnp.dot(sel_bias[g], expand, preferred_element_type=f32) + causal_bias(kt * tk, tk)
        return fn

    def win_bias_fn(kt):
        sp = kt * tkw + lax.broadcasted_iota(i32, (1, tkw), 1)
        dist = t_col - sp
        return jnp.where((dist >= 0) & (dist < WINDOW), 0.0, NEG_BIG)

    win_lo = jnp.maximum(q0 - WINDOW, 0) // tkw
    win_hi = (q0 + tq + tkw - 1) // tkw

    qi_heads = [zq[:, O_KV + h * IDX_DIM: O_KV + (h + 1) * IDX_DIM] for h in range(IDX_HEADS)]
    wi_cols = [misc[:, MISC_WI + h: MISC_WI + h + 1] for h in range(IDX_HEADS)]

    def index_tile(kt, carry):
        start = pl.multiple_of(kt * tk, tk)
        kik = ki_ref[pl.ds(start, tk), :]
        tot = jnp.zeros((tq, tk), f32)
        for h in range(IDX_HEADS):
            sc = _dot_nt(qi_heads[h], kik) * (IDX_DIM ** -0.5)
            tot = tot + jnp.maximum(sc, 0.0) * wi_cols[h]
        tot = tot * (IDX_HEADS ** -0.5)
        tot = jnp.where(tot == 0.0, 0.0, tot)
        bits = lax.bitcast_convert_type(tot, i32)
        key = jnp.where(bits < 0, bits ^ jnp.int32(0x7FFFFFFF), bits)
        sp = start + lax.broadcasted_iota(i32, (1, tk), 1)
        key_ref[kt] = jnp.where(sp <= t_col, key, INT_MIN)
        return carry

    lax.fori_loop(0, n_sel_tiles, index_tile, 0)

    def count_ge(trial):
        def body(kt, acc):
            c = (key_ref[kt] >= trial).astype(i32)
            for j in range(tk // LANES):
                acc = acc + c[:, j * LANES:(j + 1) * LANES]
            return acc
        acc = lax.fori_loop(0, n_sel_tiles, body, jnp.zeros((tq, LANES), i32))
        return jnp.sum(acc, axis=-1, keepdims=True)

    c_pos = count_ge(jnp.zeros((tq, 1), i32))
    v0 = jnp.where(c_pos >= topk, 0, INT_MIN).astype(i32)
    c0 = jnp.where(c_pos >= topk, c_pos, n_sel_tiles * tk)

    def bit_step(i, carry):
        v, cge = carry
        trial = v | lax.shift_left(jnp.int32(1), 30 - i)
        c = count_ge(trial)
        ok = c >= topk
        return jnp.where(ok, trial, v), jnp.where(ok, c, cge)

    v_thr, c_ge = lax.fori_loop(0, 31, bit_step, (v0, c0))
    v_eff = jnp.maximum(v_thr, INT_MIN + 1)

    def write_plain(kt, carry):
        dbias_ref[kt] = jnp.where(key_ref[kt] >= v_eff, 0.0, NEG_BIG)
        return carry

    def plain_path():
        lax.fori_loop(0, n_sel_tiles, write_plain, 0)

    def tie_path():
        need = topk - count_ge(v_thr + 1)

        def count_tie_below(limit):
            def body(kt, acc):
                sp = kt * tk + lax.broadcasted_iota(i32, (1, tk), 1)
                c = ((key_ref[kt] == v_thr) & (sp < limit)).astype(i32)
                for j in range(tk // LANES):
                    acc = acc + c[:, j * LANES:(j + 1) * LANES]
                return acc
            acc = lax.fori_loop(0, n_sel_tiles, body, jnp.zeros((tq, LANES), i32))
            return jnp.sum(acc, axis=-1, keepdims=True)

        n_bits = max(1, (t_len - 1).bit_length())

        def idx_step(i, w):
            trial = w | lax.shift_left(jnp.int32(1), n_bits - 1 - i)
            return jnp.where(count_tie_below(trial) < need, trial, w)

        w_last = lax.fori_loop(0, n_bits, idx_step, jnp.zeros((tq, 1), i32))

        def write_tie(kt, carry):
            sp = kt * tk + lax.broadcasted_iota(i32, (1, tk), 1)
            key = key_ref[kt]
            take = (key > v_thr) | ((key == v_thr) & (sp <= w_last))
            dbias_ref[kt] = jnp.where(take & (key >= v_eff), 0.0, NEG_BIG)
            return carry

        lax.fori_loop(0, n_sel_tiles, write_tie, 0)

    lax.cond(jnp.max(c_ge) > topk, tie_path, plain_path)

    def dsa_bias_fn(kt):
        return dbias_ref[kt]

    out_n, out_d = [], []
    for g in range(KV_GROUPS):
        qn4 = jnp.concatenate(head_cols(O_Q, g), axis=0)
        qd4 = jnp.concatenate(head_cols(O_Q + NSA_HEADS * HEAD_DIM, g), axis=0)
        sl4 = slope4(g)
        o_s = _flash_tiles(qn4, kvb_ref, 2 * LANES + g * HEAD_DIM, 3 * LANES + g * HEAD_DIM, tk,
                           0, n_sel_tiles, sl4, sel_bias_fn(g))
        o_w = _flash_tiles(qn4, kvb_ref, 6 * LANES + g * HEAD_DIM, 7 * LANES + g * HEAD_DIM, tkw,
                           win_lo, win_hi, sl4, win_bias_fn)
        o_d = _flash_tiles(qd4, kvb_ref, 4 * LANES + g * HEAD_DIM, 5 * LANES + g * HEAD_DIM, tk,
                           0, n_sel_tiles, sl4, dsa_bias_fn)
        o_n = gate4(g, 0) * o_cmp[g] + gate4(g, 1) * o_s + gate4(g, 2) * o_w
        out_n.append(unstack(o_n))
        out_d.append(unstack(o_d))
    o_ref[0] = jnp.concatenate(out_n + out_d, axis=1).astype(o_ref.dtype)


def _odd_attn_prompt(zq, misc, kvb, wc2, bsz, t_len, tq=128, tk=512, tkw=128):
    tk = min(tk, t_len)
    topk = min(DSA_TOPK_MAX, t_len // 4)
    nb = t_len // BLK
    n_kt = t_len // tk
    zq3 = zq.reshape(bsz, t_len, zq.shape[1])
    misc3 = misc.reshape(bsz, t_len, LANES)
    kvb3 = kvb.reshape(bsz, t_len, kvb.shape[1])
    kern = functools.partial(_odd_attn_kernel, t_len=t_len, tq=tq, tk=tk, tkw=tkw, topk=topk)
    out = pl.pallas_call(
        kern,
        grid=(bsz, t_len // tq),
        in_specs=[pl.BlockSpec((1, tq, zq.shape[1]), lambda b, i: (b, i, 0)),
                  pl.BlockSpec((1, tq, LANES), lambda b, i: (b, i, 0)),
                  pl.BlockSpec((1, t_len, kvb.shape[1]), lambda b, i: (b, 0, 0)),
                  pl.BlockSpec((1, t_len, LANES), lambda b, i: (b, 0, 0)),
                  pl.BlockSpec((2, BLK, LANES), lambda b, i: (0, 0, 0))],
        out_specs=pl.BlockSpec((1, tq, 2 * HALF_W), lambda b, i: (b, i, 0)),
        out_shape=jax.ShapeDtypeStruct((bsz, t_len, 2 * HALF_W), jnp.bfloat16),
        scratch_shapes=[pltpu.VMEM((nb, LANES), jnp.bfloat16),
                        pltpu.VMEM((nb, LANES), jnp.bfloat16),
                        pltpu.VMEM((t_len, IDX_DIM), jnp.bfloat16),
                        pltpu.VMEM((n_kt, tq, tk), jnp.int32),
                        pltpu.VMEM((n_kt, tq, tk), jnp.float32)],
        compiler_params=pltpu.CompilerParams(
            dimension_semantics=("arbitrary", "arbitrary"), vmem_limit_bytes=VMEM_LIMIT),
        name="odd_attn_prompt",
    )(zq3, misc3, kvb3, misc3, wc2)
    return out.reshape(bsz * t_len, 2 * HALF_W)


PAGE = 128
ISC_ROWS = 24


def _odd_sample_kernel(pt_ref, qn_ref, qd_ref, qi_ref, wi_ref, gate_ref, newb_ref, newki_ref, neww_ref,
                       win_ref, wc_ref, *rest, n_pages, topk):
    del pt_ref
    f32, bf16, i32 = jnp.float32, jnp.bfloat16, jnp.int32
    kv_refs, idx_refs = rest[:n_pages], rest[n_pages:2 * n_pages]
    o_ref, wout_ref, kc_ref, vc_ref, isc_ref = rest[2 * n_pages:]
    past = n_pages * PAGE
    nb = past // BLK + 1
    nbp = kc_ref.shape[0]
    row8 = lax.broadcasted_iota(i32, (8, 1), 0)
    lane = lax.broadcasted_iota(i32, (1, LANES), 1)
    slope8 = jnp.zeros((8, 1), f32)
    for h in range(NSA_HEADS):
        slope8 = jnp.where(row8 == h, ALIBI[h], slope8)
    low_group = row8 < NSA_HEADS // KV_GROUPS

    def blockdiag(q):
        z = jnp.zeros_like(q)
        return jnp.concatenate([jnp.where(low_group, q, z), jnp.where(low_group, z, q)], axis=1)

    def pick_half(x):
        return jnp.where(low_group, x[:, 0:HEAD_DIM], x[:, HEAD_DIM:2 * HEAD_DIM])

    qn8, qd8 = blockdiag(qn_ref[0]), blockdiag(qd_ref[0])
    qi8, wi8 = qi_ref[0], wi_ref[0]
    newb = newb_ref[0]
    gates = jax.nn.sigmoid(gate_ref[0])
    wk128 = jnp.concatenate([wc_ref[0], wc_ref[0]], axis=0)
    wv128 = jnp.concatenate([wc_ref[1], wc_ref[1]], axis=0)

    def new_score(q8, k_row):
        return jnp.sum(q8.astype(f32) * k_row.astype(f32), axis=-1, keepdims=True) + slope8 * float(past)

    s_sel, s_dsa = [], []
    for p in range(n_pages):
        kv = kv_refs[p]
        kc_ref[2 * p:2 * p + 2, :] = jnp.sum((kv[0, :, 0:LANES] * wk128).reshape(2, BLK, LANES), axis=1)
        vc_ref[2 * p:2 * p + 2, :] = jnp.sum((kv[0, :, LANES:2 * LANES] * wv128).reshape(2, BLK, LANES), axis=1)
        pos = (p * PAGE + lane).astype(f32)
        s_sel.append(_dot_nt(qn8, kv[0, :, 2 * LANES:3 * LANES].astype(bf16)) + slope8 * pos)
        s_dsa.append(_dot_nt(qd8, kv[0, :, 4 * LANES:5 * LANES].astype(bf16)) + slope8 * pos)
        ii = _dot_nt(qi8, idx_refs[p][0].astype(bf16)) * (IDX_DIM ** -0.5)
        isc_ref[p:p + 1, :] = jnp.sum(jnp.maximum(ii, 0.0) * wi8, axis=0, keepdims=True) * (IDX_HEADS ** -0.5)
    kc_ref[nb - 1:nb, :] = newb[:, 0:LANES].astype(f32) * wc_ref[0, 0:1, :]
    vc_ref[nb - 1:nb, :] = newb[:, LANES:2 * LANES].astype(f32) * wc_ref[1, 0:1, :]
    kc_ref[nb:nbp, :] = jnp.zeros((nbp - nb, LANES), f32)
    vc_ref[nb:nbp, :] = jnp.zeros((nbp - nb, LANES), f32)
    ii_new = jnp.sum(qi8.astype(f32) * newki_ref[0].astype(bf16).astype(f32), axis=-1, keepdims=True) * (IDX_DIM ** -0.5)
    isc_new = jnp.sum(jnp.maximum(ii_new, 0.0) * wi8, axis=0, keepdims=True) * (IDX_HEADS ** -0.5)
    isc_ref[n_pages:n_pages + 1, :] = jnp.where(lane == 0, isc_new, -jnp.inf)
    isc_ref[n_pages + 1:ISC_ROWS, :] = jnp.full((ISC_ROWS - n_pages - 1, LANES), -jnp.inf, f32)

    jb = lax.broadcasted_iota(i32, (1, nbp), 1)
    dist_c = past - (jb * BLK + (BLK - 1))
    mask_c = (dist_c >= 0) & (jb < nb)
    s_c = _dot_nt(qn8, kc_ref[...].astype(bf16)) - slope8 * dist_c.astype(f32)
    s_c = jnp.where(mask_c, s_c, NEG_BIG)
    m_c = jnp.max(s_c, axis=-1, keepdims=True)
    e_c = jnp.where(mask_c, jnp.exp(s_c - m_c), 0.0)
    p_c = e_c * (1.0 / jnp.maximum(jnp.sum(e_c, axis=-1, keepdims=True), 1e-30))
    o_c = pick_half(jnp.dot(p_c.astype(bf16), vc_ref[...].astype(bf16), preferred_element_type=f32))
    imp_lo = jnp.sum(jnp.where(low_group, p_c, 0.0), axis=0, keepdims=True)
    imp_hi = jnp.sum(jnp.where(low_group, 0.0, p_c), axis=0, keepdims=True)
    imp = jnp.where(low_group, imp_lo, imp_hi)
    cur = past // BLK
    imp = jnp.where((jb == 0) | (jb == cur) | (jb == cur - 1), FORCE_SCORE, imp)
    imp = jnp.where(jb <= cur, imp, -jnp.inf)
    jb8 = lax.broadcasted_iota(i32, (8, nbp), 1)
    sel = jnp.zeros((8, nbp), f32)
    for _ in range(min(NSA_TOPN, nb)):
        m = jnp.max(imp, axis=-1, keepdims=True)
        first = jnp.min(jnp.where(imp == m, jb8, nbp), axis=-1, keepdims=True)
        hit = jb8 == first
        sel = jnp.where(hit & (m > -jnp.inf), 1.0, sel)
        imp = jnp.where(hit, -jnp.inf, imp)

    tot = isc_ref[...]
    tot = jnp.where(tot == 0.0, 0.0, tot)
    bits = lax.bitcast_convert_type(tot, i32)
    key = jnp.where(bits < 0, bits ^ jnp.int32(0x7FFFFFFF), bits)
    krow = lax.broadcasted_iota(i32, (ISC_ROWS, LANES), 0)
    kidx = krow * LANES + lax.broadcasted_iota(i32, (ISC_ROWS, LANES), 1)
    key = jnp.where(kidx <= past, key, INT_MIN)

    def total(x):
        return jnp.sum(jnp.sum(x, axis=1, keepdims=True), axis=0, keepdims=True)

    def count_ge(trial):
        return total((key >= trial).astype(i32))

    c_pos = count_ge(jnp.zeros((1, 1), i32))
    v_thr = jnp.where(c_pos >= topk, 0, INT_MIN).astype(i32)
    c_ge = jnp.where(c_pos >= topk, c_pos, ISC_ROWS * LANES)
    for bit in range(30, -1, -1):
        trial = v_thr | jnp.int32(1 << bit)
        c = count_ge(trial)
        ok = c >= topk
        v_thr = jnp.where(ok, trial, v_thr)
        c_ge = jnp.where(ok, c, c_ge)
    v_eff = jnp.maximum(v_thr, INT_MIN + 1)

    def tie_mask():
        need = topk - count_ge(v_thr + 1)
        tie = key == v_thr
        w_last = jnp.zeros((1, 1), i32)
        for bit in range((ISC_ROWS * LANES - 1).bit_length() - 1, -1, -1):
            trial = w_last | jnp.int32(1 << bit)
            below = total((tie & (kidx < trial)).astype(i32))
            w_last = jnp.where(below < need, trial, w_last)
        take = (key > v_thr) | (tie & (kidx <= w_last))
        return jnp.where(take & (key >= v_eff), 1.0, 0.0)

    def plain_mask():
        return jnp.where(key >= v_eff, 1.0, 0.0)

    dsel = lax.cond(c_ge[0, 0] > topk, tie_mask, plain_mask)

    def attend(tiles, masks, s_new, new_ok, v_tile, v_new):
        tiles = [jnp.where(mk, t, NEG_BIG) for t, mk in zip(tiles, masks)]
        s_new = jnp.where(new_ok, s_new, NEG_BIG)
        m = s_new
        for t in tiles:
            m = jnp.maximum(m, jnp.max(t, axis=-1, keepdims=True))
        l = jnp.exp(s_new - m)
        acc = l * v_new.astype(f32)
        for j, t in enumerate(tiles):
            e = jnp.exp(t - m)
            l = l + jnp.sum(e, axis=-1, keepdims=True)
            acc = acc + jnp.dot(e.astype(bf16), v_tile(j), preferred_element_type=f32)
        return pick_half(acc * (1.0 / l))

    sel_masks = [jnp.where(lane < BLK, sel[:, 2 * p:2 * p + 1], sel[:, 2 * p + 1:2 * p + 2]) > 0.0
                 for p in range(n_pages)]
    o_s = attend(s_sel, sel_masks, new_score(qn8, newb[:, 2 * LANES:3 * LANES]), sel[:, nb - 1:nb] > 0.0,
                 lambda p: kv_refs[p][0, :, 3 * LANES:4 * LANES].astype(bf16), newb[:, 3 * LANES:4 * LANES])
    dsa_masks = [dsel[p:p + 1, :] > 0.0 for p in range(n_pages)]
    o_d = attend(s_dsa, dsa_masks, new_score(qd8, newb[:, 4 * LANES:5 * LANES]), dsel[n_pages:n_pages + 1, 0:1] > 0.0,
                 lambda p: kv_refs[p][0, :, 5 * LANES:6 * LANES].astype(bf16), newb[:, 5 * LANES:6 * LANES])
    win = win_ref[0]
    n_win = win.shape[0]
    wlane = lax.broadcasted_iota(i32, (1, n_win), 1)
    wpos = past - n_win + wlane
    s_w = _dot_nt(qn8, win[:, 0:LANES].astype(bf16)) + slope8 * wpos.astype(f32)
    w_ok = (past - wpos < WINDOW) & (wpos >= 0)
    o_w = attend([s_w], [w_ok], new_score(qn8, newb[:, 6 * LANES:7 * LANES]), True,
                 lambda p: win[:, LANES:2 * LANES].astype(bf16), newb[:, 7 * LANES:8 * LANES])
    o_ref[0, 0:NSA_HEADS, :] = gates[:, 0:1] * o_c + gates[:, 1:2] * o_s + gates[:, 2:3] * o_w
    o_ref[0, NSA_HEADS:2 * NSA_HEADS, :] = o_d
    wrow = lax.broadcasted_iota(i32, (n_win, 1), 0)
    wout_ref[0] = jnp.where(wrow == n_win - 1, neww_ref[0], pltpu.roll(win, n_win - 1, 0))


def _odd_attn_sample(zq, misc, kvb, win_new, cache_kv, cache_idx, cache_win, page_table, wc2):
    f32, bf16 = jnp.float32, jnp.bfloat16
    s_n, n_pages = page_table.shape
    n_pool = cache_kv.shape[0]
    n_win = cache_win.shape[1]
    past = n_pages * PAGE
    topk = min(DSA_TOPK_MAX, (past + 1) // 4)
    nbp = _round_up(past // BLK + 1, 8)
    qn = zq[:, 0:HALF_W].reshape(s_n, NSA_HEADS, HEAD_DIM)
    qd = zq[:, HALF_W:2 * HALF_W].reshape(s_n, DSA_HEADS, HEAD_DIM)
    qi = jnp.pad(zq[:, O_KV:O_KV + LANES].reshape(s_n, IDX_HEADS, IDX_DIM), ((0, 0), (0, 8 - IDX_HEADS), (0, 0)))
    wi = jnp.pad(misc[:, MISC_WI:MISC_WI + IDX_HEADS].reshape(s_n, IDX_HEADS, 1), ((0, 0), (0, 8 - IDX_HEADS), (0, 0)))
    gts = misc[:, MISC_GATE:MISC_GATE + 3 * NSA_HEADS].reshape(s_n, NSA_HEADS, 3)
    kv_pages = cache_kv.reshape(n_pool, PAGE, KV_SLOTS * LANES)
    per_seq = lambda shape: pl.BlockSpec((1,) + shape, lambda b, pt: (b, 0, 0))
    page_spec = lambda width, p: pl.BlockSpec((1, PAGE, width), lambda b, pt, p=p: (pt[b, p], 0, 0))
    in_specs = ([per_seq((NSA_HEADS, HEAD_DIM)), per_seq((DSA_HEADS, HEAD_DIM)), per_seq((8, IDX_DIM)),
                 per_seq((8, 1)), per_seq((NSA_HEADS, 3)), per_seq((1, 8 * LANES)), per_seq((1, IDX_DIM)),
                 per_seq((1, 2 * LANES)), per_seq((n_win, 2 * LANES)),
                 pl.BlockSpec((2, BLK, LANES), lambda b, pt: (0, 0, 0))]
                + [page_spec(KV_SLOTS * LANES, p) for p in range(n_pages)]
                + [page_spec(IDX_DIM, p) for p in range(n_pages)])
    kern = functools.partial(_odd_sample_kernel, n_pages=n_pages, topk=topk)
    o, win_out = pl.pallas_call(
        kern,
        grid_spec=pltpu.PrefetchScalarGridSpec(
            num_scalar_prefetch=1,
            grid=(s_n,),
            in_specs=in_specs,
            out_specs=[per_seq((2 * NSA_HEADS, HEAD_DIM)), per_seq((n_win, 2 * LANES))],
            scratch_shapes=[pltpu.VMEM((nbp, LANES), f32), pltpu.VMEM((nbp, LANES), f32),
                            pltpu.VMEM((ISC_ROWS, LANES), f32)]),
        out_shape=[jax.ShapeDtypeStruct((s_n, 2 * NSA_HEADS, HEAD_DIM), f32),
                   jax.ShapeDtypeStruct((s_n, n_win, 2 * LANES), f32)],
        compiler_params=pltpu.CompilerParams(
            dimension_semantics=("arbitrary",), vmem_limit_bytes=VMEM_LIMIT),
        name="odd_attn_sample",
    )(page_table, qn, qd, qi, wi, gts, kvb.reshape(s_n, 1, 8 * LANES), misc[:, 0:IDX_DIM].reshape(s_n, 1, IDX_DIM),
      win_new.reshape(s_n, 1, 2 * LANES), cache_win.reshape(s_n, n_win, 2 * LANES), wc2,
      *([kv_pages] * n_pages), *([cache_idx] * n_pages))
    return o.reshape(s_n, 2 * HALF_W), win_out


def _split(z, sizes):
    cuts = [int(c) for c in np.cumsum(sizes)[:-1]]
    return jnp.split(z, cuts, axis=-1)


def _alibi_slopes(n):
    return jnp.asarray(np.power(2.0, -8.0 * np.arange(1, n + 1) / n), dtype=jnp.float32)


def _masked_softmax(s, mask):
    s = jnp.where(mask, s.astype(jnp.float32), -jnp.inf)
    m = jnp.max(s, axis=-1, keepdims=True)
    e = jnp.exp(s - jnp.where(jnp.isfinite(m), m, 0.0))
    return e / jnp.maximum(jnp.sum(e, axis=-1, keepdims=True), 1e-30)


def _short_conv(u_ext, w, b):
    t = u_ext.shape[1] - (CONV_W - 1)
    return b + sum(w[j] * u_ext[:, j:j + t] for j in range(CONV_W))


def _mlstm_chunk(carry, inp):
    c_st, n_st, m_st = carry
    q, k, v, ig, lf = inp
    L = q.shape[2]
    b = jnp.cumsum(lf, axis=-1)
    causal = jnp.tril(jnp.ones((L, L), dtype=bool))
    dmat = jnp.where(causal, b[..., :, None] - b[..., None, :] + ig[..., None, :], -jnp.inf)
    inter = b + m_st[..., None]
    m_t = jnp.maximum(inter, jnp.max(dmat, axis=-1))
    w_intra = jnp.exp(dmat - m_t[..., None])
    w_inter = jnp.exp(inter - m_t)
    s = jnp.einsum('bhtd,bhsd->bhts', q, k) * w_intra
    num = jnp.einsum('bhts,bhsd->bhtd', s, v) + w_inter[..., None] * jnp.einsum('bhed,bhtd->bhte', c_st, q)
    den = jnp.sum(s, axis=-1) + w_inter * jnp.einsum('bhd,bhtd->bht', n_st, q)
    h = num / jnp.maximum(jnp.abs(den), jnp.exp(-m_t))[..., None]
    m_new = m_t[..., -1]
    w_state = jnp.exp(b[..., -1:] - b + ig - m_new[..., None])
    decay = jnp.exp(b[..., -1] + m_st - m_new)
    c_new = decay[..., None, None] * c_st + jnp.einsum('bhs,bhse,bhsd->bhed', w_state, v, k)
    n_new = decay[..., None] * n_st + jnp.einsum('bhs,bhsd->bhd', w_state, k)
    return (c_new, n_new, m_new), h


def _even_mixer(z, gate_b, conv_w, conv_b, conv_prev, ml_state, chunk):
    f32 = jnp.float32
    bn, t, _ = z.shape
    h, bg, cg, q, k, v, og, ig, fg = _split(z, E_SIZES)
    u = cg * h
    u_ext = jnp.concatenate([conv_prev.astype(u.dtype), u], axis=1)
    y_a = bg * _short_conv(u_ext, conv_w, conv_b)
    new_conv = u_ext[:, u_ext.shape[1] - (CONV_W - 1):]
    heads = lambda a: a.reshape(bn, t, ML_HEADS, ML_DIM).astype(f32)
    q, k, v = heads(q), heads(k) * (ML_DIM ** -0.5), heads(v)
    ig = (ig + gate_b[:ML_HEADS]).astype(f32)
    lf = jax.nn.log_sigmoid((fg + gate_b[ML_HEADS:]).astype(f32))
    nc = t // chunk

    def to_chunks(a):
        a = jnp.moveaxis(a.reshape((bn, nc, chunk) + a.shape[2:]), 1, 0)
        return jnp.swapaxes(a, 2, 3)

    carry = tuple(s.astype(f32) for s in ml_state)
    carry, hs = lax.scan(_mlstm_chunk, carry, tuple(to_chunks(a) for a in (q, k, v, ig, lf)))
    hs = jnp.swapaxes(jnp.moveaxis(hs, 0, 1), 2, 3).reshape(bn, t, HALF_W)
    y_b = jax.nn.sigmoid(og) * hs
    return jnp.concatenate([y_a, y_b], axis=-1), new_conv, carry


def _odd_attend(qn, gates, qd, qi, wi, q_pos, kc, vc, slc, dsa_kv, kidx, kw, kw_pos, topn, topk):
    f32 = jnp.float32
    bn, tq, g, r, _ = qn.shape
    nb = kc.shape[1]
    L = kidx.shape[1]
    scale = HEAD_DIM ** -0.5
    m_n = _alibi_slopes(NSA_HEADS).reshape(g, r)
    m_d = _alibi_slopes(DSA_HEADS).reshape(KV_GROUPS, DSA_HEADS // KV_GROUPS)
    jb = jnp.arange(nb)
    dist_c = q_pos[:, None] - (jb * BLK + BLK - 1)[None, :]
    s_c = jnp.einsum('btgrd,bjgd->btgrj', qn, kc).astype(f32) * scale - m_n[:, :, None] * dist_c[:, None, None, :].astype(f32)
    p_c = _masked_softmax(s_c, (dist_c >= 0)[:, None, None, :])
    o_c = jnp.einsum('btgrj,bjgd->btgrd', p_c.astype(vc.dtype), vc)
    cur = q_pos // BLK
    imp = jnp.sum(p_c, axis=3)
    forced = (jb[None, :] == 0) | (jb[None, :] == cur[:, None]) | (jb[None, :] == cur[:, None] - 1)
    imp = jnp.where(forced[:, None, :], FORCE_SCORE, imp)
    imp = jnp.where((jb[None, :] <= cur[:, None])[:, None, :], imp, -jnp.inf)
    top_val, top_idx = lax.top_k(imp, topn)
    b_ix = jnp.arange(bn)[:, None, None, None]
    g_ix = jnp.arange(g)[None, None, :, None]
    sel = slc[b_ix, g_ix, top_idx]
    pos = top_idx[..., None] * BLK + jnp.arange(BLK)
    dist_s = q_pos[None, :, None, None, None] - pos
    mask_s = (dist_s >= 0) & jnp.isfinite(top_val)[..., None]
    s_s = jnp.einsum('btgrd,btgnpd->btgrnp', qn, sel[..., 0, :]).astype(f32) * scale - m_n[None, None, :, :, None, None] * dist_s[:, :, :, None].astype(f32)
    shp = s_s.shape
    p_s = _masked_softmax(s_s.reshape(shp[:4] + (-1,)), mask_s[:, :, :, None].reshape(bn, tq, g, 1, -1)).reshape(shp)
    o_s = jnp.einsum('btgrnp,btgnpd->btgrd', p_s.astype(sel.dtype), sel[..., 1, :])
    dist_w = q_pos[:, None] - kw_pos[None, :]
    mask_w = (dist_w >= 0) & (dist_w < WINDOW) & (kw_pos >= 0)[None, :]
    s_w = jnp.einsum('btgrd,blgd->btgrl', qn, kw[:, :, 0]).astype(f32) * scale - m_n[:, :, None] * dist_w[:, None, None, :].astype(f32)
    p_w = _masked_softmax(s_w, mask_w[:, None, None, :])
    o_w = jnp.einsum('btgrl,blgd->btgrd', p_w.astype(kw.dtype), kw[:, :, 1])
    o_n = (gates[..., 0:1] * o_c + gates[..., 1:2] * o_s + gates[..., 2:3] * o_w).astype(qn.dtype).reshape(bn, tq, -1)
    k_pos = jnp.arange(L)
    isc = jnp.einsum('bthe,ble->bthl', qi, kidx).astype(f32) * (IDX_DIM ** -0.5)
    isc = jnp.einsum('bthl,bth->btl', jax.nn.relu(isc), wi.astype(f32)) * (IDX_HEADS ** -0.5)
    isc = jnp.where((k_pos[None, :] <= q_pos[:, None])[None], isc, -jnp.inf)
    i_val, i_idx = lax.top_k(isc, topk)
    sel_d = dsa_kv[jnp.arange(bn)[:, None, None], i_idx]
    dist_d = (q_pos[None, :, None] - i_idx).astype(f32)
    s_d = jnp.einsum('btgrd,btkgd->btgrk', qd, sel_d[:, :, :, 0]).astype(f32) * scale - m_d[None, None, :, :, None] * dist_d[:, :, None, None, :]
    p_d = _masked_softmax(s_d, jnp.isfinite(i_val)[:, :, None, None, :])
    o_d = jnp.einsum('btgrk,btkgd->btgrd', p_d.astype(sel_d.dtype), sel_d[:, :, :, 1]).reshape(bn, tq, -1)
    return jnp.concatenate([o_n, o_d.astype(qn.dtype)], axis=-1)


def _odd_mixer(z, w_cmp, past_kv, past_idx, past_win, keep):
    bn, t, _ = z.shape
    p_len = past_kv.shape[1]
    wp = past_win.shape[1]
    g, r = KV_GROUPS, NSA_HEADS // KV_GROUPS
    qn, kvc, kvs, kvw, gates, qd, kvd, qi, ki, wi = _split(z, O_SIZES)
    qn = qn.reshape(bn, t, g, r, HEAD_DIM)
    qd = qd.reshape(bn, t, g, DSA_HEADS // g, HEAD_DIM)
    gates = jax.nn.sigmoid(gates.astype(jnp.float32)).reshape(bn, t, g, r, 3)
    qi = qi.reshape(bn, t, IDX_HEADS, IDX_DIM)
    kv_new = jnp.concatenate([kvc, kvs, kvd], axis=-1).reshape(bn, t, KV_SLOTS, g, HEAD_DIM)
    kvw_new = kvw.reshape(bn, t, 2, g, HEAD_DIM)
    kv = jnp.concatenate([past_kv.astype(kv_new.dtype), kv_new], axis=1)
    kidx = jnp.concatenate([past_idx.astype(ki.dtype), ki], axis=1)
    win_all = jnp.concatenate([past_win.astype(kvw_new.dtype), kvw_new], axis=1)
    L = p_len + t
    nb = -(-L // BLK)
    kvb = jnp.pad(kv[:, :, :4], ((0, 0), (0, nb * BLK - L), (0, 0), (0, 0), (0, 0)))
    kvb = kvb.reshape(bn, nb, BLK, 4, g, HEAD_DIM)
    kc = jnp.einsum('bjpgd,pd->bjgd', kvb[:, :, :, 0], w_cmp[0])
    vc = jnp.einsum('bjpgd,pd->bjgd', kvb[:, :, :, 1], w_cmp[1])
    slc = jnp.transpose(kvb[:, :, :, 2:4], (0, 4, 1, 2, 3, 5))
    dsa_kv = kv[:, :, 4:6]
    topn = min(NSA_TOPN, nb)
    topk = min(DSA_TOPK_MAX, L // 4)
    qb = t if t <= Q_BLOCK else Q_BLOCK
    nqb = t // qb

    def blocks(a):
        return jnp.moveaxis(a.reshape((bn, nqb, qb) + a.shape[2:]), 1, 0)

    def body(args):
        qn_b, g_b, qd_b, qi_b, wi_b, start = args
        q_pos = p_len + start + jnp.arange(qb)
        kw = lax.dynamic_slice_in_dim(win_all, start, wp + qb, axis=1)
        kw_pos = p_len - wp + start + jnp.arange(wp + qb)
        return _odd_attend(qn_b, g_b, qd_b, qi_b, wi_b, q_pos, kc, vc, slc, dsa_kv, kidx, kw, kw_pos, topn, topk)

    starts = jnp.arange(nqb, dtype=jnp.int32) * qb
    o = lax.map(body, (blocks(qn), blocks(gates), blocks(qd), blocks(qi), blocks(wi), starts))
    o = jnp.moveaxis(o, 0, 1).reshape(bn, t, 2 * HALF_W)
    new_win = win_all[:, win_all.shape[1] - keep:]
    return o, kv_new, ki, new_win


def _pad_cols(w, n):
    return jnp.pad(w, ((0, 0), (0, n - w.shape[1])))


def kernel(x_prompt, x_sample, state_conv, state_C, state_n, state_m, cache_kv, cache_idx, cache_win, page_table,
           w_in_e, gate_b_e, conv_w, conv_b, w_out_e, w_in_o, w_cmp, w_out_o, w_up, w_down, ln_g, ln_b):
    f32, bf16 = jnp.float32, jnp.bfloat16
    bp, tp, d = x_prompt.shape
    dbs = x_sample.shape[0]
    past = page_table.shape[1] * cache_kv.shape[2]
    keep = cache_win.shape[2]
    yp = x_prompt.reshape(bp * tp, d)
    ys = x_sample.reshape(dbs, d)
    tm_p, tm_s = 512, dbs
    outs = {}
    for layer in range(DEPTH):
        i = layer // 2
        if layer % 2 == 0:
            n_pad = _round_up(E_COLS, LANES)
            w_in = _pad_cols(w_in_e[i], n_pad).astype(bf16)
            zs = _proj(ys, w_in, tm_s)[:, :E_COLS].reshape(dbs, 1, E_COLS)
            ya, qkv, og, gts, tails = _even_proj(yp, w_in, conv_w[i], conv_b[i], tm_p, tp)
            yb, c_fin, n_fin, m_fin = _mlstm_prompt(qkv, og, gts, gate_b_e[i], bp, tp)
            cvp = tails.reshape(bp, tp // tm_p, CONV_TAIL, HALF_W)[:, -1, CONV_TAIL - (CONV_W - 1):]
            ms, cvs, sts = _even_mixer(zs, gate_b_e[i], conv_w[i], conv_b[i], state_conv[i],
                                       (state_C[i], state_n[i], state_m[i]), 1)
            outs['conv'] = (cvp[None], cvs[None])
            outs['c'] = (c_fin[None], sts[0][None])
            outs['n'] = (n_fin[None], sts[1][None])
            outs['m'] = (m_fin[:, :, 0][None], sts[2][None])
            w_out = w_out_e[i].astype(bf16)
        else:
            n_pad = _round_up(O_COLS, LANES)
            w_in = _pad_cols(w_in_o[i], n_pad).astype(bf16)
            w_packed = _pack_w_in_o(w_in_o[i]).astype(bf16)
            wc2 = jnp.concatenate([w_cmp[i], w_cmp[i]], axis=-1)
            zq, kvp, kvb, winp, miscp = _odd_proj(yp, w_packed, tm_p)
            mp = _odd_attn_prompt(zq, miscp, kvb, wc2, bp, tp)
            kvp = kvp.reshape(bp, tp, KV_SLOTS, KV_GROUPS, HEAD_DIM)
            ixp = miscp[:, :IDX_DIM].reshape(bp, tp, IDX_DIM)
            wnp = winp.reshape(bp, tp, 2, KV_GROUPS, HEAD_DIM)[:, tp - keep:]
            zq_s, kvs, kvb_s, win_s, misc_s = _odd_proj(ys, w_packed, tm_s)
            ms, wns = _odd_attn_sample(zq_s, misc_s, kvb_s, win_s, cache_kv[i], cache_idx[i],
                                       cache_win[i].reshape(dbs, keep, 2 * LANES), page_table, wc2)
            kvs = kvs.reshape(dbs, 1, KV_SLOTS, KV_GROUPS, HEAD_DIM)
            ixs = misc_s[:, :IDX_DIM].reshape(dbs, 1, IDX_DIM)
            wns = wns.reshape(dbs, keep, 2, KV_GROUPS, HEAD_DIM)
            outs['kv'] = (kvp[None], kvs[None])
            outs['idx'] = (ixp[None], ixs[None])
            outs['win'] = (wnp[None], wns[None])
            w_out = w_out_o[i].astype(bf16)
        if layer % 2 == 0:
            yp = _outproj2_ln(ya, yb.reshape(bp * tp, HALF_W), w_out, yp, ln_g[layer, 0], ln_b[layer, 0], tm_p)
        else:
            yp = _outproj_ln(mp, w_out, yp, ln_g[layer, 0], ln_b[layer, 0], tm_p)
        ys = _outproj_ln(ms.reshape(dbs, d), w_out, ys, ln_g[layer, 0], ln_b[layer, 0], tm_s)
        wu, wd = w_up[layer].astype(bf16), w_down[layer].astype(bf16)
        yp = _mlp_ln(yp, wu, wd, ln_g[layer, 1], ln_b[layer, 1], tm_p, 1024)
        ys = _mlp_ln(ys, wu, wd, ln_g[layer, 1], ln_b[layer, 1], tm_s, 1024)
    return (yp.reshape(bp, tp, d), ys.reshape(dbs, 1, d),
            outs['conv'][0], outs['conv'][1], outs['c'][0], outs['c'][1],
            outs['n'][0], outs['n'][1], outs['m'][0], outs['m'][1],
            outs['kv'][0], outs['kv'][1], outs['idx'][0], outs['idx'][1],
            outs['win'][0], outs['win'][1])
```

```python
import functools

import jax
import jax.numpy as jnp
import numpy as np
from jax import lax
from jax.experimental import pallas as pl
from jax.experimental.pallas import tpu as pltpu

D_MODEL = 1024
DEPTH = 2
HALF_W = 512
D_FF = 4096
CONV_W = 3
ML_HEADS = 4
ML_DIM = 128
ML_CHUNK = 64
HEAD_DIM = 64
KV_GROUPS = 2
NSA_HEADS = 8
DSA_HEADS = 8
BLK = 64
NSA_TOPN = 8
WINDOW = 256
IDX_HEADS = 4
IDX_DIM = 32
DSA_TOPK_MAX = 256
Q_BLOCK = 128
KV_SLOTS = 6
FORCE_SCORE = 1e4
ALPHA = (2.0 * DEPTH) ** 0.25
LN_EPS = 1e-5
E_SIZES = (512, 512, 512, 512, 512, 512, 512, 4, 4)
O_SIZES = (512, 256, 256, 256, 24, 512, 256, 128, 32, 4)
E_COLS = sum(E_SIZES)
O_COLS = sum(O_SIZES)
LANES = 128
VMEM_LIMIT = 48 * 1024 * 1024


def _round_up(n, m):
    return -(-n // m) * m


def _proj_kernel(x_ref, w_ref, o_ref):
    o_ref[...] = jnp.dot(x_ref[...].astype(jnp.bfloat16), w_ref[...],
                         preferred_element_type=jnp.float32)


def _proj(x, w_bf16, tm):
    m, k = x.shape
    n = w_bf16.shape[1]
    tn = n
    for cand in (1024, 768, 512, 256, 128):
        if n % cand == 0:
            tn = cand
            break
    return pl.pallas_call(
        _proj_kernel,
        grid=(n // tn, m // tm),
        in_specs=[pl.BlockSpec((tm, k), lambda j, i: (i, 0)),
                  pl.BlockSpec((k, tn), lambda j, i: (0, j))],
        out_specs=pl.BlockSpec((tm, tn), lambda j, i: (i, j)),
        out_shape=jax.ShapeDtypeStruct((m, n), jnp.float32),
        compiler_params=pltpu.CompilerParams(
            dimension_semantics=("arbitrary", "arbitrary"), vmem_limit_bytes=VMEM_LIMIT),
        name="proj",
    )(x, w_bf16)


def _layer_norm_rows(v, g, b):
    mu = jnp.mean(v, axis=-1, keepdims=True)
    d = v - mu
    var = jnp.mean(d * d, axis=-1, keepdims=True)
    return d * lax.rsqrt(var + LN_EPS) * g + b


def _outproj_ln_kernel(y_ref, w_ref, x_ref, g_ref, b_ref, o_ref):
    mix = jnp.dot(y_ref[...].astype(jnp.bfloat16), w_ref[...], preferred_element_type=jnp.float32)
    o_ref[...] = _layer_norm_rows(ALPHA * x_ref[...] + mix, g_ref[...], b_ref[...])


def _outproj_ln(y, w_bf16, x, g, b, tm):
    m, d = x.shape
    k = y.shape[1]
    return pl.pallas_call(
        _outproj_ln_kernel,
        grid=(m // tm,),
        in_specs=[pl.BlockSpec((tm, k), lambda i: (i, 0)),
                  pl.BlockSpec((k, d), lambda i: (0, 0)),
                  pl.BlockSpec((tm, d), lambda i: (i, 0)),
                  pl.BlockSpec((1, d), lambda i: (0, 0)),
                  pl.BlockSpec((1, d), lambda i: (0, 0))],
        out_specs=pl.BlockSpec((tm, d), lambda i: (i, 0)),
        out_shape=jax.ShapeDtypeStruct((m, d), jnp.float32),
        compiler_params=pltpu.CompilerParams(
            dimension_semantics=("arbitrary",), vmem_limit_bytes=VMEM_LIMIT),
        name="outproj_ln",
    )(y, w_bf16, x, g.reshape(1, d), b.reshape(1, d))


def _mlp_ln_kernel(x_ref, wu_ref, wd_ref, g_ref, b_ref, o_ref, acc_ref):
    f = pl.program_id(1)

    @pl.when(f == 0)
    def _():
        acc_ref[...] = jnp.zeros_like(acc_ref)

    up = jnp.dot(x_ref[...].astype(jnp.bfloat16), wu_ref[...], preferred_element_type=jnp.float32)
    act = jnp.square(jnp.maximum(up, 0.0))
    acc_ref[...] += jnp.dot(act.astype(jnp.bfloat16), wd_ref[...], preferred_element_type=jnp.float32)

    @pl.when(f == pl.num_programs(1) - 1)
    def _():
        o_ref[...] = _layer_norm_rows(ALPHA * x_ref[...] + acc_ref[...], g_ref[...], b_ref[...])


def _mlp_ln(x, wu_bf16, wd_bf16, g, b, tm, tf):
    m, d = x.shape
    ff = wu_bf16.shape[1]
    return pl.pallas_call(
        _mlp_ln_kernel,
        grid=(m // tm, ff // tf),
        in_specs=[pl.BlockSpec((tm, d), lambda i, f: (i, 0)),
                  pl.BlockSpec((d, tf), lambda i, f: (0, f)),
                  pl.BlockSpec((tf, d), lambda i, f: (f, 0)),
                  pl.BlockSpec((1, d), lambda i, f: (0, 0)),
                  pl.BlockSpec((1, d), lambda i, f: (0, 0))],
        out_specs=pl.BlockSpec((tm, d), lambda i, f: (i, 0)),
        out_shape=jax.ShapeDtypeStruct((m, d), jnp.float32),
        scratch_shapes=[pltpu.VMEM((tm, d), jnp.float32)],
        compiler_params=pltpu.CompilerParams(
            dimension_semantics=("arbitrary", "arbitrary"), vmem_limit_bytes=VMEM_LIMIT),
        name="mlp_ln",
    )(x, wu_bf16, wd_bf16, g.reshape(1, d), b.reshape(1, d))


E_QKV = 3 * HALF_W
E_OG = 6 * HALF_W
E_GATE = 7 * HALF_W
E_PACKED = 7 * HALF_W + LANES
CONV_TAIL = 8


def _even_proj_kernel(x_ref, w_ref, cw_ref, cb_ref, ya_ref, qkv_ref, og_ref, gate_ref, tail_ref, carry_ref,
                      *, tiles_per_seq):
    f32, bf16 = jnp.float32, jnp.bfloat16
    tm = x_ref.shape[0]

    @pl.when(pl.program_id(0) % tiles_per_seq == 0)
    def _():
        carry_ref[...] = jnp.zeros_like(carry_ref)

    z = jnp.dot(x_ref[...].astype(bf16), w_ref[...], preferred_element_type=f32)
    u = z[:, 2 * HALF_W:3 * HALF_W] * z[:, 0:HALF_W]
    prev = carry_ref[...]
    row = lax.broadcasted_iota(jnp.int32, (tm, 1), 0)
    conv = cb_ref[...] + cw_ref[CONV_W - 1:CONV_W, :] * u
    for back in range(1, CONV_W):
        shifted = pltpu.roll(u, back, 0)
        for r in range(back):
            shifted = jnp.where(row == r, prev[CONV_TAIL - back + r:CONV_TAIL - back + r + 1, :], shifted)
        conv = conv + cw_ref[CONV_W - 1 - back:CONV_W - back, :] * shifted
    ya_ref[...] = (z[:, HALF_W:2 * HALF_W] * conv).astype(bf16)
    tail = u[tm - CONV_TAIL:]
    carry_ref[...] = tail
    tail_ref[0] = tail
    qkv_ref[:, 0:HALF_W] = z[:, E_QKV:E_QKV + HALF_W].astype(bf16)
    qkv_ref[:, HALF_W:2 * HALF_W] = (z[:, E_QKV + HALF_W:E_QKV + 2 * HALF_W] * (ML_DIM ** -0.5)).astype(bf16)
    qkv_ref[:, 2 * HALF_W:3 * HALF_W] = z[:, E_QKV + 2 * HALF_W:E_OG].astype(bf16)
    og_ref[...] = z[:, E_OG:E_GATE]
    gate_ref[...] = z[:, E_GATE:E_PACKED]


def _even_proj(x, w_bf16, conv_w, conv_b, tm, seq_len):
    m, k = x.shape
    widths = (HALF_W, 3 * HALF_W, HALF_W, LANES)
    dtypes = (jnp.bfloat16, jnp.bfloat16, jnp.float32, jnp.float32)
    kern = functools.partial(_even_proj_kernel, tiles_per_seq=seq_len // tm)
    return pl.pallas_call(
        kern,
        grid=(m // tm,),
        in_specs=[pl.BlockSpec((tm, k), lambda i: (i, 0)),
                  pl.BlockSpec((k, E_PACKED), lambda i: (0, 0)),
                  pl.BlockSpec((CONV_W, HALF_W), lambda i: (0, 0)),
                  pl.BlockSpec((1, HALF_W), lambda i: (0, 0))],
        out_specs=[pl.BlockSpec((tm, n), lambda i: (i, 0)) for n in widths]
        + [pl.BlockSpec((1, CONV_TAIL, HALF_W), lambda i: (i, 0, 0))],
        out_shape=[jax.ShapeDtypeStruct((m, n), dt) for n, dt in zip(widths, dtypes)]
        + [jax.ShapeDtypeStruct((m // tm, CONV_TAIL, HALF_W), jnp.float32)],
        scratch_shapes=[pltpu.VMEM((CONV_TAIL, HALF_W), jnp.float32)],
        compiler_params=pltpu.CompilerParams(
            dimension_semantics=("arbitrary",), vmem_limit_bytes=VMEM_LIMIT),
        name="even_proj",
    )(x, w_bf16, conv_w, conv_b.reshape(1, HALF_W))


def _mlstm_kernel(qkv_ref, og_ref, gate_ref, gb_ref, yb_ref, c_ref, n_ref, m_ref):
    f32, bf16 = jnp.float32, jnp.bfloat16
    L = qkv_ref.shape[1]

    @pl.when(pl.program_id(1) == 0)
    def _():
        c_ref[...] = jnp.zeros_like(c_ref)
        n_ref[...] = jnp.zeros_like(n_ref)
        m_ref[...] = jnp.zeros_like(m_ref)

    pre = gate_ref[0] + gb_ref[...]
    lf = jax.nn.log_sigmoid(pre)
    row = lax.broadcasted_iota(jnp.int32, (L, L), 0)
    col = lax.broadcasted_iota(jnp.int32, (L, L), 1)
    causal = col <= row
    tri = jnp.where(causal, 1.0, 0.0).astype(f32)
    b_all = jnp.dot(tri, lf, preferred_element_type=f32, precision=lax.Precision.HIGHEST)
    pre_t = pre.T
    b_t = b_all.T
    for h in range(ML_HEADS):
        q = qkv_ref[0, :, h * ML_DIM:(h + 1) * ML_DIM]
        k = qkv_ref[0, :, HALF_W + h * ML_DIM:HALF_W + (h + 1) * ML_DIM]
        v = qkv_ref[0, :, 2 * HALF_W + h * ML_DIM:2 * HALF_W + (h + 1) * ML_DIM]
        ig_col = pre[:, h:h + 1]
        b_col = b_all[:, ML_HEADS + h:ML_HEADS + h + 1]
        a_row = pre_t[h:h + 1, :] - b_t[ML_HEADS + h:ML_HEADS + h + 1, :]
        m_st = m_ref[0, h:h + 1, 0:1]
        c_st = c_ref[0, h]
        n_st = n_ref[0, h:h + 1, :]
        dmat = jnp.where(causal, b_col + a_row, -jnp.inf)
        inter = b_col + m_st
        m_t = jnp.maximum(inter, jnp.max(dmat, axis=-1, keepdims=True))
        w_intra = jnp.exp(dmat - m_t)
        w_inter = jnp.exp(inter - m_t)
        s = _dot_nt(q, k) * w_intra
        num = jnp.dot(s.astype(bf16), v, preferred_element_type=f32) + w_inter * _dot_nt(q, c_st.astype(bf16))
        den = jnp.sum(s, axis=-1, keepdims=True) + w_inter * jnp.sum(q.astype(f32) * n_st, axis=-1, keepdims=True)
        hs = num * (1.0 / jnp.maximum(jnp.abs(den), jnp.exp(-m_t)))
        m_new = m_t[L - 1:L, :]
        b_last = b_col[L - 1:L, :]
        w_state = jnp.exp(b_last - b_col + ig_col - m_new)
        decay = jnp.exp(b_last + m_st - m_new)
        vw = (v.astype(f32) * w_state).astype(bf16)
        c_ref[0, h] = decay * c_st + lax.dot_general(vw, k, (((0,), (0,)), ((), ())), preferred_element_type=f32)
        n_ref[0, h:h + 1, :] = decay * n_st + jnp.sum(k.astype(f32) * w_state, axis=0, keepdims=True)
        m_ref[0, h:h + 1, :] = jnp.broadcast_to(m_new, (1, LANES))
        og = og_ref[0, :, h * ML_DIM:(h + 1) * ML_DIM]
        yb_ref[0, :, h * ML_DIM:(h + 1) * ML_DIM] = (jax.nn.sigmoid(og) * hs).astype(bf16)


def _mlstm_prompt(qkv, og, gates, gate_b, bsz, t_len, chunk=128):
    gb = jnp.pad(gate_b, (0, LANES - gate_b.shape[0])).reshape(1, LANES)
    return pl.pallas_call(
        _mlstm_kernel,
        grid=(bsz, t_len // chunk),
        in_specs=[pl.BlockSpec((1, chunk, 3 * HALF_W), lambda b, c: (b, c, 0)),
                  pl.BlockSpec((1, chunk, HALF_W), lambda b, c: (b, c, 0)),
                  pl.BlockSpec((1, chunk, LANES), lambda b, c: (b, c, 0)),
                  pl.BlockSpec((1, LANES), lambda b, c: (0, 0))],
        out_specs=[pl.BlockSpec((1, chunk, HALF_W), lambda b, c: (b, c, 0)),
                   pl.BlockSpec((1, ML_HEADS, ML_DIM, ML_DIM), lambda b, c: (b, 0, 0, 0)),
                   pl.BlockSpec((1, ML_HEADS, ML_DIM), lambda b, c: (b, 0, 0)),
                   pl.BlockSpec((1, ML_HEADS, LANES), lambda b, c: (b, 0, 0))],
        out_shape=[jax.ShapeDtypeStruct((bsz, t_len, HALF_W), jnp.bfloat16),
                   jax.ShapeDtypeStruct((bsz, ML_HEADS, ML_DIM, ML_DIM), jnp.float32),
                   jax.ShapeDtypeStruct((bsz, ML_HEADS, ML_DIM), jnp.float32),
                   jax.ShapeDtypeStruct((bsz, ML_HEADS, LANES), jnp.float32)],
        compiler_params=pltpu.CompilerParams(
            dimension_semantics=("arbitrary", "arbitrary"), vmem_limit_bytes=VMEM_LIMIT),
        name="mlstm_prompt",
    )(qkv.reshape(bsz, t_len, 3 * HALF_W), og.reshape(bsz, t_len, HALF_W),
      gates.reshape(bsz, t_len, LANES), gb)


def _outproj2_ln_kernel(ya_ref, yb_ref, w_ref, x_ref, g_ref, b_ref, o_ref):
    mix = jnp.dot(ya_ref[...], w_ref[0:HALF_W, :], preferred_element_type=jnp.float32)
    mix = mix + jnp.dot(yb_ref[...], w_ref[HALF_W:, :], preferred_element_type=jnp.float32)
    o_ref[...] = _layer_norm_rows(ALPHA * x_ref[...] + mix, g_ref[...], b_ref[...])


def _outproj2_ln(ya, yb, w_bf16, x, g, b, tm):
    m, d = x.shape
    return pl.pallas_call(
        _outproj2_ln_kernel,
        grid=(m // tm,),
        in_specs=[pl.BlockSpec((tm, HALF_W), lambda i: (i, 0)),
                  pl.BlockSpec((tm, HALF_W), lambda i: (i, 0)),
                  pl.BlockSpec((2 * HALF_W, d), lambda i: (0, 0)),
                  pl.BlockSpec((tm, d), lambda i: (i, 0)),
                  pl.BlockSpec((1, d), lambda i: (0, 0)),
                  pl.BlockSpec((1, d), lambda i: (0, 0))],
        out_specs=pl.BlockSpec((tm, d), lambda i: (i, 0)),
        out_shape=jax.ShapeDtypeStruct((m, d), jnp.float32),
        compiler_params=pltpu.CompilerParams(
            dimension_semantics=("arbitrary",), vmem_limit_bytes=VMEM_LIMIT),
        name="outproj2_ln",
    )(ya, yb, w_bf16, x, g.reshape(1, d), b.reshape(1, d))


O_Q = 0
O_QD = NSA_HEADS * LANES
O_KV = 2048
O_WIN = 2816
O_QI = 3072
O_MISC = 3200
O_PACKED = 3328
ZQ_QI = 2048
MISC_WI = IDX_DIM
MISC_GATE = IDX_DIM + IDX_HEADS
NEG_BIG = -(2.0 ** 100)
INT_MIN = -(2 ** 31)
ALIBI = tuple(float(2.0 ** (-8.0 * (h + 1) / NSA_HEADS)) for h in range(NSA_HEADS))


def _pack_w_in_o(w):
    qn, kvc, kvs, kvw, gates, qd, kvd, qi, ki, wi = _split(w, O_SIZES)
    scale = HEAD_DIM ** -0.5
    rows = w.shape[0]

    def spread(q):
        q = (q * scale).reshape(rows, NSA_HEADS, HEAD_DIM)
        z = jnp.zeros_like(q)
        low = (jnp.arange(NSA_HEADS) < NSA_HEADS // KV_GROUPS)[None, :, None]
        return jnp.concatenate([jnp.where(low, q, z), jnp.where(low, z, q)], axis=-1).reshape(rows, NSA_HEADS * LANES)

    pad = jnp.zeros((rows, O_PACKED - O_MISC - IDX_DIM - IDX_HEADS - 3 * NSA_HEADS), w.dtype)
    return jnp.concatenate([spread(qn), spread(qd), kvc, kvs, kvd, kvw, qi, ki, wi, gates, pad], axis=1)


def _odd_proj_kernel(x_ref, w_ref, zq_ref, kv_ref, kvb_ref, win_ref, misc_ref):
    z = jnp.dot(x_ref[...].astype(jnp.bfloat16), w_ref[...], preferred_element_type=jnp.float32)
    zq_ref[:, 0:O_KV] = z[:, 0:O_KV].astype(jnp.bfloat16)
    zq_ref[:, ZQ_QI:ZQ_QI + LANES] = z[:, O_QI:O_MISC].astype(jnp.bfloat16)
    kv_ref[...] = z[:, O_KV:O_WIN]
    kvb_ref[...] = z[:, O_KV:O_QI].astype(jnp.bfloat16)
    win_ref[...] = z[:, O_WIN:O_QI]
    misc_ref[...] = z[:, O_MISC:O_PACKED]


def _odd_proj(x, w_packed_bf16, tm):
    m, k = x.shape
    widths = (O_KV + LANES, O_WIN - O_KV, O_QI - O_KV, O_QI - O_WIN, LANES)
    dtypes = (jnp.bfloat16, jnp.float32, jnp.bfloat16, jnp.float32, jnp.float32)
    return pl.pallas_call(
        _odd_proj_kernel,
        grid=(m // tm,),
        in_specs=[pl.BlockSpec((tm, k), lambda i: (i, 0)),
                  pl.BlockSpec((k, O_PACKED), lambda i: (0, 0))],
        out_specs=[pl.BlockSpec((tm, n), lambda i: (i, 0)) for n in widths],
        out_shape=[jax.ShapeDtypeStruct((m, n), dt) for n, dt in zip(widths, dtypes)],
        compiler_params=pltpu.CompilerParams(
            dimension_semantics=("arbitrary",), vmem_limit_bytes=VMEM_LIMIT),
        name="odd_proj",
    )(x, w_packed_bf16)


def _dot_nt(a, b):
    return lax.dot_general(a, b, (((1,), (1,)), ((), ())), preferred_element_type=jnp.float32)


def _flash_heads(q_heads, slopes, group, kv_ref, k_blk, v_blk, tk, lo, hi, bias_fn, skip_fn=None, start_fn=None):
    f32, bf16 = jnp.float32, jnp.bfloat16
    if start_fn is None:
        start_fn = lambda kt: kt * tk
    tq = q_heads[0].shape[0]
    n_heads = len(q_heads)
    lane = lax.broadcasted_iota(jnp.int32, (1, LANES), 1)
    own_half = (lane // HEAD_DIM) == group
    q_all = jnp.concatenate(q_heads, axis=0)
    slope_col = jnp.concatenate([jnp.full((tq, 1), sl, f32) for sl in slopes], axis=0)

    def tile(kt, carry):
        m, acc = carry
        start = pl.multiple_of(start_fn(kt), LANES)
        rows = pl.ds(start, tk)
        k = kv_ref[0, rows, k_blk * LANES:(k_blk + 1) * LANES]
        v = kv_ref[0, rows, v_blk * LANES:(v_blk + 1) * LANES]
        v = jnp.where(own_half, v, jnp.ones_like(v))
        sp = (start + lax.broadcasted_iota(jnp.int32, (1, tk), 1)).astype(f32)
        b = bias_fn(kt)
        s = _dot_nt(q_all, k) + slope_col * sp + jnp.concatenate([b] * n_heads, axis=0)
        m_new = jnp.maximum(m, jnp.max(s, axis=-1, keepdims=True))
        p = jnp.exp(s - m_new)
        acc = jnp.exp(m - m_new) * acc + jnp.dot(p.astype(bf16), v, preferred_element_type=f32)
        return m_new, acc

    def body(kt, carry):
        if skip_fn is None:
            return tile(kt, carry)
        return lax.cond(skip_fn(kt), lambda c: c, functools.partial(tile, kt), carry)

    m0 = jnp.full((n_heads * tq, 1), -jnp.inf, f32)
    a0 = jnp.zeros((n_heads * tq, LANES), f32)
    _, acc = lax.fori_loop(lo, hi, body, (m0, a0))
    return [acc[r * tq:(r + 1) * tq] for r in range(n_heads)]


def _odd_attn_kernel(zq_ref, miscq_ref, kvb_ref, misck_ref, wc_ref, o_ref,
                     kc_ref, vc_ref, ki_ref, key_ref, dbias_ref, flag_ref, *, t_len, tq, tk, topk):
    f32, bf16, i32 = jnp.float32, jnp.bfloat16, jnp.int32
    nb = t_len // BLK
    qi_blk = pl.program_id(1)
    q0 = qi_blk * tq

    @pl.when(qi_blk == 0)
    def _():
        ck = kvb_ref[0, :, 0:LANES].astype(f32).reshape(nb, BLK, LANES)
        kc_ref[...] = jnp.sum(ck * wc_ref[0][None], axis=1).astype(bf16)
        cv = kvb_ref[0, :, LANES:2 * LANES].astype(f32).reshape(nb, BLK, LANES)
        vc = jnp.sum(cv * wc_ref[1][None], axis=1)
        for g in range(KV_GROUPS):
            half = vc[:, g * HEAD_DIM:(g + 1) * HEAD_DIM]
            vc_ref[g] = jnp.concatenate([half, half], axis=1).astype(bf16)
        ki_ref[...] = misck_ref[0, :, 0:IDX_DIM].astype(bf16)

    misc = miscq_ref[0]
    gates = jax.nn.sigmoid(misc[:, MISC_GATE:MISC_GATE + 3 * NSA_HEADS])
    t_col = q0 + lax.broadcasted_iota(i32, (tq, 1), 0)
    lane = lax.broadcasted_iota(i32, (1, LANES), 1)
    low_half = lane < HEAD_DIM
    heads_per_group = NSA_HEADS // KV_GROUPS

    def q_head(base, h):
        return zq_ref[0, :, base + h * LANES: base + (h + 1) * LANES]

    def pair(even, odd):
        return jnp.where(low_half, even, odd)

    def normalised_pair(acc_even, acc_odd, g):
        outs = []
        for parity, acc in enumerate((acc_even, acc_odd)):
            swapped = pltpu.roll(acc, HEAD_DIM, 1)
            outs.append(acc * (1.0 / swapped) if parity == g else swapped * (1.0 / acc))
        return pair(outs[0], outs[1])

    def gate_pair(j, c):
        a = gates[:, (2 * j) * 3 + c:(2 * j) * 3 + c + 1]
        b = gates[:, (2 * j + 1) * 3 + c:(2 * j + 1) * 3 + c + 1]
        return jnp.where(low_half, a, b)

    jb = lax.broadcasted_iota(i32, (1, nb), 1)
    dist_c = t_col - (jb * BLK + (BLK - 1))
    mask_c = dist_c >= 0
    dist_cf = dist_c.astype(f32)
    cur = t_col // BLK
    forced = (jb == 0) | (jb == cur) | (jb == cur - 1)
    admissible = jb <= cur
    jb_full = lax.broadcasted_iota(i32, (tq, nb), 1)
    o_cmp, sel_bias, sel_any = [], [], []
    for g in range(KV_GROUPS):
        imp = jnp.zeros((tq, nb), f32)
        for r in range(heads_per_group):
            h = g * heads_per_group + r
            s = _dot_nt(q_head(O_Q, h), kc_ref[...]) - ALIBI[h] * dist_cf
            s = jnp.where(mask_c, s, NEG_BIG)
            m = jnp.max(s, axis=-1, keepdims=True)
            e = jnp.where(mask_c, jnp.exp(s - m), 0.0)
            p = e * (1.0 / jnp.maximum(jnp.sum(e, axis=-1, keepdims=True), 1e-30))
            imp = imp + p
            o_cmp.append(jnp.dot(p.astype(bf16), vc_ref[g], preferred_element_type=f32))
        imp = jnp.where(forced, FORCE_SCORE, imp)
        imp = jnp.where(admissible, imp, -jnp.inf)
        sel = jnp.zeros((tq, nb), f32)
        for _ in range(min(NSA_TOPN, nb)):
            m = jnp.max(imp, axis=-1, keepdims=True)
            first = jnp.min(jnp.where(imp == m, jb_full, nb), axis=-1, keepdims=True)
            hit = jb_full == first
            sel = jnp.where(hit & (m > -jnp.inf), 1.0, sel)
            imp = jnp.where(hit, -jnp.inf, imp)
        sel_bias.append(jnp.where(sel > 0.0, 0.0, NEG_BIG).astype(bf16))
        sel_any.append(jnp.max(sel, axis=0, keepdims=True))

    n_sel_tiles = (q0 + tq + tk - 1) // tk
    blocks_per_tile = tk // BLK

    for g in range(KV_GROUPS):
        for j in range(nb // blocks_per_tile):
            hit = jnp.max(sel_any[g][:, j * blocks_per_tile:(j + 1) * blocks_per_tile])
            flag_ref[g, j] = (hit > 0.0).astype(i32)

    def sel_skip_fn(g):
        return lambda kt: flag_ref[g, kt] == 0

    def causal_bias(start, width):
        sp = start + lax.broadcasted_iota(i32, (1, width), 1)
        return jnp.where(sp <= t_col, 0.0, NEG_BIG)

    def sel_bias_fn(g):
        def fn(kt):
            row = lax.broadcasted_iota(i32, (nb, tk), 0)
            col = lax.broadcasted_iota(i32, (nb, tk), 1)
            expand = jnp.where(row == kt * blocks_per_tile + col // BLK, 1.0, 0.0).astype(bf16)
            return jnp.dot(sel_bias[g], expand, preferred_element_type=f32) + causal_bias(kt * tk, tk)
        return fn

    tkw = min(WINDOW + tq, t_len)
    win_start = jnp.clip(q0 - WINDOW, 0, t_len - tkw)

    def win_bias_fn(kt):
        dist = t_col - (win_start + lax.broadcasted_iota(i32, (1, tkw), 1))
        return jnp.where((dist >= 0) & (dist < WINDOW), 0.0, NEG_BIG)

    qi_heads = [zq_ref[0, :, ZQ_QI + h * IDX_DIM: ZQ_QI + (h + 1) * IDX_DIM] for h in range(IDX_HEADS)]
    wi_cols = [misc[:, MISC_WI + h: MISC_WI + h + 1] for h in range(IDX_HEADS)]

    def index_tile(kt, carry):
        start = pl.multiple_of(kt * tk, tk)
        kik = ki_ref[pl.ds(start, tk), :]
        tot = jnp.zeros((tq, tk), f32)
        for h in range(IDX_HEADS):
            sc = _dot_nt(qi_heads[h], kik) * (IDX_DIM ** -0.5)
            tot = tot + jnp.maximum(sc, 0.0) * wi_cols[h]
        tot = tot * (IDX_HEADS ** -0.5)
        tot = jnp.where(tot == 0.0, 0.0, tot)
        bits = lax.bitcast_convert_type(tot, i32)
        key = jnp.where(bits < 0, bits ^ jnp.int32(0x7FFFFFFF), bits)
        sp = start + lax.broadcasted_iota(i32, (1, tk), 1)
        key_ref[kt] = jnp.where(sp <= t_col, key, INT_MIN)
        return carry

    lax.fori_loop(0, n_sel_tiles, index_tile, 0)

    def count_ge(trial):
        def body(kt, acc):
            c = (key_ref[kt] >= trial).astype(i32)
            for j in range(tk // LANES):
                acc = acc + c[:, j * LANES:(j + 1) * LANES]
            return acc
        acc = lax.fori_loop(0, n_sel_tiles, body, jnp.zeros((tq, LANES), i32))
        return jnp.sum(acc, axis=-1, keepdims=True)

    c_pos = count_ge(jnp.zeros((tq, 1), i32))
    v0 = jnp.where(c_pos >= topk, 0, INT_MIN).astype(i32)
    c0 = jnp.where(c_pos >= topk, c_pos, n_sel_tiles * tk)

    def bit_step(i, carry):
        v, cge = carry
        trial = v | lax.shift_left(jnp.int32(1), 30 - i)
        c = count_ge(trial)
        ok = c >= topk
        return jnp.where(ok, trial, v), jnp.where(ok, c, cge)

    v_thr, c_ge = lax.fori_loop(0, 31, bit_step, (v0, c0))
    v_eff = jnp.maximum(v_thr, INT_MIN + 1)

    def write_plain(kt, carry):
        dbias_ref[kt] = jnp.where(key_ref[kt] >= v_eff, 0.0, NEG_BIG)
        return carry

    def plain_path():
        lax.fori_loop(0, n_sel_tiles, write_plain, 0)

    def tie_path():
        need = topk - count_ge(v_thr + 1)

        def count_tie_below(limit):
            def body(kt, acc):
                sp = kt * tk + lax.broadcasted_iota(i32, (1, tk), 1)
                c = ((key_ref[kt] == v_thr) & (sp < limit)).astype(i32)
                for j in range(tk // LANES):
                    acc = acc + c[:, j * LANES:(j + 1) * LANES]
                return acc
            acc = lax.fori_loop(0, n_sel_tiles, body, jnp.zeros((tq, LANES), i32))
            return jnp.sum(acc, axis=-1, keepdims=True)

        n_bits = max(1, (t_len - 1).bit_length())

        def idx_step(i, w):
            trial = w | lax.shift_left(jnp.int32(1), n_bits - 1 - i)
            return jnp.where(count_tie_below(trial) < need, trial, w)

        w_last = lax.fori_loop(0, n_bits, idx_step, jnp.zeros((tq, 1), i32))

        def write_tie(kt, carry):
            sp = kt * tk + lax.broadcasted_iota(i32, (1, tk), 1)
            key = key_ref[kt]
            take = (key > v_thr) | ((key == v_thr) & (sp <= w_last))
            dbias_ref[kt] = jnp.where(take & (key >= v_eff), 0.0, NEG_BIG)
            return carry

        lax.fori_loop(0, n_sel_tiles, write_tie, 0)

    lax.cond(jnp.max(c_ge) > topk, tie_path, plain_path)

    def dsa_bias_fn(kt):
        return dbias_ref[kt]

    for g in range(KV_GROUPS):
        hs = [g * heads_per_group + r for r in range(heads_per_group)]
        slopes = [ALIBI[h] for h in hs]
        qn = [q_head(O_Q, h) for h in hs]
        qd = [q_head(O_QD, h) for h in hs]
        a_s = _flash_heads(qn, slopes, g, kvb_ref, 2, 3, tk, 0, n_sel_tiles, sel_bias_fn(g), sel_skip_fn(g))
        a_w = _flash_heads(qn, slopes, g, kvb_ref, 6, 7, tkw, 0, 1, win_bias_fn, start_fn=lambda kt: win_start)
        a_d = _flash_heads(qd, slopes, g, kvb_ref, 4, 5, tk, 0, n_sel_tiles, dsa_bias_fn)
        for jj in range(heads_per_group // 2):
            j = hs[2 * jj] // 2
            o_n = (gate_pair(j, 0) * pair(o_cmp[2 * j], o_cmp[2 * j + 1])
                   + gate_pair(j, 1) * normalised_pair(a_s[2 * jj], a_s[2 * jj + 1], g)
                   + gate_pair(j, 2) * normalised_pair(a_w[2 * jj], a_w[2 * jj + 1], g))
            o_ref[0, :, j * LANES:(j + 1) * LANES] = o_n.astype(o_ref.dtype)
            o_d = normalised_pair(a_d[2 * jj], a_d[2 * jj + 1], g)
            o_ref[0, :, HALF_W + j * LANES:HALF_W + (j + 1) * LANES] = o_d.astype(o_ref.dtype)


def _odd_attn_prompt(zq, misc, kvb, wc2, bsz, t_len, tq=128, tk=512):
    tk = min(tk, t_len)
    topk = min(DSA_TOPK_MAX, t_len // 4)
    nb = t_len // BLK
    n_kt = t_len // tk
    zq3 = zq.reshape(bsz, t_len, zq.shape[1])
    misc3 = misc.reshape(bsz, t_len, LANES)
    kvb3 = kvb.reshape(bsz, t_len, kvb.shape[1])
    kern = functools.partial(_odd_attn_kernel, t_len=t_len, tq=tq, tk=tk, topk=topk)
    out = pl.pallas_call(
        kern,
        grid=(bsz, t_len // tq),
        in_specs=[pl.BlockSpec((1, tq, zq.shape[1]), lambda b, i: (b, i, 0)),
                  pl.BlockSpec((1, tq, LANES), lambda b, i: (b, i, 0)),
                  pl.BlockSpec((1, t_len, kvb.shape[1]), lambda b, i: (b, 0, 0)),
                  pl.BlockSpec((1, t_len, LANES), lambda b, i: (b, 0, 0)),
                  pl.BlockSpec((2, BLK, LANES), lambda b, i: (0, 0, 0))],
        out_specs=pl.BlockSpec((1, tq, 2 * HALF_W), lambda b, i: (b, i, 0)),
        out_shape=jax.ShapeDtypeStruct((bsz, t_len, 2 * HALF_W), jnp.bfloat16),
        scratch_shapes=[pltpu.VMEM((nb, LANES), jnp.bfloat16),
                        pltpu.VMEM((KV_GROUPS, nb, LANES), jnp.bfloat16),
                        pltpu.VMEM((t_len, IDX_DIM), jnp.bfloat16),
                        pltpu.VMEM((n_kt, tq, tk), jnp.int32),
                        pltpu.VMEM((n_kt, tq, tk), jnp.float32),
                        pltpu.SMEM((KV_GROUPS, n_kt), jnp.int32)],
        compiler_params=pltpu.CompilerParams(
            dimension_semantics=("arbitrary", "arbitrary"), vmem_limit_bytes=VMEM_LIMIT),
        name="odd_attn_prompt",
    )(zq3, misc3, kvb3, misc3, wc2)
    return out.reshape(bsz * t_len, 2 * HALF_W)


PAGE = 128
ISC_ROWS = 24


def _odd_sample_kernel(pt_ref, qn_ref, qd_ref, qi_ref, wi_ref, gate_ref, newb_ref, newki_ref, neww_ref,
                       win_ref, wc_ref, *rest, n_pages, topk):
    del pt_ref
    f32, bf16, i32 = jnp.float32, jnp.bfloat16, jnp.int32
    kv_refs, idx_refs = rest[:n_pages], rest[n_pages:2 * n_pages]
    o_ref, wout_ref, kc_ref, vc_ref, isc_ref = rest[2 * n_pages:]
    past = n_pages * PAGE
    nb = past // BLK + 1
    nbp = kc_ref.shape[0]
    row8 = lax.broadcasted_iota(i32, (8, 1), 0)
    lane = lax.broadcasted_iota(i32, (1, LANES), 1)
    slope8 = jnp.zeros((8, 1), f32)
    for h in range(NSA_HEADS):
        slope8 = jnp.where(row8 == h, ALIBI[h], slope8)
    low_group = row8 < NSA_HEADS // KV_GROUPS

    def pick_half(x):
        return jnp.where(low_group, x[:, 0:HEAD_DIM], x[:, HEAD_DIM:2 * HEAD_DIM])

    qn8, qd8 = qn_ref[0], qd_ref[0]
    qi8, wi8 = qi_ref[0], wi_ref[0]
    newb = newb_ref[0]
    gates = jax.nn.sigmoid(gate_ref[0])
    wk128 = jnp.concatenate([wc_ref[0], wc_ref[0]], axis=0)
    wv128 = jnp.concatenate([wc_ref[1], wc_ref[1]], axis=0)

    def new_score(q8, k_row):
        return jnp.sum(q8.astype(f32) * k_row.astype(f32), axis=-1, keepdims=True) + slope8 * float(past)

    s_sel, s_dsa = [], []
    for p in range(n_pages):
        kv = kv_refs[p]
        kc_ref[2 * p:2 * p + 2, :] = jnp.sum((kv[0, :, 0:LANES] * wk128).reshape(2, BLK, LANES), axis=1)
        vc_ref[2 * p:2 * p + 2, :] = jnp.sum((kv[0, :, LANES:2 * LANES] * wv128).reshape(2, BLK, LANES), axis=1)
        pos = (p * PAGE + lane).astype(f32)
        s_sel.append(_dot_nt(qn8, kv[0, :, 2 * LANES:3 * LANES].astype(bf16)) + slope8 * pos)
        s_dsa.append(_dot_nt(qd8, kv[0, :, 4 * LANES:5 * LANES].astype(bf16)) + slope8 * pos)
        ii = _dot_nt(qi8, idx_refs[p][0].astype(bf16)) * (IDX_DIM ** -0.5)
        isc_ref[p:p + 1, :] = jnp.sum(jnp.maximum(ii, 0.0) * wi8, axis=0, keepdims=True) * (IDX_HEADS ** -0.5)
    kc_ref[nb - 1:nb, :] = newb[:, 0:LANES].astype(f32) * wc_ref[0, 0:1, :]
    vc_ref[nb - 1:nb, :] = newb[:, LANES:2 * LANES].astype(f32) * wc_ref[1, 0:1, :]
    kc_ref[nb:nbp, :] = jnp.zeros((nbp - nb, LANES), f32)
    vc_ref[nb:nbp, :] = jnp.zeros((nbp - nb, LANES), f32)
    ii_new = jnp.sum(qi8.astype(f32) * newki_ref[0].astype(bf16).astype(f32), axis=-1, keepdims=True) * (IDX_DIM ** -0.5)
    isc_new = jnp.sum(jnp.maximum(ii_new, 0.0) * wi8, axis=0, keepdims=True) * (IDX_HEADS ** -0.5)
    isc_ref[n_pages:n_pages + 1, :] = jnp.where(lane == 0, isc_new, -jnp.inf)
    isc_ref[n_pages + 1:ISC_ROWS, :] = jnp.full((ISC_ROWS - n_pages - 1, LANES), -jnp.inf, f32)

    jb = lax.broadcasted_iota(i32, (1, nbp), 1)
    dist_c = past - (jb * BLK + (BLK - 1))
    mask_c = (dist_c >= 0) & (jb < nb)
    s_c = _dot_nt(qn8, kc_ref[...].astype(bf16)) - slope8 * dist_c.astype(f32)
    s_c = jnp.where(mask_c, s_c, NEG_BIG)
    m_c = jnp.max(s_c, axis=-1, keepdims=True)
    e_c = jnp.where(mask_c, jnp.exp(s_c - m_c), 0.0)
    p_c = e_c * (1.0 / jnp.maximum(jnp.sum(e_c, axis=-1, keepdims=True), 1e-30))
    o_c = pick_half(jnp.dot(p_c.astype(bf16), vc_ref[...].astype(bf16), preferred_element_type=f32))
    imp_lo = jnp.sum(jnp.where(low_group, p_c, 0.0), axis=0, keepdims=True)
    imp_hi = jnp.sum(jnp.where(low_group, 0.0, p_c), axis=0, keepdims=True)
    imp = jnp.where(low_group, imp_lo, imp_hi)
    cur = past // BLK
    imp = jnp.where((jb == 0) | (jb == cur) | (jb == cur - 1), FORCE_SCORE, imp)
    imp = jnp.where(jb <= cur, imp, -jnp.inf)
    jb8 = lax.broadcasted_iota(i32, (8, nbp), 1)
    sel = jnp.zeros((8, nbp), f32)
    for _ in range(min(NSA_TOPN, nb)):
        m = jnp.max(imp, axis=-1, keepdims=True)
        first = jnp.min(jnp.where(imp == m, jb8, nbp), axis=-1, keepdims=True)
        hit = jb8 == first
        sel = jnp.where(hit & (m > -jnp.inf), 1.0, sel)
        imp = jnp.where(hit, -jnp.inf, imp)

    tot = isc_ref[...]
    tot = jnp.where(tot == 0.0, 0.0, tot)
    bits = lax.bitcast_convert_type(tot, i32)
    key = jnp.where(bits < 0, bits ^ jnp.int32(0x7FFFFFFF), bits)
    krow = lax.broadcasted_iota(i32, (ISC_ROWS, LANES), 0)
    kidx = krow * LANES + lax.broadcasted_iota(i32, (ISC_ROWS, LANES), 1)
    key = jnp.where(kidx <= past, key, INT_MIN)

    def total(x):
        return jnp.sum(jnp.sum(x, axis=1, keepdims=True), axis=0, keepdims=True)

    def count_ge(trial):
        return total((key >= trial).astype(i32))

    c_pos = count_ge(jnp.zeros((1, 1), i32))
    v_thr = jnp.where(c_pos >= topk, 0, INT_MIN).astype(i32)
    c_ge = jnp.where(c_pos >= topk, c_pos, ISC_ROWS * LANES)
    for bit in range(30, -1, -1):
        trial = v_thr | jnp.int32(1 << bit)
        c = count_ge(trial)
        ok = c >= topk
        v_thr = jnp.where(ok, trial, v_thr)
        c_ge = jnp.where(ok, c, c_ge)
    v_eff = jnp.maximum(v_thr, INT_MIN + 1)

    def tie_mask():
        need = topk - count_ge(v_thr + 1)
        tie = key == v_thr
        w_last = jnp.zeros((1, 1), i32)
        for bit in range((ISC_ROWS * LANES - 1).bit_length() - 1, -1, -1):
            trial = w_last | jnp.int32(1 << bit)
            below = total((tie & (kidx < trial)).astype(i32))
            w_last = jnp.where(below < need, trial, w_last)
        take = (key > v_thr) | (tie & (kidx <= w_last))
        return jnp.where(take & (key >= v_eff), 1.0, 0.0)

    def plain_mask():
        return jnp.where(key >= v_eff, 1.0, 0.0)

    dsel = lax.cond(c_ge[0, 0] > topk, tie_mask, plain_mask)

    def attend(tiles, masks, s_new, new_ok, v_tile, v_new):
        tiles = [jnp.where(mk, t, NEG_BIG) for t, mk in zip(tiles, masks)]
        s_new = jnp.where(new_ok, s_new, NEG_BIG)
        m = s_new
        for t in tiles:
            m = jnp.maximum(m, jnp.max(t, axis=-1, keepdims=True))
        l = jnp.exp(s_new - m)
        acc = l * v_new.astype(f32)
        for j, t in enumerate(tiles):
            e = jnp.exp(t - m)
            l = l + jnp.sum(e, axis=-1, keepdims=True)
            acc = acc + jnp.dot(e.astype(bf16), v_tile(j), preferred_element_type=f32)
        return pick_half(acc * (1.0 / l))

    sel_masks = [jnp.where(lane < BLK, sel[:, 2 * p:2 * p + 1], sel[:, 2 * p + 1:2 * p + 2]) > 0.0
                 for p in range(n_pages)]
    o_s = attend(s_sel, sel_masks, new_score(qn8, newb[:, 2 * LANES:3 * LANES]), sel[:, nb - 1:nb] > 0.0,
                 lambda p: kv_refs[p][0, :, 3 * LANES:4 * LANES].astype(bf16), newb[:, 3 * LANES:4 * LANES])
    dsa_masks = [dsel[p:p + 1, :] > 0.0 for p in range(n_pages)]
    o_d = attend(s_dsa, dsa_masks, new_score(qd8, newb[:, 4 * LANES:5 * LANES]), dsel[n_pages:n_pages + 1, 0:1] > 0.0,
                 lambda p: kv_refs[p][0, :, 5 * LANES:6 * LANES].astype(bf16), newb[:, 5 * LANES:6 * LANES])
    win = win_ref[0]
    n_win = win.shape[0]
    wlane = lax.broadcasted_iota(i32, (1, n_win), 1)
    wpos = past - n_win + wlane
    s_w = _dot_nt(qn8, win[:, 0:LANES].astype(bf16)) + slope8 * wpos.astype(f32)
    w_ok = (past - wpos < WINDOW) & (wpos >= 0)
    o_w = attend([s_w], [w_ok], new_score(qn8, newb[:, 6 * LANES:7 * LANES]), True,
                 lambda p: win[:, LANES:2 * LANES].astype(bf16), newb[:, 7 * LANES:8 * LANES])
    o_ref[0, 0:NSA_HEADS, :] = gates[:, 0:1] * o_c + gates[:, 1:2] * o_s + gates[:, 2:3] * o_w
    o_ref[0, NSA_HEADS:2 * NSA_HEADS, :] = o_d
    wrow = lax.broadcasted_iota(i32, (n_win, 1), 0)
    wout_ref[0] = jnp.where(wrow == n_win - 1, neww_ref[0], pltpu.roll(win, n_win - 1, 0))


def _odd_attn_sample(zq, misc, kvb, win_new, cache_kv, cache_idx, cache_win, page_table, wc2):
    f32, bf16 = jnp.float32, jnp.bfloat16
    s_n, n_pages = page_table.shape
    n_pool = cache_kv.shape[0]
    n_win = cache_win.shape[1]
    past = n_pages * PAGE
    topk = min(DSA_TOPK_MAX, (past + 1) // 4)
    nbp = _round_up(past // BLK + 1, 8)
    qn = zq[:, O_Q:O_QD].reshape(s_n, NSA_HEADS, LANES)
    qd = zq[:, O_QD:ZQ_QI].reshape(s_n, DSA_HEADS, LANES)
    qi = jnp.pad(zq[:, ZQ_QI:ZQ_QI + LANES].reshape(s_n, IDX_HEADS, IDX_DIM), ((0, 0), (0, 8 - IDX_HEADS), (0, 0)))
    wi = jnp.pad(misc[:, MISC_WI:MISC_WI + IDX_HEADS].reshape(s_n, IDX_HEADS, 1), ((0, 0), (0, 8 - IDX_HEADS), (0, 0)))
    gts = misc[:, MISC_GATE:MISC_GATE + 3 * NSA_HEADS].reshape(s_n, NSA_HEADS, 3)
    kv_pages = cache_kv.reshape(n_pool, PAGE, KV_SLOTS * LANES)
    per_seq = lambda shape: pl.BlockSpec((1,) + shape, lambda b, pt: (b, 0, 0))
    page_spec = lambda width, p: pl.BlockSpec((1, PAGE, width), lambda b, pt, p=p: (pt[b, p], 0, 0))
    in_specs = ([per_seq((NSA_HEADS, LANES)), per_seq((DSA_HEADS, LANES)), per_seq((8, IDX_DIM)),
                 per_seq((8, 1)), per_seq((NSA_HEADS, 3)), per_seq((1, 8 * LANES)), per_seq((1, IDX_DIM)),
                 per_seq((1, 2 * LANES)), per_seq((n_win, 2 * LANES)),
                 pl.BlockSpec((2, BLK, LANES), lambda b, pt: (0, 0, 0))]
                + [page_spec(KV_SLOTS * LANES, p) for p in range(n_pages)]
                + [page_spec(IDX_DIM, p) for p in range(n_pages)])
    kern = functools.partial(_odd_sample_kernel, n_pages=n_pages, topk=topk)
    o, win_out = pl.pallas_call(
        kern,
        grid_spec=pltpu.PrefetchScalarGridSpec(
            num_scalar_prefetch=1,
            grid=(s_n,),
            in_specs=in_specs,
            out_specs=[per_seq((2 * NSA_HEADS, HEAD_DIM)), per_seq((n_win, 2 * LANES))],
            scratch_shapes=[pltpu.VMEM((nbp, LANES), f32), pltpu.VMEM((nbp, LANES), f32),
                            pltpu.VMEM((ISC_ROWS, LANES), f32)]),
        out_shape=[jax.ShapeDtypeStruct((s_n, 2 * NSA_HEADS, HEAD_DIM), f32),
                   jax.ShapeDtypeStruct((s_n, n_win, 2 * LANES), f32)],
        compiler_params=pltpu.CompilerParams(
            dimension_semantics=("arbitrary",), vmem_limit_bytes=VMEM_LIMIT),
        name="odd_attn_sample",
    )(page_table, qn, qd, qi, wi, gts, kvb.reshape(s_n, 1, 8 * LANES), misc[:, 0:IDX_DIM].reshape(s_n, 1, IDX_DIM),
      win_new.reshape(s_n, 1, 2 * LANES), cache_win.reshape(s_n, n_win, 2 * LANES), wc2,
      *([kv_pages] * n_pages), *([cache_idx] * n_pages))
    return o.reshape(s_n, 2 * HALF_W), win_out


def _split(z, sizes):
    cuts = [int(c) for c in np.cumsum(sizes)[:-1]]
    return jnp.split(z, cuts, axis=-1)


def _alibi_slopes(n):
    return jnp.asarray(np.power(2.0, -8.0 * np.arange(1, n + 1) / n), dtype=jnp.float32)


def _masked_softmax(s, mask):
    s = jnp.where(mask, s.astype(jnp.float32), -jnp.inf)
    m = jnp.max(s, axis=-1, keepdims=True)
    e = jnp.exp(s - jnp.where(jnp.isfinite(m), m, 0.0))
    return e / jnp.maximum(jnp.sum(e, axis=-1, keepdims=True), 1e-30)


def _short_conv(u_ext, w, b):
    t = u_ext.shape[1] - (CONV_W - 1)
    return b + sum(w[j] * u_ext[:, j:j + t] for j in range(CONV_W))


def _mlstm_chunk(carry, inp):
    c_st, n_st, m_st = carry
    q, k, v, ig, lf = inp
    L = q.shape[2]
    b = jnp.cumsum(lf, axis=-1)
    causal = jnp.tril(jnp.ones((L, L), dtype=bool))
    dmat = jnp.where(causal, b[..., :, None] - b[..., None, :] + ig[..., None, :], -jnp.inf)
    inter = b + m_st[..., None]
    m_t = jnp.maximum(inter, jnp.max(dmat, axis=-1))
    w_intra = jnp.exp(dmat - m_t[..., None])
    w_inter = jnp.exp(inter - m_t)
    s = jnp.einsum('bhtd,bhsd->bhts', q, k) * w_intra
    num = jnp.einsum('bhts,bhsd->bhtd', s, v) + w_inter[..., None] * jnp.einsum('bhed,bhtd->bhte', c_st, q)
    den = jnp.sum(s, axis=-1) + w_inter * jnp.einsum('bhd,bhtd->bht', n_st, q)
    h = num / jnp.maximum(jnp.abs(den), jnp.exp(-m_t))[..., None]
    m_new = m_t[..., -1]
    w_state = jnp.exp(b[..., -1:] - b + ig - m_new[..., None])
    decay = jnp.exp(b[..., -1] + m_st - m_new)
    c_new = decay[..., None, None] * c_st + jnp.einsum('bhs,bhse,bhsd->bhed', w_state, v, k)
    n_new = decay[..., None] * n_st + jnp.einsum('bhs,bhsd->bhd', w_state, k)
    return (c_new, n_new, m_new), h


def _even_mixer(z, gate_b, conv_w, conv_b, conv_prev, ml_state, chunk):
    f32 = jnp.float32
    bn, t, _ = z.shape
    h, bg, cg, q, k, v, og, ig, fg = _split(z, E_SIZES)
    u = cg * h
    u_ext = jnp.concatenate([conv_prev.astype(u.dtype), u], axis=1)
    y_a = bg * _short_conv(u_ext, conv_w, conv_b)
    new_conv = u_ext[:, u_ext.shape[1] - (CONV_W - 1):]
    heads = lambda a: a.reshape(bn, t, ML_HEADS, ML_DIM).astype(f32)
    q, k, v = heads(q), heads(k) * (ML_DIM ** -0.5), heads(v)
    ig = (ig + gate_b[:ML_HEADS]).astype(f32)
    lf = jax.nn.log_sigmoid((fg + gate_b[ML_HEADS:]).astype(f32))
    nc = t // chunk

    def to_chunks(a):
        a = jnp.moveaxis(a.reshape((bn, nc, chunk) + a.shape[2:]), 1, 0)
        return jnp.swapaxes(a, 2, 3)

    carry = tuple(s.astype(f32) for s in ml_state)
    carry, hs = lax.scan(_mlstm_chunk, carry, tuple(to_chunks(a) for a in (q, k, v, ig, lf)))
    hs = jnp.swapaxes(jnp.moveaxis(hs, 0, 1), 2, 3).reshape(bn, t, HALF_W)
    y_b = jax.nn.sigmoid(og) * hs
    return jnp.concatenate([y_a, y_b], axis=-1), new_conv, carry


def _odd_attend(qn, gates, qd, qi, wi, q_pos, kc, vc, slc, dsa_kv, kidx, kw, kw_pos, topn, topk):
    f32 = jnp.float32
    bn, tq, g, r, _ = qn.shape
    nb = kc.shape[1]
    L = kidx.shape[1]
    scale = HEAD_DIM ** -0.5
    m_n = _alibi_slopes(NSA_HEADS).reshape(g, r)
    m_d = _alibi_slopes(DSA_HEADS).reshape(KV_GROUPS, DSA_HEADS // KV_GROUPS)
    jb = jnp.arange(nb)
    dist_c = q_pos[:, None] - (jb * BLK + BLK - 1)[None, :]
    s_c = jnp.einsum('btgrd,bjgd->btgrj', qn, kc).astype(f32) * scale - m_n[:, :, None] * dist_c[:, None, None, :].astype(f32)
    p_c = _masked_softmax(s_c, (dist_c >= 0)[:, None, None, :])
    o_c = jnp.einsum('btgrj,bjgd->btgrd', p_c.astype(vc.dtype), vc)
    cur = q_pos // BLK
    imp = jnp.sum(p_c, axis=3)
    forced = (jb[None, :] == 0) | (jb[None, :] == cur[:, None]) | (jb[None, :] == cur[:, None] - 1)
    imp = jnp.where(forced[:, None, :], FORCE_SCORE, imp)
    imp = jnp.where((jb[None, :] <= cur[:, None])[:, None, :], imp, -jnp.inf)
    top_val, top_idx = lax.top_k(imp, topn)
    b_ix = jnp.arange(bn)[:, None, None, None]
    g_ix = jnp.arange(g)[None, None, :, None]
    sel = slc[b_ix, g_ix, top_idx]
    pos = top_idx[..., None] * BLK + jnp.arange(BLK)
    dist_s = q_pos[None, :, None, None, None] - pos
    mask_s = (dist_s >= 0) & jnp.isfinite(top_val)[..., None]
    s_s = jnp.einsum('btgrd,btgnpd->btgrnp', qn, sel[..., 0, :]).astype(f32) * scale - m_n[None, None, :, :, None, None] * dist_s[:, :, :, None].astype(f32)
    shp = s_s.shape
    p_s = _masked_softmax(s_s.reshape(shp[:4] + (-1,)), mask_s[:, :, :, None].reshape(bn, tq, g, 1, -1)).reshape(shp)
    o_s = jnp.einsum('btgrnp,btgnpd->btgrd', p_s.astype(sel.dtype), sel[..., 1, :])
    dist_w = q_pos[:, None] - kw_pos[None, :]
    mask_w = (dist_w >= 0) & (dist_w < WINDOW) & (kw_pos >= 0)[None, :]
    s_w = jnp.einsum('btgrd,blgd->btgrl', qn, kw[:, :, 0]).astype(f32) * scale - m_n[:, :, None] * dist_w[:, None, None, :].astype(f32)
    p_w = _masked_softmax(s_w, mask_w[:, None, None, :])
    o_w = jnp.einsum('btgrl,blgd->btgrd', p_w.astype(kw.dtype), kw[:, :, 1])
    o_n = (gates[..., 0:1] * o_c + gates[..., 1:2] * o_s + gates[..., 2:3] * o_w).astype(qn.dtype).reshape(bn, tq, -1)
    k_pos = jnp.arange(L)
    isc = jnp.einsum('bthe,ble->bthl', qi, kidx).astype(f32) * (IDX_DIM ** -0.5)
    isc = jnp.einsum('bthl,bth->btl', jax.nn.relu(isc), wi.astype(f32)) * (IDX_HEADS ** -0.5)
    isc = jnp.where((k_pos[None, :] <= q_pos[:, None])[None], isc, -jnp.inf)
    i_val, i_idx = lax.top_k(isc, topk)
    sel_d = dsa_kv[jnp.arange(bn)[:, None, None], i_idx]
    dist_d = (q_pos[None, :, None] - i_idx).astype(f32)
    s_d = jnp.einsum('btgrd,btkgd->btgrk', qd, sel_d[:, :, :, 0]).astype(f32) * scale - m_d[None, None, :, :, None] * dist_d[:, :, None, None, :]
    p_d = _masked_softmax(s_d, jnp.isfinite(i_val)[:, :, None, None, :])
    o_d = jnp.einsum('btgrk,btkgd->btgrd', p_d.astype(sel_d.dtype), sel_d[:, :, :, 1]).reshape(bn, tq, -1)
    return jnp.concatenate([o_n, o_d.astype(qn.dtype)], axis=-1)


def _odd_mixer(z, w_cmp, past_kv, past_idx, past_win, keep):
    bn, t, _ = z.shape
    p_len = past_kv.shape[1]
    wp = past_win.shape[1]
    g, r = KV_GROUPS, NSA_HEADS // KV_GROUPS
    qn, kvc, kvs, kvw, gates, qd, kvd, qi, ki, wi = _split(z, O_SIZES)
    qn = qn.reshape(bn, t, g, r, HEAD_DIM)
    qd = qd.reshape(bn, t, g, DSA_HEADS // g, HEAD_DIM)
    gates = jax.nn.sigmoid(gates.astype(jnp.float32)).reshape(bn, t, g, r, 3)
    qi = qi.reshape(bn, t, IDX_HEADS, IDX_DIM)
    kv_new = jnp.concatenate([kvc, kvs, kvd], axis=-1).reshape(bn, t, KV_SLOTS, g, HEAD_DIM)
    kvw_new = kvw.reshape(bn, t, 2, g, HEAD_DIM)
    kv = jnp.concatenate([past_kv.astype(kv_new.dtype), kv_new], axis=1)
    kidx = jnp.concatenate([past_idx.astype(ki.dtype), ki], axis=1)
    win_all = jnp.concatenate([past_win.astype(kvw_new.dtype), kvw_new], axis=1)
    L = p_len + t
    nb = -(-L // BLK)
    kvb = jnp.pad(kv[:, :, :4], ((0, 0), (0, nb * BLK - L), (0, 0), (0, 0), (0, 0)))
    kvb = kvb.reshape(bn, nb, BLK, 4, g, HEAD_DIM)
    kc = jnp.einsum('bjpgd,pd->bjgd', kvb[:, :, :, 0], w_cmp[0])
    vc = jnp.einsum('bjpgd,pd->bjgd', kvb[:, :, :, 1], w_cmp[1])
    slc = jnp.transpose(kvb[:, :, :, 2:4], (0, 4, 1, 2, 3, 5))
    dsa_kv = kv[:, :, 4:6]
    topn = min(NSA_TOPN, nb)
    topk = min(DSA_TOPK_MAX, L // 4)
    qb = t if t <= Q_BLOCK else Q_BLOCK
    nqb = t // qb

    def blocks(a):
        return jnp.moveaxis(a.reshape((bn, nqb, qb) + a.shape[2:]), 1, 0)

    def body(args):
        qn_b, g_b, qd_b, qi_b, wi_b, start = args
        q_pos = p_len + start + jnp.arange(qb)
        kw = lax.dynamic_slice_in_dim(win_all, start, wp + qb, axis=1)
        kw_pos = p_len - wp + start + jnp.arange(wp + qb)
        return _odd_attend(qn_b, g_b, qd_b, qi_b, wi_b, q_pos, kc, vc, slc, dsa_kv, kidx, kw, kw_pos, topn, topk)

    starts = jnp.arange(nqb, dtype=jnp.int32) * qb
    o = lax.map(body, (blocks(qn), blocks(gates), blocks(qd), blocks(qi), blocks(wi), starts))
    o = jnp.moveaxis(o, 0, 1).reshape(bn, t, 2 * HALF_W)
    new_win = win_all[:, win_all.shape[1] - keep:]
    return o, kv_new, ki, new_win


def _pad_cols(w, n):
    return jnp.pad(w, ((0, 0), (0, n - w.shape[1])))


def kernel(x_prompt, x_sample, state_conv, state_C, state_n, state_m, cache_kv, cache_idx, cache_win, page_table,
           w_in_e, gate_b_e, conv_w, conv_b, w_out_e, w_in_o, w_cmp, w_out_o, w_up, w_down, ln_g, ln_b):
    f32, bf16 = jnp.float32, jnp.bfloat16
    bp, tp, d = x_prompt.shape
    dbs = x_sample.shape[0]
    past = page_table.shape[1] * cache_kv.shape[2]
    keep = cache_win.shape[2]
    yp = x_prompt.reshape(bp * tp, d)
    ys = x_sample.reshape(dbs, d)
    tm_p, tm_s = 512, dbs
    outs = {}
    for layer in range(DEPTH):
        i = layer // 2
        if layer % 2 == 0:
            n_pad = _round_up(E_COLS, LANES)
            w_in = _pad_cols(w_in_e[i], n_pad).astype(bf16)
            zs = _proj(ys, w_in, tm_s)[:, :E_COLS].reshape(dbs, 1, E_COLS)
            ya, qkv, og, gts, tails = _even_proj(yp, w_in, conv_w[i], conv_b[i], tm_p, tp)
            yb, c_fin, n_fin, m_fin = _mlstm_prompt(qkv, og, gts, gate_b_e[i], bp, tp)
            cvp = tails.reshape(bp, tp // tm_p, CONV_TAIL, HALF_W)[:, -1, CONV_TAIL - (CONV_W - 1):]
            ms, cvs, sts = _even_mixer(zs, gate_b_e[i], conv_w[i], conv_b[i], state_conv[i],
                                       (state_C[i], state_n[i], state_m[i]), 1)
            outs['conv'] = (cvp[None], cvs[None])
            outs['c'] = (c_fin[None], sts[0][None])
            outs['n'] = (n_fin[None], sts[1][None])
            outs['m'] = (m_fin[:, :, 0][None], sts[2][None])
            w_out = w_out_e[i].astype(bf16)
        else:
            n_pad = _round_up(O_COLS, LANES)
            w_in = _pad_cols(w_in_o[i], n_pad).astype(bf16)
            w_packed = _pack_w_in_o(w_in_o[i]).astype(bf16)
            wc2 = jnp.concatenate([w_cmp[i], w_cmp[i]], axis=-1)
            zq, kvp, kvb, winp, miscp = _odd_proj(yp, w_packed, tm_p)
            mp = _odd_attn_prompt(zq, miscp, kvb, wc2, bp, tp)
            kvp = kvp.reshape(bp, tp, KV_SLOTS, KV_GROUPS, HEAD_DIM)
            ixp = miscp[:, :IDX_DIM].reshape(bp, tp, IDX_DIM)
            wnp = winp.reshape(bp, tp, 2, KV_GROUPS, HEAD_DIM)[:, tp - keep:]
            zq_s, kvs, kvb_s, win_s, misc_s = _odd_proj(ys, w_packed, tm_s)
            ms, wns = _odd_attn_sample(zq_s, misc_s, kvb_s, win_s, cache_kv[i], cache_idx[i],
                                       cache_win[i].reshape(dbs, keep, 2 * LANES), page_table, wc2)
            kvs = kvs.reshape(dbs, 1, KV_SLOTS, KV_GROUPS, HEAD_DIM)
            ixs = misc_s[:, :IDX_DIM].reshape(dbs, 1, IDX_DIM)
            wns = wns.reshape(dbs, keep, 2, KV_GROUPS, HEAD_DIM)
            outs['kv'] = (kvp[None], kvs[None])
            outs['idx'] = (ixp[None], ixs[None])
            outs['win'] = (wnp[None], wns[None])
            w_out = w_out_o[i].astype(bf16)
        if layer % 2 == 0:
            yp = _outproj2_ln(ya, yb.reshape(bp * tp, HALF_W), w_out, yp, ln_g[layer, 0], ln_b[layer, 0], tm_p)
        else:
            yp = _outproj_ln(mp, w_out, yp, ln_g[layer, 0], ln_b[layer, 0], tm_p)
        ys = _outproj_ln(ms.reshape(dbs, d), w_out, ys, ln_g[layer, 0], ln_b[layer, 0], tm_s)
        wu, wd = w_up[layer].astype(bf16), w_down[layer].astype(bf16)
        yp = _mlp_ln(yp, wu, wd, ln_g[layer, 1], ln_b[layer, 1], tm_p, 1024)
        ys = _mlp_ln(ys, wu, wd, ln_g[layer, 1], ln_b[layer, 1], tm_s, 1024)
    return (yp.reshape(bp, tp, d), ys.reshape(dbs, 1, d),
            outs['conv'][0], outs['conv'][1], outs['c'][0], outs['c'][1],
            outs['n'][0], outs['n'][1], outs['m'][0], outs['m'][1],
            outs['kv'][0], outs['kv'][1], outs['idx'][0], outs['idx'][1],
            outs['win'][0], outs['win'][1])
```

```python
import functools

import jax
import jax.numpy as jnp
import numpy as np
from jax import lax
from jax.experimental import pallas as pl
from jax.experimental.pallas import tpu as pltpu

D_MODEL = 1024
DEPTH = 2
HALF_W = 512
D_FF = 4096
CONV_W = 3
ML_HEADS = 4
ML_DIM = 128
ML_CHUNK = 64
HEAD_DIM = 64
KV_GROUPS = 2
NSA_HEADS = 8
DSA_HEADS = 8
BLK = 64
NSA_TOPN = 8
WINDOW = 256
IDX_HEADS = 4
IDX_DIM = 32
DSA_TOPK_MAX = 256
Q_BLOCK = 128
KV_SLOTS = 6
FORCE_SCORE = 1e4
ALPHA = (2.0 * DEPTH) ** 0.25
LN_EPS = 1e-5
E_SIZES = (512, 512, 512, 512, 512, 512, 512, 4, 4)
O_SIZES = (512, 256, 256, 256, 24, 512, 256, 128, 32, 4)
E_COLS = sum(E_SIZES)
O_COLS = sum(O_SIZES)
LANES = 128
VMEM_LIMIT = 48 * 1024 * 1024


def _round_up(n, m):
    return -(-n // m) * m


def _proj_kernel(x_ref, w_ref, o_ref):
    o_ref[...] = jnp.dot(x_ref[...].astype(jnp.bfloat16), w_ref[...],
                         preferred_element_type=jnp.float32)


def _proj(x, w_bf16, tm):
    m, k = x.shape
    n = w_bf16.shape[1]
    tn = n
    for cand in (1024, 768, 512, 256, 128):
        if n % cand == 0:
            tn = cand
            break
    return pl.pallas_call(
        _proj_kernel,
        grid=(n // tn, m // tm),
        in_specs=[pl.BlockSpec((tm, k), lambda j, i: (i, 0)),
                  pl.BlockSpec((k, tn), lambda j, i: (0, j))],
        out_specs=pl.BlockSpec((tm, tn), lambda j, i: (i, j)),
        out_shape=jax.ShapeDtypeStruct((m, n), jnp.float32),
        compiler_params=pltpu.CompilerParams(
            dimension_semantics=("arbitrary", "arbitrary"), vmem_limit_bytes=VMEM_LIMIT),
        name="proj",
    )(x, w_bf16)


def _layer_norm_rows(v, g, b):
    mu = jnp.mean(v, axis=-1, keepdims=True)
    d = v - mu
    var = jnp.mean(d * d, axis=-1, keepdims=True)
    return d * lax.rsqrt(var + LN_EPS) * g + b


def _outproj_ln_kernel(y_ref, w_ref, x_ref, g_ref, b_ref, o_ref):
    mix = jnp.dot(y_ref[...].astype(jnp.bfloat16), w_ref[...], preferred_element_type=jnp.float32)
    o_ref[...] = _layer_norm_rows(ALPHA * x_ref[...] + mix, g_ref[...], b_ref[...])


def _outproj_ln(y, w_bf16, x, g, b, tm):
    m, d = x.shape
    k = y.shape[1]
    return pl.pallas_call(
        _outproj_ln_kernel,
        grid=(m // tm,),
        in_specs=[pl.BlockSpec((tm, k), lambda i: (i, 0)),
                  pl.BlockSpec((k, d), lambda i: (0, 0)),
                  pl.BlockSpec((tm, d), lambda i: (i, 0)),
                  pl.BlockSpec((1, d), lambda i: (0, 0)),
                  pl.BlockSpec((1, d), lambda i: (0, 0))],
        out_specs=pl.BlockSpec((tm, d), lambda i: (i, 0)),
        out_shape=jax.ShapeDtypeStruct((m, d), jnp.float32),
        compiler_params=pltpu.CompilerParams(
            dimension_semantics=("arbitrary",), vmem_limit_bytes=VMEM_LIMIT),
        name="outproj_ln",
    )(y, w_bf16, x, g.reshape(1, d), b.reshape(1, d))


def _mlp_ln_kernel(x_ref, wu_ref, wd_ref, g_ref, b_ref, o_ref, acc_ref):
    f = pl.program_id(1)

    @pl.when(f == 0)
    def _():
        acc_ref[...] = jnp.zeros_like(acc_ref)

    up = jnp.dot(x_ref[...].astype(jnp.bfloat16), wu_ref[...], preferred_element_type=jnp.float32)
    act = jnp.square(jnp.maximum(up, 0.0))
    acc_ref[...] += jnp.dot(act.astype(jnp.bfloat16), wd_ref[...], preferred_element_type=jnp.float32)

    @pl.when(f == pl.num_programs(1) - 1)
    def _():
        o_ref[...] = _layer_norm_rows(ALPHA * x_ref[...] + acc_ref[...], g_ref[...], b_ref[...])


def _mlp_ln(x, wu_bf16, wd_bf16, g, b, tm, tf):
    m, d = x.shape
    ff = wu_bf16.shape[1]
    return pl.pallas_call(
        _mlp_ln_kernel,
        grid=(m // tm, ff // tf),
        in_specs=[pl.BlockSpec((tm, d), lambda i, f: (i, 0)),
                  pl.BlockSpec((d, tf), lambda i, f: (0, f)),
                  pl.BlockSpec((tf, d), lambda i, f: (f, 0)),
                  pl.BlockSpec((1, d), lambda i, f: (0, 0)),
                  pl.BlockSpec((1, d), lambda i, f: (0, 0))],
        out_specs=pl.BlockSpec((tm, d), lambda i, f: (i, 0)),
        out_shape=jax.ShapeDtypeStruct((m, d), jnp.float32),
        scratch_shapes=[pltpu.VMEM((tm, d), jnp.float32)],
        compiler_params=pltpu.CompilerParams(
            dimension_semantics=("arbitrary", "arbitrary"), vmem_limit_bytes=VMEM_LIMIT),
        name="mlp_ln",
    )(x, wu_bf16, wd_bf16, g.reshape(1, d), b.reshape(1, d))


E_QKV = 3 * HALF_W
E_OG = 6 * HALF_W
E_GATE = 7 * HALF_W
E_PACKED = 7 * HALF_W + LANES
CONV_TAIL = 8


def _even_proj_kernel(x_ref, w_ref, cw_ref, cb_ref, ya_ref, qkv_ref, og_ref, gate_ref, tail_ref, carry_ref,
                      *, tiles_per_seq):
    f32, bf16 = jnp.float32, jnp.bfloat16
    tm = x_ref.shape[0]

    @pl.when(pl.program_id(0) % tiles_per_seq == 0)
    def _():
        carry_ref[...] = jnp.zeros_like(carry_ref)

    z = jnp.dot(x_ref[...].astype(bf16), w_ref[...], preferred_element_type=f32)
    u = z[:, 2 * HALF_W:3 * HALF_W] * z[:, 0:HALF_W]
    prev = carry_ref[...]
    row = lax.broadcasted_iota(jnp.int32, (tm, 1), 0)
    conv = cb_ref[...] + cw_ref[CONV_W - 1:CONV_W, :] * u
    for back in range(1, CONV_W):
        shifted = pltpu.roll(u, back, 0)
        for r in range(back):
            shifted = jnp.where(row == r, prev[CONV_TAIL - back + r:CONV_TAIL - back + r + 1, :], shifted)
        conv = conv + cw_ref[CONV_W - 1 - back:CONV_W - back, :] * shifted
    ya_ref[...] = (z[:, HALF_W:2 * HALF_W] * conv).astype(bf16)
    tail = u[tm - CONV_TAIL:]
    carry_ref[...] = tail
    tail_ref[0] = tail
    qkv_ref[:, 0:HALF_W] = z[:, E_QKV:E_QKV + HALF_W].astype(bf16)
    qkv_ref[:, HALF_W:2 * HALF_W] = (z[:, E_QKV + HALF_W:E_QKV + 2 * HALF_W] * (ML_DIM ** -0.5)).astype(bf16)
    qkv_ref[:, 2 * HALF_W:3 * HALF_W] = z[:, E_QKV + 2 * HALF_W:E_OG].astype(bf16)
    og_ref[...] = z[:, E_OG:E_GATE]
    gate_ref[...] = z[:, E_GATE:E_PACKED]


def _even_proj(x, w_bf16, conv_w, conv_b, tm, seq_len):
    m, k = x.shape
    widths = (HALF_W, 3 * HALF_W, HALF_W, LANES)
    dtypes = (jnp.bfloat16, jnp.bfloat16, jnp.float32, jnp.float32)
    kern = functools.partial(_even_proj_kernel, tiles_per_seq=seq_len // tm)
    return pl.pallas_call(
        kern,
        grid=(m // tm,),
        in_specs=[pl.BlockSpec((tm, k), lambda i: (i, 0)),
                  pl.BlockSpec((k, E_PACKED), lambda i: (0, 0)),
                  pl.BlockSpec((CONV_W, HALF_W), lambda i: (0, 0)),
                  pl.BlockSpec((1, HALF_W), lambda i: (0, 0))],
        out_specs=[pl.BlockSpec((tm, n), lambda i: (i, 0)) for n in widths]
        + [pl.BlockSpec((1, CONV_TAIL, HALF_W), lambda i: (i, 0, 0))],
        out_shape=[jax.ShapeDtypeStruct((m, n), dt) for n, dt in zip(widths, dtypes)]
        + [jax.ShapeDtypeStruct((m // tm, CONV_TAIL, HALF_W), jnp.float32)],
        scratch_shapes=[pltpu.VMEM((CONV_TAIL, HALF_W), jnp.float32)],
        compiler_params=pltpu.CompilerParams(
            dimension_semantics=("arbitrary",), vmem_limit_bytes=VMEM_LIMIT),
        name="even_proj",
    )(x, w_bf16, conv_w, conv_b.reshape(1, HALF_W))


def _mlstm_kernel(qkv_ref, og_ref, gate_ref, gb_ref, yb_ref, c_ref, n_ref, m_ref):
    f32, bf16 = jnp.float32, jnp.bfloat16
    L = qkv_ref.shape[1]

    @pl.when(pl.program_id(1) == 0)
    def _():
        c_ref[...] = jnp.zeros_like(c_ref)
        n_ref[...] = jnp.zeros_like(n_ref)
        m_ref[...] = jnp.zeros_like(m_ref)

    pre = gate_ref[0] + gb_ref[...]
    lf = jax.nn.log_sigmoid(pre)
    row = lax.broadcasted_iota(jnp.int32, (L, L), 0)
    col = lax.broadcasted_iota(jnp.int32, (L, L), 1)
    causal = col <= row
    tri = jnp.where(causal, 1.0, 0.0).astype(f32)
    b_all = jnp.dot(tri, lf, preferred_element_type=f32, precision=lax.Precision.HIGHEST)
    pre_t = pre.T
    b_t = b_all.T
    for h in range(ML_HEADS):
        q = qkv_ref[0, :, h * ML_DIM:(h + 1) * ML_DIM]
        k = qkv_ref[0, :, HALF_W + h * ML_DIM:HALF_W + (h + 1) * ML_DIM]
        v = qkv_ref[0, :, 2 * HALF_W + h * ML_DIM:2 * HALF_W + (h + 1) * ML_DIM]
        ig_col = pre[:, h:h + 1]
        b_col = b_all[:, ML_HEADS + h:ML_HEADS + h + 1]
        a_row = pre_t[h:h + 1, :] - b_t[ML_HEADS + h:ML_HEADS + h + 1, :]
        m_st = m_ref[0, h:h + 1, 0:1]
        c_st = c_ref[0, h]
        n_st = n_ref[0, h:h + 1, :]
        dmat = jnp.where(causal, b_col + a_row, -jnp.inf)
        inter = b_col + m_st
        m_t = jnp.maximum(inter, jnp.max(dmat, axis=-1, keepdims=True))
        w_intra = jnp.exp(dmat - m_t)
        w_inter = jnp.exp(inter - m_t)
        s = _dot_nt(q, k) * w_intra
        num = jnp.dot(s.astype(bf16), v, preferred_element_type=f32) + w_inter * _dot_nt(q, c_st.astype(bf16))
        den = jnp.sum(s, axis=-1, keepdims=True) + w_inter * jnp.sum(q.astype(f32) * n_st, axis=-1, keepdims=True)
        hs = num * (1.0 / jnp.maximum(jnp.abs(den), jnp.exp(-m_t)))
        m_new = m_t[L - 1:L, :]
        b_last = b_col[L - 1:L, :]
        w_state = jnp.exp(b_last - b_col + ig_col - m_new)
        decay = jnp.exp(b_last + m_st - m_new)
        vw = (v.astype(f32) * w_state).astype(bf16)
        c_ref[0, h] = decay * c_st + lax.dot_general(vw, k, (((0,), (0,)), ((), ())), preferred_element_type=f32)
        n_ref[0, h:h + 1, :] = decay * n_st + jnp.sum(k.astype(f32) * w_state, axis=0, keepdims=True)
        m_ref[0, h:h + 1, :] = jnp.broadcast_to(m_new, (1, LANES))
        og = og_ref[0, :, h * ML_DIM:(h + 1) * ML_DIM]
        yb_ref[0, :, h * ML_DIM:(h + 1) * ML_DIM] = (jax.nn.sigmoid(og) * hs).astype(bf16)


def _mlstm_prompt(qkv, og, gates, gate_b, bsz, t_len, chunk=128):
    gb = jnp.pad(gate_b, (0, LANES - gate_b.shape[0])).reshape(1, LANES)
    return pl.pallas_call(
        _mlstm_kernel,
        grid=(bsz, t_len // chunk),
        in_specs=[pl.BlockSpec((1, chunk, 3 * HALF_W), lambda b, c: (b, c, 0)),
                  pl.BlockSpec((1, chunk, HALF_W), lambda b, c: (b, c, 0)),
                  pl.BlockSpec((1, chunk, LANES), lambda b, c: (b, c, 0)),
                  pl.BlockSpec((1, LANES), lambda b, c: (0, 0))],
        out_specs=[pl.BlockSpec((1, chunk, HALF_W), lambda b, c: (b, c, 0)),
                   pl.BlockSpec((1, ML_HEADS, ML_DIM, ML_DIM), lambda b, c: (b, 0, 0, 0)),
                   pl.BlockSpec((1, ML_HEADS, ML_DIM), lambda b, c: (b, 0, 0)),
                   pl.BlockSpec((1, ML_HEADS, LANES), lambda b, c: (b, 0, 0))],
        out_shape=[jax.ShapeDtypeStruct((bsz, t_len, HALF_W), jnp.bfloat16),
                   jax.ShapeDtypeStruct((bsz, ML_HEADS, ML_DIM, ML_DIM), jnp.float32),
                   jax.ShapeDtypeStruct((bsz, ML_HEADS, ML_DIM), jnp.float32),
                   jax.ShapeDtypeStruct((bsz, ML_HEADS, LANES), jnp.float32)],
        compiler_params=pltpu.CompilerParams(
            dimension_semantics=("arbitrary", "arbitrary"), vmem_limit_bytes=VMEM_LIMIT),
        name="mlstm_prompt",
    )(qkv.reshape(bsz, t_len, 3 * HALF_W), og.reshape(bsz, t_len, HALF_W),
      gates.reshape(bsz, t_len, LANES), gb)


def _outproj2_ln_kernel(ya_ref, yb_ref, w_ref, x_ref, g_ref, b_ref, o_ref):
    mix = jnp.dot(ya_ref[...], w_ref[0:HALF_W, :], preferred_element_type=jnp.float32)
    mix = mix + jnp.dot(yb_ref[...], w_ref[HALF_W:, :], preferred_element_type=jnp.float32)
    o_ref[...] = _layer_norm_rows(ALPHA * x_ref[...] + mix, g_ref[...], b_ref[...])


def _outproj2_ln(ya, yb, w_bf16, x, g, b, tm):
    m, d = x.shape
    return pl.pallas_call(
        _outproj2_ln_kernel,
        grid=(m // tm,),
        in_specs=[pl.BlockSpec((tm, HALF_W), lambda i: (i, 0)),
                  pl.BlockSpec((tm, HALF_W), lambda i: (i, 0)),
                  pl.BlockSpec((2 * HALF_W, d), lambda i: (0, 0)),
                  pl.BlockSpec((tm, d), lambda i: (i, 0)),
                  pl.BlockSpec((1, d), lambda i: (0, 0)),
                  pl.BlockSpec((1, d), lambda i: (0, 0))],
        out_specs=pl.BlockSpec((tm, d), lambda i: (i, 0)),
        out_shape=jax.ShapeDtypeStruct((m, d), jnp.float32),
        compiler_params=pltpu.CompilerParams(
            dimension_semantics=("arbitrary",), vmem_limit_bytes=VMEM_LIMIT),
        name="outproj2_ln",
    )(ya, yb, w_bf16, x, g.reshape(1, d), b.reshape(1, d))


O_Q = 0
O_QD = NSA_HEADS * LANES
O_KV = 2048
O_WIN = 2816
O_QI = 3072
O_MISC = 3200
O_PACKED = 3328
ZQ_QI = 2048
MISC_WI = IDX_DIM
MISC_GATE = IDX_DIM + IDX_HEADS
NEG_BIG = -(2.0 ** 100)
INT_MIN = -(2 ** 31)
ALIBI = tuple(float(2.0 ** (-8.0 * (h + 1) / NSA_HEADS)) for h in range(NSA_HEADS))


def _pack_w_in_o(w):
    qn, kvc, kvs, kvw, gates, qd, kvd, qi, ki, wi = _split(w, O_SIZES)
    scale = HEAD_DIM ** -0.5
    rows = w.shape[0]

    def spread(q):
        q = (q * scale).reshape(rows, NSA_HEADS, HEAD_DIM)
        z = jnp.zeros_like(q)
        low = (jnp.arange(NSA_HEADS) < NSA_HEADS // KV_GROUPS)[None, :, None]
        return jnp.concatenate([jnp.where(low, q, z), jnp.where(low, z, q)], axis=-1).reshape(rows, NSA_HEADS * LANES)

    pad = jnp.zeros((rows, O_PACKED - O_MISC - IDX_DIM - IDX_HEADS - 3 * NSA_HEADS), w.dtype)
    return jnp.concatenate([spread(qn), spread(qd), kvc, kvs, kvd, kvw, qi, ki, wi, gates, pad], axis=1)


def _odd_proj_kernel(x_ref, w_ref, wkvt_ref, zq_ref, kvt_ref, kvb_ref, win_ref, misc_ref):
    xb = x_ref[...].astype(jnp.bfloat16)
    z = jnp.dot(xb, w_ref[...], preferred_element_type=jnp.float32)
    zq_ref[:, 0:O_KV] = z[:, 0:O_KV].astype(jnp.bfloat16)
    zq_ref[:, ZQ_QI:ZQ_QI + LANES] = z[:, O_QI:O_MISC].astype(jnp.bfloat16)
    kvt_ref[0] = _dot_nt(wkvt_ref[...], xb)
    kvb_ref[...] = z[:, O_KV:O_QI].astype(jnp.bfloat16)
    win_ref[...] = z[:, O_WIN:O_QI]
    misc_ref[...] = z[:, O_MISC:O_PACKED]


def _odd_proj(x, w_packed_bf16, tm, seq_len):
    m, k = x.shape
    n_kv = O_WIN - O_KV
    widths = (O_KV + LANES, O_QI - O_KV, O_QI - O_WIN, LANES)
    dtypes = (jnp.bfloat16, jnp.bfloat16, jnp.float32, jnp.float32)
    row_spec = lambda n: pl.BlockSpec((tm, n), lambda i: (i, 0))
    tiles = seq_len // tm
    out_specs = [row_spec(widths[0]), pl.BlockSpec((1, n_kv, tm), lambda i: (i // tiles, 0, i % tiles))]
    out_specs += [row_spec(n) for n in widths[1:]]
    out_shape = [jax.ShapeDtypeStruct((m, widths[0]), dtypes[0]),
                 jax.ShapeDtypeStruct((m // seq_len, n_kv, seq_len), jnp.float32)]
    out_shape += [jax.ShapeDtypeStruct((m, n), dt) for n, dt in zip(widths[1:], dtypes[1:])]
    return pl.pallas_call(
        _odd_proj_kernel,
        grid=(m // tm,),
        in_specs=[pl.BlockSpec((tm, k), lambda i: (i, 0)),
                  pl.BlockSpec((k, O_PACKED), lambda i: (0, 0)),
                  pl.BlockSpec((n_kv, k), lambda i: (0, 0))],
        out_specs=out_specs,
        out_shape=out_shape,
        compiler_params=pltpu.CompilerParams(
            dimension_semantics=("arbitrary",), vmem_limit_bytes=VMEM_LIMIT),
        name="odd_proj",
    )(x, w_packed_bf16, w_packed_bf16[:, O_KV:O_WIN].T)


def _dot_nt(a, b):
    return lax.dot_general(a, b, (((1,), (1,)), ((), ())), preferred_element_type=jnp.float32)


def _flash_heads(q_heads, slopes, group, kv_ref, k_blk, v_blk, tk, lo, hi, bias_fn, skip_fn=None, start_fn=None):
    f32, bf16 = jnp.float32, jnp.bfloat16
    if start_fn is None:
        start_fn = lambda kt: kt * tk
    tq = q_heads[0].shape[0]
    n_heads = len(q_heads)
    lane = lax.broadcasted_iota(jnp.int32, (1, LANES), 1)
    own_half = (lane // HEAD_DIM) == group
    q_all = jnp.concatenate(q_heads, axis=0)
    slope_col = jnp.concatenate([jnp.full((tq, 1), sl, f32) for sl in slopes], axis=0)

    def tile(kt, carry):
        m, acc = carry
        start = pl.multiple_of(start_fn(kt), LANES)
        rows = pl.ds(start, tk)
        k = kv_ref[0, rows, k_blk * LANES:(k_blk + 1) * LANES]
        v = kv_ref[0, rows, v_blk * LANES:(v_blk + 1) * LANES]
        v = jnp.where(own_half, v, jnp.ones_like(v))
        sp = (start + lax.broadcasted_iota(jnp.int32, (1, tk), 1)).astype(f32)
        b = bias_fn(kt)
        s = _dot_nt(q_all, k) + slope_col * sp + jnp.concatenate([b] * n_heads, axis=0)
        m_new = jnp.maximum(m, jnp.max(s, axis=-1, keepdims=True))
        p = jnp.exp(s - m_new)
        acc = jnp.exp(m - m_new) * acc + jnp.dot(p.astype(bf16), v, preferred_element_type=f32)
        return m_new, acc

    def body(kt, carry):
        if skip_fn is None:
            return tile(kt, carry)
        return lax.cond(skip_fn(kt), lambda c: c, functools.partial(tile, kt), carry)

    m0 = jnp.full((n_heads * tq, 1), -jnp.inf, f32)
    a0 = jnp.zeros((n_heads * tq, LANES), f32)
    _, acc = lax.fori_loop(lo, hi, body, (m0, a0))
    return [acc[r * tq:(r + 1) * tq] for r in range(n_heads)]


def _odd_attn_kernel(zq_ref, miscq_ref, kvb_ref, misck_ref, wc_ref, o_ref,
                     kc_ref, vc_ref, ki_ref, key_ref, dbias_ref, flag_ref, *, t_len, tq, tk, topk):
    f32, bf16, i32 = jnp.float32, jnp.bfloat16, jnp.int32
    nb = t_len // BLK
    qi_blk = pl.program_id(1)
    q0 = qi_blk * tq

    @pl.when(qi_blk == 0)
    def _():
        ck = kvb_ref[0, :, 0:LANES].astype(f32).reshape(nb, BLK, LANES)
        kc_ref[...] = jnp.sum(ck * wc_ref[0][None], axis=1).astype(bf16)
        cv = kvb_ref[0, :, LANES:2 * LANES].astype(f32).reshape(nb, BLK, LANES)
        vc = jnp.sum(cv * wc_ref[1][None], axis=1)
        for g in range(KV_GROUPS):
            half = vc[:, g * HEAD_DIM:(g + 1) * HEAD_DIM]
            vc_ref[g] = jnp.concatenate([half, half], axis=1).astype(bf16)
        ki_ref[...] = misck_ref[0, :, 0:IDX_DIM].astype(bf16)

    misc = miscq_ref[0]
    gates = jax.nn.sigmoid(misc[:, MISC_GATE:MISC_GATE + 3 * NSA_HEADS])
    t_col = q0 + lax.broadcasted_iota(i32, (tq, 1), 0)
    lane = lax.broadcasted_iota(i32, (1, LANES), 1)
    low_half = lane < HEAD_DIM
    heads_per_group = NSA_HEADS // KV_GROUPS

    def q_head(base, h):
        return zq_ref[0, :, base + h * LANES: base + (h + 1) * LANES]

    def pair(even, odd):
        return jnp.where(low_half, even, odd)

    def normalised_pair(acc_even, acc_odd, g):
        outs = []
        for parity, acc in enumerate((acc_even, acc_odd)):
            swapped = pltpu.roll(acc, HEAD_DIM, 1)
            outs.append(acc * (1.0 / swapped) if parity == g else swapped * (1.0 / acc))
        return pair(outs[0], outs[1])

    def gate_pair(j, c):
        a = gates[:, (2 * j) * 3 + c:(2 * j) * 3 + c + 1]
        b = gates[:, (2 * j + 1) * 3 + c:(2 * j + 1) * 3 + c + 1]
        return jnp.where(low_half, a, b)

    jb = lax.broadcasted_iota(i32, (1, nb), 1)
    dist_c = t_col - (jb * BLK + (BLK - 1))
    mask_c = dist_c >= 0
    dist_cf = dist_c.astype(f32)
    cur = t_col // BLK
    forced = (jb == 0) | (jb == cur) | (jb == cur - 1)
    admissible = jb <= cur
    jb_full = lax.broadcasted_iota(i32, (tq, nb), 1)
    o_cmp, sel_bias, sel_any = [], [], []
    for g in range(KV_GROUPS):
        imp = jnp.zeros((tq, nb), f32)
        for r in range(heads_per_group):
            h = g * heads_per_group + r
            s = _dot_nt(q_head(O_Q, h), kc_ref[...]) - ALIBI[h] * dist_cf
            s = jnp.where(mask_c, s, NEG_BIG)
            m = jnp.max(s, axis=-1, keepdims=True)
            e = jnp.where(mask_c, jnp.exp(s - m), 0.0)
            p = e * (1.0 / jnp.maximum(jnp.sum(e, axis=-1, keepdims=True), 1e-30))
            imp = imp + p
            o_cmp.append(jnp.dot(p.astype(bf16), vc_ref[g], preferred_element_type=f32))
        imp = jnp.where(forced, FORCE_SCORE, imp)
        imp = jnp.where(admissible, imp, -jnp.inf)
        sel = jnp.zeros((tq, nb), f32)
        for _ in range(min(NSA_TOPN, nb)):
            m = jnp.max(imp, axis=-1, keepdims=True)
            first = jnp.min(jnp.where(imp == m, jb_full, nb), axis=-1, keepdims=True)
            hit = jb_full == first
            sel = jnp.where(hit & (m > -jnp.inf), 1.0, sel)
            imp = jnp.where(hit, -jnp.inf, imp)
        sel_bias.append(jnp.where(sel > 0.0, 0.0, NEG_BIG).astype(bf16))
        sel_any.append(jnp.max(sel, axis=0, keepdims=True))

    n_sel_tiles = (q0 + tq + tk - 1) // tk
    blocks_per_tile = tk // BLK

    for g in range(KV_GROUPS):
        for j in range(nb // blocks_per_tile):
            hit = jnp.max(sel_any[g][:, j * blocks_per_tile:(j + 1) * blocks_per_tile])
            flag_ref[g, j] = (hit > 0.0).astype(i32)

    def sel_skip_fn(g):
        return lambda kt: flag_ref[g, kt] == 0

    def causal_bias(start, width):
        sp = start + lax.broadcasted_iota(i32, (1, width), 1)
        return jnp.where(sp <= t_col, 0.0, NEG_BIG)

    def sel_bias_fn(g):
        def fn(kt):
            row = lax.broadcasted_iota(i32, (nb, tk), 0)
            col = lax.broadcasted_iota(i32, (nb, tk), 1)
            expand = jnp.where(row == kt * blocks_per_tile + col // BLK, 1.0, 0.0).astype(bf16)
            return jnp.dot(sel_bias[g], expand, preferred_element_type=f32) + causal_bias(kt * tk, tk)
        return fn

    tkw = min(WINDOW + tq, t_len)
    win_start = jnp.clip(q0 - WINDOW, 0, t_len - tkw)

    def win_bias_fn(kt):
        dist = t_col - (win_start + lax.broadcasted_iota(i32, (1, tkw), 1))
        return jnp.where((dist >= 0) & (dist < WINDOW), 0.0, NEG_BIG)

    qi_heads = [zq_ref[0, :, ZQ_QI + h * IDX_DIM: ZQ_QI + (h + 1) * IDX_DIM] for h in range(IDX_HEADS)]
    wi_cols = [misc[:, MISC_WI + h: MISC_WI + h + 1] for h in range(IDX_HEADS)]

    def index_tile(kt, carry):
        start = pl.multiple_of(kt * tk, tk)
        kik = ki_ref[pl.ds(start, tk), :]
        tot = jnp.zeros((tq, tk), f32)
        for h in range(IDX_HEADS):
            sc = _dot_nt(qi_heads[h], kik) * (IDX_DIM ** -0.5)
            tot = tot + jnp.maximum(sc, 0.0) * wi_cols[h]
        tot = tot * (IDX_HEADS ** -0.5)
        tot = jnp.where(tot == 0.0, 0.0, tot)
        bits = lax.bitcast_convert_type(tot, i32)
        key = jnp.where(bits < 0, bits ^ jnp.int32(0x7FFFFFFF), bits)
        sp = start + lax.broadcasted_iota(i32, (1, tk), 1)
        key_ref[kt] = jnp.where(sp <= t_col, key, INT_MIN)
        return carry

    lax.fori_loop(0, n_sel_tiles, index_tile, 0)

    def count_ge(trial):
        def body(kt, acc):
            c = (key_ref[kt] >= trial).astype(i32)
            for j in range(tk // LANES):
                acc = acc + c[:, j * LANES:(j + 1) * LANES]
            return acc
        acc = lax.fori_loop(0, n_sel_tiles, body, jnp.zeros((tq, LANES), i32))
        return jnp.sum(acc, axis=-1, keepdims=True)

    c_pos = count_ge(jnp.zeros((tq, 1), i32))
    v0 = jnp.where(c_pos >= topk, 0, INT_MIN).astype(i32)
    c0 = jnp.where(c_pos >= topk, c_pos, n_sel_tiles * tk)

    def bit_step(i, carry):
        v, cge = carry
        trial = v | lax.shift_left(jnp.int32(1), 30 - i)
        c = count_ge(trial)
        ok = c >= topk
        return jnp.where(ok, trial, v), jnp.where(ok, c, cge)

    v_thr, c_ge = lax.fori_loop(0, 31, bit_step, (v0, c0))
    v_eff = jnp.maximum(v_thr, INT_MIN + 1)

    def write_plain(kt, carry):
        dbias_ref[kt] = jnp.where(key_ref[kt] >= v_eff, 0.0, NEG_BIG)
        return carry

    def plain_path():
        lax.fori_loop(0, n_sel_tiles, write_plain, 0)

    def tie_path():
        need = topk - count_ge(v_thr + 1)

        def count_tie_below(limit):
            def body(kt, acc):
                sp = kt * tk + lax.broadcasted_iota(i32, (1, tk), 1)
                c = ((key_ref[kt] == v_thr) & (sp < limit)).astype(i32)
                for j in range(tk // LANES):
                    acc = acc + c[:, j * LANES:(j + 1) * LANES]
                return acc
            acc = lax.fori_loop(0, n_sel_tiles, body, jnp.zeros((tq, LANES), i32))
            return jnp.sum(acc, axis=-1, keepdims=True)

        n_bits = max(1, (t_len - 1).bit_length())

        def idx_step(i, w):
            trial = w | lax.shift_left(jnp.int32(1), n_bits - 1 - i)
            return jnp.where(count_tie_below(trial) < need, trial, w)

        w_last = lax.fori_loop(0, n_bits, idx_step, jnp.zeros((tq, 1), i32))

        def write_tie(kt, carry):
            sp = kt * tk + lax.broadcasted_iota(i32, (1, tk), 1)
            key = key_ref[kt]
            take = (key > v_thr) | ((key == v_thr) & (sp <= w_last))
            dbias_ref[kt] = jnp.where(take & (key >= v_eff), 0.0, NEG_BIG)
            return carry

        lax.fori_loop(0, n_sel_tiles, write_tie, 0)

    lax.cond(jnp.max(c_ge) > topk, tie_path, plain_path)

    def dsa_bias_fn(kt):
        return dbias_ref[kt]

    for g in range(KV_GROUPS):
        hs = [g * heads_per_group + r for r in range(heads_per_group)]
        slopes = [ALIBI[h] for h in hs]
        qn = [q_head(O_Q, h) for h in hs]
        qd = [q_head(O_QD, h) for h in hs]
        a_s = _flash_heads(qn, slopes, g, kvb_ref, 2, 3, tk, 0, n_sel_tiles, sel_bias_fn(g), sel_skip_fn(g))
        a_w = _flash_heads(qn, slopes, g, kvb_ref, 6, 7, tkw, 0, 1, win_bias_fn, start_fn=lambda kt: win_start)
        a_d = _flash_heads(qd, slopes, g, kvb_ref, 4, 5, tk, 0, n_sel_tiles, dsa_bias_fn)
        for jj in range(heads_per_group // 2):
            j = hs[2 * jj] // 2
            o_n = (gate_pair(j, 0) * pair(o_cmp[2 * j], o_cmp[2 * j + 1])
                   + gate_pair(j, 1) * normalised_pair(a_s[2 * jj], a_s[2 * jj + 1], g)
                   + gate_pair(j, 2) * normalised_pair(a_w[2 * jj], a_w[2 * jj + 1], g))
            o_ref[0, :, j * LANES:(j + 1) * LANES] = o_n.astype(o_ref.dtype)
            o_d = normalised_pair(a_d[2 * jj], a_d[2 * jj + 1], g)
            o_ref[0, :, HALF_W + j * LANES:HALF_W + (j + 1) * LANES] = o_d.astype(o_ref.dtype)


def _odd_attn_prompt(zq, misc, kvb, wc2, bsz, t_len, tq=128, tk=512):
    tk = min(tk, t_len)
    topk = min(DSA_TOPK_MAX, t_len // 4)
    nb = t_len // BLK
    n_kt = t_len // tk
    zq3 = zq.reshape(bsz, t_len, zq.shape[1])
    misc3 = misc.reshape(bsz, t_len, LANES)
    kvb3 = kvb.reshape(bsz, t_len, kvb.shape[1])
    kern = functools.partial(_odd_attn_kernel, t_len=t_len, tq=tq, tk=tk, topk=topk)
    out = pl.pallas_call(
        kern,
        grid=(bsz, t_len // tq),
        in_specs=[pl.BlockSpec((1, tq, zq.shape[1]), lambda b, i: (b, i, 0)),
                  pl.BlockSpec((1, tq, LANES), lambda b, i: (b, i, 0)),
                  pl.BlockSpec((1, t_len, kvb.shape[1]), lambda b, i: (b, 0, 0)),
                  pl.BlockSpec((1, t_len, LANES), lambda b, i: (b, 0, 0)),
                  pl.BlockSpec((2, BLK, LANES), lambda b, i: (0, 0, 0))],
        out_specs=pl.BlockSpec((1, tq, 2 * HALF_W), lambda b, i: (b, i, 0)),
        out_shape=jax.ShapeDtypeStruct((bsz, t_len, 2 * HALF_W), jnp.bfloat16),
        scratch_shapes=[pltpu.VMEM((nb, LANES), jnp.bfloat16),
                        pltpu.VMEM((KV_GROUPS, nb, LANES), jnp.bfloat16),
                        pltpu.VMEM((t_len, IDX_DIM), jnp.bfloat16),
                        pltpu.VMEM((n_kt, tq, tk), jnp.int32),
                        pltpu.VMEM((n_kt, tq, tk), jnp.float32),
                        pltpu.SMEM((KV_GROUPS, n_kt), jnp.int32)],
        compiler_params=pltpu.CompilerParams(
            dimension_semantics=("arbitrary", "arbitrary"), vmem_limit_bytes=VMEM_LIMIT),
        name="odd_attn_prompt",
    )(zq3, misc3, kvb3, misc3, wc2)
    return out.reshape(bsz * t_len, 2 * HALF_W)


PAGE = 128
ISC_ROWS = 24


def _odd_sample_kernel(pt_ref, qn_ref, qd_ref, qi_ref, wi_ref, gate_ref, newb_ref, newki_ref, neww_ref,
                       win_ref, wc_ref, wct_ref, *rest, n_pages, topk, nbp):
    del pt_ref
    f32, bf16, i32 = jnp.float32, jnp.bfloat16, jnp.int32
    hi_prec = lax.Precision.HIGHEST
    kv_refs, idx_refs = rest[:n_pages], rest[n_pages:2 * n_pages]
    o_ref, wout_ref, isc_ref = rest[2 * n_pages:]
    past = n_pages * PAGE
    nb = past // BLK + 1
    row8 = lax.broadcasted_iota(i32, (8, 1), 0)
    lane = lax.broadcasted_iota(i32, (1, LANES), 1)
    slope8 = jnp.zeros((8, 1), f32)
    for h in range(NSA_HEADS):
        slope8 = jnp.where(row8 == h, ALIBI[h], slope8)
    low_group = row8 < NSA_HEADS // KV_GROUPS

    def pick_half(x):
        return jnp.where(low_group, x[:, 0:HEAD_DIM], x[:, HEAD_DIM:2 * HEAD_DIM])

    qn8, qd8 = qn_ref[0], qd_ref[0]
    qi8, wi8 = qi_ref[0], wi_ref[0]
    newb = newb_ref[0]
    gates = jax.nn.sigmoid(gate_ref[0])

    def new_score(q8, k_row):
        return jnp.sum(q8.astype(f32) * k_row.astype(f32), axis=-1, keepdims=True) + slope8 * float(past)

    def feat(p, slot):
        return kv_refs[p][0, slot * LANES:(slot + 1) * LANES, :]

    s_sel, s_dsa = [], []
    key_row = lax.broadcasted_iota(i32, (PAGE, nbp), 0)
    blk_col = lax.broadcasted_iota(i32, (PAGE, nbp), 1)
    kct = jnp.zeros((LANES, nbp), f32)
    vct = jnp.zeros((LANES, nbp), f32)
    for p in range(n_pages):
        pool = jnp.where(blk_col == 2 * p + key_row // BLK, 1.0, 0.0)
        kct = kct + jnp.dot(feat(p, 0) * wct_ref[0], pool, preferred_element_type=f32, precision=hi_prec)
        vct = vct + jnp.dot(feat(p, 1) * wct_ref[1], pool, preferred_element_type=f32, precision=hi_prec)
        pos = (p * PAGE + lane).astype(f32)
        s_sel.append(jnp.dot(qn8, feat(p, 2).astype(bf16), preferred_element_type=f32) + slope8 * pos)
        s_dsa.append(jnp.dot(qd8, feat(p, 4).astype(bf16), preferred_element_type=f32) + slope8 * pos)
        ii = jnp.dot(qi8, idx_refs[p][0].astype(bf16), preferred_element_type=f32) * (IDX_DIM ** -0.5)
        isc_ref[p:p + 1, :] = jnp.sum(jnp.maximum(ii, 0.0) * wi8, axis=0, keepdims=True) * (IDX_HEADS ** -0.5)
    first_row = row8 == 0
    place = jnp.where(first_row & (lax.broadcasted_iota(i32, (8, nbp), 1) == nb - 1), 1.0, 0.0)
    tn = (((0,), (0,)), ((), ()))
    new_k = jnp.where(first_row, newb[:, 0:LANES].astype(f32) * wc_ref[0, 0:1, :], 0.0)
    new_v = jnp.where(first_row, newb[:, LANES:2 * LANES].astype(f32) * wc_ref[1, 0:1, :], 0.0)
    kct = kct + lax.dot_general(new_k, place, tn, preferred_element_type=f32, precision=hi_prec)
    vct = vct + lax.dot_general(new_v, place, tn, preferred_element_type=f32, precision=hi_prec)
    ii_new = jnp.sum(qi8.astype(f32) * newki_ref[0].astype(bf16).astype(f32), axis=-1, keepdims=True) * (IDX_DIM ** -0.5)
    isc_new = jnp.sum(jnp.maximum(ii_new, 0.0) * wi8, axis=0, keepdims=True) * (IDX_HEADS ** -0.5)
    isc_ref[n_pages:n_pages + 1, :] = jnp.where(lane == 0, isc_new, -jnp.inf)
    isc_ref[n_pages + 1:ISC_ROWS, :] = jnp.full((ISC_ROWS - n_pages - 1, LANES), -jnp.inf, f32)

    jb = lax.broadcasted_iota(i32, (1, nbp), 1)
    dist_c = past - (jb * BLK + (BLK - 1))
    mask_c = (dist_c >= 0) & (jb < nb)
    s_c = jnp.dot(qn8, kct.astype(bf16), preferred_element_type=f32) - slope8 * dist_c.astype(f32)
    s_c = jnp.where(mask_c, s_c, NEG_BIG)
    m_c = jnp.max(s_c, axis=-1, keepdims=True)
    e_c = jnp.where(mask_c, jnp.exp(s_c - m_c), 0.0)
    p_c = e_c * (1.0 / jnp.maximum(jnp.sum(e_c, axis=-1, keepdims=True), 1e-30))
    o_c = pick_half(_dot_nt(p_c.astype(bf16), vct.astype(bf16)))
    imp_lo = jnp.sum(jnp.where(low_group, p_c, 0.0), axis=0, keepdims=True)
    imp_hi = jnp.sum(jnp.where(low_group, 0.0, p_c), axis=0, keepdims=True)
    imp = jnp.where(low_group, imp_lo, imp_hi)
    cur = past // BLK
    imp = jnp.where((jb == 0) | (jb == cur) | (jb == cur - 1), FORCE_SCORE, imp)
    imp = jnp.where(jb <= cur, imp, -jnp.inf)
    rank = jnp.zeros((8, nbp), i32)
    for j in range(nb):
        other = imp[:, j:j + 1]
        rank = rank + ((other > imp) | ((other == imp) & (j < jb))).astype(i32)
    sel = jnp.where((rank < min(NSA_TOPN, nb)) & (imp > -jnp.inf), 1.0, 0.0)

    tot = isc_ref[...]
    tot = jnp.where(tot == 0.0, 0.0, tot)
    bits = lax.bitcast_convert_type(tot, i32)
    key = jnp.where(bits < 0, bits ^ jnp.int32(0x7FFFFFFF), bits)
    krow = lax.broadcasted_iota(i32, (ISC_ROWS, LANES), 0)
    kidx = krow * LANES + lax.broadcasted_iota(i32, (ISC_ROWS, LANES), 1)
    key = jnp.where(kidx <= past, key, INT_MIN)

    def total(x):
        return jnp.sum(jnp.sum(x, axis=1, keepdims=True), axis=0, keepdims=True)

    def count_ge(trial):
        return total((key >= trial).astype(i32))

    c_pos = count_ge(jnp.zeros((1, 1), i32))
    v_thr = jnp.where(c_pos >= topk, 0, INT_MIN).astype(i32)
    c_ge = jnp.where(c_pos >= topk, c_pos, ISC_ROWS * LANES)
    for shift, width in ((27, 4), (23, 4), (19, 4), (15, 4), (11, 4), (7, 4), (3, 4), (0, 3)):
        digit = jnp.zeros((1, 1), i32)
        for j in range(1, 1 << width):
            c = count_ge(v_thr | jnp.int32(j << shift))
            ok = c >= topk
            digit = digit + ok.astype(i32)
            c_ge = jnp.where(ok, jnp.minimum(c_ge, c), c_ge)
        v_thr = v_thr | lax.shift_left(digit, jnp.int32(shift))
    v_eff = jnp.maximum(v_thr, INT_MIN + 1)

    def tie_mask():
        need = topk - count_ge(v_thr + 1)
        tie = key == v_thr
        w_last = jnp.zeros((1, 1), i32)
        for bit in range((ISC_ROWS * LANES - 1).bit_length() - 1, -1, -1):
            trial = w_last | jnp.int32(1 << bit)
            below = total((tie & (kidx < trial)).astype(i32))
            w_last = jnp.where(below < need, trial, w_last)
        take = (key > v_thr) | (tie & (kidx <= w_last))
        return jnp.where(take & (key >= v_eff), 1.0, 0.0)

    def plain_mask():
        return jnp.where(key >= v_eff, 1.0, 0.0)

    dsel = lax.cond(c_ge[0, 0] > topk, tie_mask, plain_mask)

    def attend(tiles, masks, s_new, new_ok, v_tile, v_new):
        tiles = [jnp.where(mk, t, NEG_BIG) for t, mk in zip(tiles, masks)]
        s_new = jnp.where(new_ok, s_new, NEG_BIG)
        m = s_new
        for t in tiles:
            m = jnp.maximum(m, jnp.max(t, axis=-1, keepdims=True))
        l = jnp.exp(s_new - m)
        acc = l * v_new.astype(f32)
        for j, t in enumerate(tiles):
            e = jnp.exp(t - m)
            l = l + jnp.sum(e, axis=-1, keepdims=True)
            acc = acc + _dot_nt(e.astype(bf16), v_tile(j))
        return pick_half(acc * (1.0 / l))

    sel_masks = [jnp.where(lane < BLK, sel[:, 2 * p:2 * p + 1], sel[:, 2 * p + 1:2 * p + 2]) > 0.0
                 for p in range(n_pages)]
    o_s = attend(s_sel, sel_masks, new_score(qn8, newb[:, 2 * LANES:3 * LANES]), sel[:, nb - 1:nb] > 0.0,
                 lambda p: feat(p, 3).astype(bf16), newb[:, 3 * LANES:4 * LANES])
    dsa_masks = [dsel[p:p + 1, :] > 0.0 for p in range(n_pages)]
    o_d = attend(s_dsa, dsa_masks, new_score(qd8, newb[:, 4 * LANES:5 * LANES]), dsel[n_pages:n_pages + 1, 0:1] > 0.0,
                 lambda p: feat(p, 5).astype(bf16), newb[:, 5 * LANES:6 * LANES])
    n_win = win_ref.shape[2]
    wlane = lax.broadcasted_iota(i32, (1, n_win), 1)
    wpos = past - n_win + wlane
    s_w = jnp.dot(qn8, win_ref[0, 0:LANES, :].astype(bf16), preferred_element_type=f32) + slope8 * wpos.astype(f32)
    w_ok = (past - wpos < WINDOW) & (wpos >= 0)
    o_w = attend([s_w], [w_ok], new_score(qn8, newb[:, 6 * LANES:7 * LANES]), True,
                 lambda p: win_ref[0, LANES:2 * LANES, :].astype(bf16), newb[:, 7 * LANES:8 * LANES])
    o_ref[0, 0:NSA_HEADS, :] = gates[:, 0:1] * o_c + gates[:, 1:2] * o_s + gates[:, 2:3] * o_w
    o_ref[0, NSA_HEADS:2 * NSA_HEADS, :] = o_d
    last = jnp.where(first_row & (lax.broadcasted_iota(i32, (8, n_win), 1) == n_win - 1), 1.0, 0.0)
    new_w = jnp.where(first_row, neww_ref[0], 0.0)
    placed = lax.dot_general(new_w, last, tn, preferred_element_type=f32, precision=hi_prec)
    wout_ref[0] = jnp.where(wlane == n_win - 1, placed, pltpu.roll(win_ref[0], n_win - 1, 1))


def _odd_attn_sample(zq, misc, kvb, win_new, cache_kv, cache_idx, cache_win, page_table, w_cmp):
    f32, bf16 = jnp.float32, jnp.bfloat16
    s_n, n_pages = page_table.shape
    n_pool = cache_kv.shape[0]
    n_win = cache_win.shape[1]
    past = n_pages * PAGE
    topk = min(DSA_TOPK_MAX, (past + 1) // 4)
    nbp = _round_up(past // BLK + 1, 8)
    wc2 = jnp.concatenate([w_cmp, w_cmp], axis=-1)
    wct = jnp.tile(jnp.swapaxes(w_cmp, 1, 2), (1, KV_GROUPS, PAGE // BLK))
    qn = zq[:, O_Q:O_QD].reshape(s_n, NSA_HEADS, LANES)
    qd = zq[:, O_QD:ZQ_QI].reshape(s_n, DSA_HEADS, LANES)
    qi = jnp.pad(zq[:, ZQ_QI:ZQ_QI + LANES].reshape(s_n, IDX_HEADS, IDX_DIM), ((0, 0), (0, 8 - IDX_HEADS), (0, 0)))
    wi = jnp.pad(misc[:, MISC_WI:MISC_WI + IDX_HEADS].reshape(s_n, IDX_HEADS, 1), ((0, 0), (0, 8 - IDX_HEADS), (0, 0)))
    gts = misc[:, MISC_GATE:MISC_GATE + 3 * NSA_HEADS].reshape(s_n, NSA_HEADS, 3)
    kv_pages = jnp.transpose(cache_kv, (0, 2, 3, 4, 1)).reshape(n_pool, KV_SLOTS * LANES, PAGE)
    idx_pages = jnp.transpose(cache_idx, (0, 2, 1))
    win_t = jnp.transpose(cache_win, (0, 2, 3, 4, 1)).reshape(s_n, 2 * LANES, n_win)
    per_seq = lambda shape: pl.BlockSpec((1,) + shape, lambda b, pt: (b, 0, 0))
    page_spec = lambda rows, p: pl.BlockSpec((1, rows, PAGE), lambda b, pt, p=p: (pt[b, p], 0, 0))
    in_specs = ([per_seq((NSA_HEADS, LANES)), per_seq((DSA_HEADS, LANES)), per_seq((8, IDX_DIM)),
                 per_seq((8, 1)), per_seq((NSA_HEADS, 3)), per_seq((1, 8 * LANES)), per_seq((1, IDX_DIM)),
                 per_seq((1, 2 * LANES)), per_seq((2 * LANES, n_win)),
                 pl.BlockSpec((2, BLK, LANES), lambda b, pt: (0, 0, 0)),
                 pl.BlockSpec((2, LANES, PAGE), lambda b, pt: (0, 0, 0))]
                + [page_spec(KV_SLOTS * LANES, p) for p in range(n_pages)]
                + [page_spec(IDX_DIM, p) for p in range(n_pages)])
    kern = functools.partial(_odd_sample_kernel, n_pages=n_pages, topk=topk, nbp=nbp)
    o, win_out = pl.pallas_call(
        kern,
        grid_spec=pltpu.PrefetchScalarGridSpec(
            num_scalar_prefetch=1,
            grid=(s_n,),
            in_specs=in_specs,
            out_specs=[per_seq((2 * NSA_HEADS, HEAD_DIM)), per_seq((2 * LANES, n_win))],
            scratch_shapes=[pltpu.VMEM((ISC_ROWS, LANES), f32)]),
        out_shape=[jax.ShapeDtypeStruct((s_n, 2 * NSA_HEADS, HEAD_DIM), f32),
                   jax.ShapeDtypeStruct((s_n, 2 * LANES, n_win), f32)],
        compiler_params=pltpu.CompilerParams(
            dimension_semantics=("arbitrary",), vmem_limit_bytes=VMEM_LIMIT),
        name="odd_attn_sample",
    )(page_table, qn, qd, qi, wi, gts, kvb.reshape(s_n, 1, 8 * LANES), misc[:, 0:IDX_DIM].reshape(s_n, 1, IDX_DIM),
      win_new.reshape(s_n, 1, 2 * LANES), win_t, wc2, wct,
      *([kv_pages] * n_pages), *([idx_pages] * n_pages))
    win_out = jnp.transpose(win_out.reshape(s_n, 2, KV_GROUPS, HEAD_DIM, n_win), (0, 4, 1, 2, 3))
    return o.reshape(s_n, 2 * HALF_W), win_out


def _split(z, sizes):
    cuts = [int(c) for c in np.cumsum(sizes)[:-1]]
    return jnp.split(z, cuts, axis=-1)


def _alibi_slopes(n):
    return jnp.asarray(np.power(2.0, -8.0 * np.arange(1, n + 1) / n), dtype=jnp.float32)


def _masked_softmax(s, mask):
    s = jnp.where(mask, s.astype(jnp.float32), -jnp.inf)
    m = jnp.max(s, axis=-1, keepdims=True)
    e = jnp.exp(s - jnp.where(jnp.isfinite(m), m, 0.0))
    return e / jnp.maximum(jnp.sum(e, axis=-1, keepdims=True), 1e-30)


def _short_conv(u_ext, w, b):
    t = u_ext.shape[1] - (CONV_W - 1)
    return b + sum(w[j] * u_ext[:, j:j + t] for j in range(CONV_W))


def _mlstm_chunk(carry, inp):
    c_st, n_st, m_st = carry
    q, k, v, ig, lf = inp
    L = q.shape[2]
    b = jnp.cumsum(lf, axis=-1)
    causal = jnp.tril(jnp.ones((L, L), dtype=bool))
    dmat = jnp.where(causal, b[..., :, None] - b[..., None, :] + ig[..., None, :], -jnp.inf)
    inter = b + m_st[..., None]
    m_t = jnp.maximum(inter, jnp.max(dmat, axis=-1))
    w_intra = jnp.exp(dmat - m_t[..., None])
    w_inter = jnp.exp(inter - m_t)
    s = jnp.einsum('bhtd,bhsd->bhts', q, k) * w_intra
    num = jnp.einsum('bhts,bhsd->bhtd', s, v) + w_inter[..., None] * jnp.einsum('bhed,bhtd->bhte', c_st, q)
    den = jnp.sum(s, axis=-1) + w_inter * jnp.einsum('bhd,bhtd->bht', n_st, q)
    h = num / jnp.maximum(jnp.abs(den), jnp.exp(-m_t))[..., None]
    m_new = m_t[..., -1]
    w_state = jnp.exp(b[..., -1:] - b + ig - m_new[..., None])
    decay = jnp.exp(b[..., -1] + m_st - m_new)
    c_new = decay[..., None, None] * c_st + jnp.einsum('bhs,bhse,bhsd->bhed', w_state, v, k)
    n_new = decay[..., None] * n_st + jnp.einsum('bhs,bhsd->bhd', w_state, k)
    return (c_new, n_new, m_new), h


def _even_mixer(z, gate_b, conv_w, conv_b, conv_prev, ml_state, chunk):
    f32 = jnp.float32
    bn, t, _ = z.shape
    h, bg, cg, q, k, v, og, ig, fg = _split(z, E_SIZES)
    u = cg * h
    u_ext = jnp.concatenate([conv_prev.astype(u.dtype), u], axis=1)
    y_a = bg * _short_conv(u_ext, conv_w, conv_b)
    new_conv = u_ext[:, u_ext.shape[1] - (CONV_W - 1):]
    heads = lambda a: a.reshape(bn, t, ML_HEADS, ML_DIM).astype(f32)
    q, k, v = heads(q), heads(k) * (ML_DIM ** -0.5), heads(v)
    ig = (ig + gate_b[:ML_HEADS]).astype(f32)
    lf = jax.nn.log_sigmoid((fg + gate_b[ML_HEADS:]).astype(f32))
    nc = t // chunk

    def to_chunks(a):
        a = jnp.moveaxis(a.reshape((bn, nc, chunk) + a.shape[2:]), 1, 0)
        return jnp.swapaxes(a, 2, 3)

    carry = tuple(s.astype(f32) for s in ml_state)
    carry, hs = lax.scan(_mlstm_chunk, carry, tuple(to_chunks(a) for a in (q, k, v, ig, lf)))
    hs = jnp.swapaxes(jnp.moveaxis(hs, 0, 1), 2, 3).reshape(bn, t, HALF_W)
    y_b = jax.nn.sigmoid(og) * hs
    return jnp.concatenate([y_a, y_b], axis=-1), new_conv, carry


def _odd_attend(qn, gates, qd, qi, wi, q_pos, kc, vc, slc, dsa_kv, kidx, kw, kw_pos, topn, topk):
    f32 = jnp.float32
    bn, tq, g, r, _ = qn.shape
    nb = kc.shape[1]
    L = kidx.shape[1]
    scale = HEAD_DIM ** -0.5
    m_n = _alibi_slopes(NSA_HEADS).reshape(g, r)
    m_d = _alibi_slopes(DSA_HEADS).reshape(KV_GROUPS, DSA_HEADS // KV_GROUPS)
    jb = jnp.arange(nb)
    dist_c = q_pos[:, None] - (jb * BLK + BLK - 1)[None, :]
    s_c = jnp.einsum('btgrd,bjgd->btgrj', qn, kc).astype(f32) * scale - m_n[:, :, None] * dist_c[:, None, None, :].astype(f32)
    p_c = _masked_softmax(s_c, (dist_c >= 0)[:, None, None, :])
    o_c = jnp.einsum('btgrj,bjgd->btgrd', p_c.astype(vc.dtype), vc)
    cur = q_pos // BLK
    imp = jnp.sum(p_c, axis=3)
    forced = (jb[None, :] == 0) | (jb[None, :] == cur[:, None]) | (jb[None, :] == cur[:, None] - 1)
    imp = jnp.where(forced[:, None, :], FORCE_SCORE, imp)
    imp = jnp.where((jb[None, :] <= cur[:, None])[:, None, :], imp, -jnp.inf)
    top_val, top_idx = lax.top_k(imp, topn)
    b_ix = jnp.arange(bn)[:, None, None, None]
    g_ix = jnp.arange(g)[None, None, :, None]
    sel = slc[b_ix, g_ix, top_idx]
    pos = top_idx[..., None] * BLK + jnp.arange(BLK)
    dist_s = q_pos[None, :, None, None, None] - pos
    mask_s = (dist_s >= 0) & jnp.isfinite(top_val)[..., None]
    s_s = jnp.einsum('btgrd,btgnpd->btgrnp', qn, sel[..., 0, :]).astype(f32) * scale - m_n[None, None, :, :, None, None] * dist_s[:, :, :, None].astype(f32)
    shp = s_s.shape
    p_s = _masked_softmax(s_s.reshape(shp[:4] + (-1,)), mask_s[:, :, :, None].reshape(bn, tq, g, 1, -1)).reshape(shp)
    o_s = jnp.einsum('btgrnp,btgnpd->btgrd', p_s.astype(sel.dtype), sel[..., 1, :])
    dist_w = q_pos[:, None] - kw_pos[None, :]
    mask_w = (dist_w >= 0) & (dist_w < WINDOW) & (kw_pos >= 0)[None, :]
    s_w = jnp.einsum('btgrd,blgd->btgrl', qn, kw[:, :, 0]).astype(f32) * scale - m_n[:, :, None] * dist_w[:, None, None, :].astype(f32)
    p_w = _masked_softmax(s_w, mask_w[:, None, None, :])
    o_w = jnp.einsum('btgrl,blgd->btgrd', p_w.astype(kw.dtype), kw[:, :, 1])
    o_n = (gates[..., 0:1] * o_c + gates[..., 1:2] * o_s + gates[..., 2:3] * o_w).astype(qn.dtype).reshape(bn, tq, -1)
    k_pos = jnp.arange(L)
    isc = jnp.einsum('bthe,ble->bthl', qi, kidx).astype(f32) * (IDX_DIM ** -0.5)
    isc = jnp.einsum('bthl,bth->btl', jax.nn.relu(isc), wi.astype(f32)) * (IDX_HEADS ** -0.5)
    isc = jnp.where((k_pos[None, :] <= q_pos[:, None])[None], isc, -jnp.inf)
    i_val, i_idx = lax.top_k(isc, topk)
    sel_d = dsa_kv[jnp.arange(bn)[:, None, None], i_idx]
    dist_d = (q_pos[None, :, None] - i_idx).astype(f32)
    s_d = jnp.einsum('btgrd,btkgd->btgrk', qd, sel_d[:, :, :, 0]).astype(f32) * scale - m_d[None, None, :, :, None] * dist_d[:, :, None, None, :]
    p_d = _masked_softmax(s_d, jnp.isfinite(i_val)[:, :, None, None, :])
    o_d = jnp.einsum('btgrk,btkgd->btgrd', p_d.astype(sel_d.dtype), sel_d[:, :, :, 1]).reshape(bn, tq, -1)
    return jnp.concatenate([o_n, o_d.astype(qn.dtype)], axis=-1)


def _odd_mixer(z, w_cmp, past_kv, past_idx, past_win, keep):
    bn, t, _ = z.shape
    p_len = past_kv.shape[1]
    wp = past_win.shape[1]
    g, r = KV_GROUPS, NSA_HEADS // KV_GROUPS
    qn, kvc, kvs, kvw, gates, qd, kvd, qi, ki, wi = _split(z, O_SIZES)
    qn = qn.reshape(bn, t, g, r, HEAD_DIM)
    qd = qd.reshape(bn, t, g, DSA_HEADS // g, HEAD_DIM)
    gates = jax.nn.sigmoid(gates.astype(jnp.float32)).reshape(bn, t, g, r, 3)
    qi = qi.reshape(bn, t, IDX_HEADS, IDX_DIM)
    kv_new = jnp.concatenate([kvc, kvs, kvd], axis=-1).reshape(bn, t, KV_SLOTS, g, HEAD_DIM)
    kvw_new = kvw.reshape(bn, t, 2, g, HEAD_DIM)
    kv = jnp.concatenate([past_kv.astype(kv_new.dtype), kv_new], axis=1)
    kidx = jnp.concatenate([past_idx.astype(ki.dtype), ki], axis=1)
    win_all = jnp.concatenate([past_win.astype(kvw_new.dtype), kvw_new], axis=1)
    L = p_len + t
    nb = -(-L // BLK)
    kvb = jnp.pad(kv[:, :, :4], ((0, 0), (0, nb * BLK - L), (0, 0), (0, 0), (0, 0)))
    kvb = kvb.reshape(bn, nb, BLK, 4, g, HEAD_DIM)
    kc = jnp.einsum('bjpgd,pd->bjgd', kvb[:, :, :, 0], w_cmp[0])
    vc = jnp.einsum('bjpgd,pd->bjgd', kvb[:, :, :, 1], w_cmp[1])
    slc = jnp.transpose(kvb[:, :, :, 2:4], (0, 4, 1, 2, 3, 5))
    dsa_kv = kv[:, :, 4:6]
    topn = min(NSA_TOPN, nb)
    topk = min(DSA_TOPK_MAX, L // 4)
    qb = t if t <= Q_BLOCK else Q_BLOCK
    nqb = t // qb

    def blocks(a):
        return jnp.moveaxis(a.reshape((bn, nqb, qb) + a.shape[2:]), 1, 0)

    def body(args):
        qn_b, g_b, qd_b, qi_b, wi_b, start = args
        q_pos = p_len + start + jnp.arange(qb)
        kw = lax.dynamic_slice_in_dim(win_all, start, wp + qb, axis=1)
        kw_pos = p_len - wp + start + jnp.arange(wp + qb)
        return _odd_attend(qn_b, g_b, qd_b, qi_b, wi_b, q_pos, kc, vc, slc, dsa_kv, kidx, kw, kw_pos, topn, topk)

    starts = jnp.arange(nqb, dtype=jnp.int32) * qb
    o = lax.map(body, (blocks(qn), blocks(gates), blocks(qd), blocks(qi), blocks(wi), starts))
    o = jnp.moveaxis(o, 0, 1).reshape(bn, t, 2 * HALF_W)
    new_win = win_all[:, win_all.shape[1] - keep:]
    return o, kv_new, ki, new_win


def _pad_cols(w, n):
    return jnp.pad(w, ((0, 0), (0, n - w.shape[1])))


def kernel(x_prompt, x_sample, state_conv, state_C, state_n, state_m, cache_kv, cache_idx, cache_win, page_table,
           w_in_e, gate_b_e, conv_w, conv_b, w_out_e, w_in_o, w_cmp, w_out_o, w_up, w_down, ln_g, ln_b):
    f32, bf16 = jnp.float32, jnp.bfloat16
    bp, tp, d = x_prompt.shape
    dbs = x_sample.shape[0]
    past = page_table.shape[1] * cache_kv.shape[2]
    keep = cache_win.shape[2]
    yp = x_prompt.reshape(bp * tp, d)
    ys = x_sample.reshape(dbs, d)
    tm_p, tm_s = 512, dbs
    outs = {}
    for layer in range(DEPTH):
        i = layer // 2
        if layer % 2 == 0:
            n_pad = _round_up(E_COLS, LANES)
            w_in = _pad_cols(w_in_e[i], n_pad).astype(bf16)
            zs = _proj(ys, w_in, tm_s)[:, :E_COLS].reshape(dbs, 1, E_COLS)
            ya, qkv, og, gts, tails = _even_proj(yp, w_in, conv_w[i], conv_b[i], tm_p, tp)
            yb, c_fin, n_fin, m_fin = _mlstm_prompt(qkv, og, gts, gate_b_e[i], bp, tp)
            cvp = tails.reshape(bp, tp // tm_p, CONV_TAIL, HALF_W)[:, -1, CONV_TAIL - (CONV_W - 1):]
            ms, cvs, sts = _even_mixer(zs, gate_b_e[i], conv_w[i], conv_b[i], state_conv[i],
                                       (state_C[i], state_n[i], state_m[i]), 1)
            outs['conv'] = (cvp[None], cvs[None])
            outs['c'] = (c_fin[None], sts[0][None])
            outs['n'] = (n_fin[None], sts[1][None])
            outs['m'] = (m_fin[:, :, 0][None], sts[2][None])
            w_out = w_out_e[i].astype(bf16)
        else:
            n_pad = _round_up(O_COLS, LANES)
            w_in = _pad_cols(w_in_o[i], n_pad).astype(bf16)
            w_packed = _pack_w_in_o(w_in_o[i]).astype(bf16)
            wc2 = jnp.concatenate([w_cmp[i], w_cmp[i]], axis=-1)
            rows_first = lambda kvt: jnp.transpose(
                kvt.reshape(kvt.shape[0], KV_SLOTS, KV_GROUPS, HEAD_DIM, kvt.shape[2]), (0, 4, 1, 2, 3))
            zq, kvt_p, kvb, winp, miscp = _odd_proj(yp, w_packed, tm_p, tp)
            mp = _odd_attn_prompt(zq, miscp, kvb, wc2, bp, tp)
            kvp = rows_first(kvt_p)
            ixp = miscp[:, :IDX_DIM].reshape(bp, tp, IDX_DIM)
            wnp = winp.reshape(bp, tp, 2, KV_GROUPS, HEAD_DIM)[:, tp - keep:]
            zq_s, kvt_s, kvb_s, win_s, misc_s = _odd_proj(ys, w_packed, tm_s, dbs)
            ms, wns = _odd_attn_sample(zq_s, misc_s, kvb_s, win_s, cache_kv[i], cache_idx[i], cache_win[i],
                                       page_table, w_cmp[i])
            kvs = rows_first(kvt_s).reshape(dbs, 1, KV_SLOTS, KV_GROUPS, HEAD_DIM)
            ixs = misc_s[:, :IDX_DIM].reshape(dbs, 1, IDX_DIM)
            outs['kv'] = (kvp[None], kvs[None])
            outs['idx'] = (ixp[None], ixs[None])
            outs['win'] = (wnp[None], wns[None])
            w_out = w_out_o[i].astype(bf16)
        if layer % 2 == 0:
            yp = _outproj2_ln(ya, yb.reshape(bp * tp, HALF_W), w_out, yp, ln_g[layer, 0], ln_b[layer, 0], tm_p)
        else:
            yp = _outproj_ln(mp, w_out, yp, ln_g[layer, 0], ln_b[layer, 0], tm_p)
        ys = _outproj_ln(ms.reshape(dbs, d), w_out, ys, ln_g[layer, 0], ln_b[layer, 0], tm_s)
        wu, wd = w_up[layer].astype(bf16), w_down[layer].astype(bf16)
        yp = _mlp_ln(yp, wu, wd, ln_g[layer, 1], ln_b[layer, 1], tm_p, 1024)
        ys = _mlp_ln(ys, wu, wd, ln_g[layer, 1], ln_b[layer, 1], tm_s, 1024)
    return (yp.reshape(bp, tp, d), ys.reshape(dbs, 1, d),
            outs['conv'][0], outs['conv'][1], outs['c'][0], outs['c'][1],
            outs['n'][0], outs['n'][1], outs['m'][0], outs['m'][1],
            outs['kv'][0], outs['kv'][1], outs['idx'][0], outs['idx'][1],
            outs['win'][0], outs['win'][1])
```

```python
import functools

import jax
import jax.numpy as jnp
import numpy as np
from jax import lax
from jax.experimental import pallas as pl
from jax.experimental.pallas import tpu as pltpu

D_MODEL = 1024
DEPTH = 2
HALF_W = 512
D_FF = 4096
CONV_W = 3
ML_HEADS = 4
ML_DIM = 128
HEAD_DIM = 64
KV_GROUPS = 2
NSA_HEADS = 8
DSA_HEADS = 8
BLK = 64
NSA_TOPN = 8
WINDOW = 256
IDX_HEADS = 4
IDX_DIM = 32
DSA_TOPK_MAX = 256
KV_SLOTS = 6
FORCE_SCORE = 1e4
ALPHA = (2.0 * DEPTH) ** 0.25
LN_EPS = 1e-5
O_SIZES = (512, 256, 256, 256, 24, 512, 256, 128, 32, 4)
LANES = 128
VMEM_LIMIT = 48 * 1024 * 1024


def _round_up(n, m):
    return -(-n // m) * m


def _proj_kernel(x_ref, w_ref, o_ref):
    o_ref[...] = jnp.dot(x_ref[...].astype(jnp.bfloat16), w_ref[...],
                         preferred_element_type=jnp.float32)


def _proj(x, w_bf16, tm):
    m, k = x.shape
    n = w_bf16.shape[1]
    tn = n
    for cand in (1024, 768, 512, 256, 128):
        if n % cand == 0:
            tn = cand
            break
    return pl.pallas_call(
        _proj_kernel,
        grid=(n // tn, m // tm),
        in_specs=[pl.BlockSpec((tm, k), lambda j, i: (i, 0)),
                  pl.BlockSpec((k, tn), lambda j, i: (0, j))],
        out_specs=pl.BlockSpec((tm, tn), lambda j, i: (i, j)),
        out_shape=jax.ShapeDtypeStruct((m, n), jnp.float32),
        compiler_params=pltpu.CompilerParams(
            dimension_semantics=("arbitrary", "arbitrary"), vmem_limit_bytes=VMEM_LIMIT),
        name="proj",
    )(x, w_bf16)


def _layer_norm_rows(v, g, b):
    mu = jnp.mean(v, axis=-1, keepdims=True)
    d = v - mu
    var = jnp.mean(d * d, axis=-1, keepdims=True)
    return d * lax.rsqrt(var + LN_EPS) * g + b


def _outproj_ln_kernel(y_ref, w_ref, x_ref, g_ref, b_ref, o_ref):
    mix = jnp.dot(y_ref[...].astype(jnp.bfloat16), w_ref[...], preferred_element_type=jnp.float32)
    o_ref[...] = _layer_norm_rows(ALPHA * x_ref[...] + mix, g_ref[...], b_ref[...])


def _outproj_ln(y, w_bf16, x, g, b, tm):
    m, d = x.shape
    k = y.shape[1]
    return pl.pallas_call(
        _outproj_ln_kernel,
        grid=(m // tm,),
        in_specs=[pl.BlockSpec((tm, k), lambda i: (i, 0)),
                  pl.BlockSpec((k, d), lambda i: (0, 0)),
                  pl.BlockSpec((tm, d), lambda i: (i, 0)),
                  pl.BlockSpec((1, d), lambda i: (0, 0)),
                  pl.BlockSpec((1, d), lambda i: (0, 0))],
        out_specs=pl.BlockSpec((tm, d), lambda i: (i, 0)),
        out_shape=jax.ShapeDtypeStruct((m, d), jnp.float32),
        compiler_params=pltpu.CompilerParams(
            dimension_semantics=("arbitrary",), vmem_limit_bytes=VMEM_LIMIT),
        name="outproj_ln",
    )(y, w_bf16, x, g.reshape(1, d), b.reshape(1, d))


def _mlp_ln_kernel(x_ref, wu_ref, wd_ref, g_ref, b_ref, o_ref, acc_ref):
    f = pl.program_id(1)

    @pl.when(f == 0)
    def _():
        acc_ref[...] = jnp.zeros_like(acc_ref)

    up = jnp.dot(x_ref[...].astype(jnp.bfloat16), wu_ref[...], preferred_element_type=jnp.float32)
    act = jnp.square(jnp.maximum(up, 0.0))
    acc_ref[...] += jnp.dot(act.astype(jnp.bfloat16), wd_ref[...], preferred_element_type=jnp.float32)

    @pl.when(f == pl.num_programs(1) - 1)
    def _():
        o_ref[...] = _layer_norm_rows(ALPHA * x_ref[...] + acc_ref[...], g_ref[...], b_ref[...])


def _mlp_ln(x, wu_bf16, wd_bf16, g, b, tm, tf):
    m, d = x.shape
    ff = wu_bf16.shape[1]
    return pl.pallas_call(
        _mlp_ln_kernel,
        grid=(m // tm, ff // tf),
        in_specs=[pl.BlockSpec((tm, d), lambda i, f: (i, 0)),
                  pl.BlockSpec((d, tf), lambda i, f: (0, f)),
                  pl.BlockSpec((tf, d), lambda i, f: (f, 0)),
                  pl.BlockSpec((1, d), lambda i, f: (0, 0)),
                  pl.BlockSpec((1, d), lambda i, f: (0, 0))],
        out_specs=pl.BlockSpec((tm, d), lambda i, f: (i, 0)),
        out_shape=jax.ShapeDtypeStruct((m, d), jnp.float32),
        scratch_shapes=[pltpu.VMEM((tm, d), jnp.float32)],
        compiler_params=pltpu.CompilerParams(
            dimension_semantics=("arbitrary", "arbitrary"), vmem_limit_bytes=VMEM_LIMIT),
        name="mlp_ln",
    )(x, wu_bf16, wd_bf16, g.reshape(1, d), b.reshape(1, d))


E_QKV = 3 * HALF_W
E_OG = 6 * HALF_W
E_GATE = 7 * HALF_W
E_PACKED = 7 * HALF_W + LANES
CONV_TAIL = 8


def _even_proj_kernel(x_ref, w_ref, cw_ref, cb_ref, ya_ref, qkv_ref, og_ref, gate_ref, tail_ref, carry_ref,
                      *, tiles_per_seq):
    f32, bf16 = jnp.float32, jnp.bfloat16
    tm = x_ref.shape[0]

    @pl.when(pl.program_id(0) % tiles_per_seq == 0)
    def _():
        carry_ref[...] = jnp.zeros_like(carry_ref)

    z = jnp.dot(x_ref[...].astype(bf16), w_ref[...], preferred_element_type=f32)
    u = z[:, 2 * HALF_W:3 * HALF_W] * z[:, 0:HALF_W]
    prev = carry_ref[...]
    row = lax.broadcasted_iota(jnp.int32, (tm, 1), 0)
    conv = cb_ref[...] + cw_ref[CONV_W - 1:CONV_W, :] * u
    for back in range(1, CONV_W):
        shifted = pltpu.roll(u, back, 0)
        for r in range(back):
            shifted = jnp.where(row == r, prev[CONV_TAIL - back + r:CONV_TAIL - back + r + 1, :], shifted)
        conv = conv + cw_ref[CONV_W - 1 - back:CONV_W - back, :] * shifted
    ya_ref[...] = (z[:, HALF_W:2 * HALF_W] * conv).astype(bf16)
    tail = u[tm - CONV_TAIL:]
    carry_ref[...] = tail
    tail_ref[0] = tail
    qkv_ref[:, 0:HALF_W] = z[:, E_QKV:E_QKV + HALF_W].astype(bf16)
    qkv_ref[:, HALF_W:2 * HALF_W] = (z[:, E_QKV + HALF_W:E_QKV + 2 * HALF_W] * (ML_DIM ** -0.5)).astype(bf16)
    qkv_ref[:, 2 * HALF_W:3 * HALF_W] = z[:, E_QKV + 2 * HALF_W:E_OG].astype(bf16)
    og_ref[...] = z[:, E_OG:E_GATE]
    gate_ref[...] = z[:, E_GATE:E_PACKED]


def _even_proj(x, w_bf16, conv_w, conv_b, tm, seq_len):
    m, k = x.shape
    widths = (HALF_W, 3 * HALF_W, HALF_W, LANES)
    dtypes = (jnp.bfloat16, jnp.bfloat16, jnp.float32, jnp.float32)
    kern = functools.partial(_even_proj_kernel, tiles_per_seq=seq_len // tm)
    return pl.pallas_call(
        kern,
        grid=(m // tm,),
        in_specs=[pl.BlockSpec((tm, k), lambda i: (i, 0)),
                  pl.BlockSpec((k, E_PACKED), lambda i: (0, 0)),
                  pl.BlockSpec((CONV_W, HALF_W), lambda i: (0, 0)),
                  pl.BlockSpec((1, HALF_W), lambda i: (0, 0))],
        out_specs=[pl.BlockSpec((tm, n), lambda i: (i, 0)) for n in widths]
        + [pl.BlockSpec((1, CONV_TAIL, HALF_W), lambda i: (i, 0, 0))],
        out_shape=[jax.ShapeDtypeStruct((m, n), dt) for n, dt in zip(widths, dtypes)]
        + [jax.ShapeDtypeStruct((m // tm, CONV_TAIL, HALF_W), jnp.float32)],
        scratch_shapes=[pltpu.VMEM((CONV_TAIL, HALF_W), jnp.float32)],
        compiler_params=pltpu.CompilerParams(
            dimension_semantics=("arbitrary",), vmem_limit_bytes=VMEM_LIMIT),
        name="even_proj",
    )(x, w_bf16, conv_w, conv_b.reshape(1, HALF_W))


def _mlstm_kernel(qkv_ref, og_ref, gate_ref, gb_ref, yb_ref, c_ref, n_ref, m_ref):
    f32, bf16 = jnp.float32, jnp.bfloat16
    L = qkv_ref.shape[1]

    @pl.when(pl.program_id(1) == 0)
    def _():
        c_ref[...] = jnp.zeros_like(c_ref)
        n_ref[...] = jnp.zeros_like(n_ref)
        m_ref[...] = jnp.zeros_like(m_ref)

    pre = gate_ref[0] + gb_ref[...]
    lf = jax.nn.log_sigmoid(pre)
    row = lax.broadcasted_iota(jnp.int32, (L, L), 0)
    col = lax.broadcasted_iota(jnp.int32, (L, L), 1)
    causal = col <= row
    tri = jnp.where(causal, 1.0, 0.0).astype(f32)
    b_all = jnp.dot(tri, lf, preferred_element_type=f32, precision=lax.Precision.HIGHEST)
    pre_t = pre.T
    b_t = b_all.T
    for h in range(ML_HEADS):
        q = qkv_ref[0, :, h * ML_DIM:(h + 1) * ML_DIM]
        k = qkv_ref[0, :, HALF_W + h * ML_DIM:HALF_W + (h + 1) * ML_DIM]
        v = qkv_ref[0, :, 2 * HALF_W + h * ML_DIM:2 * HALF_W + (h + 1) * ML_DIM]
        ig_col = pre[:, h:h + 1]
        b_col = b_all[:, ML_HEADS + h:ML_HEADS + h + 1]
        a_row = pre_t[h:h + 1, :] - b_t[ML_HEADS + h:ML_HEADS + h + 1, :]
        m_st = m_ref[0, h:h + 1, 0:1]
        c_st = c_ref[0, h]
        n_st = n_ref[0, h:h + 1, :]
        dmat = jnp.where(causal, b_col + a_row, -jnp.inf)
        inter = b_col + m_st
        m_t = jnp.maximum(inter, jnp.max(dmat, axis=-1, keepdims=True))
        w_intra = jnp.exp(dmat - m_t)
        w_inter = jnp.exp(inter - m_t)
        s = _dot_nt(q, k) * w_intra
        num = jnp.dot(s.astype(bf16), v, preferred_element_type=f32) + w_inter * _dot_nt(q, c_st.astype(bf16))
        den = jnp.sum(s, axis=-1, keepdims=True) + w_inter * jnp.sum(q.astype(f32) * n_st, axis=-1, keepdims=True)
        hs = num * (1.0 / jnp.maximum(jnp.abs(den), jnp.exp(-m_t)))
        m_new = m_t[L - 1:L, :]
        b_last = b_col[L - 1:L, :]
        w_state = jnp.exp(b_last - b_col + ig_col - m_new)
        decay = jnp.exp(b_last + m_st - m_new)
        vw = (v.astype(f32) * w_state).astype(bf16)
        c_ref[0, h] = decay * c_st + lax.dot_general(vw, k, (((0,), (0,)), ((), ())), preferred_element_type=f32)
        n_ref[0, h:h + 1, :] = decay * n_st + jnp.sum(k.astype(f32) * w_state, axis=0, keepdims=True)
        m_ref[0, h:h + 1, :] = jnp.broadcast_to(m_new, (1, LANES))
        og = og_ref[0, :, h * ML_DIM:(h + 1) * ML_DIM]
        yb_ref[0, :, h * ML_DIM:(h + 1) * ML_DIM] = (jax.nn.sigmoid(og) * hs).astype(bf16)


def _mlstm_prompt(qkv, og, gates, gate_b, bsz, t_len, chunk=128):
    gb = jnp.pad(gate_b, (0, LANES - gate_b.shape[0])).reshape(1, LANES)
    return pl.pallas_call(
        _mlstm_kernel,
        grid=(bsz, t_len // chunk),
        in_specs=[pl.BlockSpec((1, chunk, 3 * HALF_W), lambda b, c: (b, c, 0)),
                  pl.BlockSpec((1, chunk, HALF_W), lambda b, c: (b, c, 0)),
                  pl.BlockSpec((1, chunk, LANES), lambda b, c: (b, c, 0)),
                  pl.BlockSpec((1, LANES), lambda b, c: (0, 0))],
        out_specs=[pl.BlockSpec((1, chunk, HALF_W), lambda b, c: (b, c, 0)),
                   pl.BlockSpec((1, ML_HEADS, ML_DIM, ML_DIM), lambda b, c: (b, 0, 0, 0)),
                   pl.BlockSpec((1, ML_HEADS, ML_DIM), lambda b, c: (b, 0, 0)),
                   pl.BlockSpec((1, ML_HEADS, LANES), lambda b, c: (b, 0, 0))],
        out_shape=[jax.ShapeDtypeStruct((bsz, t_len, HALF_W), jnp.bfloat16),
                   jax.ShapeDtypeStruct((bsz, ML_HEADS, ML_DIM, ML_DIM), jnp.float32),
                   jax.ShapeDtypeStruct((bsz, ML_HEADS, ML_DIM), jnp.float32),
                   jax.ShapeDtypeStruct((bsz, ML_HEADS, LANES), jnp.float32)],
        compiler_params=pltpu.CompilerParams(
            dimension_semantics=("arbitrary", "arbitrary"), vmem_limit_bytes=VMEM_LIMIT),
        name="mlstm_prompt",
    )(qkv.reshape(bsz, t_len, 3 * HALF_W), og.reshape(bsz, t_len, HALF_W),
      gates.reshape(bsz, t_len, LANES), gb)


def _outproj2_ln_kernel(ya_ref, yb_ref, w_ref, x_ref, g_ref, b_ref, o_ref):
    mix = jnp.dot(ya_ref[...], w_ref[0:HALF_W, :], preferred_element_type=jnp.float32)
    mix = mix + jnp.dot(yb_ref[...], w_ref[HALF_W:, :], preferred_element_type=jnp.float32)
    o_ref[...] = _layer_norm_rows(ALPHA * x_ref[...] + mix, g_ref[...], b_ref[...])


def _outproj2_ln(ya, yb, w_bf16, x, g, b, tm):
    m, d = x.shape
    return pl.pallas_call(
        _outproj2_ln_kernel,
        grid=(m // tm,),
        in_specs=[pl.BlockSpec((tm, HALF_W), lambda i: (i, 0)),
                  pl.BlockSpec((tm, HALF_W), lambda i: (i, 0)),
                  pl.BlockSpec((2 * HALF_W, d), lambda i: (0, 0)),
                  pl.BlockSpec((tm, d), lambda i: (i, 0)),
                  pl.BlockSpec((1, d), lambda i: (0, 0)),
                  pl.BlockSpec((1, d), lambda i: (0, 0))],
        out_specs=pl.BlockSpec((tm, d), lambda i: (i, 0)),
        out_shape=jax.ShapeDtypeStruct((m, d), jnp.float32),
        compiler_params=pltpu.CompilerParams(
            dimension_semantics=("arbitrary",), vmem_limit_bytes=VMEM_LIMIT),
        name="outproj2_ln",
    )(ya, yb, w_bf16, x, g.reshape(1, d), b.reshape(1, d))


SAMPLE_ROWS = 8


def _even_sample_kernel(h_ref, bg_ref, cg_ref, q_ref, k_ref, v_ref, og_ref, gate_ref, gb_ref, cw_ref, cb_ref,
                        conv_ref, c_ref, n_ref, m_ref, ya_ref, yb_ref, conv_out_ref, c_out_ref, n_out_ref, m_out_ref):
    f32, bf16 = jnp.float32, jnp.bfloat16
    rows = h_ref.shape[0]
    row = lax.broadcasted_iota(jnp.int32, (rows, 1), 0)
    tn = (((0,), (0,)), ((), ()))
    u = cg_ref[...] * h_ref[...]
    prev0, prev1 = conv_ref[:, 0, :], conv_ref[:, 1, :]
    conv = cb_ref[...] + cw_ref[0:1, :] * prev0 + cw_ref[1:2, :] * prev1 + cw_ref[2:3, :] * u
    ya_ref[...] = (bg_ref[...] * conv).astype(bf16)
    conv_out_ref[:, 0, :] = prev1
    conv_out_ref[:, 1, :] = u
    pre = gate_ref[...] + gb_ref[...]
    lf_all = jax.nn.log_sigmoid(pre)
    for h in range(ML_HEADS):
        cols = slice(h * ML_DIM, (h + 1) * ML_DIM)
        q = q_ref[:, cols]
        k = k_ref[:, cols] * (ML_DIM ** -0.5)
        v = v_ref[:, cols]
        ig = pre[:, h:h + 1]
        lf = lf_all[:, ML_HEADS + h:ML_HEADS + h + 1]
        m_st = m_ref[:, h:h + 1]
        n_st = n_ref[:, h, :]
        inter = lf + m_st
        m_t = jnp.maximum(inter, ig)
        w_intra = jnp.exp(ig - m_t)
        w_inter = jnp.exp(inter - m_t)
        qb, kb = q.astype(bf16), k.astype(bf16)
        s = jnp.sum(qb.astype(f32) * kb.astype(f32), axis=-1, keepdims=True) * w_intra
        cq = jnp.zeros((rows, ML_DIM), f32)
        for b in range(rows):
            c_b = c_ref[b, h]
            cq = jnp.where(row == b, _dot_nt(qb, c_b.astype(bf16)), cq)
            vw = jnp.where(row == b, v * w_intra, 0.0)
            outer = lax.dot_general(vw, k, tn, preferred_element_type=f32, precision=lax.Precision.HIGHEST)
            c_out_ref[b, h] = w_inter[b:b + 1, :] * c_b + outer
        num = s * v + w_inter * cq
        den = s + w_inter * jnp.sum(n_st * q, axis=-1, keepdims=True)
        hs = num * (1.0 / jnp.maximum(jnp.abs(den), jnp.exp(-m_t)))
        n_out_ref[:, h, :] = w_inter * n_st + w_intra * k
        m_out_ref[:, h, :] = jnp.broadcast_to(m_t, (rows, LANES))
        yb_ref[:, cols] = (jax.nn.sigmoid(og_ref[:, cols]) * hs).astype(bf16)


def _even_sample(z, gate_b, conv_w, conv_b, state_conv, state_c, state_n, state_m):
    s_n = z.shape[0]
    f32 = jnp.float32
    r = SAMPLE_ROWS
    gb = jnp.pad(gate_b, (0, LANES - gate_b.shape[0])).reshape(1, LANES)
    col = lambda j: pl.BlockSpec((r, HALF_W), lambda i, j=j: (i, j))
    in_specs = [col(j) for j in range(7)] + [
        pl.BlockSpec((r, LANES), lambda i: (i, E_GATE // LANES)),
        pl.BlockSpec((1, LANES), lambda i: (0, 0)),
        pl.BlockSpec((CONV_W, HALF_W), lambda i: (0, 0)),
        pl.BlockSpec((1, HALF_W), lambda i: (0, 0)),
        pl.BlockSpec((r, CONV_W - 1, HALF_W), lambda i: (i, 0, 0)),
        pl.BlockSpec((r, ML_HEADS, ML_DIM, ML_DIM), lambda i: (i, 0, 0, 0)),
        pl.BlockSpec((r, ML_HEADS, ML_DIM), lambda i: (i, 0, 0)),
        pl.BlockSpec((r, ML_HEADS), lambda i: (i, 0))]
    out_specs = [pl.BlockSpec((r, HALF_W), lambda i: (i, 0)),
                 pl.BlockSpec((r, HALF_W), lambda i: (i, 0)),
                 pl.BlockSpec((r, CONV_W - 1, HALF_W), lambda i: (i, 0, 0)),
                 pl.BlockSpec((r, ML_HEADS, ML_DIM, ML_DIM), lambda i: (i, 0, 0, 0)),
                 pl.BlockSpec((r, ML_HEADS, ML_DIM), lambda i: (i, 0, 0)),
                 pl.BlockSpec((r, ML_HEADS, LANES), lambda i: (i, 0, 0))]
    out_shape = [jax.ShapeDtypeStruct((s_n, HALF_W), jnp.bfloat16),
                 jax.ShapeDtypeStruct((s_n, HALF_W), jnp.bfloat16),
                 jax.ShapeDtypeStruct((s_n, CONV_W - 1, HALF_W), f32),
                 jax.ShapeDtypeStruct((s_n, ML_HEADS, ML_DIM, ML_DIM), f32),
                 jax.ShapeDtypeStruct((s_n, ML_HEADS, ML_DIM), f32),
                 jax.ShapeDtypeStruct((s_n, ML_HEADS, LANES), f32)]
    return pl.pallas_call(
        _even_sample_kernel,
        grid=(s_n // r,),
        in_specs=in_specs,
        out_specs=out_specs,
        out_shape=out_shape,
        compiler_params=pltpu.CompilerParams(
            dimension_semantics=("arbitrary",), vmem_limit_bytes=VMEM_LIMIT),
        name="even_sample",
    )(z, z, z, z, z, z, z, z, gb, conv_w, conv_b.reshape(1, HALF_W), state_conv, state_c, state_n, state_m)


O_Q = 0
O_QD = NSA_HEADS * LANES
O_KV = 2048
O_WIN = 2816
O_QI = 3072
O_MISC = 3200
O_PACKED = 3328
ZQ_QI = 2048
MISC_WI = IDX_DIM
MISC_GATE = IDX_DIM + IDX_HEADS
NEG_BIG = -(2.0 ** 100)
INT_MIN = -(2 ** 31)
ALIBI = tuple(float(2.0 ** (-8.0 * (h + 1) / NSA_HEADS)) for h in range(NSA_HEADS))


def _pack_w_in_o(w):
    qn, kvc, kvs, kvw, gates, qd, kvd, qi, ki, wi = _split(w, O_SIZES)
    scale = HEAD_DIM ** -0.5
    rows = w.shape[0]

    def spread(q):
        q = (q * scale).reshape(rows, NSA_HEADS, HEAD_DIM)
        z = jnp.zeros_like(q)
        low = (jnp.arange(NSA_HEADS) < NSA_HEADS // KV_GROUPS)[None, :, None]
        return jnp.concatenate([jnp.where(low, q, z), jnp.where(low, z, q)], axis=-1).reshape(rows, NSA_HEADS * LANES)

    pad = jnp.zeros((rows, O_PACKED - O_MISC - IDX_DIM - IDX_HEADS - 3 * NSA_HEADS), w.dtype)
    return jnp.concatenate([spread(qn), spread(qd), kvc, kvs, kvd, kvw, qi, ki, wi, gates, pad], axis=1)


def _odd_proj_kernel(x_ref, w_ref, wkvt_ref, zq_ref, kvt_ref, kvb_ref, win_ref, misc_ref):
    xb = x_ref[...].astype(jnp.bfloat16)
    z = jnp.dot(xb, w_ref[...], preferred_element_type=jnp.float32)
    zq_ref[:, 0:O_KV] = z[:, 0:O_KV].astype(jnp.bfloat16)
    zq_ref[:, ZQ_QI:ZQ_QI + LANES] = z[:, O_QI:O_MISC].astype(jnp.bfloat16)
    kvt_ref[0] = _dot_nt(wkvt_ref[...], xb)
    kvb_ref[...] = z[:, O_KV:O_QI].astype(jnp.bfloat16)
    win_ref[...] = z[:, O_WIN:O_QI]
    misc_ref[...] = z[:, O_MISC:O_PACKED]


def _odd_proj(x, w_packed_bf16, tm, seq_len):
    m, k = x.shape
    n_kv = O_WIN - O_KV
    widths = (O_KV + LANES, O_QI - O_KV, O_QI - O_WIN, LANES)
    dtypes = (jnp.bfloat16, jnp.bfloat16, jnp.float32, jnp.float32)
    row_spec = lambda n: pl.BlockSpec((tm, n), lambda i: (i, 0))
    tiles = seq_len // tm
    out_specs = [row_spec(widths[0]), pl.BlockSpec((1, n_kv, tm), lambda i: (i // tiles, 0, i % tiles))]
    out_specs += [row_spec(n) for n in widths[1:]]
    out_shape = [jax.ShapeDtypeStruct((m, widths[0]), dtypes[0]),
                 jax.ShapeDtypeStruct((m // seq_len, n_kv, seq_len), jnp.float32)]
    out_shape += [jax.ShapeDtypeStruct((m, n), dt) for n, dt in zip(widths[1:], dtypes[1:])]
    return pl.pallas_call(
        _odd_proj_kernel,
        grid=(m // tm,),
        in_specs=[pl.BlockSpec((tm, k), lambda i: (i, 0)),
                  pl.BlockSpec((k, O_PACKED), lambda i: (0, 0)),
                  pl.BlockSpec((n_kv, k), lambda i: (0, 0))],
        out_specs=out_specs,
        out_shape=out_shape,
        compiler_params=pltpu.CompilerParams(
            dimension_semantics=("arbitrary",), vmem_limit_bytes=VMEM_LIMIT),
        name="odd_proj",
    )(x, w_packed_bf16, w_packed_bf16[:, O_KV:O_WIN].T)


def _dot_nt(a, b):
    return lax.dot_general(a, b, (((1,), (1,)), ((), ())), preferred_element_type=jnp.float32)


def _flash_heads(q_heads, slopes, group, kv_ref, k_blk, v_blk, tk, lo, hi, bias_fn, skip_fn=None, start_fn=None):
    f32, bf16 = jnp.float32, jnp.bfloat16
    if start_fn is None:
        start_fn = lambda kt: kt * tk
    tq = q_heads[0].shape[0]
    n_heads = len(q_heads)
    lane = lax.broadcasted_iota(jnp.int32, (1, LANES), 1)
    own_half = (lane // HEAD_DIM) == group
    q_all = jnp.concatenate(q_heads, axis=0)
    slope_col = jnp.concatenate([jnp.full((tq, 1), sl, f32) for sl in slopes], axis=0)

    def tile(kt, carry):
        m, acc = carry
        start = pl.multiple_of(start_fn(kt), LANES)
        rows = pl.ds(start, tk)
        k = kv_ref[0, rows, k_blk * LANES:(k_blk + 1) * LANES]
        v = kv_ref[0, rows, v_blk * LANES:(v_blk + 1) * LANES]
        v = jnp.where(own_half, v, jnp.ones_like(v))
        sp = (start + lax.broadcasted_iota(jnp.int32, (1, tk), 1)).astype(f32)
        b = bias_fn(kt)
        s = _dot_nt(q_all, k) + slope_col * sp + jnp.concatenate([b] * n_heads, axis=0)
        m_new = jnp.maximum(m, jnp.max(s, axis=-1, keepdims=True))
        p = jnp.exp(s - m_new)
        acc = jnp.exp(m - m_new) * acc + jnp.dot(p.astype(bf16), v, preferred_element_type=f32)
        return m_new, acc

    def body(kt, carry):
        if skip_fn is None:
            return tile(kt, carry)
        return lax.cond(skip_fn(kt), lambda c: c, functools.partial(tile, kt), carry)

    m0 = jnp.full((n_heads * tq, 1), -jnp.inf, f32)
    a0 = jnp.zeros((n_heads * tq, LANES), f32)
    _, acc = lax.fori_loop(lo, hi, body, (m0, a0))
    return [acc[r * tq:(r + 1) * tq] for r in range(n_heads)]


def _odd_attn_kernel(zq_ref, miscq_ref, kvb_ref, misck_ref, wc_ref, o_ref,
                     kc_ref, vc_ref, ki_ref, key_ref, dbias_ref, flag_ref, *, t_len, tq, tk, topk):
    f32, bf16, i32 = jnp.float32, jnp.bfloat16, jnp.int32
    nb = t_len // BLK
    qi_blk = pl.program_id(1)
    q0 = qi_blk * tq

    @pl.when(qi_blk == 0)
    def _():
        ck = kvb_ref[0, :, 0:LANES].astype(f32).reshape(nb, BLK, LANES)
        kc_ref[...] = jnp.sum(ck * wc_ref[0][None], axis=1).astype(bf16)
        cv = kvb_ref[0, :, LANES:2 * LANES].astype(f32).reshape(nb, BLK, LANES)
        vc = jnp.sum(cv * wc_ref[1][None], axis=1)
        for g in range(KV_GROUPS):
            half = vc[:, g * HEAD_DIM:(g + 1) * HEAD_DIM]
            vc_ref[g] = jnp.concatenate([half, half], axis=1).astype(bf16)
        ki_ref[...] = misck_ref[0, :, 0:IDX_DIM].astype(bf16)

    misc = miscq_ref[0]
    gates = jax.nn.sigmoid(misc[:, MISC_GATE:MISC_GATE + 3 * NSA_HEADS])
    t_col = q0 + lax.broadcasted_iota(i32, (tq, 1), 0)
    lane = lax.broadcasted_iota(i32, (1, LANES), 1)
    low_half = lane < HEAD_DIM
    heads_per_group = NSA_HEADS // KV_GROUPS

    def q_head(base, h):
        return zq_ref[0, :, base + h * LANES: base + (h + 1) * LANES]

    def pair(even, odd):
        return jnp.where(low_half, even, odd)

    def normalised_pair(acc_even, acc_odd, g):
        outs = []
        for parity, acc in enumerate((acc_even, acc_odd)):
            swapped = pltpu.roll(acc, HEAD_DIM, 1)
            outs.append(acc * (1.0 / swapped) if parity == g else swapped * (1.0 / acc))
        return pair(outs[0], outs[1])

    def gate_pair(j, c):
        a = gates[:, (2 * j) * 3 + c:(2 * j) * 3 + c + 1]
        b = gates[:, (2 * j + 1) * 3 + c:(2 * j + 1) * 3 + c + 1]
        return jnp.where(low_half, a, b)

    jb = lax.broadcasted_iota(i32, (1, nb), 1)
    dist_c = t_col - (jb * BLK + (BLK - 1))
    mask_c = dist_c >= 0
    dist_cf = dist_c.astype(f32)
    cur = t_col // BLK
    forced = (jb == 0) | (jb == cur) | (jb == cur - 1)
    admissible = jb <= cur
    jb_full = lax.broadcasted_iota(i32, (tq, nb), 1)
    o_cmp, sel_bias, sel_any = [], [], []
    for g in range(KV_GROUPS):
        imp = jnp.zeros((tq, nb), f32)
        for r in range(heads_per_group):
            h = g * heads_per_group + r
            s = _dot_nt(q_head(O_Q, h), kc_ref[...]) - ALIBI[h] * dist_cf
            s = jnp.where(mask_c, s, NEG_BIG)
            m = jnp.max(s, axis=-1, keepdims=True)
            e = jnp.where(mask_c, jnp.exp(s - m), 0.0)
            p = e * (1.0 / jnp.maximum(jnp.sum(e, axis=-1, keepdims=True), 1e-30))
            imp = imp + p
            o_cmp.append(jnp.dot(p.astype(bf16), vc_ref[g], preferred_element_type=f32))
        imp = jnp.where(forced, FORCE_SCORE, imp)
        imp = jnp.where(admissible, imp, -jnp.inf)
        sel = jnp.zeros((tq, nb), f32)
        for _ in range(min(NSA_TOPN, nb)):
            m = jnp.max(imp, axis=-1, keepdims=True)
            first = jnp.min(jnp.where(imp == m, jb_full, nb), axis=-1, keepdims=True)
            hit = jb_full == first
            sel = jnp.where(hit & (m > -jnp.inf), 1.0, sel)
            imp = jnp.where(hit, -jnp.inf, imp)
        sel_bias.append(jnp.where(sel > 0.0, 0.0, NEG_BIG).astype(bf16))
        sel_any.append(jnp.max(sel, axis=0, keepdims=True))

    n_sel_tiles = (q0 + tq + tk - 1) // tk
    blocks_per_tile = tk // BLK

    for g in range(KV_GROUPS):
        for j in range(nb // blocks_per_tile):
            hit = jnp.max(sel_any[g][:, j * blocks_per_tile:(j + 1) * blocks_per_tile])
            flag_ref[g, j] = (hit > 0.0).astype(i32)

    def sel_skip_fn(g):
        return lambda kt: flag_ref[g, kt] == 0

    def causal_bias(start, width):
        sp = start + lax.broadcasted_iota(i32, (1, width), 1)
        return jnp.where(sp <= t_col, 0.0, NEG_BIG)

    def sel_bias_fn(g):
        def fn(kt):
            row = lax.broadcasted_iota(i32, (nb, tk), 0)
            col = lax.broadcasted_iota(i32, (nb, tk), 1)
            expand = jnp.where(row == kt * blocks_per_tile + col // BLK, 1.0, 0.0).astype(bf16)
            return jnp.dot(sel_bias[g], expand, preferred_element_type=f32) + causal_bias(kt * tk, tk)
        return fn

    tkw = min(WINDOW + tq, t_len)
    win_start = jnp.clip(q0 - WINDOW, 0, t_len - tkw)

    def win_bias_fn(kt):
        dist = t_col - (win_start + lax.broadcasted_iota(i32, (1, tkw), 1))
        return jnp.where((dist >= 0) & (dist < WINDOW), 0.0, NEG_BIG)

    qi_heads = [zq_ref[0, :, ZQ_QI + h * IDX_DIM: ZQ_QI + (h + 1) * IDX_DIM] for h in range(IDX_HEADS)]
    wi_cols = [misc[:, MISC_WI + h: MISC_WI + h + 1] for h in range(IDX_HEADS)]

    def index_tile(kt, carry):
        start = pl.multiple_of(kt * tk, tk)
        kik = ki_ref[pl.ds(start, tk), :]
        tot = jnp.zeros((tq, tk), f32)
        for h in range(IDX_HEADS):
            sc = _dot_nt(qi_heads[h], kik) * (IDX_DIM ** -0.5)
            tot = tot + jnp.maximum(sc, 0.0) * wi_cols[h]
        tot = tot * (IDX_HEADS ** -0.5)
        tot = jnp.where(tot == 0.0, 0.0, tot)
        bits = lax.bitcast_convert_type(tot, i32)
        key = jnp.where(bits < 0, bits ^ jnp.int32(0x7FFFFFFF), bits)
        sp = start + lax.broadcasted_iota(i32, (1, tk), 1)
        key_ref[kt] = jnp.where(sp <= t_col, key, INT_MIN)
        return carry

    lax.fori_loop(0, n_sel_tiles, index_tile, 0)

    def count_ge(trial):
        def body(kt, acc):
            c = (key_ref[kt] >= trial).astype(i32)
            for j in range(tk // LANES):
                acc = acc + c[:, j * LANES:(j + 1) * LANES]
            return acc
        acc = lax.fori_loop(0, n_sel_tiles, body, jnp.zeros((tq, LANES), i32))
        return jnp.sum(acc, axis=-1, keepdims=True)

    c_pos = count_ge(jnp.zeros((tq, 1), i32))
    v0 = jnp.where(c_pos >= topk, 0, INT_MIN).astype(i32)
    c0 = jnp.where(c_pos >= topk, c_pos, n_sel_tiles * tk)

    def bit_step(i, carry):
        v, cge = carry
        trial = v | lax.shift_left(jnp.int32(1), 30 - i)
        c = count_ge(trial)
        ok = c >= topk
        return jnp.where(ok, trial, v), jnp.where(ok, c, cge)

    v_thr, c_ge = lax.fori_loop(0, 31, bit_step, (v0, c0))
    v_eff = jnp.maximum(v_thr, INT_MIN + 1)

    def write_plain(kt, carry):
        dbias_ref[kt] = jnp.where(key_ref[kt] >= v_eff, 0.0, NEG_BIG)
        return carry

    def plain_path():
        lax.fori_loop(0, n_sel_tiles, write_plain, 0)

    def tie_path():
        need = topk - count_ge(v_thr + 1)

        def count_tie_below(limit):
            def body(kt, acc):
                sp = kt * tk + lax.broadcasted_iota(i32, (1, tk), 1)
                c = ((key_ref[kt] == v_thr) & (sp < limit)).astype(i32)
                for j in range(tk // LANES):
                    acc = acc + c[:, j * LANES:(j + 1) * LANES]
                return acc
            acc = lax.fori_loop(0, n_sel_tiles, body, jnp.zeros((tq, LANES), i32))
            return jnp.sum(acc, axis=-1, keepdims=True)

        n_bits = max(1, (t_len - 1).bit_length())

        def idx_step(i, w):
            trial = w | lax.shift_left(jnp.int32(1), n_bits - 1 - i)
            return jnp.where(count_tie_below(trial) < need, trial, w)

        w_last = lax.fori_loop(0, n_bits, idx_step, jnp.zeros((tq, 1), i32))

        def write_tie(kt, carry):
            sp = kt * tk + lax.broadcasted_iota(i32, (1, tk), 1)
            key = key_ref[kt]
            take = (key > v_thr) | ((key == v_thr) & (sp <= w_last))
            dbias_ref[kt] = jnp.where(take & (key >= v_eff), 0.0, NEG_BIG)
            return carry

        lax.fori_loop(0, n_sel_tiles, write_tie, 0)

    lax.cond(jnp.max(c_ge) > topk, tie_path, plain_path)

    def dsa_bias_fn(kt):
        return dbias_ref[kt]

    for g in range(KV_GROUPS):
        hs = [g * heads_per_group + r for r in range(heads_per_group)]
        slopes = [ALIBI[h] for h in hs]
        qn = [q_head(O_Q, h) for h in hs]
        qd = [q_head(O_QD, h) for h in hs]
        a_s = _flash_heads(qn, slopes, g, kvb_ref, 2, 3, tk, 0, n_sel_tiles, sel_bias_fn(g), sel_skip_fn(g))
        a_w = _flash_heads(qn, slopes, g, kvb_ref, 6, 7, tkw, 0, 1, win_bias_fn, start_fn=lambda kt: win_start)
        a_d = _flash_heads(qd, slopes, g, kvb_ref, 4, 5, tk, 0, n_sel_tiles, dsa_bias_fn)
        for jj in range(heads_per_group // 2):
            j = hs[2 * jj] // 2
            o_n = (gate_pair(j, 0) * pair(o_cmp[2 * j], o_cmp[2 * j + 1])
                   + gate_pair(j, 1) * normalised_pair(a_s[2 * jj], a_s[2 * jj + 1], g)
                   + gate_pair(j, 2) * normalised_pair(a_w[2 * jj], a_w[2 * jj + 1], g))
            o_ref[0, :, j * LANES:(j + 1) * LANES] = o_n.astype(o_ref.dtype)
            o_d = normalised_pair(a_d[2 * jj], a_d[2 * jj + 1], g)
            o_ref[0, :, HALF_W + j * LANES:HALF_W + (j + 1) * LANES] = o_d.astype(o_ref.dtype)


def _odd_attn_prompt(zq, misc, kvb, wc2, bsz, t_len, tq=256, tk=512):
    tk = min(tk, t_len)
    topk = min(DSA_TOPK_MAX, t_len // 4)
    nb = t_len // BLK
    n_kt = t_len // tk
    zq3 = zq.reshape(bsz, t_len, zq.shape[1])
    misc3 = misc.reshape(bsz, t_len, LANES)
    kvb3 = kvb.reshape(bsz, t_len, kvb.shape[1])
    kern = functools.partial(_odd_attn_kernel, t_len=t_len, tq=tq, tk=tk, topk=topk)
    out = pl.pallas_call(
        kern,
        grid=(bsz, t_len // tq),
        in_specs=[pl.BlockSpec((1, tq, zq.shape[1]), lambda b, i: (b, i, 0)),
                  pl.BlockSpec((1, tq, LANES), lambda b, i: (b, i, 0)),
                  pl.BlockSpec((1, t_len, kvb.shape[1]), lambda b, i: (b, 0, 0)),
                  pl.BlockSpec((1, t_len, LANES), lambda b, i: (b, 0, 0)),
                  pl.BlockSpec((2, BLK, LANES), lambda b, i: (0, 0, 0))],
        out_specs=pl.BlockSpec((1, tq, 2 * HALF_W), lambda b, i: (b, i, 0)),
        out_shape=jax.ShapeDtypeStruct((bsz, t_len, 2 * HALF_W), jnp.bfloat16),
        scratch_shapes=[pltpu.VMEM((nb, LANES), jnp.bfloat16),
                        pltpu.VMEM((KV_GROUPS, nb, LANES), jnp.bfloat16),
                        pltpu.VMEM((t_len, IDX_DIM), jnp.bfloat16),
                        pltpu.VMEM((n_kt, tq, tk), jnp.int32),
                        pltpu.VMEM((n_kt, tq, tk), jnp.float32),
                        pltpu.SMEM((KV_GROUPS, n_kt), jnp.int32)],
        compiler_params=pltpu.CompilerParams(
            dimension_semantics=("arbitrary", "arbitrary"), vmem_limit_bytes=VMEM_LIMIT),
        name="odd_attn_prompt",
    )(zq3, misc3, kvb3, misc3, wc2)
    return out.reshape(bsz * t_len, 2 * HALF_W)


PAGE = 128
ISC_ROWS = 24


def _odd_sample_kernel(pt_ref, qn_ref, qd_ref, qi_ref, wi_ref, gate_ref, newb_ref, newki_ref, neww_ref,
                       win_ref, wc_ref, wct_ref, *rest, n_pages, topk, nbp):
    del pt_ref
    f32, bf16, i32 = jnp.float32, jnp.bfloat16, jnp.int32
    hi_prec = lax.Precision.HIGHEST
    kv_refs, idx_refs = rest[:n_pages], rest[n_pages:2 * n_pages]
    o_ref, wout_ref, isc_ref = rest[2 * n_pages:]
    past = n_pages * PAGE
    nb = past // BLK + 1
    row8 = lax.broadcasted_iota(i32, (8, 1), 0)
    lane = lax.broadcasted_iota(i32, (1, LANES), 1)
    slope8 = jnp.zeros((8, 1), f32)
    for h in range(NSA_HEADS):
        slope8 = jnp.where(row8 == h, ALIBI[h], slope8)
    low_group = row8 < NSA_HEADS // KV_GROUPS

    def pick_half(x):
        return jnp.where(low_group, x[:, 0:HEAD_DIM], x[:, HEAD_DIM:2 * HEAD_DIM])

    qn8, qd8 = qn_ref[0], qd_ref[0]
    qi8, wi8 = qi_ref[0], wi_ref[0]
    newb = newb_ref[0]
    gates = jax.nn.sigmoid(gate_ref[0])

    def new_score(q8, k_row):
        return jnp.sum(q8.astype(f32) * k_row.astype(f32), axis=-1, keepdims=True) + slope8 * float(past)

    def feat(p, slot):
        return kv_refs[p][0, slot * LANES:(slot + 1) * LANES, :]

    s_sel, s_dsa = [], []
    key_row = lax.broadcasted_iota(i32, (PAGE, nbp), 0)
    blk_col = lax.broadcasted_iota(i32, (PAGE, nbp), 1)
    kct = jnp.zeros((LANES, nbp), f32)
    vct = jnp.zeros((LANES, nbp), f32)
    for p in range(n_pages):
        pool = jnp.where(blk_col == 2 * p + key_row // BLK, 1.0, 0.0)
        kct = kct + jnp.dot(feat(p, 0) * wct_ref[0], pool, preferred_element_type=f32, precision=hi_prec)
        vct = vct + jnp.dot(feat(p, 1) * wct_ref[1], pool, preferred_element_type=f32, precision=hi_prec)
        pos = (p * PAGE + lane).astype(f32)
        s_sel.append(jnp.dot(qn8, feat(p, 2).astype(bf16), preferred_element_type=f32) + slope8 * pos)
        s_dsa.append(jnp.dot(qd8, feat(p, 4).astype(bf16), preferred_element_type=f32) + slope8 * pos)
        ii = jnp.dot(qi8, idx_refs[p][0].astype(bf16), preferred_element_type=f32) * (IDX_DIM ** -0.5)
        isc_ref[p:p + 1, :] = jnp.sum(jnp.maximum(ii, 0.0) * wi8, axis=0, keepdims=True) * (IDX_HEADS ** -0.5)
    first_row = row8 == 0
    place = jnp.where(first_row & (lax.broadcasted_iota(i32, (8, nbp), 1) == nb - 1), 1.0, 0.0)
    tn = (((0,), (0,)), ((), ()))
    new_k = jnp.where(first_row, newb[:, 0:LANES].astype(f32) * wc_ref[0, 0:1, :], 0.0)
    new_v = jnp.where(first_row, newb[:, LANES:2 * LANES].astype(f32) * wc_ref[1, 0:1, :], 0.0)
    kct = kct + lax.dot_general(new_k, place, tn, preferred_element_type=f32, precision=hi_prec)
    vct = vct + lax.dot_general(new_v, place, tn, preferred_element_type=f32, precision=hi_prec)
    ii_new = jnp.sum(qi8.astype(f32) * newki_ref[0].astype(bf16).astype(f32), axis=-1, keepdims=True) * (IDX_DIM ** -0.5)
    isc_new = jnp.sum(jnp.maximum(ii_new, 0.0) * wi8, axis=0, keepdims=True) * (IDX_HEADS ** -0.5)
    isc_ref[n_pages:n_pages + 1, :] = jnp.where(lane == 0, isc_new, -jnp.inf)
    isc_ref[n_pages + 1:ISC_ROWS, :] = jnp.full((ISC_ROWS - n_pages - 1, LANES), -jnp.inf, f32)

    jb = lax.broadcasted_iota(i32, (1, nbp), 1)
    dist_c = past - (jb * BLK + (BLK - 1))
    mask_c = (dist_c >= 0) & (jb < nb)
    s_c = jnp.dot(qn8, kct.astype(bf16), preferred_element_type=f32) - slope8 * dist_c.astype(f32)
    s_c = jnp.where(mask_c, s_c, NEG_BIG)
    m_c = jnp.max(s_c, axis=-1, keepdims=True)
    e_c = jnp.where(mask_c, jnp.exp(s_c - m_c), 0.0)
    p_c = e_c * (1.0 / jnp.maximum(jnp.sum(e_c, axis=-1, keepdims=True), 1e-30))
    o_c = pick_half(_dot_nt(p_c.astype(bf16), vct.astype(bf16)))
    imp_lo = jnp.sum(jnp.where(low_group, p_c, 0.0), axis=0, keepdims=True)
    imp_hi = jnp.sum(jnp.where(low_group, 0.0, p_c), axis=0, keepdims=True)
    imp = jnp.where(low_group, imp_lo, imp_hi)
    cur = past // BLK
    imp = jnp.where((jb == 0) | (jb == cur) | (jb == cur - 1), FORCE_SCORE, imp)
    imp = jnp.where(jb <= cur, imp, -jnp.inf)
    rank = jnp.zeros((8, nbp), i32)
    for j in range(nb):
        other = imp[:, j:j + 1]
        rank = rank + ((other > imp) | ((other == imp) & (j < jb))).astype(i32)
    sel = jnp.where((rank < min(NSA_TOPN, nb)) & (imp > -jnp.inf), 1.0, 0.0)

    tot = isc_ref[...]
    tot = jnp.where(tot == 0.0, 0.0, tot)
    bits = lax.bitcast_convert_type(tot, i32)
    key = jnp.where(bits < 0, bits ^ jnp.int32(0x7FFFFFFF), bits)
    krow = lax.broadcasted_iota(i32, (ISC_ROWS, LANES), 0)
    kidx = krow * LANES + lax.broadcasted_iota(i32, (ISC_ROWS, LANES), 1)
    key = jnp.where(kidx <= past, key, INT_MIN)

    def total(x):
        return jnp.sum(jnp.sum(x, axis=1, keepdims=True), axis=0, keepdims=True)

    def count_ge(trial):
        return total((key >= trial).astype(i32))

    c_pos = count_ge(jnp.zeros((1, 1), i32))
    v_thr = jnp.where(c_pos >= topk, 0, INT_MIN).astype(i32)
    c_ge = jnp.where(c_pos >= topk, c_pos, ISC_ROWS * LANES)
    for shift, width in ((27, 4), (23, 4), (19, 4), (15, 4), (11, 4), (7, 4), (3, 4), (0, 3)):
        digit = jnp.zeros((1, 1), i32)
        for j in range(1, 1 << width):
            c = count_ge(v_thr | jnp.int32(j << shift))
            ok = c >= topk
            digit = digit + ok.astype(i32)
            c_ge = jnp.where(ok, jnp.minimum(c_ge, c), c_ge)
        v_thr = v_thr | lax.shift_left(digit, jnp.int32(shift))
    v_eff = jnp.maximum(v_thr, INT_MIN + 1)

    def tie_mask():
        need = topk - count_ge(v_thr + 1)
        tie = key == v_thr
        w_last = jnp.zeros((1, 1), i32)
        for bit in range((ISC_ROWS * LANES - 1).bit_length() - 1, -1, -1):
            trial = w_last | jnp.int32(1 << bit)
            below = total((tie & (kidx < trial)).astype(i32))
            w_last = jnp.where(below < need, trial, w_last)
        take = (key > v_thr) | (tie & (kidx <= w_last))
        return jnp.where(take & (key >= v_eff), 1.0, 0.0)

    def plain_mask():
        return jnp.where(key >= v_eff, 1.0, 0.0)

    dsel = lax.cond(c_ge[0, 0] > topk, tie_mask, plain_mask)

    def attend(tiles, masks, s_new, new_ok, v_tile, v_new):
        tiles = [jnp.where(mk, t, NEG_BIG) for t, mk in zip(tiles, masks)]
        s_new = jnp.where(new_ok, s_new, NEG_BIG)
        m = s_new
        for t in tiles:
            m = jnp.maximum(m, jnp.max(t, axis=-1, keepdims=True))
        l = jnp.exp(s_new - m)
        acc = l * v_new.astype(f32)
        for j, t in enumerate(tiles):
            e = jnp.exp(t - m)
            l = l + jnp.sum(e, axis=-1, keepdims=True)
            acc = acc + _dot_nt(e.astype(bf16), v_tile(j))
        return pick_half(acc * (1.0 / l))

    sel_masks = [jnp.where(lane < BLK, sel[:, 2 * p:2 * p + 1], sel[:, 2 * p + 1:2 * p + 2]) > 0.0
                 for p in range(n_pages)]
    o_s = attend(s_sel, sel_masks, new_score(qn8, newb[:, 2 * LANES:3 * LANES]), sel[:, nb - 1:nb] > 0.0,
                 lambda p: feat(p, 3).astype(bf16), newb[:, 3 * LANES:4 * LANES])
    dsa_masks = [dsel[p:p + 1, :] > 0.0 for p in range(n_pages)]
    o_d = attend(s_dsa, dsa_masks, new_score(qd8, newb[:, 4 * LANES:5 * LANES]), dsel[n_pages:n_pages + 1, 0:1] > 0.0,
                 lambda p: feat(p, 5).astype(bf16), newb[:, 5 * LANES:6 * LANES])
    n_win = win_ref.shape[2]
    wlane = lax.broadcasted_iota(i32, (1, n_win), 1)
    wpos = past - n_win + wlane
    s_w = jnp.dot(qn8, win_ref[0, 0:LANES, :].astype(bf16), preferred_element_type=f32) + slope8 * wpos.astype(f32)
    w_ok = (past - wpos < WINDOW) & (wpos >= 0)
    o_w = attend([s_w], [w_ok], new_score(qn8, newb[:, 6 * LANES:7 * LANES]), True,
                 lambda p: win_ref[0, LANES:2 * LANES, :].astype(bf16), newb[:, 7 * LANES:8 * LANES])
    o_ref[0, 0:NSA_HEADS, :] = gates[:, 0:1] * o_c + gates[:, 1:2] * o_s + gates[:, 2:3] * o_w
    o_ref[0, NSA_HEADS:2 * NSA_HEADS, :] = o_d
    last = jnp.where(first_row & (lax.broadcasted_iota(i32, (8, n_win), 1) == n_win - 1), 1.0, 0.0)
    new_w = jnp.where(first_row, neww_ref[0], 0.0)
    placed = lax.dot_general(new_w, last, tn, preferred_element_type=f32, precision=hi_prec)
    wout_ref[0] = jnp.where(wlane == n_win - 1, placed, pltpu.roll(win_ref[0], n_win - 1, 1))


def _odd_attn_sample(zq, misc, kvb, win_new, cache_kv, cache_idx, cache_win, page_table, w_cmp):
    f32, bf16 = jnp.float32, jnp.bfloat16
    s_n, n_pages = page_table.shape
    n_pool = cache_kv.shape[0]
    n_win = cache_win.shape[1]
    past = n_pages * PAGE
    topk = min(DSA_TOPK_MAX, (past + 1) // 4)
    nbp = _round_up(past // BLK + 1, 8)
    wc2 = jnp.concatenate([w_cmp, w_cmp], axis=-1)
    wct = jnp.tile(jnp.swapaxes(w_cmp, 1, 2), (1, KV_GROUPS, PAGE // BLK))
    qn = zq[:, O_Q:O_QD].reshape(s_n, NSA_HEADS, LANES)
    qd = zq[:, O_QD:ZQ_QI].reshape(s_n, DSA_HEADS, LANES)
    qi = jnp.pad(zq[:, ZQ_QI:ZQ_QI + LANES].reshape(s_n, IDX_HEADS, IDX_DIM), ((0, 0), (0, 8 - IDX_HEADS), (0, 0)))
    wi = jnp.pad(misc[:, MISC_WI:MISC_WI + IDX_HEADS].reshape(s_n, IDX_HEADS, 1), ((0, 0), (0, 8 - IDX_HEADS), (0, 0)))
    gts = misc[:, MISC_GATE:MISC_GATE + 3 * NSA_HEADS].reshape(s_n, NSA_HEADS, 3)
    kv_pages = jnp.transpose(cache_kv, (0, 2, 3, 4, 1)).reshape(n_pool, KV_SLOTS * LANES, PAGE)
    idx_pages = jnp.transpose(cache_idx, (0, 2, 1))
    win_t = jnp.transpose(cache_win, (0, 2, 3, 4, 1)).reshape(s_n, 2 * LANES, n_win)
    per_seq = lambda shape: pl.BlockSpec((1,) + shape, lambda b, pt: (b, 0, 0))
    page_spec = lambda rows, p: pl.BlockSpec((1, rows, PAGE), lambda b, pt, p=p: (pt[b, p], 0, 0))
    in_specs = ([per_seq((NSA_HEADS, LANES)), per_seq((DSA_HEADS, LANES)), per_seq((8, IDX_DIM)),
                 per_seq((8, 1)), per_seq((NSA_HEADS, 3)), per_seq((1, 8 * LANES)), per_seq((1, IDX_DIM)),
                 per_seq((1, 2 * LANES)), per_seq((2 * LANES, n_win)),
                 pl.BlockSpec((2, BLK, LANES), lambda b, pt: (0, 0, 0)),
                 pl.BlockSpec((2, LANES, PAGE), lambda b, pt: (0, 0, 0))]
                + [page_spec(KV_SLOTS * LANES, p) for p in range(n_pages)]
                + [page_spec(IDX_DIM, p) for p in range(n_pages)])
    kern = functools.partial(_odd_sample_kernel, n_pages=n_pages, topk=topk, nbp=nbp)
    o, win_out = pl.pallas_call(
        kern,
        grid_spec=pltpu.PrefetchScalarGridSpec(
            num_scalar_prefetch=1,
            grid=(s_n,),
            in_specs=in_specs,
            out_specs=[per_seq((2 * NSA_HEADS, HEAD_DIM)), per_seq((2 * LANES, n_win))],
            scratch_shapes=[pltpu.VMEM((ISC_ROWS, LANES), f32)]),
        out_shape=[jax.ShapeDtypeStruct((s_n, 2 * NSA_HEADS, HEAD_DIM), f32),
                   jax.ShapeDtypeStruct((s_n, 2 * LANES, n_win), f32)],
        compiler_params=pltpu.CompilerParams(
            dimension_semantics=("arbitrary",), vmem_limit_bytes=VMEM_LIMIT),
        name="odd_attn_sample",
    )(page_table, qn, qd, qi, wi, gts, kvb.reshape(s_n, 1, 8 * LANES), misc[:, 0:IDX_DIM].reshape(s_n, 1, IDX_DIM),
      win_new.reshape(s_n, 1, 2 * LANES), win_t, wc2, wct,
      *([kv_pages] * n_pages), *([idx_pages] * n_pages))
    win_out = jnp.transpose(win_out.reshape(s_n, 2, KV_GROUPS, HEAD_DIM, n_win), (0, 4, 1, 2, 3))
    return o.reshape(s_n, 2 * HALF_W), win_out


def _split(z, sizes):
    cuts = [int(c) for c in np.cumsum(sizes)[:-1]]
    return jnp.split(z, cuts, axis=-1)


def _pad_cols(w, n):
    return jnp.pad(w, ((0, 0), (0, n - w.shape[1])))


def kernel(x_prompt, x_sample, state_conv, state_C, state_n, state_m, cache_kv, cache_idx, cache_win, page_table,
           w_in_e, gate_b_e, conv_w, conv_b, w_out_e, w_in_o, w_cmp, w_out_o, w_up, w_down, ln_g, ln_b):
    f32, bf16 = jnp.float32, jnp.bfloat16
    bp, tp, d = x_prompt.shape
    dbs = x_sample.shape[0]
    keep = cache_win.shape[2]
    yp = x_prompt.reshape(bp * tp, d)
    ys = x_sample.reshape(dbs, d)
    tm_p, tm_s = 512, dbs
    outs = {}
    for layer in range(DEPTH):
        i = layer // 2
        if layer % 2 == 0:
            w_in = _pad_cols(w_in_e[i], E_PACKED).astype(bf16)
            w_out = w_out_e[i].astype(bf16)
            ya, qkv, og, gts, tails = _even_proj(yp, w_in, conv_w[i], conv_b[i], tm_p, tp)
            yb, c_fin, n_fin, m_fin = _mlstm_prompt(qkv, og, gts, gate_b_e[i], bp, tp)
            cvp = tails.reshape(bp, tp // tm_p, CONV_TAIL, HALF_W)[:, -1, CONV_TAIL - (CONV_W - 1):]
            ya_s, yb_s, cvs, c_s, n_s, m_s = _even_sample(_proj(ys, w_in, tm_s), gate_b_e[i], conv_w[i], conv_b[i],
                                                          state_conv[i], state_C[i], state_n[i], state_m[i])
            outs['conv'] = (cvp[None], cvs[None])
            outs['c'] = (c_fin[None], c_s[None])
            outs['n'] = (n_fin[None], n_s[None])
            outs['m'] = (m_fin[:, :, 0][None], m_s[:, :, 0][None])
            yp = _outproj2_ln(ya, yb.reshape(bp * tp, HALF_W), w_out, yp, ln_g[layer, 0], ln_b[layer, 0], tm_p)
            ys = _outproj2_ln(ya_s, yb_s, w_out, ys, ln_g[layer, 0], ln_b[layer, 0], tm_s)
        else:
            w_out = w_out_o[i].astype(bf16)
            w_packed = _pack_w_in_o(w_in_o[i]).astype(bf16)
            wc2 = jnp.concatenate([w_cmp[i], w_cmp[i]], axis=-1)
            rows_first = lambda kvt: jnp.transpose(
                kvt.reshape(kvt.shape[0], KV_SLOTS, KV_GROUPS, HEAD_DIM, kvt.shape[2]), (0, 4, 1, 2, 3))
            zq, kvt_p, kvb, winp, miscp = _odd_proj(yp, w_packed, tm_p, tp)
            mp = _odd_attn_prompt(zq, miscp, kvb, wc2, bp, tp)
            kvp = rows_first(kvt_p)
            ixp = miscp[:, :IDX_DIM].reshape(bp, tp, IDX_DIM)
            wnp = winp.reshape(bp, tp, 2, KV_GROUPS, HEAD_DIM)[:, tp - keep:]
            zq_s, kvt_s, kvb_s, win_s, misc_s = _odd_proj(ys, w_packed, tm_s, dbs)
            ms, wns = _odd_attn_sample(zq_s, misc_s, kvb_s, win_s, cache_kv[i], cache_idx[i], cache_win[i],
                                       page_table, w_cmp[i])
            kvs = rows_first(kvt_s).reshape(dbs, 1, KV_SLOTS, KV_GROUPS, HEAD_DIM)
            ixs = misc_s[:, :IDX_DIM].reshape(dbs, 1, IDX_DIM)
            outs['kv'] = (kvp[None], kvs[None])
            outs['idx'] = (ixp[None], ixs[None])
            outs['win'] = (wnp[None], wns[None])
            yp = _outproj_ln(mp, w_out, yp, ln_g[layer, 0], ln_b[layer, 0], tm_p)
            ys = _outproj_ln(ms, w_out, ys, ln_g[layer, 0], ln_b[layer, 0], tm_s)
        wu, wd = w_up[layer].astype(bf16), w_down[layer].astype(bf16)
        yp = _mlp_ln(yp, wu, wd, ln_g[layer, 1], ln_b[layer, 1], tm_p, 1024)
        ys = _mlp_ln(ys, wu, wd, ln_g[layer, 1], ln_b[layer, 1], tm_s, 1024)
    return (yp.reshape(bp, tp, d), ys.reshape(dbs, 1, d),
            outs['conv'][0], outs['conv'][1], outs['c'][0], outs['c'][1],
            outs['n'][0], outs['n'][1], outs['m'][0], outs['m'][1],
            outs['kv'][0], outs['kv'][1], outs['idx'][0], outs['idx'][1],
            outs['win'][0], outs['win'][1])
```

```python
import functools

import jax
import jax.numpy as jnp
import numpy as np
from jax import lax
from jax.experimental import pallas as pl
from jax.experimental.pallas import tpu as pltpu

D_MODEL = 1024
DEPTH = 2
HALF_W = 512
D_FF = 4096
CONV_W = 3
ML_HEADS = 4
ML_DIM = 128
HEAD_DIM = 64
KV_GROUPS = 2
NSA_HEADS = 8
DSA_HEADS = 8
BLK = 64
NSA_TOPN = 8
WINDOW = 256
IDX_HEADS = 4
IDX_DIM = 32
DSA_TOPK_MAX = 256
KV_SLOTS = 6
FORCE_SCORE = 1e4
ALPHA = (2.0 * DEPTH) ** 0.25
LN_EPS = 1e-5
O_SIZES = (512, 256, 256, 256, 24, 512, 256, 128, 32, 4)
LANES = 128
VMEM_LIMIT = 48 * 1024 * 1024


def _round_up(n, m):
    return -(-n // m) * m


def _proj_kernel(x_ref, w_ref, o_ref):
    o_ref[...] = jnp.dot(x_ref[...].astype(jnp.bfloat16), w_ref[...],
                         preferred_element_type=jnp.float32)


def _proj(x, w_bf16, tm):
    m, k = x.shape
    n = w_bf16.shape[1]
    tn = n
    for cand in (1024, 768, 512, 256, 128):
        if n % cand == 0:
            tn = cand
            break
    return pl.pallas_call(
        _proj_kernel,
        grid=(n // tn, m // tm),
        in_specs=[pl.BlockSpec((tm, k), lambda j, i: (i, 0)),
                  pl.BlockSpec((k, tn), lambda j, i: (0, j))],
        out_specs=pl.BlockSpec((tm, tn), lambda j, i: (i, j)),
        out_shape=jax.ShapeDtypeStruct((m, n), jnp.float32),
        compiler_params=pltpu.CompilerParams(
            dimension_semantics=("arbitrary", "arbitrary"), vmem_limit_bytes=VMEM_LIMIT),
        name="proj",
    )(x, w_bf16)


def _layer_norm_rows(v, g, b):
    mu = jnp.mean(v, axis=-1, keepdims=True)
    d = v - mu
    var = jnp.mean(d * d, axis=-1, keepdims=True)
    return d * lax.rsqrt(var + LN_EPS) * g + b


def _outproj_ln_kernel(y_ref, w_ref, x_ref, g_ref, b_ref, o_ref):
    mix = jnp.dot(y_ref[...].astype(jnp.bfloat16), w_ref[...], preferred_element_type=jnp.float32)
    o_ref[...] = _layer_norm_rows(ALPHA * x_ref[...] + mix, g_ref[...], b_ref[...])


def _outproj_ln(y, w_bf16, x, g, b, tm):
    m, d = x.shape
    k = y.shape[1]
    return pl.pallas_call(
        _outproj_ln_kernel,
        grid=(m // tm,),
        in_specs=[pl.BlockSpec((tm, k), lambda i: (i, 0)),
                  pl.BlockSpec((k, d), lambda i: (0, 0)),
                  pl.BlockSpec((tm, d), lambda i: (i, 0)),
                  pl.BlockSpec((1, d), lambda i: (0, 0)),
                  pl.BlockSpec((1, d), lambda i: (0, 0))],
        out_specs=pl.BlockSpec((tm, d), lambda i: (i, 0)),
        out_shape=jax.ShapeDtypeStruct((m, d), jnp.float32),
        compiler_params=pltpu.CompilerParams(
            dimension_semantics=("arbitrary",), vmem_limit_bytes=VMEM_LIMIT),
        name="outproj_ln",
    )(y, w_bf16, x, g.reshape(1, d), b.reshape(1, d))


def _mlp_ln_kernel(x_ref, wu_ref, wd_ref, g_ref, b_ref, o_ref, acc_ref):
    f = pl.program_id(1)

    @pl.when(f == 0)
    def _():
        acc_ref[...] = jnp.zeros_like(acc_ref)

    up = jnp.dot(x_ref[...].astype(jnp.bfloat16), wu_ref[...], preferred_element_type=jnp.float32)
    act = jnp.square(jnp.maximum(up, 0.0))
    acc_ref[...] += jnp.dot(act.astype(jnp.bfloat16), wd_ref[...], preferred_element_type=jnp.float32)

    @pl.when(f == pl.num_programs(1) - 1)
    def _():
        o_ref[...] = _layer_norm_rows(ALPHA * x_ref[...] + acc_ref[...], g_ref[...], b_ref[...])


def _mlp_ln(x, wu_bf16, wd_bf16, g, b, tm, tf):
    m, d = x.shape
    ff = wu_bf16.shape[1]
    return pl.pallas_call(
        _mlp_ln_kernel,
        grid=(m // tm, ff // tf),
        in_specs=[pl.BlockSpec((tm, d), lambda i, f: (i, 0)),
                  pl.BlockSpec((d, tf), lambda i, f: (0, f)),
                  pl.BlockSpec((tf, d), lambda i, f: (f, 0)),
                  pl.BlockSpec((1, d), lambda i, f: (0, 0)),
                  pl.BlockSpec((1, d), lambda i, f: (0, 0))],
        out_specs=pl.BlockSpec((tm, d), lambda i, f: (i, 0)),
        out_shape=jax.ShapeDtypeStruct((m, d), jnp.float32),
        scratch_shapes=[pltpu.VMEM((tm, d), jnp.float32)],
        compiler_params=pltpu.CompilerParams(
            dimension_semantics=("arbitrary", "arbitrary"), vmem_limit_bytes=VMEM_LIMIT),
        name="mlp_ln",
    )(x, wu_bf16, wd_bf16, g.reshape(1, d), b.reshape(1, d))


E_QKV = 3 * HALF_W
E_OG = 6 * HALF_W
E_GATE = 7 * HALF_W
E_PACKED = 7 * HALF_W + LANES
CONV_TAIL = 8


def _even_proj_kernel(x_ref, w_ref, cw_ref, cb_ref, ya_ref, qkv_ref, og_ref, gate_ref, tail_ref, carry_ref,
                      *, tiles_per_seq):
    f32, bf16 = jnp.float32, jnp.bfloat16
    tm = x_ref.shape[0]

    @pl.when(pl.program_id(0) % tiles_per_seq == 0)
    def _():
        carry_ref[...] = jnp.zeros_like(carry_ref)

    z = jnp.dot(x_ref[...].astype(bf16), w_ref[...], preferred_element_type=f32)
    u = z[:, 2 * HALF_W:3 * HALF_W] * z[:, 0:HALF_W]
    prev = carry_ref[...]
    row = lax.broadcasted_iota(jnp.int32, (tm, 1), 0)
    conv = cb_ref[...] + cw_ref[CONV_W - 1:CONV_W, :] * u
    for back in range(1, CONV_W):
        shifted = pltpu.roll(u, back, 0)
        for r in range(back):
            shifted = jnp.where(row == r, prev[CONV_TAIL - back + r:CONV_TAIL - back + r + 1, :], shifted)
        conv = conv + cw_ref[CONV_W - 1 - back:CONV_W - back, :] * shifted
    ya_ref[...] = (z[:, HALF_W:2 * HALF_W] * conv).astype(bf16)
    tail = u[tm - CONV_TAIL:]
    carry_ref[...] = tail
    tail_ref[0] = tail
    qkv_ref[:, 0:HALF_W] = z[:, E_QKV:E_QKV + HALF_W].astype(bf16)
    qkv_ref[:, HALF_W:2 * HALF_W] = (z[:, E_QKV + HALF_W:E_QKV + 2 * HALF_W] * (ML_DIM ** -0.5)).astype(bf16)
    qkv_ref[:, 2 * HALF_W:3 * HALF_W] = z[:, E_QKV + 2 * HALF_W:E_OG].astype(bf16)
    og_ref[...] = z[:, E_OG:E_GATE]
    gate_ref[...] = z[:, E_GATE:E_PACKED]


def _even_proj(x, w_bf16, conv_w, conv_b, tm, seq_len):
    m, k = x.shape
    widths = (HALF_W, 3 * HALF_W, HALF_W, LANES)
    dtypes = (jnp.bfloat16, jnp.bfloat16, jnp.float32, jnp.float32)
    kern = functools.partial(_even_proj_kernel, tiles_per_seq=seq_len // tm)
    return pl.pallas_call(
        kern,
        grid=(m // tm,),
        in_specs=[pl.BlockSpec((tm, k), lambda i: (i, 0)),
                  pl.BlockSpec((k, E_PACKED), lambda i: (0, 0)),
                  pl.BlockSpec((CONV_W, HALF_W), lambda i: (0, 0)),
                  pl.BlockSpec((1, HALF_W), lambda i: (0, 0))],
        out_specs=[pl.BlockSpec((tm, n), lambda i: (i, 0)) for n in widths]
        + [pl.BlockSpec((1, CONV_TAIL, HALF_W), lambda i: (i, 0, 0))],
        out_shape=[jax.ShapeDtypeStruct((m, n), dt) for n, dt in zip(widths, dtypes)]
        + [jax.ShapeDtypeStruct((m // tm, CONV_TAIL, HALF_W), jnp.float32)],
        scratch_shapes=[pltpu.VMEM((CONV_TAIL, HALF_W), jnp.float32)],
        compiler_params=pltpu.CompilerParams(
            dimension_semantics=("arbitrary",), vmem_limit_bytes=VMEM_LIMIT),
        name="even_proj",
    )(x, w_bf16, conv_w, conv_b.reshape(1, HALF_W))


def _mlstm_kernel(qkv_ref, og_ref, gate_ref, gb_ref, yb_ref, c_ref, n_ref, m_ref):
    f32, bf16 = jnp.float32, jnp.bfloat16
    L = qkv_ref.shape[1]

    @pl.when(pl.program_id(1) == 0)
    def _():
        c_ref[...] = jnp.zeros_like(c_ref)
        n_ref[...] = jnp.zeros_like(n_ref)
        m_ref[...] = jnp.zeros_like(m_ref)

    pre = gate_ref[0] + gb_ref[...]
    lf = jax.nn.log_sigmoid(pre)
    row = lax.broadcasted_iota(jnp.int32, (L, L), 0)
    col = lax.broadcasted_iota(jnp.int32, (L, L), 1)
    causal = col <= row
    tri = jnp.where(causal, 1.0, 0.0).astype(f32)
    b_all = jnp.dot(tri, lf, preferred_element_type=f32, precision=lax.Precision.HIGHEST)
    pre_t = pre.T
    b_t = b_all.T
    for h in range(ML_HEADS):
        q = qkv_ref[0, :, h * ML_DIM:(h + 1) * ML_DIM]
        k = qkv_ref[0, :, HALF_W + h * ML_DIM:HALF_W + (h + 1) * ML_DIM]
        v = qkv_ref[0, :, 2 * HALF_W + h * ML_DIM:2 * HALF_W + (h + 1) * ML_DIM]
        ig_col = pre[:, h:h + 1]
        b_col = b_all[:, ML_HEADS + h:ML_HEADS + h + 1]
        a_row = pre_t[h:h + 1, :] - b_t[ML_HEADS + h:ML_HEADS + h + 1, :]
        m_st = m_ref[0, h:h + 1, 0:1]
        c_st = c_ref[0, h]
        n_st = n_ref[0, h:h + 1, :]
        dmat = jnp.where(causal, b_col + a_row, -jnp.inf)
        inter = b_col + m_st
        m_t = jnp.maximum(inter, jnp.max(dmat, axis=-1, keepdims=True))
        w_intra = jnp.exp(dmat - m_t)
        w_inter = jnp.exp(inter - m_t)
        s = _dot_nt(q, k) * w_intra
        num = jnp.dot(s.astype(bf16), v, preferred_element_type=f32) + w_inter * _dot_nt(q, c_st.astype(bf16))
        den = jnp.sum(s, axis=-1, keepdims=True) + w_inter * jnp.sum(q.astype(f32) * n_st, axis=-1, keepdims=True)
        hs = num * (1.0 / jnp.maximum(jnp.abs(den), jnp.exp(-m_t)))
        m_new = m_t[L - 1:L, :]
        b_last = b_col[L - 1:L, :]
        w_state = jnp.exp(b_last - b_col + ig_col - m_new)
        decay = jnp.exp(b_last + m_st - m_new)
        vw = (v.astype(f32) * w_state).astype(bf16)
        c_ref[0, h] = decay * c_st + lax.dot_general(vw, k, (((0,), (0,)), ((), ())), preferred_element_type=f32)
        n_ref[0, h:h + 1, :] = decay * n_st + jnp.sum(k.astype(f32) * w_state, axis=0, keepdims=True)
        m_ref[0, h:h + 1, :] = jnp.broadcast_to(m_new, (1, LANES))
        og = og_ref[0, :, h * ML_DIM:(h + 1) * ML_DIM]
        yb_ref[0, :, h * ML_DIM:(h + 1) * ML_DIM] = (jax.nn.sigmoid(og) * hs).astype(bf16)


def _mlstm_prompt(qkv, og, gates, gate_b, bsz, t_len, chunk=128):
    gb = jnp.pad(gate_b, (0, LANES - gate_b.shape[0])).reshape(1, LANES)
    return pl.pallas_call(
        _mlstm_kernel,
        grid=(bsz, t_len // chunk),
        in_specs=[pl.BlockSpec((1, chunk, 3 * HALF_W), lambda b, c: (b, c, 0)),
                  pl.BlockSpec((1, chunk, HALF_W), lambda b, c: (b, c, 0)),
                  pl.BlockSpec((1, chunk, LANES), lambda b, c: (b, c, 0)),
                  pl.BlockSpec((1, LANES), lambda b, c: (0, 0))],
        out_specs=[pl.BlockSpec((1, chunk, HALF_W), lambda b, c: (b, c, 0)),
                   pl.BlockSpec((1, ML_HEADS, ML_DIM, ML_DIM), lambda b, c: (b, 0, 0, 0)),
                   pl.BlockSpec((1, ML_HEADS, ML_DIM), lambda b, c: (b, 0, 0)),
                   pl.BlockSpec((1, ML_HEADS, LANES), lambda b, c: (b, 0, 0))],
        out_shape=[jax.ShapeDtypeStruct((bsz, t_len, HALF_W), jnp.bfloat16),
                   jax.ShapeDtypeStruct((bsz, ML_HEADS, ML_DIM, ML_DIM), jnp.float32),
                   jax.ShapeDtypeStruct((bsz, ML_HEADS, ML_DIM), jnp.float32),
                   jax.ShapeDtypeStruct((bsz, ML_HEADS, LANES), jnp.float32)],
        compiler_params=pltpu.CompilerParams(
            dimension_semantics=("arbitrary", "arbitrary"), vmem_limit_bytes=VMEM_LIMIT),
        name="mlstm_prompt",
    )(qkv.reshape(bsz, t_len, 3 * HALF_W), og.reshape(bsz, t_len, HALF_W),
      gates.reshape(bsz, t_len, LANES), gb)


def _outproj2_ln_kernel(ya_ref, yb_ref, w_ref, x_ref, g_ref, b_ref, o_ref):
    mix = jnp.dot(ya_ref[...], w_ref[0:HALF_W, :], preferred_element_type=jnp.float32)
    mix = mix + jnp.dot(yb_ref[...], w_ref[HALF_W:, :], preferred_element_type=jnp.float32)
    o_ref[...] = _layer_norm_rows(ALPHA * x_ref[...] + mix, g_ref[...], b_ref[...])


def _outproj2_ln(ya, yb, w_bf16, x, g, b, tm):
    m, d = x.shape
    return pl.pallas_call(
        _outproj2_ln_kernel,
        grid=(m // tm,),
        in_specs=[pl.BlockSpec((tm, HALF_W), lambda i: (i, 0)),
                  pl.BlockSpec((tm, HALF_W), lambda i: (i, 0)),
                  pl.BlockSpec((2 * HALF_W, d), lambda i: (0, 0)),
                  pl.BlockSpec((tm, d), lambda i: (i, 0)),
                  pl.BlockSpec((1, d), lambda i: (0, 0)),
                  pl.BlockSpec((1, d), lambda i: (0, 0))],
        out_specs=pl.BlockSpec((tm, d), lambda i: (i, 0)),
        out_shape=jax.ShapeDtypeStruct((m, d), jnp.float32),
        compiler_params=pltpu.CompilerParams(
            dimension_semantics=("arbitrary",), vmem_limit_bytes=VMEM_LIMIT),
        name="outproj2_ln",
    )(ya, yb, w_bf16, x, g.reshape(1, d), b.reshape(1, d))


SAMPLE_ROWS = 8


def _even_sample_kernel(h_ref, bg_ref, cg_ref, q_ref, k_ref, v_ref, og_ref, gate_ref, gb_ref, cw_ref, cb_ref,
                        conv_ref, c_ref, n_ref, m_ref, ya_ref, yb_ref, conv_out_ref, c_out_ref, n_out_ref, m_out_ref):
    f32, bf16 = jnp.float32, jnp.bfloat16
    rows = h_ref.shape[0]
    row = lax.broadcasted_iota(jnp.int32, (rows, 1), 0)
    tn = (((0,), (0,)), ((), ()))
    u = cg_ref[...] * h_ref[...]
    prev0, prev1 = conv_ref[:, 0, :], conv_ref[:, 1, :]
    conv = cb_ref[...] + cw_ref[0:1, :] * prev0 + cw_ref[1:2, :] * prev1 + cw_ref[2:3, :] * u
    ya_ref[...] = (bg_ref[...] * conv).astype(bf16)
    conv_out_ref[:, 0, :] = prev1
    conv_out_ref[:, 1, :] = u
    pre = gate_ref[...] + gb_ref[...]
    lf_all = jax.nn.log_sigmoid(pre)
    for h in range(ML_HEADS):
        cols = slice(h * ML_DIM, (h + 1) * ML_DIM)
        q = q_ref[:, cols]
        k = k_ref[:, cols] * (ML_DIM ** -0.5)
        v = v_ref[:, cols]
        ig = pre[:, h:h + 1]
        lf = lf_all[:, ML_HEADS + h:ML_HEADS + h + 1]
        m_st = m_ref[:, h:h + 1]
        n_st = n_ref[:, h, :]
        inter = lf + m_st
        m_t = jnp.maximum(inter, ig)
        w_intra = jnp.exp(ig - m_t)
        w_inter = jnp.exp(inter - m_t)
        qb, kb = q.astype(bf16), k.astype(bf16)
        s = jnp.sum(qb.astype(f32) * kb.astype(f32), axis=-1, keepdims=True) * w_intra
        cq = jnp.zeros((rows, ML_DIM), f32)
        for b in range(rows):
            c_b = c_ref[b, h]
            cq = jnp.where(row == b, _dot_nt(qb, c_b.astype(bf16)), cq)
            vw = jnp.where(row == b, v * w_intra, 0.0)
            outer = lax.dot_general(vw, k, tn, preferred_element_type=f32, precision=lax.Precision.HIGHEST)
            c_out_ref[b, h] = w_inter[b:b + 1, :] * c_b + outer
        num = s * v + w_inter * cq
        den = s + w_inter * jnp.sum(n_st * q, axis=-1, keepdims=True)
        hs = num * (1.0 / jnp.maximum(jnp.abs(den), jnp.exp(-m_t)))
        n_out_ref[:, h, :] = w_inter * n_st + w_intra * k
        m_out_ref[:, h, :] = jnp.broadcast_to(m_t, (rows, LANES))
        yb_ref[:, cols] = (jax.nn.sigmoid(og_ref[:, cols]) * hs).astype(bf16)


def _even_sample(z, gate_b, conv_w, conv_b, state_conv, state_c, state_n, state_m):
    s_n = z.shape[0]
    f32 = jnp.float32
    r = SAMPLE_ROWS
    gb = jnp.pad(gate_b, (0, LANES - gate_b.shape[0])).reshape(1, LANES)
    col = lambda j: pl.BlockSpec((r, HALF_W), lambda i, j=j: (i, j))
    in_specs = [col(j) for j in range(7)] + [
        pl.BlockSpec((r, LANES), lambda i: (i, E_GATE // LANES)),
        pl.BlockSpec((1, LANES), lambda i: (0, 0)),
        pl.BlockSpec((CONV_W, HALF_W), lambda i: (0, 0)),
        pl.BlockSpec((1, HALF_W), lambda i: (0, 0)),
        pl.BlockSpec((r, CONV_W - 1, HALF_W), lambda i: (i, 0, 0)),
        pl.BlockSpec((r, ML_HEADS, ML_DIM, ML_DIM), lambda i: (i, 0, 0, 0)),
        pl.BlockSpec((r, ML_HEADS, ML_DIM), lambda i: (i, 0, 0)),
        pl.BlockSpec((r, ML_HEADS), lambda i: (i, 0))]
    out_specs = [pl.BlockSpec((r, HALF_W), lambda i: (i, 0)),
                 pl.BlockSpec((r, HALF_W), lambda i: (i, 0)),
                 pl.BlockSpec((r, CONV_W - 1, HALF_W), lambda i: (i, 0, 0)),
                 pl.BlockSpec((r, ML_HEADS, ML_DIM, ML_DIM), lambda i: (i, 0, 0, 0)),
                 pl.BlockSpec((r, ML_HEADS, ML_DIM), lambda i: (i, 0, 0)),
                 pl.BlockSpec((r, ML_HEADS, LANES), lambda i: (i, 0, 0))]
    out_shape = [jax.ShapeDtypeStruct((s_n, HALF_W), jnp.bfloat16),
                 jax.ShapeDtypeStruct((s_n, HALF_W), jnp.bfloat16),
                 jax.ShapeDtypeStruct((s_n, CONV_W - 1, HALF_W), f32),
                 jax.ShapeDtypeStruct((s_n, ML_HEADS, ML_DIM, ML_DIM), f32),
                 jax.ShapeDtypeStruct((s_n, ML_HEADS, ML_DIM), f32),
                 jax.ShapeDtypeStruct((s_n, ML_HEADS, LANES), f32)]
    return pl.pallas_call(
        _even_sample_kernel,
        grid=(s_n // r,),
        in_specs=in_specs,
        out_specs=out_specs,
        out_shape=out_shape,
        compiler_params=pltpu.CompilerParams(
            dimension_semantics=("arbitrary",), vmem_limit_bytes=VMEM_LIMIT),
        name="even_sample",
    )(z, z, z, z, z, z, z, z, gb, conv_w, conv_b.reshape(1, HALF_W), state_conv, state_c, state_n, state_m)


O_Q = 0
O_QD = NSA_HEADS * LANES
O_KV = 2048
O_WIN = 2816
O_QI = 3072
O_MISC = 3200
O_PACKED = 3328
ZQ_QI = 2048
MISC_WI = IDX_DIM
MISC_GATE = IDX_DIM + IDX_HEADS
NEG_BIG = -(2.0 ** 100)
INT_MIN = -(2 ** 31)
ALIBI = tuple(float(2.0 ** (-8.0 * (h + 1) / NSA_HEADS)) for h in range(NSA_HEADS))


def _pack_w_in_o(w):
    qn, kvc, kvs, kvw, gates, qd, kvd, qi, ki, wi = _split(w, O_SIZES)
    scale = HEAD_DIM ** -0.5
    rows = w.shape[0]

    def spread(q):
        q = (q * scale).reshape(rows, NSA_HEADS, HEAD_DIM)
        z = jnp.zeros_like(q)
        low = (jnp.arange(NSA_HEADS) < NSA_HEADS // KV_GROUPS)[None, :, None]
        return jnp.concatenate([jnp.where(low, q, z), jnp.where(low, z, q)], axis=-1).reshape(rows, NSA_HEADS * LANES)

    pad = jnp.zeros((rows, O_PACKED - O_MISC - IDX_DIM - IDX_HEADS - 3 * NSA_HEADS), w.dtype)
    return jnp.concatenate([spread(qn), spread(qd), kvc, kvs, kvd, kvw, qi, ki, wi, gates, pad], axis=1)


def _odd_proj_kernel(x_ref, w_ref, wkvt_ref, zq_ref, kvt_ref, kvb_ref, win_ref, misc_ref):
    xb = x_ref[...].astype(jnp.bfloat16)
    z = jnp.dot(xb, w_ref[...], preferred_element_type=jnp.float32)
    zq_ref[:, 0:O_KV] = z[:, 0:O_KV].astype(jnp.bfloat16)
    zq_ref[:, ZQ_QI:ZQ_QI + LANES] = z[:, O_QI:O_MISC].astype(jnp.bfloat16)
    kvt_ref[0] = _dot_nt(wkvt_ref[...], xb)
    kvb_ref[...] = z[:, O_KV:O_QI].astype(jnp.bfloat16)
    win_ref[...] = z[:, O_WIN:O_QI]
    misc_ref[...] = z[:, O_MISC:O_PACKED]


def _odd_proj(x, w_packed_bf16, tm, seq_len):
    m, k = x.shape
    n_kv = O_WIN - O_KV
    widths = (O_KV + LANES, O_QI - O_KV, O_QI - O_WIN, LANES)
    dtypes = (jnp.bfloat16, jnp.bfloat16, jnp.float32, jnp.float32)
    row_spec = lambda n: pl.BlockSpec((tm, n), lambda i: (i, 0))
    tiles = seq_len // tm
    out_specs = [row_spec(widths[0]), pl.BlockSpec((1, n_kv, tm), lambda i: (i // tiles, 0, i % tiles))]
    out_specs += [row_spec(n) for n in widths[1:]]
    out_shape = [jax.ShapeDtypeStruct((m, widths[0]), dtypes[0]),
                 jax.ShapeDtypeStruct((m // seq_len, n_kv, seq_len), jnp.float32)]
    out_shape += [jax.ShapeDtypeStruct((m, n), dt) for n, dt in zip(widths[1:], dtypes[1:])]
    return pl.pallas_call(
        _odd_proj_kernel,
        grid=(m // tm,),
        in_specs=[pl.BlockSpec((tm, k), lambda i: (i, 0)),
                  pl.BlockSpec((k, O_PACKED), lambda i: (0, 0)),
                  pl.BlockSpec((n_kv, k), lambda i: (0, 0))],
        out_specs=out_specs,
        out_shape=out_shape,
        compiler_params=pltpu.CompilerParams(
            dimension_semantics=("arbitrary",), vmem_limit_bytes=VMEM_LIMIT),
        name="odd_proj",
    )(x, w_packed_bf16, w_packed_bf16[:, O_KV:O_WIN].T)


def _dot_nt(a, b):
    return lax.dot_general(a, b, (((1,), (1,)), ((), ())), preferred_element_type=jnp.float32)


def _flash_heads(q_heads, slopes, group, kv_ref, k_blk, v_blk, tk, lo, hi, bias_fn, skip_fn=None, start_fn=None):
    f32, bf16 = jnp.float32, jnp.bfloat16
    if start_fn is None:
        start_fn = lambda kt: kt * tk
    tq = q_heads[0].shape[0]
    n_heads = len(q_heads)
    lane = lax.broadcasted_iota(jnp.int32, (1, LANES), 1)
    own_half = (lane // HEAD_DIM) == group
    q_all = jnp.concatenate(q_heads, axis=0)
    slope_col = jnp.concatenate([jnp.full((tq, 1), sl, f32) for sl in slopes], axis=0)

    def tile(kt, carry):
        m, acc = carry
        start = pl.multiple_of(start_fn(kt), LANES)
        rows = pl.ds(start, tk)
        k = kv_ref[0, rows, k_blk * LANES:(k_blk + 1) * LANES]
        v = kv_ref[0, rows, v_blk * LANES:(v_blk + 1) * LANES]
        v = jnp.where(own_half, v, jnp.ones_like(v))
        sp = (start + lax.broadcasted_iota(jnp.int32, (1, tk), 1)).astype(f32)
        b = bias_fn(kt)
        s = _dot_nt(q_all, k) + slope_col * sp + jnp.concatenate([b] * n_heads, axis=0)
        m_new = jnp.maximum(m, jnp.max(s, axis=-1, keepdims=True))
        p = jnp.exp(s - m_new)
        acc = jnp.exp(m - m_new) * acc + jnp.dot(p.astype(bf16), v, preferred_element_type=f32)
        return m_new, acc

    def body(kt, carry):
        if skip_fn is None:
            return tile(kt, carry)
        return lax.cond(skip_fn(kt), lambda c: c, functools.partial(tile, kt), carry)

    m0 = jnp.full((n_heads * tq, 1), -jnp.inf, f32)
    a0 = jnp.zeros((n_heads * tq, LANES), f32)
    _, acc = lax.fori_loop(lo, hi, body, (m0, a0))
    return [acc[r * tq:(r + 1) * tq] for r in range(n_heads)]


def _odd_attn_kernel(zq_ref, miscq_ref, kvb_ref, misck_ref, wc_ref, o_ref,
                     kc_ref, vc_ref, ki_ref, key_ref, dbias_ref, flag_ref, *, t_len, tq, tk, topk):
    f32, bf16, i32 = jnp.float32, jnp.bfloat16, jnp.int32
    nb = t_len // BLK
    qi_blk = pl.program_id(1)
    q0 = qi_blk * tq

    @pl.when(qi_blk == 0)
    def _():
        ck = kvb_ref[0, :, 0:LANES].astype(f32).reshape(nb, BLK, LANES)
        kc_ref[...] = jnp.sum(ck * wc_ref[0][None], axis=1).astype(bf16)
        cv = kvb_ref[0, :, LANES:2 * LANES].astype(f32).reshape(nb, BLK, LANES)
        vc = jnp.sum(cv * wc_ref[1][None], axis=1)
        for g in range(KV_GROUPS):
            half = vc[:, g * HEAD_DIM:(g + 1) * HEAD_DIM]
            vc_ref[g] = jnp.concatenate([half, half], axis=1).astype(bf16)
        ki_ref[...] = misck_ref[0, :, 0:IDX_DIM].astype(bf16)

    misc = miscq_ref[0]
    gates = jax.nn.sigmoid(misc[:, MISC_GATE:MISC_GATE + 3 * NSA_HEADS])
    t_col = q0 + lax.broadcasted_iota(i32, (tq, 1), 0)
    lane = lax.broadcasted_iota(i32, (1, LANES), 1)
    low_half = lane < HEAD_DIM
    heads_per_group = NSA_HEADS // KV_GROUPS

    def q_head(base, h):
        return zq_ref[0, :, base + h * LANES: base + (h + 1) * LANES]

    def pair(even, odd):
        return jnp.where(low_half, even, odd)

    def normalised_pair(acc_even, acc_odd, g):
        outs = []
        for parity, acc in enumerate((acc_even, acc_odd)):
            swapped = pltpu.roll(acc, HEAD_DIM, 1)
            outs.append(acc * (1.0 / swapped) if parity == g else swapped * (1.0 / acc))
        return pair(outs[0], outs[1])

    def gate_pair(j, c):
        a = gates[:, (2 * j) * 3 + c:(2 * j) * 3 + c + 1]
        b = gates[:, (2 * j + 1) * 3 + c:(2 * j + 1) * 3 + c + 1]
        return jnp.where(low_half, a, b)

    jb = lax.broadcasted_iota(i32, (1, nb), 1)
    dist_c = t_col - (jb * BLK + (BLK - 1))
    mask_c = dist_c >= 0
    dist_cf = dist_c.astype(f32)
    cur = t_col // BLK
    forced = (jb == 0) | (jb == cur) | (jb == cur - 1)
    admissible = jb <= cur
    jb_full = lax.broadcasted_iota(i32, (tq, nb), 1)
    o_cmp, sel_bias, sel_any = [], [], []
    for g in range(KV_GROUPS):
        imp = jnp.zeros((tq, nb), f32)
        for r in range(heads_per_group):
            h = g * heads_per_group + r
            s = _dot_nt(q_head(O_Q, h), kc_ref[...]) - ALIBI[h] * dist_cf
            s = jnp.where(mask_c, s, NEG_BIG)
            m = jnp.max(s, axis=-1, keepdims=True)
            e = jnp.where(mask_c, jnp.exp(s - m), 0.0)
            p = e * (1.0 / jnp.maximum(jnp.sum(e, axis=-1, keepdims=True), 1e-30))
            imp = imp + p
            o_cmp.append(jnp.dot(p.astype(bf16), vc_ref[g], preferred_element_type=f32))
        imp = jnp.where(forced, FORCE_SCORE, imp)
        imp = jnp.where(admissible, imp, -jnp.inf)
        sel = jnp.zeros((tq, nb), f32)
        for _ in range(min(NSA_TOPN, nb)):
            m = jnp.max(imp, axis=-1, keepdims=True)
            first = jnp.min(jnp.where(imp == m, jb_full, nb), axis=-1, keepdims=True)
            hit = jb_full == first
            sel = jnp.where(hit & (m > -jnp.inf), 1.0, sel)
            imp = jnp.where(hit, -jnp.inf, imp)
        sel_bias.append(jnp.where(sel > 0.0, 0.0, NEG_BIG).astype(bf16))
        sel_any.append(jnp.max(sel, axis=0, keepdims=True))

    n_sel_tiles = (q0 + tq + tk - 1) // tk
    blocks_per_tile = tk // BLK

    for g in range(KV_GROUPS):
        for j in range(nb // blocks_per_tile):
            hit = jnp.max(sel_any[g][:, j * blocks_per_tile:(j + 1) * blocks_per_tile])
            flag_ref[g, j] = (hit > 0.0).astype(i32)

    def sel_skip_fn(g):
        return lambda kt: flag_ref[g, kt] == 0

    def causal_bias(start, width):
        sp = start + lax.broadcasted_iota(i32, (1, width), 1)
        return jnp.where(sp <= t_col, 0.0, NEG_BIG)

    def sel_bias_fn(g):
        def fn(kt):
            row = lax.broadcasted_iota(i32, (nb, tk), 0)
            col = lax.broadcasted_iota(i32, (nb, tk), 1)
            expand = jnp.where(row == kt * blocks_per_tile + col // BLK, 1.0, 0.0).astype(bf16)
            return jnp.dot(sel_bias[g], expand, preferred_element_type=f32) + causal_bias(kt * tk, tk)
        return fn

    tkw = min(WINDOW + tq, t_len)
    win_start = jnp.clip(q0 - WINDOW, 0, t_len - tkw)

    def win_bias_fn(kt):
        dist = t_col - (win_start + lax.broadcasted_iota(i32, (1, tkw), 1))
        return jnp.where((dist >= 0) & (dist < WINDOW), 0.0, NEG_BIG)

    qi_heads = [zq_ref[0, :, ZQ_QI + h * IDX_DIM: ZQ_QI + (h + 1) * IDX_DIM] for h in range(IDX_HEADS)]
    wi_cols = [misc[:, MISC_WI + h: MISC_WI + h + 1] for h in range(IDX_HEADS)]

    def index_tile(kt, carry):
        start = pl.multiple_of(kt * tk, tk)
        kik = ki_ref[pl.ds(start, tk), :]
        tot = jnp.zeros((tq, tk), f32)
        for h in range(IDX_HEADS):
            sc = _dot_nt(qi_heads[h], kik) * (IDX_DIM ** -0.5)
            tot = tot + jnp.maximum(sc, 0.0) * wi_cols[h]
        tot = tot * (IDX_HEADS ** -0.5)
        tot = jnp.where(tot == 0.0, 0.0, tot)
        bits = lax.bitcast_convert_type(tot, i32)
        key = jnp.where(bits < 0, bits ^ jnp.int32(0x7FFFFFFF), bits)
        sp = start + lax.broadcasted_iota(i32, (1, tk), 1)
        key_ref[kt] = jnp.where(sp <= t_col, key, INT_MIN)
        return carry

    lax.fori_loop(0, n_sel_tiles, index_tile, 0)

    def count_ge(trial):
        def body(kt, acc):
            c = (key_ref[kt] >= trial).astype(i32)
            for j in range(tk // LANES):
                acc = acc + c[:, j * LANES:(j + 1) * LANES]
            return acc
        acc = lax.fori_loop(0, n_sel_tiles, body, jnp.zeros((tq, LANES), i32))
        return jnp.sum(acc, axis=-1, keepdims=True)

    c_pos = count_ge(jnp.zeros((tq, 1), i32))
    v0 = jnp.where(c_pos >= topk, 0, INT_MIN).astype(i32)
    c0 = jnp.where(c_pos >= topk, c_pos, n_sel_tiles * tk)

    def bit_step(i, carry):
        v, cge = carry
        trial = v | lax.shift_left(jnp.int32(1), 30 - i)
        c = count_ge(trial)
        ok = c >= topk
        return jnp.where(ok, trial, v), jnp.where(ok, c, cge)

    v_thr, c_ge = lax.fori_loop(0, 31, bit_step, (v0, c0))
    v_eff = jnp.maximum(v_thr, INT_MIN + 1)

    def write_plain(kt, carry):
        dbias_ref[kt] = jnp.where(key_ref[kt] >= v_eff, 0.0, NEG_BIG)
        return carry

    def plain_path():
        lax.fori_loop(0, n_sel_tiles, write_plain, 0)

    def tie_path():
        need = topk - count_ge(v_thr + 1)

        def count_tie_below(limit):
            def body(kt, acc):
                sp = kt * tk + lax.broadcasted_iota(i32, (1, tk), 1)
                c = ((key_ref[kt] == v_thr) & (sp < limit)).astype(i32)
                for j in range(tk // LANES):
                    acc = acc + c[:, j * LANES:(j + 1) * LANES]
                return acc
            acc = lax.fori_loop(0, n_sel_tiles, body, jnp.zeros((tq, LANES), i32))
            return jnp.sum(acc, axis=-1, keepdims=True)

        n_bits = max(1, (t_len - 1).bit_length())

        def idx_step(i, w):
            trial = w | lax.shift_left(jnp.int32(1), n_bits - 1 - i)
            return jnp.where(count_tie_below(trial) < need, trial, w)

        w_last = lax.fori_loop(0, n_bits, idx_step, jnp.zeros((tq, 1), i32))

        def write_tie(kt, carry):
            sp = kt * tk + lax.broadcasted_iota(i32, (1, tk), 1)
            key = key_ref[kt]
            take = (key > v_thr) | ((key == v_thr) & (sp <= w_last))
            dbias_ref[kt] = jnp.where(take & (key >= v_eff), 0.0, NEG_BIG)
            return carry

        lax.fori_loop(0, n_sel_tiles, write_tie, 0)

    lax.cond(jnp.max(c_ge) > topk, tie_path, plain_path)

    def dsa_bias_fn(kt):
        return dbias_ref[kt]

    for g in range(KV_GROUPS):
        hs = [g * heads_per_group + r for r in range(heads_per_group)]
        slopes = [ALIBI[h] for h in hs]
        qn = [q_head(O_Q, h) for h in hs]
        qd = [q_head(O_QD, h) for h in hs]
        a_s = _flash_heads(qn, slopes, g, kvb_ref, 2, 3, tk, 0, n_sel_tiles, sel_bias_fn(g), sel_skip_fn(g))
        a_w = _flash_heads(qn, slopes, g, kvb_ref, 6, 7, tkw, 0, 1, win_bias_fn, start_fn=lambda kt: win_start)
        a_d = _flash_heads(qd, slopes, g, kvb_ref, 4, 5, tk, 0, n_sel_tiles, dsa_bias_fn)
        for jj in range(heads_per_group // 2):
            j = hs[2 * jj] // 2
            o_n = (gate_pair(j, 0) * pair(o_cmp[2 * j], o_cmp[2 * j + 1])
                   + gate_pair(j, 1) * normalised_pair(a_s[2 * jj], a_s[2 * jj + 1], g)
                   + gate_pair(j, 2) * normalised_pair(a_w[2 * jj], a_w[2 * jj + 1], g))
            o_ref[0, :, j * LANES:(j + 1) * LANES] = o_n.astype(o_ref.dtype)
            o_d = normalised_pair(a_d[2 * jj], a_d[2 * jj + 1], g)
            o_ref[0, :, HALF_W + j * LANES:HALF_W + (j + 1) * LANES] = o_d.astype(o_ref.dtype)


def _odd_attn_prompt(zq, misc, kvb, wc2, bsz, t_len, tq=256, tk=512):
    tk = min(tk, t_len)
    topk = min(DSA_TOPK_MAX, t_len // 4)
    nb = t_len // BLK
    n_kt = t_len // tk
    zq3 = zq.reshape(bsz, t_len, zq.shape[1])
    misc3 = misc.reshape(bsz, t_len, LANES)
    kvb3 = kvb.reshape(bsz, t_len, kvb.shape[1])
    kern = functools.partial(_odd_attn_kernel, t_len=t_len, tq=tq, tk=tk, topk=topk)
    out = pl.pallas_call(
        kern,
        grid=(bsz, t_len // tq),
        in_specs=[pl.BlockSpec((1, tq, zq.shape[1]), lambda b, i: (b, i, 0)),
                  pl.BlockSpec((1, tq, LANES), lambda b, i: (b, i, 0)),
                  pl.BlockSpec((1, t_len, kvb.shape[1]), lambda b, i: (b, 0, 0)),
                  pl.BlockSpec((1, t_len, LANES), lambda b, i: (b, 0, 0)),
                  pl.BlockSpec((2, BLK, LANES), lambda b, i: (0, 0, 0))],
        out_specs=pl.BlockSpec((1, tq, 2 * HALF_W), lambda b, i: (b, i, 0)),
        out_shape=jax.ShapeDtypeStruct((bsz, t_len, 2 * HALF_W), jnp.bfloat16),
        scratch_shapes=[pltpu.VMEM((nb, LANES), jnp.bfloat16),
                        pltpu.VMEM((KV_GROUPS, nb, LANES), jnp.bfloat16),
                        pltpu.VMEM((t_len, IDX_DIM), jnp.bfloat16),
                        pltpu.VMEM((n_kt, tq, tk), jnp.int32),
                        pltpu.VMEM((n_kt, tq, tk), jnp.float32),
                        pltpu.SMEM((KV_GROUPS, n_kt), jnp.int32)],
        compiler_params=pltpu.CompilerParams(
            dimension_semantics=("arbitrary", "arbitrary"), vmem_limit_bytes=VMEM_LIMIT),
        name="odd_attn_prompt",
    )(zq3, misc3, kvb3, misc3, wc2)
    return out.reshape(bsz * t_len, 2 * HALF_W)


PAGE = 128
ISC_ROWS = 24


def _odd_sample_kernel(pt_ref, qn_ref, qd_ref, qi_ref, wi_ref, gate_ref, newb_ref, newki_ref, neww_ref,
                       win_ref, wc_ref, wct_ref, *rest, n_pages, topk, nbp):
    del pt_ref
    f32, bf16, i32 = jnp.float32, jnp.bfloat16, jnp.int32
    hi_prec = lax.Precision.HIGHEST
    kv_refs, idx_refs = rest[:n_pages], rest[n_pages:2 * n_pages]
    o_ref, wout_ref, isc_ref = rest[2 * n_pages:]
    past = n_pages * PAGE
    nb = past // BLK + 1
    row8 = lax.broadcasted_iota(i32, (8, 1), 0)
    lane = lax.broadcasted_iota(i32, (1, LANES), 1)
    slope8 = jnp.zeros((8, 1), f32)
    for h in range(NSA_HEADS):
        slope8 = jnp.where(row8 == h, ALIBI[h], slope8)
    low_group = row8 < NSA_HEADS // KV_GROUPS

    def pick_half(x):
        return jnp.where(low_group, x[:, 0:HEAD_DIM], x[:, HEAD_DIM:2 * HEAD_DIM])

    qn8, qd8 = qn_ref[0], qd_ref[0]
    qi8, wi8 = qi_ref[0], wi_ref[0]
    newb = newb_ref[0]
    gates = jax.nn.sigmoid(gate_ref[0])

    def new_score(q8, k_row):
        return jnp.sum(q8.astype(f32) * k_row.astype(f32), axis=-1, keepdims=True) + slope8 * float(past)

    def feat(p, slot):
        return kv_refs[p][0, slot * LANES:(slot + 1) * LANES, :]

    s_sel, s_dsa = [], []
    key_row = lax.broadcasted_iota(i32, (PAGE, nbp), 0)
    blk_col = lax.broadcasted_iota(i32, (PAGE, nbp), 1)
    kct = jnp.zeros((LANES, nbp), f32)
    vct = jnp.zeros((LANES, nbp), f32)
    for p in range(n_pages):
        pool = jnp.where(blk_col == 2 * p + key_row // BLK, 1.0, 0.0).astype(bf16)
        kct = kct + jnp.dot((feat(p, 0) * wct_ref[0]).astype(bf16), pool, preferred_element_type=f32)
        vct = vct + jnp.dot((feat(p, 1) * wct_ref[1]).astype(bf16), pool, preferred_element_type=f32)
        pos = (p * PAGE + lane).astype(f32)
        s_sel.append(jnp.dot(qn8, feat(p, 2).astype(bf16), preferred_element_type=f32) + slope8 * pos)
        s_dsa.append(jnp.dot(qd8, feat(p, 4).astype(bf16), preferred_element_type=f32) + slope8 * pos)
        ii = jnp.dot(qi8, idx_refs[p][0].astype(bf16), preferred_element_type=f32) * (IDX_DIM ** -0.5)
        isc_ref[p:p + 1, :] = jnp.sum(jnp.maximum(ii, 0.0) * wi8, axis=0, keepdims=True) * (IDX_HEADS ** -0.5)
    first_row = row8 == 0
    place = jnp.where(first_row & (lax.broadcasted_iota(i32, (8, nbp), 1) == nb - 1), 1.0, 0.0)
    tn = (((0,), (0,)), ((), ()))
    new_k = jnp.where(first_row, newb[:, 0:LANES].astype(f32) * wc_ref[0, 0:1, :], 0.0)
    new_v = jnp.where(first_row, newb[:, LANES:2 * LANES].astype(f32) * wc_ref[1, 0:1, :], 0.0)
    kct = kct + lax.dot_general(new_k.astype(bf16), place.astype(bf16), tn, preferred_element_type=f32)
    vct = vct + lax.dot_general(new_v.astype(bf16), place.astype(bf16), tn, preferred_element_type=f32)
    ii_new = jnp.sum(qi8.astype(f32) * newki_ref[0].astype(bf16).astype(f32), axis=-1, keepdims=True) * (IDX_DIM ** -0.5)
    isc_new = jnp.sum(jnp.maximum(ii_new, 0.0) * wi8, axis=0, keepdims=True) * (IDX_HEADS ** -0.5)
    isc_ref[n_pages:n_pages + 1, :] = jnp.where(lane == 0, isc_new, -jnp.inf)
    isc_ref[n_pages + 1:ISC_ROWS, :] = jnp.full((ISC_ROWS - n_pages - 1, LANES), -jnp.inf, f32)

    jb = lax.broadcasted_iota(i32, (1, nbp), 1)
    dist_c = past - (jb * BLK + (BLK - 1))
    mask_c = (dist_c >= 0) & (jb < nb)
    s_c = jnp.dot(qn8, kct.astype(bf16), preferred_element_type=f32) - slope8 * dist_c.astype(f32)
    s_c = jnp.where(mask_c, s_c, NEG_BIG)
    m_c = jnp.max(s_c, axis=-1, keepdims=True)
    e_c = jnp.where(mask_c, jnp.exp(s_c - m_c), 0.0)
    p_c = e_c * (1.0 / jnp.maximum(jnp.sum(e_c, axis=-1, keepdims=True), 1e-30))
    o_c = pick_half(_dot_nt(p_c.astype(bf16), vct.astype(bf16)))
    imp_lo = jnp.sum(jnp.where(low_group, p_c, 0.0), axis=0, keepdims=True)
    imp_hi = jnp.sum(jnp.where(low_group, 0.0, p_c), axis=0, keepdims=True)
    imp = jnp.where(low_group, imp_lo, imp_hi)
    cur = past // BLK
    imp = jnp.where((jb == 0) | (jb == cur) | (jb == cur - 1), FORCE_SCORE, imp)
    imp = jnp.where(jb <= cur, imp, -jnp.inf)
    rank = jnp.zeros((8, nbp), i32)
    for j in range(nb):
        other = imp[:, j:j + 1]
        rank = rank + ((other > imp) | ((other == imp) & (j < jb))).astype(i32)
    sel = jnp.where((rank < min(NSA_TOPN, nb)) & (imp > -jnp.inf), 1.0, 0.0)

    tot = isc_ref[...]
    tot = jnp.where(tot == 0.0, 0.0, tot)
    bits = lax.bitcast_convert_type(tot, i32)
    key = jnp.where(bits < 0, bits ^ jnp.int32(0x7FFFFFFF), bits)
    krow = lax.broadcasted_iota(i32, (ISC_ROWS, LANES), 0)
    kidx = krow * LANES + lax.broadcasted_iota(i32, (ISC_ROWS, LANES), 1)
    key = jnp.where(kidx <= past, key, INT_MIN)

    def total(x):
        return jnp.sum(jnp.sum(x, axis=1, keepdims=True), axis=0, keepdims=True)

    def count_ge(trial):
        return total((key >= trial).astype(i32))

    c_pos = count_ge(jnp.zeros((1, 1), i32))
    v_thr = jnp.where(c_pos >= topk, 0, INT_MIN).astype(i32)
    c_ge = jnp.where(c_pos >= topk, c_pos, ISC_ROWS * LANES)
    for shift, width in ((27, 4), (23, 4), (19, 4), (15, 4), (11, 4), (7, 4), (3, 4), (0, 3)):
        digit = jnp.zeros((1, 1), i32)
        for j in range(1, 1 << width):
            c = count_ge(v_thr | jnp.int32(j << shift))
            ok = c >= topk
            digit = digit + ok.astype(i32)
            c_ge = jnp.where(ok, jnp.minimum(c_ge, c), c_ge)
        v_thr = v_thr | lax.shift_left(digit, jnp.int32(shift))
    v_eff = jnp.maximum(v_thr, INT_MIN + 1)

    def tie_mask():
        need = topk - count_ge(v_thr + 1)
        tie = key == v_thr
        w_last = jnp.zeros((1, 1), i32)
        for bit in range((ISC_ROWS * LANES - 1).bit_length() - 1, -1, -1):
            trial = w_last | jnp.int32(1 << bit)
            below = total((tie & (kidx < trial)).astype(i32))
            w_last = jnp.where(below < need, trial, w_last)
        take = (key > v_thr) | (tie & (kidx <= w_last))
        return jnp.where(take & (key >= v_eff), 1.0, 0.0)

    def plain_mask():
        return jnp.where(key >= v_eff, 1.0, 0.0)

    dsel = lax.cond(c_ge[0, 0] > topk, tie_mask, plain_mask)

    def attend(tiles, masks, s_new, new_ok, v_tile, v_new):
        tiles = [jnp.where(mk, t, NEG_BIG) for t, mk in zip(tiles, masks)]
        s_new = jnp.where(new_ok, s_new, NEG_BIG)
        m = s_new
        for t in tiles:
            m = jnp.maximum(m, jnp.max(t, axis=-1, keepdims=True))
        l = jnp.exp(s_new - m)
        acc = l * v_new.astype(f32)
        for j, t in enumerate(tiles):
            e = jnp.exp(t - m)
            l = l + jnp.sum(e, axis=-1, keepdims=True)
            acc = acc + _dot_nt(e.astype(bf16), v_tile(j))
        return pick_half(acc * (1.0 / l))

    sel_masks = [jnp.where(lane < BLK, sel[:, 2 * p:2 * p + 1], sel[:, 2 * p + 1:2 * p + 2]) > 0.0
                 for p in range(n_pages)]
    o_s = attend(s_sel, sel_masks, new_score(qn8, newb[:, 2 * LANES:3 * LANES]), sel[:, nb - 1:nb] > 0.0,
                 lambda p: feat(p, 3).astype(bf16), newb[:, 3 * LANES:4 * LANES])
    dsa_masks = [dsel[p:p + 1, :] > 0.0 for p in range(n_pages)]
    o_d = attend(s_dsa, dsa_masks, new_score(qd8, newb[:, 4 * LANES:5 * LANES]), dsel[n_pages:n_pages + 1, 0:1] > 0.0,
                 lambda p: feat(p, 5).astype(bf16), newb[:, 5 * LANES:6 * LANES])
    n_win = win_ref.shape[2]
    wlane = lax.broadcasted_iota(i32, (1, n_win), 1)
    wpos = past - n_win + wlane
    s_w = jnp.dot(qn8, win_ref[0, 0:LANES, :].astype(bf16), preferred_element_type=f32) + slope8 * wpos.astype(f32)
    w_ok = (past - wpos < WINDOW) & (wpos >= 0)
    o_w = attend([s_w], [w_ok], new_score(qn8, newb[:, 6 * LANES:7 * LANES]), True,
                 lambda p: win_ref[0, LANES:2 * LANES, :].astype(bf16), newb[:, 7 * LANES:8 * LANES])
    o_ref[0, 0:NSA_HEADS, :] = gates[:, 0:1] * o_c + gates[:, 1:2] * o_s + gates[:, 2:3] * o_w
    o_ref[0, NSA_HEADS:2 * NSA_HEADS, :] = o_d
    last = jnp.where(first_row & (lax.broadcasted_iota(i32, (8, n_win), 1) == n_win - 1), 1.0, 0.0)
    new_w = jnp.where(first_row, neww_ref[0], 0.0)
    placed = lax.dot_general(new_w, last, tn, preferred_element_type=f32, precision=hi_prec)
    wout_ref[0] = jnp.where(wlane == n_win - 1, placed, pltpu.roll(win_ref[0], n_win - 1, 1))


def _odd_attn_sample(zq, misc, kvb, win_new, cache_kv, cache_idx, cache_win, page_table, w_cmp):
    f32, bf16 = jnp.float32, jnp.bfloat16
    s_n, n_pages = page_table.shape
    n_pool = cache_kv.shape[0]
    n_win = cache_win.shape[1]
    past = n_pages * PAGE
    topk = min(DSA_TOPK_MAX, (past + 1) // 4)
    nbp = _round_up(past // BLK + 1, 8)
    wc2 = jnp.concatenate([w_cmp, w_cmp], axis=-1)
    wct = jnp.tile(jnp.swapaxes(w_cmp, 1, 2), (1, KV_GROUPS, PAGE // BLK))
    qn = zq[:, O_Q:O_QD].reshape(s_n, NSA_HEADS, LANES)
    qd = zq[:, O_QD:ZQ_QI].reshape(s_n, DSA_HEADS, LANES)
    qi = jnp.pad(zq[:, ZQ_QI:ZQ_QI + LANES].reshape(s_n, IDX_HEADS, IDX_DIM), ((0, 0), (0, 8 - IDX_HEADS), (0, 0)))
    wi = jnp.pad(misc[:, MISC_WI:MISC_WI + IDX_HEADS].reshape(s_n, IDX_HEADS, 1), ((0, 0), (0, 8 - IDX_HEADS), (0, 0)))
    gts = misc[:, MISC_GATE:MISC_GATE + 3 * NSA_HEADS].reshape(s_n, NSA_HEADS, 3)
    kv_pages = jnp.transpose(cache_kv, (0, 2, 3, 4, 1)).reshape(n_pool, KV_SLOTS * LANES, PAGE)
    idx_pages = jnp.transpose(cache_idx, (0, 2, 1))
    win_t = jnp.transpose(cache_win, (0, 2, 3, 4, 1)).reshape(s_n, 2 * LANES, n_win)
    per_seq = lambda shape: pl.BlockSpec((1,) + shape, lambda b, pt: (b, 0, 0))
    page_spec = lambda rows, p: pl.BlockSpec((1, rows, PAGE), lambda b, pt, p=p: (pt[b, p], 0, 0))
    in_specs = ([per_seq((NSA_HEADS, LANES)), per_seq((DSA_HEADS, LANES)), per_seq((8, IDX_DIM)),
                 per_seq((8, 1)), per_seq((NSA_HEADS, 3)), per_seq((1, 8 * LANES)), per_seq((1, IDX_DIM)),
                 per_seq((1, 2 * LANES)), per_seq((2 * LANES, n_win)),
                 pl.BlockSpec((2, BLK, LANES), lambda b, pt: (0, 0, 0)),
                 pl.BlockSpec((2, LANES, PAGE), lambda b, pt: (0, 0, 0))]
                + [page_spec(KV_SLOTS * LANES, p) for p in range(n_pages)]
                + [page_spec(IDX_DIM, p) for p in range(n_pages)])
    kern = functools.partial(_odd_sample_kernel, n_pages=n_pages, topk=topk, nbp=nbp)
    o, win_out = pl.pallas_call(
        kern,
        grid_spec=pltpu.PrefetchScalarGridSpec(
            num_scalar_prefetch=1,
            grid=(s_n,),
            in_specs=in_specs,
            out_specs=[per_seq((2 * NSA_HEADS, HEAD_DIM)), per_seq((2 * LANES, n_win))],
            scratch_shapes=[pltpu.VMEM((ISC_ROWS, LANES), f32)]),
        out_shape=[jax.ShapeDtypeStruct((s_n, 2 * NSA_HEADS, HEAD_DIM), f32),
                   jax.ShapeDtypeStruct((s_n, 2 * LANES, n_win), f32)],
        compiler_params=pltpu.CompilerParams(
            dimension_semantics=("arbitrary",), vmem_limit_bytes=VMEM_LIMIT),
        name="odd_attn_sample",
    )(page_table, qn, qd, qi, wi, gts, kvb.reshape(s_n, 1, 8 * LANES), misc[:, 0:IDX_DIM].reshape(s_n, 1, IDX_DIM),
      win_new.reshape(s_n, 1, 2 * LANES), win_t, wc2, wct,
      *([kv_pages] * n_pages), *([idx_pages] * n_pages))
    win_out = jnp.transpose(win_out.reshape(s_n, 2, KV_GROUPS, HEAD_DIM, n_win), (0, 4, 1, 2, 3))
    return o.reshape(s_n, 2 * HALF_W), win_out


def _split(z, sizes):
    cuts = [int(c) for c in np.cumsum(sizes)[:-1]]
    return jnp.split(z, cuts, axis=-1)


def _pad_cols(w, n):
    return jnp.pad(w, ((0, 0), (0, n - w.shape[1])))


def kernel(x_prompt, x_sample, state_conv, state_C, state_n, state_m, cache_kv, cache_idx, cache_win, page_table,
           w_in_e, gate_b_e, conv_w, conv_b, w_out_e, w_in_o, w_cmp, w_out_o, w_up, w_down, ln_g, ln_b):
    f32, bf16 = jnp.float32, jnp.bfloat16
    bp, tp, d = x_prompt.shape
    dbs = x_sample.shape[0]
    keep = cache_win.shape[2]
    yp = x_prompt.reshape(bp * tp, d)
    ys = x_sample.reshape(dbs, d)
    tm_p, tm_s = 512, dbs
    outs = {}
    for layer in range(DEPTH):
        i = layer // 2
        if layer % 2 == 0:
            w_in = _pad_cols(w_in_e[i], E_PACKED).astype(bf16)
            w_out = w_out_e[i].astype(bf16)
            ya, qkv, og, gts, tails = _even_proj(yp, w_in, conv_w[i], conv_b[i], tm_p, tp)
            yb, c_fin, n_fin, m_fin = _mlstm_prompt(qkv, og, gts, gate_b_e[i], bp, tp)
            cvp = tails.reshape(bp, tp // tm_p, CONV_TAIL, HALF_W)[:, -1, CONV_TAIL - (CONV_W - 1):]
            ya_s, yb_s, cvs, c_s, n_s, m_s = _even_sample(_proj(ys, w_in, tm_s), gate_b_e[i], conv_w[i], conv_b[i],
                                                          state_conv[i], state_C[i], state_n[i], state_m[i])
            outs['conv'] = (cvp[None], cvs[None])
            outs['c'] = (c_fin[None], c_s[None])
            outs['n'] = (n_fin[None], n_s[None])
            outs['m'] = (m_fin[:, :, 0][None], m_s[:, :, 0][None])
            yp = _outproj2_ln(ya, yb.reshape(bp * tp, HALF_W), w_out, yp, ln_g[layer, 0], ln_b[layer, 0], tm_p)
            ys = _outproj2_ln(ya_s, yb_s, w_out, ys, ln_g[layer, 0], ln_b[layer, 0], tm_s)
        else:
            w_out = w_out_o[i].astype(bf16)
            w_packed = _pack_w_in_o(w_in_o[i]).astype(bf16)
            wc2 = jnp.concatenate([w_cmp[i], w_cmp[i]], axis=-1)
            rows_first = lambda kvt: jnp.transpose(
                kvt.reshape(kvt.shape[0], KV_SLOTS, KV_GROUPS, HEAD_DIM, kvt.shape[2]), (0, 4, 1, 2, 3))
            zq, kvt_p, kvb, winp, miscp = _odd_proj(yp, w_packed, tm_p, tp)
            mp = _odd_attn_prompt(zq, miscp, kvb, wc2, bp, tp)
            kvp = rows_first(kvt_p)
            ixp = miscp[:, :IDX_DIM].reshape(bp, tp, IDX_DIM)
            wnp = winp.reshape(bp, tp, 2, KV_GROUPS, HEAD_DIM)[:, tp - keep:]
            zq_s, kvt_s, kvb_s, win_s, misc_s = _odd_proj(ys, w_packed, tm_s, dbs)
            ms, wns = _odd_attn_sample(zq_s, misc_s, kvb_s, win_s, cache_kv[i], cache_idx[i], cache_win[i],
                                       page_table, w_cmp[i])
            kvs = rows_first(kvt_s).reshape(dbs, 1, KV_SLOTS, KV_GROUPS, HEAD_DIM)
            ixs = misc_s[:, :IDX_DIM].reshape(dbs, 1, IDX_DIM)
            outs['kv'] = (kvp[None], kvs[None])
            outs['idx'] = (ixp[None], ixs[None])
            outs['win'] = (wnp[None], wns[None])
            yp = _outproj_ln(mp, w_out, yp, ln_g[layer, 0], ln_b[layer, 0], tm_p)
            ys = _outproj_ln(ms, w_out, ys, ln_g[layer, 0], ln_b[layer, 0], tm_s)
        wu, wd = w_up[layer].astype(bf16), w_down[layer].astype(bf16)
        yp = _mlp_ln(yp, wu, wd, ln_g[layer, 1], ln_b[layer, 1], 2 * tm_p, 1024)
        ys = _mlp_ln(ys, wu, wd, ln_g[layer, 1], ln_b[layer, 1], tm_s, 1024)
    return (yp.reshape(bp, tp, d), ys.reshape(dbs, 1, d),
            outs['conv'][0], outs['conv'][1], outs['c'][0], outs['c'][1],
            outs['n'][0], outs['n'][1], outs['m'][0], outs['m'][1],
            outs['kv'][0], outs['kv'][1], outs['idx'][0], outs['idx'][1],
            outs['win'][0], outs['win'][1])
```

```python
import functools

import jax
import jax.numpy as jnp
import numpy as np
from jax import lax
from jax.experimental import pallas as pl
from jax.experimental.pallas import tpu as pltpu

D_MODEL = 1024
DEPTH = 2
HALF_W = 512
D_FF = 4096
CONV_W = 3
ML_HEADS = 4
ML_DIM = 128
HEAD_DIM = 64
KV_GROUPS = 2
NSA_HEADS = 8
DSA_HEADS = 8
BLK = 64
NSA_TOPN = 8
WINDOW = 256
IDX_HEADS = 4
IDX_DIM = 32
DSA_TOPK_MAX = 256
KV_SLOTS = 6
FORCE_SCORE = 1e4
ALPHA = (2.0 * DEPTH) ** 0.25
LN_EPS = 1e-5
O_SIZES = (512, 256, 256, 256, 24, 512, 256, 128, 32, 4)
LANES = 128
VMEM_LIMIT = 48 * 1024 * 1024


def _round_up(n, m):
    return -(-n // m) * m


def _proj_kernel(x_ref, w_ref, o_ref):
    o_ref[...] = jnp.dot(x_ref[...].astype(jnp.bfloat16), w_ref[...],
                         preferred_element_type=jnp.float32)


def _proj(x, w_bf16, tm):
    m, k = x.shape
    n = w_bf16.shape[1]
    tn = n
    for cand in (1024, 768, 512, 256, 128):
        if n % cand == 0:
            tn = cand
            break
    return pl.pallas_call(
        _proj_kernel,
        grid=(n // tn, m // tm),
        in_specs=[pl.BlockSpec((tm, k), lambda j, i: (i, 0)),
                  pl.BlockSpec((k, tn), lambda j, i: (0, j))],
        out_specs=pl.BlockSpec((tm, tn), lambda j, i: (i, j)),
        out_shape=jax.ShapeDtypeStruct((m, n), jnp.float32),
        compiler_params=pltpu.CompilerParams(
            dimension_semantics=("arbitrary", "arbitrary"), vmem_limit_bytes=VMEM_LIMIT),
        name="proj",
    )(x, w_bf16)


def _layer_norm_rows(v, g, b):
    mu = jnp.mean(v, axis=-1, keepdims=True)
    d = v - mu
    var = jnp.mean(d * d, axis=-1, keepdims=True)
    return d * lax.rsqrt(var + LN_EPS) * g + b


def _outproj_ln_kernel(y_ref, w_ref, x_ref, g_ref, b_ref, o_ref):
    mix = jnp.dot(y_ref[...].astype(jnp.bfloat16), w_ref[...], preferred_element_type=jnp.float32)
    o_ref[...] = _layer_norm_rows(ALPHA * x_ref[...] + mix, g_ref[...], b_ref[...])


def _outproj_ln(y, w_bf16, x, g, b, tm):
    m, d = x.shape
    k = y.shape[1]
    return pl.pallas_call(
        _outproj_ln_kernel,
        grid=(m // tm,),
        in_specs=[pl.BlockSpec((tm, k), lambda i: (i, 0)),
                  pl.BlockSpec((k, d), lambda i: (0, 0)),
                  pl.BlockSpec((tm, d), lambda i: (i, 0)),
                  pl.BlockSpec((1, d), lambda i: (0, 0)),
                  pl.BlockSpec((1, d), lambda i: (0, 0))],
        out_specs=pl.BlockSpec((tm, d), lambda i: (i, 0)),
        out_shape=jax.ShapeDtypeStruct((m, d), jnp.float32),
        compiler_params=pltpu.CompilerParams(
            dimension_semantics=("arbitrary",), vmem_limit_bytes=VMEM_LIMIT),
        name="outproj_ln",
    )(y, w_bf16, x, g.reshape(1, d), b.reshape(1, d))


def _mlp_ln_kernel(x_ref, wu_ref, wd_ref, g_ref, b_ref, o_ref, acc_ref):
    f = pl.program_id(1)

    @pl.when(f == 0)
    def _():
        acc_ref[...] = jnp.zeros_like(acc_ref)

    up = jnp.dot(x_ref[...].astype(jnp.bfloat16), wu_ref[...], preferred_element_type=jnp.float32)
    act = jnp.square(jnp.maximum(up, 0.0))
    acc_ref[...] += jnp.dot(act.astype(jnp.bfloat16), wd_ref[...], preferred_element_type=jnp.float32)

    @pl.when(f == pl.num_programs(1) - 1)
    def _():
        o_ref[...] = _layer_norm_rows(ALPHA * x_ref[...] + acc_ref[...], g_ref[...], b_ref[...])


def _mlp_ln(x, wu_bf16, wd_bf16, g, b, tm, tf):
    m, d = x.shape
    ff = wu_bf16.shape[1]
    return pl.pallas_call(
        _mlp_ln_kernel,
        grid=(m // tm, ff // tf),
        in_specs=[pl.BlockSpec((tm, d), lambda i, f: (i, 0)),
                  pl.BlockSpec((d, tf), lambda i, f: (0, f)),
                  pl.BlockSpec((tf, d), lambda i, f: (f, 0)),
                  pl.BlockSpec((1, d), lambda i, f: (0, 0)),
                  pl.BlockSpec((1, d), lambda i, f: (0, 0))],
        out_specs=pl.BlockSpec((tm, d), lambda i, f: (i, 0)),
        out_shape=jax.ShapeDtypeStruct((m, d), jnp.float32),
        scratch_shapes=[pltpu.VMEM((tm, d), jnp.float32)],
        compiler_params=pltpu.CompilerParams(
            dimension_semantics=("arbitrary", "arbitrary"), vmem_limit_bytes=VMEM_LIMIT),
        name="mlp_ln",
    )(x, wu_bf16, wd_bf16, g.reshape(1, d), b.reshape(1, d))


E_QKV = 3 * HALF_W
E_OG = 6 * HALF_W
E_GATE = 7 * HALF_W
E_PACKED = 7 * HALF_W + LANES
CONV_TAIL = 8


def _even_proj_kernel(x_ref, w_ref, cw_ref, cb_ref, ya_ref, qkv_ref, og_ref, gate_ref, tail_ref, carry_ref,
                      *, tiles_per_seq):
    f32, bf16 = jnp.float32, jnp.bfloat16
    tm = x_ref.shape[0]

    @pl.when(pl.program_id(0) % tiles_per_seq == 0)
    def _():
        carry_ref[...] = jnp.zeros_like(carry_ref)

    z = jnp.dot(x_ref[...].astype(bf16), w_ref[...], preferred_element_type=f32)
    u = z[:, 2 * HALF_W:3 * HALF_W] * z[:, 0:HALF_W]
    prev = carry_ref[...]
    row = lax.broadcasted_iota(jnp.int32, (tm, 1), 0)
    conv = cb_ref[...] + cw_ref[CONV_W - 1:CONV_W, :] * u
    for back in range(1, CONV_W):
        shifted = pltpu.roll(u, back, 0)
        for r in range(back):
            shifted = jnp.where(row == r, prev[CONV_TAIL - back + r:CONV_TAIL - back + r + 1, :], shifted)
        conv = conv + cw_ref[CONV_W - 1 - back:CONV_W - back, :] * shifted
    ya_ref[...] = (z[:, HALF_W:2 * HALF_W] * conv).astype(bf16)
    tail = u[tm - CONV_TAIL:]
    carry_ref[...] = tail
    tail_ref[0] = tail
    qkv_ref[:, 0:HALF_W] = z[:, E_QKV:E_QKV + HALF_W].astype(bf16)
    qkv_ref[:, HALF_W:2 * HALF_W] = (z[:, E_QKV + HALF_W:E_QKV + 2 * HALF_W] * (ML_DIM ** -0.5)).astype(bf16)
    qkv_ref[:, 2 * HALF_W:3 * HALF_W] = z[:, E_QKV + 2 * HALF_W:E_OG].astype(bf16)
    og_ref[...] = z[:, E_OG:E_GATE]
    gate_ref[...] = z[:, E_GATE:E_PACKED]


def _even_proj(x, w_bf16, conv_w, conv_b, tm, seq_len):
    m, k = x.shape
    widths = (HALF_W, 3 * HALF_W, HALF_W, LANES)
    dtypes = (jnp.bfloat16, jnp.bfloat16, jnp.float32, jnp.float32)
    kern = functools.partial(_even_proj_kernel, tiles_per_seq=seq_len // tm)
    return pl.pallas_call(
        kern,
        grid=(m // tm,),
        in_specs=[pl.BlockSpec((tm, k), lambda i: (i, 0)),
                  pl.BlockSpec((k, E_PACKED), lambda i: (0, 0)),
                  pl.BlockSpec((CONV_W, HALF_W), lambda i: (0, 0)),
                  pl.BlockSpec((1, HALF_W), lambda i: (0, 0))],
        out_specs=[pl.BlockSpec((tm, n), lambda i: (i, 0)) for n in widths]
        + [pl.BlockSpec((1, CONV_TAIL, HALF_W), lambda i: (i, 0, 0))],
        out_shape=[jax.ShapeDtypeStruct((m, n), dt) for n, dt in zip(widths, dtypes)]
        + [jax.ShapeDtypeStruct((m // tm, CONV_TAIL, HALF_W), jnp.float32)],
        scratch_shapes=[pltpu.VMEM((CONV_TAIL, HALF_W), jnp.float32)],
        compiler_params=pltpu.CompilerParams(
            dimension_semantics=("arbitrary",), vmem_limit_bytes=VMEM_LIMIT),
        name="even_proj",
    )(x, w_bf16, conv_w, conv_b.reshape(1, HALF_W))


def _mlstm_kernel(qkv_ref, og_ref, gate_ref, gb_ref, yb_ref, c_ref, n_ref, m_ref):
    f32, bf16 = jnp.float32, jnp.bfloat16
    L = qkv_ref.shape[1]

    @pl.when(pl.program_id(1) == 0)
    def _():
        c_ref[...] = jnp.zeros_like(c_ref)
        n_ref[...] = jnp.zeros_like(n_ref)
        m_ref[...] = jnp.zeros_like(m_ref)

    pre = gate_ref[0] + gb_ref[...]
    lf = jax.nn.log_sigmoid(pre)
    row = lax.broadcasted_iota(jnp.int32, (L, L), 0)
    col = lax.broadcasted_iota(jnp.int32, (L, L), 1)
    causal = col <= row
    tri = jnp.where(causal, 1.0, 0.0).astype(f32)
    b_all = jnp.dot(tri, lf, preferred_element_type=f32, precision=lax.Precision.HIGHEST)
    pre_t = pre.T
    b_t = b_all.T
    for h in range(ML_HEADS):
        q = qkv_ref[0, :, h * ML_DIM:(h + 1) * ML_DIM]
        k = qkv_ref[0, :, HALF_W + h * ML_DIM:HALF_W + (h + 1) * ML_DIM]
        v = qkv_ref[0, :, 2 * HALF_W + h * ML_DIM:2 * HALF_W + (h + 1) * ML_DIM]
        ig_col = pre[:, h:h + 1]
        b_col = b_all[:, ML_HEADS + h:ML_HEADS + h + 1]
        a_row = pre_t[h:h + 1, :] - b_t[ML_HEADS + h:ML_HEADS + h + 1, :]
        m_st = m_ref[0, h:h + 1, 0:1]
        c_st = c_ref[0, h]
        n_st = n_ref[0, h:h + 1, :]
        dmat = jnp.where(causal, b_col + a_row, -jnp.inf)
        inter = b_col + m_st
        m_t = jnp.maximum(inter, jnp.max(dmat, axis=-1, keepdims=True))
        w_intra = jnp.exp(dmat - m_t)
        w_inter = jnp.exp(inter - m_t)
        s = _dot_nt(q, k) * w_intra
        num = jnp.dot(s.astype(bf16), v, preferred_element_type=f32) + w_inter * _dot_nt(q, c_st.astype(bf16))
        den = jnp.sum(s, axis=-1, keepdims=True) + w_inter * jnp.sum(q.astype(f32) * n_st, axis=-1, keepdims=True)
        hs = num * (1.0 / jnp.maximum(jnp.abs(den), jnp.exp(-m_t)))
        m_new = m_t[L - 1:L, :]
        b_last = b_col[L - 1:L, :]
        w_state = jnp.exp(b_last - b_col + ig_col - m_new)
        decay = jnp.exp(b_last + m_st - m_new)
        vw = (v.astype(f32) * w_state).astype(bf16)
        c_ref[0, h] = decay * c_st + lax.dot_general(vw, k, (((0,), (0,)), ((), ())), preferred_element_type=f32)
        n_ref[0, h:h + 1, :] = decay * n_st + jnp.sum(k.astype(f32) * w_state, axis=0, keepdims=True)
        m_ref[0, h:h + 1, :] = jnp.broadcast_to(m_new, (1, LANES))
        og = og_ref[0, :, h * ML_DIM:(h + 1) * ML_DIM]
        yb_ref[0, :, h * ML_DIM:(h + 1) * ML_DIM] = (jax.nn.sigmoid(og) * hs).astype(bf16)


def _mlstm_prompt(qkv, og, gates, gate_b, bsz, t_len, chunk=128):
    gb = jnp.pad(gate_b, (0, LANES - gate_b.shape[0])).reshape(1, LANES)
    return pl.pallas_call(
        _mlstm_kernel,
        grid=(bsz, t_len // chunk),
        in_specs=[pl.BlockSpec((1, chunk, 3 * HALF_W), lambda b, c: (b, c, 0)),
                  pl.BlockSpec((1, chunk, HALF_W), lambda b, c: (b, c, 0)),
                  pl.BlockSpec((1, chunk, LANES), lambda b, c: (b, c, 0)),
                  pl.BlockSpec((1, LANES), lambda b, c: (0, 0))],
        out_specs=[pl.BlockSpec((1, chunk, HALF_W), lambda b, c: (b, c, 0)),
                   pl.BlockSpec((1, ML_HEADS, ML_DIM, ML_DIM), lambda b, c: (b, 0, 0, 0)),
                   pl.BlockSpec((1, ML_HEADS, ML_DIM), lambda b, c: (b, 0, 0)),
                   pl.BlockSpec((1, ML_HEADS, LANES), lambda b, c: (b, 0, 0))],
        out_shape=[jax.ShapeDtypeStruct((bsz, t_len, HALF_W), jnp.bfloat16),
                   jax.ShapeDtypeStruct((bsz, ML_HEADS, ML_DIM, ML_DIM), jnp.float32),
                   jax.ShapeDtypeStruct((bsz, ML_HEADS, ML_DIM), jnp.float32),
                   jax.ShapeDtypeStruct((bsz, ML_HEADS, LANES), jnp.float32)],
        compiler_params=pltpu.CompilerParams(
            dimension_semantics=("arbitrary", "arbitrary"), vmem_limit_bytes=VMEM_LIMIT),
        name="mlstm_prompt",
    )(qkv.reshape(bsz, t_len, 3 * HALF_W), og.reshape(bsz, t_len, HALF_W),
      gates.reshape(bsz, t_len, LANES), gb)


def _outproj2_ln_kernel(ya_ref, yb_ref, w_ref, x_ref, g_ref, b_ref, o_ref):
    mix = jnp.dot(ya_ref[...], w_ref[0:HALF_W, :], preferred_element_type=jnp.float32)
    mix = mix + jnp.dot(yb_ref[...], w_ref[HALF_W:, :], preferred_element_type=jnp.float32)
    o_ref[...] = _layer_norm_rows(ALPHA * x_ref[...] + mix, g_ref[...], b_ref[...])


def _outproj2_ln(ya, yb, w_bf16, x, g, b, tm):
    m, d = x.shape
    return pl.pallas_call(
        _outproj2_ln_kernel,
        grid=(m // tm,),
        in_specs=[pl.BlockSpec((tm, HALF_W), lambda i: (i, 0)),
                  pl.BlockSpec((tm, HALF_W), lambda i: (i, 0)),
                  pl.BlockSpec((2 * HALF_W, d), lambda i: (0, 0)),
                  pl.BlockSpec((tm, d), lambda i: (i, 0)),
                  pl.BlockSpec((1, d), lambda i: (0, 0)),
                  pl.BlockSpec((1, d), lambda i: (0, 0))],
        out_specs=pl.BlockSpec((tm, d), lambda i: (i, 0)),
        out_shape=jax.ShapeDtypeStruct((m, d), jnp.float32),
        compiler_params=pltpu.CompilerParams(
            dimension_semantics=("arbitrary",), vmem_limit_bytes=VMEM_LIMIT),
        name="outproj2_ln",
    )(ya, yb, w_bf16, x, g.reshape(1, d), b.reshape(1, d))


SAMPLE_ROWS = 8


def _even_sample_kernel(h_ref, bg_ref, cg_ref, q_ref, k_ref, v_ref, og_ref, gate_ref, gb_ref, cw_ref, cb_ref,
                        conv_ref, c_ref, n_ref, m_ref, ya_ref, yb_ref, conv_out_ref, c_out_ref, n_out_ref, m_out_ref):
    f32, bf16 = jnp.float32, jnp.bfloat16
    rows = h_ref.shape[0]
    row = lax.broadcasted_iota(jnp.int32, (rows, 1), 0)
    tn = (((0,), (0,)), ((), ()))
    u = cg_ref[...] * h_ref[...]
    prev0, prev1 = conv_ref[:, 0, :], conv_ref[:, 1, :]
    conv = cb_ref[...] + cw_ref[0:1, :] * prev0 + cw_ref[1:2, :] * prev1 + cw_ref[2:3, :] * u
    ya_ref[...] = (bg_ref[...] * conv).astype(bf16)
    conv_out_ref[:, 0, :] = prev1
    conv_out_ref[:, 1, :] = u
    pre = gate_ref[...] + gb_ref[...]
    lf_all = jax.nn.log_sigmoid(pre)
    for h in range(ML_HEADS):
        cols = slice(h * ML_DIM, (h + 1) * ML_DIM)
        q = q_ref[:, cols]
        k = k_ref[:, cols] * (ML_DIM ** -0.5)
        v = v_ref[:, cols]
        ig = pre[:, h:h + 1]
        lf = lf_all[:, ML_HEADS + h:ML_HEADS + h + 1]
        m_st = m_ref[:, h:h + 1]
        n_st = n_ref[:, h, :]
        inter = lf + m_st
        m_t = jnp.maximum(inter, ig)
        w_intra = jnp.exp(ig - m_t)
        w_inter = jnp.exp(inter - m_t)
        qb, kb = q.astype(bf16), k.astype(bf16)
        s = jnp.sum(qb.astype(f32) * kb.astype(f32), axis=-1, keepdims=True) * w_intra
        cq = jnp.zeros((rows, ML_DIM), f32)
        for b in range(rows):
            c_b = c_ref[b, h]
            cq = jnp.where(row == b, _dot_nt(qb, c_b.astype(bf16)), cq)
            vw = jnp.where(row == b, v * w_intra, 0.0)
            outer = lax.dot_general(vw, k, tn, preferred_element_type=f32, precision=lax.Precision.HIGHEST)
            c_out_ref[b, h] = w_inter[b:b + 1, :] * c_b + outer
        num = s * v + w_inter * cq
        den = s + w_inter * jnp.sum(n_st * q, axis=-1, keepdims=True)
        hs = num * (1.0 / jnp.maximum(jnp.abs(den), jnp.exp(-m_t)))
        n_out_ref[:, h, :] = w_inter * n_st + w_intra * k
        m_out_ref[:, h, :] = jnp.broadcast_to(m_t, (rows, LANES))
        yb_ref[:, cols] = (jax.nn.sigmoid(og_ref[:, cols]) * hs).astype(bf16)


def _even_sample(z, gate_b, conv_w, conv_b, state_conv, state_c, state_n, state_m):
    s_n = z.shape[0]
    f32 = jnp.float32
    r = SAMPLE_ROWS
    gb = jnp.pad(gate_b, (0, LANES - gate_b.shape[0])).reshape(1, LANES)
    col = lambda j: pl.BlockSpec((r, HALF_W), lambda i, j=j: (i, j))
    in_specs = [col(j) for j in range(7)] + [
        pl.BlockSpec((r, LANES), lambda i: (i, E_GATE // LANES)),
        pl.BlockSpec((1, LANES), lambda i: (0, 0)),
        pl.BlockSpec((CONV_W, HALF_W), lambda i: (0, 0)),
        pl.BlockSpec((1, HALF_W), lambda i: (0, 0)),
        pl.BlockSpec((r, CONV_W - 1, HALF_W), lambda i: (i, 0, 0)),
        pl.BlockSpec((r, ML_HEADS, ML_DIM, ML_DIM), lambda i: (i, 0, 0, 0)),
        pl.BlockSpec((r, ML_HEADS, ML_DIM), lambda i: (i, 0, 0)),
        pl.BlockSpec((r, ML_HEADS), lambda i: (i, 0))]
    out_specs = [pl.BlockSpec((r, HALF_W), lambda i: (i, 0)),
                 pl.BlockSpec((r, HALF_W), lambda i: (i, 0)),
                 pl.BlockSpec((r, CONV_W - 1, HALF_W), lambda i: (i, 0, 0)),
                 pl.BlockSpec((r, ML_HEADS, ML_DIM, ML_DIM), lambda i: (i, 0, 0, 0)),
                 pl.BlockSpec((r, ML_HEADS, ML_DIM), lambda i: (i, 0, 0)),
                 pl.BlockSpec((r, ML_HEADS, LANES), lambda i: (i, 0, 0))]
    out_shape = [jax.ShapeDtypeStruct((s_n, HALF_W), jnp.bfloat16),
                 jax.ShapeDtypeStruct((s_n, HALF_W), jnp.bfloat16),
                 jax.ShapeDtypeStruct((s_n, CONV_W - 1, HALF_W), f32),
                 jax.ShapeDtypeStruct((s_n, ML_HEADS, ML_DIM, ML_DIM), f32),
                 jax.ShapeDtypeStruct((s_n, ML_HEADS, ML_DIM), f32),
                 jax.ShapeDtypeStruct((s_n, ML_HEADS, LANES), f32)]
    return pl.pallas_call(
        _even_sample_kernel,
        grid=(s_n // r,),
        in_specs=in_specs,
        out_specs=out_specs,
        out_shape=out_shape,
        compiler_params=pltpu.CompilerParams(
            dimension_semantics=("arbitrary",), vmem_limit_bytes=VMEM_LIMIT),
        name="even_sample",
    )(z, z, z, z, z, z, z, z, gb, conv_w, conv_b.reshape(1, HALF_W), state_conv, state_c, state_n, state_m)


O_Q = 0
O_QD = NSA_HEADS * LANES
O_KV = 2048
O_WIN = 2816
O_QI = 3072
O_MISC = 3200
O_PACKED = 3328
ZQ_QI = 2048
MISC_WI = IDX_DIM
MISC_GATE = IDX_DIM + IDX_HEADS
NEG_BIG = -(2.0 ** 100)
INT_MIN = -(2 ** 31)
ALIBI = tuple(float(2.0 ** (-8.0 * (h + 1) / NSA_HEADS)) for h in range(NSA_HEADS))


def _pack_w_in_o(w):
    qn, kvc, kvs, kvw, gates, qd, kvd, qi, ki, wi = _split(w, O_SIZES)
    scale = HEAD_DIM ** -0.5
    rows = w.shape[0]

    def spread(q):
        q = (q * scale).reshape(rows, NSA_HEADS, HEAD_DIM)
        z = jnp.zeros_like(q)
        low = (jnp.arange(NSA_HEADS) < NSA_HEADS // KV_GROUPS)[None, :, None]
        return jnp.concatenate([jnp.where(low, q, z), jnp.where(low, z, q)], axis=-1).reshape(rows, NSA_HEADS * LANES)

    pad = jnp.zeros((rows, O_PACKED - O_MISC - IDX_DIM - IDX_HEADS - 3 * NSA_HEADS), w.dtype)
    return jnp.concatenate([spread(qn), spread(qd), kvc, kvs, kvd, kvw, qi, ki, wi, gates, pad], axis=1)


def _odd_proj_kernel(x_ref, w_ref, wkvt_ref, zq_ref, kvt_ref, kvb_ref, win_ref, misc_ref):
    xb = x_ref[...].astype(jnp.bfloat16)
    z = jnp.dot(xb, w_ref[...], preferred_element_type=jnp.float32)
    zq_ref[:, 0:O_KV] = z[:, 0:O_KV].astype(jnp.bfloat16)
    zq_ref[:, ZQ_QI:ZQ_QI + LANES] = z[:, O_QI:O_MISC].astype(jnp.bfloat16)
    kvt_ref[0] = _dot_nt(wkvt_ref[...], xb)
    kvb_ref[...] = z[:, O_KV:O_QI].astype(jnp.bfloat16)
    win_ref[...] = z[:, O_WIN:O_QI]
    misc_ref[...] = z[:, O_MISC:O_PACKED]


def _odd_proj(x, w_packed_bf16, tm, seq_len):
    m, k = x.shape
    n_kv = O_WIN - O_KV
    widths = (O_KV + LANES, O_QI - O_KV, O_QI - O_WIN, LANES)
    dtypes = (jnp.bfloat16, jnp.bfloat16, jnp.float32, jnp.float32)
    row_spec = lambda n: pl.BlockSpec((tm, n), lambda i: (i, 0))
    tiles = seq_len // tm
    out_specs = [row_spec(widths[0]), pl.BlockSpec((1, n_kv, tm), lambda i: (i // tiles, 0, i % tiles))]
    out_specs += [row_spec(n) for n in widths[1:]]
    out_shape = [jax.ShapeDtypeStruct((m, widths[0]), dtypes[0]),
                 jax.ShapeDtypeStruct((m // seq_len, n_kv, seq_len), jnp.float32)]
    out_shape += [jax.ShapeDtypeStruct((m, n), dt) for n, dt in zip(widths[1:], dtypes[1:])]
    return pl.pallas_call(
        _odd_proj_kernel,
        grid=(m // tm,),
        in_specs=[pl.BlockSpec((tm, k), lambda i: (i, 0)),
                  pl.BlockSpec((k, O_PACKED), lambda i: (0, 0)),
                  pl.BlockSpec((n_kv, k), lambda i: (0, 0))],
        out_specs=out_specs,
        out_shape=out_shape,
        compiler_params=pltpu.CompilerParams(
            dimension_semantics=("arbitrary",), vmem_limit_bytes=VMEM_LIMIT),
        name="odd_proj",
    )(x, w_packed_bf16, w_packed_bf16[:, O_KV:O_WIN].T)


def _dot_nt(a, b):
    return lax.dot_general(a, b, (((1,), (1,)), ((), ())), preferred_element_type=jnp.float32)


def _flash_heads(q_heads, slopes, group, kv_ref, k_blk, v_blk, tk, lo, hi, bias_fn, skip_fn=None, start_fn=None):
    f32, bf16 = jnp.float32, jnp.bfloat16
    if start_fn is None:
        start_fn = lambda kt: kt * tk
    tq = q_heads[0].shape[0]
    n_heads = len(q_heads)
    lane = lax.broadcasted_iota(jnp.int32, (1, LANES), 1)
    own_half = (lane // HEAD_DIM) == group
    q_all = jnp.concatenate(q_heads, axis=0)
    slope_col = jnp.concatenate([jnp.full((tq, 1), sl, f32) for sl in slopes], axis=0)

    def tile(kt, carry):
        m, acc = carry
        start = pl.multiple_of(start_fn(kt), LANES)
        rows = pl.ds(start, tk)
        k = kv_ref[0, rows, k_blk * LANES:(k_blk + 1) * LANES]
        v = kv_ref[0, rows, v_blk * LANES:(v_blk + 1) * LANES]
        v = jnp.where(own_half, v, jnp.ones_like(v))
        sp = (start + lax.broadcasted_iota(jnp.int32, (1, tk), 1)).astype(f32)
        b = bias_fn(kt)
        s = _dot_nt(q_all, k) + slope_col * sp + jnp.concatenate([b] * n_heads, axis=0)
        m_new = jnp.maximum(m, jnp.max(s, axis=-1, keepdims=True))
        p = jnp.exp(s - m_new)
        acc = jnp.exp(m - m_new) * acc + jnp.dot(p.astype(bf16), v, preferred_element_type=f32)
        return m_new, acc

    def body(kt, carry):
        if skip_fn is None:
            return tile(kt, carry)
        return lax.cond(skip_fn(kt), lambda c: c, functools.partial(tile, kt), carry)

    m0 = jnp.full((n_heads * tq, 1), -jnp.inf, f32)
    a0 = jnp.zeros((n_heads * tq, LANES), f32)
    _, acc = lax.fori_loop(lo, hi, body, (m0, a0))
    return [acc[r * tq:(r + 1) * tq] for r in range(n_heads)]


def _odd_attn_kernel(zq_ref, miscq_ref, kvb_ref, misck_ref, wc_ref, o_ref,
                     kc_ref, vc_ref, ki_ref, key_ref, dbias_ref, flag_ref, *, t_len, tq, tk, topk):
    f32, bf16, i32 = jnp.float32, jnp.bfloat16, jnp.int32
    nb = t_len // BLK
    qi_blk = pl.program_id(1)
    q0 = qi_blk * tq

    @pl.when(qi_blk == 0)
    def _():
        ck = kvb_ref[0, :, 0:LANES].astype(f32).reshape(nb, BLK, LANES)
        kc_ref[...] = jnp.sum(ck * wc_ref[0][None], axis=1).astype(bf16)
        cv = kvb_ref[0, :, LANES:2 * LANES].astype(f32).reshape(nb, BLK, LANES)
        vc = jnp.sum(cv * wc_ref[1][None], axis=1)
        for g in range(KV_GROUPS):
            half = vc[:, g * HEAD_DIM:(g + 1) * HEAD_DIM]
            vc_ref[g] = jnp.concatenate([half, half], axis=1).astype(bf16)
        ki_ref[...] = misck_ref[0, :, 0:IDX_DIM].astype(bf16)

    misc = miscq_ref[0]
    gates = jax.nn.sigmoid(misc[:, MISC_GATE:MISC_GATE + 3 * NSA_HEADS])
    t_col = q0 + lax.broadcasted_iota(i32, (tq, 1), 0)
    lane = lax.broadcasted_iota(i32, (1, LANES), 1)
    low_half = lane < HEAD_DIM
    heads_per_group = NSA_HEADS // KV_GROUPS

    def q_head(base, h):
        return zq_ref[0, :, base + h * LANES: base + (h + 1) * LANES]

    def pair(even, odd):
        return jnp.where(low_half, even, odd)

    def normalised_pair(acc_even, acc_odd, g):
        outs = []
        for parity, acc in enumerate((acc_even, acc_odd)):
            swapped = pltpu.roll(acc, HEAD_DIM, 1)
            outs.append(acc * (1.0 / swapped) if parity == g else swapped * (1.0 / acc))
        return pair(outs[0], outs[1])

    def gate_pair(j, c):
        a = gates[:, (2 * j) * 3 + c:(2 * j) * 3 + c + 1]
        b = gates[:, (2 * j + 1) * 3 + c:(2 * j + 1) * 3 + c + 1]
        return jnp.where(low_half, a, b)

    jb = lax.broadcasted_iota(i32, (1, nb), 1)
    dist_c = t_col - (jb * BLK + (BLK - 1))
    mask_c = dist_c >= 0
    dist_cf = dist_c.astype(f32)
    cur = t_col // BLK
    forced = (jb == 0) | (jb == cur) | (jb == cur - 1)
    admissible = jb <= cur
    jb_full = lax.broadcasted_iota(i32, (tq, nb), 1)
    o_cmp, sel_bias, sel_any = [], [], []
    for g in range(KV_GROUPS):
        imp = jnp.zeros((tq, nb), f32)
        for r in range(heads_per_group):
            h = g * heads_per_group + r
            s = _dot_nt(q_head(O_Q, h), kc_ref[...]) - ALIBI[h] * dist_cf
            s = jnp.where(mask_c, s, NEG_BIG)
            m = jnp.max(s, axis=-1, keepdims=True)
            e = jnp.where(mask_c, jnp.exp(s - m), 0.0)
            p = e * (1.0 / jnp.maximum(jnp.sum(e, axis=-1, keepdims=True), 1e-30))
            imp = imp + p
            o_cmp.append(jnp.dot(p.astype(bf16), vc_ref[g], preferred_element_type=f32))
        imp = jnp.where(forced, FORCE_SCORE, imp)
        imp = jnp.where(admissible, imp, -jnp.inf)
        sel = jnp.zeros((tq, nb), f32)
        for _ in range(min(NSA_TOPN, nb)):
            m = jnp.max(imp, axis=-1, keepdims=True)
            first = jnp.min(jnp.where(imp == m, jb_full, nb), axis=-1, keepdims=True)
            hit = jb_full == first
            sel = jnp.where(hit & (m > -jnp.inf), 1.0, sel)
            imp = jnp.where(hit, -jnp.inf, imp)
        sel_bias.append(jnp.where(sel > 0.0, 0.0, NEG_BIG).astype(bf16))
        sel_any.append(jnp.max(sel, axis=0, keepdims=True))

    n_sel_tiles = (q0 + tq + tk - 1) // tk
    blocks_per_tile = tk // BLK

    for g in range(KV_GROUPS):
        for j in range(nb // blocks_per_tile):
            hit = jnp.max(sel_any[g][:, j * blocks_per_tile:(j + 1) * blocks_per_tile])
            flag_ref[g, j] = (hit > 0.0).astype(i32)

    def sel_skip_fn(g):
        return lambda kt: flag_ref[g, kt] == 0

    def causal_bias(start, width):
        sp = start + lax.broadcasted_iota(i32, (1, width), 1)
        return jnp.where(sp <= t_col, 0.0, NEG_BIG)

    def sel_bias_fn(g):
        def fn(kt):
            row = lax.broadcasted_iota(i32, (nb, tk), 0)
            col = lax.broadcasted_iota(i32, (nb, tk), 1)
            expand = jnp.where(row == kt * blocks_per_tile + col // BLK, 1.0, 0.0).astype(bf16)
            return jnp.dot(sel_bias[g], expand, preferred_element_type=f32) + causal_bias(kt * tk, tk)
        return fn

    tkw = min(WINDOW + tq, t_len)
    win_start = jnp.clip(q0 - WINDOW, 0, t_len - tkw)

    def win_bias_fn(kt):
        dist = t_col - (win_start + lax.broadcasted_iota(i32, (1, tkw), 1))
        return jnp.where((dist >= 0) & (dist < WINDOW), 0.0, NEG_BIG)

    qi_heads = [zq_ref[0, :, ZQ_QI + h * IDX_DIM: ZQ_QI + (h + 1) * IDX_DIM] for h in range(IDX_HEADS)]
    pick = jnp.where(lax.broadcasted_iota(i32, (8, LANES), 1) == MISC_WI + lax.broadcasted_iota(i32, (8, LANES), 0),
                     1.0, 0.0)
    wi_rows = lax.dot_general(pick, misc, (((1,), (1,)), ((), ())), preferred_element_type=f32,
                              precision=lax.Precision.HIGHEST)
    t_row = q0 + lax.broadcasted_iota(i32, (1, tq), 1)

    def fold8(x):
        x = x.reshape(tk // 8, 8, tq)
        while x.shape[0] > 1:
            half = x.shape[0] // 2
            x = x[:half] + x[half:]
        return x[0]

    def index_tile(kt, carry):
        start = pl.multiple_of(kt * tk, tk)
        kik = ki_ref[pl.ds(start, tk), :]
        tot = jnp.zeros((tk, tq), f32)
        for h in range(IDX_HEADS):
            sc = _dot_nt(kik, qi_heads[h]) * (IDX_DIM ** -0.5)
            tot = tot + jnp.maximum(sc, 0.0) * wi_rows[h:h + 1, :]
        tot = tot * (IDX_HEADS ** -0.5)
        tot = jnp.where(tot == 0.0, 0.0, tot)
        bits = lax.bitcast_convert_type(tot, i32)
        key = jnp.where(bits < 0, bits ^ jnp.int32(0x7FFFFFFF), bits)
        sp = start + lax.broadcasted_iota(i32, (tk, 1), 0)
        key_ref[kt] = jnp.where(sp <= t_row, key, INT_MIN)
        return carry

    lax.fori_loop(0, n_sel_tiles, index_tile, 0)

    def count_ge(trial):
        def body(kt, acc):
            return acc + fold8((key_ref[kt] >= trial).astype(i32))
        acc = lax.fori_loop(0, n_sel_tiles, body, jnp.zeros((8, tq), i32))
        return jnp.sum(acc, axis=0, keepdims=True)

    c_pos = count_ge(jnp.zeros((1, tq), i32))
    v0 = jnp.where(c_pos >= topk, 0, INT_MIN).astype(i32)
    c0 = jnp.where(c_pos >= topk, c_pos, n_sel_tiles * tk)

    def bit_step(i, carry):
        v, cge = carry
        trial = v | lax.shift_left(jnp.int32(1), 30 - i)
        c = count_ge(trial)
        ok = c >= topk
        return jnp.where(ok, trial, v), jnp.where(ok, c, cge)

    v_thr, c_ge = lax.fori_loop(0, 31, bit_step, (v0, c0))
    v_eff = jnp.maximum(v_thr, INT_MIN + 1)

    def write_plain(kt, carry):
        dbias_ref[kt] = jnp.where(key_ref[kt] >= v_eff, 0.0, NEG_BIG).T
        return carry

    def plain_path():
        lax.fori_loop(0, n_sel_tiles, write_plain, 0)

    def tie_path():
        need = topk - count_ge(v_thr + 1)

        def count_tie_below(limit):
            def body(kt, acc):
                sp = kt * tk + lax.broadcasted_iota(i32, (tk, 1), 0)
                return acc + fold8(((key_ref[kt] == v_thr) & (sp < limit)).astype(i32))
            acc = lax.fori_loop(0, n_sel_tiles, body, jnp.zeros((8, tq), i32))
            return jnp.sum(acc, axis=0, keepdims=True)

        n_bits = max(1, (t_len - 1).bit_length())

        def idx_step(i, w):
            trial = w | lax.shift_left(jnp.int32(1), n_bits - 1 - i)
            return jnp.where(count_tie_below(trial) < need, trial, w)

        w_last = lax.fori_loop(0, n_bits, idx_step, jnp.zeros((1, tq), i32))

        def write_tie(kt, carry):
            sp = kt * tk + lax.broadcasted_iota(i32, (tk, 1), 0)
            key = key_ref[kt]
            take = (key > v_thr) | ((key == v_thr) & (sp <= w_last))
            dbias_ref[kt] = jnp.where(take & (key >= v_eff), 0.0, NEG_BIG).T
            return carry

        lax.fori_loop(0, n_sel_tiles, write_tie, 0)

    lax.cond(jnp.max(c_ge) > topk, tie_path, plain_path)

    def dsa_bias_fn(kt):
        return dbias_ref[kt]

    for g in range(KV_GROUPS):
        hs = [g * heads_per_group + r for r in range(heads_per_group)]
        slopes = [ALIBI[h] for h in hs]
        qn = [q_head(O_Q, h) for h in hs]
        qd = [q_head(O_QD, h) for h in hs]
        a_s = _flash_heads(qn, slopes, g, kvb_ref, 2, 3, tk, 0, n_sel_tiles, sel_bias_fn(g), sel_skip_fn(g))
        a_w = _flash_heads(qn, slopes, g, kvb_ref, 6, 7, tkw, 0, 1, win_bias_fn, start_fn=lambda kt: win_start)
        a_d = _flash_heads(qd, slopes, g, kvb_ref, 4, 5, tk, 0, n_sel_tiles, dsa_bias_fn)
        for jj in range(heads_per_group // 2):
            j = hs[2 * jj] // 2
            o_n = (gate_pair(j, 0) * pair(o_cmp[2 * j], o_cmp[2 * j + 1])
                   + gate_pair(j, 1) * normalised_pair(a_s[2 * jj], a_s[2 * jj + 1], g)
                   + gate_pair(j, 2) * normalised_pair(a_w[2 * jj], a_w[2 * jj + 1], g))
            o_ref[0, :, j * LANES:(j + 1) * LANES] = o_n.astype(o_ref.dtype)
            o_d = normalised_pair(a_d[2 * jj], a_d[2 * jj + 1], g)
            o_ref[0, :, HALF_W + j * LANES:HALF_W + (j + 1) * LANES] = o_d.astype(o_ref.dtype)


def _odd_attn_prompt(zq, misc, kvb, wc2, bsz, t_len, tq=256, tk=512):
    tk = min(tk, t_len)
    topk = min(DSA_TOPK_MAX, t_len // 4)
    nb = t_len // BLK
    n_kt = t_len // tk
    zq3 = zq.reshape(bsz, t_len, zq.shape[1])
    misc3 = misc.reshape(bsz, t_len, LANES)
    kvb3 = kvb.reshape(bsz, t_len, kvb.shape[1])
    kern = functools.partial(_odd_attn_kernel, t_len=t_len, tq=tq, tk=tk, topk=topk)
    out = pl.pallas_call(
        kern,
        grid=(bsz, t_len // tq),
        in_specs=[pl.BlockSpec((1, tq, zq.shape[1]), lambda b, i: (b, i, 0)),
                  pl.BlockSpec((1, tq, LANES), lambda b, i: (b, i, 0)),
                  pl.BlockSpec((1, t_len, kvb.shape[1]), lambda b, i: (b, 0, 0)),
                  pl.BlockSpec((1, t_len, LANES), lambda b, i: (b, 0, 0)),
                  pl.BlockSpec((2, BLK, LANES), lambda b, i: (0, 0, 0))],
        out_specs=pl.BlockSpec((1, tq, 2 * HALF_W), lambda b, i: (b, i, 0)),
        out_shape=jax.ShapeDtypeStruct((bsz, t_len, 2 * HALF_W), jnp.bfloat16),
        scratch_shapes=[pltpu.VMEM((nb, LANES), jnp.bfloat16),
                        pltpu.VMEM((KV_GROUPS, nb, LANES), jnp.bfloat16),
                        pltpu.VMEM((t_len, IDX_DIM), jnp.bfloat16),
                        pltpu.VMEM((n_kt, tk, tq), jnp.int32),
                        pltpu.VMEM((n_kt, tq, tk), jnp.float32),
                        pltpu.SMEM((KV_GROUPS, n_kt), jnp.int32)],
        compiler_params=pltpu.CompilerParams(
            dimension_semantics=("arbitrary", "arbitrary"), vmem_limit_bytes=VMEM_LIMIT),
        name="odd_attn_prompt",
    )(zq3, misc3, kvb3, misc3, wc2)
    return out.reshape(bsz * t_len, 2 * HALF_W)


PAGE = 128
ISC_ROWS = 24


SAMPLE_SEQS = 2


def _odd_sample_kernel(pt_ref, *refs, n_pages, topk, nbp):
    del pt_ref
    for u in range(SAMPLE_SEQS):
        _odd_sample_one(u, *refs, n_pages=n_pages, topk=topk, nbp=nbp)


def _odd_sample_one(u, qn_ref, qd_ref, qi_ref, wi_ref, gate_ref, newb_ref, newki_ref, neww_ref,
                    win_ref, wc_ref, wct_ref, *rest, n_pages, topk, nbp):
    f32, bf16, i32 = jnp.float32, jnp.bfloat16, jnp.int32
    hi_prec = lax.Precision.HIGHEST
    all_pages = SAMPLE_SEQS * n_pages
    kv_refs = rest[u * n_pages:(u + 1) * n_pages]
    idx_refs = rest[all_pages + u * n_pages:all_pages + (u + 1) * n_pages]
    o_ref, wout_ref, isc_all_ref = rest[2 * all_pages:]
    isc_ref = isc_all_ref.at[u]
    past = n_pages * PAGE
    nb = past // BLK + 1
    row8 = lax.broadcasted_iota(i32, (8, 1), 0)
    lane = lax.broadcasted_iota(i32, (1, LANES), 1)
    slope8 = jnp.zeros((8, 1), f32)
    for h in range(NSA_HEADS):
        slope8 = jnp.where(row8 == h, ALIBI[h], slope8)
    low_group = row8 < NSA_HEADS // KV_GROUPS

    def pick_half(x):
        return jnp.where(low_group, x[:, 0:HEAD_DIM], x[:, HEAD_DIM:2 * HEAD_DIM])

    qn8, qd8 = qn_ref[u], qd_ref[u]
    qi8, wi8 = qi_ref[u], wi_ref[u]
    newb = newb_ref[u]
    gates = jax.nn.sigmoid(gate_ref[u])

    def new_score(q8, k_row):
        return jnp.sum(q8.astype(f32) * k_row.astype(f32), axis=-1, keepdims=True) + slope8 * float(past)

    def feat(p, slot):
        return kv_refs[p][0, slot * LANES:(slot + 1) * LANES, :]

    s_sel, s_dsa = [], []
    key_row = lax.broadcasted_iota(i32, (PAGE, nbp), 0)
    blk_col = lax.broadcasted_iota(i32, (PAGE, nbp), 1)
    kct = jnp.zeros((LANES, nbp), f32)
    vct = jnp.zeros((LANES, nbp), f32)
    for p in range(n_pages):
        pool = jnp.where(blk_col == 2 * p + key_row // BLK, 1.0, 0.0).astype(bf16)
        kct = kct + jnp.dot((feat(p, 0) * wct_ref[0]).astype(bf16), pool, preferred_element_type=f32)
        vct = vct + jnp.dot((feat(p, 1) * wct_ref[1]).astype(bf16), pool, preferred_element_type=f32)
        pos = (p * PAGE + lane).astype(f32)
        s_sel.append(jnp.dot(qn8, feat(p, 2).astype(bf16), preferred_element_type=f32) + slope8 * pos)
        s_dsa.append(jnp.dot(qd8, feat(p, 4).astype(bf16), preferred_element_type=f32) + slope8 * pos)
        ii = jnp.dot(qi8, idx_refs[p][0].astype(bf16), preferred_element_type=f32) * (IDX_DIM ** -0.5)
        isc_ref[p:p + 1, :] = jnp.sum(jnp.maximum(ii, 0.0) * wi8, axis=0, keepdims=True) * (IDX_HEADS ** -0.5)
    first_row = row8 == 0
    place = jnp.where(first_row & (lax.broadcasted_iota(i32, (8, nbp), 1) == nb - 1), 1.0, 0.0)
    tn = (((0,), (0,)), ((), ()))
    new_k = jnp.where(first_row, newb[:, 0:LANES].astype(f32) * wc_ref[0, 0:1, :], 0.0)
    new_v = jnp.where(first_row, newb[:, LANES:2 * LANES].astype(f32) * wc_ref[1, 0:1, :], 0.0)
    kct = kct + lax.dot_general(new_k.astype(bf16), place.astype(bf16), tn, preferred_element_type=f32)
    vct = vct + lax.dot_general(new_v.astype(bf16), place.astype(bf16), tn, preferred_element_type=f32)
    ii_new = jnp.sum(qi8.astype(f32) * newki_ref[u].astype(bf16).astype(f32), axis=-1, keepdims=True) * (IDX_DIM ** -0.5)
    isc_new = jnp.sum(jnp.maximum(ii_new, 0.0) * wi8, axis=0, keepdims=True) * (IDX_HEADS ** -0.5)
    isc_ref[n_pages:n_pages + 1, :] = jnp.where(lane == 0, isc_new, -jnp.inf)
    isc_ref[n_pages + 1:ISC_ROWS, :] = jnp.full((ISC_ROWS - n_pages - 1, LANES), -jnp.inf, f32)

    jb = lax.broadcasted_iota(i32, (1, nbp), 1)
    dist_c = past - (jb * BLK + (BLK - 1))
    mask_c = (dist_c >= 0) & (jb < nb)
    s_c = jnp.dot(qn8, kct.astype(bf16), preferred_element_type=f32) - slope8 * dist_c.astype(f32)
    s_c = jnp.where(mask_c, s_c, NEG_BIG)
    m_c = jnp.max(s_c, axis=-1, keepdims=True)
    e_c = jnp.where(mask_c, jnp.exp(s_c - m_c), 0.0)
    p_c = e_c * (1.0 / jnp.maximum(jnp.sum(e_c, axis=-1, keepdims=True), 1e-30))
    o_c = pick_half(_dot_nt(p_c.astype(bf16), vct.astype(bf16)))
    imp_lo = jnp.sum(jnp.where(low_group, p_c, 0.0), axis=0, keepdims=True)
    imp_hi = jnp.sum(jnp.where(low_group, 0.0, p_c), axis=0, keepdims=True)
    imp = jnp.where(low_group, imp_lo, imp_hi)
    cur = past // BLK
    imp = jnp.where((jb == 0) | (jb == cur) | (jb == cur - 1), FORCE_SCORE, imp)
    imp = jnp.where(jb <= cur, imp, -jnp.inf)
    rank = jnp.zeros((8, nbp), i32)
    for j in range(nb):
        other = imp[:, j:j + 1]
        rank = rank + ((other > imp) | ((other == imp) & (j < jb))).astype(i32)
    sel = jnp.where((rank < min(NSA_TOPN, nb)) & (imp > -jnp.inf), 1.0, 0.0)

    tot = isc_ref[...]
    tot = jnp.where(tot == 0.0, 0.0, tot)
    bits = lax.bitcast_convert_type(tot, i32)
    key = jnp.where(bits < 0, bits ^ jnp.int32(0x7FFFFFFF), bits)
    krow = lax.broadcasted_iota(i32, (ISC_ROWS, LANES), 0)
    kidx = krow * LANES + lax.broadcasted_iota(i32, (ISC_ROWS, LANES), 1)
    key = jnp.where(kidx <= past, key, INT_MIN)

    def total(x):
        return jnp.sum(jnp.sum(x, axis=1, keepdims=True), axis=0, keepdims=True)

    def count_ge(trial):
        return total((key >= trial).astype(i32))

    c_pos = count_ge(jnp.zeros((1, 1), i32))
    v_thr = jnp.where(c_pos >= topk, 0, INT_MIN).astype(i32)
    for shift, width in ((27, 4), (23, 4), (19, 4), (15, 4), (11, 4), (7, 4), (3, 4), (0, 3)):
        digit = jnp.zeros((1, 1), i32)
        for j in range(1, 1 << width):
            ok = count_ge(v_thr | jnp.int32(j << shift)) >= topk
            digit = digit + ok.astype(i32)
        v_thr = v_thr | lax.shift_left(digit, jnp.int32(shift))
    v_eff = jnp.maximum(v_thr, INT_MIN + 1)

    need = topk - count_ge(v_thr + 1)
    tie = key == v_thr
    w_last = jnp.zeros((1, 1), i32)
    for shift in (8, 4, 0):
        digit = jnp.zeros((1, 1), i32)
        for j in range(1, 16):
            below = total((tie & (kidx < (w_last | jnp.int32(j << shift)))).astype(i32))
            digit = digit + (below < need).astype(i32)
        w_last = w_last | lax.shift_left(digit, jnp.int32(shift))
    take = (key > v_thr) | (tie & (kidx <= w_last))
    dsel = jnp.where(take & (key >= v_eff), 1.0, 0.0)

    def attend(tiles, masks, s_new, new_ok, v_tile, v_new):
        tiles = [jnp.where(mk, t, NEG_BIG) for t, mk in zip(tiles, masks)]
        s_new = jnp.where(new_ok, s_new, NEG_BIG)
        m = s_new
        for t in tiles:
            m = jnp.maximum(m, jnp.max(t, axis=-1, keepdims=True))
        l = jnp.exp(s_new - m)
        acc = l * v_new.astype(f32)
        for j, t in enumerate(tiles):
            e = jnp.exp(t - m)
            l = l + jnp.sum(e, axis=-1, keepdims=True)
            acc = acc + _dot_nt(e.astype(bf16), v_tile(j))
        return pick_half(acc * (1.0 / l))

    sel_masks = [jnp.where(lane < BLK, sel[:, 2 * p:2 * p + 1], sel[:, 2 * p + 1:2 * p + 2]) > 0.0
                 for p in range(n_pages)]
    o_s = attend(s_sel, sel_masks, new_score(qn8, newb[:, 2 * LANES:3 * LANES]), sel[:, nb - 1:nb] > 0.0,
                 lambda p: feat(p, 3).astype(bf16), newb[:, 3 * LANES:4 * LANES])
    dsa_masks = [dsel[p:p + 1, :] > 0.0 for p in range(n_pages)]
    o_d = attend(s_dsa, dsa_masks, new_score(qd8, newb[:, 4 * LANES:5 * LANES]), dsel[n_pages:n_pages + 1, 0:1] > 0.0,
                 lambda p: feat(p, 5).astype(bf16), newb[:, 5 * LANES:6 * LANES])
    n_win = win_ref.shape[2]
    wlane = lax.broadcasted_iota(i32, (1, n_win), 1)
    wpos = past - n_win + wlane
    s_w = jnp.dot(qn8, win_ref[u, 0:LANES, :].astype(bf16), preferred_element_type=f32) + slope8 * wpos.astype(f32)
    w_ok = (past - wpos < WINDOW) & (wpos >= 0)
    o_w = attend([s_w], [w_ok], new_score(qn8, newb[:, 6 * LANES:7 * LANES]), True,
                 lambda p: win_ref[u, LANES:2 * LANES, :].astype(bf16), newb[:, 7 * LANES:8 * LANES])
    o_ref[u, 0:NSA_HEADS, :] = gates[:, 0:1] * o_c + gates[:, 1:2] * o_s + gates[:, 2:3] * o_w
    o_ref[u, NSA_HEADS:2 * NSA_HEADS, :] = o_d
    last = jnp.where(first_row & (lax.broadcasted_iota(i32, (8, n_win), 1) == n_win - 1), 1.0, 0.0)
    new_w = jnp.where(first_row, neww_ref[u], 0.0)
    placed = lax.dot_general(new_w, last, tn, preferred_element_type=f32, precision=hi_prec)
    wout_ref[u] = jnp.where(wlane == n_win - 1, placed, pltpu.roll(win_ref[u], n_win - 1, 1))


def _odd_attn_sample(zq, misc, kvb, win_new, cache_kv, cache_idx, cache_win, page_table, w_cmp):
    f32, bf16 = jnp.float32, jnp.bfloat16
    s_n, n_pages = page_table.shape
    n_pool = cache_kv.shape[0]
    n_win = cache_win.shape[1]
    past = n_pages * PAGE
    topk = min(DSA_TOPK_MAX, (past + 1) // 4)
    nbp = _round_up(past // BLK + 1, 8)
    wc2 = jnp.concatenate([w_cmp, w_cmp], axis=-1)
    wct = jnp.tile(jnp.swapaxes(w_cmp, 1, 2), (1, KV_GROUPS, PAGE // BLK))
    qn = zq[:, O_Q:O_QD].reshape(s_n, NSA_HEADS, LANES)
    qd = zq[:, O_QD:ZQ_QI].reshape(s_n, DSA_HEADS, LANES)
    qi = jnp.pad(zq[:, ZQ_QI:ZQ_QI + LANES].reshape(s_n, IDX_HEADS, IDX_DIM), ((0, 0), (0, 8 - IDX_HEADS), (0, 0)))
    wi = jnp.pad(misc[:, MISC_WI:MISC_WI + IDX_HEADS].reshape(s_n, IDX_HEADS, 1), ((0, 0), (0, 8 - IDX_HEADS), (0, 0)))
    gts = misc[:, MISC_GATE:MISC_GATE + 3 * NSA_HEADS].reshape(s_n, NSA_HEADS, 3)
    kv_pages = jnp.transpose(cache_kv, (0, 2, 3, 4, 1)).reshape(n_pool, KV_SLOTS * LANES, PAGE)
    idx_pages = jnp.transpose(cache_idx, (0, 2, 1))
    win_t = jnp.transpose(cache_win, (0, 2, 3, 4, 1)).reshape(s_n, 2 * LANES, n_win)
    n_u = SAMPLE_SEQS
    per_seq = lambda shape: pl.BlockSpec((n_u,) + shape, lambda b, pt: (b, 0, 0))
    page_spec = lambda rows, u, p: pl.BlockSpec((1, rows, PAGE), lambda b, pt, u=u, p=p: (pt[b * n_u + u, p], 0, 0))
    seq_pages = [(u, p) for u in range(n_u) for p in range(n_pages)]
    in_specs = ([per_seq((NSA_HEADS, LANES)), per_seq((DSA_HEADS, LANES)), per_seq((8, IDX_DIM)),
                 per_seq((8, 1)), per_seq((NSA_HEADS, 3)), per_seq((1, 8 * LANES)), per_seq((1, IDX_DIM)),
                 per_seq((1, 2 * LANES)), per_seq((2 * LANES, n_win)),
                 pl.BlockSpec((2, BLK, LANES), lambda b, pt: (0, 0, 0)),
                 pl.BlockSpec((2, LANES, PAGE), lambda b, pt: (0, 0, 0))]
                + [page_spec(KV_SLOTS * LANES, u, p) for u, p in seq_pages]
                + [page_spec(IDX_DIM, u, p) for u, p in seq_pages])
    kern = functools.partial(_odd_sample_kernel, n_pages=n_pages, topk=topk, nbp=nbp)
    o, win_out = pl.pallas_call(
        kern,
        grid_spec=pltpu.PrefetchScalarGridSpec(
            num_scalar_prefetch=1,
            grid=(s_n // n_u,),
            in_specs=in_specs,
            out_specs=[per_seq((2 * NSA_HEADS, HEAD_DIM)), per_seq((2 * LANES, n_win))],
            scratch_shapes=[pltpu.VMEM((n_u, ISC_ROWS, LANES), f32)]),
        out_shape=[jax.ShapeDtypeStruct((s_n, 2 * NSA_HEADS, HEAD_DIM), f32),
                   jax.ShapeDtypeStruct((s_n, 2 * LANES, n_win), f32)],
        compiler_params=pltpu.CompilerParams(
            dimension_semantics=("arbitrary",), vmem_limit_bytes=VMEM_LIMIT),
        name="odd_attn_sample",
    )(page_table, qn, qd, qi, wi, gts, kvb.reshape(s_n, 1, 8 * LANES), misc[:, 0:IDX_DIM].reshape(s_n, 1, IDX_DIM),
      win_new.reshape(s_n, 1, 2 * LANES), win_t, wc2, wct,
      *([kv_pages] * len(seq_pages)), *([idx_pages] * len(seq_pages)))
    win_out = jnp.transpose(win_out.reshape(s_n, 2, KV_GROUPS, HEAD_DIM, n_win), (0, 4, 1, 2, 3))
    return o.reshape(s_n, 2 * HALF_W), win_out


def _split(z, sizes):
    cuts = [int(c) for c in np.cumsum(sizes)[:-1]]
    return jnp.split(z, cuts, axis=-1)


def _pad_cols(w, n):
    return jnp.pad(w, ((0, 0), (0, n - w.shape[1])))


def kernel(x_prompt, x_sample, state_conv, state_C, state_n, state_m, cache_kv, cache_idx, cache_win, page_table,
           w_in_e, gate_b_e, conv_w, conv_b, w_out_e, w_in_o, w_cmp, w_out_o, w_up, w_down, ln_g, ln_b):
    f32, bf16 = jnp.float32, jnp.bfloat16
    bp, tp, d = x_prompt.shape
    dbs = x_sample.shape[0]
    keep = cache_win.shape[2]
    yp = x_prompt.reshape(bp * tp, d)
    ys = x_sample.reshape(dbs, d)
    tm_p, tm_s = 512, dbs
    outs = {}
    for layer in range(DEPTH):
        i = layer // 2
        if layer % 2 == 0:
            w_in = _pad_cols(w_in_e[i], E_PACKED).astype(bf16)
            w_out = w_out_e[i].astype(bf16)
            ya, qkv, og, gts, tails = _even_proj(yp, w_in, conv_w[i], conv_b[i], tm_p, tp)
            yb, c_fin, n_fin, m_fin = _mlstm_prompt(qkv, og, gts, gate_b_e[i], bp, tp)
            cvp = tails.reshape(bp, tp // tm_p, CONV_TAIL, HALF_W)[:, -1, CONV_TAIL - (CONV_W - 1):]
            ya_s, yb_s, cvs, c_s, n_s, m_s = _even_sample(_proj(ys, w_in, tm_s), gate_b_e[i], conv_w[i], conv_b[i],
                                                          state_conv[i], state_C[i], state_n[i], state_m[i])
            outs['conv'] = (cvp[None], cvs[None])
            outs['c'] = (c_fin[None], c_s[None])
            outs['n'] = (n_fin[None], n_s[None])
            outs['m'] = (m_fin[:, :, 0][None], m_s[:, :, 0][None])
            yp = _outproj2_ln(ya, yb.reshape(bp * tp, HALF_W), w_out, yp, ln_g[layer, 0], ln_b[layer, 0], tm_p)
            ys = _outproj2_ln(ya_s, yb_s, w_out, ys, ln_g[layer, 0], ln_b[layer, 0], tm_s)
        else:
            w_out = w_out_o[i].astype(bf16)
            w_packed = _pack_w_in_o(w_in_o[i]).astype(bf16)
            wc2 = jnp.concatenate([w_cmp[i], w_cmp[i]], axis=-1)
            rows_first = lambda kvt: jnp.transpose(
                kvt.reshape(kvt.shape[0], KV_SLOTS, KV_GROUPS, HEAD_DIM, kvt.shape[2]), (0, 4, 1, 2, 3))
            zq, kvt_p, kvb, winp, miscp = _odd_proj(yp, w_packed, tm_p, tp)
            mp = _odd_attn_prompt(zq, miscp, kvb, wc2, bp, tp)
            kvp = rows_first(kvt_p)
            ixp = miscp[:, :IDX_DIM].reshape(bp, tp, IDX_DIM)
            wnp = winp.reshape(bp, tp, 2, KV_GROUPS, HEAD_DIM)[:, tp - keep:]
            zq_s, kvt_s, kvb_s, win_s, misc_s = _odd_proj(ys, w_packed, tm_s, dbs)
            ms, wns = _odd_attn_sample(zq_s, misc_s, kvb_s, win_s, cache_kv[i], cache_idx[i], cache_win[i],
                                       page_table, w_cmp[i])
            kvs = rows_first(kvt_s).reshape(dbs, 1, KV_SLOTS, KV_GROUPS, HEAD_DIM)
            ixs = misc_s[:, :IDX_DIM].reshape(dbs, 1, IDX_DIM)
            outs['kv'] = (kvp[None], kvs[None])
            outs['idx'] = (ixp[None], ixs[None])
            outs['win'] = (wnp[None], wns[None])
            yp = _outproj_ln(mp, w_out, yp, ln_g[layer, 0], ln_b[layer, 0], tm_p)
            ys = _outproj_ln(ms, w_out, ys, ln_g[layer, 0], ln_b[layer, 0], tm_s)
        wu, wd = w_up[layer].astype(bf16), w_down[layer].astype(bf16)
        yp = _mlp_ln(yp, wu, wd, ln_g[layer, 1], ln_b[layer, 1], 2 * tm_p, 1024)
        ys = _mlp_ln(ys, wu, wd, ln_g[layer, 1], ln_b[layer, 1], tm_s, 1024)
    return (yp.reshape(bp, tp, d), ys.reshape(dbs, 1, d),
            outs['conv'][0], outs['conv'][1], outs['c'][0], outs['c'][1],
            outs['n'][0], outs['n'][1], outs['m'][0], outs['m'][1],
            outs['kv'][0], outs['kv'][1], outs['idx'][0], outs['idx'][1],
            outs['win'][0], outs['win'][1])
```

```python
import functools

import jax
import jax.numpy as jnp
import numpy as np
from jax import lax
from jax.experimental import pallas as pl
from jax.experimental.pallas import tpu as pltpu

D_MODEL = 1024
DEPTH = 2
HALF_W = 512
D_FF = 4096
CONV_W = 3
ML_HEADS = 4
ML_DIM = 128
HEAD_DIM = 64
KV_GROUPS = 2
NSA_HEADS = 8
DSA_HEADS = 8
BLK = 64
NSA_TOPN = 8
WINDOW = 256
IDX_HEADS = 4
IDX_DIM = 32
DSA_TOPK_MAX = 256
KV_SLOTS = 6
FORCE_SCORE = 1e4
ALPHA = (2.0 * DEPTH) ** 0.25
LN_EPS = 1e-5
O_SIZES = (512, 256, 256, 256, 24, 512, 256, 128, 32, 4)
LANES = 128
VMEM_LIMIT = 48 * 1024 * 1024


def _round_up(n, m):
    return -(-n // m) * m


def _proj_kernel(x_ref, w_ref, o_ref):
    o_ref[...] = jnp.dot(x_ref[...].astype(jnp.bfloat16), w_ref[...],
                         preferred_element_type=jnp.float32)


def _proj(x, w_bf16, tm):
    m, k = x.shape
    n = w_bf16.shape[1]
    tn = n
    for cand in (1024, 768, 512, 256, 128):
        if n % cand == 0:
            tn = cand
            break
    return pl.pallas_call(
        _proj_kernel,
        grid=(n // tn, m // tm),
        in_specs=[pl.BlockSpec((tm, k), lambda j, i: (i, 0)),
                  pl.BlockSpec((k, tn), lambda j, i: (0, j))],
        out_specs=pl.BlockSpec((tm, tn), lambda j, i: (i, j)),
        out_shape=jax.ShapeDtypeStruct((m, n), jnp.float32),
        compiler_params=pltpu.CompilerParams(
            dimension_semantics=("arbitrary", "arbitrary"), vmem_limit_bytes=VMEM_LIMIT),
        name="proj",
    )(x, w_bf16)


def _layer_norm_rows(v, g, b):
    mu = jnp.mean(v, axis=-1, keepdims=True)
    d = v - mu
    var = jnp.mean(d * d, axis=-1, keepdims=True)
    return d * lax.rsqrt(var + LN_EPS) * g + b


def _outproj_ln_kernel(y_ref, w_ref, x_ref, g_ref, b_ref, o_ref):
    mix = jnp.dot(y_ref[...].astype(jnp.bfloat16), w_ref[...], preferred_element_type=jnp.float32)
    o_ref[...] = _layer_norm_rows(ALPHA * x_ref[...] + mix, g_ref[...], b_ref[...])


def _outproj_ln(y, w_bf16, x, g, b, tm):
    m, d = x.shape
    k = y.shape[1]
    return pl.pallas_call(
        _outproj_ln_kernel,
        grid=(m // tm,),
        in_specs=[pl.BlockSpec((tm, k), lambda i: (i, 0)),
                  pl.BlockSpec((k, d), lambda i: (0, 0)),
                  pl.BlockSpec((tm, d), lambda i: (i, 0)),
                  pl.BlockSpec((1, d), lambda i: (0, 0)),
                  pl.BlockSpec((1, d), lambda i: (0, 0))],
        out_specs=pl.BlockSpec((tm, d), lambda i: (i, 0)),
        out_shape=jax.ShapeDtypeStruct((m, d), jnp.float32),
        compiler_params=pltpu.CompilerParams(
            dimension_semantics=("arbitrary",), vmem_limit_bytes=VMEM_LIMIT),
        name="outproj_ln",
    )(y, w_bf16, x, g.reshape(1, d), b.reshape(1, d))


def _mlp_ln_kernel(x_ref, wu_ref, wd_ref, g_ref, b_ref, o_ref, acc_ref):
    f = pl.program_id(1)

    @pl.when(f == 0)
    def _():
        acc_ref[...] = jnp.zeros_like(acc_ref)

    up = jnp.dot(x_ref[...].astype(jnp.bfloat16), wu_ref[...], preferred_element_type=jnp.float32)
    act = jnp.square(jnp.maximum(up, 0.0))
    acc_ref[...] += jnp.dot(act.astype(jnp.bfloat16), wd_ref[...], preferred_element_type=jnp.float32)

    @pl.when(f == pl.num_programs(1) - 1)
    def _():
        o_ref[...] = _layer_norm_rows(ALPHA * x_ref[...] + acc_ref[...], g_ref[...], b_ref[...])


def _mlp_ln(x, wu_bf16, wd_bf16, g, b, tm, tf):
    m, d = x.shape
    ff = wu_bf16.shape[1]
    return pl.pallas_call(
        _mlp_ln_kernel,
        grid=(m // tm, ff // tf),
        in_specs=[pl.BlockSpec((tm, d), lambda i, f: (i, 0)),
                  pl.BlockSpec((d, tf), lambda i, f: (0, f)),
                  pl.BlockSpec((tf, d), lambda i, f: (f, 0)),
                  pl.BlockSpec((1, d), lambda i, f: (0, 0)),
                  pl.BlockSpec((1, d), lambda i, f: (0, 0))],
        out_specs=pl.BlockSpec((tm, d), lambda i, f: (i, 0)),
        out_shape=jax.ShapeDtypeStruct((m, d), jnp.float32),
        scratch_shapes=[pltpu.VMEM((tm, d), jnp.float32)],
        compiler_params=pltpu.CompilerParams(
            dimension_semantics=("arbitrary", "arbitrary"), vmem_limit_bytes=VMEM_LIMIT),
        name="mlp_ln",
    )(x, wu_bf16, wd_bf16, g.reshape(1, d), b.reshape(1, d))


E_QKV = 3 * HALF_W
E_OG = 6 * HALF_W
E_GATE = 7 * HALF_W
E_PACKED = 7 * HALF_W + LANES
CONV_TAIL = 8


def _even_proj_kernel(x_ref, w_ref, cw_ref, cb_ref, ya_ref, qkv_ref, og_ref, gate_ref, tail_ref, carry_ref,
                      *, tiles_per_seq):
    f32, bf16 = jnp.float32, jnp.bfloat16
    tm = x_ref.shape[0]

    @pl.when(pl.program_id(0) % tiles_per_seq == 0)
    def _():
        carry_ref[...] = jnp.zeros_like(carry_ref)

    z = jnp.dot(x_ref[...].astype(bf16), w_ref[...], preferred_element_type=f32)
    u = z[:, 2 * HALF_W:3 * HALF_W] * z[:, 0:HALF_W]
    prev = carry_ref[...]
    row = lax.broadcasted_iota(jnp.int32, (tm, 1), 0)
    conv = cb_ref[...] + cw_ref[CONV_W - 1:CONV_W, :] * u
    for back in range(1, CONV_W):
        shifted = pltpu.roll(u, back, 0)
        for r in range(back):
            shifted = jnp.where(row == r, prev[CONV_TAIL - back + r:CONV_TAIL - back + r + 1, :], shifted)
        conv = conv + cw_ref[CONV_W - 1 - back:CONV_W - back, :] * shifted
    ya_ref[...] = (z[:, HALF_W:2 * HALF_W] * conv).astype(bf16)
    tail = u[tm - CONV_TAIL:]
    carry_ref[...] = tail
    tail_ref[0] = tail
    qkv_ref[:, 0:HALF_W] = z[:, E_QKV:E_QKV + HALF_W].astype(bf16)
    qkv_ref[:, HALF_W:2 * HALF_W] = (z[:, E_QKV + HALF_W:E_QKV + 2 * HALF_W] * (ML_DIM ** -0.5)).astype(bf16)
    qkv_ref[:, 2 * HALF_W:3 * HALF_W] = z[:, E_QKV + 2 * HALF_W:E_OG].astype(bf16)
    og_ref[...] = z[:, E_OG:E_GATE]
    gate_ref[...] = z[:, E_GATE:E_PACKED]


def _even_proj(x, w_bf16, conv_w, conv_b, tm, seq_len):
    m, k = x.shape
    widths = (HALF_W, 3 * HALF_W, HALF_W, LANES)
    dtypes = (jnp.bfloat16, jnp.bfloat16, jnp.float32, jnp.float32)
    kern = functools.partial(_even_proj_kernel, tiles_per_seq=seq_len // tm)
    return pl.pallas_call(
        kern,
        grid=(m // tm,),
        in_specs=[pl.BlockSpec((tm, k), lambda i: (i, 0)),
                  pl.BlockSpec((k, E_PACKED), lambda i: (0, 0)),
                  pl.BlockSpec((CONV_W, HALF_W), lambda i: (0, 0)),
                  pl.BlockSpec((1, HALF_W), lambda i: (0, 0))],
        out_specs=[pl.BlockSpec((tm, n), lambda i: (i, 0)) for n in widths]
        + [pl.BlockSpec((1, CONV_TAIL, HALF_W), lambda i: (i, 0, 0))],
        out_shape=[jax.ShapeDtypeStruct((m, n), dt) for n, dt in zip(widths, dtypes)]
        + [jax.ShapeDtypeStruct((m // tm, CONV_TAIL, HALF_W), jnp.float32)],
        scratch_shapes=[pltpu.VMEM((CONV_TAIL, HALF_W), jnp.float32)],
        compiler_params=pltpu.CompilerParams(
            dimension_semantics=("arbitrary",), vmem_limit_bytes=VMEM_LIMIT),
        name="even_proj",
    )(x, w_bf16, conv_w, conv_b.reshape(1, HALF_W))


def _mlstm_kernel(qkv_ref, og_ref, gate_ref, gb_ref, yb_ref, c_ref, n_ref, m_ref):
    f32, bf16 = jnp.float32, jnp.bfloat16
    L = qkv_ref.shape[1]

    @pl.when(pl.program_id(1) == 0)
    def _():
        c_ref[...] = jnp.zeros_like(c_ref)
        n_ref[...] = jnp.zeros_like(n_ref)
        m_ref[...] = jnp.zeros_like(m_ref)

    pre = gate_ref[0] + gb_ref[...]
    lf = jax.nn.log_sigmoid(pre)
    row = lax.broadcasted_iota(jnp.int32, (L, L), 0)
    col = lax.broadcasted_iota(jnp.int32, (L, L), 1)
    causal = col <= row
    tri = jnp.where(causal, 1.0, 0.0).astype(f32)
    b_all = jnp.dot(tri, lf, preferred_element_type=f32, precision=lax.Precision.HIGHEST)
    pre_t = pre.T
    b_t = b_all.T
    for h in range(ML_HEADS):
        q = qkv_ref[0, :, h * ML_DIM:(h + 1) * ML_DIM]
        k = qkv_ref[0, :, HALF_W + h * ML_DIM:HALF_W + (h + 1) * ML_DIM]
        v = qkv_ref[0, :, 2 * HALF_W + h * ML_DIM:2 * HALF_W + (h + 1) * ML_DIM]
        ig_col = pre[:, h:h + 1]
        b_col = b_all[:, ML_HEADS + h:ML_HEADS + h + 1]
        a_row = pre_t[h:h + 1, :] - b_t[ML_HEADS + h:ML_HEADS + h + 1, :]
        m_st = m_ref[0, h:h + 1, 0:1]
        c_st = c_ref[0, h]
        n_st = n_ref[0, h:h + 1, :]
        dmat = jnp.where(causal, b_col + a_row, -jnp.inf)
        inter = b_col + m_st
        m_t = jnp.maximum(inter, jnp.max(dmat, axis=-1, keepdims=True))
        w_intra = jnp.exp(dmat - m_t)
        w_inter = jnp.exp(inter - m_t)
        s = _dot_nt(q, k) * w_intra
        num = jnp.dot(s.astype(bf16), v, preferred_element_type=f32) + w_inter * _dot_nt(q, c_st.astype(bf16))
        den = jnp.sum(s, axis=-1, keepdims=True) + w_inter * jnp.sum(q.astype(f32) * n_st, axis=-1, keepdims=True)
        hs = num * (1.0 / jnp.maximum(jnp.abs(den), jnp.exp(-m_t)))
        m_new = m_t[L - 1:L, :]
        b_last = b_col[L - 1:L, :]
        w_state = jnp.exp(b_last - b_col + ig_col - m_new)
        decay = jnp.exp(b_last + m_st - m_new)
        vw = (v.astype(f32) * w_state).astype(bf16)
        c_ref[0, h] = decay * c_st + lax.dot_general(vw, k, (((0,), (0,)), ((), ())), preferred_element_type=f32)
        n_ref[0, h:h + 1, :] = decay * n_st + jnp.sum(k.astype(f32) * w_state, axis=0, keepdims=True)
        m_ref[0, h:h + 1, :] = jnp.broadcast_to(m_new, (1, LANES))
        og = og_ref[0, :, h * ML_DIM:(h + 1) * ML_DIM]
        yb_ref[0, :, h * ML_DIM:(h + 1) * ML_DIM] = (jax.nn.sigmoid(og) * hs).astype(bf16)


def _mlstm_prompt(qkv, og, gates, gate_b, bsz, t_len, chunk=128):
    gb = jnp.pad(gate_b, (0, LANES - gate_b.shape[0])).reshape(1, LANES)
    return pl.pallas_call(
        _mlstm_kernel,
        grid=(bsz, t_len // chunk),
        in_specs=[pl.BlockSpec((1, chunk, 3 * HALF_W), lambda b, c: (b, c, 0)),
                  pl.BlockSpec((1, chunk, HALF_W), lambda b, c: (b, c, 0)),
                  pl.BlockSpec((1, chunk, LANES), lambda b, c: (b, c, 0)),
                  pl.BlockSpec((1, LANES), lambda b, c: (0, 0))],
        out_specs=[pl.BlockSpec((1, chunk, HALF_W), lambda b, c: (b, c, 0)),
                   pl.BlockSpec((1, ML_HEADS, ML_DIM, ML_DIM), lambda b, c: (b, 0, 0, 0)),
                   pl.BlockSpec((1, ML_HEADS, ML_DIM), lambda b, c: (b, 0, 0)),
                   pl.BlockSpec((1, ML_HEADS, LANES), lambda b, c: (b, 0, 0))],
        out_shape=[jax.ShapeDtypeStruct((bsz, t_len, HALF_W), jnp.bfloat16),
                   jax.ShapeDtypeStruct((bsz, ML_HEADS, ML_DIM, ML_DIM), jnp.float32),
                   jax.ShapeDtypeStruct((bsz, ML_HEADS, ML_DIM), jnp.float32),
                   jax.ShapeDtypeStruct((bsz, ML_HEADS, LANES), jnp.float32)],
        compiler_params=pltpu.CompilerParams(
            dimension_semantics=("arbitrary", "arbitrary"), vmem_limit_bytes=VMEM_LIMIT),
        name="mlstm_prompt",
    )(qkv.reshape(bsz, t_len, 3 * HALF_W), og.reshape(bsz, t_len, HALF_W),
      gates.reshape(bsz, t_len, LANES), gb)


def _outproj2_ln_kernel(ya_ref, yb_ref, w_ref, x_ref, g_ref, b_ref, o_ref):
    mix = jnp.dot(ya_ref[...], w_ref[0:HALF_W, :], preferred_element_type=jnp.float32)
    mix = mix + jnp.dot(yb_ref[...], w_ref[HALF_W:, :], preferred_element_type=jnp.float32)
    o_ref[...] = _layer_norm_rows(ALPHA * x_ref[...] + mix, g_ref[...], b_ref[...])


def _outproj2_ln(ya, yb, w_bf16, x, g, b, tm):
    m, d = x.shape
    return pl.pallas_call(
        _outproj2_ln_kernel,
        grid=(m // tm,),
        in_specs=[pl.BlockSpec((tm, HALF_W), lambda i: (i, 0)),
                  pl.BlockSpec((tm, HALF_W), lambda i: (i, 0)),
                  pl.BlockSpec((2 * HALF_W, d), lambda i: (0, 0)),
                  pl.BlockSpec((tm, d), lambda i: (i, 0)),
                  pl.BlockSpec((1, d), lambda i: (0, 0)),
                  pl.BlockSpec((1, d), lambda i: (0, 0))],
        out_specs=pl.BlockSpec((tm, d), lambda i: (i, 0)),
        out_shape=jax.ShapeDtypeStruct((m, d), jnp.float32),
        compiler_params=pltpu.CompilerParams(
            dimension_semantics=("arbitrary",), vmem_limit_bytes=VMEM_LIMIT),
        name="outproj2_ln",
    )(ya, yb, w_bf16, x, g.reshape(1, d), b.reshape(1, d))


SAMPLE_ROWS = 8


def _even_sample_kernel(h_ref, bg_ref, cg_ref, q_ref, k_ref, v_ref, og_ref, gate_ref, gb_ref, cw_ref, cb_ref,
                        conv_ref, c_ref, n_ref, m_ref, ya_ref, yb_ref, conv_out_ref, c_out_ref, n_out_ref, m_out_ref):
    f32, bf16 = jnp.float32, jnp.bfloat16
    rows = h_ref.shape[0]
    row = lax.broadcasted_iota(jnp.int32, (rows, 1), 0)
    tn = (((0,), (0,)), ((), ()))
    u = cg_ref[...] * h_ref[...]
    prev0, prev1 = conv_ref[:, 0, :], conv_ref[:, 1, :]
    conv = cb_ref[...] + cw_ref[0:1, :] * prev0 + cw_ref[1:2, :] * prev1 + cw_ref[2:3, :] * u
    ya_ref[...] = (bg_ref[...] * conv).astype(bf16)
    conv_out_ref[:, 0, :] = prev1
    conv_out_ref[:, 1, :] = u
    pre = gate_ref[...] + gb_ref[...]
    lf_all = jax.nn.log_sigmoid(pre)
    for h in range(ML_HEADS):
        cols = slice(h * ML_DIM, (h + 1) * ML_DIM)
        q = q_ref[:, cols]
        k = k_ref[:, cols] * (ML_DIM ** -0.5)
        v = v_ref[:, cols]
        ig = pre[:, h:h + 1]
        lf = lf_all[:, ML_HEADS + h:ML_HEADS + h + 1]
        m_st = m_ref[:, h:h + 1]
        n_st = n_ref[:, h, :]
        inter = lf + m_st
        m_t = jnp.maximum(inter, ig)
        w_intra = jnp.exp(ig - m_t)
        w_inter = jnp.exp(inter - m_t)
        qb, kb = q.astype(bf16), k.astype(bf16)
        s = jnp.sum(qb.astype(f32) * kb.astype(f32), axis=-1, keepdims=True) * w_intra
        cq = jnp.zeros((rows, ML_DIM), f32)
        for b in range(rows):
            c_b = c_ref[b, h]
            cq = jnp.where(row == b, _dot_nt(qb, c_b.astype(bf16)), cq)
            vw = jnp.where(row == b, v * w_intra, 0.0)
            outer = lax.dot_general(vw, k, tn, preferred_element_type=f32, precision=lax.Precision.HIGHEST)
            c_out_ref[b, h] = w_inter[b:b + 1, :] * c_b + outer
        num = s * v + w_inter * cq
        den = s + w_inter * jnp.sum(n_st * q, axis=-1, keepdims=True)
        hs = num * (1.0 / jnp.maximum(jnp.abs(den), jnp.exp(-m_t)))
        n_out_ref[:, h, :] = w_inter * n_st + w_intra * k
        m_out_ref[:, h, :] = jnp.broadcast_to(m_t, (rows, LANES))
        yb_ref[:, cols] = (jax.nn.sigmoid(og_ref[:, cols]) * hs).astype(bf16)


def _even_sample(z, gate_b, conv_w, conv_b, state_conv, state_c, state_n, state_m):
    s_n = z.shape[0]
    f32 = jnp.float32
    r = SAMPLE_ROWS
    gb = jnp.pad(gate_b, (0, LANES - gate_b.shape[0])).reshape(1, LANES)
    col = lambda j: pl.BlockSpec((r, HALF_W), lambda i, j=j: (i, j))
    in_specs = [col(j) for j in range(7)] + [
        pl.BlockSpec((r, LANES), lambda i: (i, E_GATE // LANES)),
        pl.BlockSpec((1, LANES), lambda i: (0, 0)),
        pl.BlockSpec((CONV_W, HALF_W), lambda i: (0, 0)),
        pl.BlockSpec((1, HALF_W), lambda i: (0, 0)),
        pl.BlockSpec((r, CONV_W - 1, HALF_W), lambda i: (i, 0, 0)),
        pl.BlockSpec((r, ML_HEADS, ML_DIM, ML_DIM), lambda i: (i, 0, 0, 0)),
        pl.BlockSpec((r, ML_HEADS, ML_DIM), lambda i: (i, 0, 0)),
        pl.BlockSpec((r, ML_HEADS), lambda i: (i, 0))]
    out_specs = [pl.BlockSpec((r, HALF_W), lambda i: (i, 0)),
                 pl.BlockSpec((r, HALF_W), lambda i: (i, 0)),
                 pl.BlockSpec((r, CONV_W - 1, HALF_W), lambda i: (i, 0, 0)),
                 pl.BlockSpec((r, ML_HEADS, ML_DIM, ML_DIM), lambda i: (i, 0, 0, 0)),
                 pl.BlockSpec((r, ML_HEADS, ML_DIM), lambda i: (i, 0, 0)),
                 pl.BlockSpec((r, ML_HEADS, LANES), lambda i: (i, 0, 0))]
    out_shape = [jax.ShapeDtypeStruct((s_n, HALF_W), jnp.bfloat16),
                 jax.ShapeDtypeStruct((s_n, HALF_W), jnp.bfloat16),
                 jax.ShapeDtypeStruct((s_n, CONV_W - 1, HALF_W), f32),
                 jax.ShapeDtypeStruct((s_n, ML_HEADS, ML_DIM, ML_DIM), f32),
                 jax.ShapeDtypeStruct((s_n, ML_HEADS, ML_DIM), f32),
                 jax.ShapeDtypeStruct((s_n, ML_HEADS, LANES), f32)]
    return pl.pallas_call(
        _even_sample_kernel,
        grid=(s_n // r,),
        in_specs=in_specs,
        out_specs=out_specs,
        out_shape=out_shape,
        compiler_params=pltpu.CompilerParams(
            dimension_semantics=("arbitrary",), vmem_limit_bytes=VMEM_LIMIT),
        name="even_sample",
    )(z, z, z, z, z, z, z, z, gb, conv_w, conv_b.reshape(1, HALF_W), state_conv, state_c, state_n, state_m)


O_Q = 0
O_QD = NSA_HEADS * LANES
O_KV = 2048
O_WIN = 2816
O_QI = 3072
O_MISC = 3200
O_PACKED = 3328
ZQ_QI = 2048
MISC_WI = IDX_DIM
MISC_GATE = IDX_DIM + IDX_HEADS
NEG_BIG = -(2.0 ** 100)
INT_MIN = -(2 ** 31)
ALIBI = tuple(float(2.0 ** (-8.0 * (h + 1) / NSA_HEADS)) for h in range(NSA_HEADS))


def _pack_w_in_o(w):
    qn, kvc, kvs, kvw, gates, qd, kvd, qi, ki, wi = _split(w, O_SIZES)
    scale = HEAD_DIM ** -0.5
    rows = w.shape[0]

    def spread(q):
        q = (q * scale).reshape(rows, NSA_HEADS, HEAD_DIM)
        z = jnp.zeros_like(q)
        low = (jnp.arange(NSA_HEADS) < NSA_HEADS // KV_GROUPS)[None, :, None]
        return jnp.concatenate([jnp.where(low, q, z), jnp.where(low, z, q)], axis=-1).reshape(rows, NSA_HEADS * LANES)

    pad = jnp.zeros((rows, O_PACKED - O_MISC - IDX_DIM - IDX_HEADS - 3 * NSA_HEADS), w.dtype)
    return jnp.concatenate([spread(qn), spread(qd), kvc, kvs, kvd, kvw, qi, ki, wi, gates, pad], axis=1)


def _odd_proj_kernel(x_ref, w_ref, wkvt_ref, zq_ref, kvt_ref, kvb_ref, win_ref, misc_ref):
    xb = x_ref[...].astype(jnp.bfloat16)
    z = jnp.dot(xb, w_ref[...], preferred_element_type=jnp.float32)
    zq_ref[:, 0:O_KV] = z[:, 0:O_KV].astype(jnp.bfloat16)
    zq_ref[:, ZQ_QI:ZQ_QI + LANES] = z[:, O_QI:O_MISC].astype(jnp.bfloat16)
    kvt_ref[0] = _dot_nt(wkvt_ref[...], xb)
    kvb_ref[...] = z[:, O_KV:O_QI].astype(jnp.bfloat16)
    win_ref[...] = z[:, O_WIN:O_QI]
    misc_ref[...] = z[:, O_MISC:O_PACKED]


def _odd_proj(x, w_packed_bf16, tm, seq_len):
    m, k = x.shape
    n_kv = O_WIN - O_KV
    widths = (O_KV + LANES, O_QI - O_KV, O_QI - O_WIN, LANES)
    dtypes = (jnp.bfloat16, jnp.bfloat16, jnp.float32, jnp.float32)
    row_spec = lambda n: pl.BlockSpec((tm, n), lambda i: (i, 0))
    tiles = seq_len // tm
    out_specs = [row_spec(widths[0]), pl.BlockSpec((1, n_kv, tm), lambda i: (i // tiles, 0, i % tiles))]
    out_specs += [row_spec(n) for n in widths[1:]]
    out_shape = [jax.ShapeDtypeStruct((m, widths[0]), dtypes[0]),
                 jax.ShapeDtypeStruct((m // seq_len, n_kv, seq_len), jnp.float32)]
    out_shape += [jax.ShapeDtypeStruct((m, n), dt) for n, dt in zip(widths[1:], dtypes[1:])]
    return pl.pallas_call(
        _odd_proj_kernel,
        grid=(m // tm,),
        in_specs=[pl.BlockSpec((tm, k), lambda i: (i, 0)),
                  pl.BlockSpec((k, O_PACKED), lambda i: (0, 0)),
                  pl.BlockSpec((n_kv, k), lambda i: (0, 0))],
        out_specs=out_specs,
        out_shape=out_shape,
        compiler_params=pltpu.CompilerParams(
            dimension_semantics=("arbitrary",), vmem_limit_bytes=VMEM_LIMIT),
        name="odd_proj",
    )(x, w_packed_bf16, w_packed_bf16[:, O_KV:O_WIN].T)


def _dot_nt(a, b):
    return lax.dot_general(a, b, (((1,), (1,)), ((), ())), preferred_element_type=jnp.float32)


def _flash_heads(q_heads, slopes, group, kv_ref, k_blk, v_blk, tk, lo, hi, bias_fn, skip_fn=None, start_fn=None):
    f32, bf16 = jnp.float32, jnp.bfloat16
    if start_fn is None:
        start_fn = lambda kt: kt * tk
    tq = q_heads[0].shape[0]
    n_heads = len(q_heads)
    lane = lax.broadcasted_iota(jnp.int32, (1, LANES), 1)
    own_half = (lane // HEAD_DIM) == group
    q_all = jnp.concatenate(q_heads, axis=0)
    slope_col = jnp.concatenate([jnp.full((tq, 1), sl, f32) for sl in slopes], axis=0)

    def tile(kt, carry):
        m, acc = carry
        start = pl.multiple_of(start_fn(kt), LANES)
        rows = pl.ds(start, tk)
        k = kv_ref[0, rows, k_blk * LANES:(k_blk + 1) * LANES]
        v = kv_ref[0, rows, v_blk * LANES:(v_blk + 1) * LANES]
        v = jnp.where(own_half, v, jnp.ones_like(v))
        sp = (start + lax.broadcasted_iota(jnp.int32, (1, tk), 1)).astype(f32)
        b = bias_fn(kt)
        s = _dot_nt(q_all, k) + slope_col * sp + jnp.concatenate([b] * n_heads, axis=0)
        m_new = jnp.maximum(m, jnp.max(s, axis=-1, keepdims=True))
        p = jnp.exp(s - m_new)
        acc = jnp.exp(m - m_new) * acc + jnp.dot(p.astype(bf16), v, preferred_element_type=f32)
        return m_new, acc

    def body(kt, carry):
        if skip_fn is None:
            return tile(kt, carry)
        return lax.cond(skip_fn(kt), lambda c: c, functools.partial(tile, kt), carry)

    m0 = jnp.full((n_heads * tq, 1), -jnp.inf, f32)
    a0 = jnp.zeros((n_heads * tq, LANES), f32)
    _, acc = lax.fori_loop(lo, hi, body, (m0, a0))
    return [acc[r * tq:(r + 1) * tq] for r in range(n_heads)]


def _odd_attn_kernel(zq_ref, miscq_ref, kvb_ref, misck_ref, wc_ref, o_ref,
                     kc_ref, vc_ref, ki_ref, key_ref, dbias_ref, flag_ref, *, t_len, tq, tk, topk):
    f32, bf16, i32 = jnp.float32, jnp.bfloat16, jnp.int32
    nb = t_len // BLK
    qi_blk = pl.program_id(1)
    q0 = qi_blk * tq

    @pl.when(qi_blk == 0)
    def _():
        ck = kvb_ref[0, :, 0:LANES].astype(f32).reshape(nb, BLK, LANES)
        kc_ref[...] = jnp.sum(ck * wc_ref[0][None], axis=1).astype(bf16)
        cv = kvb_ref[0, :, LANES:2 * LANES].astype(f32).reshape(nb, BLK, LANES)
        vc = jnp.sum(cv * wc_ref[1][None], axis=1)
        for g in range(KV_GROUPS):
            half = vc[:, g * HEAD_DIM:(g + 1) * HEAD_DIM]
            vc_ref[g] = jnp.concatenate([half, half], axis=1).astype(bf16)
        ki_ref[...] = misck_ref[0, :, 0:IDX_DIM].astype(bf16)

    misc = miscq_ref[0]
    gates = jax.nn.sigmoid(misc[:, MISC_GATE:MISC_GATE + 3 * NSA_HEADS])
    t_col = q0 + lax.broadcasted_iota(i32, (tq, 1), 0)
    lane = lax.broadcasted_iota(i32, (1, LANES), 1)
    low_half = lane < HEAD_DIM
    heads_per_group = NSA_HEADS // KV_GROUPS

    def q_head(base, h):
        return zq_ref[0, :, base + h * LANES: base + (h + 1) * LANES]

    def pair(even, odd):
        return jnp.where(low_half, even, odd)

    def normalised_pair(acc_even, acc_odd, g):
        outs = []
        for parity, acc in enumerate((acc_even, acc_odd)):
            swapped = pltpu.roll(acc, HEAD_DIM, 1)
            outs.append(acc * (1.0 / swapped) if parity == g else swapped * (1.0 / acc))
        return pair(outs[0], outs[1])

    def gate_pair(j, c):
        a = gates[:, (2 * j) * 3 + c:(2 * j) * 3 + c + 1]
        b = gates[:, (2 * j + 1) * 3 + c:(2 * j + 1) * 3 + c + 1]
        return jnp.where(low_half, a, b)

    tn = (((0,), (0,)), ((), ()))
    t_lanes = q0 + lax.broadcasted_iota(i32, (1, tq), 1)
    jb = lax.broadcasted_iota(i32, (nb, 1), 0)
    dist_c = t_lanes - (jb * BLK + (BLK - 1))
    mask_c = dist_c >= 0
    dist_cf = dist_c.astype(f32)
    cur = t_lanes // BLK
    forced = (jb == 0) | (jb == cur) | (jb == cur - 1)
    admissible = jb <= cur
    jb_full = lax.broadcasted_iota(i32, (nb, tq), 0)
    o_cmp, sel_bias, sel_any = [], [], []
    for g in range(KV_GROUPS):
        imp = jnp.zeros((nb, tq), f32)
        for r in range(heads_per_group):
            h = g * heads_per_group + r
            s = _dot_nt(kc_ref[...], q_head(O_Q, h)) - ALIBI[h] * dist_cf
            s = jnp.where(mask_c, s, NEG_BIG)
            m = jnp.max(s, axis=0, keepdims=True)
            e = jnp.where(mask_c, jnp.exp(s - m), 0.0)
            p = e * (1.0 / jnp.maximum(jnp.sum(e, axis=0, keepdims=True), 1e-30))
            imp = imp + p
            o_cmp.append(lax.dot_general(p.astype(bf16), vc_ref[g], tn, preferred_element_type=f32))
        imp = jnp.where(forced, FORCE_SCORE, imp)
        imp = jnp.where(admissible, imp, -jnp.inf)
        sel = jnp.zeros((nb, tq), f32)
        for _ in range(min(NSA_TOPN, nb)):
            m = jnp.max(imp, axis=0, keepdims=True)
            first = jnp.min(jnp.where(imp == m, jb_full, nb), axis=0, keepdims=True)
            hit = jb_full == first
            sel = jnp.where(hit & (m > -jnp.inf), 1.0, sel)
            imp = jnp.where(hit, -jnp.inf, imp)
        sel_bias.append(jnp.where(sel > 0.0, 0.0, NEG_BIG).astype(bf16))
        sel_any.append(jnp.max(sel, axis=1, keepdims=True))

    n_sel_tiles = (q0 + tq + tk - 1) // tk
    blocks_per_tile = tk // BLK

    for g in range(KV_GROUPS):
        for j in range(nb // blocks_per_tile):
            hit = jnp.max(sel_any[g][j * blocks_per_tile:(j + 1) * blocks_per_tile, :])
            flag_ref[g, j] = (hit > 0.0).astype(i32)

    def sel_skip_fn(g):
        return lambda kt: flag_ref[g, kt] == 0

    def causal_bias(start, width):
        sp = start + lax.broadcasted_iota(i32, (1, width), 1)
        return jnp.where(sp <= t_col, 0.0, NEG_BIG)

    def sel_bias_fn(g):
        def fn(kt):
            row = lax.broadcasted_iota(i32, (nb, tk), 0)
            col = lax.broadcasted_iota(i32, (nb, tk), 1)
            expand = jnp.where(row == kt * blocks_per_tile + col // BLK, 1.0, 0.0).astype(bf16)
            spread = lax.dot_general(sel_bias[g], expand, tn, preferred_element_type=f32)
            return spread + causal_bias(kt * tk, tk)
        return fn

    tkw = min(WINDOW + tq, t_len)
    win_start = jnp.clip(q0 - WINDOW, 0, t_len - tkw)

    def win_bias_fn(kt):
        dist = t_col - (win_start + lax.broadcasted_iota(i32, (1, tkw), 1))
        return jnp.where((dist >= 0) & (dist < WINDOW), 0.0, NEG_BIG)

    qi_heads = [zq_ref[0, :, ZQ_QI + h * IDX_DIM: ZQ_QI + (h + 1) * IDX_DIM] for h in range(IDX_HEADS)]
    pick = jnp.where(lax.broadcasted_iota(i32, (8, LANES), 1) == MISC_WI + lax.broadcasted_iota(i32, (8, LANES), 0),
                     1.0, 0.0)
    wi_rows = lax.dot_general(pick, misc, (((1,), (1,)), ((), ())), preferred_element_type=f32,
                              precision=lax.Precision.HIGHEST)
    t_row = q0 + lax.broadcasted_iota(i32, (1, tq), 1)

    def fold8(x):
        x = x.reshape(tk // 8, 8, tq)
        while x.shape[0] > 1:
            half = x.shape[0] // 2
            x = x[:half] + x[half:]
        return x[0]

    def index_tile(kt, carry):
        start = pl.multiple_of(kt * tk, tk)
        kik = ki_ref[pl.ds(start, tk), :]
        tot = jnp.zeros((tk, tq), f32)
        for h in range(IDX_HEADS):
            sc = _dot_nt(kik, qi_heads[h]) * (IDX_DIM ** -0.5)
            tot = tot + jnp.maximum(sc, 0.0) * wi_rows[h:h + 1, :]
        tot = tot * (IDX_HEADS ** -0.5)
        tot = jnp.where(tot == 0.0, 0.0, tot)
        bits = lax.bitcast_convert_type(tot, i32)
        key = jnp.where(bits < 0, bits ^ jnp.int32(0x7FFFFFFF), bits)
        sp = start + lax.broadcasted_iota(i32, (tk, 1), 0)
        key_ref[kt] = jnp.where(sp <= t_row, key, INT_MIN)
        return carry

    lax.fori_loop(0, n_sel_tiles, index_tile, 0)

    def count_ge(trial):
        def body(kt, acc):
            return acc + fold8((key_ref[kt] >= trial).astype(i32))
        acc = lax.fori_loop(0, n_sel_tiles, body, jnp.zeros((8, tq), i32))
        return jnp.sum(acc, axis=0, keepdims=True)

    c_pos = count_ge(jnp.zeros((1, tq), i32))
    v0 = jnp.where(c_pos >= topk, 0, INT_MIN).astype(i32)
    c0 = jnp.where(c_pos >= topk, c_pos, n_sel_tiles * tk)

    def bit_step(i, carry):
        v, cge = carry
        trial = v | lax.shift_left(jnp.int32(1), 30 - i)
        c = count_ge(trial)
        ok = c >= topk
        return jnp.where(ok, trial, v), jnp.where(ok, c, cge)

    v_thr, c_ge = lax.fori_loop(0, 31, bit_step, (v0, c0))
    v_eff = jnp.maximum(v_thr, INT_MIN + 1)

    def write_plain(kt, carry):
        dbias_ref[kt] = jnp.where(key_ref[kt] >= v_eff, 0.0, NEG_BIG).T
        return carry

    def plain_path():
        lax.fori_loop(0, n_sel_tiles, write_plain, 0)

    def tie_path():
        need = topk - count_ge(v_thr + 1)

        def count_tie_below(limit):
            def body(kt, acc):
                sp = kt * tk + lax.broadcasted_iota(i32, (tk, 1), 0)
                return acc + fold8(((key_ref[kt] == v_thr) & (sp < limit)).astype(i32))
            acc = lax.fori_loop(0, n_sel_tiles, body, jnp.zeros((8, tq), i32))
            return jnp.sum(acc, axis=0, keepdims=True)

        n_bits = max(1, (t_len - 1).bit_length())

        def idx_step(i, w):
            trial = w | lax.shift_left(jnp.int32(1), n_bits - 1 - i)
            return jnp.where(count_tie_below(trial) < need, trial, w)

        w_last = lax.fori_loop(0, n_bits, idx_step, jnp.zeros((1, tq), i32))

        def write_tie(kt, carry):
            sp = kt * tk + lax.broadcasted_iota(i32, (tk, 1), 0)
            key = key_ref[kt]
            take = (key > v_thr) | ((key == v_thr) & (sp <= w_last))
            dbias_ref[kt] = jnp.where(take & (key >= v_eff), 0.0, NEG_BIG).T
            return carry

        lax.fori_loop(0, n_sel_tiles, write_tie, 0)

    lax.cond(jnp.max(c_ge) > topk, tie_path, plain_path)

    def dsa_bias_fn(kt):
        return dbias_ref[kt]

    for g in range(KV_GROUPS):
        hs = [g * heads_per_group + r for r in range(heads_per_group)]
        slopes = [ALIBI[h] for h in hs]
        qn = [q_head(O_Q, h) for h in hs]
        qd = [q_head(O_QD, h) for h in hs]
        a_s = _flash_heads(qn, slopes, g, kvb_ref, 2, 3, tk, 0, n_sel_tiles, sel_bias_fn(g), sel_skip_fn(g))
        a_w = _flash_heads(qn, slopes, g, kvb_ref, 6, 7, tkw, 0, 1, win_bias_fn, start_fn=lambda kt: win_start)
        a_d = _flash_heads(qd, slopes, g, kvb_ref, 4, 5, tk, 0, n_sel_tiles, dsa_bias_fn)
        for jj in range(heads_per_group // 2):
            j = hs[2 * jj] // 2
            o_n = (gate_pair(j, 0) * pair(o_cmp[2 * j], o_cmp[2 * j + 1])
                   + gate_pair(j, 1) * normalised_pair(a_s[2 * jj], a_s[2 * jj + 1], g)
                   + gate_pair(j, 2) * normalised_pair(a_w[2 * jj], a_w[2 * jj + 1], g))
            o_ref[0, :, j * LANES:(j + 1) * LANES] = o_n.astype(o_ref.dtype)
            o_d = normalised_pair(a_d[2 * jj], a_d[2 * jj + 1], g)
            o_ref[0, :, HALF_W + j * LANES:HALF_W + (j + 1) * LANES] = o_d.astype(o_ref.dtype)


def _odd_attn_prompt(zq, misc, kvb, wc2, bsz, t_len, tq=256, tk=512):
    tk = min(tk, t_len)
    topk = min(DSA_TOPK_MAX, t_len // 4)
    nb = t_len // BLK
    n_kt = t_len // tk
    zq3 = zq.reshape(bsz, t_len, zq.shape[1])
    misc3 = misc.reshape(bsz, t_len, LANES)
    kvb3 = kvb.reshape(bsz, t_len, kvb.shape[1])
    kern = functools.partial(_odd_attn_kernel, t_len=t_len, tq=tq, tk=tk, topk=topk)
    out = pl.pallas_call(
        kern,
        grid=(bsz, t_len // tq),
        in_specs=[pl.BlockSpec((1, tq, zq.shape[1]), lambda b, i: (b, i, 0)),
                  pl.BlockSpec((1, tq, LANES), lambda b, i: (b, i, 0)),
                  pl.BlockSpec((1, t_len, kvb.shape[1]), lambda b, i: (b, 0, 0)),
                  pl.BlockSpec((1, t_len, LANES), lambda b, i: (b, 0, 0)),
                  pl.BlockSpec((2, BLK, LANES), lambda b, i: (0, 0, 0))],
        out_specs=pl.BlockSpec((1, tq, 2 * HALF_W), lambda b, i: (b, i, 0)),
        out_shape=jax.ShapeDtypeStruct((bsz, t_len, 2 * HALF_W), jnp.bfloat16),
        scratch_shapes=[pltpu.VMEM((nb, LANES), jnp.bfloat16),
                        pltpu.VMEM((KV_GROUPS, nb, LANES), jnp.bfloat16),
                        pltpu.VMEM((t_len, IDX_DIM), jnp.bfloat16),
                        pltpu.VMEM((n_kt, tk, tq), jnp.int32),
                        pltpu.VMEM((n_kt, tq, tk), jnp.float32),
                        pltpu.SMEM((KV_GROUPS, n_kt), jnp.int32)],
        compiler_params=pltpu.CompilerParams(
            dimension_semantics=("arbitrary", "arbitrary"), vmem_limit_bytes=VMEM_LIMIT),
        name="odd_attn_prompt",
    )(zq3, misc3, kvb3, misc3, wc2)
    return out.reshape(bsz * t_len, 2 * HALF_W)


PAGE = 128
ISC_ROWS = 24


SAMPLE_SEQS = 1


def _odd_sample_kernel(pt_ref, *refs, n_pages, topk, nbp):
    del pt_ref
    for u in range(SAMPLE_SEQS):
        _odd_sample_one(u, *refs, n_pages=n_pages, topk=topk, nbp=nbp)


def _odd_sample_one(u, qn_ref, qd_ref, qi_ref, wi_ref, gate_ref, newb_ref, newki_ref, neww_ref,
                    win_ref, wc_ref, wct_ref, *rest, n_pages, topk, nbp):
    f32, bf16, i32 = jnp.float32, jnp.bfloat16, jnp.int32
    hi_prec = lax.Precision.HIGHEST
    all_pages = SAMPLE_SEQS * n_pages
    kv_refs = rest[u * n_pages:(u + 1) * n_pages]
    idx_refs = rest[all_pages + u * n_pages:all_pages + (u + 1) * n_pages]
    o_ref, wout_ref, isc_all_ref = rest[2 * all_pages:]
    isc_ref = isc_all_ref.at[u]
    past = n_pages * PAGE
    nb = past // BLK + 1
    row8 = lax.broadcasted_iota(i32, (8, 1), 0)
    lane = lax.broadcasted_iota(i32, (1, LANES), 1)
    slope8 = jnp.zeros((8, 1), f32)
    for h in range(NSA_HEADS):
        slope8 = jnp.where(row8 == h, ALIBI[h], slope8)
    low_group = row8 < NSA_HEADS // KV_GROUPS

    def pick_half(x):
        return jnp.where(low_group, x[:, 0:HEAD_DIM], x[:, HEAD_DIM:2 * HEAD_DIM])

    qn8, qd8 = qn_ref[u], qd_ref[u]
    qi8, wi8 = qi_ref[u], wi_ref[u]
    newb = newb_ref[u]
    gates = jax.nn.sigmoid(gate_ref[u])

    def new_score(q8, k_row):
        return jnp.sum(q8.astype(f32) * k_row.astype(f32), axis=-1, keepdims=True) + slope8 * float(past)

    def feat(p, slot):
        return kv_refs[p][0, slot * LANES:(slot + 1) * LANES, :]

    s_sel, s_dsa = [], []
    key_row = lax.broadcasted_iota(i32, (PAGE, nbp), 0)
    blk_col = lax.broadcasted_iota(i32, (PAGE, nbp), 1)
    kct = jnp.zeros((LANES, nbp), f32)
    vct = jnp.zeros((LANES, nbp), f32)
    for p in range(n_pages):
        pool = jnp.where(blk_col == 2 * p + key_row // BLK, 1.0, 0.0).astype(bf16)
        kct = kct + jnp.dot((feat(p, 0) * wct_ref[0]).astype(bf16), pool, preferred_element_type=f32)
        vct = vct + jnp.dot((feat(p, 1) * wct_ref[1]).astype(bf16), pool, preferred_element_type=f32)
        pos = (p * PAGE + lane).astype(f32)
        s_sel.append(jnp.dot(qn8, feat(p, 2).astype(bf16), preferred_element_type=f32) + slope8 * pos)
        s_dsa.append(jnp.dot(qd8, feat(p, 4).astype(bf16), preferred_element_type=f32) + slope8 * pos)
        ii = jnp.dot(qi8, idx_refs[p][0].astype(bf16), preferred_element_type=f32) * (IDX_DIM ** -0.5)
        isc_ref[p:p + 1, :] = jnp.sum(jnp.maximum(ii, 0.0) * wi8, axis=0, keepdims=True) * (IDX_HEADS ** -0.5)
    first_row = row8 == 0
    place = jnp.where(first_row & (lax.broadcasted_iota(i32, (8, nbp), 1) == nb - 1), 1.0, 0.0)
    tn = (((0,), (0,)), ((), ()))
    new_k = jnp.where(first_row, newb[:, 0:LANES].astype(f32) * wc_ref[0, 0:1, :], 0.0)
    new_v = jnp.where(first_row, newb[:, LANES:2 * LANES].astype(f32) * wc_ref[1, 0:1, :], 0.0)
    kct = kct + lax.dot_general(new_k.astype(bf16), place.astype(bf16), tn, preferred_element_type=f32)
    vct = vct + lax.dot_general(new_v.astype(bf16), place.astype(bf16), tn, preferred_element_type=f32)
    ii_new = jnp.sum(qi8.astype(f32) * newki_ref[u].astype(bf16).astype(f32), axis=-1, keepdims=True) * (IDX_DIM ** -0.5)
    isc_new = jnp.sum(jnp.maximum(ii_new, 0.0) * wi8, axis=0, keepdims=True) * (IDX_HEADS ** -0.5)
    isc_ref[n_pages:n_pages + 1, :] = jnp.where(lane == 0, isc_new, -jnp.inf)
    isc_ref[n_pages + 1:ISC_ROWS, :] = jnp.full((ISC_ROWS - n_pages - 1, LANES), -jnp.inf, f32)

    jb = lax.broadcasted_iota(i32, (1, nbp), 1)
    dist_c = past - (jb * BLK + (BLK - 1))
    mask_c = (dist_c >= 0) & (jb < nb)
    s_c = jnp.dot(qn8, kct.astype(bf16), preferred_element_type=f32) - slope8 * dist_c.astype(f32)
    s_c = jnp.where(mask_c, s_c, NEG_BIG)
    m_c = jnp.max(s_c, axis=-1, keepdims=True)
    e_c = jnp.where(mask_c, jnp.exp(s_c - m_c), 0.0)
    p_c = e_c * (1.0 / jnp.maximum(jnp.sum(e_c, axis=-1, keepdims=True), 1e-30))
    o_c = pick_half(_dot_nt(p_c.astype(bf16), vct.astype(bf16)))
    imp_lo = jnp.sum(jnp.where(low_group, p_c, 0.0), axis=0, keepdims=True)
    imp_hi = jnp.sum(jnp.where(low_group, 0.0, p_c), axis=0, keepdims=True)
    imp = jnp.where(low_group, imp_lo, imp_hi)
    cur = past // BLK
    imp = jnp.where((jb == 0) | (jb == cur) | (jb == cur - 1), FORCE_SCORE, imp)
    imp = jnp.where(jb <= cur, imp, -jnp.inf)
    rank = jnp.zeros((8, nbp), i32)
    for j in range(nb):
        other = imp[:, j:j + 1]
        rank = rank + ((other > imp) | ((other == imp) & (j < jb))).astype(i32)
    sel = jnp.where((rank < min(NSA_TOPN, nb)) & (imp > -jnp.inf), 1.0, 0.0)

    tot = isc_ref[...]
    tot = jnp.where(tot == 0.0, 0.0, tot)
    bits = lax.bitcast_convert_type(tot, i32)
    key = jnp.where(bits < 0, bits ^ jnp.int32(0x7FFFFFFF), bits)
    krow = lax.broadcasted_iota(i32, (ISC_ROWS, LANES), 0)
    kidx = krow * LANES + lax.broadcasted_iota(i32, (ISC_ROWS, LANES), 1)
    key = jnp.where(kidx <= past, key, INT_MIN)

    def total(x):
        return jnp.sum(jnp.sum(x, axis=1, keepdims=True), axis=0, keepdims=True)

    def count_ge(trial):
        return total((key >= trial).astype(i32))

    c_pos = count_ge(jnp.zeros((1, 1), i32))
    v_thr = jnp.where(c_pos >= topk, 0, INT_MIN).astype(i32)
    for shift, width in ((27, 4), (23, 4), (19, 4), (15, 4), (11, 4), (7, 4), (3, 4), (0, 3)):
        digit = jnp.zeros((1, 1), i32)
        for j in range(1, 1 << width):
            ok = count_ge(v_thr | jnp.int32(j << shift)) >= topk
            digit = digit + ok.astype(i32)
        v_thr = v_thr | lax.shift_left(digit, jnp.int32(shift))
    v_eff = jnp.maximum(v_thr, INT_MIN + 1)

    def tie_mask():
        need = topk - count_ge(v_thr + 1)
        tie = key == v_thr
        w_last = jnp.zeros((1, 1), i32)
        for shift in (8, 4, 0):
            digit = jnp.zeros((1, 1), i32)
            for j in range(1, 16):
                below = total((tie & (kidx < (w_last | jnp.int32(j << shift)))).astype(i32))
                digit = digit + (below < need).astype(i32)
            w_last = w_last | lax.shift_left(digit, jnp.int32(shift))
        take = (key > v_thr) | (tie & (kidx <= w_last))
        return jnp.where(take & (key >= v_eff), 1.0, 0.0)

    def plain_mask():
        return jnp.where(key >= v_eff, 1.0, 0.0)

    has_surplus_tie = count_ge(v_thr)[0, 0] > topk
    dsel = lax.cond(has_surplus_tie, tie_mask, plain_mask)

    def attend(tiles, masks, s_new, new_ok, v_tile, v_new):
        tiles = [jnp.where(mk, t, NEG_BIG) for t, mk in zip(tiles, masks)]
        s_new = jnp.where(new_ok, s_new, NEG_BIG)
        m = s_new
        for t in tiles:
            m = jnp.maximum(m, jnp.max(t, axis=-1, keepdims=True))
        l = jnp.exp(s_new - m)
        acc = l * v_new.astype(f32)
        for j, t in enumerate(tiles):
            e = jnp.exp(t - m)
            l = l + jnp.sum(e, axis=-1, keepdims=True)
            acc = acc + _dot_nt(e.astype(bf16), v_tile(j))
        return pick_half(acc * (1.0 / l))

    sel_masks = [jnp.where(lane < BLK, sel[:, 2 * p:2 * p + 1], sel[:, 2 * p + 1:2 * p + 2]) > 0.0
                 for p in range(n_pages)]
    o_s = attend(s_sel, sel_masks, new_score(qn8, newb[:, 2 * LANES:3 * LANES]), sel[:, nb - 1:nb] > 0.0,
                 lambda p: feat(p, 3).astype(bf16), newb[:, 3 * LANES:4 * LANES])
    dsa_masks = [dsel[p:p + 1, :] > 0.0 for p in range(n_pages)]
    o_d = attend(s_dsa, dsa_masks, new_score(qd8, newb[:, 4 * LANES:5 * LANES]), dsel[n_pages:n_pages + 1, 0:1] > 0.0,
                 lambda p: feat(p, 5).astype(bf16), newb[:, 5 * LANES:6 * LANES])
    n_win = win_ref.shape[2]
    wlane = lax.broadcasted_iota(i32, (1, n_win), 1)
    wpos = past - n_win + wlane
    s_w = jnp.dot(qn8, win_ref[u, 0:LANES, :].astype(bf16), preferred_element_type=f32) + slope8 * wpos.astype(f32)
    w_ok = (past - wpos < WINDOW) & (wpos >= 0)
    o_w = attend([s_w], [w_ok], new_score(qn8, newb[:, 6 * LANES:7 * LANES]), True,
                 lambda p: win_ref[u, LANES:2 * LANES, :].astype(bf16), newb[:, 7 * LANES:8 * LANES])
    o_ref[u, 0:NSA_HEADS, :] = gates[:, 0:1] * o_c + gates[:, 1:2] * o_s + gates[:, 2:3] * o_w
    o_ref[u, NSA_HEADS:2 * NSA_HEADS, :] = o_d
    last = jnp.where(first_row & (lax.broadcasted_iota(i32, (8, n_win), 1) == n_win - 1), 1.0, 0.0)
    new_w = jnp.where(first_row, neww_ref[u], 0.0)
    placed = lax.dot_general(new_w, last, tn, preferred_element_type=f32, precision=hi_prec)
    wout_ref[u] = jnp.where(wlane == n_win - 1, placed, pltpu.roll(win_ref[u], n_win - 1, 1))


def _odd_attn_sample(zq, misc, kvb, win_new, cache_kv, cache_idx, cache_win, page_table, w_cmp):
    f32, bf16 = jnp.float32, jnp.bfloat16
    s_n, n_pages = page_table.shape
    n_pool = cache_kv.shape[0]
    n_win = cache_win.shape[1]
    past = n_pages * PAGE
    topk = min(DSA_TOPK_MAX, (past + 1) // 4)
    nbp = _round_up(past // BLK + 1, 8)
    wc2 = jnp.concatenate([w_cmp, w_cmp], axis=-1)
    wct = jnp.tile(jnp.swapaxes(w_cmp, 1, 2), (1, KV_GROUPS, PAGE // BLK))
    qn = zq[:, O_Q:O_QD].reshape(s_n, NSA_HEADS, LANES)
    qd = zq[:, O_QD:ZQ_QI].reshape(s_n, DSA_HEADS, LANES)
    qi = jnp.pad(zq[:, ZQ_QI:ZQ_QI + LANES].reshape(s_n, IDX_HEADS, IDX_DIM), ((0, 0), (0, 8 - IDX_HEADS), (0, 0)))
    wi = jnp.pad(misc[:, MISC_WI:MISC_WI + IDX_HEADS].reshape(s_n, IDX_HEADS, 1), ((0, 0), (0, 8 - IDX_HEADS), (0, 0)))
    gts = misc[:, MISC_GATE:MISC_GATE + 3 * NSA_HEADS].reshape(s_n, NSA_HEADS, 3)
    kv_pages = jnp.transpose(cache_kv, (0, 2, 3, 4, 1)).reshape(n_pool, KV_SLOTS * LANES, PAGE)
    idx_pages = jnp.transpose(cache_idx, (0, 2, 1))
    win_t = jnp.transpose(cache_win, (0, 2, 3, 4, 1)).reshape(s_n, 2 * LANES, n_win)
    n_u = SAMPLE_SEQS
    per_seq = lambda shape: pl.BlockSpec((n_u,) + shape, lambda b, pt: (b, 0, 0))
    page_spec = lambda rows, u, p: pl.BlockSpec((1, rows, PAGE), lambda b, pt, u=u, p=p: (pt[b * n_u + u, p], 0, 0))
    seq_pages = [(u, p) for u in range(n_u) for p in range(n_pages)]
    in_specs = ([per_seq((NSA_HEADS, LANES)), per_seq((DSA_HEADS, LANES)), per_seq((8, IDX_DIM)),
                 per_seq((8, 1)), per_seq((NSA_HEADS, 3)), per_seq((1, 8 * LANES)), per_seq((1, IDX_DIM)),
                 per_seq((1, 2 * LANES)), per_seq((2 * LANES, n_win)),
                 pl.BlockSpec((2, BLK, LANES), lambda b, pt: (0, 0, 0)),
                 pl.BlockSpec((2, LANES, PAGE), lambda b, pt: (0, 0, 0))]
                + [page_spec(KV_SLOTS * LANES, u, p) for u, p in seq_pages]
                + [page_spec(IDX_DIM, u, p) for u, p in seq_pages])
    kern = functools.partial(_odd_sample_kernel, n_pages=n_pages, topk=topk, nbp=nbp)
    o, win_out = pl.pallas_call(
        kern,
        grid_spec=pltpu.PrefetchScalarGridSpec(
            num_scalar_prefetch=1,
            grid=(s_n // n_u,),
            in_specs=in_specs,
            out_specs=[per_seq((2 * NSA_HEADS, HEAD_DIM)), per_seq((2 * LANES, n_win))],
            scratch_shapes=[pltpu.VMEM((n_u, ISC_ROWS, LANES), f32)]),
        out_shape=[jax.ShapeDtypeStruct((s_n, 2 * NSA_HEADS, HEAD_DIM), f32),
                   jax.ShapeDtypeStruct((s_n, 2 * LANES, n_win), f32)],
        compiler_params=pltpu.CompilerParams(
            dimension_semantics=("arbitrary",), vmem_limit_bytes=VMEM_LIMIT),
        name="odd_attn_sample",
    )(page_table, qn, qd, qi, wi, gts, kvb.reshape(s_n, 1, 8 * LANES), misc[:, 0:IDX_DIM].reshape(s_n, 1, IDX_DIM),
      win_new.reshape(s_n, 1, 2 * LANES), win_t, wc2, wct,
      *([kv_pages] * len(seq_pages)), *([idx_pages] * len(seq_pages)))
    win_out = jnp.transpose(win_out.reshape(s_n, 2, KV_GROUPS, HEAD_DIM, n_win), (0, 4, 1, 2, 3))
    return o.reshape(s_n, 2 * HALF_W), win_out


def _split(z, sizes):
    cuts = [int(c) for c in np.cumsum(sizes)[:-1]]
    return jnp.split(z, cuts, axis=-1)


def _pad_cols(w, n):
    return jnp.pad(w, ((0, 0), (0, n - w.shape[1])))


def kernel(x_prompt, x_sample, state_conv, state_C, state_n, state_m, cache_kv, cache_idx, cache_win, page_table,
           w_in_e, gate_b_e, conv_w, conv_b, w_out_e, w_in_o, w_cmp, w_out_o, w_up, w_down, ln_g, ln_b):
    f32, bf16 = jnp.float32, jnp.bfloat16
    bp, tp, d = x_prompt.shape
    dbs = x_sample.shape[0]
    keep = cache_win.shape[2]
    yp = x_prompt.reshape(bp * tp, d)
    ys = x_sample.reshape(dbs, d)
    tm_p, tm_s = 512, dbs
    outs = {}
    for layer in range(DEPTH):
        i = layer // 2
        if layer % 2 == 0:
            w_in = _pad_cols(w_in_e[i], E_PACKED).astype(bf16)
            w_out = w_out_e[i].astype(bf16)
            ya, qkv, og, gts, tails = _even_proj(yp, w_in, conv_w[i], conv_b[i], tm_p, tp)
            yb, c_fin, n_fin, m_fin = _mlstm_prompt(qkv, og, gts, gate_b_e[i], bp, tp)
            cvp = tails.reshape(bp, tp // tm_p, CONV_TAIL, HALF_W)[:, -1, CONV_TAIL - (CONV_W - 1):]
            ya_s, yb_s, cvs, c_s, n_s, m_s = _even_sample(_proj(ys, w_in, tm_s), gate_b_e[i], conv_w[i], conv_b[i],
                                                          state_conv[i], state_C[i], state_n[i], state_m[i])
            outs['conv'] = (cvp[None], cvs[None])
            outs['c'] = (c_fin[None], c_s[None])
            outs['n'] = (n_fin[None], n_s[None])
            outs['m'] = (m_fin[:, :, 0][None], m_s[:, :, 0][None])
            yp = _outproj2_ln(ya, yb.reshape(bp * tp, HALF_W), w_out, yp, ln_g[layer, 0], ln_b[layer, 0], tm_p)
            ys = _outproj2_ln(ya_s, yb_s, w_out, ys, ln_g[layer, 0], ln_b[layer, 0], tm_s)
        else:
            w_out = w_out_o[i].astype(bf16)
            w_packed = _pack_w_in_o(w_in_o[i]).astype(bf16)
            wc2 = jnp.concatenate([w_cmp[i], w_cmp[i]], axis=-1)
            rows_first = lambda kvt: jnp.transpose(
                kvt.reshape(kvt.shape[0], KV_SLOTS, KV_GROUPS, HEAD_DIM, kvt.shape[2]), (0, 4, 1, 2, 3))
            zq, kvt_p, kvb, winp, miscp = _odd_proj(yp, w_packed, tm_p, tp)
            mp = _odd_attn_prompt(zq, miscp, kvb, wc2, bp, tp)
            kvp = rows_first(kvt_p)
            ixp = miscp[:, :IDX_DIM].reshape(bp, tp, IDX_DIM)
            wnp = winp.reshape(bp, tp, 2, KV_GROUPS, HEAD_DIM)[:, tp - keep:]
            zq_s, kvt_s, kvb_s, win_s, misc_s = _odd_proj(ys, w_packed, tm_s, dbs)
            ms, wns = _odd_attn_sample(zq_s, misc_s, kvb_s, win_s, cache_kv[i], cache_idx[i], cache_win[i],
                                       page_table, w_cmp[i])
            kvs = rows_first(kvt_s).reshape(dbs, 1, KV_SLOTS, KV_GROUPS, HEAD_DIM)
            ixs = misc_s[:, :IDX_DIM].reshape(dbs, 1, IDX_DIM)
            outs['kv'] = (kvp[None], kvs[None])
            outs['idx'] = (ixp[None], ixs[None])
            outs['win'] = (wnp[None], wns[None])
            yp = _outproj_ln(mp, w_out, yp, ln_g[layer, 0], ln_b[layer, 0], tm_p)
            ys = _outproj_ln(ms, w_out, ys, ln_g[layer, 0], ln_b[layer, 0], tm_s)
        wu, wd = w_up[layer].astype(bf16), w_down[layer].astype(bf16)
        yp = _mlp_ln(yp, wu, wd, ln_g[layer, 1], ln_b[layer, 1], 2 * tm_p, 1024)
        ys = _mlp_ln(ys, wu, wd, ln_g[layer, 1], ln_b[layer, 1], tm_s, 1024)
    return (yp.reshape(bp, tp, d), ys.reshape(dbs, 1, d),
            outs['conv'][0], outs['conv'][1], outs['c'][0], outs['c'][1],
            outs['n'][0], outs['n'][1], outs['m'][0], outs['m'][1],
            outs['kv'][0], outs['kv'][1], outs['idx'][0], outs['idx'][1],
            outs['win'][0], outs['win'][1])
```

```python
import functools

import jax
import jax.numpy as jnp
import numpy as np
from jax import lax
from jax.experimental import pallas as pl
from jax.experimental.pallas import tpu as pltpu

D_MODEL = 1024
DEPTH = 2
HALF_W = 512
D_FF = 4096
CONV_W = 3
ML_HEADS = 4
ML_DIM = 128
HEAD_DIM = 64
KV_GROUPS = 2
NSA_HEADS = 8
DSA_HEADS = 8
BLK = 64
NSA_TOPN = 8
WINDOW = 256
IDX_HEADS = 4
IDX_DIM = 32
DSA_TOPK_MAX = 256
KV_SLOTS = 6
FORCE_SCORE = 1e4
ALPHA = (2.0 * DEPTH) ** 0.25
LN_EPS = 1e-5
O_SIZES = (512, 256, 256, 256, 24, 512, 256, 128, 32, 4)
LANES = 128
VMEM_LIMIT = 48 * 1024 * 1024


def _round_up(n, m):
    return -(-n // m) * m


def _proj_kernel(x_ref, w_ref, o_ref):
    o_ref[...] = jnp.dot(x_ref[...].astype(jnp.bfloat16), w_ref[...],
                         preferred_element_type=jnp.float32)


def _proj(x, w_bf16, tm):
    m, k = x.shape
    n = w_bf16.shape[1]
    tn = n
    for cand in (1024, 768, 512, 256, 128):
        if n % cand == 0:
            tn = cand
            break
    return pl.pallas_call(
        _proj_kernel,
        grid=(n // tn, m // tm),
        in_specs=[pl.BlockSpec((tm, k), lambda j, i: (i, 0)),
                  pl.BlockSpec((k, tn), lambda j, i: (0, j))],
        out_specs=pl.BlockSpec((tm, tn), lambda j, i: (i, j)),
        out_shape=jax.ShapeDtypeStruct((m, n), jnp.float32),
        compiler_params=pltpu.CompilerParams(
            dimension_semantics=("arbitrary", "arbitrary"), vmem_limit_bytes=VMEM_LIMIT),
        name="proj",
    )(x, w_bf16)


def _layer_norm_rows(v, g, b):
    mu = jnp.mean(v, axis=-1, keepdims=True)
    d = v - mu
    var = jnp.mean(d * d, axis=-1, keepdims=True)
    return d * lax.rsqrt(var + LN_EPS) * g + b


def _outproj_ln_kernel(y_ref, w_ref, x_ref, g_ref, b_ref, o_ref):
    mix = jnp.dot(y_ref[...].astype(jnp.bfloat16), w_ref[...], preferred_element_type=jnp.float32)
    o_ref[...] = _layer_norm_rows(ALPHA * x_ref[...] + mix, g_ref[...], b_ref[...])


def _outproj_ln(y, w_bf16, x, g, b, tm):
    m, d = x.shape
    k = y.shape[1]
    return pl.pallas_call(
        _outproj_ln_kernel,
        grid=(m // tm,),
        in_specs=[pl.BlockSpec((tm, k), lambda i: (i, 0)),
                  pl.BlockSpec((k, d), lambda i: (0, 0)),
                  pl.BlockSpec((tm, d), lambda i: (i, 0)),
                  pl.BlockSpec((1, d), lambda i: (0, 0)),
                  pl.BlockSpec((1, d), lambda i: (0, 0))],
        out_specs=pl.BlockSpec((tm, d), lambda i: (i, 0)),
        out_shape=jax.ShapeDtypeStruct((m, d), jnp.float32),
        compiler_params=pltpu.CompilerParams(
            dimension_semantics=("arbitrary",), vmem_limit_bytes=VMEM_LIMIT),
        name="outproj_ln",
    )(y, w_bf16, x, g.reshape(1, d), b.reshape(1, d))


def _mlp_ln_kernel(x_ref, wu_ref, wd_ref, g_ref, b_ref, o_ref, acc_ref):
    f = pl.program_id(1)

    @pl.when(f == 0)
    def _():
        acc_ref[...] = jnp.zeros_like(acc_ref)

    up = jnp.dot(x_ref[...].astype(jnp.bfloat16), wu_ref[...], preferred_element_type=jnp.float32)
    act = jnp.square(jnp.maximum(up, 0.0))
    acc_ref[...] += jnp.dot(act.astype(jnp.bfloat16), wd_ref[...], preferred_element_type=jnp.float32)

    @pl.when(f == pl.num_programs(1) - 1)
    def _():
        o_ref[...] = _layer_norm_rows(ALPHA * x_ref[...] + acc_ref[...], g_ref[...], b_ref[...])


def _mlp_ln(x, wu_bf16, wd_bf16, g, b, tm, tf):
    m, d = x.shape
    ff = wu_bf16.shape[1]
    return pl.pallas_call(
        _mlp_ln_kernel,
        grid=(m // tm, ff // tf),
        in_specs=[pl.BlockSpec((tm, d), lambda i, f: (i, 0)),
                  pl.BlockSpec((d, tf), lambda i, f: (0, f)),
                  pl.BlockSpec((tf, d), lambda i, f: (f, 0)),
                  pl.BlockSpec((1, d), lambda i, f: (0, 0)),
                  pl.BlockSpec((1, d), lambda i, f: (0, 0))],
        out_specs=pl.BlockSpec((tm, d), lambda i, f: (i, 0)),
        out_shape=jax.ShapeDtypeStruct((m, d), jnp.float32),
        scratch_shapes=[pltpu.VMEM((tm, d), jnp.float32)],
        compiler_params=pltpu.CompilerParams(
            dimension_semantics=("arbitrary", "arbitrary"), vmem_limit_bytes=VMEM_LIMIT),
        name="mlp_ln",
    )(x, wu_bf16, wd_bf16, g.reshape(1, d), b.reshape(1, d))


E_QKV = 3 * HALF_W
E_OG = 6 * HALF_W
E_GATE = 7 * HALF_W
E_PACKED = 7 * HALF_W + LANES
CONV_TAIL = 8


def _even_proj_kernel(x_ref, w_ref, cw_ref, cb_ref, ya_ref, qkv_ref, og_ref, gate_ref, tail_ref, carry_ref,
                      *, tiles_per_seq):
    f32, bf16 = jnp.float32, jnp.bfloat16
    tm = x_ref.shape[0]

    @pl.when(pl.program_id(0) % tiles_per_seq == 0)
    def _():
        carry_ref[...] = jnp.zeros_like(carry_ref)

    z = jnp.dot(x_ref[...].astype(bf16), w_ref[...], preferred_element_type=f32)
    u = z[:, 2 * HALF_W:3 * HALF_W] * z[:, 0:HALF_W]
    prev = carry_ref[...]
    row = lax.broadcasted_iota(jnp.int32, (tm, 1), 0)
    conv = cb_ref[...] + cw_ref[CONV_W - 1:CONV_W, :] * u
    for back in range(1, CONV_W):
        shifted = pltpu.roll(u, back, 0)
        for r in range(back):
            shifted = jnp.where(row == r, prev[CONV_TAIL - back + r:CONV_TAIL - back + r + 1, :], shifted)
        conv = conv + cw_ref[CONV_W - 1 - back:CONV_W - back, :] * shifted
    ya_ref[...] = (z[:, HALF_W:2 * HALF_W] * conv).astype(bf16)
    tail = u[tm - CONV_TAIL:]
    carry_ref[...] = tail
    tail_ref[0] = tail
    qkv_ref[:, 0:HALF_W] = z[:, E_QKV:E_QKV + HALF_W].astype(bf16)
    qkv_ref[:, HALF_W:2 * HALF_W] = (z[:, E_QKV + HALF_W:E_QKV + 2 * HALF_W] * (ML_DIM ** -0.5)).astype(bf16)
    qkv_ref[:, 2 * HALF_W:3 * HALF_W] = z[:, E_QKV + 2 * HALF_W:E_OG].astype(bf16)
    og_ref[...] = z[:, E_OG:E_GATE]
    gate_ref[...] = z[:, E_GATE:E_PACKED]


def _even_proj(x, w_bf16, conv_w, conv_b, tm, seq_len):
    m, k = x.shape
    widths = (HALF_W, 3 * HALF_W, HALF_W, LANES)
    dtypes = (jnp.bfloat16, jnp.bfloat16, jnp.float32, jnp.float32)
    kern = functools.partial(_even_proj_kernel, tiles_per_seq=seq_len // tm)
    return pl.pallas_call(
        kern,
        grid=(m // tm,),
        in_specs=[pl.BlockSpec((tm, k), lambda i: (i, 0)),
                  pl.BlockSpec((k, E_PACKED), lambda i: (0, 0)),
                  pl.BlockSpec((CONV_W, HALF_W), lambda i: (0, 0)),
                  pl.BlockSpec((1, HALF_W), lambda i: (0, 0))],
        out_specs=[pl.BlockSpec((tm, n), lambda i: (i, 0)) for n in widths]
        + [pl.BlockSpec((1, CONV_TAIL, HALF_W), lambda i: (i, 0, 0))],
        out_shape=[jax.ShapeDtypeStruct((m, n), dt) for n, dt in zip(widths, dtypes)]
        + [jax.ShapeDtypeStruct((m // tm, CONV_TAIL, HALF_W), jnp.float32)],
        scratch_shapes=[pltpu.VMEM((CONV_TAIL, HALF_W), jnp.float32)],
        compiler_params=pltpu.CompilerParams(
            dimension_semantics=("arbitrary",), vmem_limit_bytes=VMEM_LIMIT),
        name="even_proj",
    )(x, w_bf16, conv_w, conv_b.reshape(1, HALF_W))


def _mlstm_kernel(qkv_ref, og_ref, gate_ref, gb_ref, yb_ref, c_ref, n_ref, m_ref):
    f32, bf16 = jnp.float32, jnp.bfloat16
    L = qkv_ref.shape[1]

    @pl.when(pl.program_id(1) == 0)
    def _():
        c_ref[...] = jnp.zeros_like(c_ref)
        n_ref[...] = jnp.zeros_like(n_ref)
        m_ref[...] = jnp.zeros_like(m_ref)

    pre = gate_ref[0] + gb_ref[...]
    lf = jax.nn.log_sigmoid(pre)
    row = lax.broadcasted_iota(jnp.int32, (L, L), 0)
    col = lax.broadcasted_iota(jnp.int32, (L, L), 1)
    causal = col <= row
    tri = jnp.where(causal, 1.0, 0.0).astype(f32)
    b_all = jnp.dot(tri, lf, preferred_element_type=f32, precision=lax.Precision.HIGHEST)
    pre_t = pre.T
    b_t = b_all.T
    for h in range(ML_HEADS):
        q = qkv_ref[0, :, h * ML_DIM:(h + 1) * ML_DIM]
        k = qkv_ref[0, :, HALF_W + h * ML_DIM:HALF_W + (h + 1) * ML_DIM]
        v = qkv_ref[0, :, 2 * HALF_W + h * ML_DIM:2 * HALF_W + (h + 1) * ML_DIM]
        ig_col = pre[:, h:h + 1]
        b_col = b_all[:, ML_HEADS + h:ML_HEADS + h + 1]
        a_row = pre_t[h:h + 1, :] - b_t[ML_HEADS + h:ML_HEADS + h + 1, :]
        m_st = m_ref[0, h:h + 1, 0:1]
        c_st = c_ref[0, h]
        n_st = n_ref[0, h:h + 1, :]
        dmat = jnp.where(causal, b_col + a_row, -jnp.inf)
        inter = b_col + m_st
        m_t = jnp.maximum(inter, jnp.max(dmat, axis=-1, keepdims=True))
        w_intra = jnp.exp(dmat - m_t)
        w_inter = jnp.exp(inter - m_t)
        s = _dot_nt(q, k) * w_intra
        num = jnp.dot(s.astype(bf16), v, preferred_element_type=f32) + w_inter * _dot_nt(q, c_st.astype(bf16))
        den = jnp.sum(s, axis=-1, keepdims=True) + w_inter * jnp.sum(q.astype(f32) * n_st, axis=-1, keepdims=True)
        hs = num * (1.0 / jnp.maximum(jnp.abs(den), jnp.exp(-m_t)))
        m_new = m_t[L - 1:L, :]
        b_last = b_col[L - 1:L, :]
        w_state = jnp.exp(b_last - b_col + ig_col - m_new)
        decay = jnp.exp(b_last + m_st - m_new)
        vw = (v.astype(f32) * w_state).astype(bf16)
        c_ref[0, h] = decay * c_st + lax.dot_general(vw, k, (((0,), (0,)), ((), ())), preferred_element_type=f32)
        n_ref[0, h:h + 1, :] = decay * n_st + jnp.sum(k.astype(f32) * w_state, axis=0, keepdims=True)
        m_ref[0, h:h + 1, :] = jnp.broadcast_to(m_new, (1, LANES))
        og = og_ref[0, :, h * ML_DIM:(h + 1) * ML_DIM]
        yb_ref[0, :, h * ML_DIM:(h + 1) * ML_DIM] = (jax.nn.sigmoid(og) * hs).astype(bf16)


def _mlstm_prompt(qkv, og, gates, gate_b, bsz, t_len, chunk=128):
    gb = jnp.pad(gate_b, (0, LANES - gate_b.shape[0])).reshape(1, LANES)
    return pl.pallas_call(
        _mlstm_kernel,
        grid=(bsz, t_len // chunk),
        in_specs=[pl.BlockSpec((1, chunk, 3 * HALF_W), lambda b, c: (b, c, 0)),
                  pl.BlockSpec((1, chunk, HALF_W), lambda b, c: (b, c, 0)),
                  pl.BlockSpec((1, chunk, LANES), lambda b, c: (b, c, 0)),
                  pl.BlockSpec((1, LANES), lambda b, c: (0, 0))],
        out_specs=[pl.BlockSpec((1, chunk, HALF_W), lambda b, c: (b, c, 0)),
                   pl.BlockSpec((1, ML_HEADS, ML_DIM, ML_DIM), lambda b, c: (b, 0, 0, 0)),
                   pl.BlockSpec((1, ML_HEADS, ML_DIM), lambda b, c: (b, 0, 0)),
                   pl.BlockSpec((1, ML_HEADS, LANES), lambda b, c: (b, 0, 0))],
        out_shape=[jax.ShapeDtypeStruct((bsz, t_len, HALF_W), jnp.bfloat16),
                   jax.ShapeDtypeStruct((bsz, ML_HEADS, ML_DIM, ML_DIM), jnp.float32),
                   jax.ShapeDtypeStruct((bsz, ML_HEADS, ML_DIM), jnp.float32),
                   jax.ShapeDtypeStruct((bsz, ML_HEADS, LANES), jnp.float32)],
        compiler_params=pltpu.CompilerParams(
            dimension_semantics=("arbitrary", "arbitrary"), vmem_limit_bytes=VMEM_LIMIT),
        name="mlstm_prompt",
    )(qkv.reshape(bsz, t_len, 3 * HALF_W), og.reshape(bsz, t_len, HALF_W),
      gates.reshape(bsz, t_len, LANES), gb)


def _outproj2_ln_kernel(ya_ref, yb_ref, w_ref, x_ref, g_ref, b_ref, o_ref):
    mix = jnp.dot(ya_ref[...], w_ref[0:HALF_W, :], preferred_element_type=jnp.float32)
    mix = mix + jnp.dot(yb_ref[...], w_ref[HALF_W:, :], preferred_element_type=jnp.float32)
    o_ref[...] = _layer_norm_rows(ALPHA * x_ref[...] + mix, g_ref[...], b_ref[...])


def _outproj2_ln(ya, yb, w_bf16, x, g, b, tm):
    m, d = x.shape
    return pl.pallas_call(
        _outproj2_ln_kernel,
        grid=(m // tm,),
        in_specs=[pl.BlockSpec((tm, HALF_W), lambda i: (i, 0)),
                  pl.BlockSpec((tm, HALF_W), lambda i: (i, 0)),
                  pl.BlockSpec((2 * HALF_W, d), lambda i: (0, 0)),
                  pl.BlockSpec((tm, d), lambda i: (i, 0)),
                  pl.BlockSpec((1, d), lambda i: (0, 0)),
                  pl.BlockSpec((1, d), lambda i: (0, 0))],
        out_specs=pl.BlockSpec((tm, d), lambda i: (i, 0)),
        out_shape=jax.ShapeDtypeStruct((m, d), jnp.float32),
        compiler_params=pltpu.CompilerParams(
            dimension_semantics=("arbitrary",), vmem_limit_bytes=VMEM_LIMIT),
        name="outproj2_ln",
    )(ya, yb, w_bf16, x, g.reshape(1, d), b.reshape(1, d))


SAMPLE_ROWS = 8


def _even_sample_kernel(h_ref, bg_ref, cg_ref, q_ref, k_ref, v_ref, og_ref, gate_ref, gb_ref, cw_ref, cb_ref,
                        conv_ref, c_ref, n_ref, m_ref, ya_ref, yb_ref, conv_out_ref, c_out_ref, n_out_ref, m_out_ref):
    f32, bf16 = jnp.float32, jnp.bfloat16
    rows = h_ref.shape[0]
    row = lax.broadcasted_iota(jnp.int32, (rows, 1), 0)
    tn = (((0,), (0,)), ((), ()))
    u = cg_ref[...] * h_ref[...]
    prev0, prev1 = conv_ref[:, 0, :], conv_ref[:, 1, :]
    conv = cb_ref[...] + cw_ref[0:1, :] * prev0 + cw_ref[1:2, :] * prev1 + cw_ref[2:3, :] * u
    ya_ref[...] = (bg_ref[...] * conv).astype(bf16)
    conv_out_ref[:, 0, :] = prev1
    conv_out_ref[:, 1, :] = u
    pre = gate_ref[...] + gb_ref[...]
    lf_all = jax.nn.log_sigmoid(pre)
    for h in range(ML_HEADS):
        cols = slice(h * ML_DIM, (h + 1) * ML_DIM)
        q = q_ref[:, cols]
        k = k_ref[:, cols] * (ML_DIM ** -0.5)
        v = v_ref[:, cols]
        ig = pre[:, h:h + 1]
        lf = lf_all[:, ML_HEADS + h:ML_HEADS + h + 1]
        m_st = m_ref[:, h:h + 1]
        n_st = n_ref[:, h, :]
        inter = lf + m_st
        m_t = jnp.maximum(inter, ig)
        w_intra = jnp.exp(ig - m_t)
        w_inter = jnp.exp(inter - m_t)
        qb, kb = q.astype(bf16), k.astype(bf16)
        s = jnp.sum(qb.astype(f32) * kb.astype(f32), axis=-1, keepdims=True) * w_intra
        cq = jnp.zeros((rows, ML_DIM), f32)
        for b in range(rows):
            c_b = c_ref[b, h]
            cq = jnp.where(row == b, _dot_nt(qb, c_b.astype(bf16)), cq)
            vw = jnp.where(row == b, v * w_intra, 0.0)
            outer = lax.dot_general(vw, k, tn, preferred_element_type=f32, precision=lax.Precision.HIGHEST)
            c_out_ref[b, h] = w_inter[b:b + 1, :] * c_b + outer
        num = s * v + w_inter * cq
        den = s + w_inter * jnp.sum(n_st * q, axis=-1, keepdims=True)
        hs = num * (1.0 / jnp.maximum(jnp.abs(den), jnp.exp(-m_t)))
        n_out_ref[:, h, :] = w_inter * n_st + w_intra * k
        m_out_ref[:, h, :] = jnp.broadcast_to(m_t, (rows, LANES))
        yb_ref[:, cols] = (jax.nn.sigmoid(og_ref[:, cols]) * hs).astype(bf16)


def _even_sample(z, gate_b, conv_w, conv_b, state_conv, state_c, state_n, state_m):
    s_n = z.shape[0]
    f32 = jnp.float32
    r = SAMPLE_ROWS
    gb = jnp.pad(gate_b, (0, LANES - gate_b.shape[0])).reshape(1, LANES)
    col = lambda j: pl.BlockSpec((r, HALF_W), lambda i, j=j: (i, j))
    in_specs = [col(j) for j in range(7)] + [
        pl.BlockSpec((r, LANES), lambda i: (i, E_GATE // LANES)),
        pl.BlockSpec((1, LANES), lambda i: (0, 0)),
        pl.BlockSpec((CONV_W, HALF_W), lambda i: (0, 0)),
        pl.BlockSpec((1, HALF_W), lambda i: (0, 0)),
        pl.BlockSpec((r, CONV_W - 1, HALF_W), lambda i: (i, 0, 0)),
        pl.BlockSpec((r, ML_HEADS, ML_DIM, ML_DIM), lambda i: (i, 0, 0, 0)),
        pl.BlockSpec((r, ML_HEADS, ML_DIM), lambda i: (i, 0, 0)),
        pl.BlockSpec((r, ML_HEADS), lambda i: (i, 0))]
    out_specs = [pl.BlockSpec((r, HALF_W), lambda i: (i, 0)),
                 pl.BlockSpec((r, HALF_W), lambda i: (i, 0)),
                 pl.BlockSpec((r, CONV_W - 1, HALF_W), lambda i: (i, 0, 0)),
                 pl.BlockSpec((r, ML_HEADS, ML_DIM, ML_DIM), lambda i: (i, 0, 0, 0)),
                 pl.BlockSpec((r, ML_HEADS, ML_DIM), lambda i: (i, 0, 0)),
                 pl.BlockSpec((r, ML_HEADS, LANES), lambda i: (i, 0, 0))]
    out_shape = [jax.ShapeDtypeStruct((s_n, HALF_W), jnp.bfloat16),
                 jax.ShapeDtypeStruct((s_n, HALF_W), jnp.bfloat16),
                 jax.ShapeDtypeStruct((s_n, CONV_W - 1, HALF_W), f32),
                 jax.ShapeDtypeStruct((s_n, ML_HEADS, ML_DIM, ML_DIM), f32),
                 jax.ShapeDtypeStruct((s_n, ML_HEADS, ML_DIM), f32),
                 jax.ShapeDtypeStruct((s_n, ML_HEADS, LANES), f32)]
    return pl.pallas_call(
        _even_sample_kernel,
        grid=(s_n // r,),
        in_specs=in_specs,
        out_specs=out_specs,
        out_shape=out_shape,
        compiler_params=pltpu.CompilerParams(
            dimension_semantics=("arbitrary",), vmem_limit_bytes=VMEM_LIMIT),
        name="even_sample",
    )(z, z, z, z, z, z, z, z, gb, conv_w, conv_b.reshape(1, HALF_W), state_conv, state_c, state_n, state_m)


O_Q = 0
O_QD = NSA_HEADS * LANES
O_KV = 2048
O_WIN = 2816
O_QI = 3072
O_MISC = 3200
O_PACKED = 3328
ZQ_QI = 2048
MISC_WI = IDX_DIM
MISC_GATE = IDX_DIM + IDX_HEADS
NEG_BIG = -(2.0 ** 100)
INT_MIN = -(2 ** 31)
ALIBI = tuple(float(2.0 ** (-8.0 * (h + 1) / NSA_HEADS)) for h in range(NSA_HEADS))


def _pack_w_in_o(w):
    qn, kvc, kvs, kvw, gates, qd, kvd, qi, ki, wi = _split(w, O_SIZES)
    scale = HEAD_DIM ** -0.5
    rows = w.shape[0]

    def spread(q):
        q = (q * scale).reshape(rows, NSA_HEADS, HEAD_DIM)
        z = jnp.zeros_like(q)
        low = (jnp.arange(NSA_HEADS) < NSA_HEADS // KV_GROUPS)[None, :, None]
        return jnp.concatenate([jnp.where(low, q, z), jnp.where(low, z, q)], axis=-1).reshape(rows, NSA_HEADS * LANES)

    pad = jnp.zeros((rows, O_PACKED - O_MISC - IDX_DIM - IDX_HEADS - 3 * NSA_HEADS), w.dtype)
    return jnp.concatenate([spread(qn), spread(qd), kvc, kvs, kvd, kvw, qi, ki, wi, gates, pad], axis=1)


def _odd_proj_kernel(x_ref, w_ref, wkvt_ref, zq_ref, kvt_ref, kvb_ref, win_ref, misc_ref):
    xb = x_ref[...].astype(jnp.bfloat16)
    z = jnp.dot(xb, w_ref[...], preferred_element_type=jnp.float32)
    zq_ref[:, 0:O_KV] = z[:, 0:O_KV].astype(jnp.bfloat16)
    zq_ref[:, ZQ_QI:ZQ_QI + LANES] = z[:, O_QI:O_MISC].astype(jnp.bfloat16)
    kvt_ref[0] = _dot_nt(wkvt_ref[...], xb)
    kvb_ref[...] = z[:, O_KV:O_QI].astype(jnp.bfloat16)
    win_ref[...] = z[:, O_WIN:O_QI]
    misc_ref[...] = z[:, O_MISC:O_PACKED]


def _odd_proj(x, w_packed_bf16, tm, seq_len):
    m, k = x.shape
    n_kv = O_WIN - O_KV
    widths = (O_KV + LANES, O_QI - O_KV, O_QI - O_WIN, LANES)
    dtypes = (jnp.bfloat16, jnp.bfloat16, jnp.float32, jnp.float32)
    row_spec = lambda n: pl.BlockSpec((tm, n), lambda i: (i, 0))
    tiles = seq_len // tm
    out_specs = [row_spec(widths[0]), pl.BlockSpec((1, n_kv, tm), lambda i: (i // tiles, 0, i % tiles))]
    out_specs += [row_spec(n) for n in widths[1:]]
    out_shape = [jax.ShapeDtypeStruct((m, widths[0]), dtypes[0]),
                 jax.ShapeDtypeStruct((m // seq_len, n_kv, seq_len), jnp.float32)]
    out_shape += [jax.ShapeDtypeStruct((m, n), dt) for n, dt in zip(widths[1:], dtypes[1:])]
    return pl.pallas_call(
        _odd_proj_kernel,
        grid=(m // tm,),
        in_specs=[pl.BlockSpec((tm, k), lambda i: (i, 0)),
                  pl.BlockSpec((k, O_PACKED), lambda i: (0, 0)),
                  pl.BlockSpec((n_kv, k), lambda i: (0, 0))],
        out_specs=out_specs,
        out_shape=out_shape,
        compiler_params=pltpu.CompilerParams(
            dimension_semantics=("arbitrary",), vmem_limit_bytes=VMEM_LIMIT),
        name="odd_proj",
    )(x, w_packed_bf16, w_packed_bf16[:, O_KV:O_WIN].T)


def _dot_nt(a, b):
    return lax.dot_general(a, b, (((1,), (1,)), ((), ())), preferred_element_type=jnp.float32)


def _flash_heads(q_heads, slopes, group, kv_ref, pos_ref, k_blk, v_blk, tk, lo, hi, bias_fn, skip_fn=None,
                 start_fn=None):
    f32, bf16 = jnp.float32, jnp.bfloat16
    if start_fn is None:
        start_fn = lambda kt: kt * tk
    tq = q_heads[0].shape[0]
    n_heads = len(q_heads)
    lane = lax.broadcasted_iota(jnp.int32, (1, LANES), 1)
    own_half = (lane // HEAD_DIM) == group
    feat_lane = lax.broadcasted_iota(jnp.int32, (tq, LANES), 1)
    slope_feat = [jnp.where(feat_lane == 0, BLK * sl, jnp.where(feat_lane == 1, sl, 0.0)).astype(bf16)
                  for sl in slopes]
    q_all = jnp.concatenate([jnp.concatenate(q_heads, axis=0), jnp.concatenate(slope_feat, axis=0)], axis=1)
    tn = (((0,), (0,)), ((), ()))

    def tile(kt, carry):
        m, acc = carry
        start = pl.multiple_of(start_fn(kt), LANES)
        rows = pl.ds(start, tk)
        k = kv_ref[0, rows, k_blk * LANES:(k_blk + 1) * LANES]
        v = kv_ref[0, rows, v_blk * LANES:(v_blk + 1) * LANES]
        v = jnp.where(own_half, v, jnp.ones_like(v))
        k_pos = jnp.concatenate([k, pos_ref[rows, :]], axis=1)
        s = _dot_nt(k_pos, q_all) + jnp.concatenate([bias_fn(kt)] * n_heads, axis=1)
        m_new = jnp.maximum(m, jnp.max(s, axis=0, keepdims=True))
        p = jnp.exp(s - m_new)
        acc = jnp.exp(m - m_new) * acc + lax.dot_general(v, p.astype(bf16), tn, preferred_element_type=f32)
        return m_new, acc

    def body(kt, carry):
        if skip_fn is None:
            return tile(kt, carry)
        return lax.cond(skip_fn(kt), lambda c: c, functools.partial(tile, kt), carry)

    m0 = jnp.full((1, n_heads * tq), -jnp.inf, f32)
    a0 = jnp.zeros((LANES, n_heads * tq), f32)
    _, acc = lax.fori_loop(lo, hi, body, (m0, a0))
    return acc


def _odd_attn_kernel(zq_ref, miscq_ref, kvb_ref, misck_ref, wc_ref, o_ref,
                     kc_ref, vc_ref, ki_ref, key_ref, dbias_ref, flag_ref, pos_ref, *, t_len, tq, tk, topk):
    f32, bf16, i32 = jnp.float32, jnp.bfloat16, jnp.int32
    nb = t_len // BLK
    qi_blk = pl.program_id(1)
    q0 = qi_blk * tq

    @pl.when(qi_blk == 0)
    def _():
        ck = kvb_ref[0, :, 0:LANES].astype(f32).reshape(nb, BLK, LANES)
        kc_ref[...] = jnp.sum(ck * wc_ref[0][None], axis=1).astype(bf16)
        cv = kvb_ref[0, :, LANES:2 * LANES].astype(f32).reshape(nb, BLK, LANES)
        vc = jnp.sum(cv * wc_ref[1][None], axis=1)
        for g in range(KV_GROUPS):
            half = vc[:, g * HEAD_DIM:(g + 1) * HEAD_DIM]
            vc_ref[g] = jnp.concatenate([half, half], axis=1).astype(bf16)
        ki_ref[...] = misck_ref[0, :, 0:IDX_DIM].astype(bf16)
        key_idx = lax.broadcasted_iota(i32, (t_len, LANES), 0)
        feat = lax.broadcasted_iota(i32, (t_len, LANES), 1)
        pos_ref[...] = jnp.where(feat == 0, key_idx // BLK, jnp.where(feat == 1, key_idx % BLK, 0)).astype(bf16)

    misc = miscq_ref[0]
    gates = jax.nn.sigmoid(misc[:, MISC_GATE:MISC_GATE + 3 * NSA_HEADS])
    t_col = q0 + lax.broadcasted_iota(i32, (tq, 1), 0)
    lane = lax.broadcasted_iota(i32, (1, LANES), 1)
    low_half = lane < HEAD_DIM
    heads_per_group = NSA_HEADS // KV_GROUPS

    def q_head(base, h):
        return zq_ref[0, :, base + h * LANES: base + (h + 1) * LANES]

    def pair(even, odd):
        return jnp.where(low_half, even, odd)

    def normalised_pair(acc_t, r_even, g):
        halves = []
        for r in (r_even, r_even + 1):
            cols = slice(r * tq, (r + 1) * tq)
            out = acc_t[g * HEAD_DIM:(g + 1) * HEAD_DIM, cols]
            den = acc_t[(1 - g) * HEAD_DIM:(1 - g) * HEAD_DIM + 1, cols]
            halves.append(out * (1.0 / den))
        return jnp.concatenate(halves, axis=0).T

    def gate_pair(j, c):
        a = gates[:, (2 * j) * 3 + c:(2 * j) * 3 + c + 1]
        b = gates[:, (2 * j + 1) * 3 + c:(2 * j + 1) * 3 + c + 1]
        return jnp.where(low_half, a, b)

    tn = (((0,), (0,)), ((), ()))
    t_lanes = q0 + lax.broadcasted_iota(i32, (1, tq), 1)
    jb = lax.broadcasted_iota(i32, (nb, 1), 0)
    dist_c = t_lanes - (jb * BLK + (BLK - 1))
    mask_c = dist_c >= 0
    dist_cf = dist_c.astype(f32)
    cur = t_lanes // BLK
    forced = (jb == 0) | (jb == cur) | (jb == cur - 1)
    admissible = jb <= cur
    jb_full = lax.broadcasted_iota(i32, (nb, tq), 0)
    o_cmp, sel_bias, sel_any = [], [], []
    for g in range(KV_GROUPS):
        imp = jnp.zeros((nb, tq), f32)
        for r in range(heads_per_group):
            h = g * heads_per_group + r
            s = _dot_nt(kc_ref[...], q_head(O_Q, h)) - ALIBI[h] * dist_cf
            s = jnp.where(mask_c, s, NEG_BIG)
            m = jnp.max(s, axis=0, keepdims=True)
            e = jnp.where(mask_c, jnp.exp(s - m), 0.0)
            p = e * (1.0 / jnp.maximum(jnp.sum(e, axis=0, keepdims=True), 1e-30))
            imp = imp + p
            o_cmp.append(lax.dot_general(p.astype(bf16), vc_ref[g], tn, preferred_element_type=f32))
        imp = jnp.where(forced, FORCE_SCORE, imp)
        imp = jnp.where(admissible, imp, -jnp.inf)
        sel = jnp.zeros((nb, tq), f32)
        for _ in range(min(NSA_TOPN, nb)):
            m = jnp.max(imp, axis=0, keepdims=True)
            first = jnp.min(jnp.where(imp == m, jb_full, nb), axis=0, keepdims=True)
            hit = jb_full == first
            sel = jnp.where(hit & (m > -jnp.inf), 1.0, sel)
            imp = jnp.where(hit, -jnp.inf, imp)
        sel_bias.append(jnp.where(sel > 0.0, 0.0, NEG_BIG).astype(bf16))
        sel_any.append(jnp.max(sel, axis=1, keepdims=True))

    n_sel_tiles = (q0 + tq + tk - 1) // tk
    blocks_per_tile = tk // BLK

    for g in range(KV_GROUPS):
        for j in range(nb // blocks_per_tile):
            hit = jnp.max(sel_any[g][j * blocks_per_tile:(j + 1) * blocks_per_tile, :])
            flag_ref[g, j] = (hit > 0.0).astype(i32)

    def sel_skip_fn(g):
        return lambda kt: flag_ref[g, kt] == 0

    def causal_bias(start, width):
        sp = start + lax.broadcasted_iota(i32, (width, 1), 0)
        return jnp.where(sp <= t_lanes, 0.0, NEG_BIG)

    def sel_bias_fn(g):
        def fn(kt):
            row = lax.broadcasted_iota(i32, (nb, tk), 0)
            col = lax.broadcasted_iota(i32, (nb, tk), 1)
            expand = jnp.where(row == kt * blocks_per_tile + col // BLK, 1.0, 0.0).astype(bf16)
            spread = lax.dot_general(expand, sel_bias[g], tn, preferred_element_type=f32)
            return spread + causal_bias(kt * tk, tk)
        return fn

    tkw = min(WINDOW + tq, t_len)
    win_start = jnp.clip(q0 - WINDOW, 0, t_len - tkw)

    def win_bias_fn(kt):
        dist = t_lanes - (win_start + lax.broadcasted_iota(i32, (tkw, 1), 0))
        return jnp.where((dist >= 0) & (dist < WINDOW), 0.0, NEG_BIG)

    qi_heads = [zq_ref[0, :, ZQ_QI + h * IDX_DIM: ZQ_QI + (h + 1) * IDX_DIM] for h in range(IDX_HEADS)]
    pick = jnp.where(lax.broadcasted_iota(i32, (8, LANES), 1) == MISC_WI + lax.broadcasted_iota(i32, (8, LANES), 0),
                     1.0, 0.0)
    wi_rows = lax.dot_general(pick, misc, (((1,), (1,)), ((), ())), preferred_element_type=f32,
                              precision=lax.Precision.HIGHEST)
    t_row = q0 + lax.broadcasted_iota(i32, (1, tq), 1)

    def fold8(x):
        x = x.reshape(tk // 8, 8, tq)
        while x.shape[0] > 1:
            half = x.shape[0] // 2
            x = x[:half] + x[half:]
        return x[0]

    def index_tile(kt, carry):
        start = pl.multiple_of(kt * tk, tk)
        kik = ki_ref[pl.ds(start, tk), :]
        tot = jnp.zeros((tk, tq), f32)
        for h in range(IDX_HEADS):
            sc = _dot_nt(kik, qi_heads[h]) * (IDX_DIM ** -0.5)
            tot = tot + jnp.maximum(sc, 0.0) * wi_rows[h:h + 1, :]
        tot = tot * (IDX_HEADS ** -0.5)
        tot = jnp.where(tot == 0.0, 0.0, tot)
        bits = lax.bitcast_convert_type(tot, i32)
        key = jnp.where(bits < 0, bits ^ jnp.int32(0x7FFFFFFF), bits)
        sp = start + lax.broadcasted_iota(i32, (tk, 1), 0)
        key_ref[kt] = jnp.where(sp <= t_row, key, INT_MIN)
        return carry

    lax.fori_loop(0, n_sel_tiles, index_tile, 0)

    def count_ge(trial):
        def body(kt, acc):
            return acc + fold8((key_ref[kt] >= trial).astype(i32))
        acc = lax.fori_loop(0, n_sel_tiles, body, jnp.zeros((8, tq), i32))
        return jnp.sum(acc, axis=0, keepdims=True)

    c_pos = count_ge(jnp.zeros((1, tq), i32))
    v0 = jnp.where(c_pos >= topk, 0, INT_MIN).astype(i32)
    c0 = jnp.where(c_pos >= topk, c_pos, n_sel_tiles * tk)

    def bit_step(i, carry):
        v, cge = carry
        trial = v | lax.shift_left(jnp.int32(1), 30 - i)
        c = count_ge(trial)
        ok = c >= topk
        return jnp.where(ok, trial, v), jnp.where(ok, c, cge)

    v_thr, c_ge = lax.fori_loop(0, 31, bit_step, (v0, c0))
    v_eff = jnp.maximum(v_thr, INT_MIN + 1)

    def write_plain(kt, carry):
        dbias_ref[kt] = jnp.where(key_ref[kt] >= v_eff, 0.0, NEG_BIG)
        return carry

    def plain_path():
        lax.fori_loop(0, n_sel_tiles, write_plain, 0)

    def tie_path():
        need = topk - count_ge(v_thr + 1)

        def count_tie_below(limit):
            def body(kt, acc):
                sp = kt * tk + lax.broadcasted_iota(i32, (tk, 1), 0)
                return acc + fold8(((key_ref[kt] == v_thr) & (sp < limit)).astype(i32))
            acc = lax.fori_loop(0, n_sel_tiles, body, jnp.zeros((8, tq), i32))
            return jnp.sum(acc, axis=0, keepdims=True)

        n_bits = max(1, (t_len - 1).bit_length())

        def idx_step(i, w):
            trial = w | lax.shift_left(jnp.int32(1), n_bits - 1 - i)
            return jnp.where(count_tie_below(trial) < need, trial, w)

        w_last = lax.fori_loop(0, n_bits, idx_step, jnp.zeros((1, tq), i32))

        def write_tie(kt, carry):
            sp = kt * tk + lax.broadcasted_iota(i32, (tk, 1), 0)
            key = key_ref[kt]
            take = (key > v_thr) | ((key == v_thr) & (sp <= w_last))
            dbias_ref[kt] = jnp.where(take & (key >= v_eff), 0.0, NEG_BIG)
            return carry

        lax.fori_loop(0, n_sel_tiles, write_tie, 0)

    lax.cond(jnp.max(c_ge) > topk, tie_path, plain_path)

    def dsa_bias_fn(kt):
        return dbias_ref[kt]

    for g in range(KV_GROUPS):
        hs = [g * heads_per_group + r for r in range(heads_per_group)]
        slopes = [ALIBI[h] for h in hs]
        qn = [q_head(O_Q, h) for h in hs]
        qd = [q_head(O_QD, h) for h in hs]
        a_s = _flash_heads(qn, slopes, g, kvb_ref, pos_ref, 2, 3, tk, 0, n_sel_tiles, sel_bias_fn(g), sel_skip_fn(g))
        a_w = _flash_heads(qn, slopes, g, kvb_ref, pos_ref, 6, 7, tkw, 0, 1, win_bias_fn,
                           start_fn=lambda kt: win_start)
        a_d = _flash_heads(qd, slopes, g, kvb_ref, pos_ref, 4, 5, tk, 0, n_sel_tiles, dsa_bias_fn)
        for jj in range(heads_per_group // 2):
            j = hs[2 * jj] // 2
            o_n = (gate_pair(j, 0) * pair(o_cmp[2 * j], o_cmp[2 * j + 1])
                   + gate_pair(j, 1) * normalised_pair(a_s, 2 * jj, g)
                   + gate_pair(j, 2) * normalised_pair(a_w, 2 * jj, g))
            o_ref[0, :, j * LANES:(j + 1) * LANES] = o_n.astype(o_ref.dtype)
            o_d = normalised_pair(a_d, 2 * jj, g)
            o_ref[0, :, HALF_W + j * LANES:HALF_W + (j + 1) * LANES] = o_d.astype(o_ref.dtype)


def _odd_attn_prompt(zq, misc, kvb, wc2, bsz, t_len, tq=256, tk=512):
    tk = min(tk, t_len)
    topk = min(DSA_TOPK_MAX, t_len // 4)
    nb = t_len // BLK
    n_kt = t_len // tk
    zq3 = zq.reshape(bsz, t_len, zq.shape[1])
    misc3 = misc.reshape(bsz, t_len, LANES)
    kvb3 = kvb.reshape(bsz, t_len, kvb.shape[1])
    kern = functools.partial(_odd_attn_kernel, t_len=t_len, tq=tq, tk=tk, topk=topk)
    out = pl.pallas_call(
        kern,
        grid=(bsz, t_len // tq),
        in_specs=[pl.BlockSpec((1, tq, zq.shape[1]), lambda b, i: (b, i, 0)),
                  pl.BlockSpec((1, tq, LANES), lambda b, i: (b, i, 0)),
                  pl.BlockSpec((1, t_len, kvb.shape[1]), lambda b, i: (b, 0, 0)),
                  pl.BlockSpec((1, t_len, LANES), lambda b, i: (b, 0, 0)),
                  pl.BlockSpec((2, BLK, LANES), lambda b, i: (0, 0, 0))],
        out_specs=pl.BlockSpec((1, tq, 2 * HALF_W), lambda b, i: (b, i, 0)),
        out_shape=jax.ShapeDtypeStruct((bsz, t_len, 2 * HALF_W), jnp.bfloat16),
        scratch_shapes=[pltpu.VMEM((nb, LANES), jnp.bfloat16),
                        pltpu.VMEM((KV_GROUPS, nb, LANES), jnp.bfloat16),
                        pltpu.VMEM((t_len, IDX_DIM), jnp.bfloat16),
                        pltpu.VMEM((n_kt, tk, tq), jnp.int32),
                        pltpu.VMEM((n_kt, tk, tq), jnp.float32),
                        pltpu.SMEM((KV_GROUPS, n_kt), jnp.int32),
                        pltpu.VMEM((t_len, LANES), jnp.bfloat16)],
        compiler_params=pltpu.CompilerParams(
            dimension_semantics=("arbitrary", "arbitrary"), vmem_limit_bytes=VMEM_LIMIT),
        name="odd_attn_prompt",
    )(zq3, misc3, kvb3, misc3, wc2)
    return out.reshape(bsz * t_len, 2 * HALF_W)


PAGE = 128
ISC_ROWS = 24


SAMPLE_SEQS = 1


def _odd_sample_kernel(pt_ref, *refs, n_pages, topk, nbp):
    del pt_ref
    for u in range(SAMPLE_SEQS):
        _odd_sample_one(u, *refs, n_pages=n_pages, topk=topk, nbp=nbp)


def _odd_sample_one(u, qn_ref, qd_ref, qi_ref, wi_ref, gate_ref, newb_ref, newki_ref, neww_ref,
                    win_ref, wc_ref, wct_ref, *rest, n_pages, topk, nbp):
    f32, bf16, i32 = jnp.float32, jnp.bfloat16, jnp.int32
    hi_prec = lax.Precision.HIGHEST
    all_pages = SAMPLE_SEQS * n_pages
    kv_refs = rest[u * n_pages:(u + 1) * n_pages]
    idx_refs = rest[all_pages + u * n_pages:all_pages + (u + 1) * n_pages]
    o_ref, wout_ref, isc_all_ref = rest[2 * all_pages:]
    isc_ref = isc_all_ref.at[u]
    past = n_pages * PAGE
    nb = past // BLK + 1
    row8 = lax.broadcasted_iota(i32, (8, 1), 0)
    lane = lax.broadcasted_iota(i32, (1, LANES), 1)
    slope8 = jnp.zeros((8, 1), f32)
    for h in range(NSA_HEADS):
        slope8 = jnp.where(row8 == h, ALIBI[h], slope8)
    low_group = row8 < NSA_HEADS // KV_GROUPS

    def pick_half(x):
        return jnp.where(low_group, x[:, 0:HEAD_DIM], x[:, HEAD_DIM:2 * HEAD_DIM])

    qn8, qd8 = qn_ref[u], qd_ref[u]
    qi8, wi8 = qi_ref[u], wi_ref[u]
    newb = newb_ref[u]
    gates = jax.nn.sigmoid(gate_ref[u])

    def new_score(q8, k_row):
        return jnp.sum(q8.astype(f32) * k_row.astype(f32), axis=-1, keepdims=True) + slope8 * float(past)

    def feat(p, slot):
        return kv_refs[p][0, slot * LANES:(slot + 1) * LANES, :]

    s_sel, s_dsa = [], []
    key_row = lax.broadcasted_iota(i32, (PAGE, nbp), 0)
    blk_col = lax.broadcasted_iota(i32, (PAGE, nbp), 1)
    kct = jnp.zeros((LANES, nbp), f32)
    vct = jnp.zeros((LANES, nbp), f32)
    for p in range(n_pages):
        pool = jnp.where(blk_col == 2 * p + key_row // BLK, 1.0, 0.0).astype(bf16)
        kct = kct + jnp.dot((feat(p, 0) * wct_ref[0]).astype(bf16), pool, preferred_element_type=f32)
        vct = vct + jnp.dot((feat(p, 1) * wct_ref[1]).astype(bf16), pool, preferred_element_type=f32)
        pos = (p * PAGE + lane).astype(f32)
        s_sel.append(jnp.dot(qn8, feat(p, 2).astype(bf16), preferred_element_type=f32) + slope8 * pos)
        s_dsa.append(jnp.dot(qd8, feat(p, 4).astype(bf16), preferred_element_type=f32) + slope8 * pos)
        ii = jnp.dot(qi8, idx_refs[p][0].astype(bf16), preferred_element_type=f32) * (IDX_DIM ** -0.5)
        isc_ref[p:p + 1, :] = jnp.sum(jnp.maximum(ii, 0.0) * wi8, axis=0, keepdims=True) * (IDX_HEADS ** -0.5)
    first_row = row8 == 0
    place = jnp.where(first_row & (lax.broadcasted_iota(i32, (8, nbp), 1) == nb - 1), 1.0, 0.0)
    tn = (((0,), (0,)), ((), ()))
    new_k = jnp.where(first_row, newb[:, 0:LANES].astype(f32) * wc_ref[0, 0:1, :], 0.0)
    new_v = jnp.where(first_row, newb[:, LANES:2 * LANES].astype(f32) * wc_ref[1, 0:1, :], 0.0)
    kct = kct + lax.dot_general(new_k.astype(bf16), place.astype(bf16), tn, preferred_element_type=f32)
    vct = vct + lax.dot_general(new_v.astype(bf16), place.astype(bf16), tn, preferred_element_type=f32)
    ii_new = jnp.sum(qi8.astype(f32) * newki_ref[u].astype(bf16).astype(f32), axis=-1, keepdims=True) * (IDX_DIM ** -0.5)
    isc_new = jnp.sum(jnp.maximum(ii_new, 0.0) * wi8, axis=0, keepdims=True) * (IDX_HEADS ** -0.5)
    isc_ref[n_pages:n_pages + 1, :] = jnp.where(lane == 0, isc_new, -jnp.inf)
    isc_ref[n_pages + 1:ISC_ROWS, :] = jnp.full((ISC_ROWS - n_pages - 1, LANES), -jnp.inf, f32)

    jb = lax.broadcasted_iota(i32, (1, nbp), 1)
    dist_c = past - (jb * BLK + (BLK - 1))
    mask_c = (dist_c >= 0) & (jb < nb)
    s_c = jnp.dot(qn8, kct.astype(bf16), preferred_element_type=f32) - slope8 * dist_c.astype(f32)
    s_c = jnp.where(mask_c, s_c, NEG_BIG)
    m_c = jnp.max(s_c, axis=-1, keepdims=True)
    e_c = jnp.where(mask_c, jnp.exp(s_c - m_c), 0.0)
    p_c = e_c * (1.0 / jnp.maximum(jnp.sum(e_c, axis=-1, keepdims=True), 1e-30))
    o_c = pick_half(_dot_nt(p_c.astype(bf16), vct.astype(bf16)))
    imp_lo = jnp.sum(jnp.where(low_group, p_c, 0.0), axis=0, keepdims=True)
    imp_hi = jnp.sum(jnp.where(low_group, 0.0, p_c), axis=0, keepdims=True)
    imp = jnp.where(low_group, imp_lo, imp_hi)
    cur = past // BLK
    imp = jnp.where((jb == 0) | (jb == cur) | (jb == cur - 1), FORCE_SCORE, imp)
    imp = jnp.where(jb <= cur, imp, -jnp.inf)
    rank = jnp.zeros((8, nbp), i32)
    for j in range(nb):
        other = imp[:, j:j + 1]
        rank = rank + ((other > imp) | ((other == imp) & (j < jb))).astype(i32)
    sel = jnp.where((rank < min(NSA_TOPN, nb)) & (imp > -jnp.inf), 1.0, 0.0)

    tot = isc_ref[...]
    tot = jnp.where(tot == 0.0, 0.0, tot)
    bits = lax.bitcast_convert_type(tot, i32)
    key = jnp.where(bits < 0, bits ^ jnp.int32(0x7FFFFFFF), bits)
    krow = lax.broadcasted_iota(i32, (ISC_ROWS, LANES), 0)
    kidx = krow * LANES + lax.broadcasted_iota(i32, (ISC_ROWS, LANES), 1)
    key = jnp.where(kidx <= past, key, INT_MIN)

    def total(x):
        return jnp.sum(jnp.sum(x, axis=1, keepdims=True), axis=0, keepdims=True)

    def count_ge(trial):
        return total((key >= trial).astype(i32))

    c_pos = count_ge(jnp.zeros((1, 1), i32))
    v_thr = jnp.where(c_pos >= topk, 0, INT_MIN).astype(i32)
    for shift, width in ((27, 4), (23, 4), (19, 4), (15, 4), (11, 4), (7, 4), (3, 4), (0, 3)):
        digit = jnp.zeros((1, 1), i32)
        for j in range(1, 1 << width):
            ok = count_ge(v_thr | jnp.int32(j << shift)) >= topk
            digit = digit + ok.astype(i32)
        v_thr = v_thr | lax.shift_left(digit, jnp.int32(shift))
    v_eff = jnp.maximum(v_thr, INT_MIN + 1)

    def tie_mask():
        need = topk - count_ge(v_thr + 1)
        tie = key == v_thr
        w_last = jnp.zeros((1, 1), i32)
        for shift in (8, 4, 0):
            digit = jnp.zeros((1, 1), i32)
            for j in range(1, 16):
                below = total((tie & (kidx < (w_last | jnp.int32(j << shift)))).astype(i32))
                digit = digit + (below < need).astype(i32)
            w_last = w_last | lax.shift_left(digit, jnp.int32(shift))
        take = (key > v_thr) | (tie & (kidx <= w_last))
        return jnp.where(take & (key >= v_eff), 1.0, 0.0)

    def plain_mask():
        return jnp.where(key >= v_eff, 1.0, 0.0)

    has_surplus_tie = count_ge(v_thr)[0, 0] > topk
    dsel = lax.cond(has_surplus_tie, tie_mask, plain_mask)

    def attend(tiles, masks, s_new, new_ok, v_tile, v_new):
        tiles = [jnp.where(mk, t, NEG_BIG) for t, mk in zip(tiles, masks)]
        s_new = jnp.where(new_ok, s_new, NEG_BIG)
        m = s_new
        for t in tiles:
            m = jnp.maximum(m, jnp.max(t, axis=-1, keepdims=True))
        l = jnp.exp(s_new - m)
        acc = l * v_new.astype(f32)
        for j, t in enumerate(tiles):
            e = jnp.exp(t - m)
            l = l + jnp.sum(e, axis=-1, keepdims=True)
            acc = acc + _dot_nt(e.astype(bf16), v_tile(j))
        return pick_half(acc * (1.0 / l))

    sel_masks = [jnp.where(lane < BLK, sel[:, 2 * p:2 * p + 1], sel[:, 2 * p + 1:2 * p + 2]) > 0.0
                 for p in range(n_pages)]
    o_s = attend(s_sel, sel_masks, new_score(qn8, newb[:, 2 * LANES:3 * LANES]), sel[:, nb - 1:nb] > 0.0,
                 lambda p: feat(p, 3).astype(bf16), newb[:, 3 * LANES:4 * LANES])
    dsa_masks = [dsel[p:p + 1, :] > 0.0 for p in range(n_pages)]
    o_d = attend(s_dsa, dsa_masks, new_score(qd8, newb[:, 4 * LANES:5 * LANES]), dsel[n_pages:n_pages + 1, 0:1] > 0.0,
                 lambda p: feat(p, 5).astype(bf16), newb[:, 5 * LANES:6 * LANES])
    n_win = win_ref.shape[2]
    wlane = lax.broadcasted_iota(i32, (1, n_win), 1)
    wpos = past - n_win + wlane
    s_w = jnp.dot(qn8, win_ref[u, 0:LANES, :].astype(bf16), preferred_element_type=f32) + slope8 * wpos.astype(f32)
    w_ok = (past - wpos < WINDOW) & (wpos >= 0)
    o_w = attend([s_w], [w_ok], new_score(qn8, newb[:, 6 * LANES:7 * LANES]), True,
                 lambda p: win_ref[u, LANES:2 * LANES, :].astype(bf16), newb[:, 7 * LANES:8 * LANES])
    o_ref[u, 0:NSA_HEADS, :] = gates[:, 0:1] * o_c + gates[:, 1:2] * o_s + gates[:, 2:3] * o_w
    o_ref[u, NSA_HEADS:2 * NSA_HEADS, :] = o_d
    last = jnp.where(first_row & (lax.broadcasted_iota(i32, (8, n_win), 1) == n_win - 1), 1.0, 0.0)
    new_w = jnp.where(first_row, neww_ref[u], 0.0)
    placed = lax.dot_general(new_w, last, tn, preferred_element_type=f32, precision=hi_prec)
    wout_ref[u] = jnp.where(wlane == n_win - 1, placed, pltpu.roll(win_ref[u], n_win - 1, 1))


def _odd_attn_sample(zq, misc, kvb, win_new, cache_kv, cache_idx, cache_win, page_table, w_cmp):
    f32, bf16 = jnp.float32, jnp.bfloat16
    s_n, n_pages = page_table.shape
    n_pool = cache_kv.shape[0]
    n_win = cache_win.shape[1]
    past = n_pages * PAGE
    topk = min(DSA_TOPK_MAX, (past + 1) // 4)
    nbp = _round_up(past // BLK + 1, 8)
    wc2 = jnp.concatenate([w_cmp, w_cmp], axis=-1)
    wct = jnp.tile(jnp.swapaxes(w_cmp, 1, 2), (1, KV_GROUPS, PAGE // BLK))
    qn = zq[:, O_Q:O_QD].reshape(s_n, NSA_HEADS, LANES)
    qd = zq[:, O_QD:ZQ_QI].reshape(s_n, DSA_HEADS, LANES)
    qi = jnp.pad(zq[:, ZQ_QI:ZQ_QI + LANES].reshape(s_n, IDX_HEADS, IDX_DIM), ((0, 0), (0, 8 - IDX_HEADS), (0, 0)))
    wi = jnp.pad(misc[:, MISC_WI:MISC_WI + IDX_HEADS].reshape(s_n, IDX_HEADS, 1), ((0, 0), (0, 8 - IDX_HEADS), (0, 0)))
    gts = misc[:, MISC_GATE:MISC_GATE + 3 * NSA_HEADS].reshape(s_n, NSA_HEADS, 3)
    kv_pages = jnp.transpose(cache_kv, (0, 2, 3, 4, 1)).reshape(n_pool, KV_SLOTS * LANES, PAGE)
    idx_pages = jnp.transpose(cache_idx, (0, 2, 1))
    win_t = jnp.transpose(cache_win, (0, 2, 3, 4, 1)).reshape(s_n, 2 * LANES, n_win)
    n_u = SAMPLE_SEQS
    per_seq = lambda shape: pl.BlockSpec((n_u,) + shape, lambda b, pt: (b, 0, 0))
    page_spec = lambda rows, u, p: pl.BlockSpec((1, rows, PAGE), lambda b, pt, u=u, p=p: (pt[b * n_u + u, p], 0, 0))
    seq_pages = [(u, p) for u in range(n_u) for p in range(n_pages)]
    in_specs = ([per_seq((NSA_HEADS, LANES)), per_seq((DSA_HEADS, LANES)), per_seq((8, IDX_DIM)),
                 per_seq((8, 1)), per_seq((NSA_HEADS, 3)), per_seq((1, 8 * LANES)), per_seq((1, IDX_DIM)),
                 per_seq((1, 2 * LANES)), per_seq((2 * LANES, n_win)),
                 pl.BlockSpec((2, BLK, LANES), lambda b, pt: (0, 0, 0)),
                 pl.BlockSpec((2, LANES, PAGE), lambda b, pt: (0, 0, 0))]
                + [page_spec(KV_SLOTS * LANES, u, p) for u, p in seq_pages]
                + [page_spec(IDX_DIM, u, p) for u, p in seq_pages])
    kern = functools.partial(_odd_sample_kernel, n_pages=n_pages, topk=topk, nbp=nbp)
    o, win_out = pl.pallas_call(
        kern,
        grid_spec=pltpu.PrefetchScalarGridSpec(
            num_scalar_prefetch=1,
            grid=(s_n // n_u,),
            in_specs=in_specs,
            out_specs=[per_seq((2 * NSA_HEADS, HEAD_DIM)), per_seq((2 * LANES, n_win))],
            scratch_shapes=[pltpu.VMEM((n_u, ISC_ROWS, LANES), f32)]),
        out_shape=[jax.ShapeDtypeStruct((s_n, 2 * NSA_HEADS, HEAD_DIM), f32),
                   jax.ShapeDtypeStruct((s_n, 2 * LANES, n_win), f32)],
        compiler_params=pltpu.CompilerParams(
            dimension_semantics=("arbitrary",), vmem_limit_bytes=VMEM_LIMIT),
        name="odd_attn_sample",
    )(page_table, qn, qd, qi, wi, gts, kvb.reshape(s_n, 1, 8 * LANES), misc[:, 0:IDX_DIM].reshape(s_n, 1, IDX_DIM),
      win_new.reshape(s_n, 1, 2 * LANES), win_t, wc2, wct,
      *([kv_pages] * len(seq_pages)), *([idx_pages] * len(seq_pages)))
    win_out = jnp.transpose(win_out.reshape(s_n, 2, KV_GROUPS, HEAD_DIM, n_win), (0, 4, 1, 2, 3))
    return o.reshape(s_n, 2 * HALF_W), win_out


def _split(z, sizes):
    cuts = [int(c) for c in np.cumsum(sizes)[:-1]]
    return jnp.split(z, cuts, axis=-1)


def _pad_cols(w, n):
    return jnp.pad(w, ((0, 0), (0, n - w.shape[1])))


def kernel(x_prompt, x_sample, state_conv, state_C, state_n, state_m, cache_kv, cache_idx, cache_win, page_table,
           w_in_e, gate_b_e, conv_w, conv_b, w_out_e, w_in_o, w_cmp, w_out_o, w_up, w_down, ln_g, ln_b):
    f32, bf16 = jnp.float32, jnp.bfloat16
    bp, tp, d = x_prompt.shape
    dbs = x_sample.shape[0]
    keep = cache_win.shape[2]
    yp = x_prompt.reshape(bp * tp, d)
    ys = x_sample.reshape(dbs, d)
    tm_p, tm_s = 512, dbs
    outs = {}
    for layer in range(DEPTH):
        i = layer // 2
        if layer % 2 == 0:
            w_in = _pad_cols(w_in_e[i], E_PACKED).astype(bf16)
            w_out = w_out_e[i].astype(bf16)
            ya, qkv, og, gts, tails = _even_proj(yp, w_in, conv_w[i], conv_b[i], tm_p, tp)
            yb, c_fin, n_fin, m_fin = _mlstm_prompt(qkv, og, gts, gate_b_e[i], bp, tp)
            cvp = tails.reshape(bp, tp // tm_p, CONV_TAIL, HALF_W)[:, -1, CONV_TAIL - (CONV_W - 1):]
            ya_s, yb_s, cvs, c_s, n_s, m_s = _even_sample(_proj(ys, w_in, tm_s), gate_b_e[i], conv_w[i], conv_b[i],
                                                          state_conv[i], state_C[i], state_n[i], state_m[i])
            outs['conv'] = (cvp[None], cvs[None])
            outs['c'] = (c_fin[None], c_s[None])
            outs['n'] = (n_fin[None], n_s[None])
            outs['m'] = (m_fin[:, :, 0][None], m_s[:, :, 0][None])
            yp = _outproj2_ln(ya, yb.reshape(bp * tp, HALF_W), w_out, yp, ln_g[layer, 0], ln_b[layer, 0], tm_p)
            ys = _outproj2_ln(ya_s, yb_s, w_out, ys, ln_g[layer, 0], ln_b[layer, 0], tm_s)
        else:
            w_out = w_out_o[i].astype(bf16)
            w_packed = _pack_w_in_o(w_in_o[i]).astype(bf16)
            wc2 = jnp.concatenate([w_cmp[i], w_cmp[i]], axis=-1)
            rows_first = lambda kvt: jnp.transpose(
                kvt.reshape(kvt.shape[0], KV_SLOTS, KV_GROUPS, HEAD_DIM, kvt.shape[2]), (0, 4, 1, 2, 3))
            zq, kvt_p, kvb, winp, miscp = _odd_proj(yp, w_packed, tm_p, tp)
            mp = _odd_attn_prompt(zq, miscp, kvb, wc2, bp, tp)
            kvp = rows_first(kvt_p)
            ixp = miscp[:, :IDX_DIM].reshape(bp, tp, IDX_DIM)
            wnp = winp.reshape(bp, tp, 2, KV_GROUPS, HEAD_DIM)[:, tp - keep:]
            zq_s, kvt_s, kvb_s, win_s, misc_s = _odd_proj(ys, w_packed, tm_s, dbs)
            ms, wns = _odd_attn_sample(zq_s, misc_s, kvb_s, win_s, cache_kv[i], cache_idx[i], cache_win[i],
                                       page_table, w_cmp[i])
            kvs = rows_first(kvt_s).reshape(dbs, 1, KV_SLOTS, KV_GROUPS, HEAD_DIM)
            ixs = misc_s[:, :IDX_DIM].reshape(dbs, 1, IDX_DIM)
            outs['kv'] = (kvp[None], kvs[None])
            outs['idx'] = (ixp[None], ixs[None])
            outs['win'] = (wnp[None], wns[None])
            yp = _outproj_ln(mp, w_out, yp, ln_g[layer, 0], ln_b[layer, 0], tm_p)
            ys = _outproj_ln(ms, w_out, ys, ln_g[layer, 0], ln_b[layer, 0], tm_s)
        wu, wd = w_up[layer].astype(bf16), w_down[layer].astype(bf16)
        yp = _mlp_ln(yp, wu, wd, ln_g[layer, 1], ln_b[layer, 1], 2 * tm_p, 1024)
        ys = _mlp_ln(ys, wu, wd, ln_g[layer, 1], ln_b[layer, 1], tm_s, 1024)
    return (yp.reshape(bp, tp, d), ys.reshape(dbs, 1, d),
            outs['conv'][0], outs['conv'][1], outs['c'][0], outs['c'][1],
            outs['n'][0], outs['n'][1], outs['m'][0], outs['m'][1],
            outs['kv'][0], outs['kv'][1], outs['idx'][0], outs['idx'][1],
            outs['win'][0], outs['win'][1])
```

```python
import functools

import jax
import jax.numpy as jnp
import numpy as np
from jax import lax
from jax.experimental import pallas as pl
from jax.experimental.pallas import tpu as pltpu

D_MODEL = 1024
DEPTH = 2
HALF_W = 512
D_FF = 4096
CONV_W = 3
ML_HEADS = 4
ML_DIM = 128
HEAD_DIM = 64
KV_GROUPS = 2
NSA_HEADS = 8
DSA_HEADS = 8
BLK = 64
NSA_TOPN = 8
WINDOW = 256
IDX_HEADS = 4
IDX_DIM = 32
DSA_TOPK_MAX = 256
KV_SLOTS = 6
FORCE_SCORE = 1e4
ALPHA = (2.0 * DEPTH) ** 0.25
LN_EPS = 1e-5
O_SIZES = (512, 256, 256, 256, 24, 512, 256, 128, 32, 4)
LANES = 128
VMEM_LIMIT = 48 * 1024 * 1024


def _round_up(n, m):
    return -(-n // m) * m


def _proj_kernel(x_ref, w_ref, o_ref):
    o_ref[...] = jnp.dot(x_ref[...].astype(jnp.bfloat16), w_ref[...],
                         preferred_element_type=jnp.float32)


def _proj(x, w_bf16, tm):
    m, k = x.shape
    n = w_bf16.shape[1]
    tn = n
    for cand in (1024, 768, 512, 256, 128):
        if n % cand == 0:
            tn = cand
            break
    return pl.pallas_call(
        _proj_kernel,
        grid=(n // tn, m // tm),
        in_specs=[pl.BlockSpec((tm, k), lambda j, i: (i, 0)),
                  pl.BlockSpec((k, tn), lambda j, i: (0, j))],
        out_specs=pl.BlockSpec((tm, tn), lambda j, i: (i, j)),
        out_shape=jax.ShapeDtypeStruct((m, n), jnp.float32),
        compiler_params=pltpu.CompilerParams(
            dimension_semantics=("arbitrary", "arbitrary"), vmem_limit_bytes=VMEM_LIMIT),
        name="proj",
    )(x, w_bf16)


def _layer_norm_rows(v, g, b):
    mu = jnp.mean(v, axis=-1, keepdims=True)
    d = v - mu
    var = jnp.mean(d * d, axis=-1, keepdims=True)
    return d * lax.rsqrt(var + LN_EPS) * g + b


def _outproj_ln_kernel(y_ref, w_ref, x_ref, g_ref, b_ref, o_ref):
    mix = jnp.dot(y_ref[...].astype(jnp.bfloat16), w_ref[...], preferred_element_type=jnp.float32)
    o_ref[...] = _layer_norm_rows(ALPHA * x_ref[...] + mix, g_ref[...], b_ref[...])


def _outproj_ln(y, w_bf16, x, g, b, tm):
    m, d = x.shape
    k = y.shape[1]
    return pl.pallas_call(
        _outproj_ln_kernel,
        grid=(m // tm,),
        in_specs=[pl.BlockSpec((tm, k), lambda i: (i, 0)),
                  pl.BlockSpec((k, d), lambda i: (0, 0)),
                  pl.BlockSpec((tm, d), lambda i: (i, 0)),
                  pl.BlockSpec((1, d), lambda i: (0, 0)),
                  pl.BlockSpec((1, d), lambda i: (0, 0))],
        out_specs=pl.BlockSpec((tm, d), lambda i: (i, 0)),
        out_shape=jax.ShapeDtypeStruct((m, d), jnp.float32),
        compiler_params=pltpu.CompilerParams(
            dimension_semantics=("arbitrary",), vmem_limit_bytes=VMEM_LIMIT),
        name="outproj_ln",
    )(y, w_bf16, x, g.reshape(1, d), b.reshape(1, d))


def _mlp_ln_kernel(x_ref, wu_ref, wd_ref, g_ref, b_ref, o_ref, acc_ref):
    f = pl.program_id(1)

    @pl.when(f == 0)
    def _():
        acc_ref[...] = jnp.zeros_like(acc_ref)

    up = jnp.dot(x_ref[...].astype(jnp.bfloat16), wu_ref[...], preferred_element_type=jnp.float32)
    act = jnp.square(jnp.maximum(up, 0.0))
    acc_ref[...] += jnp.dot(act.astype(jnp.bfloat16), wd_ref[...], preferred_element_type=jnp.float32)

    @pl.when(f == pl.num_programs(1) - 1)
    def _():
        o_ref[...] = _layer_norm_rows(ALPHA * x_ref[...] + acc_ref[...], g_ref[...], b_ref[...])


def _mlp_ln(x, wu_bf16, wd_bf16, g, b, tm, tf):
    m, d = x.shape
    ff = wu_bf16.shape[1]
    return pl.pallas_call(
        _mlp_ln_kernel,
        grid=(m // tm, ff // tf),
        in_specs=[pl.BlockSpec((tm, d), lambda i, f: (i, 0)),
                  pl.BlockSpec((d, tf), lambda i, f: (0, f)),
                  pl.BlockSpec((tf, d), lambda i, f: (f, 0)),
                  pl.BlockSpec((1, d), lambda i, f: (0, 0)),
                  pl.BlockSpec((1, d), lambda i, f: (0, 0))],
        out_specs=pl.BlockSpec((tm, d), lambda i, f: (i, 0)),
        out_shape=jax.ShapeDtypeStruct((m, d), jnp.float32),
        scratch_shapes=[pltpu.VMEM((tm, d), jnp.float32)],
        compiler_params=pltpu.CompilerParams(
            dimension_semantics=("arbitrary", "arbitrary"), vmem_limit_bytes=VMEM_LIMIT),
        name="mlp_ln",
    )(x, wu_bf16, wd_bf16, g.reshape(1, d), b.reshape(1, d))


E_QKV = 3 * HALF_W
E_OG = 6 * HALF_W
E_GATE = 7 * HALF_W
E_PACKED = 7 * HALF_W + LANES
CONV_TAIL = 8


def _even_proj_kernel(x_ref, w_ref, cw_ref, cb_ref, ya_ref, qkv_ref, og_ref, gate_ref, tail_ref, carry_ref,
                      *, tiles_per_seq):
    f32, bf16 = jnp.float32, jnp.bfloat16
    tm = x_ref.shape[0]

    @pl.when(pl.program_id(0) % tiles_per_seq == 0)
    def _():
        carry_ref[...] = jnp.zeros_like(carry_ref)

    z = jnp.dot(x_ref[...].astype(bf16), w_ref[...], preferred_element_type=f32)
    u = z[:, 2 * HALF_W:3 * HALF_W] * z[:, 0:HALF_W]
    prev = carry_ref[...]
    row = lax.broadcasted_iota(jnp.int32, (tm, 1), 0)
    conv = cb_ref[...] + cw_ref[CONV_W - 1:CONV_W, :] * u
    for back in range(1, CONV_W):
        shifted = pltpu.roll(u, back, 0)
        for r in range(back):
            shifted = jnp.where(row == r, prev[CONV_TAIL - back + r:CONV_TAIL - back + r + 1, :], shifted)
        conv = conv + cw_ref[CONV_W - 1 - back:CONV_W - back, :] * shifted
    ya_ref[...] = (z[:, HALF_W:2 * HALF_W] * conv).astype(bf16)
    tail = u[tm - CONV_TAIL:]
    carry_ref[...] = tail
    tail_ref[0] = tail
    qkv_ref[:, 0:HALF_W] = z[:, E_QKV:E_QKV + HALF_W].astype(bf16)
    qkv_ref[:, HALF_W:2 * HALF_W] = (z[:, E_QKV + HALF_W:E_QKV + 2 * HALF_W] * (ML_DIM ** -0.5)).astype(bf16)
    qkv_ref[:, 2 * HALF_W:3 * HALF_W] = z[:, E_QKV + 2 * HALF_W:E_OG].astype(bf16)
    og_ref[...] = z[:, E_OG:E_GATE]
    gate_ref[...] = z[:, E_GATE:E_PACKED]


def _even_proj(x, w_bf16, conv_w, conv_b, tm, seq_len):
    m, k = x.shape
    widths = (HALF_W, 3 * HALF_W, HALF_W, LANES)
    dtypes = (jnp.bfloat16, jnp.bfloat16, jnp.float32, jnp.float32)
    kern = functools.partial(_even_proj_kernel, tiles_per_seq=seq_len // tm)
    return pl.pallas_call(
        kern,
        grid=(m // tm,),
        in_specs=[pl.BlockSpec((tm, k), lambda i: (i, 0)),
                  pl.BlockSpec((k, E_PACKED), lambda i: (0, 0)),
                  pl.BlockSpec((CONV_W, HALF_W), lambda i: (0, 0)),
                  pl.BlockSpec((1, HALF_W), lambda i: (0, 0))],
        out_specs=[pl.BlockSpec((tm, n), lambda i: (i, 0)) for n in widths]
        + [pl.BlockSpec((1, CONV_TAIL, HALF_W), lambda i: (i, 0, 0))],
        out_shape=[jax.ShapeDtypeStruct((m, n), dt) for n, dt in zip(widths, dtypes)]
        + [jax.ShapeDtypeStruct((m // tm, CONV_TAIL, HALF_W), jnp.float32)],
        scratch_shapes=[pltpu.VMEM((CONV_TAIL, HALF_W), jnp.float32)],
        compiler_params=pltpu.CompilerParams(
            dimension_semantics=("arbitrary",), vmem_limit_bytes=VMEM_LIMIT),
        name="even_proj",
    )(x, w_bf16, conv_w, conv_b.reshape(1, HALF_W))


def _mlstm_kernel(qkv_ref, og_ref, gate_ref, gb_ref, yb_ref, c_ref, n_ref, m_ref):
    f32, bf16 = jnp.float32, jnp.bfloat16
    L = qkv_ref.shape[1]

    @pl.when(pl.program_id(1) == 0)
    def _():
        c_ref[...] = jnp.zeros_like(c_ref)
        n_ref[...] = jnp.zeros_like(n_ref)
        m_ref[...] = jnp.zeros_like(m_ref)

    pre = gate_ref[0] + gb_ref[...]
    lf = jax.nn.log_sigmoid(pre)
    row = lax.broadcasted_iota(jnp.int32, (L, L), 0)
    col = lax.broadcasted_iota(jnp.int32, (L, L), 1)
    causal = col <= row
    tri = jnp.where(causal, 1.0, 0.0).astype(f32)
    b_all = jnp.dot(tri, lf, preferred_element_type=f32, precision=lax.Precision.HIGHEST)
    pre_t = pre.T
    b_t = b_all.T
    for h in range(ML_HEADS):
        q = qkv_ref[0, :, h * ML_DIM:(h + 1) * ML_DIM]
        k = qkv_ref[0, :, HALF_W + h * ML_DIM:HALF_W + (h + 1) * ML_DIM]
        v = qkv_ref[0, :, 2 * HALF_W + h * ML_DIM:2 * HALF_W + (h + 1) * ML_DIM]
        ig_col = pre[:, h:h + 1]
        b_col = b_all[:, ML_HEADS + h:ML_HEADS + h + 1]
        a_row = pre_t[h:h + 1, :] - b_t[ML_HEADS + h:ML_HEADS + h + 1, :]
        m_st = m_ref[0, h:h + 1, 0:1]
        c_st = c_ref[0, h]
        n_st = n_ref[0, h:h + 1, :]
        dmat = jnp.where(causal, b_col + a_row, -jnp.inf)
        inter = b_col + m_st
        m_t = jnp.maximum(inter, jnp.max(dmat, axis=-1, keepdims=True))
        w_intra = jnp.exp(dmat - m_t)
        w_inter = jnp.exp(inter - m_t)
        s = _dot_nt(q, k) * w_intra
        num = jnp.dot(s.astype(bf16), v, preferred_element_type=f32) + w_inter * _dot_nt(q, c_st.astype(bf16))
        den = jnp.sum(s, axis=-1, keepdims=True) + w_inter * jnp.sum(q.astype(f32) * n_st, axis=-1, keepdims=True)
        hs = num * (1.0 / jnp.maximum(jnp.abs(den), jnp.exp(-m_t)))
        m_new = m_t[L - 1:L, :]
        b_last = b_col[L - 1:L, :]
        w_state = jnp.exp(b_last - b_col + ig_col - m_new)
        decay = jnp.exp(b_last + m_st - m_new)
        vw = (v.astype(f32) * w_state).astype(bf16)
        c_ref[0, h] = decay * c_st + lax.dot_general(vw, k, (((0,), (0,)), ((), ())), preferred_element_type=f32)
        n_ref[0, h:h + 1, :] = decay * n_st + jnp.sum(k.astype(f32) * w_state, axis=0, keepdims=True)
        m_ref[0, h:h + 1, :] = jnp.broadcast_to(m_new, (1, LANES))
        og = og_ref[0, :, h * ML_DIM:(h + 1) * ML_DIM]
        yb_ref[0, :, h * ML_DIM:(h + 1) * ML_DIM] = (jax.nn.sigmoid(og) * hs).astype(bf16)


def _mlstm_prompt(qkv, og, gates, gate_b, bsz, t_len, chunk=128):
    gb = jnp.pad(gate_b, (0, LANES - gate_b.shape[0])).reshape(1, LANES)
    return pl.pallas_call(
        _mlstm_kernel,
        grid=(bsz, t_len // chunk),
        in_specs=[pl.BlockSpec((1, chunk, 3 * HALF_W), lambda b, c: (b, c, 0)),
                  pl.BlockSpec((1, chunk, HALF_W), lambda b, c: (b, c, 0)),
                  pl.BlockSpec((1, chunk, LANES), lambda b, c: (b, c, 0)),
                  pl.BlockSpec((1, LANES), lambda b, c: (0, 0))],
        out_specs=[pl.BlockSpec((1, chunk, HALF_W), lambda b, c: (b, c, 0)),
                   pl.BlockSpec((1, ML_HEADS, ML_DIM, ML_DIM), lambda b, c: (b, 0, 0, 0)),
                   pl.BlockSpec((1, ML_HEADS, ML_DIM), lambda b, c: (b, 0, 0)),
                   pl.BlockSpec((1, ML_HEADS, LANES), lambda b, c: (b, 0, 0))],
        out_shape=[jax.ShapeDtypeStruct((bsz, t_len, HALF_W), jnp.bfloat16),
                   jax.ShapeDtypeStruct((bsz, ML_HEADS, ML_DIM, ML_DIM), jnp.float32),
                   jax.ShapeDtypeStruct((bsz, ML_HEADS, ML_DIM), jnp.float32),
                   jax.ShapeDtypeStruct((bsz, ML_HEADS, LANES), jnp.float32)],
        compiler_params=pltpu.CompilerParams(
            dimension_semantics=("arbitrary", "arbitrary"), vmem_limit_bytes=VMEM_LIMIT),
        name="mlstm_prompt",
    )(qkv.reshape(bsz, t_len, 3 * HALF_W), og.reshape(bsz, t_len, HALF_W),
      gates.reshape(bsz, t_len, LANES), gb)


def _outproj2_ln_kernel(ya_ref, yb_ref, w_ref, x_ref, g_ref, b_ref, o_ref):
    mix = jnp.dot(ya_ref[...], w_ref[0:HALF_W, :], preferred_element_type=jnp.float32)
    mix = mix + jnp.dot(yb_ref[...], w_ref[HALF_W:, :], preferred_element_type=jnp.float32)
    o_ref[...] = _layer_norm_rows(ALPHA * x_ref[...] + mix, g_ref[...], b_ref[...])


def _outproj2_ln(ya, yb, w_bf16, x, g, b, tm):
    m, d = x.shape
    return pl.pallas_call(
        _outproj2_ln_kernel,
        grid=(m // tm,),
        in_specs=[pl.BlockSpec((tm, HALF_W), lambda i: (i, 0)),
                  pl.BlockSpec((tm, HALF_W), lambda i: (i, 0)),
                  pl.BlockSpec((2 * HALF_W, d), lambda i: (0, 0)),
                  pl.BlockSpec((tm, d), lambda i: (i, 0)),
                  pl.BlockSpec((1, d), lambda i: (0, 0)),
                  pl.BlockSpec((1, d), lambda i: (0, 0))],
        out_specs=pl.BlockSpec((tm, d), lambda i: (i, 0)),
        out_shape=jax.ShapeDtypeStruct((m, d), jnp.float32),
        compiler_params=pltpu.CompilerParams(
            dimension_semantics=("arbitrary",), vmem_limit_bytes=VMEM_LIMIT),
        name="outproj2_ln",
    )(ya, yb, w_bf16, x, g.reshape(1, d), b.reshape(1, d))


SAMPLE_ROWS = 8


def _even_sample_kernel(h_ref, bg_ref, cg_ref, q_ref, k_ref, v_ref, og_ref, gate_ref, gb_ref, cw_ref, cb_ref,
                        conv_ref, c_ref, n_ref, m_ref, ya_ref, yb_ref, conv_out_ref, c_out_ref, n_out_ref, m_out_ref):
    f32, bf16 = jnp.float32, jnp.bfloat16
    rows = h_ref.shape[0]
    row = lax.broadcasted_iota(jnp.int32, (rows, 1), 0)
    tn = (((0,), (0,)), ((), ()))
    u = cg_ref[...] * h_ref[...]
    prev0, prev1 = conv_ref[:, 0, :], conv_ref[:, 1, :]
    conv = cb_ref[...] + cw_ref[0:1, :] * prev0 + cw_ref[1:2, :] * prev1 + cw_ref[2:3, :] * u
    ya_ref[...] = (bg_ref[...] * conv).astype(bf16)
    conv_out_ref[:, 0, :] = prev1
    conv_out_ref[:, 1, :] = u
    pre = gate_ref[...] + gb_ref[...]
    lf_all = jax.nn.log_sigmoid(pre)
    for h in range(ML_HEADS):
        cols = slice(h * ML_DIM, (h + 1) * ML_DIM)
        q = q_ref[:, cols]
        k = k_ref[:, cols] * (ML_DIM ** -0.5)
        v = v_ref[:, cols]
        ig = pre[:, h:h + 1]
        lf = lf_all[:, ML_HEADS + h:ML_HEADS + h + 1]
        m_st = m_ref[:, h:h + 1]
        n_st = n_ref[:, h, :]
        inter = lf + m_st
        m_t = jnp.maximum(inter, ig)
        w_intra = jnp.exp(ig - m_t)
        w_inter = jnp.exp(inter - m_t)
        qb, kb = q.astype(bf16), k.astype(bf16)
        s = jnp.sum(qb.astype(f32) * kb.astype(f32), axis=-1, keepdims=True) * w_intra
        cq = jnp.zeros((rows, ML_DIM), f32)
        for b in range(rows):
            c_b = c_ref[b, h]
            cq = jnp.where(row == b, _dot_nt(qb, c_b.astype(bf16)), cq)
            vw = jnp.where(row == b, v * w_intra, 0.0)
            outer = lax.dot_general(vw, k, tn, preferred_element_type=f32, precision=lax.Precision.HIGHEST)
            c_out_ref[b, h] = w_inter[b:b + 1, :] * c_b + outer
        num = s * v + w_inter * cq
        den = s + w_inter * jnp.sum(n_st * q, axis=-1, keepdims=True)
        hs = num * (1.0 / jnp.maximum(jnp.abs(den), jnp.exp(-m_t)))
        n_out_ref[:, h, :] = w_inter * n_st + w_intra * k
        m_out_ref[:, h, :] = jnp.broadcast_to(m_t, (rows, LANES))
        yb_ref[:, cols] = (jax.nn.sigmoid(og_ref[:, cols]) * hs).astype(bf16)


def _even_sample(z, gate_b, conv_w, conv_b, state_conv, state_c, state_n, state_m):
    s_n = z.shape[0]
    f32 = jnp.float32
    r = SAMPLE_ROWS
    gb = jnp.pad(gate_b, (0, LANES - gate_b.shape[0])).reshape(1, LANES)
    col = lambda j: pl.BlockSpec((r, HALF_W), lambda i, j=j: (i, j))
    in_specs = [col(j) for j in range(7)] + [
        pl.BlockSpec((r, LANES), lambda i: (i, E_GATE // LANES)),
        pl.BlockSpec((1, LANES), lambda i: (0, 0)),
        pl.BlockSpec((CONV_W, HALF_W), lambda i: (0, 0)),
        pl.BlockSpec((1, HALF_W), lambda i: (0, 0)),
        pl.BlockSpec((r, CONV_W - 1, HALF_W), lambda i: (i, 0, 0)),
        pl.BlockSpec((r, ML_HEADS, ML_DIM, ML_DIM), lambda i: (i, 0, 0, 0)),
        pl.BlockSpec((r, ML_HEADS, ML_DIM), lambda i: (i, 0, 0)),
        pl.BlockSpec((r, ML_HEADS), lambda i: (i, 0))]
    out_specs = [pl.BlockSpec((r, HALF_W), lambda i: (i, 0)),
                 pl.BlockSpec((r, HALF_W), lambda i: (i, 0)),
                 pl.BlockSpec((r, CONV_W - 1, HALF_W), lambda i: (i, 0, 0)),
                 pl.BlockSpec((r, ML_HEADS, ML_DIM, ML_DIM), lambda i: (i, 0, 0, 0)),
                 pl.BlockSpec((r, ML_HEADS, ML_DIM), lambda i: (i, 0, 0)),
                 pl.BlockSpec((r, ML_HEADS, LANES), lambda i: (i, 0, 0))]
    out_shape = [jax.ShapeDtypeStruct((s_n, HALF_W), jnp.bfloat16),
                 jax.ShapeDtypeStruct((s_n, HALF_W), jnp.bfloat16),
                 jax.ShapeDtypeStruct((s_n, CONV_W - 1, HALF_W), f32),
                 jax.ShapeDtypeStruct((s_n, ML_HEADS, ML_DIM, ML_DIM), f32),
                 jax.ShapeDtypeStruct((s_n, ML_HEADS, ML_DIM), f32),
                 jax.ShapeDtypeStruct((s_n, ML_HEADS, LANES), f32)]
    return pl.pallas_call(
        _even_sample_kernel,
        grid=(s_n // r,),
        in_specs=in_specs,
        out_specs=out_specs,
        out_shape=out_shape,
        compiler_params=pltpu.CompilerParams(
            dimension_semantics=("arbitrary",), vmem_limit_bytes=VMEM_LIMIT),
        name="even_sample",
    )(z, z, z, z, z, z, z, z, gb, conv_w, conv_b.reshape(1, HALF_W), state_conv, state_c, state_n, state_m)


O_Q = 0
O_QD = NSA_HEADS * LANES
O_KV = 2048
O_WIN = 2816
O_QI = 3072
O_MISC = 3200
O_PACKED = 3328
ZQ_QI = 2048
MISC_WI = IDX_DIM
MISC_GATE = IDX_DIM + IDX_HEADS
NEG_BIG = -(2.0 ** 100)
INT_MIN = -(2 ** 31)
ALIBI = tuple(float(2.0 ** (-8.0 * (h + 1) / NSA_HEADS)) for h in range(NSA_HEADS))


def _pack_w_in_o(w):
    qn, kvc, kvs, kvw, gates, qd, kvd, qi, ki, wi = _split(w, O_SIZES)
    scale = HEAD_DIM ** -0.5
    rows = w.shape[0]

    def spread(q):
        q = (q * scale).reshape(rows, NSA_HEADS, HEAD_DIM)
        z = jnp.zeros_like(q)
        low = (jnp.arange(NSA_HEADS) < NSA_HEADS // KV_GROUPS)[None, :, None]
        return jnp.concatenate([jnp.where(low, q, z), jnp.where(low, z, q)], axis=-1).reshape(rows, NSA_HEADS * LANES)

    pad = jnp.zeros((rows, O_PACKED - O_MISC - IDX_DIM - IDX_HEADS - 3 * NSA_HEADS), w.dtype)
    return jnp.concatenate([spread(qn), spread(qd), kvc, kvs, kvd, kvw, qi, ki, wi, gates, pad], axis=1)


def _odd_proj_kernel(x_ref, w_ref, wkvt_ref, zq_ref, kvt_ref, kvb_ref, win_ref, misc_ref):
    xb = x_ref[...].astype(jnp.bfloat16)
    z = jnp.dot(xb, w_ref[...], preferred_element_type=jnp.float32)
    zq_ref[:, 0:O_KV] = z[:, 0:O_KV].astype(jnp.bfloat16)
    zq_ref[:, ZQ_QI:ZQ_QI + LANES] = z[:, O_QI:O_MISC].astype(jnp.bfloat16)
    kvt_ref[0] = _dot_nt(wkvt_ref[...], xb)
    kvb_ref[...] = z[:, O_KV:O_QI].astype(jnp.bfloat16)
    win_ref[...] = z[:, O_WIN:O_QI]
    misc_ref[...] = z[:, O_MISC:O_PACKED]


def _odd_proj(x, w_packed_bf16, tm, seq_len):
    m, k = x.shape
    n_kv = O_WIN - O_KV
    widths = (O_KV + LANES, O_QI - O_KV, O_QI - O_WIN, LANES)
    dtypes = (jnp.bfloat16, jnp.bfloat16, jnp.float32, jnp.float32)
    row_spec = lambda n: pl.BlockSpec((tm, n), lambda i: (i, 0))
    tiles = seq_len // tm
    out_specs = [row_spec(widths[0]), pl.BlockSpec((1, n_kv, tm), lambda i: (i // tiles, 0, i % tiles))]
    out_specs += [row_spec(n) for n in widths[1:]]
    out_shape = [jax.ShapeDtypeStruct((m, widths[0]), dtypes[0]),
                 jax.ShapeDtypeStruct((m // seq_len, n_kv, seq_len), jnp.float32)]
    out_shape += [jax.ShapeDtypeStruct((m, n), dt) for n, dt in zip(widths[1:], dtypes[1:])]
    return pl.pallas_call(
        _odd_proj_kernel,
        grid=(m // tm,),
        in_specs=[pl.BlockSpec((tm, k), lambda i: (i, 0)),
                  pl.BlockSpec((k, O_PACKED), lambda i: (0, 0)),
                  pl.BlockSpec((n_kv, k), lambda i: (0, 0))],
        out_specs=out_specs,
        out_shape=out_shape,
        compiler_params=pltpu.CompilerParams(
            dimension_semantics=("arbitrary",), vmem_limit_bytes=VMEM_LIMIT),
        name="odd_proj",
    )(x, w_packed_bf16, w_packed_bf16[:, O_KV:O_WIN].T)


def _dot_nt(a, b):
    return lax.dot_general(a, b, (((1,), (1,)), ((), ())), preferred_element_type=jnp.float32)


def _flash_heads(q_heads, slopes, group, kv_ref, pos_ref, k_blk, v_blk, tk, lo, hi, bias_fn, skip_fn=None,
                 start_fn=None):
    f32, bf16 = jnp.float32, jnp.bfloat16
    if start_fn is None:
        start_fn = lambda kt: kt * tk
    tq = q_heads[0].shape[0]
    n_heads = len(q_heads)
    lane = lax.broadcasted_iota(jnp.int32, (1, LANES), 1)
    own_half = (lane // HEAD_DIM) == group
    feat_lane = lax.broadcasted_iota(jnp.int32, (tq, LANES), 1)
    slope_feat = [jnp.where(feat_lane == 0, BLK * sl, jnp.where(feat_lane == 1, sl, 0.0)).astype(bf16)
                  for sl in slopes]
    q_all = jnp.concatenate([jnp.concatenate(q_heads, axis=0), jnp.concatenate(slope_feat, axis=0)], axis=1)
    tn = (((0,), (0,)), ((), ()))

    def tile(kt, carry):
        m, acc = carry
        start = pl.multiple_of(start_fn(kt), LANES)
        rows = pl.ds(start, tk)
        k = kv_ref[0, rows, k_blk * LANES:(k_blk + 1) * LANES]
        v = kv_ref[0, rows, v_blk * LANES:(v_blk + 1) * LANES]
        v = jnp.where(own_half, v, jnp.ones_like(v))
        k_pos = jnp.concatenate([k, pos_ref[rows, :]], axis=1)
        s = _dot_nt(k_pos, q_all) + jnp.concatenate([bias_fn(kt)] * n_heads, axis=1)
        m_new = jnp.maximum(m, jnp.max(s, axis=0, keepdims=True))
        p = jnp.exp(s - m_new)
        acc = jnp.exp(m - m_new) * acc + lax.dot_general(v, p.astype(bf16), tn, preferred_element_type=f32)
        return m_new, acc

    def body(kt, carry):
        if skip_fn is None:
            return tile(kt, carry)
        return lax.cond(skip_fn(kt), lambda c: c, functools.partial(tile, kt), carry)

    m0 = jnp.full((1, n_heads * tq), -jnp.inf, f32)
    a0 = jnp.zeros((LANES, n_heads * tq), f32)
    _, acc = lax.fori_loop(lo, hi, body, (m0, a0))
    return acc


def _odd_attn_kernel(zq_ref, miscq_ref, kvb_ref, misck_ref, wc_ref, o_ref,
                     kc_ref, vc_ref, ki_ref, key_ref, dbias_ref, flag_ref, pos_ref, *, t_len, tq, tk, topk):
    f32, bf16, i32 = jnp.float32, jnp.bfloat16, jnp.int32
    nb = t_len // BLK
    qi_blk = pl.program_id(1)
    q0 = qi_blk * tq

    @pl.when(qi_blk == 0)
    def _():
        ck = kvb_ref[0, :, 0:LANES].astype(f32).reshape(nb, BLK, LANES)
        kc_ref[...] = jnp.sum(ck * wc_ref[0][None], axis=1).astype(bf16)
        cv = kvb_ref[0, :, LANES:2 * LANES].astype(f32).reshape(nb, BLK, LANES)
        vc = jnp.sum(cv * wc_ref[1][None], axis=1)
        for g in range(KV_GROUPS):
            half = vc[:, g * HEAD_DIM:(g + 1) * HEAD_DIM]
            vc_ref[g] = jnp.concatenate([half, half], axis=1).astype(bf16)
        ki_ref[...] = misck_ref[0, :, 0:IDX_DIM].astype(bf16)
        key_idx = lax.broadcasted_iota(i32, (t_len, LANES), 0)
        feat = lax.broadcasted_iota(i32, (t_len, LANES), 1)
        pos_ref[...] = jnp.where(feat == 0, key_idx // BLK, jnp.where(feat == 1, key_idx % BLK, 0)).astype(bf16)

    misc = miscq_ref[0]
    gates = jax.nn.sigmoid(misc[:, MISC_GATE:MISC_GATE + 3 * NSA_HEADS])
    t_col = q0 + lax.broadcasted_iota(i32, (tq, 1), 0)
    lane = lax.broadcasted_iota(i32, (1, LANES), 1)
    low_half = lane < HEAD_DIM
    heads_per_group = NSA_HEADS // KV_GROUPS

    def q_head(base, h):
        return zq_ref[0, :, base + h * LANES: base + (h + 1) * LANES]

    def pair(even, odd):
        return jnp.where(low_half, even, odd)

    def normalised_pair(acc_t, r_even, g):
        halves = []
        for r in (r_even, r_even + 1):
            cols = slice(r * tq, (r + 1) * tq)
            out = acc_t[g * HEAD_DIM:(g + 1) * HEAD_DIM, cols]
            den = acc_t[(1 - g) * HEAD_DIM:(1 - g) * HEAD_DIM + 1, cols]
            halves.append(out * (1.0 / den))
        return jnp.concatenate(halves, axis=0).T

    def gate_pair(j, c):
        a = gates[:, (2 * j) * 3 + c:(2 * j) * 3 + c + 1]
        b = gates[:, (2 * j + 1) * 3 + c:(2 * j + 1) * 3 + c + 1]
        return jnp.where(low_half, a, b)

    tn = (((0,), (0,)), ((), ()))
    t_lanes = q0 + lax.broadcasted_iota(i32, (1, tq), 1)
    jb = lax.broadcasted_iota(i32, (nb, 1), 0)
    dist_c = t_lanes - (jb * BLK + (BLK - 1))
    mask_c = dist_c >= 0
    dist_cf = dist_c.astype(f32)
    cur = t_lanes // BLK
    forced = (jb == 0) | (jb == cur) | (jb == cur - 1)
    admissible = jb <= cur
    jb_full = lax.broadcasted_iota(i32, (nb, tq), 0)
    o_cmp, sel_bias, sel_any = [], [], []
    for g in range(KV_GROUPS):
        imp = jnp.zeros((nb, tq), f32)
        for r in range(heads_per_group):
            h = g * heads_per_group + r
            s = _dot_nt(kc_ref[...], q_head(O_Q, h)) - ALIBI[h] * dist_cf
            s = jnp.where(mask_c, s, NEG_BIG)
            m = jnp.max(s, axis=0, keepdims=True)
            e = jnp.where(mask_c, jnp.exp(s - m), 0.0)
            p = e * (1.0 / jnp.maximum(jnp.sum(e, axis=0, keepdims=True), 1e-30))
            imp = imp + p
            o_cmp.append(lax.dot_general(p.astype(bf16), vc_ref[g], tn, preferred_element_type=f32))
        imp = jnp.where(forced, FORCE_SCORE, imp)
        imp = jnp.where(admissible, imp, -jnp.inf)
        sel = jnp.zeros((nb, tq), f32)
        for _ in range(min(NSA_TOPN, nb)):
            m = jnp.max(imp, axis=0, keepdims=True)
            first = jnp.min(jnp.where(imp == m, jb_full, nb), axis=0, keepdims=True)
            hit = jb_full == first
            sel = jnp.where(hit & (m > -jnp.inf), 1.0, sel)
            imp = jnp.where(hit, -jnp.inf, imp)
        sel_bias.append(jnp.where(sel > 0.0, 0.0, NEG_BIG).astype(bf16))
        sel_any.append(jnp.max(sel, axis=1, keepdims=True))

    n_sel_tiles = (q0 + tq + tk - 1) // tk
    blocks_per_tile = tk // BLK

    for g in range(KV_GROUPS):
        for j in range(nb // blocks_per_tile):
            hit = jnp.max(sel_any[g][j * blocks_per_tile:(j + 1) * blocks_per_tile, :])
            flag_ref[g, j] = (hit > 0.0).astype(i32)

    def sel_skip_fn(g):
        return lambda kt: flag_ref[g, kt] == 0

    def causal_bias(start, width):
        sp = start + lax.broadcasted_iota(i32, (width, 1), 0)
        return jnp.where(sp <= t_lanes, 0.0, NEG_BIG)

    def sel_bias_fn(g):
        def fn(kt):
            row = lax.broadcasted_iota(i32, (nb, tk), 0)
            col = lax.broadcasted_iota(i32, (nb, tk), 1)
            expand = jnp.where(row == kt * blocks_per_tile + col // BLK, 1.0, 0.0).astype(bf16)
            spread = lax.dot_general(expand, sel_bias[g], tn, preferred_element_type=f32)
            return spread + causal_bias(kt * tk, tk)
        return fn

    tkw = min(WINDOW + tq, t_len)
    win_start = jnp.clip(q0 - WINDOW, 0, t_len - tkw)

    def win_bias_fn(kt):
        dist = t_lanes - (win_start + lax.broadcasted_iota(i32, (tkw, 1), 0))
        return jnp.where((dist >= 0) & (dist < WINDOW), 0.0, NEG_BIG)

    qi_heads = [zq_ref[0, :, ZQ_QI + h * IDX_DIM: ZQ_QI + (h + 1) * IDX_DIM] for h in range(IDX_HEADS)]
    pick = jnp.where(lax.broadcasted_iota(i32, (8, LANES), 1) == MISC_WI + lax.broadcasted_iota(i32, (8, LANES), 0),
                     1.0, 0.0)
    wi_rows = lax.dot_general(pick, misc, (((1,), (1,)), ((), ())), preferred_element_type=f32,
                              precision=lax.Precision.HIGHEST)
    t_row = q0 + lax.broadcasted_iota(i32, (1, tq), 1)

    def fold8(x):
        x = x.reshape(tk // 8, 8, tq)
        while x.shape[0] > 1:
            half = x.shape[0] // 2
            x = x[:half] + x[half:]
        return x[0]

    def index_tile(kt, carry):
        start = pl.multiple_of(kt * tk, tk)
        kik = ki_ref[pl.ds(start, tk), :]
        tot = jnp.zeros((tk, tq), f32)
        for h in range(IDX_HEADS):
            sc = _dot_nt(kik, qi_heads[h]) * (IDX_DIM ** -0.5)
            tot = tot + jnp.maximum(sc, 0.0) * wi_rows[h:h + 1, :]
        tot = tot * (IDX_HEADS ** -0.5)
        tot = jnp.where(tot == 0.0, 0.0, tot)
        bits = lax.bitcast_convert_type(tot, i32)
        key = jnp.where(bits < 0, bits ^ jnp.int32(0x7FFFFFFF), bits)
        sp = start + lax.broadcasted_iota(i32, (tk, 1), 0)
        key_ref[kt] = jnp.where(sp <= t_row, key, INT_MIN)
        return carry

    lax.fori_loop(0, n_sel_tiles, index_tile, 0)

    def count_ge(trial):
        def body(kt, acc):
            return acc + fold8((key_ref[kt] >= trial).astype(i32))
        acc = lax.fori_loop(0, n_sel_tiles, body, jnp.zeros((8, tq), i32))
        return jnp.sum(acc, axis=0, keepdims=True)

    c_pos = count_ge(jnp.zeros((1, tq), i32))
    v0 = jnp.where(c_pos >= topk, 0, INT_MIN).astype(i32)
    c0 = jnp.where(c_pos >= topk, c_pos, n_sel_tiles * tk)

    def bit_step(i, carry):
        v, cge = carry
        trial = v | lax.shift_left(jnp.int32(1), 30 - i)
        c = count_ge(trial)
        ok = c >= topk
        return jnp.where(ok, trial, v), jnp.where(ok, c, cge)

    v_thr, c_ge = lax.fori_loop(0, 31, bit_step, (v0, c0))
    v_eff = jnp.maximum(v_thr, INT_MIN + 1)

    def write_plain(kt, carry):
        dbias_ref[kt] = jnp.where(key_ref[kt] >= v_eff, 0.0, NEG_BIG)
        return carry

    def plain_path():
        lax.fori_loop(0, n_sel_tiles, write_plain, 0)

    def tie_path():
        need = topk - count_ge(v_thr + 1)

        def count_tie_below(limit):
            def body(kt, acc):
                sp = kt * tk + lax.broadcasted_iota(i32, (tk, 1), 0)
                return acc + fold8(((key_ref[kt] == v_thr) & (sp < limit)).astype(i32))
            acc = lax.fori_loop(0, n_sel_tiles, body, jnp.zeros((8, tq), i32))
            return jnp.sum(acc, axis=0, keepdims=True)

        n_bits = max(1, (t_len - 1).bit_length())

        def idx_step(i, w):
            trial = w | lax.shift_left(jnp.int32(1), n_bits - 1 - i)
            return jnp.where(count_tie_below(trial) < need, trial, w)

        w_last = lax.fori_loop(0, n_bits, idx_step, jnp.zeros((1, tq), i32))

        def write_tie(kt, carry):
            sp = kt * tk + lax.broadcasted_iota(i32, (tk, 1), 0)
            key = key_ref[kt]
            take = (key > v_thr) | ((key == v_thr) & (sp <= w_last))
            dbias_ref[kt] = jnp.where(take & (key >= v_eff), 0.0, NEG_BIG)
            return carry

        lax.fori_loop(0, n_sel_tiles, write_tie, 0)

    lax.cond(jnp.max(c_ge) > topk, tie_path, plain_path)

    def dsa_bias_fn(kt):
        return dbias_ref[kt]

    for g in range(KV_GROUPS):
        hs = [g * heads_per_group + r for r in range(heads_per_group)]
        slopes = [ALIBI[h] for h in hs]
        qn = [q_head(O_Q, h) for h in hs]
        qd = [q_head(O_QD, h) for h in hs]
        a_s = _flash_heads(qn, slopes, g, kvb_ref, pos_ref, 2, 3, tk, 0, n_sel_tiles, sel_bias_fn(g), sel_skip_fn(g))
        a_w = _flash_heads(qn, slopes, g, kvb_ref, pos_ref, 6, 7, tkw, 0, 1, win_bias_fn,
                           start_fn=lambda kt: win_start)
        a_d = _flash_heads(qd, slopes, g, kvb_ref, pos_ref, 4, 5, tk, 0, n_sel_tiles, dsa_bias_fn)
        for jj in range(heads_per_group // 2):
            j = hs[2 * jj] // 2
            o_n = (gate_pair(j, 0) * pair(o_cmp[2 * j], o_cmp[2 * j + 1])
                   + gate_pair(j, 1) * normalised_pair(a_s, 2 * jj, g)
                   + gate_pair(j, 2) * normalised_pair(a_w, 2 * jj, g))
            o_ref[0, :, j * LANES:(j + 1) * LANES] = o_n.astype(o_ref.dtype)
            o_d = normalised_pair(a_d, 2 * jj, g)
            o_ref[0, :, HALF_W + j * LANES:HALF_W + (j + 1) * LANES] = o_d.astype(o_ref.dtype)


def _odd_attn_prompt(zq, misc, kvb, wc2, bsz, t_len, tq=256, tk=512):
    tk = min(tk, t_len)
    topk = min(DSA_TOPK_MAX, t_len // 4)
    nb = t_len // BLK
    n_kt = t_len // tk
    zq3 = zq.reshape(bsz, t_len, zq.shape[1])
    misc3 = misc.reshape(bsz, t_len, LANES)
    kvb3 = kvb.reshape(bsz, t_len, kvb.shape[1])
    kern = functools.partial(_odd_attn_kernel, t_len=t_len, tq=tq, tk=tk, topk=topk)
    out = pl.pallas_call(
        kern,
        grid=(bsz, t_len // tq),
        in_specs=[pl.BlockSpec((1, tq, zq.shape[1]), lambda b, i: (b, i, 0)),
                  pl.BlockSpec((1, tq, LANES), lambda b, i: (b, i, 0)),
                  pl.BlockSpec((1, t_len, kvb.shape[1]), lambda b, i: (b, 0, 0)),
                  pl.BlockSpec((1, t_len, LANES), lambda b, i: (b, 0, 0)),
                  pl.BlockSpec((2, BLK, LANES), lambda b, i: (0, 0, 0))],
        out_specs=pl.BlockSpec((1, tq, 2 * HALF_W), lambda b, i: (b, i, 0)),
        out_shape=jax.ShapeDtypeStruct((bsz, t_len, 2 * HALF_W), jnp.bfloat16),
        scratch_shapes=[pltpu.VMEM((nb, LANES), jnp.bfloat16),
                        pltpu.VMEM((KV_GROUPS, nb, LANES), jnp.bfloat16),
                        pltpu.VMEM((t_len, IDX_DIM), jnp.bfloat16),
                        pltpu.VMEM((n_kt, tk, tq), jnp.int32),
                        pltpu.VMEM((n_kt, tk, tq), jnp.float32),
                        pltpu.SMEM((KV_GROUPS, n_kt), jnp.int32),
                        pltpu.VMEM((t_len, LANES), jnp.bfloat16)],
        compiler_params=pltpu.CompilerParams(
            dimension_semantics=("arbitrary", "arbitrary"), vmem_limit_bytes=VMEM_LIMIT),
        name="odd_attn_prompt",
    )(zq3, misc3, kvb3, misc3, wc2)
    return out.reshape(bsz * t_len, 2 * HALF_W)


PAGE = 128
ISC_ROWS = 24


SAMPLE_SEQS = 2
_DONE = object()


def _odd_sample_kernel(pt_ref, *refs, n_pages, topk, nbp):
    del pt_ref
    seqs = [_odd_sample_one(u, *refs, n_pages=n_pages, topk=topk, nbp=nbp) for u in range(SAMPLE_SEQS)]
    while seqs:
        seqs = [g for g in seqs if next(g, _DONE) is not _DONE]


def _odd_sample_one(u, qn_ref, qd_ref, qi_ref, wi_ref, gate_ref, newb_ref, newki_ref, neww_ref,
                    win_ref, wc_ref, wct_ref, *rest, n_pages, topk, nbp):
    f32, bf16, i32 = jnp.float32, jnp.bfloat16, jnp.int32
    all_pages = SAMPLE_SEQS * n_pages
    kv_refs = rest[u * n_pages:(u + 1) * n_pages]
    idx_refs = rest[all_pages + u * n_pages:all_pages + (u + 1) * n_pages]
    o_ref, wout_ref, isc_all_ref = rest[2 * all_pages:]
    isc_ref = isc_all_ref.at[u]
    past = n_pages * PAGE
    nb = past // BLK + 1
    row8 = lax.broadcasted_iota(i32, (8, 1), 0)
    lane = lax.broadcasted_iota(i32, (1, LANES), 1)
    slope8 = jnp.zeros((8, 1), f32)
    for h in range(NSA_HEADS):
        slope8 = jnp.where(row8 == h, ALIBI[h], slope8)
    low_group = row8 < NSA_HEADS // KV_GROUPS

    def pick_half(x):
        return jnp.where(low_group, x[:, 0:HEAD_DIM], x[:, HEAD_DIM:2 * HEAD_DIM])

    qn8, qd8 = qn_ref[u], qd_ref[u]
    qi8, wi8 = qi_ref[u], wi_ref[u]
    newb = newb_ref[u]
    gates = jax.nn.sigmoid(gate_ref[u])

    def new_score(q8, k_row):
        return jnp.sum(q8.astype(f32) * k_row.astype(f32), axis=-1, keepdims=True) + slope8 * float(past)

    def feat(p, slot):
        return kv_refs[p][0, slot * LANES:(slot + 1) * LANES, :]

    s_sel, s_dsa = [], []
    key_row = lax.broadcasted_iota(i32, (PAGE, nbp), 0)
    blk_col = lax.broadcasted_iota(i32, (PAGE, nbp), 1)
    kct = jnp.zeros((LANES, nbp), f32)
    vct = jnp.zeros((LANES, nbp), f32)
    for p in range(n_pages):
        pool = jnp.where(blk_col == 2 * p + key_row // BLK, 1.0, 0.0).astype(bf16)
        kct = kct + jnp.dot((feat(p, 0) * wct_ref[0]).astype(bf16), pool, preferred_element_type=f32)
        vct = vct + jnp.dot((feat(p, 1) * wct_ref[1]).astype(bf16), pool, preferred_element_type=f32)
        pos = (p * PAGE + lane).astype(f32)
        s_sel.append(jnp.dot(qn8, feat(p, 2).astype(bf16), preferred_element_type=f32) + slope8 * pos)
        s_dsa.append(jnp.dot(qd8, feat(p, 4).astype(bf16), preferred_element_type=f32) + slope8 * pos)
        ii = jnp.dot(qi8, idx_refs[p][0].astype(bf16), preferred_element_type=f32) * (IDX_DIM ** -0.5)
        isc_ref[p:p + 1, :] = jnp.sum(jnp.maximum(ii, 0.0) * wi8, axis=0, keepdims=True) * (IDX_HEADS ** -0.5)
    first_row = row8 == 0
    place = jnp.where(first_row & (lax.broadcasted_iota(i32, (8, nbp), 1) == nb - 1), 1.0, 0.0)
    tn = (((0,), (0,)), ((), ()))
    new_k = jnp.where(first_row, newb[:, 0:LANES].astype(f32) * wc_ref[0, 0:1, :], 0.0)
    new_v = jnp.where(first_row, newb[:, LANES:2 * LANES].astype(f32) * wc_ref[1, 0:1, :], 0.0)
    kct = kct + lax.dot_general(new_k.astype(bf16), place.astype(bf16), tn, preferred_element_type=f32)
    vct = vct + lax.dot_general(new_v.astype(bf16), place.astype(bf16), tn, preferred_element_type=f32)
    ii_new = jnp.sum(qi8.astype(f32) * newki_ref[u].astype(bf16).astype(f32), axis=-1, keepdims=True) * (IDX_DIM ** -0.5)
    isc_new = jnp.sum(jnp.maximum(ii_new, 0.0) * wi8, axis=0, keepdims=True) * (IDX_HEADS ** -0.5)
    isc_ref[n_pages:n_pages + 1, :] = jnp.where(lane == 0, isc_new, -jnp.inf)
    isc_ref[n_pages + 1:ISC_ROWS, :] = jnp.full((ISC_ROWS - n_pages - 1, LANES), -jnp.inf, f32)

    jb = lax.broadcasted_iota(i32, (1, nbp), 1)
    dist_c = past - (jb * BLK + (BLK - 1))
    mask_c = (dist_c >= 0) & (jb < nb)
    s_c = jnp.dot(qn8, kct.astype(bf16), preferred_element_type=f32) - slope8 * dist_c.astype(f32)
    s_c = jnp.where(mask_c, s_c, NEG_BIG)
    m_c = jnp.max(s_c, axis=-1, keepdims=True)
    e_c = jnp.where(mask_c, jnp.exp(s_c - m_c), 0.0)
    p_c = e_c * (1.0 / jnp.maximum(jnp.sum(e_c, axis=-1, keepdims=True), 1e-30))
    o_c = pick_half(_dot_nt(p_c.astype(bf16), vct.astype(bf16)))
    imp_lo = jnp.sum(jnp.where(low_group, p_c, 0.0), axis=0, keepdims=True)
    imp_hi = jnp.sum(jnp.where(low_group, 0.0, p_c), axis=0, keepdims=True)
    imp = jnp.where(low_group, imp_lo, imp_hi)
    cur = past // BLK
    imp = jnp.where((jb == 0) | (jb == cur) | (jb == cur - 1), FORCE_SCORE, imp)
    imp = jnp.where(jb <= cur, imp, -jnp.inf)
    rank = jnp.zeros((8, nbp), i32)
    for j in range(nb):
        other = imp[:, j:j + 1]
        rank = rank + ((other > imp) | ((other == imp) & (j < jb))).astype(i32)
    sel = jnp.where((rank < min(NSA_TOPN, nb)) & (imp > -jnp.inf), 1.0, 0.0)

    tot = isc_ref[...]
    tot = jnp.where(tot == 0.0, 0.0, tot)
    bits = lax.bitcast_convert_type(tot, i32)
    key = jnp.where(bits < 0, bits ^ jnp.int32(0x7FFFFFFF), bits)
    krow = lax.broadcasted_iota(i32, (ISC_ROWS, LANES), 0)
    kidx = krow * LANES + lax.broadcasted_iota(i32, (ISC_ROWS, LANES), 1)
    key = jnp.where(kidx <= past, key, INT_MIN)

    def total(x):
        folded = jnp.sum(x.reshape(ISC_ROWS // 8, 8, LANES), axis=0)
        return jnp.sum(jnp.sum(folded, axis=1, keepdims=True), axis=0, keepdims=True)

    def count_ge(trial):
        return total((key >= trial).astype(i32))

    c_pos = count_ge(jnp.zeros((1, 1), i32))
    v_thr = jnp.where(c_pos >= topk, 0, INT_MIN).astype(i32)
    for shift, width in ((27, 4), (23, 4), (19, 4), (15, 4), (11, 4), (7, 4), (3, 4), (0, 3)):
        digit = jnp.zeros((1, 1), i32)
        for j in range(1, 1 << width):
            ok = count_ge(v_thr | jnp.int32(j << shift)) >= topk
            digit = digit + ok.astype(i32)
        v_thr = v_thr | lax.shift_left(digit, jnp.int32(shift))
    v_eff = jnp.maximum(v_thr, INT_MIN + 1)

    def tie_mask():
        need = topk - count_ge(v_thr + 1)
        tie = key == v_thr
        w_last = jnp.zeros((1, 1), i32)
        for shift in (8, 4, 0):
            digit = jnp.zeros((1, 1), i32)
            for j in range(1, 16):
                below = total((tie & (kidx < (w_last | jnp.int32(j << shift)))).astype(i32))
                digit = digit + (below < need).astype(i32)
            w_last = w_last | lax.shift_left(digit, jnp.int32(shift))
        take = (key > v_thr) | (tie & (kidx <= w_last))
        return jnp.where(take & (key >= v_eff), 1.0, 0.0)

    def plain_mask():
        return jnp.where(key >= v_eff, 1.0, 0.0)

    has_surplus_tie = count_ge(v_thr)[0, 0] > topk
    yield
    dsel = lax.cond(has_surplus_tie, tie_mask, plain_mask)
    yield

    def attend(tiles, masks, s_new, new_ok, v_tile, v_new):
        tiles = [jnp.where(mk, t, NEG_BIG) for t, mk in zip(tiles, masks)]
        s_new = jnp.where(new_ok, s_new, NEG_BIG)
        m = s_new
        for t in tiles:
            m = jnp.maximum(m, jnp.max(t, axis=-1, keepdims=True))
        l = jnp.exp(s_new - m)
        acc = l * v_new.astype(f32)
        for j, t in enumerate(tiles):
            e = jnp.exp(t - m)
            l = l + jnp.sum(e, axis=-1, keepdims=True)
            acc = acc + _dot_nt(e.astype(bf16), v_tile(j))
        return pick_half(acc * (1.0 / l))

    sel_masks = [jnp.where(lane < BLK, sel[:, 2 * p:2 * p + 1], sel[:, 2 * p + 1:2 * p + 2]) > 0.0
                 for p in range(n_pages)]
    o_s = attend(s_sel, sel_masks, new_score(qn8, newb[:, 2 * LANES:3 * LANES]), sel[:, nb - 1:nb] > 0.0,
                 lambda p: feat(p, 3).astype(bf16), newb[:, 3 * LANES:4 * LANES])
    dsa_masks = [dsel[p:p + 1, :] > 0.0 for p in range(n_pages)]
    o_d = attend(s_dsa, dsa_masks, new_score(qd8, newb[:, 4 * LANES:5 * LANES]), dsel[n_pages:n_pages + 1, 0:1] > 0.0,
                 lambda p: feat(p, 5).astype(bf16), newb[:, 5 * LANES:6 * LANES])
    n_win = win_ref.shape[2]
    wlane = lax.broadcasted_iota(i32, (1, n_win), 1)
    wpos = past - n_win + wlane
    s_w = jnp.dot(qn8, win_ref[u, 0:LANES, :].astype(bf16), preferred_element_type=f32) + slope8 * wpos.astype(f32)
    w_ok = (past - wpos < WINDOW) & (wpos >= 0)
    o_w = attend([s_w], [w_ok], new_score(qn8, newb[:, 6 * LANES:7 * LANES]), True,
                 lambda p: win_ref[u, LANES:2 * LANES, :].astype(bf16), newb[:, 7 * LANES:8 * LANES])
    o_ref[u, 0:NSA_HEADS, :] = gates[:, 0:1] * o_c + gates[:, 1:2] * o_s + gates[:, 2:3] * o_w
    o_ref[u, NSA_HEADS:2 * NSA_HEADS, :] = o_d
    last = jnp.where(first_row & (lax.broadcasted_iota(i32, (8, n_win), 1) == n_win - 1), 1.0, 0.0)
    new_w = jnp.where(first_row, neww_ref[u], 0.0)
    w_hi = new_w.astype(bf16).astype(f32)
    w_mid = (new_w - w_hi).astype(bf16).astype(f32)
    w_lo = new_w - w_hi - w_mid
    pieces = jnp.concatenate([w_hi, w_mid, w_lo, jnp.zeros_like(new_w)], axis=0).astype(bf16)
    placed = lax.dot_general(pieces, jnp.concatenate([last] * 4, axis=0).astype(bf16), tn,
                             preferred_element_type=f32)
    wout_ref[u] = jnp.where(wlane == n_win - 1, placed, pltpu.roll(win_ref[u], n_win - 1, 1))


def _odd_attn_sample(zq, misc, kvb, win_new, cache_kv, cache_idx, cache_win, page_table, w_cmp):
    f32, bf16 = jnp.float32, jnp.bfloat16
    s_n, n_pages = page_table.shape
    n_pool = cache_kv.shape[0]
    n_win = cache_win.shape[1]
    past = n_pages * PAGE
    topk = min(DSA_TOPK_MAX, (past + 1) // 4)
    nbp = _round_up(past // BLK + 1, 8)
    wc2 = jnp.concatenate([w_cmp, w_cmp], axis=-1)
    wct = jnp.tile(jnp.swapaxes(w_cmp, 1, 2), (1, KV_GROUPS, PAGE // BLK))
    qn = zq[:, O_Q:O_QD].reshape(s_n, NSA_HEADS, LANES)
    qd = zq[:, O_QD:ZQ_QI].reshape(s_n, DSA_HEADS, LANES)
    qi = jnp.pad(zq[:, ZQ_QI:ZQ_QI + LANES].reshape(s_n, IDX_HEADS, IDX_DIM), ((0, 0), (0, 8 - IDX_HEADS), (0, 0)))
    wi = jnp.pad(misc[:, MISC_WI:MISC_WI + IDX_HEADS].reshape(s_n, IDX_HEADS, 1), ((0, 0), (0, 8 - IDX_HEADS), (0, 0)))
    gts = misc[:, MISC_GATE:MISC_GATE + 3 * NSA_HEADS].reshape(s_n, NSA_HEADS, 3)
    kv_pages = jnp.transpose(cache_kv, (0, 2, 3, 4, 1)).reshape(n_pool, KV_SLOTS * LANES, PAGE)
    idx_pages = jnp.transpose(cache_idx, (0, 2, 1))
    win_t = jnp.transpose(cache_win, (0, 2, 3, 4, 1)).reshape(s_n, 2 * LANES, n_win)
    n_u = SAMPLE_SEQS
    per_seq = lambda shape: pl.BlockSpec((n_u,) + shape, lambda b, pt: (b, 0, 0))
    page_spec = lambda rows, u, p: pl.BlockSpec((1, rows, PAGE), lambda b, pt, u=u, p=p: (pt[b * n_u + u, p], 0, 0))
    seq_pages = [(u, p) for u in range(n_u) for p in range(n_pages)]
    in_specs = ([per_seq((NSA_HEADS, LANES)), per_seq((DSA_HEADS, LANES)), per_seq((8, IDX_DIM)),
                 per_seq((8, 1)), per_seq((NSA_HEADS, 3)), per_seq((1, 8 * LANES)), per_seq((1, IDX_DIM)),
                 per_seq((1, 2 * LANES)), per_seq((2 * LANES, n_win)),
                 pl.BlockSpec((2, BLK, LANES), lambda b, pt: (0, 0, 0)),
                 pl.BlockSpec((2, LANES, PAGE), lambda b, pt: (0, 0, 0))]
                + [page_spec(KV_SLOTS * LANES, u, p) for u, p in seq_pages]
                + [page_spec(IDX_DIM, u, p) for u, p in seq_pages])
    kern = functools.partial(_odd_sample_kernel, n_pages=n_pages, topk=topk, nbp=nbp)
    o, win_out = pl.pallas_call(
        kern,
        grid_spec=pltpu.PrefetchScalarGridSpec(
            num_scalar_prefetch=1,
            grid=(s_n // n_u,),
            in_specs=in_specs,
            out_specs=[per_seq((2 * NSA_HEADS, HEAD_DIM)), per_seq((2 * LANES, n_win))],
            scratch_shapes=[pltpu.VMEM((n_u, ISC_ROWS, LANES), f32)]),
        out_shape=[jax.ShapeDtypeStruct((s_n, 2 * NSA_HEADS, HEAD_DIM), f32),
                   jax.ShapeDtypeStruct((s_n, 2 * LANES, n_win), f32)],
        compiler_params=pltpu.CompilerParams(
            dimension_semantics=("arbitrary",), vmem_limit_bytes=VMEM_LIMIT),
        name="odd_attn_sample",
    )(page_table, qn, qd, qi, wi, gts, kvb.reshape(s_n, 1, 8 * LANES), misc[:, 0:IDX_DIM].reshape(s_n, 1, IDX_DIM),
      win_new.reshape(s_n, 1, 2 * LANES), win_t, wc2, wct,
      *([kv_pages] * len(seq_pages)), *([idx_pages] * len(seq_pages)))
    win_out = jnp.transpose(win_out.reshape(s_n, 2, KV_GROUPS, HEAD_DIM, n_win), (0, 4, 1, 2, 3))
    return o.reshape(s_n, 2 * HALF_W), win_out


def _split(z, sizes):
    cuts = [int(c) for c in np.cumsum(sizes)[:-1]]
    return jnp.split(z, cuts, axis=-1)


def _pad_cols(w, n):
    return jnp.pad(w, ((0, 0), (0, n - w.shape[1])))


def kernel(x_prompt, x_sample, state_conv, state_C, state_n, state_m, cache_kv, cache_idx, cache_win, page_table,
           w_in_e, gate_b_e, conv_w, conv_b, w_out_e, w_in_o, w_cmp, w_out_o, w_up, w_down, ln_g, ln_b):
    f32, bf16 = jnp.float32, jnp.bfloat16
    bp, tp, d = x_prompt.shape
    dbs = x_sample.shape[0]
    keep = cache_win.shape[2]
    yp = x_prompt.reshape(bp * tp, d)
    ys = x_sample.reshape(dbs, d)
    tm_p, tm_s = 512, dbs
    outs = {}
    for layer in range(DEPTH):
        i = layer // 2
        if layer % 2 == 0:
            w_in = _pad_cols(w_in_e[i], E_PACKED).astype(bf16)
            w_out = w_out_e[i].astype(bf16)
            ya, qkv, og, gts, tails = _even_proj(yp, w_in, conv_w[i], conv_b[i], tm_p, tp)
            yb, c_fin, n_fin, m_fin = _mlstm_prompt(qkv, og, gts, gate_b_e[i], bp, tp)
            cvp = tails.reshape(bp, tp // tm_p, CONV_TAIL, HALF_W)[:, -1, CONV_TAIL - (CONV_W - 1):]
            ya_s, yb_s, cvs, c_s, n_s, m_s = _even_sample(_proj(ys, w_in, tm_s), gate_b_e[i], conv_w[i], conv_b[i],
                                                          state_conv[i], state_C[i], state_n[i], state_m[i])
            outs['conv'] = (cvp[None], cvs[None])
            outs['c'] = (c_fin[None], c_s[None])
            outs['n'] = (n_fin[None], n_s[None])
            outs['m'] = (m_fin[:, :, 0][None], m_s[:, :, 0][None])
            yp = _outproj2_ln(ya, yb.reshape(bp * tp, HALF_W), w_out, yp, ln_g[layer, 0], ln_b[layer, 0], tm_p)
            ys = _outproj2_ln(ya_s, yb_s, w_out, ys, ln_g[layer, 0], ln_b[layer, 0], tm_s)
        else:
            w_out = w_out_o[i].astype(bf16)
            w_packed = _pack_w_in_o(w_in_o[i]).astype(bf16)
            wc2 = jnp.concatenate([w_cmp[i], w_cmp[i]], axis=-1)
            rows_first = lambda kvt: jnp.transpose(
                kvt.reshape(kvt.shape[0], KV_SLOTS, KV_GROUPS, HEAD_DIM, kvt.shape[2]), (0, 4, 1, 2, 3))
            zq, kvt_p, kvb, winp, miscp = _odd_proj(yp, w_packed, tm_p, tp)
            mp = _odd_attn_prompt(zq, miscp, kvb, wc2, bp, tp)
            kvp = rows_first(kvt_p)
            ixp = miscp[:, :IDX_DIM].reshape(bp, tp, IDX_DIM)
            wnp = winp.reshape(bp, tp, 2, KV_GROUPS, HEAD_DIM)[:, tp - keep:]
            zq_s, kvt_s, kvb_s, win_s, misc_s = _odd_proj(ys, w_packed, tm_s, dbs)
            ms, wns = _odd_attn_sample(zq_s, misc_s, kvb_s, win_s, cache_kv[i], cache_idx[i], cache_win[i],
                                       page_table, w_cmp[i])
            kvs = rows_first(kvt_s).reshape(dbs, 1, KV_SLOTS, KV_GROUPS, HEAD_DIM)
            ixs = misc_s[:, :IDX_DIM].reshape(dbs, 1, IDX_DIM)
            outs['kv'] = (kvp[None], kvs[None])
            outs['idx'] = (ixp[None], ixs[None])
            outs['win'] = (wnp[None], wns[None])
            yp = _outproj_ln(mp, w_out, yp, ln_g[layer, 0], ln_b[layer, 0], tm_p)
            ys = _outproj_ln(ms, w_out, ys, ln_g[layer, 0], ln_b[layer, 0], tm_s)
        wu, wd = w_up[layer].astype(bf16), w_down[layer].astype(bf16)
        yp = _mlp_ln(yp, wu, wd, ln_g[layer, 1], ln_b[layer, 1], 2 * tm_p, 1024)
        ys = _mlp_ln(ys, wu, wd, ln_g[layer, 1], ln_b[layer, 1], tm_s, 1024)
    return (yp.reshape(bp, tp, d), ys.reshape(dbs, 1, d),
            outs['conv'][0], outs['conv'][1], outs['c'][0], outs['c'][1],
            outs['n'][0], outs['n'][1], outs['m'][0], outs['m'][1],
            outs['kv'][0], outs['kv'][1], outs['idx'][0], outs['idx'][1],
            outs['win'][0], outs['win'][1])
```

```python
import functools

import jax
import jax.numpy as jnp
import numpy as np
from jax import lax
from jax.experimental import pallas as pl
from jax.experimental.pallas import tpu as pltpu

D_MODEL = 1024
DEPTH = 2
HALF_W = 512
D_FF = 4096
CONV_W = 3
ML_HEADS = 4
ML_DIM = 128
HEAD_DIM = 64
KV_GROUPS = 2
NSA_HEADS = 8
DSA_HEADS = 8
BLK = 64
NSA_TOPN = 8
WINDOW = 256
IDX_HEADS = 4
IDX_DIM = 32
DSA_TOPK_MAX = 256
KV_SLOTS = 6
FORCE_SCORE = 1e4
ALPHA = (2.0 * DEPTH) ** 0.25
LN_EPS = 1e-5
O_SIZES = (512, 256, 256, 256, 24, 512, 256, 128, 32, 4)
LANES = 128
VMEM_LIMIT = 48 * 1024 * 1024


def _round_up(n, m):
    return -(-n // m) * m


def _proj_kernel(x_ref, w_ref, o_ref):
    o_ref[...] = jnp.dot(x_ref[...].astype(jnp.bfloat16), w_ref[...],
                         preferred_element_type=jnp.float32)


def _proj(x, w_bf16, tm):
    m, k = x.shape
    n = w_bf16.shape[1]
    tn = n
    for cand in (1024, 768, 512, 256, 128):
        if n % cand == 0:
            tn = cand
            break
    return pl.pallas_call(
        _proj_kernel,
        grid=(n // tn, m // tm),
        in_specs=[pl.BlockSpec((tm, k), lambda j, i: (i, 0)),
                  pl.BlockSpec((k, tn), lambda j, i: (0, j))],
        out_specs=pl.BlockSpec((tm, tn), lambda j, i: (i, j)),
        out_shape=jax.ShapeDtypeStruct((m, n), jnp.float32),
        compiler_params=pltpu.CompilerParams(
            dimension_semantics=("arbitrary", "arbitrary"), vmem_limit_bytes=VMEM_LIMIT),
        name="proj",
    )(x, w_bf16)


def _layer_norm_rows(v, g, b):
    mu = jnp.mean(v, axis=-1, keepdims=True)
    d = v - mu
    var = jnp.mean(d * d, axis=-1, keepdims=True)
    return d * lax.rsqrt(var + LN_EPS) * g + b


def _outproj_ln_kernel(y_ref, w_ref, x_ref, g_ref, b_ref, o_ref):
    mix = jnp.dot(y_ref[...].astype(jnp.bfloat16), w_ref[...], preferred_element_type=jnp.float32)
    o_ref[...] = _layer_norm_rows(ALPHA * x_ref[...] + mix, g_ref[...], b_ref[...])


def _outproj_ln(y, w_bf16, x, g, b, tm):
    m, d = x.shape
    k = y.shape[1]
    return pl.pallas_call(
        _outproj_ln_kernel,
        grid=(m // tm,),
        in_specs=[pl.BlockSpec((tm, k), lambda i: (i, 0)),
                  pl.BlockSpec((k, d), lambda i: (0, 0)),
                  pl.BlockSpec((tm, d), lambda i: (i, 0)),
                  pl.BlockSpec((1, d), lambda i: (0, 0)),
                  pl.BlockSpec((1, d), lambda i: (0, 0))],
        out_specs=pl.BlockSpec((tm, d), lambda i: (i, 0)),
        out_shape=jax.ShapeDtypeStruct((m, d), jnp.float32),
        compiler_params=pltpu.CompilerParams(
            dimension_semantics=("arbitrary",), vmem_limit_bytes=VMEM_LIMIT),
        name="outproj_ln",
    )(y, w_bf16, x, g.reshape(1, d), b.reshape(1, d))


def _mlp_ln_kernel(x_ref, wu_ref, wd_ref, g_ref, b_ref, o_ref, acc_ref):
    f = pl.program_id(1)

    @pl.when(f == 0)
    def _():
        acc_ref[...] = jnp.zeros_like(acc_ref)

    up = jnp.dot(x_ref[...].astype(jnp.bfloat16), wu_ref[...], preferred_element_type=jnp.float32)
    act = jnp.square(jnp.maximum(up, 0.0))
    acc_ref[...] += jnp.dot(act.astype(jnp.bfloat16), wd_ref[...], preferred_element_type=jnp.float32)

    @pl.when(f == pl.num_programs(1) - 1)
    def _():
        o_ref[...] = _layer_norm_rows(ALPHA * x_ref[...] + acc_ref[...], g_ref[...], b_ref[...])


def _mlp_ln(x, wu_bf16, wd_bf16, g, b, tm, tf):
    m, d = x.shape
    ff = wu_bf16.shape[1]
    return pl.pallas_call(
        _mlp_ln_kernel,
        grid=(m // tm, ff // tf),
        in_specs=[pl.BlockSpec((tm, d), lambda i, f: (i, 0)),
                  pl.BlockSpec((d, tf), lambda i, f: (0, f)),
                  pl.BlockSpec((tf, d), lambda i, f: (f, 0)),
                  pl.BlockSpec((1, d), lambda i, f: (0, 0)),
                  pl.BlockSpec((1, d), lambda i, f: (0, 0))],
        out_specs=pl.BlockSpec((tm, d), lambda i, f: (i, 0)),
        out_shape=jax.ShapeDtypeStruct((m, d), jnp.float32),
        scratch_shapes=[pltpu.VMEM((tm, d), jnp.float32)],
        compiler_params=pltpu.CompilerParams(
            dimension_semantics=("arbitrary", "arbitrary"), vmem_limit_bytes=VMEM_LIMIT),
        name="mlp_ln",
    )(x, wu_bf16, wd_bf16, g.reshape(1, d), b.reshape(1, d))


E_QKV = 3 * HALF_W
E_OG = 6 * HALF_W
E_GATE = 7 * HALF_W
E_PACKED = 7 * HALF_W + LANES
CONV_TAIL = 8


def _even_proj_kernel(x_ref, w_ref, cw_ref, cb_ref, ya_ref, qkv_ref, og_ref, gate_ref, tail_ref, carry_ref,
                      *, tiles_per_seq):
    f32, bf16 = jnp.float32, jnp.bfloat16
    tm = x_ref.shape[0]

    @pl.when(pl.program_id(0) % tiles_per_seq == 0)
    def _():
        carry_ref[...] = jnp.zeros_like(carry_ref)

    z = jnp.dot(x_ref[...].astype(bf16), w_ref[...], preferred_element_type=f32)
    u = z[:, 2 * HALF_W:3 * HALF_W] * z[:, 0:HALF_W]
    prev = carry_ref[...]
    row = lax.broadcasted_iota(jnp.int32, (tm, 1), 0)
    conv = cb_ref[...] + cw_ref[CONV_W - 1:CONV_W, :] * u
    for back in range(1, CONV_W):
        shifted = pltpu.roll(u, back, 0)
        for r in range(back):
            shifted = jnp.where(row == r, prev[CONV_TAIL - back + r:CONV_TAIL - back + r + 1, :], shifted)
        conv = conv + cw_ref[CONV_W - 1 - back:CONV_W - back, :] * shifted
    ya_ref[...] = (z[:, HALF_W:2 * HALF_W] * conv).astype(bf16)
    tail = u[tm - CONV_TAIL:]
    carry_ref[...] = tail
    tail_ref[0] = tail
    qkv_ref[:, 0:HALF_W] = z[:, E_QKV:E_QKV + HALF_W].astype(bf16)
    qkv_ref[:, HALF_W:2 * HALF_W] = (z[:, E_QKV + HALF_W:E_QKV + 2 * HALF_W] * (ML_DIM ** -0.5)).astype(bf16)
    qkv_ref[:, 2 * HALF_W:3 * HALF_W] = z[:, E_QKV + 2 * HALF_W:E_OG].astype(bf16)
    og_ref[...] = z[:, E_OG:E_GATE]
    gate_ref[...] = z[:, E_GATE:E_PACKED]


def _even_proj(x, w_bf16, conv_w, conv_b, tm, seq_len):
    m, k = x.shape
    widths = (HALF_W, 3 * HALF_W, HALF_W, LANES)
    dtypes = (jnp.bfloat16, jnp.bfloat16, jnp.float32, jnp.float32)
    kern = functools.partial(_even_proj_kernel, tiles_per_seq=seq_len // tm)
    return pl.pallas_call(
        kern,
        grid=(m // tm,),
        in_specs=[pl.BlockSpec((tm, k), lambda i: (i, 0)),
                  pl.BlockSpec((k, E_PACKED), lambda i: (0, 0)),
                  pl.BlockSpec((CONV_W, HALF_W), lambda i: (0, 0)),
                  pl.BlockSpec((1, HALF_W), lambda i: (0, 0))],
        out_specs=[pl.BlockSpec((tm, n), lambda i: (i, 0)) for n in widths]
        + [pl.BlockSpec((1, CONV_TAIL, HALF_W), lambda i: (i, 0, 0))],
        out_shape=[jax.ShapeDtypeStruct((m, n), dt) for n, dt in zip(widths, dtypes)]
        + [jax.ShapeDtypeStruct((m // tm, CONV_TAIL, HALF_W), jnp.float32)],
        scratch_shapes=[pltpu.VMEM((CONV_TAIL, HALF_W), jnp.float32)],
        compiler_params=pltpu.CompilerParams(
            dimension_semantics=("arbitrary",), vmem_limit_bytes=VMEM_LIMIT),
        name="even_proj",
    )(x, w_bf16, conv_w, conv_b.reshape(1, HALF_W))


def _mlstm_kernel(qkv_ref, og_ref, gate_ref, gb_ref, yb_ref, c_ref, n_ref, m_ref):
    f32, bf16 = jnp.float32, jnp.bfloat16
    L = qkv_ref.shape[1]

    @pl.when(pl.program_id(1) == 0)
    def _():
        c_ref[...] = jnp.zeros_like(c_ref)
        n_ref[...] = jnp.zeros_like(n_ref)
        m_ref[...] = jnp.zeros_like(m_ref)

    pre = gate_ref[0] + gb_ref[...]
    lf = jax.nn.log_sigmoid(pre)
    row = lax.broadcasted_iota(jnp.int32, (L, L), 0)
    col = lax.broadcasted_iota(jnp.int32, (L, L), 1)
    causal = col <= row
    tri = jnp.where(causal, 1.0, 0.0).astype(f32)
    b_all = jnp.dot(tri, lf, preferred_element_type=f32, precision=lax.Precision.HIGHEST)
    pre_t = pre.T
    b_t = b_all.T
    for h in range(ML_HEADS):
        q = qkv_ref[0, :, h * ML_DIM:(h + 1) * ML_DIM]
        k = qkv_ref[0, :, HALF_W + h * ML_DIM:HALF_W + (h + 1) * ML_DIM]
        v = qkv_ref[0, :, 2 * HALF_W + h * ML_DIM:2 * HALF_W + (h + 1) * ML_DIM]
        ig_col = pre[:, h:h + 1]
        b_col = b_all[:, ML_HEADS + h:ML_HEADS + h + 1]
        a_row = pre_t[h:h + 1, :] - b_t[ML_HEADS + h:ML_HEADS + h + 1, :]
        m_st = m_ref[0, h:h + 1, 0:1]
        c_st = c_ref[0, h]
        n_st = n_ref[0, h:h + 1, :]
        dmat = jnp.where(causal, b_col + a_row, -jnp.inf)
        inter = b_col + m_st
        m_t = jnp.maximum(inter, jnp.max(dmat, axis=-1, keepdims=True))
        w_intra = jnp.exp(dmat - m_t)
        w_inter = jnp.exp(inter - m_t)
        s = _dot_nt(q, k) * w_intra
        num = jnp.dot(s.astype(bf16), v, preferred_element_type=f32) + w_inter * _dot_nt(q, c_st.astype(bf16))
        den = jnp.sum(s, axis=-1, keepdims=True) + w_inter * jnp.sum(q.astype(f32) * n_st, axis=-1, keepdims=True)
        hs = num * (1.0 / jnp.maximum(jnp.abs(den), jnp.exp(-m_t)))
        m_new = m_t[L - 1:L, :]
        b_last = b_col[L - 1:L, :]
        w_state = jnp.exp(b_last - b_col + ig_col - m_new)
        decay = jnp.exp(b_last + m_st - m_new)
        vw = (v.astype(f32) * w_state).astype(bf16)
        c_ref[0, h] = decay * c_st + lax.dot_general(vw, k, (((0,), (0,)), ((), ())), preferred_element_type=f32)
        n_ref[0, h:h + 1, :] = decay * n_st + jnp.sum(k.astype(f32) * w_state, axis=0, keepdims=True)
        m_ref[0, h:h + 1, :] = jnp.broadcast_to(m_new, (1, LANES))
        og = og_ref[0, :, h * ML_DIM:(h + 1) * ML_DIM]
        yb_ref[0, :, h * ML_DIM:(h + 1) * ML_DIM] = (jax.nn.sigmoid(og) * hs).astype(bf16)


def _mlstm_prompt(qkv, og, gates, gate_b, bsz, t_len, chunk=128):
    gb = jnp.pad(gate_b, (0, LANES - gate_b.shape[0])).reshape(1, LANES)
    return pl.pallas_call(
        _mlstm_kernel,
        grid=(bsz, t_len // chunk),
        in_specs=[pl.BlockSpec((1, chunk, 3 * HALF_W), lambda b, c: (b, c, 0)),
                  pl.BlockSpec((1, chunk, HALF_W), lambda b, c: (b, c, 0)),
                  pl.BlockSpec((1, chunk, LANES), lambda b, c: (b, c, 0)),
                  pl.BlockSpec((1, LANES), lambda b, c: (0, 0))],
        out_specs=[pl.BlockSpec((1, chunk, HALF_W), lambda b, c: (b, c, 0)),
                   pl.BlockSpec((1, ML_HEADS, ML_DIM, ML_DIM), lambda b, c: (b, 0, 0, 0)),
                   pl.BlockSpec((1, ML_HEADS, ML_DIM), lambda b, c: (b, 0, 0)),
                   pl.BlockSpec((1, ML_HEADS, LANES), lambda b, c: (b, 0, 0))],
        out_shape=[jax.ShapeDtypeStruct((bsz, t_len, HALF_W), jnp.bfloat16),
                   jax.ShapeDtypeStruct((bsz, ML_HEADS, ML_DIM, ML_DIM), jnp.float32),
                   jax.ShapeDtypeStruct((bsz, ML_HEADS, ML_DIM), jnp.float32),
                   jax.ShapeDtypeStruct((bsz, ML_HEADS, LANES), jnp.float32)],
        compiler_params=pltpu.CompilerParams(
            dimension_semantics=("arbitrary", "arbitrary"), vmem_limit_bytes=VMEM_LIMIT),
        name="mlstm_prompt",
    )(qkv.reshape(bsz, t_len, 3 * HALF_W), og.reshape(bsz, t_len, HALF_W),
      gates.reshape(bsz, t_len, LANES), gb)


def _outproj2_ln_kernel(ya_ref, yb_ref, w_ref, x_ref, g_ref, b_ref, o_ref):
    mix = jnp.dot(ya_ref[...], w_ref[0:HALF_W, :], preferred_element_type=jnp.float32)
    mix = mix + jnp.dot(yb_ref[...], w_ref[HALF_W:, :], preferred_element_type=jnp.float32)
    o_ref[...] = _layer_norm_rows(ALPHA * x_ref[...] + mix, g_ref[...], b_ref[...])


def _outproj2_ln(ya, yb, w_bf16, x, g, b, tm):
    m, d = x.shape
    return pl.pallas_call(
        _outproj2_ln_kernel,
        grid=(m // tm,),
        in_specs=[pl.BlockSpec((tm, HALF_W), lambda i: (i, 0)),
                  pl.BlockSpec((tm, HALF_W), lambda i: (i, 0)),
                  pl.BlockSpec((2 * HALF_W, d), lambda i: (0, 0)),
                  pl.BlockSpec((tm, d), lambda i: (i, 0)),
                  pl.BlockSpec((1, d), lambda i: (0, 0)),
                  pl.BlockSpec((1, d), lambda i: (0, 0))],
        out_specs=pl.BlockSpec((tm, d), lambda i: (i, 0)),
        out_shape=jax.ShapeDtypeStruct((m, d), jnp.float32),
        compiler_params=pltpu.CompilerParams(
            dimension_semantics=("arbitrary",), vmem_limit_bytes=VMEM_LIMIT),
        name="outproj2_ln",
    )(ya, yb, w_bf16, x, g.reshape(1, d), b.reshape(1, d))


SAMPLE_ROWS = 8


def _even_sample_kernel(h_ref, bg_ref, cg_ref, q_ref, k_ref, v_ref, og_ref, gate_ref, gb_ref, cw_ref, cb_ref,
                        conv_ref, c_ref, n_ref, m_ref, ya_ref, yb_ref, conv_out_ref, c_out_ref, n_out_ref, m_out_ref):
    f32, bf16 = jnp.float32, jnp.bfloat16
    rows = h_ref.shape[0]
    row = lax.broadcasted_iota(jnp.int32, (rows, 1), 0)
    tn = (((0,), (0,)), ((), ()))
    u = cg_ref[...] * h_ref[...]
    prev0, prev1 = conv_ref[:, 0, :], conv_ref[:, 1, :]
    conv = cb_ref[...] + cw_ref[0:1, :] * prev0 + cw_ref[1:2, :] * prev1 + cw_ref[2:3, :] * u
    ya_ref[...] = (bg_ref[...] * conv).astype(bf16)
    conv_out_ref[:, 0, :] = prev1
    conv_out_ref[:, 1, :] = u
    pre = gate_ref[...] + gb_ref[...]
    lf_all = jax.nn.log_sigmoid(pre)
    for h in range(ML_HEADS):
        cols = slice(h * ML_DIM, (h + 1) * ML_DIM)
        q = q_ref[:, cols]
        k = k_ref[:, cols] * (ML_DIM ** -0.5)
        v = v_ref[:, cols]
        ig = pre[:, h:h + 1]
        lf = lf_all[:, ML_HEADS + h:ML_HEADS + h + 1]
        m_st = m_ref[:, h:h + 1]
        n_st = n_ref[:, h, :]
        inter = lf + m_st
        m_t = jnp.maximum(inter, ig)
        w_intra = jnp.exp(ig - m_t)
        w_inter = jnp.exp(inter - m_t)
        qb, kb = q.astype(bf16), k.astype(bf16)
        s = jnp.sum(qb.astype(f32) * kb.astype(f32), axis=-1, keepdims=True) * w_intra
        cq = jnp.zeros((rows, ML_DIM), f32)
        for b in range(rows):
            c_b = c_ref[b, h]
            cq = jnp.where(row == b, _dot_nt(qb, c_b.astype(bf16)), cq)
            vw = jnp.where(row == b, v * w_intra, 0.0)
            outer = lax.dot_general(vw, k, tn, preferred_element_type=f32, precision=lax.Precision.HIGHEST)
            c_out_ref[b, h] = w_inter[b:b + 1, :] * c_b + outer
        num = s * v + w_inter * cq
        den = s + w_inter * jnp.sum(n_st * q, axis=-1, keepdims=True)
        hs = num * (1.0 / jnp.maximum(jnp.abs(den), jnp.exp(-m_t)))
        n_out_ref[:, h, :] = w_inter * n_st + w_intra * k
        m_out_ref[:, h, :] = jnp.broadcast_to(m_t, (rows, LANES))
        yb_ref[:, cols] = (jax.nn.sigmoid(og_ref[:, cols]) * hs).astype(bf16)


def _even_sample(z, gate_b, conv_w, conv_b, state_conv, state_c, state_n, state_m):
    s_n = z.shape[0]
    f32 = jnp.float32
    r = SAMPLE_ROWS
    gb = jnp.pad(gate_b, (0, LANES - gate_b.shape[0])).reshape(1, LANES)
    col = lambda j: pl.BlockSpec((r, HALF_W), lambda i, j=j: (i, j))
    in_specs = [col(j) for j in range(7)] + [
        pl.BlockSpec((r, LANES), lambda i: (i, E_GATE // LANES)),
        pl.BlockSpec((1, LANES), lambda i: (0, 0)),
        pl.BlockSpec((CONV_W, HALF_W), lambda i: (0, 0)),
        pl.BlockSpec((1, HALF_W), lambda i: (0, 0)),
        pl.BlockSpec((r, CONV_W - 1, HALF_W), lambda i: (i, 0, 0)),
        pl.BlockSpec((r, ML_HEADS, ML_DIM, ML_DIM), lambda i: (i, 0, 0, 0)),
        pl.BlockSpec((r, ML_HEADS, ML_DIM), lambda i: (i, 0, 0)),
        pl.BlockSpec((r, ML_HEADS), lambda i: (i, 0))]
    out_specs = [pl.BlockSpec((r, HALF_W), lambda i: (i, 0)),
                 pl.BlockSpec((r, HALF_W), lambda i: (i, 0)),
                 pl.BlockSpec((r, CONV_W - 1, HALF_W), lambda i: (i, 0, 0)),
                 pl.BlockSpec((r, ML_HEADS, ML_DIM, ML_DIM), lambda i: (i, 0, 0, 0)),
                 pl.BlockSpec((r, ML_HEADS, ML_DIM), lambda i: (i, 0, 0)),
                 pl.BlockSpec((r, ML_HEADS, LANES), lambda i: (i, 0, 0))]
    out_shape = [jax.ShapeDtypeStruct((s_n, HALF_W), jnp.bfloat16),
                 jax.ShapeDtypeStruct((s_n, HALF_W), jnp.bfloat16),
                 jax.ShapeDtypeStruct((s_n, CONV_W - 1, HALF_W), f32),
                 jax.ShapeDtypeStruct((s_n, ML_HEADS, ML_DIM, ML_DIM), f32),
                 jax.ShapeDtypeStruct((s_n, ML_HEADS, ML_DIM), f32),
                 jax.ShapeDtypeStruct((s_n, ML_HEADS, LANES), f32)]
    return pl.pallas_call(
        _even_sample_kernel,
        grid=(s_n // r,),
        in_specs=in_specs,
        out_specs=out_specs,
        out_shape=out_shape,
        compiler_params=pltpu.CompilerParams(
            dimension_semantics=("arbitrary",), vmem_limit_bytes=VMEM_LIMIT),
        name="even_sample",
    )(z, z, z, z, z, z, z, z, gb, conv_w, conv_b.reshape(1, HALF_W), state_conv, state_c, state_n, state_m)


O_Q = 0
O_QD = NSA_HEADS * LANES
O_KV = 2048
O_WIN = 2816
O_QI = 3072
O_MISC = 3200
O_PACKED = 3328
ZQ_QI = 2048
MISC_WI = IDX_DIM
MISC_GATE = IDX_DIM + IDX_HEADS
NEG_BIG = -(2.0 ** 100)
INT_MIN = -(2 ** 31)
ALIBI = tuple(float(2.0 ** (-8.0 * (h + 1) / NSA_HEADS)) for h in range(NSA_HEADS))


def _pack_w_in_o(w):
    qn, kvc, kvs, kvw, gates, qd, kvd, qi, ki, wi = _split(w, O_SIZES)
    scale = HEAD_DIM ** -0.5
    rows = w.shape[0]

    def spread(q):
        q = (q * scale).reshape(rows, NSA_HEADS, HEAD_DIM)
        z = jnp.zeros_like(q)
        low = (jnp.arange(NSA_HEADS) < NSA_HEADS // KV_GROUPS)[None, :, None]
        return jnp.concatenate([jnp.where(low, q, z), jnp.where(low, z, q)], axis=-1).reshape(rows, NSA_HEADS * LANES)

    pad = jnp.zeros((rows, O_PACKED - O_MISC - IDX_DIM - IDX_HEADS - 3 * NSA_HEADS), w.dtype)
    return jnp.concatenate([spread(qn), spread(qd), kvc, kvs, kvd, kvw, qi, ki, wi, gates, pad], axis=1)


def _odd_proj_kernel(x_ref, w_ref, wkvt_ref, zq_ref, kvt_ref, kvb_ref, win_ref, misc_ref):
    xb = x_ref[...].astype(jnp.bfloat16)
    z = jnp.dot(xb, w_ref[...], preferred_element_type=jnp.float32)
    zq_ref[:, 0:O_KV] = z[:, 0:O_KV].astype(jnp.bfloat16)
    zq_ref[:, ZQ_QI:ZQ_QI + LANES] = z[:, O_QI:O_MISC].astype(jnp.bfloat16)
    kvt_ref[0] = _dot_nt(wkvt_ref[...], xb)
    kvb_ref[...] = z[:, O_KV:O_QI].astype(jnp.bfloat16)
    win_ref[...] = z[:, O_WIN:O_QI]
    misc_ref[...] = z[:, O_MISC:O_PACKED]


def _odd_proj(x, w_packed_bf16, tm, seq_len):
    m, k = x.shape
    n_kv = O_WIN - O_KV
    widths = (O_KV + LANES, O_QI - O_KV, O_QI - O_WIN, LANES)
    dtypes = (jnp.bfloat16, jnp.bfloat16, jnp.float32, jnp.float32)
    row_spec = lambda n: pl.BlockSpec((tm, n), lambda i: (i, 0))
    tiles = seq_len // tm
    out_specs = [row_spec(widths[0]), pl.BlockSpec((1, n_kv, tm), lambda i: (i // tiles, 0, i % tiles))]
    out_specs += [row_spec(n) for n in widths[1:]]
    out_shape = [jax.ShapeDtypeStruct((m, widths[0]), dtypes[0]),
                 jax.ShapeDtypeStruct((m // seq_len, n_kv, seq_len), jnp.float32)]
    out_shape += [jax.ShapeDtypeStruct((m, n), dt) for n, dt in zip(widths[1:], dtypes[1:])]
    return pl.pallas_call(
        _odd_proj_kernel,
        grid=(m // tm,),
        in_specs=[pl.BlockSpec((tm, k), lambda i: (i, 0)),
                  pl.BlockSpec((k, O_PACKED), lambda i: (0, 0)),
                  pl.BlockSpec((n_kv, k), lambda i: (0, 0))],
        out_specs=out_specs,
        out_shape=out_shape,
        compiler_params=pltpu.CompilerParams(
            dimension_semantics=("arbitrary",), vmem_limit_bytes=VMEM_LIMIT),
        name="odd_proj",
    )(x, w_packed_bf16, w_packed_bf16[:, O_KV:O_WIN].T)


def _dot_nt(a, b):
    return lax.dot_general(a, b, (((1,), (1,)), ((), ())), preferred_element_type=jnp.float32)


def _flash_groups(q_by_group, slopes_by_group, kv_ref, pos_ref, k_blk, v_blk, tk, lo, hi, bias_fns, skip_fn=None,
                  start_fn=None):
    f32, bf16 = jnp.float32, jnp.bfloat16
    if start_fn is None:
        start_fn = lambda kt: kt * tk
    n_groups = len(q_by_group)
    tq = q_by_group[0][0].shape[0]
    n_heads = len(q_by_group[0])
    lane = lax.broadcasted_iota(jnp.int32, (1, LANES), 1)
    own_half = [(lane // HEAD_DIM) == g for g in range(n_groups)]
    feat_lane = lax.broadcasted_iota(jnp.int32, (tq, LANES), 1)
    q_all = []
    for q_heads, slopes in zip(q_by_group, slopes_by_group):
        slope_feat = [jnp.where(feat_lane == 0, BLK * sl, jnp.where(feat_lane == 1, sl, 0.0)).astype(bf16)
                      for sl in slopes]
        q_all.append(jnp.concatenate([jnp.concatenate(q_heads, axis=0), jnp.concatenate(slope_feat, axis=0)], axis=1))
    tn = (((0,), (0,)), ((), ()))

    def tile(kt, carry):
        start = pl.multiple_of(start_fn(kt), LANES)
        rows = pl.ds(start, tk)
        k = kv_ref[0, rows, k_blk * LANES:(k_blk + 1) * LANES]
        v = kv_ref[0, rows, v_blk * LANES:(v_blk + 1) * LANES]
        k_pos = jnp.concatenate([k, pos_ref[rows, :]], axis=1)
        scores = [_dot_nt(k_pos, q) for q in q_all]
        biases = [fn(kt) for fn in bias_fns] if len(bias_fns) > 1 else [bias_fns[0](kt)] * n_groups
        new_m, probs, alphas = [], [], []
        for g in range(n_groups):
            m = carry[g][0]
            s = scores[g] + jnp.concatenate([biases[g]] * n_heads, axis=1)
            m_new = jnp.maximum(m, jnp.max(s, axis=0, keepdims=True))
            probs.append(jnp.exp(s - m_new).astype(bf16))
            alphas.append(jnp.exp(m - m_new))
            new_m.append(m_new)
        out = []
        for g in range(n_groups):
            v_g = jnp.where(own_half[g], v, jnp.ones_like(v))
            pv = lax.dot_general(v_g, probs[g], tn, preferred_element_type=f32)
            out.append((new_m[g], alphas[g] * carry[g][1] + pv))
        return tuple(out)

    def body(kt, carry):
        if skip_fn is None:
            return tile(kt, carry)
        return lax.cond(skip_fn(kt), lambda c: c, functools.partial(tile, kt), carry)

    init = tuple((jnp.full((1, n_heads * tq), -jnp.inf, f32), jnp.zeros((LANES, n_heads * tq), f32))
                 for _ in range(n_groups))
    final = lax.fori_loop(lo, hi, body, init)
    return [acc for _, acc in final]


def _odd_attn_kernel(zq_ref, miscq_ref, kvb_ref, misck_ref, wc_ref, o_ref,
                     kc_ref, vc_ref, ki_ref, key_ref, dbias_ref, flag_ref, pos_ref, *, t_len, tq, tk, topk):
    f32, bf16, i32 = jnp.float32, jnp.bfloat16, jnp.int32
    nb = t_len // BLK
    qi_blk = pl.program_id(1)
    q0 = qi_blk * tq

    @pl.when(qi_blk == 0)
    def _():
        ck = kvb_ref[0, :, 0:LANES].astype(f32).reshape(nb, BLK, LANES)
        kc_ref[...] = jnp.sum(ck * wc_ref[0][None], axis=1).astype(bf16)
        cv = kvb_ref[0, :, LANES:2 * LANES].astype(f32).reshape(nb, BLK, LANES)
        vc = jnp.sum(cv * wc_ref[1][None], axis=1)
        for g in range(KV_GROUPS):
            half = vc[:, g * HEAD_DIM:(g + 1) * HEAD_DIM]
            vc_ref[g] = jnp.concatenate([half, half], axis=1).astype(bf16)
        ki_ref[...] = misck_ref[0, :, 0:IDX_DIM].astype(bf16)
        key_idx = lax.broadcasted_iota(i32, (t_len, LANES), 0)
        feat = lax.broadcasted_iota(i32, (t_len, LANES), 1)
        pos_ref[...] = jnp.where(feat == 0, key_idx // BLK, jnp.where(feat == 1, key_idx % BLK, 0)).astype(bf16)

    misc = miscq_ref[0]
    gates = jax.nn.sigmoid(misc[:, MISC_GATE:MISC_GATE + 3 * NSA_HEADS])
    t_col = q0 + lax.broadcasted_iota(i32, (tq, 1), 0)
    lane = lax.broadcasted_iota(i32, (1, LANES), 1)
    low_half = lane < HEAD_DIM
    heads_per_group = NSA_HEADS // KV_GROUPS

    def q_head(base, h):
        return zq_ref[0, :, base + h * LANES: base + (h + 1) * LANES]

    def pair(even, odd):
        return jnp.where(low_half, even, odd)

    def normalised_pair(acc_t, r_even, g):
        halves = []
        for r in (r_even, r_even + 1):
            cols = slice(r * tq, (r + 1) * tq)
            out = acc_t[g * HEAD_DIM:(g + 1) * HEAD_DIM, cols]
            den = acc_t[(1 - g) * HEAD_DIM:(1 - g) * HEAD_DIM + 1, cols]
            halves.append(out * (1.0 / den))
        return jnp.concatenate(halves, axis=0).T

    def gate_pair(j, c):
        a = gates[:, (2 * j) * 3 + c:(2 * j) * 3 + c + 1]
        b = gates[:, (2 * j + 1) * 3 + c:(2 * j + 1) * 3 + c + 1]
        return jnp.where(low_half, a, b)

    tn = (((0,), (0,)), ((), ()))
    t_lanes = q0 + lax.broadcasted_iota(i32, (1, tq), 1)
    jb = lax.broadcasted_iota(i32, (nb, 1), 0)
    dist_c = t_lanes - (jb * BLK + (BLK - 1))
    mask_c = dist_c >= 0
    dist_cf = dist_c.astype(f32)
    cur = t_lanes // BLK
    forced = (jb == 0) | (jb == cur) | (jb == cur - 1)
    admissible = jb <= cur
    jb_full = lax.broadcasted_iota(i32, (nb, tq), 0)
    o_cmp, sel_bias, sel_any = [], [], []
    for g in range(KV_GROUPS):
        imp = jnp.zeros((nb, tq), f32)
        for r in range(heads_per_group):
            h = g * heads_per_group + r
            s = _dot_nt(kc_ref[...], q_head(O_Q, h)) - ALIBI[h] * dist_cf
            s = jnp.where(mask_c, s, NEG_BIG)
            m = jnp.max(s, axis=0, keepdims=True)
            e = jnp.where(mask_c, jnp.exp(s - m), 0.0)
            p = e * (1.0 / jnp.maximum(jnp.sum(e, axis=0, keepdims=True), 1e-30))
            imp = imp + p
            o_cmp.append(lax.dot_general(p.astype(bf16), vc_ref[g], tn, preferred_element_type=f32))
        imp = jnp.where(forced, FORCE_SCORE, imp)
        imp = jnp.where(admissible, imp, -jnp.inf)
        sel = jnp.zeros((nb, tq), f32)
        for _ in range(min(NSA_TOPN, nb)):
            m = jnp.max(imp, axis=0, keepdims=True)
            first = jnp.min(jnp.where(imp == m, jb_full, nb), axis=0, keepdims=True)
            hit = jb_full == first
            sel = jnp.where(hit & (m > -jnp.inf), 1.0, sel)
            imp = jnp.where(hit, -jnp.inf, imp)
        sel_bias.append(jnp.where(sel > 0.0, 0.0, NEG_BIG).astype(bf16))
        sel_any.append(jnp.max(sel, axis=1, keepdims=True))

    n_sel_tiles = (q0 + tq + tk - 1) // tk
    blocks_per_tile = tk // BLK

    for g in range(KV_GROUPS):
        for j in range(nb // blocks_per_tile):
            hit = jnp.max(sel_any[g][j * blocks_per_tile:(j + 1) * blocks_per_tile, :])
            flag_ref[g, j] = (hit > 0.0).astype(i32)

    def causal_bias(start, width):
        sp = start + lax.broadcasted_iota(i32, (width, 1), 0)
        return jnp.where(sp <= t_lanes, 0.0, NEG_BIG)

    def sel_bias_fn(g):
        def fn(kt):
            row = lax.broadcasted_iota(i32, (nb, tk), 0)
            col = lax.broadcasted_iota(i32, (nb, tk), 1)
            expand = jnp.where(row == kt * blocks_per_tile + col // BLK, 1.0, 0.0).astype(bf16)
            spread = lax.dot_general(expand, sel_bias[g], tn, preferred_element_type=f32)
            return spread + causal_bias(kt * tk, tk)
        return fn

    tkw = min(WINDOW + tq, t_len)
    win_start = jnp.clip(q0 - WINDOW, 0, t_len - tkw)

    def win_bias_fn(kt):
        dist = t_lanes - (win_start + lax.broadcasted_iota(i32, (tkw, 1), 0))
        return jnp.where((dist >= 0) & (dist < WINDOW), 0.0, NEG_BIG)

    qi_heads = [zq_ref[0, :, ZQ_QI + h * IDX_DIM: ZQ_QI + (h + 1) * IDX_DIM] for h in range(IDX_HEADS)]
    pick = jnp.where(lax.broadcasted_iota(i32, (8, LANES), 1) == MISC_WI + lax.broadcasted_iota(i32, (8, LANES), 0),
                     1.0, 0.0)
    wi_rows = lax.dot_general(pick, misc, (((1,), (1,)), ((), ())), preferred_element_type=f32,
                              precision=lax.Precision.HIGHEST)
    t_row = q0 + lax.broadcasted_iota(i32, (1, tq), 1)

    def fold8(x):
        x = x.reshape(tk // 8, 8, tq)
        while x.shape[0] > 1:
            half = x.shape[0] // 2
            x = x[:half] + x[half:]
        return x[0]

    def index_tile(kt, carry):
        start = pl.multiple_of(kt * tk, tk)
        kik = ki_ref[pl.ds(start, tk), :]
        tot = jnp.zeros((tk, tq), f32)
        for h in range(IDX_HEADS):
            sc = _dot_nt(kik, qi_heads[h]) * (IDX_DIM ** -0.5)
            tot = tot + jnp.maximum(sc, 0.0) * wi_rows[h:h + 1, :]
        tot = tot * (IDX_HEADS ** -0.5)
        tot = jnp.where(tot == 0.0, 0.0, tot)
        bits = lax.bitcast_convert_type(tot, i32)
        key = jnp.where(bits < 0, bits ^ jnp.int32(0x7FFFFFFF), bits)
        sp = start + lax.broadcasted_iota(i32, (tk, 1), 0)
        key_ref[kt] = jnp.where(sp <= t_row, key, INT_MIN)
        return carry

    lax.fori_loop(0, n_sel_tiles, index_tile, 0)

    def count_ge(trial):
        def body(kt, acc):
            return acc + fold8((key_ref[kt] >= trial).astype(i32))
        acc = lax.fori_loop(0, n_sel_tiles, body, jnp.zeros((8, tq), i32))
        return jnp.sum(acc, axis=0, keepdims=True)

    c_pos = count_ge(jnp.zeros((1, tq), i32))
    v0 = jnp.where(c_pos >= topk, 0, INT_MIN).astype(i32)
    c0 = jnp.where(c_pos >= topk, c_pos, n_sel_tiles * tk)

    def bit_step(i, carry):
        v, cge = carry
        trial = v | lax.shift_left(jnp.int32(1), 30 - i)
        c = count_ge(trial)
        ok = c >= topk
        return jnp.where(ok, trial, v), jnp.where(ok, c, cge)

    v_thr, c_ge = lax.fori_loop(0, 31, bit_step, (v0, c0))
    v_eff = jnp.maximum(v_thr, INT_MIN + 1)

    def write_plain(kt, carry):
        dbias_ref[kt] = jnp.where(key_ref[kt] >= v_eff, 0.0, NEG_BIG)
        return carry

    def plain_path():
        lax.fori_loop(0, n_sel_tiles, write_plain, 0)

    def tie_path():
        need = topk - count_ge(v_thr + 1)

        def count_tie_below(limit):
            def body(kt, acc):
                sp = kt * tk + lax.broadcasted_iota(i32, (tk, 1), 0)
                return acc + fold8(((key_ref[kt] == v_thr) & (sp < limit)).astype(i32))
            acc = lax.fori_loop(0, n_sel_tiles, body, jnp.zeros((8, tq), i32))
            return jnp.sum(acc, axis=0, keepdims=True)

        n_bits = max(1, (t_len - 1).bit_length())

        def idx_step(i, w):
            trial = w | lax.shift_left(jnp.int32(1), n_bits - 1 - i)
            return jnp.where(count_tie_below(trial) < need, trial, w)

        w_last = lax.fori_loop(0, n_bits, idx_step, jnp.zeros((1, tq), i32))

        def write_tie(kt, carry):
            sp = kt * tk + lax.broadcasted_iota(i32, (tk, 1), 0)
            key = key_ref[kt]
            take = (key > v_thr) | ((key == v_thr) & (sp <= w_last))
            dbias_ref[kt] = jnp.where(take & (key >= v_eff), 0.0, NEG_BIG)
            return carry

        lax.fori_loop(0, n_sel_tiles, write_tie, 0)

    lax.cond(jnp.max(c_ge) > topk, tie_path, plain_path)

    def dsa_bias_fn(kt):
        return dbias_ref[kt]

    heads_of = [[g * heads_per_group + r for r in range(heads_per_group)] for g in range(KV_GROUPS)]
    slopes = [[ALIBI[h] for h in hs] for hs in heads_of]
    qn = [[q_head(O_Q, h) for h in hs] for hs in heads_of]
    qd = [[q_head(O_QD, h) for h in hs] for hs in heads_of]
    sel_skip = lambda kt: (flag_ref[0, kt] + flag_ref[1, kt]) == 0
    a_s = _flash_groups(qn, slopes, kvb_ref, pos_ref, 2, 3, tk, 0, n_sel_tiles,
                        [sel_bias_fn(g) for g in range(KV_GROUPS)], sel_skip)
    a_w = _flash_groups(qn, slopes, kvb_ref, pos_ref, 6, 7, tkw, 0, 1, [win_bias_fn], start_fn=lambda kt: win_start)
    a_d = _flash_groups(qd, slopes, kvb_ref, pos_ref, 4, 5, tk, 0, n_sel_tiles, [dsa_bias_fn])
    for g in range(KV_GROUPS):
        for jj in range(heads_per_group // 2):
            j = heads_of[g][2 * jj] // 2
            o_n = (gate_pair(j, 0) * pair(o_cmp[2 * j], o_cmp[2 * j + 1])
                   + gate_pair(j, 1) * normalised_pair(a_s[g], 2 * jj, g)
                   + gate_pair(j, 2) * normalised_pair(a_w[g], 2 * jj, g))
            o_ref[0, :, j * LANES:(j + 1) * LANES] = o_n.astype(o_ref.dtype)
            o_d = normalised_pair(a_d[g], 2 * jj, g)
            o_ref[0, :, HALF_W + j * LANES:HALF_W + (j + 1) * LANES] = o_d.astype(o_ref.dtype)


def _odd_attn_prompt(zq, misc, kvb, wc2, bsz, t_len, tq=256, tk=512):
    tk = min(tk, t_len)
    topk = min(DSA_TOPK_MAX, t_len // 4)
    nb = t_len // BLK
    n_kt = t_len // tk
    zq3 = zq.reshape(bsz, t_len, zq.shape[1])
    misc3 = misc.reshape(bsz, t_len, LANES)
    kvb3 = kvb.reshape(bsz, t_len, kvb.shape[1])
    kern = functools.partial(_odd_attn_kernel, t_len=t_len, tq=tq, tk=tk, topk=topk)
    out = pl.pallas_call(
        kern,
        grid=(bsz, t_len // tq),
        in_specs=[pl.BlockSpec((1, tq, zq.shape[1]), lambda b, i: (b, i, 0)),
                  pl.BlockSpec((1, tq, LANES), lambda b, i: (b, i, 0)),
                  pl.BlockSpec((1, t_len, kvb.shape[1]), lambda b, i: (b, 0, 0)),
                  pl.BlockSpec((1, t_len, LANES), lambda b, i: (b, 0, 0)),
                  pl.BlockSpec((2, BLK, LANES), lambda b, i: (0, 0, 0))],
        out_specs=pl.BlockSpec((1, tq, 2 * HALF_W), lambda b, i: (b, i, 0)),
        out_shape=jax.ShapeDtypeStruct((bsz, t_len, 2 * HALF_W), jnp.bfloat16),
        scratch_shapes=[pltpu.VMEM((nb, LANES), jnp.bfloat16),
                        pltpu.VMEM((KV_GROUPS, nb, LANES), jnp.bfloat16),
                        pltpu.VMEM((t_len, IDX_DIM), jnp.bfloat16),
                        pltpu.VMEM((n_kt, tk, tq), jnp.int32),
                        pltpu.VMEM((n_kt, tk, tq), jnp.float32),
                        pltpu.SMEM((KV_GROUPS, n_kt), jnp.int32),
                        pltpu.VMEM((t_len, LANES), jnp.bfloat16)],
        compiler_params=pltpu.CompilerParams(
            dimension_semantics=("arbitrary", "arbitrary"), vmem_limit_bytes=VMEM_LIMIT),
        name="odd_attn_prompt",
    )(zq3, misc3, kvb3, misc3, wc2)
    return out.reshape(bsz * t_len, 2 * HALF_W)


PAGE = 128
ISC_ROWS = 24


SAMPLE_SEQS = 2
_DONE = object()


def _odd_sample_kernel(pt_ref, *refs, n_pages, topk, nbp):
    del pt_ref
    seqs = [_odd_sample_one(u, *refs, n_pages=n_pages, topk=topk, nbp=nbp) for u in range(SAMPLE_SEQS)]
    while seqs:
        seqs = [g for g in seqs if next(g, _DONE) is not _DONE]


def _odd_sample_one(u, qn_ref, qd_ref, qi_ref, wi_ref, gate_ref, newb_ref, newki_ref, neww_ref,
                    win_ref, wc_ref, wct_ref, *rest, n_pages, topk, nbp):
    f32, bf16, i32 = jnp.float32, jnp.bfloat16, jnp.int32
    all_pages = SAMPLE_SEQS * n_pages
    kv_refs = rest[u * n_pages:(u + 1) * n_pages]
    idx_refs = rest[all_pages + u * n_pages:all_pages + (u + 1) * n_pages]
    o_ref, wout_ref, isc_all_ref = rest[2 * all_pages:]
    isc_ref = isc_all_ref.at[u]
    past = n_pages * PAGE
    nb = past // BLK + 1
    row8 = lax.broadcasted_iota(i32, (8, 1), 0)
    lane = lax.broadcasted_iota(i32, (1, LANES), 1)
    slope8 = jnp.zeros((8, 1), f32)
    for h in range(NSA_HEADS):
        slope8 = jnp.where(row8 == h, ALIBI[h], slope8)
    low_group = row8 < NSA_HEADS // KV_GROUPS

    def pick_half(x):
        return jnp.where(low_group, x[:, 0:HEAD_DIM], x[:, HEAD_DIM:2 * HEAD_DIM])

    qn8, qd8 = qn_ref[u], qd_ref[u]
    qi8, wi8 = qi_ref[u], wi_ref[u]
    newb = newb_ref[u]
    gates = jax.nn.sigmoid(gate_ref[u])

    def new_score(q8, k_row):
        return jnp.sum(q8.astype(f32) * k_row.astype(f32), axis=-1, keepdims=True) + slope8 * float(past)

    def feat(p, slot):
        return kv_refs[p][0, slot * LANES:(slot + 1) * LANES, :]

    s_sel, s_dsa = [], []
    key_row = lax.broadcasted_iota(i32, (PAGE, nbp), 0)
    blk_col = lax.broadcasted_iota(i32, (PAGE, nbp), 1)
    kct = jnp.zeros((LANES, nbp), f32)
    vct = jnp.zeros((LANES, nbp), f32)
    for p in range(n_pages):
        pool = jnp.where(blk_col == 2 * p + key_row // BLK, 1.0, 0.0).astype(bf16)
        kct = kct + jnp.dot((feat(p, 0) * wct_ref[0]).astype(bf16), pool, preferred_element_type=f32)
        vct = vct + jnp.dot((feat(p, 1) * wct_ref[1]).astype(bf16), pool, preferred_element_type=f32)
        pos = (p * PAGE + lane).astype(f32)
        s_sel.append(jnp.dot(qn8, feat(p, 2).astype(bf16), preferred_element_type=f32) + slope8 * pos)
        s_dsa.append(jnp.dot(qd8, feat(p, 4).astype(bf16), preferred_element_type=f32) + slope8 * pos)
        ii = jnp.dot(qi8, idx_refs[p][0].astype(bf16), preferred_element_type=f32) * (IDX_DIM ** -0.5)
        isc_ref[p:p + 1, :] = jnp.sum(jnp.maximum(ii, 0.0) * wi8, axis=0, keepdims=True) * (IDX_HEADS ** -0.5)
    first_row = row8 == 0
    place = jnp.where(first_row & (lax.broadcasted_iota(i32, (8, nbp), 1) == nb - 1), 1.0, 0.0)
    tn = (((0,), (0,)), ((), ()))
    new_k = jnp.where(first_row, newb[:, 0:LANES].astype(f32) * wc_ref[0, 0:1, :], 0.0)
    new_v = jnp.where(first_row, newb[:, LANES:2 * LANES].astype(f32) * wc_ref[1, 0:1, :], 0.0)
    kct = kct + lax.dot_general(new_k.astype(bf16), place.astype(bf16), tn, preferred_element_type=f32)
    vct = vct + lax.dot_general(new_v.astype(bf16), place.astype(bf16), tn, preferred_element_type=f32)
    ii_new = jnp.sum(qi8.astype(f32) * newki_ref[u].astype(bf16).astype(f32), axis=-1, keepdims=True) * (IDX_DIM ** -0.5)
    isc_new = jnp.sum(jnp.maximum(ii_new, 0.0) * wi8, axis=0, keepdims=True) * (IDX_HEADS ** -0.5)
    isc_ref[n_pages:n_pages + 1, :] = jnp.where(lane == 0, isc_new, -jnp.inf)
    isc_ref[n_pages + 1:ISC_ROWS, :] = jnp.full((ISC_ROWS - n_pages - 1, LANES), -jnp.inf, f32)

    jb = lax.broadcasted_iota(i32, (1, nbp), 1)
    dist_c = past - (jb * BLK + (BLK - 1))
    mask_c = (dist_c >= 0) & (jb < nb)
    s_c = jnp.dot(qn8, kct.astype(bf16), preferred_element_type=f32) - slope8 * dist_c.astype(f32)
    s_c = jnp.where(mask_c, s_c, NEG_BIG)
    m_c = jnp.max(s_c, axis=-1, keepdims=True)
    e_c = jnp.where(mask_c, jnp.exp(s_c - m_c), 0.0)
    p_c = e_c * (1.0 / jnp.maximum(jnp.sum(e_c, axis=-1, keepdims=True), 1e-30))
    o_c = pick_half(_dot_nt(p_c.astype(bf16), vct.astype(bf16)))
    imp_lo = jnp.sum(jnp.where(low_group, p_c, 0.0), axis=0, keepdims=True)
    imp_hi = jnp.sum(jnp.where(low_group, 0.0, p_c), axis=0, keepdims=True)
    imp = jnp.where(low_group, imp_lo, imp_hi)
    cur = past // BLK
    imp = jnp.where((jb == 0) | (jb == cur) | (jb == cur - 1), FORCE_SCORE, imp)
    imp = jnp.where(jb <= cur, imp, -jnp.inf)
    rank = jnp.zeros((8, nbp), i32)
    for j in range(nb):
        other = imp[:, j:j + 1]
        rank = rank + ((other > imp) | ((other == imp) & (j < jb))).astype(i32)
    sel = jnp.where((rank < min(NSA_TOPN, nb)) & (imp > -jnp.inf), 1.0, 0.0)

    tot = isc_ref[...]
    tot = jnp.where(tot == 0.0, 0.0, tot)
    bits = lax.bitcast_convert_type(tot, i32)
    key = jnp.where(bits < 0, bits ^ jnp.int32(0x7FFFFFFF), bits)
    krow = lax.broadcasted_iota(i32, (ISC_ROWS, LANES), 0)
    kidx = krow * LANES + lax.broadcasted_iota(i32, (ISC_ROWS, LANES), 1)
    key = jnp.where(kidx <= past, key, INT_MIN)

    def total(x):
        folded = jnp.sum(x.reshape(ISC_ROWS // 8, 8, LANES), axis=0)
        return jnp.sum(jnp.sum(folded, axis=1, keepdims=True), axis=0, keepdims=True)

    def count_ge(trial):
        return total((key >= trial).astype(i32))

    c_pos = count_ge(jnp.zeros((1, 1), i32))
    v_thr = jnp.where(c_pos >= topk, 0, INT_MIN).astype(i32)
    for shift, width in ((27, 4), (23, 4), (19, 4), (15, 4), (11, 4), (7, 4), (3, 4), (0, 3)):
        digit = jnp.zeros((1, 1), i32)
        for j in range(1, 1 << width):
            ok = count_ge(v_thr | jnp.int32(j << shift)) >= topk
            digit = digit + ok.astype(i32)
        v_thr = v_thr | lax.shift_left(digit, jnp.int32(shift))
    v_eff = jnp.maximum(v_thr, INT_MIN + 1)

    def tie_mask():
        need = topk - count_ge(v_thr + 1)
        tie = key == v_thr
        w_last = jnp.zeros((1, 1), i32)
        for shift in (8, 4, 0):
            digit = jnp.zeros((1, 1), i32)
            for j in range(1, 16):
                below = total((tie & (kidx < (w_last | jnp.int32(j << shift)))).astype(i32))
                digit = digit + (below < need).astype(i32)
            w_last = w_last | lax.shift_left(digit, jnp.int32(shift))
        take = (key > v_thr) | (tie & (kidx <= w_last))
        return jnp.where(take & (key >= v_eff), 1.0, 0.0)

    def plain_mask():
        return jnp.where(key >= v_eff, 1.0, 0.0)

    has_surplus_tie = count_ge(v_thr)[0, 0] > topk
    yield
    dsel = lax.cond(has_surplus_tie, tie_mask, plain_mask)
    yield

    def attend(tiles, masks, s_new, new_ok, v_tile, v_new):
        tiles = [jnp.where(mk, t, NEG_BIG) for t, mk in zip(tiles, masks)]
        s_new = jnp.where(new_ok, s_new, NEG_BIG)
        m = s_new
        for t in tiles:
            m = jnp.maximum(m, jnp.max(t, axis=-1, keepdims=True))
        l = jnp.exp(s_new - m)
        acc = l * v_new.astype(f32)
        for j, t in enumerate(tiles):
            e = jnp.exp(t - m)
            l = l + jnp.sum(e, axis=-1, keepdims=True)
            acc = acc + _dot_nt(e.astype(bf16), v_tile(j))
        return pick_half(acc * (1.0 / l))

    sel_masks = [jnp.where(lane < BLK, sel[:, 2 * p:2 * p + 1], sel[:, 2 * p + 1:2 * p + 2]) > 0.0
                 for p in range(n_pages)]
    o_s = attend(s_sel, sel_masks, new_score(qn8, newb[:, 2 * LANES:3 * LANES]), sel[:, nb - 1:nb] > 0.0,
                 lambda p: feat(p, 3).astype(bf16), newb[:, 3 * LANES:4 * LANES])
    dsa_masks = [dsel[p:p + 1, :] > 0.0 for p in range(n_pages)]
    o_d = attend(s_dsa, dsa_masks, new_score(qd8, newb[:, 4 * LANES:5 * LANES]), dsel[n_pages:n_pages + 1, 0:1] > 0.0,
                 lambda p: feat(p, 5).astype(bf16), newb[:, 5 * LANES:6 * LANES])
    n_win = win_ref.shape[2]
    wlane = lax.broadcasted_iota(i32, (1, n_win), 1)
    wpos = past - n_win + wlane
    s_w = jnp.dot(qn8, win_ref[u, 0:LANES, :].astype(bf16), preferred_element_type=f32) + slope8 * wpos.astype(f32)
    w_ok = (past - wpos < WINDOW) & (wpos >= 0)
    o_w = attend([s_w], [w_ok], new_score(qn8, newb[:, 6 * LANES:7 * LANES]), True,
                 lambda p: win_ref[u, LANES:2 * LANES, :].astype(bf16), newb[:, 7 * LANES:8 * LANES])
    o_ref[u, 0:NSA_HEADS, :] = gates[:, 0:1] * o_c + gates[:, 1:2] * o_s + gates[:, 2:3] * o_w
    o_ref[u, NSA_HEADS:2 * NSA_HEADS, :] = o_d
    last = jnp.where(first_row & (lax.broadcasted_iota(i32, (8, n_win), 1) == n_win - 1), 1.0, 0.0)
    new_w = jnp.where(first_row, neww_ref[u], 0.0)
    w_hi = new_w.astype(bf16).astype(f32)
    w_mid = (new_w - w_hi).astype(bf16).astype(f32)
    w_lo = new_w - w_hi - w_mid
    pieces = jnp.concatenate([w_hi, w_mid, w_lo, jnp.zeros_like(new_w)], axis=0).astype(bf16)
    placed = lax.dot_general(pieces, jnp.concatenate([last] * 4, axis=0).astype(bf16), tn,
                             preferred_element_type=f32)
    wout_ref[u] = jnp.where(wlane == n_win - 1, placed, pltpu.roll(win_ref[u], n_win - 1, 1))


def _odd_attn_sample(zq, misc, kvb, win_new, cache_kv, cache_idx, cache_win, page_table, w_cmp):
    f32, bf16 = jnp.float32, jnp.bfloat16
    s_n, n_pages = page_table.shape
    n_pool = cache_kv.shape[0]
    n_win = cache_win.shape[1]
    past = n_pages * PAGE
    topk = min(DSA_TOPK_MAX, (past + 1) // 4)
    nbp = _round_up(past // BLK + 1, 8)
    wc2 = jnp.concatenate([w_cmp, w_cmp], axis=-1)
    wct = jnp.tile(jnp.swapaxes(w_cmp, 1, 2), (1, KV_GROUPS, PAGE // BLK))
    qn = zq[:, O_Q:O_QD].reshape(s_n, NSA_HEADS, LANES)
    qd = zq[:, O_QD:ZQ_QI].reshape(s_n, DSA_HEADS, LANES)
    qi = jnp.pad(zq[:, ZQ_QI:ZQ_QI + LANES].reshape(s_n, IDX_HEADS, IDX_DIM), ((0, 0), (0, 8 - IDX_HEADS), (0, 0)))
    wi = jnp.pad(misc[:, MISC_WI:MISC_WI + IDX_HEADS].reshape(s_n, IDX_HEADS, 1), ((0, 0), (0, 8 - IDX_HEADS), (0, 0)))
    gts = misc[:, MISC_GATE:MISC_GATE + 3 * NSA_HEADS].reshape(s_n, NSA_HEADS, 3)
    kv_pages = jnp.transpose(cache_kv, (0, 2, 3, 4, 1)).reshape(n_pool, KV_SLOTS * LANES, PAGE)
    idx_pages = jnp.transpose(cache_idx, (0, 2, 1))
    win_t = jnp.transpose(cache_win, (0, 2, 3, 4, 1)).reshape(s_n, 2 * LANES, n_win)
    n_u = SAMPLE_SEQS
    per_seq = lambda shape: pl.BlockSpec((n_u,) + shape, lambda b, pt: (b, 0, 0))
    page_spec = lambda rows, u, p: pl.BlockSpec((1, rows, PAGE), lambda b, pt, u=u, p=p: (pt[b * n_u + u, p], 0, 0))
    seq_pages = [(u, p) for u in range(n_u) for p in range(n_pages)]
    in_specs = ([per_seq((NSA_HEADS, LANES)), per_seq((DSA_HEADS, LANES)), per_seq((8, IDX_DIM)),
                 per_seq((8, 1)), per_seq((NSA_HEADS, 3)), per_seq((1, 8 * LANES)), per_seq((1, IDX_DIM)),
                 per_seq((1, 2 * LANES)), per_seq((2 * LANES, n_win)),
                 pl.BlockSpec((2, BLK, LANES), lambda b, pt: (0, 0, 0)),
                 pl.BlockSpec((2, LANES, PAGE), lambda b, pt: (0, 0, 0))]
                + [page_spec(KV_SLOTS * LANES, u, p) for u, p in seq_pages]
                + [page_spec(IDX_DIM, u, p) for u, p in seq_pages])
    kern = functools.partial(_odd_sample_kernel, n_pages=n_pages, topk=topk, nbp=nbp)
    o, win_out = pl.pallas_call(
        kern,
        grid_spec=pltpu.PrefetchScalarGridSpec(
            num_scalar_prefetch=1,
            grid=(s_n // n_u,),
            in_specs=in_specs,
            out_specs=[per_seq((2 * NSA_HEADS, HEAD_DIM)), per_seq((2 * LANES, n_win))],
            scratch_shapes=[pltpu.VMEM((n_u, ISC_ROWS, LANES), f32)]),
        out_shape=[jax.ShapeDtypeStruct((s_n, 2 * NSA_HEADS, HEAD_DIM), f32),
                   jax.ShapeDtypeStruct((s_n, 2 * LANES, n_win), f32)],
        compiler_params=pltpu.CompilerParams(
            dimension_semantics=("arbitrary",), vmem_limit_bytes=VMEM_LIMIT),
        name="odd_attn_sample",
    )(page_table, qn, qd, qi, wi, gts, kvb.reshape(s_n, 1, 8 * LANES), misc[:, 0:IDX_DIM].reshape(s_n, 1, IDX_DIM),
      win_new.reshape(s_n, 1, 2 * LANES), win_t, wc2, wct,
      *([kv_pages] * len(seq_pages)), *([idx_pages] * len(seq_pages)))
    win_out = jnp.transpose(win_out.reshape(s_n, 2, KV_GROUPS, HEAD_DIM, n_win), (0, 4, 1, 2, 3))
    return o.reshape(s_n, 2 * HALF_W), win_out


def _split(z, sizes):
    cuts = [int(c) for c in np.cumsum(sizes)[:-1]]
    return jnp.split(z, cuts, axis=-1)


def _pad_cols(w, n):
    return jnp.pad(w, ((0, 0), (0, n - w.shape[1])))


def kernel(x_prompt, x_sample, state_conv, state_C, state_n, state_m, cache_kv, cache_idx, cache_win, page_table,
           w_in_e, gate_b_e, conv_w, conv_b, w_out_e, w_in_o, w_cmp, w_out_o, w_up, w_down, ln_g, ln_b):
    f32, bf16 = jnp.float32, jnp.bfloat16
    bp, tp, d = x_prompt.shape
    dbs = x_sample.shape[0]
    keep = cache_win.shape[2]
    yp = x_prompt.reshape(bp * tp, d)
    ys = x_sample.reshape(dbs, d)
    tm_p, tm_s = 512, dbs
    outs = {}
    for layer in range(DEPTH):
        i = layer // 2
        if layer % 2 == 0:
            w_in = _pad_cols(w_in_e[i], E_PACKED).astype(bf16)
            w_out = w_out_e[i].astype(bf16)
            ya, qkv, og, gts, tails = _even_proj(yp, w_in, conv_w[i], conv_b[i], tm_p, tp)
            yb, c_fin, n_fin, m_fin = _mlstm_prompt(qkv, og, gts, gate_b_e[i], bp, tp)
            cvp = tails.reshape(bp, tp // tm_p, CONV_TAIL, HALF_W)[:, -1, CONV_TAIL - (CONV_W - 1):]
            ya_s, yb_s, cvs, c_s, n_s, m_s = _even_sample(_proj(ys, w_in, tm_s), gate_b_e[i], conv_w[i], conv_b[i],
                                                          state_conv[i], state_C[i], state_n[i], state_m[i])
            outs['conv'] = (cvp[None], cvs[None])
            outs['c'] = (c_fin[None], c_s[None])
            outs['n'] = (n_fin[None], n_s[None])
            outs['m'] = (m_fin[:, :, 0][None], m_s[:, :, 0][None])
            yp = _outproj2_ln(ya, yb.reshape(bp * tp, HALF_W), w_out, yp, ln_g[layer, 0], ln_b[layer, 0], tm_p)
            ys = _outproj2_ln(ya_s, yb_s, w_out, ys, ln_g[layer, 0], ln_b[layer, 0], tm_s)
        else:
            w_out = w_out_o[i].astype(bf16)
            w_packed = _pack_w_in_o(w_in_o[i]).astype(bf16)
            wc2 = jnp.concatenate([w_cmp[i], w_cmp[i]], axis=-1)
            rows_first = lambda kvt: jnp.transpose(
                kvt.reshape(kvt.shape[0], KV_SLOTS, KV_GROUPS, HEAD_DIM, kvt.shape[2]), (0, 4, 1, 2, 3))
            zq, kvt_p, kvb, winp, miscp = _odd_proj(yp, w_packed, tm_p, tp)
            mp = _odd_attn_prompt(zq, miscp, kvb, wc2, bp, tp)
            kvp = rows_first(kvt_p)
            ixp = miscp[:, :IDX_DIM].reshape(bp, tp, IDX_DIM)
            wnp = winp.reshape(bp, tp, 2, KV_GROUPS, HEAD_DIM)[:, tp - keep:]
            zq_s, kvt_s, kvb_s, win_s, misc_s = _odd_proj(ys, w_packed, tm_s, dbs)
            ms, wns = _odd_attn_sample(zq_s, misc_s, kvb_s, win_s, cache_kv[i], cache_idx[i], cache_win[i],
                                       page_table, w_cmp[i])
            kvs = rows_first(kvt_s).reshape(dbs, 1, KV_SLOTS, KV_GROUPS, HEAD_DIM)
            ixs = misc_s[:, :IDX_DIM].reshape(dbs, 1, IDX_DIM)
            outs['kv'] = (kvp[None], kvs[None])
            outs['idx'] = (ixp[None], ixs[None])
            outs['win'] = (wnp[None], wns[None])
            yp = _outproj_ln(mp, w_out, yp, ln_g[layer, 0], ln_b[layer, 0], tm_p)
            ys = _outproj_ln(ms, w_out, ys, ln_g[layer, 0], ln_b[layer, 0], tm_s)
        wu, wd = w_up[layer].astype(bf16), w_down[layer].astype(bf16)
        yp = _mlp_ln(yp, wu, wd, ln_g[layer, 1], ln_b[layer, 1], 2 * tm_p, 1024)
        ys = _mlp_ln(ys, wu, wd, ln_g[layer, 1], ln_b[layer, 1], tm_s, 1024)
    return (yp.reshape(bp, tp, d), ys.reshape(dbs, 1, d),
            outs['conv'][0], outs['conv'][1], outs['c'][0], outs['c'][1],
            outs['n'][0], outs['n'][1], outs['m'][0], outs['m'][1],
            outs['kv'][0], outs['kv'][1], outs['idx'][0], outs['idx'][1],
            outs['win'][0], outs['win'][1])
```

```python
import functools

import jax
import jax.numpy as jnp
import numpy as np
from jax import lax
from jax.experimental import pallas as pl
from jax.experimental.pallas import tpu as pltpu

D_MODEL = 1024
DEPTH = 2
HALF_W = 512
D_FF = 4096
CONV_W = 3
ML_HEADS = 4
ML_DIM = 128
HEAD_DIM = 64
KV_GROUPS = 2
NSA_HEADS = 8
DSA_HEADS = 8
BLK = 64
NSA_TOPN = 8
WINDOW = 256
IDX_HEADS = 4
IDX_DIM = 32
DSA_TOPK_MAX = 256
KV_SLOTS = 6
FORCE_SCORE = 1e4
ALPHA = (2.0 * DEPTH) ** 0.25
LN_EPS = 1e-5
O_SIZES = (512, 256, 256, 256, 24, 512, 256, 128, 32, 4)
LANES = 128
VMEM_LIMIT = 48 * 1024 * 1024


def _round_up(n, m):
    return -(-n // m) * m


def _proj_kernel(x_ref, w_ref, o_ref):
    o_ref[...] = jnp.dot(x_ref[...].astype(jnp.bfloat16), w_ref[...],
                         preferred_element_type=jnp.float32)


def _proj(x, w_bf16, tm):
    m, k = x.shape
    n = w_bf16.shape[1]
    tn = n
    for cand in (1024, 768, 512, 256, 128):
        if n % cand == 0:
            tn = cand
            break
    return pl.pallas_call(
        _proj_kernel,
        grid=(n // tn, m // tm),
        in_specs=[pl.BlockSpec((tm, k), lambda j, i: (i, 0)),
                  pl.BlockSpec((k, tn), lambda j, i: (0, j))],
        out_specs=pl.BlockSpec((tm, tn), lambda j, i: (i, j)),
        out_shape=jax.ShapeDtypeStruct((m, n), jnp.float32),
        compiler_params=pltpu.CompilerParams(
            dimension_semantics=("arbitrary", "arbitrary"), vmem_limit_bytes=VMEM_LIMIT),
        name="proj",
    )(x, w_bf16)


def _layer_norm_rows(v, g, b):
    mu = jnp.mean(v, axis=-1, keepdims=True)
    d = v - mu
    var = jnp.mean(d * d, axis=-1, keepdims=True)
    return d * lax.rsqrt(var + LN_EPS) * g + b


def _outproj_ln_kernel(y_ref, w_ref, x_ref, g_ref, b_ref, o_ref):
    mix = jnp.dot(y_ref[...].astype(jnp.bfloat16), w_ref[...], preferred_element_type=jnp.float32)
    o_ref[...] = _layer_norm_rows(ALPHA * x_ref[...] + mix, g_ref[...], b_ref[...])


def _outproj_ln(y, w_bf16, x, g, b, tm):
    m, d = x.shape
    k = y.shape[1]
    return pl.pallas_call(
        _outproj_ln_kernel,
        grid=(m // tm,),
        in_specs=[pl.BlockSpec((tm, k), lambda i: (i, 0)),
                  pl.BlockSpec((k, d), lambda i: (0, 0)),
                  pl.BlockSpec((tm, d), lambda i: (i, 0)),
                  pl.BlockSpec((1, d), lambda i: (0, 0)),
                  pl.BlockSpec((1, d), lambda i: (0, 0))],
        out_specs=pl.BlockSpec((tm, d), lambda i: (i, 0)),
        out_shape=jax.ShapeDtypeStruct((m, d), jnp.float32),
        compiler_params=pltpu.CompilerParams(
            dimension_semantics=("arbitrary",), vmem_limit_bytes=VMEM_LIMIT),
        name="outproj_ln",
    )(y, w_bf16, x, g.reshape(1, d), b.reshape(1, d))


def _mlp_ln_kernel(x_ref, wu_ref, wd_ref, g_ref, b_ref, o_ref, acc_ref):
    f = pl.program_id(1)

    @pl.when(f == 0)
    def _():
        acc_ref[...] = jnp.zeros_like(acc_ref)

    up = jnp.dot(x_ref[...].astype(jnp.bfloat16), wu_ref[...], preferred_element_type=jnp.float32)
    act = jnp.square(jnp.maximum(up, 0.0))
    acc_ref[...] += jnp.dot(act.astype(jnp.bfloat16), wd_ref[...], preferred_element_type=jnp.float32)

    @pl.when(f == pl.num_programs(1) - 1)
    def _():
        o_ref[...] = _layer_norm_rows(ALPHA * x_ref[...] + acc_ref[...], g_ref[...], b_ref[...])


def _mlp_ln(x, wu_bf16, wd_bf16, g, b, tm, tf):
    m, d = x.shape
    ff = wu_bf16.shape[1]
    return pl.pallas_call(
        _mlp_ln_kernel,
        grid=(m // tm, ff // tf),
        in_specs=[pl.BlockSpec((tm, d), lambda i, f: (i, 0)),
                  pl.BlockSpec((d, tf), lambda i, f: (0, f)),
                  pl.BlockSpec((tf, d), lambda i, f: (f, 0)),
                  pl.BlockSpec((1, d), lambda i, f: (0, 0)),
                  pl.BlockSpec((1, d), lambda i, f: (0, 0))],
        out_specs=pl.BlockSpec((tm, d), lambda i, f: (i, 0)),
        out_shape=jax.ShapeDtypeStruct((m, d), jnp.float32),
        scratch_shapes=[pltpu.VMEM((tm, d), jnp.float32)],
        compiler_params=pltpu.CompilerParams(
            dimension_semantics=("arbitrary", "arbitrary"), vmem_limit_bytes=VMEM_LIMIT),
        name="mlp_ln",
    )(x, wu_bf16, wd_bf16, g.reshape(1, d), b.reshape(1, d))


E_QKV = 3 * HALF_W
E_OG = 6 * HALF_W
E_GATE = 7 * HALF_W
E_PACKED = 7 * HALF_W + LANES
CONV_TAIL = 8


def _even_proj_kernel(x_ref, w_ref, cw_ref, cb_ref, ya_ref, qkv_ref, og_ref, gate_ref, tail_ref, carry_ref,
                      *, tiles_per_seq):
    f32, bf16 = jnp.float32, jnp.bfloat16
    tm = x_ref.shape[0]

    @pl.when(pl.program_id(0) % tiles_per_seq == 0)
    def _():
        carry_ref[...] = jnp.zeros_like(carry_ref)

    z = jnp.dot(x_ref[...].astype(bf16), w_ref[...], preferred_element_type=f32)
    u = z[:, 2 * HALF_W:3 * HALF_W] * z[:, 0:HALF_W]
    prev = carry_ref[...]
    row = lax.broadcasted_iota(jnp.int32, (tm, 1), 0)
    conv = cb_ref[...] + cw_ref[CONV_W - 1:CONV_W, :] * u
    for back in range(1, CONV_W):
        shifted = pltpu.roll(u, back, 0)
        for r in range(back):
            shifted = jnp.where(row == r, prev[CONV_TAIL - back + r:CONV_TAIL - back + r + 1, :], shifted)
        conv = conv + cw_ref[CONV_W - 1 - back:CONV_W - back, :] * shifted
    ya_ref[...] = (z[:, HALF_W:2 * HALF_W] * conv).astype(bf16)
    tail = u[tm - CONV_TAIL:]
    carry_ref[...] = tail
    tail_ref[0] = tail
    qkv_ref[:, 0:HALF_W] = z[:, E_QKV:E_QKV + HALF_W].astype(bf16)
    qkv_ref[:, HALF_W:2 * HALF_W] = (z[:, E_QKV + HALF_W:E_QKV + 2 * HALF_W] * (ML_DIM ** -0.5)).astype(bf16)
    qkv_ref[:, 2 * HALF_W:3 * HALF_W] = z[:, E_QKV + 2 * HALF_W:E_OG].astype(bf16)
    og_ref[...] = z[:, E_OG:E_GATE]
    gate_ref[...] = z[:, E_GATE:E_PACKED]


def _even_proj(x, w_bf16, conv_w, conv_b, tm, seq_len):
    m, k = x.shape
    widths = (HALF_W, 3 * HALF_W, HALF_W, LANES)
    dtypes = (jnp.bfloat16, jnp.bfloat16, jnp.float32, jnp.float32)
    kern = functools.partial(_even_proj_kernel, tiles_per_seq=seq_len // tm)
    return pl.pallas_call(
        kern,
        grid=(m // tm,),
        in_specs=[pl.BlockSpec((tm, k), lambda i: (i, 0)),
                  pl.BlockSpec((k, E_PACKED), lambda i: (0, 0)),
                  pl.BlockSpec((CONV_W, HALF_W), lambda i: (0, 0)),
                  pl.BlockSpec((1, HALF_W), lambda i: (0, 0))],
        out_specs=[pl.BlockSpec((tm, n), lambda i: (i, 0)) for n in widths]
        + [pl.BlockSpec((1, CONV_TAIL, HALF_W), lambda i: (i, 0, 0))],
        out_shape=[jax.ShapeDtypeStruct((m, n), dt) for n, dt in zip(widths, dtypes)]
        + [jax.ShapeDtypeStruct((m // tm, CONV_TAIL, HALF_W), jnp.float32)],
        scratch_shapes=[pltpu.VMEM((CONV_TAIL, HALF_W), jnp.float32)],
        compiler_params=pltpu.CompilerParams(
            dimension_semantics=("arbitrary",), vmem_limit_bytes=VMEM_LIMIT),
        name="even_proj",
    )(x, w_bf16, conv_w, conv_b.reshape(1, HALF_W))


def _mlstm_kernel(qkv_ref, og_ref, gate_ref, gb_ref, yb_ref, c_ref, n_ref, m_ref):
    f32, bf16 = jnp.float32, jnp.bfloat16
    L = qkv_ref.shape[1]

    @pl.when(pl.program_id(1) == 0)
    def _():
        c_ref[...] = jnp.zeros_like(c_ref)
        n_ref[...] = jnp.zeros_like(n_ref)
        m_ref[...] = jnp.zeros_like(m_ref)

    pre = gate_ref[0] + gb_ref[...]
    lf = jax.nn.log_sigmoid(pre)
    row = lax.broadcasted_iota(jnp.int32, (L, L), 0)
    col = lax.broadcasted_iota(jnp.int32, (L, L), 1)
    causal = col <= row
    tri = jnp.where(causal, 1.0, 0.0).astype(f32)
    b_all = jnp.dot(tri, lf, preferred_element_type=f32, precision=lax.Precision.HIGHEST)
    pre_t = pre.T
    b_t = b_all.T
    for h in range(ML_HEADS):
        q = qkv_ref[0, :, h * ML_DIM:(h + 1) * ML_DIM]
        k = qkv_ref[0, :, HALF_W + h * ML_DIM:HALF_W + (h + 1) * ML_DIM]
        v = qkv_ref[0, :, 2 * HALF_W + h * ML_DIM:2 * HALF_W + (h + 1) * ML_DIM]
        ig_col = pre[:, h:h + 1]
        b_col = b_all[:, ML_HEADS + h:ML_HEADS + h + 1]
        a_row = pre_t[h:h + 1, :] - b_t[ML_HEADS + h:ML_HEADS + h + 1, :]
        m_st = m_ref[0, h:h + 1, 0:1]
        c_st = c_ref[0, h]
        n_st = n_ref[0, h:h + 1, :]
        dmat = jnp.where(causal, b_col + a_row, -jnp.inf)
        inter = b_col + m_st
        m_t = jnp.maximum(inter, jnp.max(dmat, axis=-1, keepdims=True))
        w_intra = jnp.exp(dmat - m_t)
        w_inter = jnp.exp(inter - m_t)
        s = _dot_nt(q, k) * w_intra
        num = jnp.dot(s.astype(bf16), v, preferred_element_type=f32) + w_inter * _dot_nt(q, c_st.astype(bf16))
        den = jnp.sum(s, axis=-1, keepdims=True) + w_inter * jnp.sum(q.astype(f32) * n_st, axis=-1, keepdims=True)
        hs = num * (1.0 / jnp.maximum(jnp.abs(den), jnp.exp(-m_t)))
        m_new = m_t[L - 1:L, :]
        b_last = b_col[L - 1:L, :]
        w_state = jnp.exp(b_last - b_col + ig_col - m_new)
        decay = jnp.exp(b_last + m_st - m_new)
        vw = (v.astype(f32) * w_state).astype(bf16)
        c_ref[0, h] = decay * c_st + lax.dot_general(vw, k, (((0,), (0,)), ((), ())), preferred_element_type=f32)
        n_ref[0, h:h + 1, :] = decay * n_st + jnp.sum(k.astype(f32) * w_state, axis=0, keepdims=True)
        m_ref[0, h:h + 1, :] = jnp.broadcast_to(m_new, (1, LANES))
        og = og_ref[0, :, h * ML_DIM:(h + 1) * ML_DIM]
        yb_ref[0, :, h * ML_DIM:(h + 1) * ML_DIM] = (jax.nn.sigmoid(og) * hs).astype(bf16)


def _mlstm_prompt(qkv, og, gates, gate_b, bsz, t_len, chunk=128):
    gb = jnp.pad(gate_b, (0, LANES - gate_b.shape[0])).reshape(1, LANES)
    return pl.pallas_call(
        _mlstm_kernel,
        grid=(bsz, t_len // chunk),
        in_specs=[pl.BlockSpec((1, chunk, 3 * HALF_W), lambda b, c: (b, c, 0)),
                  pl.BlockSpec((1, chunk, HALF_W), lambda b, c: (b, c, 0)),
                  pl.BlockSpec((1, chunk, LANES), lambda b, c: (b, c, 0)),
                  pl.BlockSpec((1, LANES), lambda b, c: (0, 0))],
        out_specs=[pl.BlockSpec((1, chunk, HALF_W), lambda b, c: (b, c, 0)),
                   pl.BlockSpec((1, ML_HEADS, ML_DIM, ML_DIM), lambda b, c: (b, 0, 0, 0)),
                   pl.BlockSpec((1, ML_HEADS, ML_DIM), lambda b, c: (b, 0, 0)),
                   pl.BlockSpec((1, ML_HEADS, LANES), lambda b, c: (b, 0, 0))],
        out_shape=[jax.ShapeDtypeStruct((bsz, t_len, HALF_W), jnp.bfloat16),
                   jax.ShapeDtypeStruct((bsz, ML_HEADS, ML_DIM, ML_DIM), jnp.float32),
                   jax.ShapeDtypeStruct((bsz, ML_HEADS, ML_DIM), jnp.float32),
                   jax.ShapeDtypeStruct((bsz, ML_HEADS, LANES), jnp.float32)],
        compiler_params=pltpu.CompilerParams(
            dimension_semantics=("arbitrary", "arbitrary"), vmem_limit_bytes=VMEM_LIMIT),
        name="mlstm_prompt",
    )(qkv.reshape(bsz, t_len, 3 * HALF_W), og.reshape(bsz, t_len, HALF_W),
      gates.reshape(bsz, t_len, LANES), gb)


def _outproj2_ln_kernel(ya_ref, yb_ref, w_ref, x_ref, g_ref, b_ref, o_ref):
    mix = jnp.dot(ya_ref[...], w_ref[0:HALF_W, :], preferred_element_type=jnp.float32)
    mix = mix + jnp.dot(yb_ref[...], w_ref[HALF_W:, :], preferred_element_type=jnp.float32)
    o_ref[...] = _layer_norm_rows(ALPHA * x_ref[...] + mix, g_ref[...], b_ref[...])


def _outproj2_ln(ya, yb, w_bf16, x, g, b, tm):
    m, d = x.shape
    return pl.pallas_call(
        _outproj2_ln_kernel,
        grid=(m // tm,),
        in_specs=[pl.BlockSpec((tm, HALF_W), lambda i: (i, 0)),
                  pl.BlockSpec((tm, HALF_W), lambda i: (i, 0)),
                  pl.BlockSpec((2 * HALF_W, d), lambda i: (0, 0)),
                  pl.BlockSpec((tm, d), lambda i: (i, 0)),
                  pl.BlockSpec((1, d), lambda i: (0, 0)),
                  pl.BlockSpec((1, d), lambda i: (0, 0))],
        out_specs=pl.BlockSpec((tm, d), lambda i: (i, 0)),
        out_shape=jax.ShapeDtypeStruct((m, d), jnp.float32),
        compiler_params=pltpu.CompilerParams(
            dimension_semantics=("arbitrary",), vmem_limit_bytes=VMEM_LIMIT),
        name="outproj2_ln",
    )(ya, yb, w_bf16, x, g.reshape(1, d), b.reshape(1, d))


SAMPLE_ROWS = 8


def _even_sample_kernel(h_ref, bg_ref, cg_ref, q_ref, k_ref, v_ref, og_ref, gate_ref, gb_ref, cw_ref, cb_ref,
                        conv_ref, c_ref, n_ref, m_ref, ya_ref, yb_ref, conv_out_ref, c_out_ref, n_out_ref, m_out_ref):
    f32, bf16 = jnp.float32, jnp.bfloat16
    rows = h_ref.shape[0]
    row = lax.broadcasted_iota(jnp.int32, (rows, 1), 0)
    tn = (((0,), (0,)), ((), ()))
    u = cg_ref[...] * h_ref[...]
    prev0, prev1 = conv_ref[:, 0, :], conv_ref[:, 1, :]
    conv = cb_ref[...] + cw_ref[0:1, :] * prev0 + cw_ref[1:2, :] * prev1 + cw_ref[2:3, :] * u
    ya_ref[...] = (bg_ref[...] * conv).astype(bf16)
    conv_out_ref[:, 0, :] = prev1
    conv_out_ref[:, 1, :] = u
    pre = gate_ref[...] + gb_ref[...]
    lf_all = jax.nn.log_sigmoid(pre)
    for h in range(ML_HEADS):
        cols = slice(h * ML_DIM, (h + 1) * ML_DIM)
        q = q_ref[:, cols]
        k = k_ref[:, cols] * (ML_DIM ** -0.5)
        v = v_ref[:, cols]
        ig = pre[:, h:h + 1]
        lf = lf_all[:, ML_HEADS + h:ML_HEADS + h + 1]
        m_st = m_ref[:, h:h + 1]
        n_st = n_ref[:, h, :]
        inter = lf + m_st
        m_t = jnp.maximum(inter, ig)
        w_intra = jnp.exp(ig - m_t)
        w_inter = jnp.exp(inter - m_t)
        qb, kb = q.astype(bf16), k.astype(bf16)
        s = jnp.sum(qb.astype(f32) * kb.astype(f32), axis=-1, keepdims=True) * w_intra
        cq = jnp.zeros((rows, ML_DIM), f32)
        for b in range(rows):
            c_b = c_ref[b, h]
            cq = jnp.where(row == b, _dot_nt(qb, c_b.astype(bf16)), cq)
            vw = jnp.where(row == b, v * w_intra, 0.0)
            outer = lax.dot_general(vw, k, tn, preferred_element_type=f32, precision=lax.Precision.HIGHEST)
            c_out_ref[b, h] = w_inter[b:b + 1, :] * c_b + outer
        num = s * v + w_inter * cq
        den = s + w_inter * jnp.sum(n_st * q, axis=-1, keepdims=True)
        hs = num * (1.0 / jnp.maximum(jnp.abs(den), jnp.exp(-m_t)))
        n_out_ref[:, h, :] = w_inter * n_st + w_intra * k
        m_out_ref[:, h, :] = jnp.broadcast_to(m_t, (rows, LANES))
        yb_ref[:, cols] = (jax.nn.sigmoid(og_ref[:, cols]) * hs).astype(bf16)


def _even_sample(z, gate_b, conv_w, conv_b, state_conv, state_c, state_n, state_m):
    s_n = z.shape[0]
    f32 = jnp.float32
    r = SAMPLE_ROWS
    gb = jnp.pad(gate_b, (0, LANES - gate_b.shape[0])).reshape(1, LANES)
    col = lambda j: pl.BlockSpec((r, HALF_W), lambda i, j=j: (i, j))
    in_specs = [col(j) for j in range(7)] + [
        pl.BlockSpec((r, LANES), lambda i: (i, E_GATE // LANES)),
        pl.BlockSpec((1, LANES), lambda i: (0, 0)),
        pl.BlockSpec((CONV_W, HALF_W), lambda i: (0, 0)),
        pl.BlockSpec((1, HALF_W), lambda i: (0, 0)),
        pl.BlockSpec((r, CONV_W - 1, HALF_W), lambda i: (i, 0, 0)),
        pl.BlockSpec((r, ML_HEADS, ML_DIM, ML_DIM), lambda i: (i, 0, 0, 0)),
        pl.BlockSpec((r, ML_HEADS, ML_DIM), lambda i: (i, 0, 0)),
        pl.BlockSpec((r, ML_HEADS), lambda i: (i, 0))]
    out_specs = [pl.BlockSpec((r, HALF_W), lambda i: (i, 0)),
                 pl.BlockSpec((r, HALF_W), lambda i: (i, 0)),
                 pl.BlockSpec((r, CONV_W - 1, HALF_W), lambda i: (i, 0, 0)),
                 pl.BlockSpec((r, ML_HEADS, ML_DIM, ML_DIM), lambda i: (i, 0, 0, 0)),
                 pl.BlockSpec((r, ML_HEADS, ML_DIM), lambda i: (i, 0, 0)),
                 pl.BlockSpec((r, ML_HEADS, LANES), lambda i: (i, 0, 0))]
    out_shape = [jax.ShapeDtypeStruct((s_n, HALF_W), jnp.bfloat16),
                 jax.ShapeDtypeStruct((s_n, HALF_W), jnp.bfloat16),
                 jax.ShapeDtypeStruct((s_n, CONV_W - 1, HALF_W), f32),
                 jax.ShapeDtypeStruct((s_n, ML_HEADS, ML_DIM, ML_DIM), f32),
                 jax.ShapeDtypeStruct((s_n, ML_HEADS, ML_DIM), f32),
                 jax.ShapeDtypeStruct((s_n, ML_HEADS, LANES), f32)]
    return pl.pallas_call(
        _even_sample_kernel,
        grid=(s_n // r,),
        in_specs=in_specs,
        out_specs=out_specs,
        out_shape=out_shape,
        compiler_params=pltpu.CompilerParams(
            dimension_semantics=("arbitrary",), vmem_limit_bytes=VMEM_LIMIT),
        name="even_sample",
    )(z, z, z, z, z, z, z, z, gb, conv_w, conv_b.reshape(1, HALF_W), state_conv, state_c, state_n, state_m)


O_Q = 0
O_QD = NSA_HEADS * LANES
O_KV = 2048
O_WIN = 2816
O_QI = 3072
O_MISC = 3200
O_PACKED = 3328
ZQ_QI = 2048
MISC_WI = IDX_DIM
MISC_GATE = IDX_DIM + IDX_HEADS
NEG_BIG = -(2.0 ** 100)
INT_MIN = -(2 ** 31)
ALIBI = tuple(float(2.0 ** (-8.0 * (h + 1) / NSA_HEADS)) for h in range(NSA_HEADS))


def _pack_w_in_o(w):
    qn, kvc, kvs, kvw, gates, qd, kvd, qi, ki, wi = _split(w, O_SIZES)
    scale = HEAD_DIM ** -0.5
    rows = w.shape[0]

    def spread(q):
        q = (q * scale).reshape(rows, NSA_HEADS, HEAD_DIM)
        z = jnp.zeros_like(q)
        low = (jnp.arange(NSA_HEADS) < NSA_HEADS // KV_GROUPS)[None, :, None]
        return jnp.concatenate([jnp.where(low, q, z), jnp.where(low, z, q)], axis=-1).reshape(rows, NSA_HEADS * LANES)

    pad = jnp.zeros((rows, O_PACKED - O_MISC - IDX_DIM - IDX_HEADS - 3 * NSA_HEADS), w.dtype)
    return jnp.concatenate([spread(qn), spread(qd), kvc, kvs, kvd, kvw, qi, ki, wi, gates, pad], axis=1)


def _odd_proj_kernel(x_ref, w_ref, wkvt_ref, zq_ref, kvt_ref, kvb_ref, win_ref, misc_ref):
    xb = x_ref[...].astype(jnp.bfloat16)
    z = jnp.dot(xb, w_ref[...], preferred_element_type=jnp.float32)
    zq_ref[:, 0:O_KV] = z[:, 0:O_KV].astype(jnp.bfloat16)
    zq_ref[:, ZQ_QI:ZQ_QI + LANES] = z[:, O_QI:O_MISC].astype(jnp.bfloat16)
    kvt_ref[0] = _dot_nt(wkvt_ref[...], xb)
    kvb_ref[...] = z[:, O_KV:O_QI].astype(jnp.bfloat16)
    win_ref[...] = z[:, O_WIN:O_QI]
    misc_ref[...] = z[:, O_MISC:O_PACKED]


def _odd_proj(x, w_packed_bf16, tm, seq_len):
    m, k = x.shape
    n_kv = O_WIN - O_KV
    widths = (O_KV + LANES, O_QI - O_KV, O_QI - O_WIN, LANES)
    dtypes = (jnp.bfloat16, jnp.bfloat16, jnp.float32, jnp.float32)
    row_spec = lambda n: pl.BlockSpec((tm, n), lambda i: (i, 0))
    tiles = seq_len // tm
    out_specs = [row_spec(widths[0]), pl.BlockSpec((1, n_kv, tm), lambda i: (i // tiles, 0, i % tiles))]
    out_specs += [row_spec(n) for n in widths[1:]]
    out_shape = [jax.ShapeDtypeStruct((m, widths[0]), dtypes[0]),
                 jax.ShapeDtypeStruct((m // seq_len, n_kv, seq_len), jnp.float32)]
    out_shape += [jax.ShapeDtypeStruct((m, n), dt) for n, dt in zip(widths[1:], dtypes[1:])]
    return pl.pallas_call(
        _odd_proj_kernel,
        grid=(m // tm,),
        in_specs=[pl.BlockSpec((tm, k), lambda i: (i, 0)),
                  pl.BlockSpec((k, O_PACKED), lambda i: (0, 0)),
                  pl.BlockSpec((n_kv, k), lambda i: (0, 0))],
        out_specs=out_specs,
        out_shape=out_shape,
        compiler_params=pltpu.CompilerParams(
            dimension_semantics=("arbitrary",), vmem_limit_bytes=VMEM_LIMIT),
        name="odd_proj",
    )(x, w_packed_bf16, w_packed_bf16[:, O_KV:O_WIN].T)


def _dot_nt(a, b):
    return lax.dot_general(a, b, (((1,), (1,)), ((), ())), preferred_element_type=jnp.float32)


def _flash_groups(q_by_group, slopes_by_group, kv_ref, pos_ref, k_blk, v_blk, tk, lo, hi, bias_fns, skip_fn=None,
                  start_fn=None):
    f32, bf16 = jnp.float32, jnp.bfloat16
    if start_fn is None:
        start_fn = lambda kt: kt * tk
    n_groups = len(q_by_group)
    tq = q_by_group[0][0].shape[0]
    n_heads = len(q_by_group[0])
    lane = lax.broadcasted_iota(jnp.int32, (1, LANES), 1)
    own_half = [(lane // HEAD_DIM) == g for g in range(n_groups)]
    feat_lane = lax.broadcasted_iota(jnp.int32, (tq, LANES), 1)
    q_all = []
    for q_heads, slopes in zip(q_by_group, slopes_by_group):
        slope_feat = [jnp.where(feat_lane == 0, BLK * sl, jnp.where(feat_lane == 1, sl, 0.0)).astype(bf16)
                      for sl in slopes]
        q_all.append(jnp.concatenate([jnp.concatenate(q_heads, axis=0), jnp.concatenate(slope_feat, axis=0)], axis=1))
    tn = (((0,), (0,)), ((), ()))

    def tile(kt, carry):
        start = pl.multiple_of(start_fn(kt), LANES)
        rows = pl.ds(start, tk)
        k = kv_ref[0, rows, k_blk * LANES:(k_blk + 1) * LANES]
        v = kv_ref[0, rows, v_blk * LANES:(v_blk + 1) * LANES]
        k_pos = jnp.concatenate([k, pos_ref[rows, :]], axis=1)
        scores = [_dot_nt(k_pos, q) for q in q_all]
        biases = [fn(kt) for fn in bias_fns] if len(bias_fns) > 1 else [bias_fns[0](kt)] * n_groups
        new_m, probs, alphas = [], [], []
        for g in range(n_groups):
            m = carry[g][0]
            s = scores[g] + jnp.concatenate([biases[g]] * n_heads, axis=1)
            m_new = jnp.maximum(m, jnp.max(s, axis=0, keepdims=True))
            probs.append(jnp.exp(s - m_new).astype(bf16))
            alphas.append(jnp.exp(m - m_new))
            new_m.append(m_new)
        out = []
        for g in range(n_groups):
            v_g = jnp.where(own_half[g], v, jnp.ones_like(v))
            pv = lax.dot_general(v_g, probs[g], tn, preferred_element_type=f32)
            out.append((new_m[g], alphas[g] * carry[g][1] + pv))
        return tuple(out)

    def body(kt, carry):
        if skip_fn is None:
            return tile(kt, carry)
        return lax.cond(skip_fn(kt), lambda c: c, functools.partial(tile, kt), carry)

    init = tuple((jnp.full((1, n_heads * tq), -jnp.inf, f32), jnp.zeros((LANES, n_heads * tq), f32))
                 for _ in range(n_groups))
    final = lax.fori_loop(lo, hi, body, init)
    return [acc for _, acc in final]


def _odd_attn_kernel(zq_ref, miscq_ref, kvb_ref, misck_ref, wc_ref, o_ref,
                     kc_ref, vc_ref, ki_ref, key_ref, dbias_ref, flag_ref, pos_ref, *, t_len, tq, tk, topk):
    f32, bf16, i32 = jnp.float32, jnp.bfloat16, jnp.int32
    nb = t_len // BLK
    qi_blk = pl.program_id(1)
    q0 = qi_blk * tq

    @pl.when(qi_blk == 0)
    def _():
        ck = kvb_ref[0, :, 0:LANES].astype(f32).reshape(nb, BLK, LANES)
        kc_ref[...] = jnp.sum(ck * wc_ref[0][None], axis=1).astype(bf16)
        cv = kvb_ref[0, :, LANES:2 * LANES].astype(f32).reshape(nb, BLK, LANES)
        vc = jnp.sum(cv * wc_ref[1][None], axis=1)
        for g in range(KV_GROUPS):
            half = vc[:, g * HEAD_DIM:(g + 1) * HEAD_DIM]
            vc_ref[g] = jnp.concatenate([half, half], axis=1).astype(bf16)
        ki_ref[...] = misck_ref[0, :, 0:IDX_DIM].astype(bf16)
        key_idx = lax.broadcasted_iota(i32, (t_len, LANES), 0)
        feat = lax.broadcasted_iota(i32, (t_len, LANES), 1)
        pos_ref[...] = jnp.where(feat == 0, key_idx // BLK, jnp.where(feat == 1, key_idx % BLK, 0)).astype(bf16)

    misc = miscq_ref[0]
    gates = jax.nn.sigmoid(misc[:, MISC_GATE:MISC_GATE + 3 * NSA_HEADS])
    lane = lax.broadcasted_iota(i32, (1, LANES), 1)
    low_half = lane < HEAD_DIM
    heads_per_group = NSA_HEADS // KV_GROUPS

    def q_head(base, h):
        return zq_ref[0, :, base + h * LANES: base + (h + 1) * LANES]

    def pair(even, odd):
        return jnp.where(low_half, even, odd)

    def normalised_pair(acc_t, r_even, g):
        halves = []
        for r in (r_even, r_even + 1):
            cols = slice(r * tq, (r + 1) * tq)
            out = acc_t[g * HEAD_DIM:(g + 1) * HEAD_DIM, cols]
            den = acc_t[(1 - g) * HEAD_DIM:(1 - g) * HEAD_DIM + 1, cols]
            halves.append(out * (1.0 / den))
        return jnp.concatenate(halves, axis=0).T

    def gate_pair(j, c):
        a = gates[:, (2 * j) * 3 + c:(2 * j) * 3 + c + 1]
        b = gates[:, (2 * j + 1) * 3 + c:(2 * j + 1) * 3 + c + 1]
        return jnp.where(low_half, a, b)

    tn = (((0,), (0,)), ((), ()))
    t_lanes = q0 + lax.broadcasted_iota(i32, (1, tq), 1)
    jb = lax.broadcasted_iota(i32, (nb, 1), 0)
    dist_c = t_lanes - (jb * BLK + (BLK - 1))
    mask_c = dist_c >= 0
    dist_cf = dist_c.astype(f32)
    cur = t_lanes // BLK
    forced = (jb == 0) | (jb == cur) | (jb == cur - 1)
    admissible = jb <= cur
    jb_full = lax.broadcasted_iota(i32, (nb, tq), 0)
    o_cmp, sel_bias, sel_any = [], [], []
    for g in range(KV_GROUPS):
        imp = jnp.zeros((nb, tq), f32)
        for r in range(heads_per_group):
            h = g * heads_per_group + r
            s = _dot_nt(kc_ref[...], q_head(O_Q, h)) - ALIBI[h] * dist_cf
            s = jnp.where(mask_c, s, NEG_BIG)
            m = jnp.max(s, axis=0, keepdims=True)
            e = jnp.where(mask_c, jnp.exp(s - m), 0.0)
            p = e * (1.0 / jnp.maximum(jnp.sum(e, axis=0, keepdims=True), 1e-30))
            imp = imp + p
            o_cmp.append(lax.dot_general(p.astype(bf16), vc_ref[g], tn, preferred_element_type=f32))
        imp = jnp.where(forced, FORCE_SCORE, imp)
        imp = jnp.where(admissible, imp, -jnp.inf)
        sel = jnp.zeros((nb, tq), f32)
        for _ in range(min(NSA_TOPN, nb)):
            m = jnp.max(imp, axis=0, keepdims=True)
            first = jnp.min(jnp.where(imp == m, jb_full, nb), axis=0, keepdims=True)
            hit = jb_full == first
            sel = jnp.where(hit & (m > -jnp.inf), 1.0, sel)
            imp = jnp.where(hit, -jnp.inf, imp)
        sel_bias.append(jnp.where(sel > 0.0, 0.0, NEG_BIG).astype(bf16))
        sel_any.append(jnp.max(sel, axis=1, keepdims=True))

    n_sel_tiles = (q0 + tq + tk - 1) // tk
    blocks_per_tile = tk // BLK

    for g in range(KV_GROUPS):
        for j in range(nb // blocks_per_tile):
            hit = jnp.max(sel_any[g][j * blocks_per_tile:(j + 1) * blocks_per_tile, :])
            flag_ref[g, j] = (hit > 0.0).astype(i32)

    def causal_bias(start, width):
        sp = start + lax.broadcasted_iota(i32, (width, 1), 0)
        return jnp.where(sp <= t_lanes, 0.0, NEG_BIG)

    def sel_bias_fn(g):
        def fn(kt):
            row = lax.broadcasted_iota(i32, (nb, tk), 0)
            col = lax.broadcasted_iota(i32, (nb, tk), 1)
            expand = jnp.where(row == kt * blocks_per_tile + col // BLK, 1.0, 0.0).astype(bf16)
            spread = lax.dot_general(expand, sel_bias[g], tn, preferred_element_type=f32)
            return spread + causal_bias(kt * tk, tk)
        return fn

    tkw = min(WINDOW + tq, t_len)
    win_start = jnp.clip(q0 - WINDOW, 0, t_len - tkw)

    def win_bias_fn(kt):
        dist = t_lanes - (win_start + lax.broadcasted_iota(i32, (tkw, 1), 0))
        return jnp.where((dist >= 0) & (dist < WINDOW), 0.0, NEG_BIG)

    qi_heads = [zq_ref[0, :, ZQ_QI + h * IDX_DIM: ZQ_QI + (h + 1) * IDX_DIM] for h in range(IDX_HEADS)]
    pick = jnp.where(lax.broadcasted_iota(i32, (8, LANES), 1) == MISC_WI + lax.broadcasted_iota(i32, (8, LANES), 0),
                     1.0, 0.0)
    wi_rows = lax.dot_general(pick, misc, (((1,), (1,)), ((), ())), preferred_element_type=f32,
                              precision=lax.Precision.HIGHEST)
    t_row = q0 + lax.broadcasted_iota(i32, (1, tq), 1)

    def fold8(x):
        x = x.reshape(tk // 8, 8, tq)
        while x.shape[0] > 1:
            half = x.shape[0] // 2
            x = x[:half] + x[half:]
        return x[0]

    def index_tile(kt, carry):
        start = pl.multiple_of(kt * tk, tk)
        kik = ki_ref[pl.ds(start, tk), :]
        tot = jnp.zeros((tk, tq), f32)
        for h in range(IDX_HEADS):
            sc = _dot_nt(kik, qi_heads[h]) * (IDX_DIM ** -0.5)
            tot = tot + jnp.maximum(sc, 0.0) * wi_rows[h:h + 1, :]
        tot = tot * (IDX_HEADS ** -0.5)
        sp = start + lax.broadcasted_iota(i32, (tk, 1), 0)
        key_ref[kt] = jnp.where(sp <= t_row, tot, -jnp.inf)
        return carry

    lax.fori_loop(0, n_sel_tiles, index_tile, 0)

    def as_float(pattern):
        bits = jnp.where(pattern < 0, pattern ^ jnp.int32(0x7FFFFFFF), pattern)
        return lax.bitcast_convert_type(bits, f32)

    def count_ge(trial):
        trial_f = as_float(trial)

        def body(kt, acc):
            return acc + fold8((key_ref[kt] >= trial_f).astype(i32))
        acc = lax.fori_loop(0, n_sel_tiles, body, jnp.zeros((8, tq), i32))
        return jnp.sum(acc, axis=0, keepdims=True)

    c_pos = count_ge(jnp.zeros((1, tq), i32))
    v0 = jnp.where(c_pos >= topk, 0, INT_MIN).astype(i32)
    c0 = jnp.where(c_pos >= topk, c_pos, n_sel_tiles * tk)

    def bit_step(i, carry):
        v, cge = carry
        trial = v | lax.shift_left(jnp.int32(1), 30 - i)
        c = count_ge(trial)
        ok = c >= topk
        return jnp.where(ok, trial, v), jnp.where(ok, c, cge)

    v_thr, c_ge = lax.fori_loop(0, 31, bit_step, (v0, c0))
    v_f = jnp.where(t_row + 1 <= topk, -jnp.inf, as_float(v_thr))

    def write_plain(kt, carry):
        key = key_ref[kt]
        dbias_ref[kt] = jnp.where((key >= v_f) & (key > -jnp.inf), 0.0, NEG_BIG)
        return carry

    def plain_path():
        lax.fori_loop(0, n_sel_tiles, write_plain, 0)

    def tie_path():
        need = topk - count_ge(v_thr + 1)

        def count_tie_below(limit):
            def body(kt, acc):
                sp = kt * tk + lax.broadcasted_iota(i32, (tk, 1), 0)
                return acc + fold8(((key_ref[kt] == v_f) & (sp < limit)).astype(i32))
            acc = lax.fori_loop(0, n_sel_tiles, body, jnp.zeros((8, tq), i32))
            return jnp.sum(acc, axis=0, keepdims=True)

        n_bits = max(1, (t_len - 1).bit_length())

        def idx_step(i, w):
            trial = w | lax.shift_left(jnp.int32(1), n_bits - 1 - i)
            return jnp.where(count_tie_below(trial) < need, trial, w)

        w_last = lax.fori_loop(0, n_bits, idx_step, jnp.zeros((1, tq), i32))

        def write_tie(kt, carry):
            sp = kt * tk + lax.broadcasted_iota(i32, (tk, 1), 0)
            key = key_ref[kt]
            take = (key > v_f) | ((key == v_f) & (sp <= w_last))
            dbias_ref[kt] = jnp.where(take & (key > -jnp.inf), 0.0, NEG_BIG)
            return carry

        lax.fori_loop(0, n_sel_tiles, write_tie, 0)

    lax.cond(jnp.max(c_ge) > topk, tie_path, plain_path)

    def dsa_bias_fn(kt):
        return dbias_ref[kt]

    heads_of = [[g * heads_per_group + r for r in range(heads_per_group)] for g in range(KV_GROUPS)]
    slopes = [[ALIBI[h] for h in hs] for hs in heads_of]
    qn = [[q_head(O_Q, h) for h in hs] for hs in heads_of]
    qd = [[q_head(O_QD, h) for h in hs] for hs in heads_of]
    sel_skip = lambda kt: (flag_ref[0, kt] + flag_ref[1, kt]) == 0
    a_s = _flash_groups(qn, slopes, kvb_ref, pos_ref, 2, 3, tk, 0, n_sel_tiles,
                        [sel_bias_fn(g) for g in range(KV_GROUPS)], sel_skip)
    a_w = _flash_groups(qn, slopes, kvb_ref, pos_ref, 6, 7, tkw, 0, 1, [win_bias_fn], start_fn=lambda kt: win_start)
    a_d = _flash_groups(qd, slopes, kvb_ref, pos_ref, 4, 5, tk, 0, n_sel_tiles, [dsa_bias_fn])
    for g in range(KV_GROUPS):
        for jj in range(heads_per_group // 2):
            j = heads_of[g][2 * jj] // 2
            o_n = (gate_pair(j, 0) * pair(o_cmp[2 * j], o_cmp[2 * j + 1])
                   + gate_pair(j, 1) * normalised_pair(a_s[g], 2 * jj, g)
                   + gate_pair(j, 2) * normalised_pair(a_w[g], 2 * jj, g))
            o_ref[0, :, j * LANES:(j + 1) * LANES] = o_n.astype(o_ref.dtype)
            o_d = normalised_pair(a_d[g], 2 * jj, g)
            o_ref[0, :, HALF_W + j * LANES:HALF_W + (j + 1) * LANES] = o_d.astype(o_ref.dtype)


def _odd_attn_prompt(zq, misc, kvb, wc2, bsz, t_len, tq=256, tk=512):
    tk = min(tk, t_len)
    topk = min(DSA_TOPK_MAX, t_len // 4)
    nb = t_len // BLK
    n_kt = t_len // tk
    zq3 = zq.reshape(bsz, t_len, zq.shape[1])
    misc3 = misc.reshape(bsz, t_len, LANES)
    kvb3 = kvb.reshape(bsz, t_len, kvb.shape[1])
    kern = functools.partial(_odd_attn_kernel, t_len=t_len, tq=tq, tk=tk, topk=topk)
    out = pl.pallas_call(
        kern,
        grid=(bsz, t_len // tq),
        in_specs=[pl.BlockSpec((1, tq, zq.shape[1]), lambda b, i: (b, i, 0)),
                  pl.BlockSpec((1, tq, LANES), lambda b, i: (b, i, 0)),
                  pl.BlockSpec((1, t_len, kvb.shape[1]), lambda b, i: (b, 0, 0)),
                  pl.BlockSpec((1, t_len, LANES), lambda b, i: (b, 0, 0)),
                  pl.BlockSpec((2, BLK, LANES), lambda b, i: (0, 0, 0))],
        out_specs=pl.BlockSpec((1, tq, 2 * HALF_W), lambda b, i: (b, i, 0)),
        out_shape=jax.ShapeDtypeStruct((bsz, t_len, 2 * HALF_W), jnp.bfloat16),
        scratch_shapes=[pltpu.VMEM((nb, LANES), jnp.bfloat16),
                        pltpu.VMEM((KV_GROUPS, nb, LANES), jnp.bfloat16),
                        pltpu.VMEM((t_len, IDX_DIM), jnp.bfloat16),
                        pltpu.VMEM((n_kt, tk, tq), jnp.float32),
                        pltpu.VMEM((n_kt, tk, tq), jnp.float32),
                        pltpu.SMEM((KV_GROUPS, n_kt), jnp.int32),
                        pltpu.VMEM((t_len, LANES), jnp.bfloat16)],
        compiler_params=pltpu.CompilerParams(
            dimension_semantics=("arbitrary", "arbitrary"), vmem_limit_bytes=VMEM_LIMIT),
        name="odd_attn_prompt",
    )(zq3, misc3, kvb3, misc3, wc2)
    return out.reshape(bsz * t_len, 2 * HALF_W)


PAGE = 128
ISC_ROWS = 24


SAMPLE_SEQS = 2
_DONE = object()


def _odd_sample_kernel(pt_ref, *refs, n_pages, topk, nbp):
    del pt_ref
    seqs = [_odd_sample_one(u, *refs, n_pages=n_pages, topk=topk, nbp=nbp) for u in range(SAMPLE_SEQS)]
    while seqs:
        seqs = [g for g in seqs if next(g, _DONE) is not _DONE]


def _odd_sample_one(u, qn_ref, qd_ref, qi_ref, wi_ref, gate_ref, newb_ref, newki_ref, neww_ref,
                    win_ref, wc_ref, wct_ref, *rest, n_pages, topk, nbp):
    f32, bf16, i32 = jnp.float32, jnp.bfloat16, jnp.int32
    all_pages = SAMPLE_SEQS * n_pages
    kv_refs = rest[u * n_pages:(u + 1) * n_pages]
    idx_refs = rest[all_pages + u * n_pages:all_pages + (u + 1) * n_pages]
    o_ref, wout_ref, isc_all_ref = rest[2 * all_pages:]
    isc_ref = isc_all_ref.at[u]
    past = n_pages * PAGE
    nb = past // BLK + 1
    row8 = lax.broadcasted_iota(i32, (8, 1), 0)
    lane = lax.broadcasted_iota(i32, (1, LANES), 1)
    slope8 = jnp.zeros((8, 1), f32)
    for h in range(NSA_HEADS):
        slope8 = jnp.where(row8 == h, ALIBI[h], slope8)
    low_group = row8 < NSA_HEADS // KV_GROUPS

    def pick_half(x):
        return jnp.where(low_group, x[:, 0:HEAD_DIM], x[:, HEAD_DIM:2 * HEAD_DIM])

    qn8, qd8 = qn_ref[u], qd_ref[u]
    qi8, wi8 = qi_ref[u], wi_ref[u]
    newb = newb_ref[u]
    gates = jax.nn.sigmoid(gate_ref[u])

    def new_score(q8, k_row):
        return jnp.sum(q8.astype(f32) * k_row.astype(f32), axis=-1, keepdims=True) + slope8 * float(past)

    def feat(p, slot):
        return kv_refs[p][0, slot * LANES:(slot + 1) * LANES, :]

    s_sel, s_dsa = [], []
    key_row = lax.broadcasted_iota(i32, (PAGE, nbp), 0)
    blk_col = lax.broadcasted_iota(i32, (PAGE, nbp), 1)
    kct = jnp.zeros((LANES, nbp), f32)
    vct = jnp.zeros((LANES, nbp), f32)
    for p in range(n_pages):
        pool = jnp.where(blk_col == 2 * p + key_row // BLK, 1.0, 0.0).astype(bf16)
        kct = kct + jnp.dot((feat(p, 0) * wct_ref[0]).astype(bf16), pool, preferred_element_type=f32)
        vct = vct + jnp.dot((feat(p, 1) * wct_ref[1]).astype(bf16), pool, preferred_element_type=f32)
        pos = (p * PAGE + lane).astype(f32)
        s_sel.append(jnp.dot(qn8, feat(p, 2).astype(bf16), preferred_element_type=f32) + slope8 * pos)
        s_dsa.append(jnp.dot(qd8, feat(p, 4).astype(bf16), preferred_element_type=f32) + slope8 * pos)
        ii = jnp.dot(qi8, idx_refs[p][0].astype(bf16), preferred_element_type=f32) * (IDX_DIM ** -0.5)
        isc_ref[p:p + 1, :] = jnp.sum(jnp.maximum(ii, 0.0) * wi8, axis=0, keepdims=True) * (IDX_HEADS ** -0.5)
    first_row = row8 == 0
    place = jnp.where(first_row & (lax.broadcasted_iota(i32, (8, nbp), 1) == nb - 1), 1.0, 0.0)
    tn = (((0,), (0,)), ((), ()))
    new_k = jnp.where(first_row, newb[:, 0:LANES].astype(f32) * wc_ref[0, 0:1, :], 0.0)
    new_v = jnp.where(first_row, newb[:, LANES:2 * LANES].astype(f32) * wc_ref[1, 0:1, :], 0.0)
    kct = kct + lax.dot_general(new_k.astype(bf16), place.astype(bf16), tn, preferred_element_type=f32)
    vct = vct + lax.dot_general(new_v.astype(bf16), place.astype(bf16), tn, preferred_element_type=f32)
    ii_new = jnp.sum(qi8.astype(f32) * newki_ref[u].astype(bf16).astype(f32), axis=-1, keepdims=True) * (IDX_DIM ** -0.5)
    isc_new = jnp.sum(jnp.maximum(ii_new, 0.0) * wi8, axis=0, keepdims=True) * (IDX_HEADS ** -0.5)
    isc_ref[n_pages:n_pages + 1, :] = jnp.where(lane == 0, isc_new, -jnp.inf)
    isc_ref[n_pages + 1:ISC_ROWS, :] = jnp.full((ISC_ROWS - n_pages - 1, LANES), -jnp.inf, f32)

    jb = lax.broadcasted_iota(i32, (1, nbp), 1)
    dist_c = past - (jb * BLK + (BLK - 1))
    mask_c = (dist_c >= 0) & (jb < nb)
    s_c = jnp.dot(qn8, kct.astype(bf16), preferred_element_type=f32) - slope8 * dist_c.astype(f32)
    s_c = jnp.where(mask_c, s_c, NEG_BIG)
    m_c = jnp.max(s_c, axis=-1, keepdims=True)
    e_c = jnp.where(mask_c, jnp.exp(s_c - m_c), 0.0)
    p_c = e_c * (1.0 / jnp.maximum(jnp.sum(e_c, axis=-1, keepdims=True), 1e-30))
    o_c = pick_half(_dot_nt(p_c.astype(bf16), vct.astype(bf16)))
    imp_lo = jnp.sum(jnp.where(low_group, p_c, 0.0), axis=0, keepdims=True)
    imp_hi = jnp.sum(jnp.where(low_group, 0.0, p_c), axis=0, keepdims=True)
    imp = jnp.where(low_group, imp_lo, imp_hi)
    cur = past // BLK
    imp = jnp.where((jb == 0) | (jb == cur) | (jb == cur - 1), FORCE_SCORE, imp)
    imp = jnp.where(jb <= cur, imp, -jnp.inf)
    rank = jnp.zeros((8, nbp), i32)
    for j in range(nb):
        other = imp[:, j:j + 1]
        rank = rank + ((other > imp) | ((other == imp) & (j < jb))).astype(i32)
    sel = jnp.where((rank < min(NSA_TOPN, nb)) & (imp > -jnp.inf), 1.0, 0.0)

    krow = lax.broadcasted_iota(i32, (ISC_ROWS, LANES), 0)
    kidx = krow * LANES + lax.broadcasted_iota(i32, (ISC_ROWS, LANES), 1)
    key = jnp.where(kidx <= past, isc_ref[...], -jnp.inf)

    def total(x):
        folded = jnp.sum(x.reshape(ISC_ROWS // 8, 8, LANES), axis=0)
        return jnp.sum(jnp.sum(folded, axis=1, keepdims=True), axis=0, keepdims=True)

    def as_float(pattern):
        bits = jnp.where(pattern < 0, pattern ^ jnp.int32(0x7FFFFFFF), pattern)
        return lax.bitcast_convert_type(bits, f32)

    def count_ge(trial):
        return total((key >= as_float(trial)).astype(i32))

    c_pos = count_ge(jnp.zeros((1, 1), i32))
    v_thr = jnp.where(c_pos >= topk, 0, INT_MIN).astype(i32)
    for shift, width in ((27, 4), (23, 4), (19, 4), (15, 4), (11, 4), (7, 4), (3, 4), (0, 3)):
        digit = jnp.zeros((1, 1), i32)
        for j in range(1, 1 << width):
            ok = count_ge(v_thr | jnp.int32(j << shift)) >= topk
            digit = digit + ok.astype(i32)
        v_thr = v_thr | lax.shift_left(digit, jnp.int32(shift))
    v_f = as_float(v_thr) if past + 1 > topk else jnp.full((1, 1), -jnp.inf, f32)
    valid = key > -jnp.inf

    def tie_mask():
        need = topk - count_ge(v_thr + 1)
        tie = key == v_f
        w_last = jnp.zeros((1, 1), i32)
        for shift in (8, 4, 0):
            digit = jnp.zeros((1, 1), i32)
            for j in range(1, 16):
                below = total((tie & (kidx < (w_last | jnp.int32(j << shift)))).astype(i32))
                digit = digit + (below < need).astype(i32)
            w_last = w_last | lax.shift_left(digit, jnp.int32(shift))
        take = (key > v_f) | (tie & (kidx <= w_last))
        return jnp.where(take & valid, 1.0, 0.0)

    def plain_mask():
        return jnp.where((key >= v_f) & valid, 1.0, 0.0)

    has_surplus_tie = count_ge(v_thr)[0, 0] > topk
    yield
    dsel = lax.cond(has_surplus_tie, tie_mask, plain_mask)
    yield

    def attend(tiles, masks, s_new, new_ok, v_tile, v_new):
        tiles = [jnp.where(mk, t, NEG_BIG) for t, mk in zip(tiles, masks)]
        s_new = jnp.where(new_ok, s_new, NEG_BIG)
        m = s_new
        for t in tiles:
            m = jnp.maximum(m, jnp.max(t, axis=-1, keepdims=True))
        l = jnp.exp(s_new - m)
        acc = l * v_new.astype(f32)
        for j, t in enumerate(tiles):
            e = jnp.exp(t - m)
            l = l + jnp.sum(e, axis=-1, keepdims=True)
            acc = acc + _dot_nt(e.astype(bf16), v_tile(j))
        return pick_half(acc * (1.0 / l))

    sel_masks = [jnp.where(lane < BLK, sel[:, 2 * p:2 * p + 1], sel[:, 2 * p + 1:2 * p + 2]) > 0.0
                 for p in range(n_pages)]
    o_s = attend(s_sel, sel_masks, new_score(qn8, newb[:, 2 * LANES:3 * LANES]), sel[:, nb - 1:nb] > 0.0,
                 lambda p: feat(p, 3).astype(bf16), newb[:, 3 * LANES:4 * LANES])
    dsa_masks = [dsel[p:p + 1, :] > 0.0 for p in range(n_pages)]
    o_d = attend(s_dsa, dsa_masks, new_score(qd8, newb[:, 4 * LANES:5 * LANES]), dsel[n_pages:n_pages + 1, 0:1] > 0.0,
                 lambda p: feat(p, 5).astype(bf16), newb[:, 5 * LANES:6 * LANES])
    n_win = win_ref.shape[2]
    wlane = lax.broadcasted_iota(i32, (1, n_win), 1)
    wpos = past - n_win + wlane
    s_w = jnp.dot(qn8, win_ref[u, 0:LANES, :].astype(bf16), preferred_element_type=f32) + slope8 * wpos.astype(f32)
    w_ok = (past - wpos < WINDOW) & (wpos >= 0)
    o_w = attend([s_w], [w_ok], new_score(qn8, newb[:, 6 * LANES:7 * LANES]), True,
                 lambda p: win_ref[u, LANES:2 * LANES, :].astype(bf16), newb[:, 7 * LANES:8 * LANES])
    o_ref[u, 0:NSA_HEADS, :] = gates[:, 0:1] * o_c + gates[:, 1:2] * o_s + gates[:, 2:3] * o_w
    o_ref[u, NSA_HEADS:2 * NSA_HEADS, :] = o_d
    last = jnp.where(first_row & (lax.broadcasted_iota(i32, (8, n_win), 1) == n_win - 1), 1.0, 0.0)
    new_w = jnp.where(first_row, neww_ref[u], 0.0)
    w_hi = new_w.astype(bf16).astype(f32)
    w_mid = (new_w - w_hi).astype(bf16).astype(f32)
    w_lo = new_w - w_hi - w_mid
    pieces = jnp.concatenate([w_hi, w_mid, w_lo, jnp.zeros_like(new_w)], axis=0).astype(bf16)
    placed = lax.dot_general(pieces, jnp.concatenate([last] * 4, axis=0).astype(bf16), tn,
                             preferred_element_type=f32)
    wout_ref[u] = jnp.where(wlane == n_win - 1, placed, pltpu.roll(win_ref[u], n_win - 1, 1))


def _odd_attn_sample(zq, misc, kvb, win_new, cache_kv, cache_idx, cache_win, page_table, w_cmp):
    f32 = jnp.float32
    s_n, n_pages = page_table.shape
    n_pool = cache_kv.shape[0]
    n_win = cache_win.shape[1]
    past = n_pages * PAGE
    topk = min(DSA_TOPK_MAX, (past + 1) // 4)
    nbp = _round_up(past // BLK + 1, 8)
    wc2 = jnp.concatenate([w_cmp, w_cmp], axis=-1)
    wct = jnp.tile(jnp.swapaxes(w_cmp, 1, 2), (1, KV_GROUPS, PAGE // BLK))
    qn = zq[:, O_Q:O_QD].reshape(s_n, NSA_HEADS, LANES)
    qd = zq[:, O_QD:ZQ_QI].reshape(s_n, DSA_HEADS, LANES)
    qi = jnp.pad(zq[:, ZQ_QI:ZQ_QI + LANES].reshape(s_n, IDX_HEADS, IDX_DIM), ((0, 0), (0, 8 - IDX_HEADS), (0, 0)))
    wi = jnp.pad(misc[:, MISC_WI:MISC_WI + IDX_HEADS].reshape(s_n, IDX_HEADS, 1), ((0, 0), (0, 8 - IDX_HEADS), (0, 0)))
    gts = misc[:, MISC_GATE:MISC_GATE + 3 * NSA_HEADS].reshape(s_n, NSA_HEADS, 3)
    kv_pages = jnp.transpose(cache_kv, (0, 2, 3, 4, 1)).reshape(n_pool, KV_SLOTS * LANES, PAGE)
    idx_pages = jnp.transpose(cache_idx, (0, 2, 1))
    win_t = jnp.transpose(cache_win, (0, 2, 3, 4, 1)).reshape(s_n, 2 * LANES, n_win)
    n_u = SAMPLE_SEQS
    per_seq = lambda shape: pl.BlockSpec((n_u,) + shape, lambda b, pt: (b, 0, 0))
    page_spec = lambda rows, u, p: pl.BlockSpec((1, rows, PAGE), lambda b, pt, u=u, p=p: (pt[b * n_u + u, p], 0, 0))
    seq_pages = [(u, p) for u in range(n_u) for p in range(n_pages)]
    in_specs = ([per_seq((NSA_HEADS, LANES)), per_seq((DSA_HEADS, LANES)), per_seq((8, IDX_DIM)),
                 per_seq((8, 1)), per_seq((NSA_HEADS, 3)), per_seq((1, 8 * LANES)), per_seq((1, IDX_DIM)),
                 per_seq((1, 2 * LANES)), per_seq((2 * LANES, n_win)),
                 pl.BlockSpec((2, BLK, LANES), lambda b, pt: (0, 0, 0)),
                 pl.BlockSpec((2, LANES, PAGE), lambda b, pt: (0, 0, 0))]
                + [page_spec(KV_SLOTS * LANES, u, p) for u, p in seq_pages]
                + [page_spec(IDX_DIM, u, p) for u, p in seq_pages])
    kern = functools.partial(_odd_sample_kernel, n_pages=n_pages, topk=topk, nbp=nbp)
    o, win_out = pl.pallas_call(
        kern,
        grid_spec=pltpu.PrefetchScalarGridSpec(
            num_scalar_prefetch=1,
            grid=(s_n // n_u,),
            in_specs=in_specs,
            out_specs=[per_seq((2 * NSA_HEADS, HEAD_DIM)), per_seq((2 * LANES, n_win))],
            scratch_shapes=[pltpu.VMEM((n_u, ISC_ROWS, LANES), f32)]),
        out_shape=[jax.ShapeDtypeStruct((s_n, 2 * NSA_HEADS, HEAD_DIM), f32),
                   jax.ShapeDtypeStruct((s_n, 2 * LANES, n_win), f32)],
        compiler_params=pltpu.CompilerParams(
            dimension_semantics=("arbitrary",), vmem_limit_bytes=VMEM_LIMIT),
        name="odd_attn_sample",
    )(page_table, qn, qd, qi, wi, gts, kvb.reshape(s_n, 1, 8 * LANES), misc[:, 0:IDX_DIM].reshape(s_n, 1, IDX_DIM),
      win_new.reshape(s_n, 1, 2 * LANES), win_t, wc2, wct,
      *([kv_pages] * len(seq_pages)), *([idx_pages] * len(seq_pages)))
    win_out = jnp.transpose(win_out.reshape(s_n, 2, KV_GROUPS, HEAD_DIM, n_win), (0, 4, 1, 2, 3))
    return o.reshape(s_n, 2 * HALF_W), win_out


def _split(z, sizes):
    cuts = [int(c) for c in np.cumsum(sizes)[:-1]]
    return jnp.split(z, cuts, axis=-1)


def _pad_cols(w, n):
    return jnp.pad(w, ((0, 0), (0, n - w.shape[1])))


def kernel(x_prompt, x_sample, state_conv, state_C, state_n, state_m, cache_kv, cache_idx, cache_win, page_table,
           w_in_e, gate_b_e, conv_w, conv_b, w_out_e, w_in_o, w_cmp, w_out_o, w_up, w_down, ln_g, ln_b):
    bf16 = jnp.bfloat16
    bp, tp, d = x_prompt.shape
    dbs = x_sample.shape[0]
    keep = cache_win.shape[2]
    yp = x_prompt.reshape(bp * tp, d)
    ys = x_sample.reshape(dbs, d)
    tm_p, tm_s = 512, dbs
    outs = {}
    for layer in range(DEPTH):
        i = layer // 2
        if layer % 2 == 0:
            w_in = _pad_cols(w_in_e[i], E_PACKED).astype(bf16)
            w_out = w_out_e[i].astype(bf16)
            ya, qkv, og, gts, tails = _even_proj(yp, w_in, conv_w[i], conv_b[i], tm_p, tp)
            yb, c_fin, n_fin, m_fin = _mlstm_prompt(qkv, og, gts, gate_b_e[i], bp, tp)
            cvp = tails.reshape(bp, tp // tm_p, CONV_TAIL, HALF_W)[:, -1, CONV_TAIL - (CONV_W - 1):]
            ya_s, yb_s, cvs, c_s, n_s, m_s = _even_sample(_proj(ys, w_in, tm_s), gate_b_e[i], conv_w[i], conv_b[i],
                                                          state_conv[i], state_C[i], state_n[i], state_m[i])
            outs['conv'] = (cvp[None], cvs[None])
            outs['c'] = (c_fin[None], c_s[None])
            outs['n'] = (n_fin[None], n_s[None])
            outs['m'] = (m_fin[:, :, 0][None], m_s[:, :, 0][None])
            yp = _outproj2_ln(ya, yb.reshape(bp * tp, HALF_W), w_out, yp, ln_g[layer, 0], ln_b[layer, 0], tm_p)
            ys = _outproj2_ln(ya_s, yb_s, w_out, ys, ln_g[layer, 0], ln_b[layer, 0], tm_s)
        else:
            w_out = w_out_o[i].astype(bf16)
            w_packed = _pack_w_in_o(w_in_o[i]).astype(bf16)
            wc2 = jnp.concatenate([w_cmp[i], w_cmp[i]], axis=-1)
            rows_first = lambda kvt: jnp.transpose(
                kvt.reshape(kvt.shape[0], KV_SLOTS, KV_GROUPS, HEAD_DIM, kvt.shape[2]), (0, 4, 1, 2, 3))
            zq, kvt_p, kvb, winp, miscp = _odd_proj(yp, w_packed, tm_p, tp)
            mp = _odd_attn_prompt(zq, miscp, kvb, wc2, bp, tp)
            kvp = rows_first(kvt_p)
            ixp = miscp[:, :IDX_DIM].reshape(bp, tp, IDX_DIM)
            wnp = winp.reshape(bp, tp, 2, KV_GROUPS, HEAD_DIM)[:, tp - keep:]
            zq_s, kvt_s, kvb_s, win_s, misc_s = _odd_proj(ys, w_packed, tm_s, dbs)
            ms, wns = _odd_attn_sample(zq_s, misc_s, kvb_s, win_s, cache_kv[i], cache_idx[i], cache_win[i],
                                       page_table, w_cmp[i])
            kvs = rows_first(kvt_s).reshape(dbs, 1, KV_SLOTS, KV_GROUPS, HEAD_DIM)
            ixs = misc_s[:, :IDX_DIM].reshape(dbs, 1, IDX_DIM)
            outs['kv'] = (kvp[None], kvs[None])
            outs['idx'] = (ixp[None], ixs[None])
            outs['win'] = (wnp[None], wns[None])
            yp = _outproj_ln(mp, w_out, yp, ln_g[layer, 0], ln_b[layer, 0], tm_p)
            ys = _outproj_ln(ms, w_out, ys, ln_g[layer, 0], ln_b[layer, 0], tm_s)
        wu, wd = w_up[layer].astype(bf16), w_down[layer].astype(bf16)
        yp = _mlp_ln(yp, wu, wd, ln_g[layer, 1], ln_b[layer, 1], 2 * tm_p, 1024)
        ys = _mlp_ln(ys, wu, wd, ln_g[layer, 1], ln_b[layer, 1], tm_s, 1024)
    return (yp.reshape(bp, tp, d), ys.reshape(dbs, 1, d),
            outs['conv'][0], outs['conv'][1], outs['c'][0], outs['c'][1],
            outs['n'][0], outs['n'][1], outs['m'][0], outs['m'][1],
            outs['kv'][0], outs['kv'][1], outs['idx'][0], outs['idx'][1],
            outs['win'][0], outs['win'][1])
```

```python
import functools

import jax
import jax.numpy as jnp
import numpy as np
from jax import lax
from jax.experimental import pallas as pl
from jax.experimental.pallas import tpu as pltpu

D_MODEL = 1024
DEPTH = 2
HALF_W = 512
D_FF = 4096
CONV_W = 3
ML_HEADS = 4
ML_DIM = 128
HEAD_DIM = 64
KV_GROUPS = 2
NSA_HEADS = 8
DSA_HEADS = 8
BLK = 64
NSA_TOPN = 8
WINDOW = 256
IDX_HEADS = 4
IDX_DIM = 32
DSA_TOPK_MAX = 256
KV_SLOTS = 6
FORCE_SCORE = 1e4
ALPHA = (2.0 * DEPTH) ** 0.25
LN_EPS = 1e-5
O_SIZES = (512, 256, 256, 256, 24, 512, 256, 128, 32, 4)
LANES = 128
VMEM_LIMIT = 48 * 1024 * 1024


def _round_up(n, m):
    return -(-n // m) * m


def _proj_kernel(x_ref, w_ref, o_ref):
    o_ref[...] = jnp.dot(x_ref[...].astype(jnp.bfloat16), w_ref[...],
                         preferred_element_type=jnp.float32)


def _proj(x, w_bf16, tm):
    m, k = x.shape
    n = w_bf16.shape[1]
    tn = n
    for cand in (1024, 768, 512, 256, 128):
        if n % cand == 0:
            tn = cand
            break
    return pl.pallas_call(
        _proj_kernel,
        grid=(n // tn, m // tm),
        in_specs=[pl.BlockSpec((tm, k), lambda j, i: (i, 0)),
                  pl.BlockSpec((k, tn), lambda j, i: (0, j))],
        out_specs=pl.BlockSpec((tm, tn), lambda j, i: (i, j)),
        out_shape=jax.ShapeDtypeStruct((m, n), jnp.float32),
        compiler_params=pltpu.CompilerParams(
            dimension_semantics=("arbitrary", "arbitrary"), vmem_limit_bytes=VMEM_LIMIT),
        name="proj",
    )(x, w_bf16)


def _layer_norm_rows(v, g, b):
    mu = jnp.mean(v, axis=-1, keepdims=True)
    d = v - mu
    var = jnp.mean(d * d, axis=-1, keepdims=True)
    return d * lax.rsqrt(var + LN_EPS) * g + b


def _outproj_ln_kernel(y_ref, w_ref, x_ref, g_ref, b_ref, o_ref):
    mix = jnp.dot(y_ref[...].astype(jnp.bfloat16), w_ref[...], preferred_element_type=jnp.float32)
    o_ref[...] = _layer_norm_rows(ALPHA * x_ref[...] + mix, g_ref[...], b_ref[...])


def _outproj_ln(y, w_bf16, x, g, b, tm):
    m, d = x.shape
    k = y.shape[1]
    return pl.pallas_call(
        _outproj_ln_kernel,
        grid=(m // tm,),
        in_specs=[pl.BlockSpec((tm, k), lambda i: (i, 0)),
                  pl.BlockSpec((k, d), lambda i: (0, 0)),
                  pl.BlockSpec((tm, d), lambda i: (i, 0)),
                  pl.BlockSpec((1, d), lambda i: (0, 0)),
                  pl.BlockSpec((1, d), lambda i: (0, 0))],
        out_specs=pl.BlockSpec((tm, d), lambda i: (i, 0)),
        out_shape=jax.ShapeDtypeStruct((m, d), jnp.float32),
        compiler_params=pltpu.CompilerParams(
            dimension_semantics=("arbitrary",), vmem_limit_bytes=VMEM_LIMIT),
        name="outproj_ln",
    )(y, w_bf16, x, g.reshape(1, d), b.reshape(1, d))


def _mlp_ln_kernel(x_ref, wu_ref, wd_ref, g_ref, b_ref, o_ref, acc_ref):
    f = pl.program_id(1)

    @pl.when(f == 0)
    def _():
        acc_ref[...] = jnp.zeros_like(acc_ref)

    up = jnp.dot(x_ref[...].astype(jnp.bfloat16), wu_ref[...], preferred_element_type=jnp.float32)
    act = jnp.square(jnp.maximum(up, 0.0))
    acc_ref[...] += jnp.dot(act.astype(jnp.bfloat16), wd_ref[...], preferred_element_type=jnp.float32)

    @pl.when(f == pl.num_programs(1) - 1)
    def _():
        o_ref[...] = _layer_norm_rows(ALPHA * x_ref[...] + acc_ref[...], g_ref[...], b_ref[...])


def _mlp_ln(x, wu_bf16, wd_bf16, g, b, tm, tf):
    m, d = x.shape
    ff = wu_bf16.shape[1]
    return pl.pallas_call(
        _mlp_ln_kernel,
        grid=(m // tm, ff // tf),
        in_specs=[pl.BlockSpec((tm, d), lambda i, f: (i, 0)),
                  pl.BlockSpec((d, tf), lambda i, f: (0, f)),
                  pl.BlockSpec((tf, d), lambda i, f: (f, 0)),
                  pl.BlockSpec((1, d), lambda i, f: (0, 0)),
                  pl.BlockSpec((1, d), lambda i, f: (0, 0))],
        out_specs=pl.BlockSpec((tm, d), lambda i, f: (i, 0)),
        out_shape=jax.ShapeDtypeStruct((m, d), jnp.float32),
        scratch_shapes=[pltpu.VMEM((tm, d), jnp.float32)],
        compiler_params=pltpu.CompilerParams(
            dimension_semantics=("arbitrary", "arbitrary"), vmem_limit_bytes=VMEM_LIMIT),
        name="mlp_ln",
    )(x, wu_bf16, wd_bf16, g.reshape(1, d), b.reshape(1, d))


E_QKV = 3 * HALF_W
E_OG = 6 * HALF_W
E_GATE = 7 * HALF_W
E_PACKED = 7 * HALF_W + LANES
CONV_TAIL = 8


def _even_proj_kernel(x_ref, w_ref, cw_ref, cb_ref, ya_ref, qkv_ref, og_ref, gate_ref, tail_ref, carry_ref,
                      *, tiles_per_seq):
    f32, bf16 = jnp.float32, jnp.bfloat16
    tm = x_ref.shape[0]

    @pl.when(pl.program_id(0) % tiles_per_seq == 0)
    def _():
        carry_ref[...] = jnp.zeros_like(carry_ref)

    z = jnp.dot(x_ref[...].astype(bf16), w_ref[...], preferred_element_type=f32)
    u = z[:, 2 * HALF_W:3 * HALF_W] * z[:, 0:HALF_W]
    prev = carry_ref[...]
    row = lax.broadcasted_iota(jnp.int32, (tm, 1), 0)
    conv = cb_ref[...] + cw_ref[CONV_W - 1:CONV_W, :] * u
    for back in range(1, CONV_W):
        shifted = pltpu.roll(u, back, 0)
        for r in range(back):
            shifted = jnp.where(row == r, prev[CONV_TAIL - back + r:CONV_TAIL - back + r + 1, :], shifted)
        conv = conv + cw_ref[CONV_W - 1 - back:CONV_W - back, :] * shifted
    ya_ref[...] = (z[:, HALF_W:2 * HALF_W] * conv).astype(bf16)
    tail = u[tm - CONV_TAIL:]
    carry_ref[...] = tail
    tail_ref[0] = tail
    qkv_ref[:, 0:HALF_W] = z[:, E_QKV:E_QKV + HALF_W].astype(bf16)
    qkv_ref[:, HALF_W:2 * HALF_W] = (z[:, E_QKV + HALF_W:E_QKV + 2 * HALF_W] * (ML_DIM ** -0.5)).astype(bf16)
    qkv_ref[:, 2 * HALF_W:3 * HALF_W] = z[:, E_QKV + 2 * HALF_W:E_OG].astype(bf16)
    og_ref[...] = z[:, E_OG:E_GATE]
    gate_ref[...] = z[:, E_GATE:E_PACKED]


def _even_proj(x, w_bf16, conv_w, conv_b, tm, seq_len):
    m, k = x.shape
    widths = (HALF_W, 3 * HALF_W, HALF_W, LANES)
    dtypes = (jnp.bfloat16, jnp.bfloat16, jnp.float32, jnp.float32)
    kern = functools.partial(_even_proj_kernel, tiles_per_seq=seq_len // tm)
    return pl.pallas_call(
        kern,
        grid=(m // tm,),
        in_specs=[pl.BlockSpec((tm, k), lambda i: (i, 0)),
                  pl.BlockSpec((k, E_PACKED), lambda i: (0, 0)),
                  pl.BlockSpec((CONV_W, HALF_W), lambda i: (0, 0)),
                  pl.BlockSpec((1, HALF_W), lambda i: (0, 0))],
        out_specs=[pl.BlockSpec((tm, n), lambda i: (i, 0)) for n in widths]
        + [pl.BlockSpec((1, CONV_TAIL, HALF_W), lambda i: (i, 0, 0))],
        out_shape=[jax.ShapeDtypeStruct((m, n), dt) for n, dt in zip(widths, dtypes)]
        + [jax.ShapeDtypeStruct((m // tm, CONV_TAIL, HALF_W), jnp.float32)],
        scratch_shapes=[pltpu.VMEM((CONV_TAIL, HALF_W), jnp.float32)],
        compiler_params=pltpu.CompilerParams(
            dimension_semantics=("arbitrary",), vmem_limit_bytes=VMEM_LIMIT),
        name="even_proj",
    )(x, w_bf16, conv_w, conv_b.reshape(1, HALF_W))


def _mlstm_kernel(qkv_ref, og_ref, gate_ref, gb_ref, yb_ref, c_ref, n_ref, m_ref):
    f32, bf16 = jnp.float32, jnp.bfloat16
    L = qkv_ref.shape[1]

    @pl.when(pl.program_id(1) == 0)
    def _():
        c_ref[...] = jnp.zeros_like(c_ref)
        n_ref[...] = jnp.zeros_like(n_ref)
        m_ref[...] = jnp.zeros_like(m_ref)

    pre = gate_ref[0] + gb_ref[...]
    lf = jax.nn.log_sigmoid(pre)
    row = lax.broadcasted_iota(jnp.int32, (L, L), 0)
    col = lax.broadcasted_iota(jnp.int32, (L, L), 1)
    causal = col <= row
    tri = jnp.where(causal, 1.0, 0.0).astype(f32)
    b_all = jnp.dot(tri, lf, preferred_element_type=f32, precision=lax.Precision.HIGHEST)
    pre_t = pre.T
    b_t = b_all.T
    for h in range(ML_HEADS):
        q = qkv_ref[0, :, h * ML_DIM:(h + 1) * ML_DIM]
        k = qkv_ref[0, :, HALF_W + h * ML_DIM:HALF_W + (h + 1) * ML_DIM]
        v = qkv_ref[0, :, 2 * HALF_W + h * ML_DIM:2 * HALF_W + (h + 1) * ML_DIM]
        ig_col = pre[:, h:h + 1]
        b_col = b_all[:, ML_HEADS + h:ML_HEADS + h + 1]
        a_row = pre_t[h:h + 1, :] - b_t[ML_HEADS + h:ML_HEADS + h + 1, :]
        m_st = m_ref[0, h:h + 1, 0:1]
        c_st = c_ref[0, h]
        n_st = n_ref[0, h:h + 1, :]
        dmat = jnp.where(causal, b_col + a_row, -jnp.inf)
        inter = b_col + m_st
        m_t = jnp.maximum(inter, jnp.max(dmat, axis=-1, keepdims=True))
        w_intra = jnp.exp(dmat - m_t)
        w_inter = jnp.exp(inter - m_t)
        s = _dot_nt(q, k) * w_intra
        num = jnp.dot(s.astype(bf16), v, preferred_element_type=f32) + w_inter * _dot_nt(q, c_st.astype(bf16))
        den = jnp.sum(s, axis=-1, keepdims=True) + w_inter * jnp.sum(q.astype(f32) * n_st, axis=-1, keepdims=True)
        hs = num * (1.0 / jnp.maximum(jnp.abs(den), jnp.exp(-m_t)))
        m_new = m_t[L - 1:L, :]
        b_last = b_col[L - 1:L, :]
        w_state = jnp.exp(b_last - b_col + ig_col - m_new)
        decay = jnp.exp(b_last + m_st - m_new)
        vw = (v.astype(f32) * w_state).astype(bf16)
        c_ref[0, h] = decay * c_st + lax.dot_general(vw, k, (((0,), (0,)), ((), ())), preferred_element_type=f32)
        n_ref[0, h:h + 1, :] = decay * n_st + jnp.sum(k.astype(f32) * w_state, axis=0, keepdims=True)
        m_ref[0, h:h + 1, :] = jnp.broadcast_to(m_new, (1, LANES))
        og = og_ref[0, :, h * ML_DIM:(h + 1) * ML_DIM]
        yb_ref[0, :, h * ML_DIM:(h + 1) * ML_DIM] = (jax.nn.sigmoid(og) * hs).astype(bf16)


def _mlstm_prompt(qkv, og, gates, gate_b, bsz, t_len, chunk=128):
    gb = jnp.pad(gate_b, (0, LANES - gate_b.shape[0])).reshape(1, LANES)
    return pl.pallas_call(
        _mlstm_kernel,
        grid=(bsz, t_len // chunk),
        in_specs=[pl.BlockSpec((1, chunk, 3 * HALF_W), lambda b, c: (b, c, 0)),
                  pl.BlockSpec((1, chunk, HALF_W), lambda b, c: (b, c, 0)),
                  pl.BlockSpec((1, chunk, LANES), lambda b, c: (b, c, 0)),
                  pl.BlockSpec((1, LANES), lambda b, c: (0, 0))],
        out_specs=[pl.BlockSpec((1, chunk, HALF_W), lambda b, c: (b, c, 0)),
                   pl.BlockSpec((1, ML_HEADS, ML_DIM, ML_DIM), lambda b, c: (b, 0, 0, 0)),
                   pl.BlockSpec((1, ML_HEADS, ML_DIM), lambda b, c: (b, 0, 0)),
                   pl.BlockSpec((1, ML_HEADS, LANES), lambda b, c: (b, 0, 0))],
        out_shape=[jax.ShapeDtypeStruct((bsz, t_len, HALF_W), jnp.bfloat16),
                   jax.ShapeDtypeStruct((bsz, ML_HEADS, ML_DIM, ML_DIM), jnp.float32),
                   jax.ShapeDtypeStruct((bsz, ML_HEADS, ML_DIM), jnp.float32),
                   jax.ShapeDtypeStruct((bsz, ML_HEADS, LANES), jnp.float32)],
        compiler_params=pltpu.CompilerParams(
            dimension_semantics=("arbitrary", "arbitrary"), vmem_limit_bytes=VMEM_LIMIT),
        name="mlstm_prompt",
    )(qkv.reshape(bsz, t_len, 3 * HALF_W), og.reshape(bsz, t_len, HALF_W),
      gates.reshape(bsz, t_len, LANES), gb)


def _outproj2_ln_kernel(ya_ref, yb_ref, w_ref, x_ref, g_ref, b_ref, o_ref):
    mix = jnp.dot(ya_ref[...], w_ref[0:HALF_W, :], preferred_element_type=jnp.float32)
    mix = mix + jnp.dot(yb_ref[...], w_ref[HALF_W:, :], preferred_element_type=jnp.float32)
    o_ref[...] = _layer_norm_rows(ALPHA * x_ref[...] + mix, g_ref[...], b_ref[...])


def _outproj2_ln(ya, yb, w_bf16, x, g, b, tm):
    m, d = x.shape
    return pl.pallas_call(
        _outproj2_ln_kernel,
        grid=(m // tm,),
        in_specs=[pl.BlockSpec((tm, HALF_W), lambda i: (i, 0)),
                  pl.BlockSpec((tm, HALF_W), lambda i: (i, 0)),
                  pl.BlockSpec((2 * HALF_W, d), lambda i: (0, 0)),
                  pl.BlockSpec((tm, d), lambda i: (i, 0)),
                  pl.BlockSpec((1, d), lambda i: (0, 0)),
                  pl.BlockSpec((1, d), lambda i: (0, 0))],
        out_specs=pl.BlockSpec((tm, d), lambda i: (i, 0)),
        out_shape=jax.ShapeDtypeStruct((m, d), jnp.float32),
        compiler_params=pltpu.CompilerParams(
            dimension_semantics=("arbitrary",), vmem_limit_bytes=VMEM_LIMIT),
        name="outproj2_ln",
    )(ya, yb, w_bf16, x, g.reshape(1, d), b.reshape(1, d))


SAMPLE_ROWS = 8


def _even_sample_kernel(h_ref, bg_ref, cg_ref, q_ref, k_ref, v_ref, og_ref, gate_ref, gb_ref, cw_ref, cb_ref,
                        conv_ref, c_ref, n_ref, m_ref, ya_ref, yb_ref, conv_out_ref, c_out_ref, n_out_ref, m_out_ref):
    f32, bf16 = jnp.float32, jnp.bfloat16
    rows = h_ref.shape[0]
    row = lax.broadcasted_iota(jnp.int32, (rows, 1), 0)
    tn = (((0,), (0,)), ((), ()))
    u = cg_ref[...] * h_ref[...]
    prev0, prev1 = conv_ref[:, 0, :], conv_ref[:, 1, :]
    conv = cb_ref[...] + cw_ref[0:1, :] * prev0 + cw_ref[1:2, :] * prev1 + cw_ref[2:3, :] * u
    ya_ref[...] = (bg_ref[...] * conv).astype(bf16)
    conv_out_ref[:, 0, :] = prev1
    conv_out_ref[:, 1, :] = u
    pre = gate_ref[...] + gb_ref[...]
    lf_all = jax.nn.log_sigmoid(pre)
    for h in range(ML_HEADS):
        cols = slice(h * ML_DIM, (h + 1) * ML_DIM)
        q = q_ref[:, cols]
        k = k_ref[:, cols] * (ML_DIM ** -0.5)
        v = v_ref[:, cols]
        ig = pre[:, h:h + 1]
        lf = lf_all[:, ML_HEADS + h:ML_HEADS + h + 1]
        m_st = m_ref[:, h:h + 1]
        n_st = n_ref[:, h, :]
        inter = lf + m_st
        m_t = jnp.maximum(inter, ig)
        w_intra = jnp.exp(ig - m_t)
        w_inter = jnp.exp(inter - m_t)
        qb, kb = q.astype(bf16), k.astype(bf16)
        s = jnp.sum(qb.astype(f32) * kb.astype(f32), axis=-1, keepdims=True) * w_intra
        cq = jnp.zeros((rows, ML_DIM), f32)
        for b in range(rows):
            c_b = c_ref[b, h]
            cq = jnp.where(row == b, _dot_nt(qb, c_b.astype(bf16)), cq)
            vw = jnp.where(row == b, v * w_intra, 0.0)
            outer = lax.dot_general(vw, k, tn, preferred_element_type=f32, precision=lax.Precision.HIGHEST)
            c_out_ref[b, h] = w_inter[b:b + 1, :] * c_b + outer
        num = s * v + w_inter * cq
        den = s + w_inter * jnp.sum(n_st * q, axis=-1, keepdims=True)
        hs = num * (1.0 / jnp.maximum(jnp.abs(den), jnp.exp(-m_t)))
        n_out_ref[:, h, :] = w_inter * n_st + w_intra * k
        m_out_ref[:, h, :] = jnp.broadcast_to(m_t, (rows, LANES))
        yb_ref[:, cols] = (jax.nn.sigmoid(og_ref[:, cols]) * hs).astype(bf16)


def _even_sample(z, gate_b, conv_w, conv_b, state_conv, state_c, state_n, state_m):
    s_n = z.shape[0]
    f32 = jnp.float32
    r = SAMPLE_ROWS
    gb = jnp.pad(gate_b, (0, LANES - gate_b.shape[0])).reshape(1, LANES)
    col = lambda j: pl.BlockSpec((r, HALF_W), lambda i, j=j: (i, j))
    in_specs = [col(j) for j in range(7)] + [
        pl.BlockSpec((r, LANES), lambda i: (i, E_GATE // LANES)),
        pl.BlockSpec((1, LANES), lambda i: (0, 0)),
        pl.BlockSpec((CONV_W, HALF_W), lambda i: (0, 0)),
        pl.BlockSpec((1, HALF_W), lambda i: (0, 0)),
        pl.BlockSpec((r, CONV_W - 1, HALF_W), lambda i: (i, 0, 0)),
        pl.BlockSpec((r, ML_HEADS, ML_DIM, ML_DIM), lambda i: (i, 0, 0, 0)),
        pl.BlockSpec((r, ML_HEADS, ML_DIM), lambda i: (i, 0, 0)),
        pl.BlockSpec((r, ML_HEADS), lambda i: (i, 0))]
    out_specs = [pl.BlockSpec((r, HALF_W), lambda i: (i, 0)),
                 pl.BlockSpec((r, HALF_W), lambda i: (i, 0)),
                 pl.BlockSpec((r, CONV_W - 1, HALF_W), lambda i: (i, 0, 0)),
                 pl.BlockSpec((r, ML_HEADS, ML_DIM, ML_DIM), lambda i: (i, 0, 0, 0)),
                 pl.BlockSpec((r, ML_HEADS, ML_DIM), lambda i: (i, 0, 0)),
                 pl.BlockSpec((r, ML_HEADS, LANES), lambda i: (i, 0, 0))]
    out_shape = [jax.ShapeDtypeStruct((s_n, HALF_W), jnp.bfloat16),
                 jax.ShapeDtypeStruct((s_n, HALF_W), jnp.bfloat16),
                 jax.ShapeDtypeStruct((s_n, CONV_W - 1, HALF_W), f32),
                 jax.ShapeDtypeStruct((s_n, ML_HEADS, ML_DIM, ML_DIM), f32),
                 jax.ShapeDtypeStruct((s_n, ML_HEADS, ML_DIM), f32),
                 jax.ShapeDtypeStruct((s_n, ML_HEADS, LANES), f32)]
    return pl.pallas_call(
        _even_sample_kernel,
        grid=(s_n // r,),
        in_specs=in_specs,
        out_specs=out_specs,
        out_shape=out_shape,
        compiler_params=pltpu.CompilerParams(
            dimension_semantics=("arbitrary",), vmem_limit_bytes=VMEM_LIMIT),
        name="even_sample",
    )(z, z, z, z, z, z, z, z, gb, conv_w, conv_b.reshape(1, HALF_W), state_conv, state_c, state_n, state_m)


O_Q = 0
O_QD = NSA_HEADS * LANES
O_KV = 2048
O_WIN = 2816
O_QI = 3072
O_MISC = 3200
O_PACKED = 3328
ZQ_QI = 2048
MISC_WI = IDX_DIM
MISC_GATE = IDX_DIM + IDX_HEADS
COUNT_ROWS = 128
NEG_BIG = -(2.0 ** 100)
INT_MIN = -(2 ** 31)
ALIBI = tuple(float(2.0 ** (-8.0 * (h + 1) / NSA_HEADS)) for h in range(NSA_HEADS))


def _pack_w_in_o(w):
    qn, kvc, kvs, kvw, gates, qd, kvd, qi, ki, wi = _split(w, O_SIZES)
    scale = HEAD_DIM ** -0.5
    rows = w.shape[0]

    def spread(q):
        q = (q * scale).reshape(rows, NSA_HEADS, HEAD_DIM)
        z = jnp.zeros_like(q)
        low = (jnp.arange(NSA_HEADS) < NSA_HEADS // KV_GROUPS)[None, :, None]
        return jnp.concatenate([jnp.where(low, q, z), jnp.where(low, z, q)], axis=-1).reshape(rows, NSA_HEADS * LANES)

    pad = jnp.zeros((rows, O_PACKED - O_MISC - IDX_DIM - IDX_HEADS - 3 * NSA_HEADS), w.dtype)
    return jnp.concatenate([spread(qn), spread(qd), kvc, kvs, kvd, kvw, qi, ki, wi, gates, pad], axis=1)


def _odd_proj_kernel(x_ref, w_ref, wkvt_ref, zq_ref, kvt_ref, kvb_ref, win_ref, misc_ref):
    xb = x_ref[...].astype(jnp.bfloat16)
    z = jnp.dot(xb, w_ref[...], preferred_element_type=jnp.float32)
    zq_ref[:, 0:O_KV] = z[:, 0:O_KV].astype(jnp.bfloat16)
    zq_ref[:, ZQ_QI:ZQ_QI + LANES] = z[:, O_QI:O_MISC].astype(jnp.bfloat16)
    kvt_ref[0] = _dot_nt(wkvt_ref[...], xb)
    kvb_ref[...] = z[:, O_KV:O_QI].astype(jnp.bfloat16)
    win_ref[...] = z[:, O_WIN:O_QI]
    misc_ref[...] = z[:, O_MISC:O_PACKED]


def _odd_proj(x, w_packed_bf16, tm, seq_len):
    m, k = x.shape
    n_kv = O_WIN - O_KV
    widths = (O_KV + LANES, O_QI - O_KV, O_QI - O_WIN, LANES)
    dtypes = (jnp.bfloat16, jnp.bfloat16, jnp.float32, jnp.float32)
    row_spec = lambda n: pl.BlockSpec((tm, n), lambda i: (i, 0))
    tiles = seq_len // tm
    out_specs = [row_spec(widths[0]), pl.BlockSpec((1, n_kv, tm), lambda i: (i // tiles, 0, i % tiles))]
    out_specs += [row_spec(n) for n in widths[1:]]
    out_shape = [jax.ShapeDtypeStruct((m, widths[0]), dtypes[0]),
                 jax.ShapeDtypeStruct((m // seq_len, n_kv, seq_len), jnp.float32)]
    out_shape += [jax.ShapeDtypeStruct((m, n), dt) for n, dt in zip(widths[1:], dtypes[1:])]
    return pl.pallas_call(
        _odd_proj_kernel,
        grid=(m // tm,),
        in_specs=[pl.BlockSpec((tm, k), lambda i: (i, 0)),
                  pl.BlockSpec((k, O_PACKED), lambda i: (0, 0)),
                  pl.BlockSpec((n_kv, k), lambda i: (0, 0))],
        out_specs=out_specs,
        out_shape=out_shape,
        compiler_params=pltpu.CompilerParams(
            dimension_semantics=("arbitrary",), vmem_limit_bytes=VMEM_LIMIT),
        name="odd_proj",
    )(x, w_packed_bf16, w_packed_bf16[:, O_KV:O_WIN].T)


def _dot_nt(a, b):
    return lax.dot_general(a, b, (((1,), (1,)), ((), ())), preferred_element_type=jnp.float32)


def _flash_groups(q_by_group, slopes_by_group, kv_ref, pos_ref, k_blk, v_blk, tk, lo, hi, bias_fns, skip_fn=None,
                  start_fn=None):
    f32, bf16 = jnp.float32, jnp.bfloat16
    if start_fn is None:
        start_fn = lambda kt: kt * tk
    n_groups = len(q_by_group)
    tq = q_by_group[0][0].shape[0]
    n_heads = len(q_by_group[0])
    lane = lax.broadcasted_iota(jnp.int32, (1, LANES), 1)
    own_half = [(lane // HEAD_DIM) == g for g in range(n_groups)]
    feat_lane = lax.broadcasted_iota(jnp.int32, (tq, LANES), 1)
    q_all = []
    for q_heads, slopes in zip(q_by_group, slopes_by_group):
        slope_feat = [jnp.where(feat_lane == 0, BLK * sl, jnp.where(feat_lane == 1, sl, 0.0)).astype(bf16)
                      for sl in slopes]
        q_all.append(jnp.concatenate([jnp.concatenate(q_heads, axis=0), jnp.concatenate(slope_feat, axis=0)], axis=1))
    tn = (((0,), (0,)), ((), ()))

    def tile(kt, carry):
        start = pl.multiple_of(start_fn(kt), LANES)
        rows = pl.ds(start, tk)
        k = kv_ref[0, rows, k_blk * LANES:(k_blk + 1) * LANES]
        v = kv_ref[0, rows, v_blk * LANES:(v_blk + 1) * LANES]
        k_pos = jnp.concatenate([k, pos_ref[rows, :]], axis=1)
        scores = [_dot_nt(k_pos, q) for q in q_all]
        biases = [fn(kt) for fn in bias_fns] if len(bias_fns) > 1 else [bias_fns[0](kt)] * n_groups
        new_m, probs, alphas = [], [], []
        for g in range(n_groups):
            m = carry[g][0]
            s = scores[g] + jnp.concatenate([biases[g]] * n_heads, axis=1)
            m_new = jnp.maximum(m, jnp.max(s, axis=0, keepdims=True))
            probs.append(jnp.exp(s - m_new).astype(bf16))
            alphas.append(jnp.exp(m - m_new))
            new_m.append(m_new)
        out = []
        for g in range(n_groups):
            v_g = jnp.where(own_half[g], v, jnp.ones_like(v))
            pv = lax.dot_general(v_g, probs[g], tn, preferred_element_type=f32)
            out.append((new_m[g], alphas[g] * carry[g][1] + pv))
        return tuple(out)

    def body(kt, carry):
        if skip_fn is None:
            return tile(kt, carry)
        return lax.cond(skip_fn(kt), lambda c: c, functools.partial(tile, kt), carry)

    init = tuple((jnp.full((1, n_heads * tq), -jnp.inf, f32), jnp.zeros((LANES, n_heads * tq), f32))
                 for _ in range(n_groups))
    final = lax.fori_loop(lo, hi, body, init)
    return [acc for _, acc in final]


def _odd_attn_kernel(zq_ref, miscq_ref, kvb_ref, misck_ref, wc_ref, o_ref,
                     kc_ref, vc_ref, ki_ref, key_ref, dbias_ref, flag_ref, pos_ref, *, t_len, tq, tk, topk):
    f32, bf16, i32 = jnp.float32, jnp.bfloat16, jnp.int32
    nb = t_len // BLK
    qi_blk = pl.program_id(1)
    q0 = qi_blk * tq

    @pl.when(qi_blk == 0)
    def _():
        ck = kvb_ref[0, :, 0:LANES].astype(f32).reshape(nb, BLK, LANES)
        kc_ref[...] = jnp.sum(ck * wc_ref[0][None], axis=1).astype(bf16)
        cv = kvb_ref[0, :, LANES:2 * LANES].astype(f32).reshape(nb, BLK, LANES)
        vc = jnp.sum(cv * wc_ref[1][None], axis=1)
        for g in range(KV_GROUPS):
            half = vc[:, g * HEAD_DIM:(g + 1) * HEAD_DIM]
            vc_ref[g] = jnp.concatenate([half, half], axis=1).astype(bf16)
        ki_ref[...] = misck_ref[0, :, 0:IDX_DIM].astype(bf16)
        key_idx = lax.broadcasted_iota(i32, (t_len, LANES), 0)
        feat = lax.broadcasted_iota(i32, (t_len, LANES), 1)
        pos_ref[...] = jnp.where(feat == 0, key_idx // BLK, jnp.where(feat == 1, key_idx % BLK, 0)).astype(bf16)

    misc = miscq_ref[0]
    gates = jax.nn.sigmoid(misc[:, MISC_GATE:MISC_GATE + 3 * NSA_HEADS])
    lane = lax.broadcasted_iota(i32, (1, LANES), 1)
    low_half = lane < HEAD_DIM
    heads_per_group = NSA_HEADS // KV_GROUPS

    def q_head(base, h):
        return zq_ref[0, :, base + h * LANES: base + (h + 1) * LANES]

    def pair(even, odd):
        return jnp.where(low_half, even, odd)

    def normalised_pair(acc_t, r_even, g):
        halves = []
        for r in (r_even, r_even + 1):
            cols = slice(r * tq, (r + 1) * tq)
            out = acc_t[g * HEAD_DIM:(g + 1) * HEAD_DIM, cols]
            den = acc_t[(1 - g) * HEAD_DIM:(1 - g) * HEAD_DIM + 1, cols]
            halves.append(out * (1.0 / den))
        return jnp.concatenate(halves, axis=0).T

    def gate_pair(j, c):
        a = gates[:, (2 * j) * 3 + c:(2 * j) * 3 + c + 1]
        b = gates[:, (2 * j + 1) * 3 + c:(2 * j + 1) * 3 + c + 1]
        return jnp.where(low_half, a, b)

    tn = (((0,), (0,)), ((), ()))
    t_lanes = q0 + lax.broadcasted_iota(i32, (1, tq), 1)
    jb = lax.broadcasted_iota(i32, (nb, 1), 0)
    dist_c = t_lanes - (jb * BLK + (BLK - 1))
    mask_c = dist_c >= 0
    dist_cf = dist_c.astype(f32)
    cur = t_lanes // BLK
    forced = (jb == 0) | (jb == cur) | (jb == cur - 1)
    admissible = jb <= cur
    jb_full = lax.broadcasted_iota(i32, (nb, tq), 0)
    o_cmp, sel_bias, sel_any = [], [], []
    for g in range(KV_GROUPS):
        imp = jnp.zeros((nb, tq), f32)
        for r in range(heads_per_group):
            h = g * heads_per_group + r
            s = _dot_nt(kc_ref[...], q_head(O_Q, h)) - ALIBI[h] * dist_cf
            s = jnp.where(mask_c, s, NEG_BIG)
            m = jnp.max(s, axis=0, keepdims=True)
            e = jnp.where(mask_c, jnp.exp(s - m), 0.0)
            p = e * (1.0 / jnp.maximum(jnp.sum(e, axis=0, keepdims=True), 1e-30))
            imp = imp + p
            o_cmp.append(lax.dot_general(p.astype(bf16), vc_ref[g], tn, preferred_element_type=f32))
        imp = jnp.where(forced, FORCE_SCORE, imp)
        imp = jnp.where(admissible, imp, -jnp.inf)
        sel = jnp.zeros((nb, tq), f32)
        for _ in range(min(NSA_TOPN, nb)):
            m = jnp.max(imp, axis=0, keepdims=True)
            first = jnp.min(jnp.where(imp == m, jb_full, nb), axis=0, keepdims=True)
            hit = jb_full == first
            sel = jnp.where(hit & (m > -jnp.inf), 1.0, sel)
            imp = jnp.where(hit, -jnp.inf, imp)
        sel_bias.append(jnp.where(sel > 0.0, 0.0, NEG_BIG).astype(bf16))
        sel_any.append(jnp.max(sel, axis=1, keepdims=True))

    n_sel_tiles = (q0 + tq + tk - 1) // tk
    blocks_per_tile = tk // BLK

    for g in range(KV_GROUPS):
        for j in range(nb // blocks_per_tile):
            hit = jnp.max(sel_any[g][j * blocks_per_tile:(j + 1) * blocks_per_tile, :])
            flag_ref[g, j] = (hit > 0.0).astype(i32)

    def causal_bias(start, width):
        sp = start + lax.broadcasted_iota(i32, (width, 1), 0)
        return jnp.where(sp <= t_lanes, 0.0, NEG_BIG)

    def sel_bias_fn(g):
        def fn(kt):
            row = lax.broadcasted_iota(i32, (nb, tk), 0)
            col = lax.broadcasted_iota(i32, (nb, tk), 1)
            expand = jnp.where(row == kt * blocks_per_tile + col // BLK, 1.0, 0.0).astype(bf16)
            spread = lax.dot_general(expand, sel_bias[g], tn, preferred_element_type=f32)
            return spread + causal_bias(kt * tk, tk)
        return fn

    tkw = min(WINDOW + tq, t_len)
    win_start = jnp.clip(q0 - WINDOW, 0, t_len - tkw)

    def win_bias_fn(kt):
        dist = t_lanes - (win_start + lax.broadcasted_iota(i32, (tkw, 1), 0))
        return jnp.where((dist >= 0) & (dist < WINDOW), 0.0, NEG_BIG)

    qi_heads = [zq_ref[0, :, ZQ_QI + h * IDX_DIM: ZQ_QI + (h + 1) * IDX_DIM] for h in range(IDX_HEADS)]
    pick = jnp.where(lax.broadcasted_iota(i32, (8, LANES), 1) == MISC_WI + lax.broadcasted_iota(i32, (8, LANES), 0),
                     1.0, 0.0)
    wi_rows = lax.dot_general(pick, misc, (((1,), (1,)), ((), ())), preferred_element_type=f32,
                              precision=lax.Precision.HIGHEST)
    wi_scaled = wi_rows * ((IDX_DIM ** -0.5) * (IDX_HEADS ** -0.5))
    t_row = q0 + lax.broadcasted_iota(i32, (1, tq), 1)

    def fold8(x):
        x = x.reshape(x.shape[0] // 8, 8, tq)
        while x.shape[0] > 1:
            half = x.shape[0] // 2
            x = x[:half] + x[half:]
        return x[0]

    def index_tile(kt, carry):
        start = pl.multiple_of(kt * tk, tk)
        kik = ki_ref[pl.ds(start, tk), :]
        tot = jnp.zeros((tk, tq), f32)
        for h in range(IDX_HEADS):
            tot = tot + jnp.maximum(_dot_nt(kik, qi_heads[h]), 0.0) * wi_scaled[h:h + 1, :]
        sp = start + lax.broadcasted_iota(i32, (tk, 1), 0)
        key_ref[kt] = jnp.where(sp <= t_row, tot, -jnp.inf)
        return carry

    lax.fori_loop(0, n_sel_tiles, index_tile, 0)

    def as_float(pattern):
        bits = jnp.where(pattern < 0, pattern ^ jnp.int32(0x7FFFFFFF), pattern)
        return lax.bitcast_convert_type(bits, f32)

    def count_ge(trial):
        trial_f = as_float(trial)

        def body(kt, acc):
            for part in range(0, tk, COUNT_ROWS):
                acc = acc + fold8((key_ref[kt, part:part + COUNT_ROWS, :] >= trial_f).astype(i32))
            return acc
        acc = lax.fori_loop(0, n_sel_tiles, body, jnp.zeros((8, tq), i32))
        return jnp.sum(acc, axis=0, keepdims=True)

    c_pos = count_ge(jnp.zeros((1, tq), i32))
    v0 = jnp.where(c_pos >= topk, 0, INT_MIN).astype(i32)
    c0 = jnp.where(c_pos >= topk, c_pos, n_sel_tiles * tk)

    def bit_step(i, carry):
        v, cge = carry
        trial = v | lax.shift_left(jnp.int32(1), 30 - i)
        c = count_ge(trial)
        ok = c >= topk
        return jnp.where(ok, trial, v), jnp.where(ok, c, cge)

    v_thr, c_ge = lax.fori_loop(0, 31, bit_step, (v0, c0))
    v_f = jnp.where(t_row + 1 <= topk, -jnp.inf, as_float(v_thr))

    def write_plain(kt, carry):
        key = key_ref[kt]
        dbias_ref[kt] = jnp.where((key >= v_f) & (key > -jnp.inf), 0.0, NEG_BIG)
        return carry

    def plain_path():
        lax.fori_loop(0, n_sel_tiles, write_plain, 0)

    def tie_path():
        need = topk - count_ge(v_thr + 1)

        def count_tie_below(limit):
            def body(kt, acc):
                sp = kt * tk + lax.broadcasted_iota(i32, (tk, 1), 0)
                return acc + fold8(((key_ref[kt] == v_f) & (sp < limit)).astype(i32))
            acc = lax.fori_loop(0, n_sel_tiles, body, jnp.zeros((8, tq), i32))
            return jnp.sum(acc, axis=0, keepdims=True)

        n_bits = max(1, (t_len - 1).bit_length())

        def idx_step(i, w):
            trial = w | lax.shift_left(jnp.int32(1), n_bits - 1 - i)
            return jnp.where(count_tie_below(trial) < need, trial, w)

        w_last = lax.fori_loop(0, n_bits, idx_step, jnp.zeros((1, tq), i32))

        def write_tie(kt, carry):
            sp = kt * tk + lax.broadcasted_iota(i32, (tk, 1), 0)
            key = key_ref[kt]
            take = (key > v_f) | ((key == v_f) & (sp <= w_last))
            dbias_ref[kt] = jnp.where(take & (key > -jnp.inf), 0.0, NEG_BIG)
            return carry

        lax.fori_loop(0, n_sel_tiles, write_tie, 0)

    lax.cond(jnp.max(c_ge) > topk, tie_path, plain_path)

    def dsa_bias_fn(kt):
        return dbias_ref[kt]

    heads_of = [[g * heads_per_group + r for r in range(heads_per_group)] for g in range(KV_GROUPS)]
    slopes = [[ALIBI[h] for h in hs] for hs in heads_of]
    qn = [[q_head(O_Q, h) for h in hs] for hs in heads_of]
    qd = [[q_head(O_QD, h) for h in hs] for hs in heads_of]
    sel_skip = lambda kt: (flag_ref[0, kt] + flag_ref[1, kt]) == 0
    a_s = _flash_groups(qn, slopes, kvb_ref, pos_ref, 2, 3, tk, 0, n_sel_tiles,
                        [sel_bias_fn(g) for g in range(KV_GROUPS)], sel_skip)
    a_w = _flash_groups(qn, slopes, kvb_ref, pos_ref, 6, 7, tkw, 0, 1, [win_bias_fn], start_fn=lambda kt: win_start)
    a_d = _flash_groups(qd, slopes, kvb_ref, pos_ref, 4, 5, tk, 0, n_sel_tiles, [dsa_bias_fn])
    for g in range(KV_GROUPS):
        for jj in range(heads_per_group // 2):
            j = heads_of[g][2 * jj] // 2
            o_n = (gate_pair(j, 0) * pair(o_cmp[2 * j], o_cmp[2 * j + 1])
                   + gate_pair(j, 1) * normalised_pair(a_s[g], 2 * jj, g)
                   + gate_pair(j, 2) * normalised_pair(a_w[g], 2 * jj, g))
            o_ref[0, :, j * LANES:(j + 1) * LANES] = o_n.astype(o_ref.dtype)
            o_d = normalised_pair(a_d[g], 2 * jj, g)
            o_ref[0, :, HALF_W + j * LANES:HALF_W + (j + 1) * LANES] = o_d.astype(o_ref.dtype)


def _odd_attn_prompt(zq, misc, kvb, wc2, bsz, t_len, tq=256, tk=512):
    tk = min(tk, t_len)
    topk = min(DSA_TOPK_MAX, t_len // 4)
    nb = t_len // BLK
    n_kt = t_len // tk
    zq3 = zq.reshape(bsz, t_len, zq.shape[1])
    misc3 = misc.reshape(bsz, t_len, LANES)
    kvb3 = kvb.reshape(bsz, t_len, kvb.shape[1])
    kern = functools.partial(_odd_attn_kernel, t_len=t_len, tq=tq, tk=tk, topk=topk)
    out = pl.pallas_call(
        kern,
        grid=(bsz, t_len // tq),
        in_specs=[pl.BlockSpec((1, tq, zq.shape[1]), lambda b, i: (b, i, 0)),
                  pl.BlockSpec((1, tq, LANES), lambda b, i: (b, i, 0)),
                  pl.BlockSpec((1, t_len, kvb.shape[1]), lambda b, i: (b, 0, 0)),
                  pl.BlockSpec((1, t_len, LANES), lambda b, i: (b, 0, 0)),
                  pl.BlockSpec((2, BLK, LANES), lambda b, i: (0, 0, 0))],
        out_specs=pl.BlockSpec((1, tq, 2 * HALF_W), lambda b, i: (b, i, 0)),
        out_shape=jax.ShapeDtypeStruct((bsz, t_len, 2 * HALF_W), jnp.bfloat16),
        scratch_shapes=[pltpu.VMEM((nb, LANES), jnp.bfloat16),
                        pltpu.VMEM((KV_GROUPS, nb, LANES), jnp.bfloat16),
                        pltpu.VMEM((t_len, IDX_DIM), jnp.bfloat16),
                        pltpu.VMEM((n_kt, tk, tq), jnp.float32),
                        pltpu.VMEM((n_kt, tk, tq), jnp.float32),
                        pltpu.SMEM((KV_GROUPS, n_kt), jnp.int32),
                        pltpu.VMEM((t_len, LANES), jnp.bfloat16)],
        compiler_params=pltpu.CompilerParams(
            dimension_semantics=("arbitrary", "arbitrary"), vmem_limit_bytes=VMEM_LIMIT),
        name="odd_attn_prompt",
    )(zq3, misc3, kvb3, misc3, wc2)
    return out.reshape(bsz * t_len, 2 * HALF_W)


PAGE = 128
ISC_ROWS = 24


SAMPLE_SEQS = 2
_DONE = object()


def _odd_sample_kernel(pt_ref, *refs, n_pages, topk, nbp):
    del pt_ref
    seqs = [_odd_sample_one(u, *refs, n_pages=n_pages, topk=topk, nbp=nbp) for u in range(SAMPLE_SEQS)]
    while seqs:
        seqs = [g for g in seqs if next(g, _DONE) is not _DONE]


def _odd_sample_one(u, qn_ref, qd_ref, qi_ref, wi_ref, gate_ref, newb_ref, newki_ref, neww_ref,
                    win_ref, wc_ref, wct_ref, *rest, n_pages, topk, nbp):
    f32, bf16, i32 = jnp.float32, jnp.bfloat16, jnp.int32
    all_pages = SAMPLE_SEQS * n_pages
    kv_refs = rest[u * n_pages:(u + 1) * n_pages]
    idx_refs = rest[all_pages + u * n_pages:all_pages + (u + 1) * n_pages]
    o_ref, wout_ref, isc_all_ref = rest[2 * all_pages:]
    isc_ref = isc_all_ref.at[u]
    past = n_pages * PAGE
    nb = past // BLK + 1
    row8 = lax.broadcasted_iota(i32, (8, 1), 0)
    lane = lax.broadcasted_iota(i32, (1, LANES), 1)
    slope8 = jnp.zeros((8, 1), f32)
    for h in range(NSA_HEADS):
        slope8 = jnp.where(row8 == h, ALIBI[h], slope8)
    low_group = row8 < NSA_HEADS // KV_GROUPS

    def pick_half(x):
        return jnp.where(low_group, x[:, 0:HEAD_DIM], x[:, HEAD_DIM:2 * HEAD_DIM])

    qn8, qd8 = qn_ref[u], qd_ref[u]
    qi8, wi8 = qi_ref[u], wi_ref[u]
    newb = newb_ref[u]
    gates = jax.nn.sigmoid(gate_ref[u])

    def new_score(q8, k_row):
        return jnp.sum(q8.astype(f32) * k_row.astype(f32), axis=-1, keepdims=True) + slope8 * float(past)

    def feat(p, slot):
        return kv_refs[p][0, slot * LANES:(slot + 1) * LANES, :]

    s_sel, s_dsa = [], []
    key_row = lax.broadcasted_iota(i32, (PAGE, nbp), 0)
    blk_col = lax.broadcasted_iota(i32, (PAGE, nbp), 1)
    kct = jnp.zeros((LANES, nbp), f32)
    vct = jnp.zeros((LANES, nbp), f32)
    for p in range(n_pages):
        pool = jnp.where(blk_col == 2 * p + key_row // BLK, 1.0, 0.0).astype(bf16)
        kct = kct + jnp.dot((feat(p, 0) * wct_ref[0]).astype(bf16), pool, preferred_element_type=f32)
        vct = vct + jnp.dot((feat(p, 1) * wct_ref[1]).astype(bf16), pool, preferred_element_type=f32)
        pos = (p * PAGE + lane).astype(f32)
        s_sel.append(jnp.dot(qn8, feat(p, 2).astype(bf16), preferred_element_type=f32) + slope8 * pos)
        s_dsa.append(jnp.dot(qd8, feat(p, 4).astype(bf16), preferred_element_type=f32) + slope8 * pos)
        ii = jnp.dot(qi8, idx_refs[p][0].astype(bf16), preferred_element_type=f32) * (IDX_DIM ** -0.5)
        isc_ref[p:p + 1, :] = jnp.sum(jnp.maximum(ii, 0.0) * wi8, axis=0, keepdims=True) * (IDX_HEADS ** -0.5)
    first_row = row8 == 0
    place = jnp.where(first_row & (lax.broadcasted_iota(i32, (8, nbp), 1) == nb - 1), 1.0, 0.0)
    tn = (((0,), (0,)), ((), ()))
    new_k = jnp.where(first_row, newb[:, 0:LANES].astype(f32) * wc_ref[0, 0:1, :], 0.0)
    new_v = jnp.where(first_row, newb[:, LANES:2 * LANES].astype(f32) * wc_ref[1, 0:1, :], 0.0)
    kct = kct + lax.dot_general(new_k.astype(bf16), place.astype(bf16), tn, preferred_element_type=f32)
    vct = vct + lax.dot_general(new_v.astype(bf16), place.astype(bf16), tn, preferred_element_type=f32)
    ii_new = jnp.sum(qi8.astype(f32) * newki_ref[u].astype(bf16).astype(f32), axis=-1, keepdims=True) * (IDX_DIM ** -0.5)
    isc_new = jnp.sum(jnp.maximum(ii_new, 0.0) * wi8, axis=0, keepdims=True) * (IDX_HEADS ** -0.5)
    isc_ref[n_pages:n_pages + 1, :] = jnp.where(lane == 0, isc_new, -jnp.inf)
    isc_ref[n_pages + 1:ISC_ROWS, :] = jnp.full((ISC_ROWS - n_pages - 1, LANES), -jnp.inf, f32)

    jb = lax.broadcasted_iota(i32, (1, nbp), 1)
    dist_c = past - (jb * BLK + (BLK - 1))
    mask_c = (dist_c >= 0) & (jb < nb)
    s_c = jnp.dot(qn8, kct.astype(bf16), preferred_element_type=f32) - slope8 * dist_c.astype(f32)
    s_c = jnp.where(mask_c, s_c, NEG_BIG)
    m_c = jnp.max(s_c, axis=-1, keepdims=True)
    e_c = jnp.where(mask_c, jnp.exp(s_c - m_c), 0.0)
    p_c = e_c * (1.0 / jnp.maximum(jnp.sum(e_c, axis=-1, keepdims=True), 1e-30))
    o_c = pick_half(_dot_nt(p_c.astype(bf16), vct.astype(bf16)))
    imp_lo = jnp.sum(jnp.where(low_group, p_c, 0.0), axis=0, keepdims=True)
    imp_hi = jnp.sum(jnp.where(low_group, 0.0, p_c), axis=0, keepdims=True)
    imp = jnp.where(low_group, imp_lo, imp_hi)
    cur = past // BLK
    imp = jnp.where((jb == 0) | (jb == cur) | (jb == cur - 1), FORCE_SCORE, imp)
    imp = jnp.where(jb <= cur, imp, -jnp.inf)
    rank = jnp.zeros((8, nbp), i32)
    for j in range(nb):
        other = imp[:, j:j + 1]
        rank = rank + ((other > imp) | ((other == imp) & (j < jb))).astype(i32)
    sel = jnp.where((rank < min(NSA_TOPN, nb)) & (imp > -jnp.inf), 1.0, 0.0)

    krow = lax.broadcasted_iota(i32, (ISC_ROWS, LANES), 0)
    kidx = krow * LANES + lax.broadcasted_iota(i32, (ISC_ROWS, LANES), 1)
    key = jnp.where(kidx <= past, isc_ref[...], -jnp.inf)

    def total(x):
        folded = jnp.sum(x.reshape(ISC_ROWS // 8, 8, LANES), axis=0)
        return jnp.sum(jnp.sum(folded, axis=1, keepdims=True), axis=0, keepdims=True)

    def as_float(pattern):
        bits = jnp.where(pattern < 0, pattern ^ jnp.int32(0x7FFFFFFF), pattern)
        return lax.bitcast_convert_type(bits, f32)

    def count_ge(trial):
        return total((key >= as_float(trial)).astype(i32))

    c_pos = count_ge(jnp.zeros((1, 1), i32))
    v_thr = jnp.where(c_pos >= topk, 0, INT_MIN).astype(i32)
    for shift, width in ((27, 4), (23, 4), (19, 4), (15, 4), (11, 4), (7, 4), (3, 4), (0, 3)):
        digit = jnp.zeros((1, 1), i32)
        for j in range(1, 1 << width):
            ok = count_ge(v_thr | jnp.int32(j << shift)) >= topk
            digit = digit + ok.astype(i32)
        v_thr = v_thr | lax.shift_left(digit, jnp.int32(shift))
    v_f = as_float(v_thr) if past + 1 > topk else jnp.full((1, 1), -jnp.inf, f32)
    valid = key > -jnp.inf

    def tie_mask():
        need = topk - count_ge(v_thr + 1)
        tie = key == v_f
        w_last = jnp.zeros((1, 1), i32)
        for shift in (8, 4, 0):
            digit = jnp.zeros((1, 1), i32)
            for j in range(1, 16):
                below = total((tie & (kidx < (w_last | jnp.int32(j << shift)))).astype(i32))
                digit = digit + (below < need).astype(i32)
            w_last = w_last | lax.shift_left(digit, jnp.int32(shift))
        take = (key > v_f) | (tie & (kidx <= w_last))
        return jnp.where(take & valid, 1.0, 0.0)

    def plain_mask():
        return jnp.where((key >= v_f) & valid, 1.0, 0.0)

    has_surplus_tie = count_ge(v_thr)[0, 0] > topk
    yield
    dsel = lax.cond(has_surplus_tie, tie_mask, plain_mask)
    yield

    def attend(tiles, masks, s_new, new_ok, v_tile, v_new):
        tiles = [jnp.where(mk, t, NEG_BIG) for t, mk in zip(tiles, masks)]
        s_new = jnp.where(new_ok, s_new, NEG_BIG)
        m = s_new
        for t in tiles:
            m = jnp.maximum(m, jnp.max(t, axis=-1, keepdims=True))
        l = jnp.exp(s_new - m)
        acc = l * v_new.astype(f32)
        for j, t in enumerate(tiles):
            e = jnp.exp(t - m)
            l = l + jnp.sum(e, axis=-1, keepdims=True)
            acc = acc + _dot_nt(e.astype(bf16), v_tile(j))
        return pick_half(acc * (1.0 / l))

    sel_masks = [jnp.where(lane < BLK, sel[:, 2 * p:2 * p + 1], sel[:, 2 * p + 1:2 * p + 2]) > 0.0
                 for p in range(n_pages)]
    o_s = attend(s_sel, sel_masks, new_score(qn8, newb[:, 2 * LANES:3 * LANES]), sel[:, nb - 1:nb] > 0.0,
                 lambda p: feat(p, 3).astype(bf16), newb[:, 3 * LANES:4 * LANES])
    dsa_masks = [dsel[p:p + 1, :] > 0.0 for p in range(n_pages)]
    o_d = attend(s_dsa, dsa_masks, new_score(qd8, newb[:, 4 * LANES:5 * LANES]), dsel[n_pages:n_pages + 1, 0:1] > 0.0,
                 lambda p: feat(p, 5).astype(bf16), newb[:, 5 * LANES:6 * LANES])
    n_win = win_ref.shape[2]
    wlane = lax.broadcasted_iota(i32, (1, n_win), 1)
    wpos = past - n_win + wlane
    s_w = jnp.dot(qn8, win_ref[u, 0:LANES, :].astype(bf16), preferred_element_type=f32) + slope8 * wpos.astype(f32)
    w_ok = (past - wpos < WINDOW) & (wpos >= 0)
    o_w = attend([s_w], [w_ok], new_score(qn8, newb[:, 6 * LANES:7 * LANES]), True,
                 lambda p: win_ref[u, LANES:2 * LANES, :].astype(bf16), newb[:, 7 * LANES:8 * LANES])
    o_ref[u, 0:NSA_HEADS, :] = gates[:, 0:1] * o_c + gates[:, 1:2] * o_s + gates[:, 2:3] * o_w
    o_ref[u, NSA_HEADS:2 * NSA_HEADS, :] = o_d
    last = jnp.where(first_row & (lax.broadcasted_iota(i32, (8, n_win), 1) == n_win - 1), 1.0, 0.0)
    new_w = jnp.where(first_row, neww_ref[u], 0.0)
    w_hi = new_w.astype(bf16).astype(f32)
    w_mid = (new_w - w_hi).astype(bf16).astype(f32)
    w_lo = new_w - w_hi - w_mid
    pieces = jnp.concatenate([w_hi, w_mid, w_lo, jnp.zeros_like(new_w)], axis=0).astype(bf16)
    placed = lax.dot_general(pieces, jnp.concatenate([last] * 4, axis=0).astype(bf16), tn,
                             preferred_element_type=f32)
    wout_ref[u] = jnp.where(wlane == n_win - 1, placed, pltpu.roll(win_ref[u], n_win - 1, 1))


def _odd_attn_sample(zq, misc, kvb, win_new, cache_kv, cache_idx, cache_win, page_table, w_cmp):
    f32 = jnp.float32
    s_n, n_pages = page_table.shape
    n_pool = cache_kv.shape[0]
    n_win = cache_win.shape[1]
    past = n_pages * PAGE
    topk = min(DSA_TOPK_MAX, (past + 1) // 4)
    nbp = _round_up(past // BLK + 1, 8)
    wc2 = jnp.concatenate([w_cmp, w_cmp], axis=-1)
    wct = jnp.tile(jnp.swapaxes(w_cmp, 1, 2), (1, KV_GROUPS, PAGE // BLK))
    qn = zq[:, O_Q:O_QD].reshape(s_n, NSA_HEADS, LANES)
    qd = zq[:, O_QD:ZQ_QI].reshape(s_n, DSA_HEADS, LANES)
    qi = jnp.pad(zq[:, ZQ_QI:ZQ_QI + LANES].reshape(s_n, IDX_HEADS, IDX_DIM), ((0, 0), (0, 8 - IDX_HEADS), (0, 0)))
    wi = jnp.pad(misc[:, MISC_WI:MISC_WI + IDX_HEADS].reshape(s_n, IDX_HEADS, 1), ((0, 0), (0, 8 - IDX_HEADS), (0, 0)))
    gts = misc[:, MISC_GATE:MISC_GATE + 3 * NSA_HEADS].reshape(s_n, NSA_HEADS, 3)
    kv_pages = jnp.transpose(cache_kv, (0, 2, 3, 4, 1)).reshape(n_pool, KV_SLOTS * LANES, PAGE)
    idx_pages = jnp.transpose(cache_idx, (0, 2, 1))
    win_t = jnp.transpose(cache_win, (0, 2, 3, 4, 1)).reshape(s_n, 2 * LANES, n_win)
    n_u = SAMPLE_SEQS
    per_seq = lambda shape: pl.BlockSpec((n_u,) + shape, lambda b, pt: (b, 0, 0))
    page_spec = lambda rows, u, p: pl.BlockSpec((1, rows, PAGE), lambda b, pt, u=u, p=p: (pt[b * n_u + u, p], 0, 0))
    seq_pages = [(u, p) for u in range(n_u) for p in range(n_pages)]
    in_specs = ([per_seq((NSA_HEADS, LANES)), per_seq((DSA_HEADS, LANES)), per_seq((8, IDX_DIM)),
                 per_seq((8, 1)), per_seq((NSA_HEADS, 3)), per_seq((1, 8 * LANES)), per_seq((1, IDX_DIM)),
                 per_seq((1, 2 * LANES)), per_seq((2 * LANES, n_win)),
                 pl.BlockSpec((2, BLK, LANES), lambda b, pt: (0, 0, 0)),
                 pl.BlockSpec((2, LANES, PAGE), lambda b, pt: (0, 0, 0))]
                + [page_spec(KV_SLOTS * LANES, u, p) for u, p in seq_pages]
                + [page_spec(IDX_DIM, u, p) for u, p in seq_pages])
    kern = functools.partial(_odd_sample_kernel, n_pages=n_pages, topk=topk, nbp=nbp)
    o, win_out = pl.pallas_call(
        kern,
        grid_spec=pltpu.PrefetchScalarGridSpec(
            num_scalar_prefetch=1,
            grid=(s_n // n_u,),
            in_specs=in_specs,
            out_specs=[per_seq((2 * NSA_HEADS, HEAD_DIM)), per_seq((2 * LANES, n_win))],
            scratch_shapes=[pltpu.VMEM((n_u, ISC_ROWS, LANES), f32)]),
        out_shape=[jax.ShapeDtypeStruct((s_n, 2 * NSA_HEADS, HEAD_DIM), f32),
                   jax.ShapeDtypeStruct((s_n, 2 * LANES, n_win), f32)],
        compiler_params=pltpu.CompilerParams(
            dimension_semantics=("arbitrary",), vmem_limit_bytes=VMEM_LIMIT),
        name="odd_attn_sample",
    )(page_table, qn, qd, qi, wi, gts, kvb.reshape(s_n, 1, 8 * LANES), misc[:, 0:IDX_DIM].reshape(s_n, 1, IDX_DIM),
      win_new.reshape(s_n, 1, 2 * LANES), win_t, wc2, wct,
      *([kv_pages] * len(seq_pages)), *([idx_pages] * len(seq_pages)))
    win_out = jnp.transpose(win_out.reshape(s_n, 2, KV_GROUPS, HEAD_DIM, n_win), (0, 4, 1, 2, 3))
    return o.reshape(s_n, 2 * HALF_W), win_out


def _split(z, sizes):
    cuts = [int(c) for c in np.cumsum(sizes)[:-1]]
    return jnp.split(z, cuts, axis=-1)


def _pad_cols(w, n):
    return jnp.pad(w, ((0, 0), (0, n - w.shape[1])))


def kernel(x_prompt, x_sample, state_conv, state_C, state_n, state_m, cache_kv, cache_idx, cache_win, page_table,
           w_in_e, gate_b_e, conv_w, conv_b, w_out_e, w_in_o, w_cmp, w_out_o, w_up, w_down, ln_g, ln_b):
    bf16 = jnp.bfloat16
    bp, tp, d = x_prompt.shape
    dbs = x_sample.shape[0]
    keep = cache_win.shape[2]
    yp = x_prompt.reshape(bp * tp, d)
    ys = x_sample.reshape(dbs, d)
    tm_p, tm_s = 512, dbs
    outs = {}
    for layer in range(DEPTH):
        i = layer // 2
        if layer % 2 == 0:
            w_in = _pad_cols(w_in_e[i], E_PACKED).astype(bf16)
            w_out = w_out_e[i].astype(bf16)
            ya, qkv, og, gts, tails = _even_proj(yp, w_in, conv_w[i], conv_b[i], tm_p, tp)
            yb, c_fin, n_fin, m_fin = _mlstm_prompt(qkv, og, gts, gate_b_e[i], bp, tp)
            cvp = tails.reshape(bp, tp // tm_p, CONV_TAIL, HALF_W)[:, -1, CONV_TAIL - (CONV_W - 1):]
            ya_s, yb_s, cvs, c_s, n_s, m_s = _even_sample(_proj(ys, w_in, tm_s), gate_b_e[i], conv_w[i], conv_b[i],
                                                          state_conv[i], state_C[i], state_n[i], state_m[i])
            outs['conv'] = (cvp[None], cvs[None])
            outs['c'] = (c_fin[None], c_s[None])
            outs['n'] = (n_fin[None], n_s[None])
            outs['m'] = (m_fin[:, :, 0][None], m_s[:, :, 0][None])
            yp = _outproj2_ln(ya, yb.reshape(bp * tp, HALF_W), w_out, yp, ln_g[layer, 0], ln_b[layer, 0], tm_p)
            ys = _outproj2_ln(ya_s, yb_s, w_out, ys, ln_g[layer, 0], ln_b[layer, 0], tm_s)
        else:
            w_out = w_out_o[i].astype(bf16)
            w_packed = _pack_w_in_o(w_in_o[i]).astype(bf16)
            wc2 = jnp.concatenate([w_cmp[i], w_cmp[i]], axis=-1)
            rows_first = lambda kvt: jnp.transpose(
                kvt.reshape(kvt.shape[0], KV_SLOTS, KV_GROUPS, HEAD_DIM, kvt.shape[2]), (0, 4, 1, 2, 3))
            zq, kvt_p, kvb, winp, miscp = _odd_proj(yp, w_packed, tm_p, tp)
            mp = _odd_attn_prompt(zq, miscp, kvb, wc2, bp, tp)
            kvp = rows_first(kvt_p)
            ixp = miscp[:, :IDX_DIM].reshape(bp, tp, IDX_DIM)
            wnp = winp.reshape(bp, tp, 2, KV_GROUPS, HEAD_DIM)[:, tp - keep:]
            zq_s, kvt_s, kvb_s, win_s, misc_s = _odd_proj(ys, w_packed, tm_s, dbs)
            ms, wns = _odd_attn_sample(zq_s, misc_s, kvb_s, win_s, cache_kv[i], cache_idx[i], cache_win[i],
                                       page_table, w_cmp[i])
            kvs = rows_first(kvt_s).reshape(dbs, 1, KV_SLOTS, KV_GROUPS, HEAD_DIM)
            ixs = misc_s[:, :IDX_DIM].reshape(dbs, 1, IDX_DIM)
            outs['kv'] = (kvp[None], kvs[None])
            outs['idx'] = (ixp[None], ixs[None])
            outs['win'] = (wnp[None], wns[None])
            yp = _outproj_ln(mp, w_out, yp, ln_g[layer, 0], ln_b[layer, 0], tm_p)
            ys = _outproj_ln(ms, w_out, ys, ln_g[layer, 0], ln_b[layer, 0], tm_s)
        wu, wd = w_up[layer].astype(bf16), w_down[layer].astype(bf16)
        yp = _mlp_ln(yp, wu, wd, ln_g[layer, 1], ln_b[layer, 1], 2 * tm_p, 1024)
        ys = _mlp_ln(ys, wu, wd, ln_g[layer, 1], ln_b[layer, 1], tm_s, 1024)
    return (yp.reshape(bp, tp, d), ys.reshape(dbs, 1, d),
            outs['conv'][0], outs['conv'][1], outs['c'][0], outs['c'][1],
            outs['n'][0], outs['n'][1], outs['m'][0], outs['m'][1],
            outs['kv'][0], outs['kv'][1], outs['idx'][0], outs['idx'][1],
            outs['win'][0], outs['win'][1])
```

```python
import functools

import jax
import jax.numpy as jnp
import numpy as np
from jax import lax
from jax.experimental import pallas as pl
from jax.experimental.pallas import tpu as pltpu

D_MODEL = 1024
DEPTH = 2
HALF_W = 512
D_FF = 4096
CONV_W = 3
ML_HEADS = 4
ML_DIM = 128
HEAD_DIM = 64
KV_GROUPS = 2
NSA_HEADS = 8
DSA_HEADS = 8
BLK = 64
NSA_TOPN = 8
WINDOW = 256
IDX_HEADS = 4
IDX_DIM = 32
DSA_TOPK_MAX = 256
KV_SLOTS = 6
FORCE_SCORE = 1e4
ALPHA = (2.0 * DEPTH) ** 0.25
LN_EPS = 1e-5
O_SIZES = (512, 256, 256, 256, 24, 512, 256, 128, 32, 4)
LANES = 128
VMEM_LIMIT = 48 * 1024 * 1024


def _round_up(n, m):
    return -(-n // m) * m


def _proj_kernel(x_ref, w_ref, o_ref):
    o_ref[...] = jnp.dot(x_ref[...].astype(jnp.bfloat16), w_ref[...],
                         preferred_element_type=jnp.float32)


def _proj(x, w_bf16, tm):
    m, k = x.shape
    n = w_bf16.shape[1]
    tn = n
    for cand in (1024, 768, 512, 256, 128):
        if n % cand == 0:
            tn = cand
            break
    return pl.pallas_call(
        _proj_kernel,
        grid=(n // tn, m // tm),
        in_specs=[pl.BlockSpec((tm, k), lambda j, i: (i, 0)),
                  pl.BlockSpec((k, tn), lambda j, i: (0, j))],
        out_specs=pl.BlockSpec((tm, tn), lambda j, i: (i, j)),
        out_shape=jax.ShapeDtypeStruct((m, n), jnp.float32),
        compiler_params=pltpu.CompilerParams(
            dimension_semantics=("arbitrary", "arbitrary"), vmem_limit_bytes=VMEM_LIMIT),
        name="proj",
    )(x, w_bf16)


def _layer_norm_rows(v, g, b):
    mu = jnp.mean(v, axis=-1, keepdims=True)
    d = v - mu
    var = jnp.mean(d * d, axis=-1, keepdims=True)
    return d * lax.rsqrt(var + LN_EPS) * g + b


def _outproj_ln_kernel(y_ref, w_ref, x_ref, g_ref, b_ref, o_ref):
    mix = jnp.dot(y_ref[...].astype(jnp.bfloat16), w_ref[...], preferred_element_type=jnp.float32)
    o_ref[...] = _layer_norm_rows(ALPHA * x_ref[...] + mix, g_ref[...], b_ref[...])


def _outproj_ln(y, w_bf16, x, g, b, tm):
    m, d = x.shape
    k = y.shape[1]
    return pl.pallas_call(
        _outproj_ln_kernel,
        grid=(m // tm,),
        in_specs=[pl.BlockSpec((tm, k), lambda i: (i, 0)),
                  pl.BlockSpec((k, d), lambda i: (0, 0)),
                  pl.BlockSpec((tm, d), lambda i: (i, 0)),
                  pl.BlockSpec((1, d), lambda i: (0, 0)),
                  pl.BlockSpec((1, d), lambda i: (0, 0))],
        out_specs=pl.BlockSpec((tm, d), lambda i: (i, 0)),
        out_shape=jax.ShapeDtypeStruct((m, d), jnp.float32),
        compiler_params=pltpu.CompilerParams(
            dimension_semantics=("arbitrary",), vmem_limit_bytes=VMEM_LIMIT),
        name="outproj_ln",
    )(y, w_bf16, x, g.reshape(1, d), b.reshape(1, d))


def _mlp_ln_kernel(x_ref, wu_ref, wd_ref, g_ref, b_ref, o_ref, acc_ref):
    f = pl.program_id(1)

    @pl.when(f == 0)
    def _():
        acc_ref[...] = jnp.zeros_like(acc_ref)

    up = jnp.dot(x_ref[...].astype(jnp.bfloat16), wu_ref[...], preferred_element_type=jnp.float32)
    act = jnp.square(jnp.maximum(up, 0.0))
    acc_ref[...] += jnp.dot(act.astype(jnp.bfloat16), wd_ref[...], preferred_element_type=jnp.float32)

    @pl.when(f == pl.num_programs(1) - 1)
    def _():
        o_ref[...] = _layer_norm_rows(ALPHA * x_ref[...] + acc_ref[...], g_ref[...], b_ref[...])


def _mlp_ln(x, wu_bf16, wd_bf16, g, b, tm, tf):
    m, d = x.shape
    ff = wu_bf16.shape[1]
    return pl.pallas_call(
        _mlp_ln_kernel,
        grid=(m // tm, ff // tf),
        in_specs=[pl.BlockSpec((tm, d), lambda i, f: (i, 0)),
                  pl.BlockSpec((d, tf), lambda i, f: (0, f)),
                  pl.BlockSpec((tf, d), lambda i, f: (f, 0)),
                  pl.BlockSpec((1, d), lambda i, f: (0, 0)),
                  pl.BlockSpec((1, d), lambda i, f: (0, 0))],
        out_specs=pl.BlockSpec((tm, d), lambda i, f: (i, 0)),
        out_shape=jax.ShapeDtypeStruct((m, d), jnp.float32),
        scratch_shapes=[pltpu.VMEM((tm, d), jnp.float32)],
        compiler_params=pltpu.CompilerParams(
            dimension_semantics=("arbitrary", "arbitrary"), vmem_limit_bytes=VMEM_LIMIT),
        name="mlp_ln",
    )(x, wu_bf16, wd_bf16, g.reshape(1, d), b.reshape(1, d))


E_QKV = 3 * HALF_W
E_OG = 6 * HALF_W
E_GATE = 7 * HALF_W
E_PACKED = 7 * HALF_W + LANES
CONV_TAIL = 8


def _even_proj_kernel(x_ref, w_ref, cw_ref, cb_ref, ya_ref, qkv_ref, og_ref, gate_ref, tail_ref, carry_ref,
                      *, tiles_per_seq):
    f32, bf16 = jnp.float32, jnp.bfloat16
    tm = x_ref.shape[0]

    @pl.when(pl.program_id(0) % tiles_per_seq == 0)
    def _():
        carry_ref[...] = jnp.zeros_like(carry_ref)

    z = jnp.dot(x_ref[...].astype(bf16), w_ref[...], preferred_element_type=f32)
    u = z[:, 2 * HALF_W:3 * HALF_W] * z[:, 0:HALF_W]
    prev = carry_ref[...]
    row = lax.broadcasted_iota(jnp.int32, (tm, 1), 0)
    conv = cb_ref[...] + cw_ref[CONV_W - 1:CONV_W, :] * u
    for back in range(1, CONV_W):
        shifted = pltpu.roll(u, back, 0)
        for r in range(back):
            shifted = jnp.where(row == r, prev[CONV_TAIL - back + r:CONV_TAIL - back + r + 1, :], shifted)
        conv = conv + cw_ref[CONV_W - 1 - back:CONV_W - back, :] * shifted
    ya_ref[...] = (z[:, HALF_W:2 * HALF_W] * conv).astype(bf16)
    tail = u[tm - CONV_TAIL:]
    carry_ref[...] = tail
    tail_ref[0] = tail
    qkv_ref[:, 0:HALF_W] = z[:, E_QKV:E_QKV + HALF_W].astype(bf16)
    qkv_ref[:, HALF_W:2 * HALF_W] = (z[:, E_QKV + HALF_W:E_QKV + 2 * HALF_W] * (ML_DIM ** -0.5)).astype(bf16)
    qkv_ref[:, 2 * HALF_W:3 * HALF_W] = z[:, E_QKV + 2 * HALF_W:E_OG].astype(bf16)
    og_ref[...] = z[:, E_OG:E_GATE]
    gate_ref[...] = z[:, E_GATE:E_PACKED]


def _even_proj(x, w_bf16, conv_w, conv_b, tm, seq_len):
    m, k = x.shape
    widths = (HALF_W, 3 * HALF_W, HALF_W, LANES)
    dtypes = (jnp.bfloat16, jnp.bfloat16, jnp.float32, jnp.float32)
    kern = functools.partial(_even_proj_kernel, tiles_per_seq=seq_len // tm)
    return pl.pallas_call(
        kern,
        grid=(m // tm,),
        in_specs=[pl.BlockSpec((tm, k), lambda i: (i, 0)),
                  pl.BlockSpec((k, E_PACKED), lambda i: (0, 0)),
                  pl.BlockSpec((CONV_W, HALF_W), lambda i: (0, 0)),
                  pl.BlockSpec((1, HALF_W), lambda i: (0, 0))],
        out_specs=[pl.BlockSpec((tm, n), lambda i: (i, 0)) for n in widths]
        + [pl.BlockSpec((1, CONV_TAIL, HALF_W), lambda i: (i, 0, 0))],
        out_shape=[jax.ShapeDtypeStruct((m, n), dt) for n, dt in zip(widths, dtypes)]
        + [jax.ShapeDtypeStruct((m // tm, CONV_TAIL, HALF_W), jnp.float32)],
        scratch_shapes=[pltpu.VMEM((CONV_TAIL, HALF_W), jnp.float32)],
        compiler_params=pltpu.CompilerParams(
            dimension_semantics=("arbitrary",), vmem_limit_bytes=VMEM_LIMIT),
        name="even_proj",
    )(x, w_bf16, conv_w, conv_b.reshape(1, HALF_W))


def _mlstm_kernel(qkv_ref, og_ref, gate_ref, gb_ref, yb_ref, c_ref, n_ref, m_ref):
    f32, bf16 = jnp.float32, jnp.bfloat16
    L = qkv_ref.shape[1]

    @pl.when(pl.program_id(1) == 0)
    def _():
        c_ref[...] = jnp.zeros_like(c_ref)
        n_ref[...] = jnp.zeros_like(n_ref)
        m_ref[...] = jnp.zeros_like(m_ref)

    pre = gate_ref[0] + gb_ref[...]
    lf = jax.nn.log_sigmoid(pre)
    row = lax.broadcasted_iota(jnp.int32, (L, L), 0)
    col = lax.broadcasted_iota(jnp.int32, (L, L), 1)
    causal = col <= row
    tri = jnp.where(causal, 1.0, 0.0).astype(f32)
    b_all = jnp.dot(tri, lf, preferred_element_type=f32, precision=lax.Precision.HIGHEST)
    pre_t = pre.T
    b_t = b_all.T
    for h in range(ML_HEADS):
        q = qkv_ref[0, :, h * ML_DIM:(h + 1) * ML_DIM]
        k = qkv_ref[0, :, HALF_W + h * ML_DIM:HALF_W + (h + 1) * ML_DIM]
        v = qkv_ref[0, :, 2 * HALF_W + h * ML_DIM:2 * HALF_W + (h + 1) * ML_DIM]
        ig_col = pre[:, h:h + 1]
        b_col = b_all[:, ML_HEADS + h:ML_HEADS + h + 1]
        a_row = pre_t[h:h + 1, :] - b_t[ML_HEADS + h:ML_HEADS + h + 1, :]
        m_st = m_ref[0, h:h + 1, 0:1]
        c_st = c_ref[0, h]
        n_st = n_ref[0, h:h + 1, :]
        dmat = jnp.where(causal, b_col + a_row, -jnp.inf)
        inter = b_col + m_st
        m_t = jnp.maximum(inter, jnp.max(dmat, axis=-1, keepdims=True))
        w_intra = jnp.exp(dmat - m_t)
        w_inter = jnp.exp(inter - m_t)
        s = _dot_nt(q, k) * w_intra
        num = jnp.dot(s.astype(bf16), v, preferred_element_type=f32) + w_inter * _dot_nt(q, c_st.astype(bf16))
        den = jnp.sum(s, axis=-1, keepdims=True) + w_inter * jnp.sum(q.astype(f32) * n_st, axis=-1, keepdims=True)
        hs = num * (1.0 / jnp.maximum(jnp.abs(den), jnp.exp(-m_t)))
        m_new = m_t[L - 1:L, :]
        b_last = b_col[L - 1:L, :]
        w_state = jnp.exp(b_last - b_col + ig_col - m_new)
        decay = jnp.exp(b_last + m_st - m_new)
        vw = (v.astype(f32) * w_state).astype(bf16)
        c_ref[0, h] = decay * c_st + lax.dot_general(vw, k, (((0,), (0,)), ((), ())), preferred_element_type=f32)
        n_ref[0, h:h + 1, :] = decay * n_st + jnp.sum(k.astype(f32) * w_state, axis=0, keepdims=True)
        m_ref[0, h:h + 1, :] = jnp.broadcast_to(m_new, (1, LANES))
        og = og_ref[0, :, h * ML_DIM:(h + 1) * ML_DIM]
        yb_ref[0, :, h * ML_DIM:(h + 1) * ML_DIM] = (jax.nn.sigmoid(og) * hs).astype(bf16)


def _mlstm_prompt(qkv, og, gates, gate_b, bsz, t_len, chunk=128):
    gb = jnp.pad(gate_b, (0, LANES - gate_b.shape[0])).reshape(1, LANES)
    return pl.pallas_call(
        _mlstm_kernel,
        grid=(bsz, t_len // chunk),
        in_specs=[pl.BlockSpec((1, chunk, 3 * HALF_W), lambda b, c: (b, c, 0)),
                  pl.BlockSpec((1, chunk, HALF_W), lambda b, c: (b, c, 0)),
                  pl.BlockSpec((1, chunk, LANES), lambda b, c: (b, c, 0)),
                  pl.BlockSpec((1, LANES), lambda b, c: (0, 0))],
        out_specs=[pl.BlockSpec((1, chunk, HALF_W), lambda b, c: (b, c, 0)),
                   pl.BlockSpec((1, ML_HEADS, ML_DIM, ML_DIM), lambda b, c: (b, 0, 0, 0)),
                   pl.BlockSpec((1, ML_HEADS, ML_DIM), lambda b, c: (b, 0, 0)),
                   pl.BlockSpec((1, ML_HEADS, LANES), lambda b, c: (b, 0, 0))],
        out_shape=[jax.ShapeDtypeStruct((bsz, t_len, HALF_W), jnp.bfloat16),
                   jax.ShapeDtypeStruct((bsz, ML_HEADS, ML_DIM, ML_DIM), jnp.float32),
                   jax.ShapeDtypeStruct((bsz, ML_HEADS, ML_DIM), jnp.float32),
                   jax.ShapeDtypeStruct((bsz, ML_HEADS, LANES), jnp.float32)],
        compiler_params=pltpu.CompilerParams(
            dimension_semantics=("arbitrary", "arbitrary"), vmem_limit_bytes=VMEM_LIMIT),
        name="mlstm_prompt",
    )(qkv.reshape(bsz, t_len, 3 * HALF_W), og.reshape(bsz, t_len, HALF_W),
      gates.reshape(bsz, t_len, LANES), gb)


def _outproj2_ln_kernel(ya_ref, yb_ref, w_ref, x_ref, g_ref, b_ref, o_ref):
    mix = jnp.dot(ya_ref[...], w_ref[0:HALF_W, :], preferred_element_type=jnp.float32)
    mix = mix + jnp.dot(yb_ref[...], w_ref[HALF_W:, :], preferred_element_type=jnp.float32)
    o_ref[...] = _layer_norm_rows(ALPHA * x_ref[...] + mix, g_ref[...], b_ref[...])


def _outproj2_ln(ya, yb, w_bf16, x, g, b, tm):
    m, d = x.shape
    return pl.pallas_call(
        _outproj2_ln_kernel,
        grid=(m // tm,),
        in_specs=[pl.BlockSpec((tm, HALF_W), lambda i: (i, 0)),
                  pl.BlockSpec((tm, HALF_W), lambda i: (i, 0)),
                  pl.BlockSpec((2 * HALF_W, d), lambda i: (0, 0)),
                  pl.BlockSpec((tm, d), lambda i: (i, 0)),
                  pl.BlockSpec((1, d), lambda i: (0, 0)),
                  pl.BlockSpec((1, d), lambda i: (0, 0))],
        out_specs=pl.BlockSpec((tm, d), lambda i: (i, 0)),
        out_shape=jax.ShapeDtypeStruct((m, d), jnp.float32),
        compiler_params=pltpu.CompilerParams(
            dimension_semantics=("arbitrary",), vmem_limit_bytes=VMEM_LIMIT),
        name="outproj2_ln",
    )(ya, yb, w_bf16, x, g.reshape(1, d), b.reshape(1, d))


SAMPLE_ROWS = 8


def _even_sample_kernel(h_ref, bg_ref, cg_ref, q_ref, k_ref, v_ref, og_ref, gate_ref, gb_ref, cw_ref, cb_ref,
                        conv_ref, c_ref, n_ref, m_ref, ya_ref, yb_ref, conv_out_ref, c_out_ref, n_out_ref, m_out_ref):
    f32, bf16 = jnp.float32, jnp.bfloat16
    rows = h_ref.shape[0]
    row = lax.broadcasted_iota(jnp.int32, (rows, 1), 0)
    tn = (((0,), (0,)), ((), ()))
    u = cg_ref[...] * h_ref[...]
    prev0, prev1 = conv_ref[:, 0, :], conv_ref[:, 1, :]
    conv = cb_ref[...] + cw_ref[0:1, :] * prev0 + cw_ref[1:2, :] * prev1 + cw_ref[2:3, :] * u
    ya_ref[...] = (bg_ref[...] * conv).astype(bf16)
    conv_out_ref[:, 0, :] = prev1
    conv_out_ref[:, 1, :] = u
    pre = gate_ref[...] + gb_ref[...]
    lf_all = jax.nn.log_sigmoid(pre)
    for h in range(ML_HEADS):
        cols = slice(h * ML_DIM, (h + 1) * ML_DIM)
        q = q_ref[:, cols]
        k = k_ref[:, cols] * (ML_DIM ** -0.5)
        v = v_ref[:, cols]
        ig = pre[:, h:h + 1]
        lf = lf_all[:, ML_HEADS + h:ML_HEADS + h + 1]
        m_st = m_ref[:, h:h + 1]
        n_st = n_ref[:, h, :]
        inter = lf + m_st
        m_t = jnp.maximum(inter, ig)
        w_intra = jnp.exp(ig - m_t)
        w_inter = jnp.exp(inter - m_t)
        qb, kb = q.astype(bf16), k.astype(bf16)
        s = jnp.sum(qb.astype(f32) * kb.astype(f32), axis=-1, keepdims=True) * w_intra
        cq = jnp.zeros((rows, ML_DIM), f32)
        for b in range(rows):
            c_b = c_ref[b, h]
            cq = jnp.where(row == b, _dot_nt(qb, c_b.astype(bf16)), cq)
            vw = jnp.where(row == b, v * w_intra, 0.0)
            outer = lax.dot_general(vw, k, tn, preferred_element_type=f32, precision=lax.Precision.HIGHEST)
            c_out_ref[b, h] = w_inter[b:b + 1, :] * c_b + outer
        num = s * v + w_inter * cq
        den = s + w_inter * jnp.sum(n_st * q, axis=-1, keepdims=True)
        hs = num * (1.0 / jnp.maximum(jnp.abs(den), jnp.exp(-m_t)))
        n_out_ref[:, h, :] = w_inter * n_st + w_intra * k
        m_out_ref[:, h, :] = jnp.broadcast_to(m_t, (rows, LANES))
        yb_ref[:, cols] = (jax.nn.sigmoid(og_ref[:, cols]) * hs).astype(bf16)


def _even_sample(z, gate_b, conv_w, conv_b, state_conv, state_c, state_n, state_m):
    s_n = z.shape[0]
    f32 = jnp.float32
    r = SAMPLE_ROWS
    gb = jnp.pad(gate_b, (0, LANES - gate_b.shape[0])).reshape(1, LANES)
    col = lambda j: pl.BlockSpec((r, HALF_W), lambda i, j=j: (i, j))
    in_specs = [col(j) for j in range(7)] + [
        pl.BlockSpec((r, LANES), lambda i: (i, E_GATE // LANES)),
        pl.BlockSpec((1, LANES), lambda i: (0, 0)),
        pl.BlockSpec((CONV_W, HALF_W), lambda i: (0, 0)),
        pl.BlockSpec((1, HALF_W), lambda i: (0, 0)),
        pl.BlockSpec((r, CONV_W - 1, HALF_W), lambda i: (i, 0, 0)),
        pl.BlockSpec((r, ML_HEADS, ML_DIM, ML_DIM), lambda i: (i, 0, 0, 0)),
        pl.BlockSpec((r, ML_HEADS, ML_DIM), lambda i: (i, 0, 0)),
        pl.BlockSpec((r, ML_HEADS), lambda i: (i, 0))]
    out_specs = [pl.BlockSpec((r, HALF_W), lambda i: (i, 0)),
                 pl.BlockSpec((r, HALF_W), lambda i: (i, 0)),
                 pl.BlockSpec((r, CONV_W - 1, HALF_W), lambda i: (i, 0, 0)),
                 pl.BlockSpec((r, ML_HEADS, ML_DIM, ML_DIM), lambda i: (i, 0, 0, 0)),
                 pl.BlockSpec((r, ML_HEADS, ML_DIM), lambda i: (i, 0, 0)),
                 pl.BlockSpec((r, ML_HEADS, LANES), lambda i: (i, 0, 0))]
    out_shape = [jax.ShapeDtypeStruct((s_n, HALF_W), jnp.bfloat16),
                 jax.ShapeDtypeStruct((s_n, HALF_W), jnp.bfloat16),
                 jax.ShapeDtypeStruct((s_n, CONV_W - 1, HALF_W), f32),
                 jax.ShapeDtypeStruct((s_n, ML_HEADS, ML_DIM, ML_DIM), f32),
                 jax.ShapeDtypeStruct((s_n, ML_HEADS, ML_DIM), f32),
                 jax.ShapeDtypeStruct((s_n, ML_HEADS, LANES), f32)]
    return pl.pallas_call(
        _even_sample_kernel,
        grid=(s_n // r,),
        in_specs=in_specs,
        out_specs=out_specs,
        out_shape=out_shape,
        compiler_params=pltpu.CompilerParams(
            dimension_semantics=("arbitrary",), vmem_limit_bytes=VMEM_LIMIT),
        name="even_sample",
    )(z, z, z, z, z, z, z, z, gb, conv_w, conv_b.reshape(1, HALF_W), state_conv, state_c, state_n, state_m)


O_Q = 0
O_QD = NSA_HEADS * LANES
O_KV = 2048
O_WIN = 2816
O_QI = 3072
O_MISC = 3200
O_PACKED = 3328
ZQ_QI = 2048
MISC_WI = IDX_DIM
MISC_GATE = IDX_DIM + IDX_HEADS
COUNT_ROWS = 128
NEG_BIG = -(2.0 ** 100)
INT_MIN = -(2 ** 31)
ALIBI = tuple(float(2.0 ** (-8.0 * (h + 1) / NSA_HEADS)) for h in range(NSA_HEADS))


def _pack_w_in_o(w):
    qn, kvc, kvs, kvw, gates, qd, kvd, qi, ki, wi = _split(w, O_SIZES)
    scale = HEAD_DIM ** -0.5
    rows = w.shape[0]

    def spread(q):
        q = (q * scale).reshape(rows, NSA_HEADS, HEAD_DIM)
        z = jnp.zeros_like(q)
        low = (jnp.arange(NSA_HEADS) < NSA_HEADS // KV_GROUPS)[None, :, None]
        return jnp.concatenate([jnp.where(low, q, z), jnp.where(low, z, q)], axis=-1).reshape(rows, NSA_HEADS * LANES)

    pad = jnp.zeros((rows, O_PACKED - O_MISC - IDX_DIM - IDX_HEADS - 3 * NSA_HEADS), w.dtype)
    return jnp.concatenate([spread(qn), spread(qd), kvc, kvs, kvd, kvw, qi, ki, wi, gates, pad], axis=1)


def _odd_proj_kernel(x_ref, w_ref, wkvt_ref, zq_ref, kvt_ref, kvb_ref, win_ref, misc_ref):
    xb = x_ref[...].astype(jnp.bfloat16)
    z = jnp.dot(xb, w_ref[...], preferred_element_type=jnp.float32)
    zq_ref[:, 0:O_KV] = z[:, 0:O_KV].astype(jnp.bfloat16)
    zq_ref[:, ZQ_QI:ZQ_QI + LANES] = z[:, O_QI:O_MISC].astype(jnp.bfloat16)
    kvt_ref[0] = _dot_nt(wkvt_ref[...], xb)
    kvb_ref[...] = z[:, O_KV:O_QI].astype(jnp.bfloat16)
    win_ref[...] = z[:, O_WIN:O_QI]
    misc_ref[...] = z[:, O_MISC:O_PACKED]


def _odd_proj(x, w_packed_bf16, tm, seq_len):
    m, k = x.shape
    n_kv = O_WIN - O_KV
    widths = (O_KV + LANES, O_QI - O_KV, O_QI - O_WIN, LANES)
    dtypes = (jnp.bfloat16, jnp.bfloat16, jnp.float32, jnp.float32)
    row_spec = lambda n: pl.BlockSpec((tm, n), lambda i: (i, 0))
    tiles = seq_len // tm
    out_specs = [row_spec(widths[0]), pl.BlockSpec((1, n_kv, tm), lambda i: (i // tiles, 0, i % tiles))]
    out_specs += [row_spec(n) for n in widths[1:]]
    out_shape = [jax.ShapeDtypeStruct((m, widths[0]), dtypes[0]),
                 jax.ShapeDtypeStruct((m // seq_len, n_kv, seq_len), jnp.float32)]
    out_shape += [jax.ShapeDtypeStruct((m, n), dt) for n, dt in zip(widths[1:], dtypes[1:])]
    return pl.pallas_call(
        _odd_proj_kernel,
        grid=(m // tm,),
        in_specs=[pl.BlockSpec((tm, k), lambda i: (i, 0)),
                  pl.BlockSpec((k, O_PACKED), lambda i: (0, 0)),
                  pl.BlockSpec((n_kv, k), lambda i: (0, 0))],
        out_specs=out_specs,
        out_shape=out_shape,
        compiler_params=pltpu.CompilerParams(
            dimension_semantics=("arbitrary",), vmem_limit_bytes=VMEM_LIMIT),
        name="odd_proj",
    )(x, w_packed_bf16, w_packed_bf16[:, O_KV:O_WIN].T)


def _dot_nt(a, b):
    return lax.dot_general(a, b, (((1,), (1,)), ((), ())), preferred_element_type=jnp.float32)


def _flash_groups(q_by_group, slopes_by_group, kv_ref, pos_ref, k_blk, v_blk, tk, lo, hi, bias_fns, skip_fn=None,
                  start_fn=None, group_of=None):
    f32, bf16 = jnp.float32, jnp.bfloat16
    if start_fn is None:
        start_fn = lambda kt: kt * tk
    n_groups = len(q_by_group)
    tq = q_by_group[0][0].shape[0]
    n_heads = len(q_by_group[0])
    lane = lax.broadcasted_iota(jnp.int32, (1, LANES), 1)
    if group_of is None:
        group_of = list(range(n_groups))
    own_half = [(lane // HEAD_DIM) == group_of[c] for c in range(n_groups)]
    feat_lane = lax.broadcasted_iota(jnp.int32, (tq, LANES), 1)
    q_all = []
    for q_heads, slopes in zip(q_by_group, slopes_by_group):
        slope_feat = [jnp.where(feat_lane == 0, BLK * sl, jnp.where(feat_lane == 1, sl, 0.0)).astype(bf16)
                      for sl in slopes]
        q_all.append(jnp.concatenate([jnp.concatenate(q_heads, axis=0), jnp.concatenate(slope_feat, axis=0)], axis=1))
    tn = (((0,), (0,)), ((), ()))

    def tile(kt, carry):
        start = pl.multiple_of(start_fn(kt), LANES)
        rows = pl.ds(start, tk)
        k = kv_ref[0, rows, k_blk * LANES:(k_blk + 1) * LANES]
        v = kv_ref[0, rows, v_blk * LANES:(v_blk + 1) * LANES]
        k_pos = jnp.concatenate([k, pos_ref[rows, :]], axis=1)
        scores = [_dot_nt(k_pos, q) for q in q_all]
        fns = bias_fns if len(bias_fns) > 1 else bias_fns * n_groups
        tiles = {}
        for fn in fns:
            if id(fn) not in tiles:
                tiles[id(fn)] = fn(kt)
        biases = [tiles[id(fn)] for fn in fns]
        new_m, probs, alphas = [], [], []
        for g in range(n_groups):
            m = carry[g][0]
            s = scores[g] + jnp.concatenate([biases[g]] * n_heads, axis=1)
            m_new = jnp.maximum(m, jnp.max(s, axis=0, keepdims=True))
            probs.append(jnp.exp(s - m_new).astype(bf16))
            alphas.append(jnp.exp(m - m_new))
            new_m.append(m_new)
        out = []
        for g in range(n_groups):
            v_g = jnp.where(own_half[g], v, jnp.ones_like(v))
            pv = lax.dot_general(v_g, probs[g], tn, preferred_element_type=f32)
            out.append((new_m[g], alphas[g] * carry[g][1] + pv))
        return tuple(out)

    def body(kt, carry):
        if skip_fn is None:
            return tile(kt, carry)
        return lax.cond(skip_fn(kt), lambda c: c, functools.partial(tile, kt), carry)

    init = tuple((jnp.full((1, n_heads * tq), -jnp.inf, f32), jnp.zeros((LANES, n_heads * tq), f32))
                 for _ in range(n_groups))
    final = lax.fori_loop(lo, hi, body, init)
    return [acc for _, acc in final]


def _odd_attn_kernel(zq_ref, miscq_ref, kvb_ref, misck_ref, wc_ref, o_ref,
                     kc_ref, vc_ref, ki_ref, key_ref, dbias_ref, flag_ref, pos_ref, *, t_len, tq, tk, topk):
    f32, bf16, i32 = jnp.float32, jnp.bfloat16, jnp.int32
    nb = t_len // BLK
    qi_blk = pl.program_id(1)
    q0 = qi_blk * tq

    @pl.when(qi_blk == 0)
    def _():
        ck = kvb_ref[0, :, 0:LANES].astype(f32).reshape(nb, BLK, LANES)
        kc_ref[...] = jnp.sum(ck * wc_ref[0][None], axis=1).astype(bf16)
        cv = kvb_ref[0, :, LANES:2 * LANES].astype(f32).reshape(nb, BLK, LANES)
        vc = jnp.sum(cv * wc_ref[1][None], axis=1)
        for g in range(KV_GROUPS):
            half = vc[:, g * HEAD_DIM:(g + 1) * HEAD_DIM]
            vc_ref[g] = jnp.concatenate([half, half], axis=1).astype(bf16)
        ki_ref[...] = misck_ref[0, :, 0:IDX_DIM].astype(bf16)
        key_idx = lax.broadcasted_iota(i32, (t_len, LANES), 0)
        feat = lax.broadcasted_iota(i32, (t_len, LANES), 1)
        pos_ref[...] = jnp.where(feat == 0, key_idx // BLK, jnp.where(feat == 1, key_idx % BLK, 0)).astype(bf16)

    misc = miscq_ref[0]
    gates = jax.nn.sigmoid(misc[:, MISC_GATE:MISC_GATE + 3 * NSA_HEADS])
    lane = lax.broadcasted_iota(i32, (1, LANES), 1)
    low_half = lane < HEAD_DIM
    heads_per_group = NSA_HEADS // KV_GROUPS

    def q_head(base, h):
        return zq_ref[0, :, base + h * LANES: base + (h + 1) * LANES]

    def pair(even, odd):
        return jnp.where(low_half, even, odd)

    def normalised_pair(acc_t, r_even, g):
        halves = []
        for r in (r_even, r_even + 1):
            cols = slice(r * tq, (r + 1) * tq)
            out = acc_t[g * HEAD_DIM:(g + 1) * HEAD_DIM, cols]
            den = acc_t[(1 - g) * HEAD_DIM:(1 - g) * HEAD_DIM + 1, cols]
            halves.append(out * (1.0 / den))
        return jnp.concatenate(halves, axis=0).T

    def gate_pair(j, c):
        a = gates[:, (2 * j) * 3 + c:(2 * j) * 3 + c + 1]
        b = gates[:, (2 * j + 1) * 3 + c:(2 * j + 1) * 3 + c + 1]
        return jnp.where(low_half, a, b)

    tn = (((0,), (0,)), ((), ()))
    t_lanes = q0 + lax.broadcasted_iota(i32, (1, tq), 1)
    jb = lax.broadcasted_iota(i32, (nb, 1), 0)
    dist_c = t_lanes - (jb * BLK + (BLK - 1))
    mask_c = dist_c >= 0
    dist_cf = dist_c.astype(f32)
    cur = t_lanes // BLK
    forced = (jb == 0) | (jb == cur) | (jb == cur - 1)
    admissible = jb <= cur
    jb_full = lax.broadcasted_iota(i32, (nb, tq), 0)
    o_cmp, sel_bias, sel_any = [], [], []
    for g in range(KV_GROUPS):
        imp = jnp.zeros((nb, tq), f32)
        for r in range(heads_per_group):
            h = g * heads_per_group + r
            s = _dot_nt(kc_ref[...], q_head(O_Q, h)) - ALIBI[h] * dist_cf
            s = jnp.where(mask_c, s, NEG_BIG)
            m = jnp.max(s, axis=0, keepdims=True)
            e = jnp.where(mask_c, jnp.exp(s - m), 0.0)
            p = e * (1.0 / jnp.maximum(jnp.sum(e, axis=0, keepdims=True), 1e-30))
            imp = imp + p
            o_cmp.append(lax.dot_general(p.astype(bf16), vc_ref[g], tn, preferred_element_type=f32))
        imp = jnp.where(forced, FORCE_SCORE, imp)
        imp = jnp.where(admissible, imp, -jnp.inf)
        sel = jnp.zeros((nb, tq), f32)
        for _ in range(min(NSA_TOPN, nb)):
            m = jnp.max(imp, axis=0, keepdims=True)
            first = jnp.min(jnp.where(imp == m, jb_full, nb), axis=0, keepdims=True)
            hit = jb_full == first
            sel = jnp.where(hit & (m > -jnp.inf), 1.0, sel)
            imp = jnp.where(hit, -jnp.inf, imp)
        sel_bias.append(jnp.where(sel > 0.0, 0.0, NEG_BIG).astype(bf16))
        sel_any.append(jnp.max(sel, axis=1, keepdims=True))

    n_sel_tiles = (q0 + tq + tk - 1) // tk
    blocks_per_tile = tk // BLK

    for g in range(KV_GROUPS):
        for j in range(nb // blocks_per_tile):
            hit = jnp.max(sel_any[g][j * blocks_per_tile:(j + 1) * blocks_per_tile, :])
            flag_ref[g, j] = (hit > 0.0).astype(i32)

    def causal_bias(start, width):
        sp = start + lax.broadcasted_iota(i32, (width, 1), 0)
        return jnp.where(sp <= t_lanes, 0.0, NEG_BIG)

    def sel_bias_fn(g):
        def fn(kt):
            row = lax.broadcasted_iota(i32, (nb, tk), 0)
            col = lax.broadcasted_iota(i32, (nb, tk), 1)
            expand = jnp.where(row == kt * blocks_per_tile + col // BLK, 1.0, 0.0).astype(bf16)
            spread = lax.dot_general(expand, sel_bias[g], tn, preferred_element_type=f32)
            return spread + causal_bias(kt * tk, tk)
        return fn

    tkw = min(WINDOW + tq, t_len)
    win_start = jnp.clip(q0 - WINDOW, 0, t_len - tkw)

    def win_bias_fn(kt):
        dist = t_lanes - (win_start + lax.broadcasted_iota(i32, (tkw, 1), 0))
        return jnp.where((dist >= 0) & (dist < WINDOW), 0.0, NEG_BIG)

    qi_heads = [zq_ref[0, :, ZQ_QI + h * IDX_DIM: ZQ_QI + (h + 1) * IDX_DIM] for h in range(IDX_HEADS)]
    pick = jnp.where(lax.broadcasted_iota(i32, (8, LANES), 1) == MISC_WI + lax.broadcasted_iota(i32, (8, LANES), 0),
                     1.0, 0.0)
    wi_rows = lax.dot_general(pick, misc, (((1,), (1,)), ((), ())), preferred_element_type=f32,
                              precision=lax.Precision.HIGHEST)
    wi_scaled = wi_rows * ((IDX_DIM ** -0.5) * (IDX_HEADS ** -0.5))
    t_row = q0 + lax.broadcasted_iota(i32, (1, tq), 1)

    def fold8(x):
        x = x.reshape(x.shape[0] // 8, 8, tq)
        while x.shape[0] > 1:
            half = x.shape[0] // 2
            x = x[:half] + x[half:]
        return x[0]

    def index_tile(kt, carry):
        start = pl.multiple_of(kt * tk, tk)
        kik = ki_ref[pl.ds(start, tk), :]
        tot = jnp.zeros((tk, tq), f32)
        for h in range(IDX_HEADS):
            tot = tot + jnp.maximum(_dot_nt(kik, qi_heads[h]), 0.0) * wi_scaled[h:h + 1, :]
        sp = start + lax.broadcasted_iota(i32, (tk, 1), 0)
        key_ref[kt] = jnp.where(sp <= t_row, tot, -jnp.inf)
        return carry

    lax.fori_loop(0, n_sel_tiles, index_tile, 0)

    def as_float(pattern):
        bits = jnp.where(pattern < 0, pattern ^ jnp.int32(0x7FFFFFFF), pattern)
        return lax.bitcast_convert_type(bits, f32)

    def count_ge(trial):
        trial_f = as_float(trial)

        def body(kt, acc):
            for part in range(0, tk, COUNT_ROWS):
                acc = acc + fold8((key_ref[kt, part:part + COUNT_ROWS, :] >= trial_f).astype(i32))
            return acc
        acc = lax.fori_loop(0, n_sel_tiles, body, jnp.zeros((8, tq), i32))
        return jnp.sum(acc, axis=0, keepdims=True)

    c_pos = count_ge(jnp.zeros((1, tq), i32))
    v0 = jnp.where(c_pos >= topk, 0, INT_MIN).astype(i32)
    c0 = jnp.where(c_pos >= topk, c_pos, n_sel_tiles * tk)

    def bit_step(i, carry):
        v, cge = carry
        trial = v | lax.shift_left(jnp.int32(1), 30 - i)
        c = count_ge(trial)
        ok = c >= topk
        return jnp.where(ok, trial, v), jnp.where(ok, c, cge)

    v_thr, c_ge = lax.fori_loop(0, 31, bit_step, (v0, c0))
    v_f = jnp.where(t_row + 1 <= topk, -jnp.inf, as_float(v_thr))

    def write_plain(kt, carry):
        key = key_ref[kt]
        dbias_ref[kt] = jnp.where((key >= v_f) & (key > -jnp.inf), 0.0, NEG_BIG)
        return carry

    def plain_path():
        lax.fori_loop(0, n_sel_tiles, write_plain, 0)

    def tie_path():
        need = topk - count_ge(v_thr + 1)

        def count_tie_below(limit):
            def body(kt, acc):
                sp = kt * tk + lax.broadcasted_iota(i32, (tk, 1), 0)
                return acc + fold8(((key_ref[kt] == v_f) & (sp < limit)).astype(i32))
            acc = lax.fori_loop(0, n_sel_tiles, body, jnp.zeros((8, tq), i32))
            return jnp.sum(acc, axis=0, keepdims=True)

        n_bits = max(1, (t_len - 1).bit_length())

        def idx_step(i, w):
            trial = w | lax.shift_left(jnp.int32(1), n_bits - 1 - i)
            return jnp.where(count_tie_below(trial) < need, trial, w)

        w_last = lax.fori_loop(0, n_bits, idx_step, jnp.zeros((1, tq), i32))

        def write_tie(kt, carry):
            sp = kt * tk + lax.broadcasted_iota(i32, (tk, 1), 0)
            key = key_ref[kt]
            take = (key > v_f) | ((key == v_f) & (sp <= w_last))
            dbias_ref[kt] = jnp.where(take & (key > -jnp.inf), 0.0, NEG_BIG)
            return carry

        lax.fori_loop(0, n_sel_tiles, write_tie, 0)

    lax.cond(jnp.max(c_ge) > topk, tie_path, plain_path)

    def dsa_bias_fn(kt):
        return dbias_ref[kt]

    n_pairs = NSA_HEADS // 2
    heads_of = [[2 * j, 2 * j + 1] for j in range(n_pairs)]
    group_of = [hs[0] // heads_per_group for hs in heads_of]
    slopes = [[ALIBI[h] for h in hs] for hs in heads_of]
    qn = [[q_head(O_Q, h) for h in hs] for hs in heads_of]
    qd = [[q_head(O_QD, h) for h in hs] for hs in heads_of]
    sel_skip = lambda kt: (flag_ref[0, kt] + flag_ref[1, kt]) == 0
    sel_fns = [sel_bias_fn(g) for g in range(KV_GROUPS)]
    a_s = _flash_groups(qn, slopes, kvb_ref, pos_ref, 2, 3, tk, 0, n_sel_tiles,
                        [sel_fns[g] for g in group_of], sel_skip, group_of=group_of)
    a_w = _flash_groups(qn, slopes, kvb_ref, pos_ref, 6, 7, tkw, 0, 1, [win_bias_fn], start_fn=lambda kt: win_start,
                        group_of=group_of)
    a_d = _flash_groups(qd, slopes, kvb_ref, pos_ref, 4, 5, tk, 0, n_sel_tiles, [dsa_bias_fn], group_of=group_of)
    for j in range(n_pairs):
        g = group_of[j]
        o_n = (gate_pair(j, 0) * pair(o_cmp[2 * j], o_cmp[2 * j + 1])
               + gate_pair(j, 1) * normalised_pair(a_s[j], 0, g)
               + gate_pair(j, 2) * normalised_pair(a_w[j], 0, g))
        o_ref[0, :, j * LANES:(j + 1) * LANES] = o_n.astype(o_ref.dtype)
        o_d = normalised_pair(a_d[j], 0, g)
        o_ref[0, :, HALF_W + j * LANES:HALF_W + (j + 1) * LANES] = o_d.astype(o_ref.dtype)


def _odd_attn_prompt(zq, misc, kvb, wc2, bsz, t_len, tq=256, tk=512):
    tk = min(tk, t_len)
    topk = min(DSA_TOPK_MAX, t_len // 4)
    nb = t_len // BLK
    n_kt = t_len // tk
    zq3 = zq.reshape(bsz, t_len, zq.shape[1])
    misc3 = misc.reshape(bsz, t_len, LANES)
    kvb3 = kvb.reshape(bsz, t_len, kvb.shape[1])
    kern = functools.partial(_odd_attn_kernel, t_len=t_len, tq=tq, tk=tk, topk=topk)
    out = pl.pallas_call(
        kern,
        grid=(bsz, t_len // tq),
        in_specs=[pl.BlockSpec((1, tq, zq.shape[1]), lambda b, i: (b, i, 0)),
                  pl.BlockSpec((1, tq, LANES), lambda b, i: (b, i, 0)),
                  pl.BlockSpec((1, t_len, kvb.shape[1]), lambda b, i: (b, 0, 0)),
                  pl.BlockSpec((1, t_len, LANES), lambda b, i: (b, 0, 0)),
                  pl.BlockSpec((2, BLK, LANES), lambda b, i: (0, 0, 0))],
        out_specs=pl.BlockSpec((1, tq, 2 * HALF_W), lambda b, i: (b, i, 0)),
        out_shape=jax.ShapeDtypeStruct((bsz, t_len, 2 * HALF_W), jnp.bfloat16),
        scratch_shapes=[pltpu.VMEM((nb, LANES), jnp.bfloat16),
                        pltpu.VMEM((KV_GROUPS, nb, LANES), jnp.bfloat16),
                        pltpu.VMEM((t_len, IDX_DIM), jnp.bfloat16),
                        pltpu.VMEM((n_kt, tk, tq), jnp.float32),
                        pltpu.VMEM((n_kt, tk, tq), jnp.float32),
                        pltpu.SMEM((KV_GROUPS, n_kt), jnp.int32),
                        pltpu.VMEM((t_len, LANES), jnp.bfloat16)],
        compiler_params=pltpu.CompilerParams(
            dimension_semantics=("arbitrary", "arbitrary"), vmem_limit_bytes=VMEM_LIMIT),
        name="odd_attn_prompt",
    )(zq3, misc3, kvb3, misc3, wc2)
    return out.reshape(bsz * t_len, 2 * HALF_W)


PAGE = 128
ISC_ROWS = 24


SAMPLE_SEQS = 2
_DONE = object()


def _odd_sample_kernel(pt_ref, *refs, n_pages, topk, nbp):
    del pt_ref
    seqs = [_odd_sample_one(u, *refs, n_pages=n_pages, topk=topk, nbp=nbp) for u in range(SAMPLE_SEQS)]
    while seqs:
        seqs = [g for g in seqs if next(g, _DONE) is not _DONE]


def _odd_sample_one(u, qn_ref, qd_ref, qi_ref, wi_ref, gate_ref, newb_ref, newki_ref, neww_ref,
                    win_ref, wc_ref, wct_ref, *rest, n_pages, topk, nbp):
    f32, bf16, i32 = jnp.float32, jnp.bfloat16, jnp.int32
    all_pages = SAMPLE_SEQS * n_pages
    kv_refs = rest[u * n_pages:(u + 1) * n_pages]
    idx_refs = rest[all_pages + u * n_pages:all_pages + (u + 1) * n_pages]
    o_ref, wout_ref, isc_all_ref = rest[2 * all_pages:]
    isc_ref = isc_all_ref.at[u]
    past = n_pages * PAGE
    nb = past // BLK + 1
    row8 = lax.broadcasted_iota(i32, (8, 1), 0)
    lane = lax.broadcasted_iota(i32, (1, LANES), 1)
    slope8 = jnp.zeros((8, 1), f32)
    for h in range(NSA_HEADS):
        slope8 = jnp.where(row8 == h, ALIBI[h], slope8)
    low_group = row8 < NSA_HEADS // KV_GROUPS

    def pick_half(x):
        return jnp.where(low_group, x[:, 0:HEAD_DIM], x[:, HEAD_DIM:2 * HEAD_DIM])

    qn8, qd8 = qn_ref[u], qd_ref[u]
    qi8, wi8 = qi_ref[u], wi_ref[u]
    newb = newb_ref[u]
    gates = jax.nn.sigmoid(gate_ref[u])

    def new_score(q8, k_row):
        return jnp.sum(q8.astype(f32) * k_row.astype(f32), axis=-1, keepdims=True) + slope8 * float(past)

    def feat(p, slot):
        return kv_refs[p][0, slot * LANES:(slot + 1) * LANES, :]

    s_sel, s_dsa = [], []
    key_row = lax.broadcasted_iota(i32, (PAGE, nbp), 0)
    blk_col = lax.broadcasted_iota(i32, (PAGE, nbp), 1)
    kct = jnp.zeros((LANES, nbp), f32)
    vct = jnp.zeros((LANES, nbp), f32)
    for p in range(n_pages):
        pool = jnp.where(blk_col == 2 * p + key_row // BLK, 1.0, 0.0).astype(bf16)
        kct = kct + jnp.dot((feat(p, 0) * wct_ref[0]).astype(bf16), pool, preferred_element_type=f32)
        vct = vct + jnp.dot((feat(p, 1) * wct_ref[1]).astype(bf16), pool, preferred_element_type=f32)
        pos = (p * PAGE + lane).astype(f32)
        s_sel.append(jnp.dot(qn8, feat(p, 2).astype(bf16), preferred_element_type=f32) + slope8 * pos)
        s_dsa.append(jnp.dot(qd8, feat(p, 4).astype(bf16), preferred_element_type=f32) + slope8 * pos)
        ii = jnp.dot(qi8, idx_refs[p][0].astype(bf16), preferred_element_type=f32) * (IDX_DIM ** -0.5)
        isc_ref[p:p + 1, :] = jnp.sum(jnp.maximum(ii, 0.0) * wi8, axis=0, keepdims=True) * (IDX_HEADS ** -0.5)
    first_row = row8 == 0
    place = jnp.where(first_row & (lax.broadcasted_iota(i32, (8, nbp), 1) == nb - 1), 1.0, 0.0)
    tn = (((0,), (0,)), ((), ()))
    new_k = jnp.where(first_row, newb[:, 0:LANES].astype(f32) * wc_ref[0, 0:1, :], 0.0)
    new_v = jnp.where(first_row, newb[:, LANES:2 * LANES].astype(f32) * wc_ref[1, 0:1, :], 0.0)
    kct = kct + lax.dot_general(new_k.astype(bf16), place.astype(bf16), tn, preferred_element_type=f32)
    vct = vct + lax.dot_general(new_v.astype(bf16), place.astype(bf16), tn, preferred_element_type=f32)
    ii_new = jnp.sum(qi8.astype(f32) * newki_ref[u].astype(bf16).astype(f32), axis=-1, keepdims=True) * (IDX_DIM ** -0.5)
    isc_new = jnp.sum(jnp.maximum(ii_new, 0.0) * wi8, axis=0, keepdims=True) * (IDX_HEADS ** -0.5)
    isc_ref[n_pages:n_pages + 1, :] = jnp.where(lane == 0, isc_new, -jnp.inf)
    isc_ref[n_pages + 1:ISC_ROWS, :] = jnp.full((ISC_ROWS - n_pages - 1, LANES), -jnp.inf, f32)

    jb = lax.broadcasted_iota(i32, (1, nbp), 1)
    dist_c = past - (jb * BLK + (BLK - 1))
    mask_c = (dist_c >= 0) & (jb < nb)
    s_c = jnp.dot(qn8, kct.astype(bf16), preferred_element_type=f32) - slope8 * dist_c.astype(f32)
    s_c = jnp.where(mask_c, s_c, NEG_BIG)
    m_c = jnp.max(s_c, axis=-1, keepdims=True)
    e_c = jnp.where(mask_c, jnp.exp(s_c - m_c), 0.0)
    p_c = e_c * (1.0 / jnp.maximum(jnp.sum(e_c, axis=-1, keepdims=True), 1e-30))
    o_c = pick_half(_dot_nt(p_c.astype(bf16), vct.astype(bf16)))
    imp_lo = jnp.sum(jnp.where(low_group, p_c, 0.0), axis=0, keepdims=True)
    imp_hi = jnp.sum(jnp.where(low_group, 0.0, p_c), axis=0, keepdims=True)
    imp = jnp.where(low_group, imp_lo, imp_hi)
    cur = past // BLK
    imp = jnp.where((jb == 0) | (jb == cur) | (jb == cur - 1), FORCE_SCORE, imp)
    imp = jnp.where(jb <= cur, imp, -jnp.inf)
    rank = jnp.zeros((8, nbp), i32)
    for j in range(nb):
        other = imp[:, j:j + 1]
        rank = rank + ((other > imp) | ((other == imp) & (j < jb))).astype(i32)
    sel = jnp.where((rank < min(NSA_TOPN, nb)) & (imp > -jnp.inf), 1.0, 0.0)

    krow = lax.broadcasted_iota(i32, (ISC_ROWS, LANES), 0)
    kidx = krow * LANES + lax.broadcasted_iota(i32, (ISC_ROWS, LANES), 1)
    key = jnp.where(kidx <= past, isc_ref[...], -jnp.inf)

    def total(x):
        folded = jnp.sum(x.reshape(ISC_ROWS // 8, 8, LANES), axis=0)
        return jnp.sum(jnp.sum(folded, axis=1, keepdims=True), axis=0, keepdims=True)

    def as_float(pattern):
        bits = jnp.where(pattern < 0, pattern ^ jnp.int32(0x7FFFFFFF), pattern)
        return lax.bitcast_convert_type(bits, f32)

    def count_ge(trial):
        return total((key >= as_float(trial)).astype(i32))

    c_pos = count_ge(jnp.zeros((1, 1), i32))
    v_thr = jnp.where(c_pos >= topk, 0, INT_MIN).astype(i32)
    for shift, width in ((27, 4), (23, 4), (19, 4), (15, 4), (11, 4), (7, 4), (3, 4), (0, 3)):
        digit = jnp.zeros((1, 1), i32)
        for j in range(1, 1 << width):
            ok = count_ge(v_thr | jnp.int32(j << shift)) >= topk
            digit = digit + ok.astype(i32)
        v_thr = v_thr | lax.shift_left(digit, jnp.int32(shift))
    v_f = as_float(v_thr) if past + 1 > topk else jnp.full((1, 1), -jnp.inf, f32)
    valid = key > -jnp.inf

    def tie_mask():
        need = topk - count_ge(v_thr + 1)
        tie = key == v_f
        w_last = jnp.zeros((1, 1), i32)
        for shift in (8, 4, 0):
            digit = jnp.zeros((1, 1), i32)
            for j in range(1, 16):
                below = total((tie & (kidx < (w_last | jnp.int32(j << shift)))).astype(i32))
                digit = digit + (below < need).astype(i32)
            w_last = w_last | lax.shift_left(digit, jnp.int32(shift))
        take = (key > v_f) | (tie & (kidx <= w_last))
        return jnp.where(take & valid, 1.0, 0.0)

    def plain_mask():
        return jnp.where((key >= v_f) & valid, 1.0, 0.0)

    has_surplus_tie = count_ge(v_thr)[0, 0] > topk
    yield
    dsel = lax.cond(has_surplus_tie, tie_mask, plain_mask)
    yield

    def attend(tiles, masks, s_new, new_ok, v_tile, v_new):
        tiles = [jnp.where(mk, t, NEG_BIG) for t, mk in zip(tiles, masks)]
        s_new = jnp.where(new_ok, s_new, NEG_BIG)
        m = s_new
        for t in tiles:
            m = jnp.maximum(m, jnp.max(t, axis=-1, keepdims=True))
        l = jnp.exp(s_new - m)
        acc = l * v_new.astype(f32)
        for j, t in enumerate(tiles):
            e = jnp.exp(t - m)
            l = l + jnp.sum(e, axis=-1, keepdims=True)
            acc = acc + _dot_nt(e.astype(bf16), v_tile(j))
        return pick_half(acc * (1.0 / l))

    sel_masks = [jnp.where(lane < BLK, sel[:, 2 * p:2 * p + 1], sel[:, 2 * p + 1:2 * p + 2]) > 0.0
                 for p in range(n_pages)]
    o_s = attend(s_sel, sel_masks, new_score(qn8, newb[:, 2 * LANES:3 * LANES]), sel[:, nb - 1:nb] > 0.0,
                 lambda p: feat(p, 3).astype(bf16), newb[:, 3 * LANES:4 * LANES])
    dsa_masks = [dsel[p:p + 1, :] > 0.0 for p in range(n_pages)]
    o_d = attend(s_dsa, dsa_masks, new_score(qd8, newb[:, 4 * LANES:5 * LANES]), dsel[n_pages:n_pages + 1, 0:1] > 0.0,
                 lambda p: feat(p, 5).astype(bf16), newb[:, 5 * LANES:6 * LANES])
    n_win = win_ref.shape[2]
    wlane = lax.broadcasted_iota(i32, (1, n_win), 1)
    wpos = past - n_win + wlane
    s_w = jnp.dot(qn8, win_ref[u, 0:LANES, :].astype(bf16), preferred_element_type=f32) + slope8 * wpos.astype(f32)
    w_ok = (past - wpos < WINDOW) & (wpos >= 0)
    o_w = attend([s_w], [w_ok], new_score(qn8, newb[:, 6 * LANES:7 * LANES]), True,
                 lambda p: win_ref[u, LANES:2 * LANES, :].astype(bf16), newb[:, 7 * LANES:8 * LANES])
    o_ref[u, 0:NSA_HEADS, :] = gates[:, 0:1] * o_c + gates[:, 1:2] * o_s + gates[:, 2:3] * o_w
    o_ref[u, NSA_HEADS:2 * NSA_HEADS, :] = o_d
    last = jnp.where(first_row & (lax.broadcasted_iota(i32, (8, n_win), 1) == n_win - 1), 1.0, 0.0)
    new_w = jnp.where(first_row, neww_ref[u], 0.0)
    w_hi = new_w.astype(bf16).astype(f32)
    w_mid = (new_w - w_hi).astype(bf16).astype(f32)
    w_lo = new_w - w_hi - w_mid
    pieces = jnp.concatenate([w_hi, w_mid, w_lo, jnp.zeros_like(new_w)], axis=0).astype(bf16)
    placed = lax.dot_general(pieces, jnp.concatenate([last] * 4, axis=0).astype(bf16), tn,
                             preferred_element_type=f32)
    wout_ref[u] = jnp.where(wlane == n_win - 1, placed, pltpu.roll(win_ref[u], n_win - 1, 1))


def _odd_attn_sample(zq, misc, kvb, win_new, cache_kv, cache_idx, cache_win, page_table, w_cmp):
    f32 = jnp.float32
    s_n, n_pages = page_table.shape
    n_pool = cache_kv.shape[0]
    n_win = cache_win.shape[1]
    past = n_pages * PAGE
    topk = min(DSA_TOPK_MAX, (past + 1) // 4)
    nbp = _round_up(past // BLK + 1, 8)
    wc2 = jnp.concatenate([w_cmp, w_cmp], axis=-1)
    wct = jnp.tile(jnp.swapaxes(w_cmp, 1, 2), (1, KV_GROUPS, PAGE // BLK))
    qn = zq[:, O_Q:O_QD].reshape(s_n, NSA_HEADS, LANES)
    qd = zq[:, O_QD:ZQ_QI].reshape(s_n, DSA_HEADS, LANES)
    qi = jnp.pad(zq[:, ZQ_QI:ZQ_QI + LANES].reshape(s_n, IDX_HEADS, IDX_DIM), ((0, 0), (0, 8 - IDX_HEADS), (0, 0)))
    wi = jnp.pad(misc[:, MISC_WI:MISC_WI + IDX_HEADS].reshape(s_n, IDX_HEADS, 1), ((0, 0), (0, 8 - IDX_HEADS), (0, 0)))
    gts = misc[:, MISC_GATE:MISC_GATE + 3 * NSA_HEADS].reshape(s_n, NSA_HEADS, 3)
    kv_pages = jnp.transpose(cache_kv, (0, 2, 3, 4, 1)).reshape(n_pool, KV_SLOTS * LANES, PAGE)
    idx_pages = jnp.transpose(cache_idx, (0, 2, 1))
    win_t = jnp.transpose(cache_win, (0, 2, 3, 4, 1)).reshape(s_n, 2 * LANES, n_win)
    n_u = SAMPLE_SEQS
    per_seq = lambda shape: pl.BlockSpec((n_u,) + shape, lambda b, pt: (b, 0, 0))
    page_spec = lambda rows, u, p: pl.BlockSpec((1, rows, PAGE), lambda b, pt, u=u, p=p: (pt[b * n_u + u, p], 0, 0))
    seq_pages = [(u, p) for u in range(n_u) for p in range(n_pages)]
    in_specs = ([per_seq((NSA_HEADS, LANES)), per_seq((DSA_HEADS, LANES)), per_seq((8, IDX_DIM)),
                 per_seq((8, 1)), per_seq((NSA_HEADS, 3)), per_seq((1, 8 * LANES)), per_seq((1, IDX_DIM)),
                 per_seq((1, 2 * LANES)), per_seq((2 * LANES, n_win)),
                 pl.BlockSpec((2, BLK, LANES), lambda b, pt: (0, 0, 0)),
                 pl.BlockSpec((2, LANES, PAGE), lambda b, pt: (0, 0, 0))]
                + [page_spec(KV_SLOTS * LANES, u, p) for u, p in seq_pages]
                + [page_spec(IDX_DIM, u, p) for u, p in seq_pages])
    kern = functools.partial(_odd_sample_kernel, n_pages=n_pages, topk=topk, nbp=nbp)
    o, win_out = pl.pallas_call(
        kern,
        grid_spec=pltpu.PrefetchScalarGridSpec(
            num_scalar_prefetch=1,
            grid=(s_n // n_u,),
            in_specs=in_specs,
            out_specs=[per_seq((2 * NSA_HEADS, HEAD_DIM)), per_seq((2 * LANES, n_win))],
            scratch_shapes=[pltpu.VMEM((n_u, ISC_ROWS, LANES), f32)]),
        out_shape=[jax.ShapeDtypeStruct((s_n, 2 * NSA_HEADS, HEAD_DIM), f32),
                   jax.ShapeDtypeStruct((s_n, 2 * LANES, n_win), f32)],
        compiler_params=pltpu.CompilerParams(
            dimension_semantics=("arbitrary",), vmem_limit_bytes=VMEM_LIMIT),
        name="odd_attn_sample",
    )(page_table, qn, qd, qi, wi, gts, kvb.reshape(s_n, 1, 8 * LANES), misc[:, 0:IDX_DIM].reshape(s_n, 1, IDX_DIM),
      win_new.reshape(s_n, 1, 2 * LANES), win_t, wc2, wct,
      *([kv_pages] * len(seq_pages)), *([idx_pages] * len(seq_pages)))
    win_out = jnp.transpose(win_out.reshape(s_n, 2, KV_GROUPS, HEAD_DIM, n_win), (0, 4, 1, 2, 3))
    return o.reshape(s_n, 2 * HALF_W), win_out


def _split(z, sizes):
    cuts = [int(c) for c in np.cumsum(sizes)[:-1]]
    return jnp.split(z, cuts, axis=-1)


def _pad_cols(w, n):
    return jnp.pad(w, ((0, 0), (0, n - w.shape[1])))


def kernel(x_prompt, x_sample, state_conv, state_C, state_n, state_m, cache_kv, cache_idx, cache_win, page_table,
           w_in_e, gate_b_e, conv_w, conv_b, w_out_e, w_in_o, w_cmp, w_out_o, w_up, w_down, ln_g, ln_b):
    bf16 = jnp.bfloat16
    bp, tp, d = x_prompt.shape
    dbs = x_sample.shape[0]
    keep = cache_win.shape[2]
    yp = x_prompt.reshape(bp * tp, d)
    ys = x_sample.reshape(dbs, d)
    tm_p, tm_s = 512, dbs
    outs = {}
    for layer in range(DEPTH):
        i = layer // 2
        if layer % 2 == 0:
            w_in = _pad_cols(w_in_e[i], E_PACKED).astype(bf16)
            w_out = w_out_e[i].astype(bf16)
            ya, qkv, og, gts, tails = _even_proj(yp, w_in, conv_w[i], conv_b[i], tm_p, tp)
            yb, c_fin, n_fin, m_fin = _mlstm_prompt(qkv, og, gts, gate_b_e[i], bp, tp)
            cvp = tails.reshape(bp, tp // tm_p, CONV_TAIL, HALF_W)[:, -1, CONV_TAIL - (CONV_W - 1):]
            ya_s, yb_s, cvs, c_s, n_s, m_s = _even_sample(_proj(ys, w_in, tm_s), gate_b_e[i], conv_w[i], conv_b[i],
                                                          state_conv[i], state_C[i], state_n[i], state_m[i])
            outs['conv'] = (cvp[None], cvs[None])
            outs['c'] = (c_fin[None], c_s[None])
            outs['n'] = (n_fin[None], n_s[None])
            outs['m'] = (m_fin[:, :, 0][None], m_s[:, :, 0][None])
            yp = _outproj2_ln(ya, yb.reshape(bp * tp, HALF_W), w_out, yp, ln_g[layer, 0], ln_b[layer, 0], tm_p)
            ys = _outproj2_ln(ya_s, yb_s, w_out, ys, ln_g[layer, 0], ln_b[layer, 0], tm_s)
        else:
            w_out = w_out_o[i].astype(bf16)
            w_packed = _pack_w_in_o(w_in_o[i]).astype(bf16)
            wc2 = jnp.concatenate([w_cmp[i], w_cmp[i]], axis=-1)
            rows_first = lambda kvt: jnp.transpose(
                kvt.reshape(kvt.shape[0], KV_SLOTS, KV_GROUPS, HEAD_DIM, kvt.shape[2]), (0, 4, 1, 2, 3))
            zq, kvt_p, kvb, winp, miscp = _odd_proj(yp, w_packed, tm_p, tp)
            mp = _odd_attn_prompt(zq, miscp, kvb, wc2, bp, tp)
            kvp = rows_first(kvt_p)
            ixp = miscp[:, :IDX_DIM].reshape(bp, tp, IDX_DIM)
            wnp = winp.reshape(bp, tp, 2, KV_GROUPS, HEAD_DIM)[:, tp - keep:]
            zq_s, kvt_s, kvb_s, win_s, misc_s = _odd_proj(ys, w_packed, tm_s, dbs)
            ms, wns = _odd_attn_sample(zq_s, misc_s, kvb_s, win_s, cache_kv[i], cache_idx[i], cache_win[i],
                                       page_table, w_cmp[i])
            kvs = rows_first(kvt_s).reshape(dbs, 1, KV_SLOTS, KV_GROUPS, HEAD_DIM)
            ixs = misc_s[:, :IDX_DIM].reshape(dbs, 1, IDX_DIM)
            outs['kv'] = (kvp[None], kvs[None])
            outs['idx'] = (ixp[None], ixs[None])
            outs['win'] = (wnp[None], wns[None])
            yp = _outproj_ln(mp, w_out, yp, ln_g[layer, 0], ln_b[layer, 0], tm_p)
            ys = _outproj_ln(ms, w_out, ys, ln_g[layer, 0], ln_b[layer, 0], tm_s)
        wu, wd = w_up[layer].astype(bf16), w_down[layer].astype(bf16)
        yp = _mlp_ln(yp, wu, wd, ln_g[layer, 1], ln_b[layer, 1], 2 * tm_p, 1024)
        ys = _mlp_ln(ys, wu, wd, ln_g[layer, 1], ln_b[layer, 1], tm_s, 1024)
    return (yp.reshape(bp, tp, d), ys.reshape(dbs, 1, d),
            outs['conv'][0], outs['conv'][1], outs['c'][0], outs['c'][1],
            outs['n'][0], outs['n'][1], outs['m'][0], outs['m'][1],
            outs['kv'][0], outs['kv'][1], outs['idx'][0], outs['idx'][1],
            outs['win'][0], outs['win'][1])
```
